```python
import math
import jax, jax.numpy as jnp
from jax import lax
import numpy as np

D_MODEL = 1024
BATCH = 32
SEQ = 2048
DEPTH = 1

CTX_LEN = 256
GRID_W = 64
D_INNER = 2 * D_MODEL
D_SSM = D_INNER // 2
D_CONV = D_INNER - D_SSM
SSM_HEAD_DIM = 64
SSM_HEADS = D_SSM // SSM_HEAD_DIM
SSM_GROUPS = 2
SSM_STATE = 128
SSM_CONV_W = 5
SSM_CHUNK = 128
CONF_KERNEL = 31
CONF_CH_GROUP = 64
D_FF = ((int(8 * D_MODEL / 3) + 255) // 256) * 256
GN = SSM_GROUPS * SSM_STATE
OFF_Z = 0
OFF_X = OFF_Z + D_SSM
OFF_B = OFF_X + D_SSM
OFF_C = OFF_B + GN
OFF_DT = OFF_C + GN
OFF_GLU = OFF_DT + 2 * SSM_HEADS
D_IN_PROJ = OFF_GLU + 2 * D_CONV
N_MOD = 9
EPS = 1e-6

kernel_name = "hybrid_ssd_conformer_macaron_dit_block"


def rmsnorm(x, w):
    xf = x.astype(jnp.float32)
    y = xf * lax.rsqrt(jnp.mean(xf * xf, axis=-1, keepdims=True) + EPS)
    return (y * w.astype(jnp.float32)).astype(x.dtype)


def group_rmsnorm(x, w, groups):
    xf = x.astype(jnp.float32).reshape(*x.shape[:-1], groups, x.shape[-1] // groups)
    y = xf * lax.rsqrt(jnp.mean(xf * xf, axis=-1, keepdims=True) + EPS)
    return (y.reshape(x.shape) * w.astype(jnp.float32)).astype(x.dtype)


def layernorm(x, w, b):
    xf = x.astype(jnp.float32)
    mu = jnp.mean(xf, axis=-1, keepdims=True)
    var = jnp.mean(jnp.square(xf - mu), axis=-1, keepdims=True)
    y = (xf - mu) * lax.rsqrt(var + EPS)
    return (y * w.astype(jnp.float32) + b.astype(jnp.float32)).astype(x.dtype)


def modulate(h, shift, scale):
    return h * (1.0 + scale) + shift


def swiglu(h, w_gate, w_up, w_down):
    return (jax.nn.silu(h @ w_gate) * (h @ w_up)) @ w_down


def _flip(t):
    return jnp.flip(t, axis=1)


def dwconv1d(x, w, b):
    k = w.shape[0]
    pad = k // 2
    y = lax.conv_general_dilated(x, w[:, None, :], (1,), [(pad, pad)],
                                 dimension_numbers=("NWC", "WIO", "NWC"),
                                 feature_group_count=x.shape[-1])
    return y + b


def axial_dwconv(u, w, b, rows):
    bsz, seqlen, ch = u.shape
    k = w.shape[0]
    pad = k // 2
    half = ch // 2
    g = u.reshape(bsz, rows, GRID_W, ch)
    kh = w[:, :half][None, :, None, :]
    kv = w[:, half:][:, None, None, :]
    yh = lax.conv_general_dilated(g[..., :half], kh, (1, 1), [(0, 0), (pad, pad)],
                                  dimension_numbers=("NHWC", "HWIO", "NHWC"),
                                  feature_group_count=half)
    yv = lax.conv_general_dilated(g[..., half:], kv, (1, 1), [(pad, pad), (0, 0)],
                                  dimension_numbers=("NHWC", "HWIO", "NHWC"),
                                  feature_group_count=ch - half)
    return jnp.concatenate([yh, yv], axis=-1).reshape(bsz, seqlen, ch) + b


def ssd_chunked(xh, dt, a, bm, cm, h0):
    bsz, seqlen, nh, hd = xh.shape
    ng, ns = bm.shape[2], bm.shape[3]
    ne = nh // ng
    nc = seqlen // SSM_CHUNK
    dtype = xh.dtype
    xs = (xh * dt[..., None]).reshape(bsz, nc, SSM_CHUNK, ng, ne, hd)
    da = (dt.astype(jnp.float32) * a.astype(jnp.float32)).reshape(bsz, nc, SSM_CHUNK, ng, ne)
    cs = jnp.cumsum(da, axis=2)
    bc = bm.reshape(bsz, nc, SSM_CHUNK, ng, ns)
    cc = cm.reshape(bsz, nc, SSM_CHUNK, ng, ns)
    seg = cs[:, :, :, None] - cs[:, :, None, :]
    scan_order = jnp.tril(jnp.ones((SSM_CHUNK, SSM_CHUNK), dtype=bool))[None, None, :, :, None, None]
    decay = jnp.exp(jnp.where(scan_order, seg, -jnp.inf)).astype(dtype)
    scores = jnp.einsum("bclgn,bcsgn->bclsg", cc, bc)
    y_diag = jnp.einsum("bclsge,bcsgep->bclgep", scores[..., None] * decay, xs)
    w_state = jnp.exp(cs[:, :, -1:] - cs).astype(dtype)
    chunk_states = jnp.einsum("bclgn,bclge,bclgep->bcgepn", bc, w_state, xs)
    chunk_decay = jnp.exp(cs[:, :, -1]).astype(dtype)

    def step(h, inp):
        s, d = inp
        return h * d[..., None, None] + s, h

    _, h_prev = lax.scan(step, h0.reshape(bsz, ng, ne, hd, ns).astype(dtype),
                         (jnp.moveaxis(chunk_states, 1, 0), jnp.moveaxis(chunk_decay, 1, 0)))
    y_off = jnp.einsum("bclgn,cbgepn,bclge->bclgep", cc, h_prev, jnp.exp(cs).astype(dtype))
    return (y_diag + y_off).reshape(bsz, seqlen, nh, hd)


def ssd_final_state(xh, dt, a, bm):
    bsz, seqlen, nh, hd = xh.shape
    ng, ns = bm.shape[2], bm.shape[3]
    ne = nh // ng
    cs = jnp.cumsum(dt.astype(jnp.float32) * a.astype(jnp.float32), axis=1)
    w = (jnp.exp(cs[:, -1:] - cs) * dt.astype(jnp.float32)).astype(xh.dtype).reshape(bsz, seqlen, ng, ne)
    st = jnp.einsum("blgn,blge,blgep->bgepn", bm, w, xh.reshape(bsz, seqlen, ng, ne, hd))
    return st.reshape(bsz, nh, hd, ns)


def ctx_ssd_states(hc, w_in, conv_w, conv_b, dtb_f, dtb_b, alog_f, alog_b):
    bsz, clen, _ = hc.shape
    xb = jax.nn.silu(dwconv1d(hc @ w_in[:, OFF_X:OFF_C], conv_w[:, :D_SSM + GN], conv_b[:D_SSM + GN]))
    dt_raw = hc @ w_in[:, OFF_DT:OFF_GLU]
    xh = xb[..., :D_SSM].reshape(bsz, clen, SSM_HEADS, SSM_HEAD_DIM)
    bm = xb[..., D_SSM:].reshape(bsz, clen, SSM_GROUPS, SSM_STATE)
    dt_f = jax.nn.softplus(dt_raw[..., :SSM_HEADS] + dtb_f)
    dt_b = jax.nn.softplus(dt_raw[..., SSM_HEADS:] + dtb_b)
    s_f = ssd_final_state(xh, dt_f, -jnp.exp(alog_f), bm)
    s_b = ssd_final_state(_flip(xh), _flip(dt_b), -jnp.exp(alog_b), _flip(bm))
    return s_f, s_b


def mixer(h, w_in, w_out, conv_w, conv_b, dtb_f, dtb_b, alog_f, alog_b, d_skip, norm_w,
          cw, cb, ln_w, ln_b, h0_f, h0_b, rows):
    bsz, seqlen, _ = h.shape
    proj = h @ w_in
    z = proj[..., OFF_Z:OFF_X]
    xbc = jax.nn.silu(dwconv1d(proj[..., OFF_X:OFF_DT], conv_w, conv_b))
    dt_raw = proj[..., OFF_DT:OFF_GLU]
    glu = proj[..., OFF_GLU:]
    xh = xbc[..., :D_SSM].reshape(bsz, seqlen, SSM_HEADS, SSM_HEAD_DIM)
    bm = xbc[..., D_SSM:D_SSM + GN].reshape(bsz, seqlen, SSM_GROUPS, SSM_STATE)
    cm = xbc[..., D_SSM + GN:].reshape(bsz, seqlen, SSM_GROUPS, SSM_STATE)
    dt_f = jax.nn.softplus(dt_raw[..., :SSM_HEADS] + dtb_f)
    dt_b = jax.nn.softplus(dt_raw[..., SSM_HEADS:] + dtb_b)
    y_f = ssd_chunked(xh, dt_f, -jnp.exp(alog_f), bm, cm, h0_f)
    y_b = _flip(ssd_chunked(_flip(xh), _flip(dt_b), -jnp.exp(alog_b), _flip(bm), _flip(cm), h0_b))
    y = (y_f + y_b + d_skip[:, None] * xh).reshape(bsz, seqlen, D_SSM)
    y = group_rmsnorm(y * jax.nn.silu(z), norm_w, SSM_GROUPS)
    u = glu[..., :D_CONV] * jax.nn.sigmoid(glu[..., D_CONV:])
    if rows is None:
        u = dwconv1d(u, cw, cb)
    else:
        u = axial_dwconv(u, cw, cb, rows)
    u = jax.nn.silu(layernorm(u, ln_w, ln_b))
    return jnp.concatenate([y, u], axis=-1) @ w_out


def _fwd_setup_inputs(seed: int = 0) -> dict:
    key = jax.random.key(seed)
    ks = iter(jax.random.split(key, 40))

    def nrm(shape, scale):
        return jax.random.normal(next(ks), shape, jnp.float32) * scale

    L = DEPTH
    d_in_scale = D_MODEL ** -0.5
    u_dt = jax.random.uniform(next(ks), (2, L, SSM_HEADS), jnp.float32)
    dt0 = jnp.exp(u_dt * (math.log(0.1) - math.log(1e-3)) + math.log(1e-3))
    dt_bias = dt0 + jnp.log(-jnp.expm1(-dt0))
    a_log = jnp.log(jax.random.uniform(next(ks), (2, L, SSM_HEADS), jnp.float32, 1.0, 16.0))
    return {
        "x": nrm((BATCH, SEQ, D_MODEL), 1.0),
        "c": nrm((BATCH, D_MODEL), 1.0),
        "ctx": nrm((BATCH, CTX_LEN, D_MODEL), 1.0),
        "c_ctx": nrm((D_MODEL,), 1.0),
        "w_mod": nrm((L, D_MODEL, N_MOD * D_MODEL), 0.5 * d_in_scale),
        "b_mod": nrm((L, N_MOD * D_MODEL), 0.02),
        "norm_ffn1": 1.0 + nrm((L, D_MODEL), 0.02),
        "ffn1_gate": nrm((L, D_MODEL, D_FF), d_in_scale),
        "ffn1_up": nrm((L, D_MODEL, D_FF), d_in_scale),
        "ffn1_down": nrm((L, D_FF, D_MODEL), D_FF ** -0.5),
        "norm_mix": 1.0 + nrm((L, D_MODEL), 0.02),
        "w_in": nrm((L, D_MODEL, D_IN_PROJ), d_in_scale),
        "ssm_conv_w": nrm((L, SSM_CONV_W, D_SSM + 2 * GN), SSM_CONV_W ** -0.5),
        "ssm_conv_b": nrm((L, D_SSM + 2 * GN), 0.02),
        "dt_bias_fwd": dt_bias[0],
        "dt_bias_bwd": dt_bias[1],
        "a_log_fwd": a_log[0],
        "a_log_bwd": a_log[1],
        "ssm_d": 1.0 + nrm((L, SSM_HEADS), 0.02),
        "ssm_norm_w": 1.0 + nrm((L, D_SSM), 0.02),
        "cconv_w": nrm((L, CONF_KERNEL, D_CONV), CONF_KERNEL ** -0.5),
        "cconv_b": nrm((L, D_CONV), 0.02),
        "cconv_ln_w": 1.0 + nrm((L, D_CONV), 0.02),
        "cconv_ln_b": nrm((L, D_CONV), 0.02),
        "w_out": nrm((L, D_INNER, D_MODEL), D_INNER ** -0.5),
        "norm_ffn2": 1.0 + nrm((L, D_MODEL), 0.02),
        "ffn2_gate": nrm((L, D_MODEL, D_FF), d_in_scale),
        "ffn2_up": nrm((L, D_MODEL, D_FF), d_in_scale),
        "ffn2_down": nrm((L, D_FF, D_MODEL), D_FF ** -0.5),
        "final_norm": 1.0 + nrm((D_MODEL,), 0.02),
    }


def _fwd_reference(x, c, ctx, c_ctx, w_mod, b_mod, norm_ffn1, ffn1_gate, ffn1_up, ffn1_down, norm_mix,
              w_in, ssm_conv_w, ssm_conv_b, dt_bias_fwd, dt_bias_bwd, a_log_fwd, a_log_bwd, ssm_d,
              ssm_norm_w, cconv_w, cconv_b, cconv_ln_w, cconv_ln_b, w_out, norm_ffn2, ffn2_gate,
              ffn2_up, ffn2_down, final_norm):
    bsz = x.shape[0]
    rows = x.shape[1] // GRID_W
    xc = ctx
    for i in range(DEPTH):
        last = i == DEPTH - 1
        mod = (jax.nn.silu(c) @ w_mod[i] + b_mod[i]).reshape(bsz, N_MOD, 1, D_MODEL)
        n_c = 5 if last else N_MOD
        mod_c = (jax.nn.silu(c_ctx) @ w_mod[i][:, :n_c * D_MODEL] + b_mod[i][:n_c * D_MODEL]).reshape(n_c, 1, D_MODEL)
        x = x + 0.5 * mod[:, 2] * swiglu(modulate(rmsnorm(x, norm_ffn1[i]), mod[:, 0], mod[:, 1]),
                                         ffn1_gate[i], ffn1_up[i], ffn1_down[i])
        xc = xc + 0.5 * mod_c[2] * swiglu(modulate(rmsnorm(xc, norm_ffn1[i]), mod_c[0], mod_c[1]),
                                          ffn1_gate[i], ffn1_up[i], ffn1_down[i])
        hx = modulate(rmsnorm(x, norm_mix[i]), mod[:, 3], mod[:, 4])
        hc = modulate(rmsnorm(xc, norm_mix[i]), mod_c[3], mod_c[4])
        s_f, s_b = ctx_ssd_states(hc, w_in[i], ssm_conv_w[i], ssm_conv_b[i], dt_bias_fwd[i],
                                  dt_bias_bwd[i], a_log_fwd[i], a_log_bwd[i])
        x = x + mod[:, 5] * mixer(hx, w_in[i], w_out[i], ssm_conv_w[i], ssm_conv_b[i], dt_bias_fwd[i],
                                  dt_bias_bwd[i], a_log_fwd[i], a_log_bwd[i], ssm_d[i], ssm_norm_w[i],
                                  cconv_w[i], cconv_b[i], cconv_ln_w[i], cconv_ln_b[i], s_f, s_b, rows)
        if not last:
            zero_state = jnp.zeros_like(s_f)
            xc = xc + mod_c[5] * mixer(hc, w_in[i], w_out[i], ssm_conv_w[i], ssm_conv_b[i], dt_bias_fwd[i],
                                       dt_bias_bwd[i], a_log_fwd[i], a_log_bwd[i], ssm_d[i], ssm_norm_w[i],
                                       cconv_w[i], cconv_b[i], cconv_ln_w[i], cconv_ln_b[i],
                                       zero_state, zero_state, None)
        x = x + 0.5 * mod[:, 8] * swiglu(modulate(rmsnorm(x, norm_ffn2[i]), mod[:, 6], mod[:, 7]),
                                         ffn2_gate[i], ffn2_up[i], ffn2_down[i])
        if not last:
            xc = xc + 0.5 * mod_c[8] * swiglu(modulate(rmsnorm(xc, norm_ffn2[i]), mod_c[6], mod_c[7]),
                                              ffn2_gate[i], ffn2_up[i], ffn2_down[i])
    return rmsnorm(x, final_norm)


import jax as _jax
import jax.numpy as _jnp

TWIN_FORMAT = 'train_step'
FWD_PARAMS = ['x', 'c', 'ctx', 'c_ctx', 'w_mod', 'b_mod', 'norm_ffn1', 'ffn1_gate', 'ffn1_up', 'ffn1_down', 'norm_mix', 'w_in', 'ssm_conv_w', 'ssm_conv_b', 'dt_bias_fwd', 'dt_bias_bwd', 'a_log_fwd', 'a_log_bwd', 'ssm_d', 'ssm_norm_w', 'cconv_w', 'cconv_b', 'cconv_ln_w', 'cconv_ln_b', 'w_out', 'norm_ffn2', 'ffn2_gate', 'ffn2_up', 'ffn2_down', 'final_norm']
TWIN_WEIGHTS = ['c_ctx', 'w_mod', 'b_mod', 'norm_ffn1', 'ffn1_gate', 'ffn1_up', 'ffn1_down', 'norm_mix', 'w_in', 'ssm_conv_w', 'ssm_conv_b', 'dt_bias_fwd', 'dt_bias_bwd', 'a_log_fwd', 'a_log_bwd', 'ssm_d', 'ssm_norm_w', 'cconv_w', 'cconv_b', 'cconv_ln_w', 'cconv_ln_b', 'w_out', 'norm_ffn2', 'ffn2_gate', 'ffn2_up', 'ffn2_down', 'final_norm']
TWIN_DIFF_INPUT = 'x'
TWIN_INPUTS = ['x', 'c', 'ctx', 'c_ctx', 'w_mod', 'b_mod', 'norm_ffn1', 'ffn1_gate', 'ffn1_up', 'ffn1_down', 'norm_mix', 'w_in', 'ssm_conv_w', 'ssm_conv_b', 'dt_bias_fwd', 'dt_bias_bwd', 'a_log_fwd', 'a_log_bwd', 'ssm_d', 'ssm_norm_w', 'cconv_w', 'cconv_b', 'cconv_ln_w', 'cconv_ln_b', 'w_out', 'norm_ffn2', 'ffn2_gate', 'ffn2_up', 'ffn2_down', 'final_norm', 'loss_target', 'm_c_ctx', 'm_w_mod', 'm_b_mod', 'm_norm_ffn1', 'm_ffn1_gate', 'm_ffn1_up', 'm_ffn1_down', 'm_norm_mix', 'm_w_in', 'm_ssm_conv_w', 'm_ssm_conv_b', 'm_dt_bias_fwd', 'm_dt_bias_bwd', 'm_a_log_fwd', 'm_a_log_bwd', 'm_ssm_d', 'm_ssm_norm_w', 'm_cconv_w', 'm_cconv_b', 'm_cconv_ln_w', 'm_cconv_ln_b', 'm_w_out', 'm_norm_ffn2', 'm_ffn2_gate', 'm_ffn2_up', 'm_ffn2_down', 'm_final_norm', 'v_c_ctx', 'v_w_mod', 'v_b_mod', 'v_norm_ffn1', 'v_ffn1_gate', 'v_ffn1_up', 'v_ffn1_down', 'v_norm_mix', 'v_w_in', 'v_ssm_conv_w', 'v_ssm_conv_b', 'v_dt_bias_fwd', 'v_dt_bias_bwd', 'v_a_log_fwd', 'v_a_log_bwd', 'v_ssm_d', 'v_ssm_norm_w', 'v_cconv_w', 'v_cconv_b', 'v_cconv_ln_w', 'v_cconv_ln_b', 'v_w_out', 'v_norm_ffn2', 'v_ffn2_gate', 'v_ffn2_up', 'v_ffn2_down', 'v_final_norm']
TWIN_OUTPUTS = ['loss', 'grad_x', 'grad_c_ctx', 'grad_w_mod', 'grad_b_mod', 'grad_norm_ffn1', 'grad_ffn1_gate', 'grad_ffn1_up', 'grad_ffn1_down', 'grad_norm_mix', 'grad_w_in', 'grad_ssm_conv_w', 'grad_ssm_conv_b', 'grad_dt_bias_fwd', 'grad_dt_bias_bwd', 'grad_a_log_fwd', 'grad_a_log_bwd', 'grad_ssm_d', 'grad_ssm_norm_w', 'grad_cconv_w', 'grad_cconv_b', 'grad_cconv_ln_w', 'grad_cconv_ln_b', 'grad_w_out', 'grad_norm_ffn2', 'grad_ffn2_gate', 'grad_ffn2_up', 'grad_ffn2_down', 'grad_final_norm', 'delta_c_ctx', 'delta_w_mod', 'delta_b_mod', 'delta_norm_ffn1', 'delta_ffn1_gate', 'delta_ffn1_up', 'delta_ffn1_down', 'delta_norm_mix', 'delta_w_in', 'delta_ssm_conv_w', 'delta_ssm_conv_b', 'delta_dt_bias_fwd', 'delta_dt_bias_bwd', 'delta_a_log_fwd', 'delta_a_log_bwd', 'delta_ssm_d', 'delta_ssm_norm_w', 'delta_cconv_w', 'delta_cconv_b', 'delta_cconv_ln_w', 'delta_cconv_ln_b', 'delta_w_out', 'delta_norm_ffn2', 'delta_ffn2_gate', 'delta_ffn2_up', 'delta_ffn2_down', 'delta_final_norm', 'new_m_c_ctx', 'new_m_w_mod', 'new_m_b_mod', 'new_m_norm_ffn1', 'new_m_ffn1_gate', 'new_m_ffn1_up', 'new_m_ffn1_down', 'new_m_norm_mix', 'new_m_w_in', 'new_m_ssm_conv_w', 'new_m_ssm_conv_b', 'new_m_dt_bias_fwd', 'new_m_dt_bias_bwd', 'new_m_a_log_fwd', 'new_m_a_log_bwd', 'new_m_ssm_d', 'new_m_ssm_norm_w', 'new_m_cconv_w', 'new_m_cconv_b', 'new_m_cconv_ln_w', 'new_m_cconv_ln_b', 'new_m_w_out', 'new_m_norm_ffn2', 'new_m_ffn2_gate', 'new_m_ffn2_up', 'new_m_ffn2_down', 'new_m_final_norm', 'new_v_c_ctx', 'new_v_w_mod', 'new_v_b_mod', 'new_v_norm_ffn1', 'new_v_ffn1_gate', 'new_v_ffn1_up', 'new_v_ffn1_down', 'new_v_norm_mix', 'new_v_w_in', 'new_v_ssm_conv_w', 'new_v_ssm_conv_b', 'new_v_dt_bias_fwd', 'new_v_dt_bias_bwd', 'new_v_a_log_fwd', 'new_v_a_log_bwd', 'new_v_ssm_d', 'new_v_ssm_norm_w', 'new_v_cconv_w', 'new_v_cconv_b', 'new_v_cconv_ln_w', 'new_v_cconv_ln_b', 'new_v_w_out', 'new_v_norm_ffn2', 'new_v_ffn2_gate', 'new_v_ffn2_up', 'new_v_ffn2_down', 'new_v_final_norm']
TWIN_LEAF_KINDS = {'loss': 'loss', 'grad_x': 'grad_x', 'grad_c_ctx': 'grad_w', 'grad_w_mod': 'grad_w', 'grad_b_mod': 'grad_w', 'grad_norm_ffn1': 'grad_w', 'grad_ffn1_gate': 'grad_w', 'grad_ffn1_up': 'grad_w', 'grad_ffn1_down': 'grad_w', 'grad_norm_mix': 'grad_w', 'grad_w_in': 'grad_w', 'grad_ssm_conv_w': 'grad_w', 'grad_ssm_conv_b': 'grad_w', 'grad_dt_bias_fwd': 'grad_w', 'grad_dt_bias_bwd': 'grad_w', 'grad_a_log_fwd': 'grad_w', 'grad_a_log_bwd': 'grad_w', 'grad_ssm_d': 'grad_w', 'grad_ssm_norm_w': 'grad_w', 'grad_cconv_w': 'grad_w', 'grad_cconv_b': 'grad_w', 'grad_cconv_ln_w': 'grad_w', 'grad_cconv_ln_b': 'grad_w', 'grad_w_out': 'grad_w', 'grad_norm_ffn2': 'grad_w', 'grad_ffn2_gate': 'grad_w', 'grad_ffn2_up': 'grad_w', 'grad_ffn2_down': 'grad_w', 'grad_final_norm': 'grad_w', 'delta_c_ctx': 'delta_w', 'delta_w_mod': 'delta_w', 'delta_b_mod': 'delta_w', 'delta_norm_ffn1': 'delta_w', 'delta_ffn1_gate': 'delta_w', 'delta_ffn1_up': 'delta_w', 'delta_ffn1_down': 'delta_w', 'delta_norm_mix': 'delta_w', 'delta_w_in': 'delta_w', 'delta_ssm_conv_w': 'delta_w', 'delta_ssm_conv_b': 'delta_w', 'delta_dt_bias_fwd': 'delta_w', 'delta_dt_bias_bwd': 'delta_w', 'delta_a_log_fwd': 'delta_w', 'delta_a_log_bwd': 'delta_w', 'delta_ssm_d': 'delta_w', 'delta_ssm_norm_w': 'delta_w', 'delta_cconv_w': 'delta_w', 'delta_cconv_b': 'delta_w', 'delta_cconv_ln_w': 'delta_w', 'delta_cconv_ln_b': 'delta_w', 'delta_w_out': 'delta_w', 'delta_norm_ffn2': 'delta_w', 'delta_ffn2_gate': 'delta_w', 'delta_ffn2_up': 'delta_w', 'delta_ffn2_down': 'delta_w', 'delta_final_norm': 'delta_w', 'new_m_c_ctx': 'new_m', 'new_m_w_mod': 'new_m', 'new_m_b_mod': 'new_m', 'new_m_norm_ffn1': 'new_m', 'new_m_ffn1_gate': 'new_m', 'new_m_ffn1_up': 'new_m', 'new_m_ffn1_down': 'new_m', 'new_m_norm_mix': 'new_m', 'new_m_w_in': 'new_m', 'new_m_ssm_conv_w': 'new_m', 'new_m_ssm_conv_b': 'new_m', 'new_m_dt_bias_fwd': 'new_m', 'new_m_dt_bias_bwd': 'new_m', 'new_m_a_log_fwd': 'new_m', 'new_m_a_log_bwd': 'new_m', 'new_m_ssm_d': 'new_m', 'new_m_ssm_norm_w': 'new_m', 'new_m_cconv_w': 'new_m', 'new_m_cconv_b': 'new_m', 'new_m_cconv_ln_w': 'new_m', 'new_m_cconv_ln_b': 'new_m', 'new_m_w_out': 'new_m', 'new_m_norm_ffn2': 'new_m', 'new_m_ffn2_gate': 'new_m', 'new_m_ffn2_up': 'new_m', 'new_m_ffn2_down': 'new_m', 'new_m_final_norm': 'new_m', 'new_v_c_ctx': 'new_v', 'new_v_w_mod': 'new_v', 'new_v_b_mod': 'new_v', 'new_v_norm_ffn1': 'new_v', 'new_v_ffn1_gate': 'new_v', 'new_v_ffn1_up': 'new_v', 'new_v_ffn1_down': 'new_v', 'new_v_norm_mix': 'new_v', 'new_v_w_in': 'new_v', 'new_v_ssm_conv_w': 'new_v', 'new_v_ssm_conv_b': 'new_v', 'new_v_dt_bias_fwd': 'new_v', 'new_v_dt_bias_bwd': 'new_v', 'new_v_a_log_fwd': 'new_v', 'new_v_a_log_bwd': 'new_v', 'new_v_ssm_d': 'new_v', 'new_v_ssm_norm_w': 'new_v', 'new_v_cconv_w': 'new_v', 'new_v_cconv_b': 'new_v', 'new_v_cconv_ln_w': 'new_v', 'new_v_cconv_ln_b': 'new_v', 'new_v_w_out': 'new_v', 'new_v_norm_ffn2': 'new_v', 'new_v_ffn2_gate': 'new_v', 'new_v_ffn2_up': 'new_v', 'new_v_ffn2_down': 'new_v', 'new_v_final_norm': 'new_v'}


def _forward(args):
    return _fwd_reference(*[args[k] for k in FWD_PARAMS])


def _output_shape():
    out = _jax.eval_shape(lambda: _forward(_fwd_setup_inputs(0)))
    return out.shape, out.dtype

N_MICROBATCH = 1
ADAM_LR = 0.001
ADAM_B1 = 0.9
ADAM_B2 = 0.999
ADAM_EPS = 1e-08
ADAM_WD = 0.01
ADAM_STEP = 10
PER_EXAMPLE_BATCH_AXIS = {'x': 0, 'c': 0, 'ctx': 0, 'loss_target': 0}
SHARED_INPUTS = []
_WEIGHT_DTYPES = {'c_ctx': _jnp.float32, 'w_mod': _jnp.float32, 'b_mod': _jnp.float32, 'norm_ffn1': _jnp.float32, 'ffn1_gate': _jnp.float32, 'ffn1_up': _jnp.float32, 'ffn1_down': _jnp.float32, 'norm_mix': _jnp.float32, 'w_in': _jnp.float32, 'ssm_conv_w': _jnp.float32, 'ssm_conv_b': _jnp.float32, 'dt_bias_fwd': _jnp.float32, 'dt_bias_bwd': _jnp.float32, 'a_log_fwd': _jnp.float32, 'a_log_bwd': _jnp.float32, 'ssm_d': _jnp.float32, 'ssm_norm_w': _jnp.float32, 'cconv_w': _jnp.float32, 'cconv_b': _jnp.float32, 'cconv_ln_w': _jnp.float32, 'cconv_ln_b': _jnp.float32, 'w_out': _jnp.float32, 'norm_ffn2': _jnp.float32, 'ffn2_gate': _jnp.float32, 'ffn2_up': _jnp.float32, 'ffn2_down': _jnp.float32, 'final_norm': _jnp.float32}
MOMENT_SCALE = {'c_ctx': 4.072552e-03, 'w_mod': 5.801144e-02, 'b_mod': 9.480400e-02, 'norm_ffn1': 4.171027e-02, 'ffn1_gate': 1.684971e-02, 'ffn1_up': 1.633136e-02, 'ffn1_down': 2.729049e-02, 'norm_mix': 7.844858e-02, 'w_in': 3.899261e-02, 'ssm_conv_w': 4.273586e-02, 'ssm_conv_b': 6.579684e-02, 'dt_bias_fwd': 9.824790e-02, 'dt_bias_bwd': 2.595755e-01, 'a_log_fwd': 1.246388e-01, 'a_log_bwd': 2.475921e-01, 'ssm_d': 2.186613e-01, 'ssm_norm_w': 5.545694e-02, 'cconv_w': 3.271440e-02, 'cconv_b': 7.448317e-02, 'cconv_ln_w': 3.987853e-02, 'cconv_ln_b': 3.091305e-02, 'w_out': 5.953707e-02, 'norm_ffn2': 3.561426e-02, 'ffn2_gate': 1.625191e-02, 'ffn2_up': 1.571546e-02, 'ffn2_down': 2.604968e-02, 'final_norm': 6.389156e+01}


def _to_microbatches(a, axis):
    t = _jnp.moveaxis(a, axis, 0)
    t = t.reshape((N_MICROBATCH, t.shape[0] // N_MICROBATCH) + t.shape[1:])
    return _jnp.moveaxis(t, 1, axis + 1)


def setup_inputs(seed: int = 0) -> dict:
    inp = _fwd_setup_inputs(seed)
    key = _jax.random.fold_in(_jax.random.key(seed), 7919)
    shape, _ = _output_shape()
    out = dict(inp)
    out["loss_target"] = _jax.random.normal(_jax.random.fold_in(key, 0), shape, _jnp.float32)
    for i, name in enumerate(TWIN_WEIGHTS):
        w = inp[name].astype(_jnp.float32)
        if MOMENT_SCALE is None:
            s = _jnp.sqrt(_jnp.mean(_jnp.square(w)) + 1e-30)
        else:
            s = MOMENT_SCALE[name]
        km, kv = _jax.random.split(_jax.random.fold_in(key, i + 1))
        out[name] = w
        out["m_" + name] = s * _jax.random.normal(km, w.shape, _jnp.float32)
        out["v_" + name] = (s * s) * _jax.random.uniform(kv, w.shape, _jnp.float32, 0.5, 1.5)
    if N_MICROBATCH > 1:
        for name, axis in PER_EXAMPLE_BATCH_AXIS.items():
            out[name] = _to_microbatches(out[name], axis)
    return {'x': out['x'], 'c': out['c'], 'ctx': out['ctx'], 'c_ctx': out['c_ctx'], 'w_mod': out['w_mod'], 'b_mod': out['b_mod'], 'norm_ffn1': out['norm_ffn1'], 'ffn1_gate': out['ffn1_gate'], 'ffn1_up': out['ffn1_up'], 'ffn1_down': out['ffn1_down'], 'norm_mix': out['norm_mix'], 'w_in': out['w_in'], 'ssm_conv_w': out['ssm_conv_w'], 'ssm_conv_b': out['ssm_conv_b'], 'dt_bias_fwd': out['dt_bias_fwd'], 'dt_bias_bwd': out['dt_bias_bwd'], 'a_log_fwd': out['a_log_fwd'], 'a_log_bwd': out['a_log_bwd'], 'ssm_d': out['ssm_d'], 'ssm_norm_w': out['ssm_norm_w'], 'cconv_w': out['cconv_w'], 'cconv_b': out['cconv_b'], 'cconv_ln_w': out['cconv_ln_w'], 'cconv_ln_b': out['cconv_ln_b'], 'w_out': out['w_out'], 'norm_ffn2': out['norm_ffn2'], 'ffn2_gate': out['ffn2_gate'], 'ffn2_up': out['ffn2_up'], 'ffn2_down': out['ffn2_down'], 'final_norm': out['final_norm'], 'loss_target': out['loss_target'], 'm_c_ctx': out['m_c_ctx'], 'm_w_mod': out['m_w_mod'], 'm_b_mod': out['m_b_mod'], 'm_norm_ffn1': out['m_norm_ffn1'], 'm_ffn1_gate': out['m_ffn1_gate'], 'm_ffn1_up': out['m_ffn1_up'], 'm_ffn1_down': out['m_ffn1_down'], 'm_norm_mix': out['m_norm_mix'], 'm_w_in': out['m_w_in'], 'm_ssm_conv_w': out['m_ssm_conv_w'], 'm_ssm_conv_b': out['m_ssm_conv_b'], 'm_dt_bias_fwd': out['m_dt_bias_fwd'], 'm_dt_bias_bwd': out['m_dt_bias_bwd'], 'm_a_log_fwd': out['m_a_log_fwd'], 'm_a_log_bwd': out['m_a_log_bwd'], 'm_ssm_d': out['m_ssm_d'], 'm_ssm_norm_w': out['m_ssm_norm_w'], 'm_cconv_w': out['m_cconv_w'], 'm_cconv_b': out['m_cconv_b'], 'm_cconv_ln_w': out['m_cconv_ln_w'], 'm_cconv_ln_b': out['m_cconv_ln_b'], 'm_w_out': out['m_w_out'], 'm_norm_ffn2': out['m_norm_ffn2'], 'm_ffn2_gate': out['m_ffn2_gate'], 'm_ffn2_up': out['m_ffn2_up'], 'm_ffn2_down': out['m_ffn2_down'], 'm_final_norm': out['m_final_norm'], 'v_c_ctx': out['v_c_ctx'], 'v_w_mod': out['v_w_mod'], 'v_b_mod': out['v_b_mod'], 'v_norm_ffn1': out['v_norm_ffn1'], 'v_ffn1_gate': out['v_ffn1_gate'], 'v_ffn1_up': out['v_ffn1_up'], 'v_ffn1_down': out['v_ffn1_down'], 'v_norm_mix': out['v_norm_mix'], 'v_w_in': out['v_w_in'], 'v_ssm_conv_w': out['v_ssm_conv_w'], 'v_ssm_conv_b': out['v_ssm_conv_b'], 'v_dt_bias_fwd': out['v_dt_bias_fwd'], 'v_dt_bias_bwd': out['v_dt_bias_bwd'], 'v_a_log_fwd': out['v_a_log_fwd'], 'v_a_log_bwd': out['v_a_log_bwd'], 'v_ssm_d': out['v_ssm_d'], 'v_ssm_norm_w': out['v_ssm_norm_w'], 'v_cconv_w': out['v_cconv_w'], 'v_cconv_b': out['v_cconv_b'], 'v_cconv_ln_w': out['v_cconv_ln_w'], 'v_cconv_ln_b': out['v_cconv_ln_b'], 'v_w_out': out['v_w_out'], 'v_norm_ffn2': out['v_norm_ffn2'], 'v_ffn2_gate': out['v_ffn2_gate'], 'v_ffn2_up': out['v_ffn2_up'], 'v_ffn2_down': out['v_ffn2_down'], 'v_final_norm': out['v_final_norm']}


def _loss(weights, diff, rest, loss_target):
    with _jax.named_scope("forward"):
        args = {**rest, TWIN_DIFF_INPUT: diff, **{k: w.astype(_WEIGHT_DTYPES[k]) for k, w in weights.items()}}
        y = _forward(args)
    with _jax.named_scope("loss_head"):
        err = _jnp.square(y.astype(_jnp.float32) - loss_target)
        return 0.5 * _jnp.sum(_jnp.mean(err, axis=-1)) if err.ndim else 0.5 * err


def _adamw(w, g, m, v):
    m = ADAM_B1 * m + (1.0 - ADAM_B1) * g
    v = ADAM_B2 * v + (1.0 - ADAM_B2) * _jnp.square(g)
    m_hat = m / (1.0 - ADAM_B1 ** ADAM_STEP)
    v_hat = v / (1.0 - ADAM_B2 ** ADAM_STEP)
    delta = -ADAM_LR * (m_hat / (_jnp.sqrt(v_hat) + ADAM_EPS) + ADAM_WD * w)
    return delta, m, v


def reference(x, c, ctx, c_ctx, w_mod, b_mod, norm_ffn1, ffn1_gate, ffn1_up, ffn1_down, norm_mix, w_in, ssm_conv_w, ssm_conv_b, dt_bias_fwd, dt_bias_bwd, a_log_fwd, a_log_bwd, ssm_d, ssm_norm_w, cconv_w, cconv_b, cconv_ln_w, cconv_ln_b, w_out, norm_ffn2, ffn2_gate, ffn2_up, ffn2_down, final_norm, loss_target, m_c_ctx, m_w_mod, m_b_mod, m_norm_ffn1, m_ffn1_gate, m_ffn1_up, m_ffn1_down, m_norm_mix, m_w_in, m_ssm_conv_w, m_ssm_conv_b, m_dt_bias_fwd, m_dt_bias_bwd, m_a_log_fwd, m_a_log_bwd, m_ssm_d, m_ssm_norm_w, m_cconv_w, m_cconv_b, m_cconv_ln_w, m_cconv_ln_b, m_w_out, m_norm_ffn2, m_ffn2_gate, m_ffn2_up, m_ffn2_down, m_final_norm, v_c_ctx, v_w_mod, v_b_mod, v_norm_ffn1, v_ffn1_gate, v_ffn1_up, v_ffn1_down, v_norm_mix, v_w_in, v_ssm_conv_w, v_ssm_conv_b, v_dt_bias_fwd, v_dt_bias_bwd, v_a_log_fwd, v_a_log_bwd, v_ssm_d, v_ssm_norm_w, v_cconv_w, v_cconv_b, v_cconv_ln_w, v_cconv_ln_b, v_w_out, v_norm_ffn2, v_ffn2_gate, v_ffn2_up, v_ffn2_down, v_final_norm):
    given = dict(x=x, c=c, ctx=ctx, c_ctx=c_ctx, w_mod=w_mod, b_mod=b_mod, norm_ffn1=norm_ffn1, ffn1_gate=ffn1_gate, ffn1_up=ffn1_up, ffn1_down=ffn1_down, norm_mix=norm_mix, w_in=w_in, ssm_conv_w=ssm_conv_w, ssm_conv_b=ssm_conv_b, dt_bias_fwd=dt_bias_fwd, dt_bias_bwd=dt_bias_bwd, a_log_fwd=a_log_fwd, a_log_bwd=a_log_bwd, ssm_d=ssm_d, ssm_norm_w=ssm_norm_w, cconv_w=cconv_w, cconv_b=cconv_b, cconv_ln_w=cconv_ln_w, cconv_ln_b=cconv_ln_b, w_out=w_out, norm_ffn2=norm_ffn2, ffn2_gate=ffn2_gate, ffn2_up=ffn2_up, ffn2_down=ffn2_down, final_norm=final_norm, loss_target=loss_target, m_c_ctx=m_c_ctx, m_w_mod=m_w_mod, m_b_mod=m_b_mod, m_norm_ffn1=m_norm_ffn1, m_ffn1_gate=m_ffn1_gate, m_ffn1_up=m_ffn1_up, m_ffn1_down=m_ffn1_down, m_norm_mix=m_norm_mix, m_w_in=m_w_in, m_ssm_conv_w=m_ssm_conv_w, m_ssm_conv_b=m_ssm_conv_b, m_dt_bias_fwd=m_dt_bias_fwd, m_dt_bias_bwd=m_dt_bias_bwd, m_a_log_fwd=m_a_log_fwd, m_a_log_bwd=m_a_log_bwd, m_ssm_d=m_ssm_d, m_ssm_norm_w=m_ssm_norm_w, m_cconv_w=m_cconv_w, m_cconv_b=m_cconv_b, m_cconv_ln_w=m_cconv_ln_w, m_cconv_ln_b=m_cconv_ln_b, m_w_out=m_w_out, m_norm_ffn2=m_norm_ffn2, m_ffn2_gate=m_ffn2_gate, m_ffn2_up=m_ffn2_up, m_ffn2_down=m_ffn2_down, m_final_norm=m_final_norm, v_c_ctx=v_c_ctx, v_w_mod=v_w_mod, v_b_mod=v_b_mod, v_norm_ffn1=v_norm_ffn1, v_ffn1_gate=v_ffn1_gate, v_ffn1_up=v_ffn1_up, v_ffn1_down=v_ffn1_down, v_norm_mix=v_norm_mix, v_w_in=v_w_in, v_ssm_conv_w=v_ssm_conv_w, v_ssm_conv_b=v_ssm_conv_b, v_dt_bias_fwd=v_dt_bias_fwd, v_dt_bias_bwd=v_dt_bias_bwd, v_a_log_fwd=v_a_log_fwd, v_a_log_bwd=v_a_log_bwd, v_ssm_d=v_ssm_d, v_ssm_norm_w=v_ssm_norm_w, v_cconv_w=v_cconv_w, v_cconv_b=v_cconv_b, v_cconv_ln_w=v_cconv_ln_w, v_cconv_ln_b=v_cconv_ln_b, v_w_out=v_w_out, v_norm_ffn2=v_norm_ffn2, v_ffn2_gate=v_ffn2_gate, v_ffn2_up=v_ffn2_up, v_ffn2_down=v_ffn2_down, v_final_norm=v_final_norm)
    weights = {n: given[n] for n in TWIN_WEIGHTS}
    shared = {n: given[n] for n in SHARED_INPUTS}
    per_example = {n: given[n] for n in ['x', 'c', 'ctx']}
    grad_fn = _jax.value_and_grad(_loss, argnums=(0, 1))

    def one_microbatch(ex, loss_target):
        ex = dict(ex)
        diff = ex.pop(TWIN_DIFF_INPUT)
        return grad_fn(weights, diff, {**shared, **ex}, loss_target)

    if N_MICROBATCH == 1:
        loss, (grad_w, grad_x) = one_microbatch(per_example, given["loss_target"])
    else:
        def body(carry, xs):
            loss_sum, grad_sum = carry
            l_k, (gw_k, gx_k) = one_microbatch(xs[0], xs[1])
            with _jax.named_scope("update"):
                return (loss_sum + l_k, _jax.tree.map(_jnp.add, grad_sum, gw_k)), gx_k

        init = (_jnp.zeros((), _jnp.float32), _jax.tree.map(_jnp.zeros_like, weights))
        (loss, grad_w), grad_x = _jax.lax.scan(body, init, (per_example, given["loss_target"]))
    with _jax.named_scope("update"):
        delta_w, new_m, new_v = {}, {}, {}
        for n in TWIN_WEIGHTS:
            delta_w[n], new_m[n], new_v[n] = _adamw(weights[n], grad_w[n], given["m_" + n], given["v_" + n])
    return (loss, grad_x, *[grad_w[n] for n in TWIN_WEIGHTS], *[delta_w[n] for n in TWIN_WEIGHTS],
            *[new_m[n] for n in TWIN_WEIGHTS], *[new_v[n] for n in TWIN_WEIGHTS])
```

```python
import functools
import math

import jax
import jax.numpy as jnp
from jax import lax
from jax.experimental import pallas as pl
from jax.experimental.pallas import tpu as pltpu

F32 = jnp.float32
BF16 = jnp.bfloat16
MESH = pl.DeviceIdType.MESH

N_DEV = 8
N_CHIP_PEERS = 3
HEAD_DIM = 64
N_STATE = 128
SSD_GROUPS = 2
CHUNK = 128
GRID_W = 64
N_MOD = 9
EPS = 1e-6
DT_PAD = 256
DT_LANES = 128
HALO = 8
VMEM_LIMIT = 48 * 1024 * 1024
NEG_BIG = -1e30

ADAM_LR = 0.001
ADAM_B1 = 0.9
ADAM_B2 = 0.999
ADAM_EPS = 1e-08
ADAM_WD = 0.01
ADAM_STEP = 10


def _pick(n, prefs):
    for p in prefs:
        if n % p == 0:
            return p
    return n


def _cparams(ndim):
    return pltpu.CompilerParams(dimension_semantics=("arbitrary",) * ndim, vmem_limit_bytes=VMEM_LIMIT)


def _silu(v):
    return v * jax.nn.sigmoid(v)


def _mm(a, b, mode, out_dtype, name):
    if mode == "tn":
        (K, M), (K2, N) = a.shape, b.shape
    elif mode == "nt":
        (M, K), (N, K2) = a.shape, b.shape
    else:
        (M, K), (K2, N) = a.shape, b.shape
    assert K == K2, (name, a.shape, b.shape)
    tm = M if M <= 512 else _pick(M, (512, 256, 128, 64, 32, 16, 8))
    tn = _pick(N, (512, 256, 128))
    tk = _pick(K, (1024, 512, 256, 128))
    nk = K // tk
    if mode == "tn":
        a_spec = pl.BlockSpec((tk, tm), lambda i, j, k: (k, i))
        dn = (((0,), (0,)), ((), ()))
    else:
        a_spec = pl.BlockSpec((tm, tk), lambda i, j, k: (i, k))
        dn = (((1,), (1,)), ((), ())) if mode == "nt" else (((1,), (0,)), ((), ()))
    if mode == "nt":
        b_spec = pl.BlockSpec((tn, tk), lambda i, j, k: (j, k))
    else:
        b_spec = pl.BlockSpec((tk, tn), lambda i, j, k: (k, j))

    def body(a_ref, b_ref, o_ref, acc_ref):
        k = pl.program_id(2)

        @pl.when(k == 0)
        def _():
            acc_ref[...] = jnp.zeros_like(acc_ref)

        acc_ref[...] += lax.dot_general(a_ref[...].astype(BF16), b_ref[...].astype(BF16), dn,
                                        preferred_element_type=F32)

        @pl.when(k == nk - 1)
        def _():
            o_ref[...] = acc_ref[...].astype(o_ref.dtype)

    return pl.pallas_call(
        body, grid=(M // tm, N // tn, nk), in_specs=[a_spec, b_spec],
        out_specs=pl.BlockSpec((tm, tn), lambda i, j, k: (i, j)),
        out_shape=jax.ShapeDtypeStruct((M, N), out_dtype),
        scratch_shapes=[pltpu.VMEM((tm, tn), F32)], name=name, compiler_params=_cparams(3),
    )(a, b)


class _Lay:
    def __init__(self, bl, seq, clen, d):
        self.bl, self.seq, self.clen, self.d = bl, seq, clen, d
        self.tt = min(256, clen)
        assert seq % self.tt == 0 and clen % self.tt == 0 and self.tt % CHUNK == 0
        self.spb = seq // self.tt
        self.spc = clen // self.tt
        self.nsx = bl * self.spb
        self.nsc = bl * self.spc
        self.ns = self.nsx + self.nsc
        self.tx = bl * seq
        self.ta = self.tx + bl * clen

    def mrow(self, s):
        return jnp.where(s < self.nsx, s // self.spb, self.bl)

    def first_of_row(self, s):
        return jnp.logical_or(jnp.logical_and(s < self.nsx, s % self.spb == 0), s == self.nsx)

    def seq_first(self, s):
        return jnp.where(s < self.nsx, s % self.spb == 0, (s - self.nsx) % self.spc == 0)

    def seq_last(self, s):
        return jnp.where(s < self.nsx, s % self.spb == self.spb - 1, (s - self.nsx) % self.spc == self.spc - 1)


def _tok(lay, c, cb=0, clamp=None):
    if clamp is None:
        return pl.BlockSpec((lay.tt, c), lambda j, s: (s, cb + j))
    return pl.BlockSpec((lay.tt, c), lambda j, s: (jnp.minimum(s, clamp), cb + j))


def _halo_prev(lay, c, cb=0):
    u = lay.tt // HALO
    return pl.BlockSpec((HALO, c), lambda j, s: (jnp.maximum(s * u - 1, 0), cb + j))


def _halo_next(lay, c, cb=0):
    u = lay.tt // HALO
    last = lay.ta // HALO - 1
    return pl.BlockSpec((HALO, c), lambda j, s: (jnp.minimum((s + 1) * u, last), cb + j))


def _row(lay, k, c):
    return pl.BlockSpec((None, k, c), lambda j, s: (lay.mrow(s), 0, 0))


def _glob(k, c, cb=None):
    if cb is None:
        return pl.BlockSpec((k, c), lambda j, s: (0, 0))
    return pl.BlockSpec((k, c), lambda j, s: (0, cb + j))


def _tok_call(name, body, ncb, nseg, in_specs, out_specs, out_shape, inputs, scratch=()):
    return pl.pallas_call(body, grid=(ncb, nseg), in_specs=in_specs, out_specs=out_specs, out_shape=out_shape,
                          scratch_shapes=list(scratch), name=name, compiler_params=_cparams(2))(*inputs)


def _acc(ref, first, val):
    @pl.when(first)
    def _():
        ref[...] = val

    @pl.when(jnp.logical_not(first))
    def _():
        ref[...] += val


def _norm_mod_f(x, w, sh, sc):
    y = x * lax.rsqrt(jnp.mean(x * x, axis=-1, keepdims=True) + EPS) * w
    return y * (1.0 + sc) + sh


def _norm_mod_fwd(lay, nseg, x, w, modv, ksh, name):
    d = lay.d

    def body(x_ref, w_ref, m_ref, h_ref):
        h = _norm_mod_f(x_ref[...], w_ref[...], m_ref[ksh:ksh + 1, :], m_ref[ksh + 1:ksh + 2, :])
        h_ref[...] = h.astype(h_ref.dtype)

    return _tok_call(name, body, 1, nseg, [_tok(lay, d), _glob(1, d), _row(lay, N_MOD, d)], _tok(lay, d),
                     jax.ShapeDtypeStruct((nseg * lay.tt, d), BF16), (x, w, modv))


def _norm_mod_bwd(lay, nseg, nres, x, w, modv, ksh, dh, dres, name):
    d = lay.d
    nrow = lay.bl + (1 if nseg > lay.nsx else 0)

    def body(x_ref, w_ref, m_ref, dh_ref, dres_ref, dx_ref, dw_ref, dm_ref):
        s = pl.program_id(1)
        _, vjp = jax.vjp(_norm_mod_f, x_ref[...], w_ref[...], m_ref[ksh:ksh + 1, :], m_ref[ksh + 1:ksh + 2, :])
        dx, dw, dsh, dsc = vjp(dh_ref[...])
        res = jnp.where(s < nres, dres_ref[...], 0.0)
        dx_ref[...] = dx + res
        _acc(dw_ref, s == 0, dw)
        _acc(dm_ref, lay.first_of_row(s), jnp.concatenate([dsh, dsc], axis=0))

    return _tok_call(
        name, body, 1, nseg,
        [_tok(lay, d), _glob(1, d), _row(lay, N_MOD, d), _tok(lay, d), _tok(lay, d, clamp=nres - 1)],
        [_tok(lay, d), _glob(1, d), _row(lay, 2, d)],
        [jax.ShapeDtypeStruct((nseg * lay.tt, d), F32), jax.ShapeDtypeStruct((1, d), F32),
         jax.ShapeDtypeStruct((nrow, 2, d), F32)],
        (x, w, modv, dh, dres))


def _resid_fwd(lay, nseg, x, o, modv, kg, coef, name):
    d = lay.d

    def body(x_ref, o_ref, m_ref, y_ref):
        y_ref[...] = x_ref[...] + (coef * m_ref[kg:kg + 1, :]) * o_ref[...]

    return _tok_call(name, body, 1, nseg, [_tok(lay, d), _tok(lay, d), _row(lay, N_MOD, d)], _tok(lay, d),
                     jax.ShapeDtypeStruct((nseg * lay.tt, d), F32), (x, o, modv))


def _resid_bwd(lay, nseg, dy, o, modv, kg, coef, name):
    d = lay.d
    nrow = lay.bl + (1 if nseg > lay.nsx else 0)

    def body(dy_ref, o_ref, m_ref, do_ref, dg_ref):
        s = pl.program_id(1)
        dy = dy_ref[...]
        do_ref[...] = (dy * (coef * m_ref[kg:kg + 1, :])).astype(do_ref.dtype)
        _acc(dg_ref, lay.first_of_row(s), jnp.sum(dy * o_ref[...], axis=0, keepdims=True) * coef)

    return _tok_call(name, body, 1, nseg, [_tok(lay, d), _tok(lay, d), _row(lay, N_MOD, d)],
                     [_tok(lay, d), _row(lay, 1, d)],
                     [jax.ShapeDtypeStruct((nseg * lay.tt, d), BF16), jax.ShapeDtypeStruct((nrow, 1, d), F32)],
                     (dy, o, modv))


def _swiglu_fwd(lay, nseg, gu, name):
    f = gu.shape[1] // 2

    def body(g_ref, u_ref, a_ref):
        a_ref[...] = (_silu(g_ref[...]) * u_ref[...]).astype(a_ref.dtype)

    return _tok_call(name, body, 1, nseg, [_tok(lay, f, 0), _tok(lay, f, 1)], _tok(lay, f),
                     jax.ShapeDtypeStruct((nseg * lay.tt, f), BF16), (gu, gu))


def _swiglu_bwd(lay, nseg, gu, da, name):
    f = gu.shape[1] // 2

    def body(g_ref, u_ref, da_ref, dg_ref, du_ref):
        g, da = g_ref[...], da_ref[...]
        sg = jax.nn.sigmoid(g)
        dg_ref[...] = (da * u_ref[...] * (sg * (1.0 + g * (1.0 - sg)))).astype(dg_ref.dtype)
        du_ref[...] = (da * (g * sg)).astype(du_ref.dtype)

    sh = jax.ShapeDtypeStruct((nseg * lay.tt, f), BF16)
    return _tok_call(name, body, 1, nseg, [_tok(lay, f, 0), _tok(lay, f, 1), _tok(lay, f)],
                     [_tok(lay, f), _tok(lay, f)], [sh, sh], (gu, gu, da))


def _final_loss(lay, x, wf, target, name):
    d = lay.d

    def body(x_ref, w_ref, t_ref, loss_ref, dx_ref, dw_ref):
        s = pl.program_id(1)

        def f(xv, wv):
            return xv * lax.rsqrt(jnp.mean(xv * xv, axis=-1, keepdims=True) + EPS) * wv

        y, vjp = jax.vjp(f, x_ref[...], w_ref[...])
        err = y - t_ref[...]
        part = 0.5 * jnp.sum(jnp.sum(err * err, axis=-1, keepdims=True), axis=0, keepdims=True) / d
        dx, dw = vjp(err / d)
        dx_ref[...] = dx
        _acc(loss_ref, s == 0, part)
        _acc(dw_ref, s == 0, dw)

    return _tok_call(name, body, 1, lay.nsx, [_tok(lay, d), _glob(1, d), _tok(lay, d)],
                     [_glob(1, 1), _tok(lay, d), _glob(1, d)],
                     [jax.ShapeDtypeStruct((1, 1), F32), jax.ShapeDtypeStruct((lay.tx, d), F32),
                      jax.ShapeDtypeStruct((1, d), F32)], (x, wf, target))


class _Mix:
    def __init__(self, d, heads):
        self.d_ssm = d
        self.d_conv = d
        self.heads = heads
        assert heads * HEAD_DIM == d and heads % (2 * SSD_GROUPS) == 0 and 2 * heads <= DT_LANES
        self.gn = SSD_GROUPS * N_STATE
        self.xw = d + 2 * self.gn
        self.off_x = d
        self.off_dt = d + self.xw
        self.off_glu = self.off_dt + DT_PAD
        self.pw = self.off_glu + 2 * d
        self.off_dt_ref = d + self.xw
        self.off_glu_ref = self.off_dt_ref + 2 * heads
        self.cc = _pick(self.xw, (512, 256, 128))


def _conv5_fwd(lay, mx, proj, cw, cb, name):
    c, tt = mx.cc, lay.tt
    cb0 = mx.off_x // c
    assert mx.off_x % c == 0

    def body(prev_ref, cur_ref, next_ref, w_ref, b_ref, pre_ref, act_ref, ext_ref):
        s = pl.program_id(1)
        ext_ref[0:HALO, :] = jnp.where(lay.seq_first(s), 0.0, prev_ref[...])
        ext_ref[HALO:HALO + tt, :] = cur_ref[...]
        ext_ref[HALO + tt:, :] = jnp.where(lay.seq_last(s), 0.0, next_ref[...])
        acc = jnp.zeros((tt, c), F32) + b_ref[...]
        for k in range(5):
            acc = acc + w_ref[k:k + 1, :] * ext_ref[pl.ds(HALO + k - 2, tt), :]
        pre_ref[...] = acc
        act_ref[...] = _silu(acc)

    sh = jax.ShapeDtypeStruct((lay.ta, mx.xw), F32)
    return _tok_call(name, body, mx.xw // c, lay.ns,
                     [_halo_prev(lay, c, cb0), _tok(lay, c, cb0), _halo_next(lay, c, cb0), _glob(8, c, 0), _glob(1, c, 0)],
                     [_tok(lay, c), _tok(lay, c)], [sh, sh], (proj, proj, proj, cw, cb),
                     scratch=[pltpu.VMEM((tt + 2 * HALO, c), F32)])


def _conv5_bwd(lay, mx, proj, pre, dact_f, dact_b, cw, name):
    c, tt = mx.cc, lay.tt
    cb0 = mx.off_x // c

    def dsilu(p):
        sg = jax.nn.sigmoid(p)
        return sg * (1.0 + p * (1.0 - sg))

    def body(xp_ref, xc_ref, xn_ref, pp_ref, pc_ref, pn_ref, fp_ref, fc_ref, fn_ref, bp_ref, bc_ref, bn_ref, w_ref,
             dx_ref, dw_ref, db_ref, extx_ref, extd_ref):
        s = pl.program_id(1)
        first, last = lay.seq_first(s), lay.seq_last(s)
        dcur = (fc_ref[...] + bc_ref[...]) * dsilu(pc_ref[...])
        extd_ref[0:HALO, :] = jnp.where(first, 0.0, (fp_ref[...] + bp_ref[...]) * dsilu(pp_ref[...]))
        extd_ref[HALO:HALO + tt, :] = dcur
        extd_ref[HALO + tt:, :] = jnp.where(last, 0.0, (fn_ref[...] + bn_ref[...]) * dsilu(pn_ref[...]))
        extx_ref[0:HALO, :] = jnp.where(first, 0.0, xp_ref[...])
        extx_ref[HALO:HALO + tt, :] = xc_ref[...]
        extx_ref[HALO + tt:, :] = jnp.where(last, 0.0, xn_ref[...])
        dx = jnp.zeros((tt, c), F32)
        rows = []
        for k in range(5):
            dx = dx + w_ref[k:k + 1, :] * extd_ref[pl.ds(HALO - (k - 2), tt), :]
            rows.append(jnp.sum(dcur * extx_ref[pl.ds(HALO + k - 2, tt), :], axis=0, keepdims=True))
        dx_ref[...] = dx.astype(dx_ref.dtype)
        rows.append(jnp.zeros((3, c), F32))
        _acc(dw_ref, s == 0, jnp.concatenate(rows, axis=0))
        _acc(db_ref, s == 0, jnp.sum(dcur, axis=0, keepdims=True))

    three = lambda cbx: [_halo_prev(lay, c, cbx), _tok(lay, c, cbx), _halo_next(lay, c, cbx)]
    ext = pltpu.VMEM((tt + 2 * HALO, c), F32)
    return _tok_call(name, body, mx.xw // c, lay.ns,
                     three(cb0) + three(0) + three(0) + three(0) + [_glob(8, c, 0)],
                     [_tok(lay, c), _glob(8, c, 0), _glob(1, c, 0)],
                     [jax.ShapeDtypeStruct((lay.ta, mx.xw), BF16), jax.ShapeDtypeStruct((8, mx.xw), F32),
                      jax.ShapeDtypeStruct((1, mx.xw), F32)],
                     (proj, proj, proj, pre, pre, pre, dact_f, dact_f, dact_f, dact_b, dact_b, dact_b, cw),
                     scratch=[ext, ext])


def _softplus(v):
    return jnp.maximum(v, 0.0) + jnp.log1p(jnp.exp(-jnp.abs(v)))


def _dt_fwd(lay, mx, proj, bias, name):
    cb = mx.off_dt // DT_LANES

    def body(p_ref, b_ref, dt_ref):
        dt_ref[...] = _softplus(p_ref[...] + b_ref[...])

    return _tok_call(name, body, 1, lay.ns, [_tok(lay, DT_LANES, cb), _glob(1, DT_LANES)], _tok(lay, DT_LANES),
                     jax.ShapeDtypeStruct((lay.ta, DT_LANES), F32), (proj, bias))


def _dt_bwd(lay, mx, proj, bias, parts, name):
    cb = mx.off_dt // DT_LANES

    def body(p_ref, b_ref, a_ref, b2_ref, c_ref, d_ref, dp_ref, db_ref):
        s = pl.program_id(1)
        ddt = (a_ref[...] + b2_ref[...]) + (c_ref[...] + d_ref[...])
        draw = ddt * jax.nn.sigmoid(p_ref[...] + b_ref[...])
        dp_ref[:, 0:DT_LANES] = draw.astype(dp_ref.dtype)
        dp_ref[:, DT_LANES:] = jnp.zeros((lay.tt, DT_PAD - DT_LANES), dp_ref.dtype)
        _acc(db_ref, s == 0, jnp.sum(draw, axis=0, keepdims=True))

    t = _tok(lay, DT_LANES)
    return _tok_call(name, body, 1, lay.ns, [_tok(lay, DT_LANES, cb), _glob(1, DT_LANES), t, t, t, t],
                     [_tok(lay, DT_PAD), _glob(1, DT_LANES)],
                     [jax.ShapeDtypeStruct((lay.ta, DT_PAD), BF16), jax.ShapeDtypeStruct((1, DT_LANES), F32)],
                     (proj, bias) + tuple(parts))


def _ssd_chunk(xh_pairs, bcs, ccs, dtc, dtr, a_row, a_col, st_pairs, *, rev, heads, col0):
    n = CHUNK
    r = lax.broadcasted_iota(jnp.int32, (n, n), 0)
    c = lax.broadcasted_iota(jnp.int32, (n, n), 1)
    mask = (r <= c) if rev else (r >= c)
    tri = mask.astype(F32)
    hi = lax.Precision.HIGHEST
    da_c = dtc * a_row
    cs_c = jnp.dot(tri, da_c, precision=hi, preferred_element_type=F32)
    da_r = dtr * a_col
    cs_r = lax.dot_general(da_r, tri, (((1,), (1,)), ((), ())), precision=hi, preferred_element_type=F32)
    tot = jnp.sum(da_c, axis=0, keepdims=True)
    lane = lax.broadcasted_iota(jnp.int32, (n, DT_LANES), 1)
    sub = lax.broadcasted_iota(jnp.int32, (DT_LANES, n), 0)
    lane1 = lax.broadcasted_iota(jnp.int32, (1, DT_LANES), 1)
    left = lax.broadcasted_iota(jnp.int32, (n, 2 * HEAD_DIM), 1) < HEAD_DIM
    top = lax.broadcasted_iota(jnp.int32, (2 * HEAD_DIM, 1), 0) < HEAD_DIM
    nt = (((1,), (1,)), ((), ()))
    tn = (((0,), (0,)), ((), ()))
    scores = [lax.dot_general(ccs[g].astype(BF16), bcs[g].astype(BF16), nt, preferred_element_type=F32)
              for g in range(SSD_GROUPS)]
    ys, sts = [], []
    for p in range(heads // 2):
        g = (2 * p) // (heads // SSD_GROUPS)
        per = []
        for h in (2 * p, 2 * p + 1):
            col = col0 + h
            csv = jnp.sum(jnp.where(lane == col, cs_c, 0.0), axis=1, keepdims=True)
            csr = jnp.sum(jnp.where(sub == col, cs_r, 0.0), axis=0, keepdims=True)
            dtv = jnp.sum(jnp.where(lane == col, dtc, 0.0), axis=1, keepdims=True)
            tv = jnp.sum(jnp.where(lane1 == col, tot, 0.0), axis=1, keepdims=True)
            m = scores[g] * jnp.exp(jnp.where(mask, csv - csr, NEG_BIG))
            per.append((csv, dtv, tv, m))
        (cs1, dt1, t1, m1), (cs2, dt2, t2, m2) = per
        xs = xh_pairs[p] * jnp.where(left, dt1, dt2)
        xsb = xs.astype(BF16)
        y_diag = jnp.where(left, jnp.dot(m1.astype(BF16), xsb, preferred_element_type=F32),
                           jnp.dot(m2.astype(BF16), xsb, preferred_element_type=F32))
        y_off = lax.dot_general(ccs[g].astype(BF16), st_pairs[p].astype(BF16), nt, preferred_element_type=F32)
        y_off = y_off * jnp.where(left, jnp.exp(cs1), jnp.exp(cs2))
        wst = jnp.where(left, jnp.exp(t1 - cs1), jnp.exp(t2 - cs2))
        cst = lax.dot_general((xs * wst).astype(BF16), bcs[g].astype(BF16), tn, preferred_element_type=F32)
        ys.append(y_diag + y_off)
        sts.append(st_pairs[p] * jnp.where(top, jnp.exp(t1), jnp.exp(t2)) + cst)
    return ys, sts


class _Scan:
    def __init__(self, lay, rev):
        self.ncx, self.ncc, self.bl, self.rev = lay.seq // CHUNK, lay.clen // CHUNK, lay.bl, rev
        self.nct = self.ncx + self.ncc

    def chunk(self, b, pos):
        kc = (self.ncc - 1 - pos) if self.rev else pos
        kx = (self.ncx - 1 - (pos - self.ncc)) if self.rev else (pos - self.ncc)
        return jnp.where(pos < self.ncc, self.bl * self.ncx + b * self.ncc + kc, b * self.ncx + kx)


def _ssd_io(mx, x_ref, st_src):
    np_ = mx.heads // 2
    d = mx.d_ssm
    xh = [x_ref[:, 128 * p:128 * (p + 1)] for p in range(np_)]
    bcs = [x_ref[:, d + N_STATE * g:d + N_STATE * (g + 1)] for g in range(SSD_GROUPS)]
    ccs = [x_ref[:, d + mx.gn + N_STATE * g:d + mx.gn + N_STATE * (g + 1)] for g in range(SSD_GROUPS)]
    sts = [st_src[128 * p:128 * (p + 1), :] for p in range(np_)]
    return xh, bcs, ccs, sts


def _ssd_fwd(lay, mx, xbc, dt, dtt, a_row, a_col, rev, name):
    sc = _Scan(lay, rev)
    col0 = mx.heads if rev else 0
    hp = mx.heads * HEAD_DIM

    def body(x_ref, dt_ref, dtt_ref, ar_ref, ac_ref, y_ref, hp_ref, st_ref):
        @pl.when(pl.program_id(1) == 0)
        def _():
            st_ref[...] = jnp.zeros_like(st_ref)

        hp_ref[...] = st_ref[...]
        xh, bcs, ccs, sts = _ssd_io(mx, x_ref, st_ref)
        ys, new = _ssd_chunk(xh, bcs, ccs, dt_ref[...], dtt_ref[...], ar_ref[...], ac_ref[...], sts,
                             rev=rev, heads=mx.heads, col0=col0)
        for p in range(mx.heads // 2):
            y_ref[:, 128 * p:128 * (p + 1)] = ys[p]
            st_ref[128 * p:128 * (p + 1), :] = new[p]

    ch = sc.chunk
    return pl.pallas_call(
        body, grid=(lay.bl, sc.nct),
        in_specs=[pl.BlockSpec((CHUNK, mx.xw), lambda b, i: (ch(b, i), 0)),
                  pl.BlockSpec((CHUNK, DT_LANES), lambda b, i: (ch(b, i), 0)),
                  pl.BlockSpec((DT_LANES, CHUNK), lambda b, i: (0, ch(b, i))),
                  pl.BlockSpec((1, DT_LANES), lambda b, i: (0, 0)),
                  pl.BlockSpec((DT_LANES, 1), lambda b, i: (0, 0))],
        out_specs=[pl.BlockSpec((CHUNK, mx.d_ssm), lambda b, i: (ch(b, i), 0)),
                   pl.BlockSpec((hp, N_STATE), lambda b, i: (b * sc.nct + i, 0))],
        out_shape=[jax.ShapeDtypeStruct((lay.ta, mx.d_ssm), F32),
                   jax.ShapeDtypeStruct((lay.bl * sc.nct * hp, N_STATE), F32)],
        scratch_shapes=[pltpu.VMEM((hp, N_STATE), F32)], name=name, compiler_params=_cparams(2),
    )(xbc, dt, dtt, a_row, a_col)


def _ssd_bwd(lay, mx, xbc, dt, dtt, a_row, a_col, hprev, dy, dskip, rev, name):
    sc = _Scan(lay, rev)
    col0 = mx.heads if rev else 0
    hp = mx.heads * HEAD_DIM
    np_ = mx.heads // 2
    d = mx.d_ssm
    with_skip = dskip is not None

    def body(*refs):
        if with_skip:
            x_ref, dt_ref, dtt_ref, ar_ref, ac_ref, hp_ref, dy_ref, sk_ref = refs[:8]
            rest = refs[8:]
        else:
            x_ref, dt_ref, dtt_ref, ar_ref, ac_ref, hp_ref, dy_ref = refs[:7]
            rest = refs[7:]
        dx_ref, ddc_ref, ddr_ref, dar_ref, dac_ref, ds_ref = rest
        b, i = pl.program_id(0), pl.program_id(1)

        @pl.when(i == 0)
        def _():
            ds_ref[...] = jnp.zeros_like(ds_ref)

        xh, bcs, ccs, sts = _ssd_io(mx, x_ref, hp_ref)
        fn = functools.partial(_ssd_chunk, rev=rev, heads=mx.heads, col0=col0)
        _, vjp = jax.vjp(fn, xh, bcs, ccs, dt_ref[...], dtt_ref[...], ar_ref[...], ac_ref[...], sts)
        dys = [dy_ref[:, 128 * p:128 * (p + 1)] for p in range(np_)]
        dsts = [ds_ref[128 * p:128 * (p + 1), :] for p in range(np_)]
        dxh, dbc, dcc, ddc, ddr, dar, dac, dst = vjp((dys, dsts))
        for p in range(np_):
            v = dxh[p]
            if with_skip:
                v = v + dys[p] * sk_ref[:, 128 * p:128 * (p + 1)]
            dx_ref[:, 128 * p:128 * (p + 1)] = v
            ds_ref[128 * p:128 * (p + 1), :] = dst[p]
        for g in range(SSD_GROUPS):
            dx_ref[:, d + N_STATE * g:d + N_STATE * (g + 1)] = dbc[g]
            dx_ref[:, d + mx.gn + N_STATE * g:d + mx.gn + N_STATE * (g + 1)] = dcc[g]
        ddc_ref[...] = ddc
        ddr_ref[...] = ddr
        first = jnp.logical_and(b == 0, i == 0)
        _acc(dar_ref, first, dar)
        _acc(dac_ref, first, dac)

    ch = lambda b, i: sc.chunk(b, sc.nct - 1 - i)
    in_specs = [pl.BlockSpec((CHUNK, mx.xw), lambda b, i: (ch(b, i), 0)),
                pl.BlockSpec((CHUNK, DT_LANES), lambda b, i: (ch(b, i), 0)),
                pl.BlockSpec((DT_LANES, CHUNK), lambda b, i: (0, ch(b, i))),
                pl.BlockSpec((1, DT_LANES), lambda b, i: (0, 0)),
                pl.BlockSpec((DT_LANES, 1), lambda b, i: (0, 0)),
                pl.BlockSpec((hp, N_STATE), lambda b, i: (b * sc.nct + sc.nct - 1 - i, 0)),
                pl.BlockSpec((CHUNK, d), lambda b, i: (ch(b, i), 0))]
    inputs = [xbc, dt, dtt, a_row, a_col, hprev, dy]
    if with_skip:
        in_specs.append(pl.BlockSpec((1, d), lambda b, i: (0, 0)))
        inputs.append(dskip)
    return pl.pallas_call(
        body, grid=(lay.bl, sc.nct), in_specs=in_specs,
        out_specs=[pl.BlockSpec((CHUNK, mx.xw), lambda b, i: (ch(b, i), 0)),
                   pl.BlockSpec((CHUNK, DT_LANES), lambda b, i: (ch(b, i), 0)),
                   pl.BlockSpec((DT_LANES, CHUNK), lambda b, i: (0, ch(b, i))),
                   pl.BlockSpec((1, DT_LANES), lambda b, i: (0, 0)),
                   pl.BlockSpec((DT_LANES, 1), lambda b, i: (0, 0))],
        out_shape=[jax.ShapeDtypeStruct((lay.ta, mx.xw), F32), jax.ShapeDtypeStruct((lay.ta, DT_LANES), F32),
                   jax.ShapeDtypeStruct((DT_LANES, lay.ta), F32), jax.ShapeDtypeStruct((1, DT_LANES), F32),
                   jax.ShapeDtypeStruct((DT_LANES, 1), F32)],
        scratch_shapes=[pltpu.VMEM((hp, N_STATE), F32)], name=name, compiler_params=_cparams(2),
    )(*inputs)


def _gate_f(yf, yb, xh, z, drow, nw):
    dd = yf.shape[-1]
    half = dd // SSD_GROUPS
    yz = (yf + yb + drow * xh) * _silu(z)
    lo = lax.broadcasted_iota(jnp.int32, yz.shape, 1) < half
    sq = yz * yz
    ms1 = jnp.sum(jnp.where(lo, sq, 0.0), axis=-1, keepdims=True) / half
    ms2 = jnp.sum(jnp.where(lo, 0.0, sq), axis=-1, keepdims=True) / half
    return yz * jnp.where(lo, lax.rsqrt(ms1 + EPS), lax.rsqrt(ms2 + EPS)) * nw


def _gate_fwd(lay, mx, yf, yb, xbc, proj, drow, nw, name):
    d = mx.d_ssm

    def body(yf_ref, yb_ref, xh_ref, z_ref, d_ref, w_ref, o_ref):
        o_ref[...] = _gate_f(yf_ref[...], yb_ref[...], xh_ref[...], z_ref[...], d_ref[...], w_ref[...]).astype(o_ref.dtype)

    t = _tok(lay, d)
    return _tok_call(name, body, 1, lay.nsx, [t, t, t, t, _glob(1, d), _glob(1, d)], t,
                     jax.ShapeDtypeStruct((lay.tx, d), BF16), (yf, yb, xbc, proj, drow, nw))


def _gate_bwd(lay, mx, yf, yb, xbc, proj, drow, nw, dcat, name):
    d = mx.d_ssm
    nsx = lay.nsx

    def body(yf_ref, yb_ref, xh_ref, z_ref, d_ref, w_ref, dc_ref, dy_ref, dz_ref, dd_ref, dw_ref):
        s = pl.program_id(1)

        @pl.when(s < nsx)
        def _():
            _, vjp = jax.vjp(_gate_f, yf_ref[...], yb_ref[...], xh_ref[...], z_ref[...], d_ref[...], w_ref[...])
            dyf, _, _, dz, dd, dw = vjp(dc_ref[...])
            dy_ref[...] = dyf
            dz_ref[...] = dz.astype(dz_ref.dtype)
            _acc(dd_ref, s == 0, dd)
            _acc(dw_ref, s == 0, dw)

        @pl.when(s >= nsx)
        def _():
            dy_ref[...] = jnp.zeros_like(dy_ref)
            dz_ref[...] = jnp.zeros_like(dz_ref)

    t = _tok(lay, d)
    return _tok_call(name, body, 1, lay.ns, [t, t, t, t, _glob(1, d), _glob(1, d), _tok(lay, d, clamp=nsx - 1)],
                     [t, t, _glob(1, d), _glob(1, d)],
                     [jax.ShapeDtypeStruct((lay.ta, d), F32), jax.ShapeDtypeStruct((lay.ta, d), BF16),
                      jax.ShapeDtypeStruct((1, d), F32), jax.ShapeDtypeStruct((1, d), F32)],
                     (yf, yb, xbc, proj, drow, nw, dcat))


def _glu_fwd(lay, mx, proj, name):
    d = mx.d_conv
    c = math.gcd(mx.off_glu, d)
    cb = mx.off_glu // c

    def body(a_ref, b_ref, o_ref):
        o_ref[...] = a_ref[...] * jax.nn.sigmoid(b_ref[...])

    return _tok_call(name, body, d // c, lay.nsx, [_tok(lay, c, cb), _tok(lay, c, cb + d // c)], _tok(lay, c),
                     jax.ShapeDtypeStruct((lay.tx, d), F32), (proj, proj))


def _glu_bwd(lay, mx, proj, du, name):
    d = mx.d_conv
    c = math.gcd(mx.off_glu, d)
    cb = mx.off_glu // c
    nsx = lay.nsx

    def body(a_ref, b_ref, du_ref, da_ref, db_ref):
        s = pl.program_id(1)

        @pl.when(s < nsx)
        def _():
            sg = jax.nn.sigmoid(b_ref[...])
            duv = du_ref[...]
            da_ref[...] = (duv * sg).astype(da_ref.dtype)
            db_ref[...] = (duv * a_ref[...] * sg * (1.0 - sg)).astype(db_ref.dtype)

        @pl.when(s >= nsx)
        def _():
            da_ref[...] = jnp.zeros_like(da_ref)
            db_ref[...] = jnp.zeros_like(db_ref)

    sh = jax.ShapeDtypeStruct((lay.ta, d), BF16)
    return _tok_call(name, body, d // c, lay.ns,
                     [_tok(lay, c, cb), _tok(lay, c, cb + d // c), _tok(lay, c, clamp=nsx - 1)],
                     [_tok(lay, c), _tok(lay, c)], [sh, sh], (proj, proj, du))


def _axial(lay, mx, u, dy, cw, cb, name):
    d, seq = mx.d_conv, lay.seq
    kw = cw.shape[0]
    pad = kw // 2
    c = _pick(d // 2, (256, 128))
    ncb = d // c
    zpad = GRID_W * pad
    zpad = -(-zpad // 8) * 8
    backward = dy is not None

    def shifted(ext_ref, off):
        return ext_ref[pl.ds(zpad + off, seq), :]

    def valid_row(off):
        col = lax.broadcasted_iota(jnp.int32, (seq, c), 0) % GRID_W
        return jnp.logical_and(col + off >= 0, col + off < GRID_W)

    def fill(ext_ref, v):
        ext_ref[0:zpad, :] = jnp.zeros((zpad, c), F32)
        ext_ref[zpad:zpad + seq, :] = v
        ext_ref[zpad + seq:, :] = jnp.zeros((zpad, c), F32)

    def conv(ext_ref, w_ref, is_row, sign):
        acc = jnp.zeros((seq, c), F32)
        for k in range(kw):
            off = sign * ((k - pad) if is_row else GRID_W * (k - pad))
            v = shifted(ext_ref, off)
            if is_row:
                v = jnp.where(valid_row(off), v, 0.0)
            acc = acc + w_ref[k:k + 1, :] * v
        return acc

    def fwd_body(u_ref, w_ref, b_ref, o_ref, ext_ref):
        j = pl.program_id(0)
        fill(ext_ref, u_ref[...])

        @pl.when(j < ncb // 2)
        def _():
            o_ref[...] = conv(ext_ref, w_ref, True, 1) + b_ref[...]

        @pl.when(j >= ncb // 2)
        def _():
            o_ref[...] = conv(ext_ref, w_ref, False, 1) + b_ref[...]

    def bwd_body(u_ref, dy_ref, w_ref, du_ref, dw_ref, db_ref, extu_ref, extd_ref):
        j, b = pl.program_id(0), pl.program_id(1)
        dyv = dy_ref[...]
        fill(extu_ref, u_ref[...])
        fill(extd_ref, dyv)

        def grads(is_row):
            du_ref[...] = conv(extd_ref, w_ref, is_row, -1)
            rows = []
            for k in range(kw):
                off = (k - pad) if is_row else GRID_W * (k - pad)
                v = shifted(extu_ref, off)
                if is_row:
                    v = jnp.where(valid_row(off), v, 0.0)
                rows.append(jnp.sum(dyv * v, axis=0, keepdims=True))
            _acc(dw_ref, b == 0, jnp.concatenate(rows, axis=0))

        @pl.when(j < ncb // 2)
        def _():
            grads(True)

        @pl.when(j >= ncb // 2)
        def _():
            grads(False)

        _acc(db_ref, b == 0, jnp.sum(dyv, axis=0, keepdims=True))

    seq_spec = pl.BlockSpec((seq, c), lambda j, b: (b, j))
    w_spec = pl.BlockSpec((kw, c), lambda j, b: (0, j))
    b_spec = pl.BlockSpec((1, c), lambda j, b: (0, j))
    ext = pltpu.VMEM((seq + 2 * zpad, c), F32)
    if not backward:
        return pl.pallas_call(fwd_body, grid=(ncb, lay.bl), in_specs=[seq_spec, w_spec, b_spec], out_specs=seq_spec,
                              out_shape=jax.ShapeDtypeStruct((lay.tx, d), F32), scratch_shapes=[ext], name=name,
                              compiler_params=_cparams(2))(u, cw, cb)
    return pl.pallas_call(bwd_body, grid=(ncb, lay.bl), in_specs=[seq_spec, seq_spec, w_spec],
                          out_specs=[seq_spec, w_spec, b_spec],
                          out_shape=[jax.ShapeDtypeStruct((lay.tx, d), F32), jax.ShapeDtypeStruct((kw, d), F32),
                                     jax.ShapeDtypeStruct((1, d), F32)],
                          scratch_shapes=[ext, ext], name=name, compiler_params=_cparams(2))(u, dy, cw)


def _ln_silu_f(u, w, b):
    mu = jnp.mean(u, axis=-1, keepdims=True)
    var = jnp.mean(jnp.square(u - mu), axis=-1, keepdims=True)
    return _silu((u - mu) * lax.rsqrt(var + EPS) * w + b)


def _ln_fwd(lay, mx, u, w, b, name):
    d = mx.d_conv

    def body(u_ref, w_ref, b_ref, o_ref):
        o_ref[...] = _ln_silu_f(u_ref[...], w_ref[...], b_ref[...]).astype(o_ref.dtype)

    return _tok_call(name, body, 1, lay.nsx, [_tok(lay, d), _glob(1, d), _glob(1, d)], _tok(lay, d),
                     jax.ShapeDtypeStruct((lay.tx, d), BF16), (u, w, b))


def _ln_bwd(lay, mx, u, w, b, dcat, name):
    d = mx.d_conv

    def body(u_ref, w_ref, b_ref, dc_ref, du_ref, dw_ref, db_ref):
        s = pl.program_id(1)
        _, vjp = jax.vjp(_ln_silu_f, u_ref[...], w_ref[...], b_ref[...])
        du, dw, db = vjp(dc_ref[...])
        du_ref[...] = du
        _acc(dw_ref, s == 0, dw)
        _acc(db_ref, s == 0, db)

    return _tok_call(name, body, 1, lay.nsx, [_tok(lay, d), _glob(1, d), _glob(1, d), _tok(lay, d, 1)],
                     [_tok(lay, d), _glob(1, d), _glob(1, d)],
                     [jax.ShapeDtypeStruct((lay.tx, d), F32), jax.ShapeDtypeStruct((1, d), F32),
                      jax.ShapeDtypeStruct((1, d), F32)], (u, w, b, dcat))


def _ffn_fwd(lay, nseg, x, nw, modv, k0, wgu, wd, tag):
    h = _norm_mod_fwd(lay, nseg, x, nw, modv, k0, tag + "_norm")
    gu = _mm(h, wgu, "nn", F32, tag + "_gu")
    a = _swiglu_fwd(lay, nseg, gu, tag + "_act")
    o = _mm(a, wd, "nn", F32, tag + "_down")
    y = _resid_fwd(lay, nseg, x, o, modv, k0 + 2, 0.5, tag + "_res")
    return y, (x, h, gu, a, o)


def _ffn_bwd(lay, nseg, dy, saved, nw, modv, k0, wgu, wd, tag):
    x, h, gu, a, o = saved
    do, dgate = _resid_bwd(lay, nseg, dy, o, modv, k0 + 2, 0.5, tag + "_dres")
    da = _mm(do, wd, "nt", F32, tag + "_da")
    dwd = _mm(a, do, "tn", F32, tag + "_dwd")
    dg, du = _swiglu_bwd(lay, nseg, gu, da, tag + "_dact")
    dgu = jnp.concatenate([dg, du], axis=1)
    dh = _mm(dgu, wgu, "nt", F32, tag + "_dh")
    dwgu = _mm(h, dgu, "tn", F32, tag + "_dwgu")
    dx, dnw, dss = _norm_mod_bwd(lay, nseg, nseg, x, nw, modv, k0, dh, dy, tag + "_dnorm")
    return dx, dwgu, dwd, dnw, jnp.concatenate([dss, dgate], axis=1)


def _local_step(lay, mx, xa, target, modv, w):
    d, bl = lay.d, lay.bl
    g = {}
    xa1, ffn1 = _ffn_fwd(lay, lay.ns, xa, w["norm_ffn1"], modv, 0, w["wgu1"], w["wd1"], "ffn1")
    ha = _norm_mod_fwd(lay, lay.ns, xa1, w["norm_mix"], modv, 3, "mix_norm")
    proj = _mm(ha, w["w_in"], "nn", F32, "mix_in")
    pre, xbc = _conv5_fwd(lay, mx, proj, w["conv_w"], w["conv_b"], "mix_conv")
    dt = _dt_fwd(lay, mx, proj, w["dt_bias"], "mix_dt")
    dtt = dt.T
    yf, hpf = _ssd_fwd(lay, mx, xbc, dt, dtt, w["a_row"], w["a_col"], False, "ssd_f")
    yb, hpb = _ssd_fwd(lay, mx, xbc, dt, dtt, w["a_row"], w["a_col"], True, "ssd_b")
    cat_y = _gate_fwd(lay, mx, yf, yb, xbc, proj, w["d_row"], w["ssm_norm_w"], "mix_gate")
    u0 = _glu_fwd(lay, mx, proj, "mix_glu")
    uc = _axial(lay, mx, u0, None, w["cconv_w"], w["cconv_b"], "mix_axial")
    cat_u = _ln_fwd(lay, mx, uc, w["ln_w"], w["ln_b"], "mix_ln")
    cat = jnp.concatenate([cat_y, cat_u], axis=1)
    mix = _mm(cat, w["w_out"], "nn", F32, "mix_out")
    x2 = _resid_fwd(lay, lay.nsx, xa1, mix, modv, 5, 1.0, "mix_res")
    x3, ffn2 = _ffn_fwd(lay, lay.nsx, x2, w["norm_ffn2"], modv, 6, w["wgu2"], w["wd2"], "ffn2")
    loss, dx3, g["final_norm"] = _final_loss(lay, x3, w["final_norm"], target, "loss")
    dx2, g["wgu2"], g["wd2"], g["norm_ffn2"], dmod2 = _ffn_bwd(lay, lay.nsx, dx3, ffn2, w["norm_ffn2"], modv, 6,
                                                                 w["wgu2"], w["wd2"], "ffn2")
    dmix, dg2 = _resid_bwd(lay, lay.nsx, dx2, mix, modv, 5, 1.0, "mix_dres")
    dcat = _mm(dmix, w["w_out"], "nt", F32, "mix_dcat")
    g["w_out"] = _mm(cat, dmix, "tn", F32, "mix_dwout")
    duc, g["ln_w"], g["ln_b"] = _ln_bwd(lay, mx, uc, w["ln_w"], w["ln_b"], dcat, "mix_dln")
    du0, g["cconv_w"], g["cconv_b"] = _axial(lay, mx, u0, duc, w["cconv_w"], None, "mix_daxial")
    dglu_a, dglu_b = _glu_bwd(lay, mx, proj, du0, "mix_dglu")
    dyssd, dz, g["d_row"], g["ssm_norm_w"] = _gate_bwd(lay, mx, yf, yb, xbc, proj, w["d_row"], w["ssm_norm_w"], dcat,
                                                       "mix_dgate")
    dxf, ddcf, ddrf, darf, dacf = _ssd_bwd(lay, mx, xbc, dt, dtt, w["a_row"], w["a_col"], hpf, dyssd, w["d_row"],
                                           False, "ssd_df")
    dxb, ddcb, ddrb, darb, dacb = _ssd_bwd(lay, mx, xbc, dt, dtt, w["a_row"], w["a_col"], hpb, dyssd, None,
                                           True, "ssd_db")
    g["a_row"] = (darf + darb) + (dacf + dacb).T
    dxbc, g["conv_w"], g["conv_b"] = _conv5_bwd(lay, mx, proj, pre, dxf, dxb, w["conv_w"], "mix_dconv")
    ddtraw, g["dt_bias"] = _dt_bwd(lay, mx, proj, w["dt_bias"], (ddcf, ddcb, ddrf.T, ddrb.T), "mix_ddt")
    dproj = jnp.concatenate([dz, dxbc, ddtraw, dglu_a, dglu_b], axis=1)
    dha = _mm(dproj, w["w_in"], "nt", F32, "mix_dha")
    g["w_in"] = _mm(ha, dproj, "tn", F32, "mix_dwin")
    dxa1, g["norm_mix"], dss_mix = _norm_mod_bwd(lay, lay.ns, lay.nsx, xa1, w["norm_mix"], modv, 3, dha, dx2, "mix_dnorm")
    dxa, g["wgu1"], g["wd1"], g["norm_ffn1"], dmod1 = _ffn_bwd(lay, lay.ns, dxa1, ffn1, w["norm_ffn1"], modv, 0,
                                                               w["wgu1"], w["wd1"], "ffn1")
    zrow = lambda t: jnp.concatenate([t, jnp.zeros((1,) + t.shape[1:], F32)], axis=0)
    dmodv = jnp.concatenate([dmod1, dss_mix, zrow(dg2), zrow(dmod2)], axis=1)
    return loss, dxa[:lay.tx], g, dmodv


def _all_gather(x, name, in_hbm):
    m_per, n = x.shape

    def body(x_ref, out_ref, send_sems, recv_sems, local_sem):
        mx_, my_, mc_ = lax.axis_index("x"), lax.axis_index("y"), lax.axis_index("c")
        me, sibling = (mx_, my_, mc_), (mx_, my_, 1 - mc_)
        chips = [(1 - mx_, my_), (mx_, 1 - my_), (1 - mx_, 1 - my_)]

        def rows(px, py, pc):
            return out_ref.at[pl.ds((4 * px + 2 * py + pc) * m_per, m_per), :]

        def copy(k, block, to, src=None):
            return pltpu.make_async_remote_copy(
                src_ref=rows(*block) if src is None else src, dst_ref=rows(*block),
                send_sem=send_sems.at[k], recv_sem=recv_sems.at[k], device_id=to, device_id_type=MESH)

        mine = pltpu.make_async_copy(x_ref, rows(*me), local_sem)
        mine.start()
        first = [copy(0, me, sibling, src=x_ref)]
        first += [copy(1 + j, me, (*chip, mc_), src=x_ref) for j, chip in enumerate(chips)]
        for cp in first:
            cp.start()
        passed = [copy(4 + j, (*chip, mc_), sibling) for j, chip in enumerate(chips)]
        for j, chip in enumerate(chips):
            copy(1 + j, (*chip, mc_), me).wait_recv()
            passed[j].start()
        copy(0, sibling, me).wait_recv()
        for j, chip in enumerate(chips):
            copy(4 + j, (*chip, 1 - mc_), me).wait_recv()
        for cp in first + passed:
            cp.wait_send()
        mine.wait()

    space = pl.ANY if in_hbm else pltpu.VMEM
    return pl.pallas_call(
        body, out_shape=jax.ShapeDtypeStruct((N_DEV * m_per, n), x.dtype),
        in_specs=[pl.BlockSpec(memory_space=space)], out_specs=pl.BlockSpec(memory_space=space),
        scratch_shapes=[pltpu.SemaphoreType.DMA((7,)), pltpu.SemaphoreType.DMA((7,)), pltpu.SemaphoreType.DMA],
        name=name,
    )(x)


def _swap_sibling(g, name):
    _, nchip, r, n = g.shape

    def body(g_ref, out_ref, send_sems, recv_sems):
        mx_, my_, mc_ = lax.axis_index("x"), lax.axis_index("y"), lax.axis_index("c")
        copies = [pltpu.make_async_remote_copy(
            src_ref=g_ref.at[1 - mc_, k], dst_ref=out_ref.at[k], send_sem=send_sems.at[k], recv_sem=recv_sems.at[k],
            device_id=(mx_, my_, 1 - mc_), device_id_type=MESH) for k in range(nchip)]
        for cp in copies:
            cp.start()
        for cp in copies:
            cp.wait_recv()
        for cp in copies:
            cp.wait_send()

    return pl.pallas_call(
        body, out_shape=jax.ShapeDtypeStruct((nchip, r, n), g.dtype),
        in_specs=[pl.BlockSpec(memory_space=pl.ANY)], out_specs=pl.BlockSpec(memory_space=pl.ANY),
        scratch_shapes=[pltpu.SemaphoreType.DMA((nchip,)), pltpu.SemaphoreType.DMA((nchip,))], name=name,
    )(g)


def _pair_add(g, got, name):
    _, nchip, r, n = g.shape
    tr = _pick(r, (1024, 512, 256, 128, 64, 32, 16, 8))

    def body(g_ref, got_ref, o_ref):
        o_ref[...] = g_ref[...] + got_ref[...]

    return pl.pallas_call(
        body, grid=(nchip, r // tr),
        in_specs=[pl.BlockSpec((None, None, tr, n), lambda k, i: (lax.axis_index("c"), k, i, 0)),
                  pl.BlockSpec((None, tr, n), lambda k, i: (k, i, 0))],
        out_specs=pl.BlockSpec((None, tr, n), lambda k, i: (k, i, 0)),
        out_shape=jax.ShapeDtypeStruct((nchip, r, n), g.dtype), name=name, compiler_params=_cparams(2),
    )(g, got)


def _swap_chips(p, name):
    _, r, n = p.shape

    def body(p_ref, out_ref, send_sems, recv_sems):
        mx_, my_, mc_ = lax.axis_index("x"), lax.axis_index("y"), lax.axis_index("c")
        chips = [(1 - mx_, my_), (mx_, 1 - my_), (1 - mx_, 1 - my_)]
        copies = [pltpu.make_async_remote_copy(
            src_ref=p_ref.at[2 * cx + cy], dst_ref=out_ref.at[j], send_sem=send_sems.at[j], recv_sem=recv_sems.at[j],
            device_id=(cx, cy, mc_), device_id_type=MESH) for j, (cx, cy) in enumerate(chips)]
        for cp in copies:
            cp.start()
        for cp in copies:
            cp.wait_recv()
        for cp in copies:
            cp.wait_send()

    return pl.pallas_call(
        body, out_shape=jax.ShapeDtypeStruct((N_CHIP_PEERS, r, n), p.dtype),
        in_specs=[pl.BlockSpec(memory_space=pl.ANY)], out_specs=pl.BlockSpec(memory_space=pl.ANY),
        scratch_shapes=[pltpu.SemaphoreType.DMA((N_CHIP_PEERS,)), pltpu.SemaphoreType.DMA((N_CHIP_PEERS,))], name=name,
    )(p)


def _sum_rows(x, name):
    k, n = x.shape
    tn = _pick(n, (8192, 4096, 2048, 1024, 512, 256, 128))

    def body(x_ref, o_ref):
        acc = x_ref[0:1, :]
        for i in range(1, k):
            acc = acc + x_ref[i:i + 1, :]
        o_ref[...] = acc

    return pl.pallas_call(body, grid=(n // tn,), in_specs=[pl.BlockSpec((k, tn), lambda i: (0, i))],
                          out_specs=pl.BlockSpec((1, tn), lambda i: (0, i)),
                          out_shape=jax.ShapeDtypeStruct((1, n), x.dtype), name=name, compiler_params=_cparams(1))(x)


def _adamw(w, parts, m, v, name):
    shape = w.shape
    cols = shape[-1]
    rows = math.prod(shape[:-1])
    to2 = lambda t: t.reshape(rows, cols)
    tr = rows
    if rows * cols * 4 > (1 << 20):
        tr = _pick(rows, (256, 128, 64, 32, 16, 8))
    npart = len(parts)

    def body(*refs):
        w_ref, m_ref, v_ref = refs[0], refs[1 + npart], refs[2 + npart]
        g_ref, d_ref, nm_ref, nv_ref = refs[3 + npart:]
        g = refs[1][...]
        for q in range(1, npart):
            g = g + refs[1 + q][...]
        mm = ADAM_B1 * m_ref[...] + (1.0 - ADAM_B1) * g
        vv = ADAM_B2 * v_ref[...] + (1.0 - ADAM_B2) * jnp.square(g)
        m_hat = mm / (1.0 - ADAM_B1 ** ADAM_STEP)
        v_hat = vv / (1.0 - ADAM_B2 ** ADAM_STEP)
        g_ref[...] = g
        d_ref[...] = -ADAM_LR * (m_hat / (jnp.sqrt(v_hat) + ADAM_EPS) + ADAM_WD * w_ref[...])
        nm_ref[...] = mm
        nv_ref[...] = vv

    spec = pl.BlockSpec((tr, cols), lambda i: (i, 0))
    sh = jax.ShapeDtypeStruct((rows, cols), F32)
    outs = pl.pallas_call(body, grid=(rows // tr,), in_specs=[spec] * (3 + npart), out_specs=[spec] * 4,
                          out_shape=[sh] * 4, name=name, compiler_params=_cparams(1),
                          )(to2(w), *[to2(p) for p in parts], to2(m), to2(v))
    return tuple(o.reshape(shape) for o in outs)


def _pack(arrs, width, mult):
    flat = jnp.concatenate([a.reshape(-1) for a in arrs])
    n = flat.shape[0]
    rows = -(-n // (width * mult)) * mult
    flat = jnp.concatenate([flat, jnp.zeros((rows * width - n,), flat.dtype)])
    return flat.reshape(rows, width)


def _unpack(flat, shapes):
    out, o = [], 0
    for sh in shapes:
        n = math.prod(sh)
        out.append(flat[o:o + n].reshape(sh))
        o += n
    return out


def _cols_full(t):
    return jnp.transpose(t, (1, 0, 2)).reshape(t.shape[1], -1)


def _cols_shards(t):
    d = t.shape[0]
    return jnp.transpose(t.reshape(d, N_DEV, -1), (1, 0, 2))


BIG = ("ffn1_gate", "ffn1_up", "ffn1_down", "w_in", "w_out", "ffn2_gate", "ffn2_up", "ffn2_down")
ROW_SHARDED = ("ffn1_down", "w_out", "ffn2_down")


def kernel(x, c, ctx, c_ctx, w_mod, b_mod, norm_ffn1, ffn1_gate, ffn1_up, ffn1_down, norm_mix, w_in, ssm_conv_w, ssm_conv_b, dt_bias_fwd, dt_bias_bwd, a_log_fwd, a_log_bwd, ssm_d, ssm_norm_w, cconv_w, cconv_b, cconv_ln_w, cconv_ln_b, w_out, norm_ffn2, ffn2_gate, ffn2_up, ffn2_down, final_norm, loss_target, m_c_ctx, m_w_mod, m_b_mod, m_norm_ffn1, m_ffn1_gate, m_ffn1_up, m_ffn1_down, m_norm_mix, m_w_in, m_ssm_conv_w, m_ssm_conv_b, m_dt_bias_fwd, m_dt_bias_bwd, m_a_log_fwd, m_a_log_bwd, m_ssm_d, m_ssm_norm_w, m_cconv_w, m_cconv_b, m_cconv_ln_w, m_cconv_ln_b, m_w_out, m_norm_ffn2, m_ffn2_gate, m_ffn2_up, m_ffn2_down, m_final_norm, v_c_ctx, v_w_mod, v_b_mod, v_norm_ffn1, v_ffn1_gate, v_ffn1_up, v_ffn1_down, v_norm_mix, v_w_in, v_ssm_conv_w, v_ssm_conv_b, v_dt_bias_fwd, v_dt_bias_bwd, v_a_log_fwd, v_a_log_bwd, v_ssm_d, v_ssm_norm_w, v_cconv_w, v_cconv_b, v_cconv_ln_w, v_cconv_ln_b, v_w_out, v_norm_ffn2, v_ffn2_gate, v_ffn2_up, v_ffn2_down, v_final_norm):
    args = dict(locals())
    names = ("c_ctx", "w_mod", "b_mod", "norm_ffn1", "ffn1_gate", "ffn1_up", "ffn1_down", "norm_mix", "w_in",
             "ssm_conv_w", "ssm_conv_b", "dt_bias_fwd", "dt_bias_bwd", "a_log_fwd", "a_log_bwd", "ssm_d", "ssm_norm_w",
             "cconv_w", "cconv_b", "cconv_ln_w", "cconv_ln_b", "w_out", "norm_ffn2", "ffn2_gate", "ffn2_up",
             "ffn2_down", "final_norm")
    wts = {n: args[n] for n in names}
    bl, seq, d = x.shape
    clen = ctx.shape[1]
    heads = dt_bias_fwd.shape[1]
    dff = ffn1_gate.shape[2] * N_DEV
    lay = _Lay(bl, seq, clen, d)
    mx = _Mix(d, heads)
    nb = bl * N_DEV
    me = 4 * lax.axis_index("x") + 2 * lax.axis_index("y") + lax.axis_index("c")
    mcols = w_mod.shape[2]
    n_ctx_mod = 5 * d

    small_shapes = [(bl, d), ssm_conv_w.shape[1:], cconv_w.shape[1:]]
    g1 = _all_gather(_pack([c, ssm_conv_w, cconv_w], 128, 8), "gather_small", False)
    g1 = g1.reshape(N_DEV, -1)
    per = [_unpack(g1[j], small_shapes) for j in range(N_DEV)]
    c_all = jnp.concatenate([p[0] for p in per], axis=0)
    conv_w_full = jnp.concatenate([p[1] for p in per], axis=1)
    cconv_w_full = jnp.concatenate([p[2] for p in per], axis=1)

    s_rows = nb + 8
    s_all = jnp.concatenate([_silu(c_all), _silu(c_ctx)[None, :], jnp.zeros((7, d), F32)], axis=0)
    mod_cols = _mm(s_all, w_mod[0], "nn", F32, "mod_cols")
    g2 = _all_gather(mod_cols, "gather_mod", False).reshape(N_DEV, s_rows, mcols)
    mod_all = _cols_full(g2) + b_mod
    mod_mine = jnp.concatenate([lax.dynamic_slice_in_dim(mod_all, me * bl, bl, axis=0), mod_all[nb:nb + 1]], axis=0)
    modv = mod_mine.reshape(bl + 1, N_MOD, d)

    big_shapes = [wts[n].shape[1:] for n in BIG]
    g3 = _all_gather(_pack([wts[n].astype(BF16) for n in BIG], 128, 16), "gather_weights", True).reshape(N_DEV, -1)
    full = {}
    o = 0
    for n, sh in zip(BIG, big_shapes):
        sz = math.prod(sh)
        t = g3[:, o:o + sz].reshape((N_DEV,) + tuple(sh))
        full[n] = t.reshape(-1, sh[1]) if n in ROW_SHARDED else _cols_full(t)
        o += sz
    hh = 2 * heads
    w_in_f = full["w_in"]
    w_in_p = jnp.concatenate([w_in_f[:, :mx.off_dt_ref + hh], jnp.zeros((d, DT_PAD - hh), BF16),
                              w_in_f[:, mx.off_glu_ref:]], axis=1)
    lanes_pad = lambda a, b: jnp.concatenate([a, b, jnp.zeros((1, DT_LANES - hh), F32)], axis=1)
    a_vals = lanes_pad(-jnp.exp(a_log_fwd), -jnp.exp(a_log_bwd))
    w = {
        "norm_ffn1": norm_ffn1, "norm_mix": norm_mix, "norm_ffn2": norm_ffn2, "final_norm": final_norm[None, :],
        "wgu1": jnp.concatenate([full["ffn1_gate"], full["ffn1_up"]], axis=1), "wd1": full["ffn1_down"],
        "wgu2": jnp.concatenate([full["ffn2_gate"], full["ffn2_up"]], axis=1), "wd2": full["ffn2_down"],
        "w_in": w_in_p, "w_out": full["w_out"],
        "conv_w": jnp.concatenate([conv_w_full, jnp.zeros((3, mx.xw), F32)], axis=0), "conv_b": ssm_conv_b,
        "dt_bias": lanes_pad(dt_bias_fwd, dt_bias_bwd), "a_row": a_vals, "a_col": a_vals.T,
        "d_row": jnp.repeat(ssm_d, HEAD_DIM, axis=1), "ssm_norm_w": ssm_norm_w,
        "cconv_w": cconv_w_full, "cconv_b": cconv_b, "ln_w": cconv_ln_w, "ln_b": cconv_ln_b,
    }

    xa = jnp.concatenate([x.reshape(bl * seq, d), ctx.reshape(bl * clen, d)], axis=0)
    loss, grad_x, g, dmodv = _local_step(lay, mx, xa, loss_target.reshape(bl * seq, d), modv, w)
    loss = lax.psum(loss[0, 0], ("x", "y", "c"))

    dw_in = jnp.concatenate([g["w_in"][:, :mx.off_dt_ref + hh], g["w_in"][:, mx.off_glu:]], axis=1)
    gbig = {"ffn1_gate": g["wgu1"][:, :dff], "ffn1_up": g["wgu1"][:, dff:], "ffn1_down": g["wd1"], "w_in": dw_in,
            "w_out": g["w_out"], "ffn2_gate": g["wgu2"][:, :dff], "ffn2_up": g["wgu2"][:, dff:], "ffn2_down": g["wd2"]}
    shards = [(gbig[n].reshape(N_DEV, -1) if n in ROW_SHARDED else _cols_shards(gbig[n]).reshape(N_DEV, -1)) for n in BIG]
    flat = jnp.concatenate(shards, axis=1)
    nflat = flat.shape[1]
    rr = -(-nflat // (128 * 8)) * 8
    flat = jnp.concatenate([flat, jnp.zeros((N_DEV, rr * 128 - nflat), F32)], axis=1)
    by_core = jnp.transpose(flat.reshape(4, 2, rr, 128), (1, 0, 2, 3))
    got = _swap_sibling(by_core, "rs_sibling")
    chip_sum = _pair_add(by_core, got, "rs_pair_add")
    from_chips = _swap_chips(chip_sum, "rs_chips")
    my_chip = 2 * lax.axis_index("x") + lax.axis_index("y")
    own = lax.dynamic_index_in_dim(chip_sum, my_chip, axis=0, keepdims=False)
    big_parts = [_unpack(t.reshape(-1), big_shapes) for t in (own, from_chips[0], from_chips[1], from_chips[2])]

    n9 = N_MOD * d
    dmod_rows = dmodv.reshape(bl + 1, n9)
    ctx_row = jnp.concatenate([dmod_rows[bl, :n_ctx_mod], jnp.zeros((n9 - n_ctx_mod,), F32)])
    summed = [ctx_row, g["norm_ffn1"], g["norm_mix"], g["norm_ffn2"], g["final_norm"], g["conv_b"], g["dt_bias"],
              g["a_row"], g["d_row"], g["ssm_norm_w"], g["cconv_b"], g["ln_w"], g["ln_b"], g["conv_w"][:5], g["cconv_w"]]
    sum_shapes = [t.shape for t in summed]
    pay = _pack([dmod_rows[:bl]] + summed, 128, 8)
    prow = pay.shape[0]
    g4 = _all_gather(pay, "gather_small_grads", False).reshape(N_DEV, -1)
    nbatch = bl * n9
    dmod_batch = g4[:, :nbatch].reshape(nb, n9)
    tot = _sum_rows(g4[:, nbatch:], "sum_small_grads")[0]
    (dctx, g_n1, g_nm, g_n2, g_fn, g_cb, g_dtb, g_a, g_drow, g_snw, g_ccb, g_lnw, g_lnb, g_cw, g_ccw) = _unpack(tot, sum_shapes)
    dmod_all = jnp.concatenate([dmod_batch, dctx[None, :], jnp.zeros((7, n9), F32)], axis=0)

    dmod_my_cols = lax.dynamic_slice_in_dim(dmod_all, me * mcols, mcols, axis=1)
    g_w_mod = _mm(s_all, dmod_my_cols, "tn", F32, "dw_mod")[None]
    g_b_mod = _sum_rows(dmod_all, "db_mod")
    ds_part = _mm(dmod_my_cols[nb:nb + 8], w_mod[0], "nt", F32, "ds_ctx")
    g5 = _all_gather(jnp.concatenate([ds_part[0:1], jnp.zeros((7, d), F32)], axis=0), "gather_ds_ctx", False)
    ds_ctx = _sum_rows(g5.reshape(N_DEV, 8 * d), "sum_ds_ctx")[0, :d]
    sg = jax.nn.sigmoid(c_ctx)
    g_c_ctx = ds_ctx * (sg * (1.0 + c_ctx * (1.0 - sg)))

    a_f, a_b = a_vals[:, :heads], a_vals[:, heads:hh]
    grads = {
        "c_ctx": [g_c_ctx], "w_mod": [g_w_mod], "b_mod": [g_b_mod],
        "norm_ffn1": [g_n1], "norm_mix": [g_nm], "norm_ffn2": [g_n2], "final_norm": [g_fn.reshape(-1)],
        "ssm_conv_w": [lax.dynamic_slice_in_dim(g_cw, me * ssm_conv_w.shape[2], ssm_conv_w.shape[2], axis=1)[None]],
        "ssm_conv_b": [g_cb],
        "dt_bias_fwd": [g_dtb[:, :heads]], "dt_bias_bwd": [g_dtb[:, heads:hh]],
        "a_log_fwd": [g_a[:, :heads] * a_f], "a_log_bwd": [g_a[:, heads:hh] * a_b],
        "ssm_d": [jnp.sum(g_drow.reshape(1, heads, HEAD_DIM), axis=2)], "ssm_norm_w": [g_snw],
        "cconv_w": [lax.dynamic_slice_in_dim(g_ccw, me * cconv_w.shape[2], cconv_w.shape[2], axis=1)[None]],
        "cconv_b": [g_ccb], "cconv_ln_w": [g_lnw], "cconv_ln_b": [g_lnb],
    }
    for i, n in enumerate(BIG):
        grads[n] = [bp[i][None] for bp in big_parts]

    out_g, out_d, out_m, out_v = [], [], [], []
    for n in names:
        gr, de, nm, nv = _adamw(wts[n], grads[n], args["m_" + n], args["v_" + n], "adamw_" + n)
        out_g.append(gr)
        out_d.append(de)
        out_m.append(nm)
        out_v.append(nv)
    return (loss, grad_x.reshape(bl, seq, d), *out_g, *out_d, *out_m, *out_v)
```

```python
import functools
import math

import jax
import jax.numpy as jnp
from jax import lax
from jax.experimental import pallas as pl
from jax.experimental.pallas import tpu as pltpu

F32 = jnp.float32
BF16 = jnp.bfloat16
MESH = pl.DeviceIdType.MESH

N_DEV = 8
N_CHIP_PEERS = 3
HEAD_DIM = 64
N_STATE = 128
SSD_GROUPS = 2
CHUNK = 128
GRID_W = 64
N_MOD = 9
EPS = 1e-6
DT_PAD = 256
DT_LANES = 128
HALO = 8
VMEM_LIMIT = 48 * 1024 * 1024
NEG_BIG = -1e30

ADAM_LR = 0.001
ADAM_B1 = 0.9
ADAM_B2 = 0.999
ADAM_EPS = 1e-08
ADAM_WD = 0.01
ADAM_STEP = 10


def _pick(n, prefs):
    for p in prefs:
        if n % p == 0:
            return p
    return n


MM_TILE_CAP = 2560
MM_TILE_ELEMS = 2560 * 1024


def _big_tile(n, cap):
    if n <= cap:
        return n
    best = 0
    for t in range(128, cap + 1, 128):
        if n % t == 0:
            best = t
    return best or n


def _cparams(ndim):
    return pltpu.CompilerParams(dimension_semantics=("arbitrary",) * ndim, vmem_limit_bytes=VMEM_LIMIT)


def _silu(v):
    return v * jax.nn.sigmoid(v)


def _mm(a, b, mode, out_dtype, name):
    if mode == "tn":
        (K, M), (K2, N) = a.shape, b.shape
    elif mode == "nt":
        (M, K), (N, K2) = a.shape, b.shape
    else:
        (M, K), (K2, N) = a.shape, b.shape
    assert K == K2, (name, a.shape, b.shape)
    tm = M if M <= 512 else _pick(M, (512, 256, 128, 64, 32, 16, 8))
    tn = _big_tile(N, MM_TILE_CAP)
    tk = _big_tile(K, min(MM_TILE_CAP, MM_TILE_ELEMS // tn))
    nk = K // tk
    if mode == "tn":
        a_spec = pl.BlockSpec((tk, tm), lambda i, j, k: (k, i))
        dn = (((0,), (0,)), ((), ()))
    else:
        a_spec = pl.BlockSpec((tm, tk), lambda i, j, k: (i, k))
        dn = (((1,), (1,)), ((), ())) if mode == "nt" else (((1,), (0,)), ((), ()))
    if mode == "nt":
        b_spec = pl.BlockSpec((tn, tk), lambda i, j, k: (j, k))
    else:
        b_spec = pl.BlockSpec((tk, tn), lambda i, j, k: (k, j))

    def body(a_ref, b_ref, o_ref, *acc):
        part = lax.dot_general(a_ref[...].astype(BF16), b_ref[...].astype(BF16), dn, preferred_element_type=F32)
        if nk == 1:
            o_ref[...] = part.astype(o_ref.dtype)
            return
        acc_ref, k = acc[0], pl.program_id(2)
        _acc(acc_ref, k == 0, part)

        @pl.when(k == nk - 1)
        def _():
            o_ref[...] = acc_ref[...].astype(o_ref.dtype)

    return pl.pallas_call(
        body, grid=(M // tm, N // tn, nk), in_specs=[a_spec, b_spec],
        out_specs=pl.BlockSpec((tm, tn), lambda i, j, k: (i, j)),
        out_shape=jax.ShapeDtypeStruct((M, N), out_dtype),
        scratch_shapes=[pltpu.VMEM((tm, tn), F32)] if nk > 1 else [], name=name, compiler_params=_cparams(3),
    )(a, b)


def _ffn_tm(t):
    return _pick(t, (1024, 512, 256, 128, 64, 32, 16, 8))


def _dsilu_mul(g, u, da):
    sg = jax.nn.sigmoid(g)
    return da * u * (sg * (1.0 + g * (1.0 - sg))), da * (g * sg)


def _ffn_gu(h, wg, wu, name):
    t, d = h.shape
    nb, _, fb = wg.shape
    tm = _ffn_tm(t)

    def body(h_ref, wg_ref, wu_ref, g_ref, u_ref, a_ref):
        hv = h_ref[...]
        g = jnp.dot(hv, wg_ref[...], preferred_element_type=F32)
        u = jnp.dot(hv, wu_ref[...], preferred_element_type=F32)
        g_ref[...] = g
        u_ref[...] = u
        a_ref[...] = (_silu(g) * u).astype(a_ref.dtype)

    wspec = pl.BlockSpec((None, d, fb), lambda i, j: (j, 0, 0))
    ospec = pl.BlockSpec((None, tm, fb), lambda i, j: (j, i, 0))
    sh = jax.ShapeDtypeStruct((nb, t, fb), F32)
    return pl.pallas_call(body, grid=(t // tm, nb), in_specs=[pl.BlockSpec((tm, d), lambda i, j: (i, 0)), wspec, wspec],
                          out_specs=[ospec, ospec, ospec], out_shape=[sh, sh, jax.ShapeDtypeStruct((nb, t, fb), BF16)],
                          name=name, compiler_params=_cparams(2))(h, wg, wu)


def _ffn_contract(xs, ws, mode, name):
    nb, t, fb = xs[0].shape
    d = ws[0].shape[2] if mode == "nn" else ws[0].shape[1]
    tm = _ffn_tm(t)
    npair = len(xs)
    dn = (((1,), (0,)), ((), ())) if mode == "nn" else (((1,), (1,)), ((), ()))

    def body(*refs):
        o_ref, acc_ref = refs[2 * npair], refs[2 * npair + 1]
        j = pl.program_id(1)
        part = lax.dot_general(refs[0][...], refs[npair][...], dn, preferred_element_type=F32)
        for p in range(1, npair):
            part = part + lax.dot_general(refs[p][...], refs[npair + p][...], dn, preferred_element_type=F32)
        _acc(acc_ref, j == 0, part)

        @pl.when(j == nb - 1)
        def _():
            o_ref[...] = acc_ref[...]

    xspec = pl.BlockSpec((None, tm, fb), lambda i, j: (j, i, 0))
    wspec = pl.BlockSpec((None,) + tuple(ws[0].shape[1:]), lambda i, j: (j, 0, 0))
    return pl.pallas_call(body, grid=(t // tm, nb), in_specs=[xspec] * npair + [wspec] * npair,
                          out_specs=pl.BlockSpec((tm, d), lambda i, j: (i, 0)),
                          out_shape=jax.ShapeDtypeStruct((t, d), F32), scratch_shapes=[pltpu.VMEM((tm, d), F32)],
                          name=name, compiler_params=_cparams(2))(*xs, *ws)


def _ffn_da(do, wd, g, u, name):
    t, d = do.shape
    nb, fb, _ = wd.shape
    tm = _ffn_tm(t)

    def body(do_ref, wd_ref, g_ref, u_ref, dg_ref, du_ref):
        da = lax.dot_general(do_ref[...], wd_ref[...], (((1,), (1,)), ((), ())), preferred_element_type=F32)
        dg, du = _dsilu_mul(g_ref[...], u_ref[...], da)
        dg_ref[...] = dg.astype(dg_ref.dtype)
        du_ref[...] = du.astype(du_ref.dtype)

    bspec = pl.BlockSpec((None, tm, fb), lambda i, j: (j, i, 0))
    sh = jax.ShapeDtypeStruct((nb, t, fb), BF16)
    return pl.pallas_call(body, grid=(t // tm, nb),
                          in_specs=[pl.BlockSpec((tm, d), lambda i, j: (i, 0)),
                                    pl.BlockSpec((None, fb, d), lambda i, j: (j, 0, 0)), bspec, bspec],
                          out_specs=[bspec, bspec], out_shape=[sh, sh], name=name, compiler_params=_cparams(2),
                          )(do, wd, g, u)


def _ffn_dw(h, xs, name):
    t, d = h.shape
    nb, _, fb = xs[0].shape
    tk = _ffn_tm(t)
    nk = t // tk
    npair = len(xs)
    tn_dims = (((0,), (0,)), ((), ()))

    def body(*refs):
        h_ref = refs[0]
        k = pl.program_id(1)
        hv = h_ref[...]
        for p in range(npair):
            part = lax.dot_general(hv, refs[1 + p][...], tn_dims, preferred_element_type=F32)
            _acc(refs[1 + npair + p], k == 0, part)

    xspec = pl.BlockSpec((None, tk, fb), lambda j, k: (j, k, 0))
    ospec = pl.BlockSpec((None, d, fb), lambda j, k: (j, 0, 0))
    sh = jax.ShapeDtypeStruct((nb, d, fb), F32)
    return pl.pallas_call(body, grid=(nb, nk), in_specs=[pl.BlockSpec((tk, d), lambda j, k: (k, 0))] + [xspec] * npair,
                          out_specs=[ospec] * npair, out_shape=[sh] * npair, name=name, compiler_params=_cparams(2),
                          )(h, *xs)


def _ffn_dwd(a, do, name):
    nb, t, fb = a.shape
    d = do.shape[1]
    tk = _ffn_tm(t)

    def body(a_ref, do_ref, o_ref):
        part = lax.dot_general(a_ref[...], do_ref[...], (((0,), (0,)), ((), ())), preferred_element_type=F32)
        _acc(o_ref, pl.program_id(1) == 0, part)

    return pl.pallas_call(body, grid=(nb, t // tk),
                          in_specs=[pl.BlockSpec((None, tk, fb), lambda j, k: (j, k, 0)),
                                    pl.BlockSpec((tk, d), lambda j, k: (k, 0))],
                          out_specs=pl.BlockSpec((None, fb, d), lambda j, k: (j, 0, 0)),
                          out_shape=jax.ShapeDtypeStruct((nb, fb, d), F32), name=name, compiler_params=_cparams(2),
                          )(a, do)


class _Lay:
    def __init__(self, bl, seq, clen, d):
        self.bl, self.seq, self.clen, self.d = bl, seq, clen, d
        self.tt = min(256, clen)
        assert seq % self.tt == 0 and clen % self.tt == 0 and self.tt % CHUNK == 0
        self.spb = seq // self.tt
        self.spc = clen // self.tt
        self.nsx = bl * self.spb
        self.nsc = bl * self.spc
        self.ns = self.nsx + self.nsc
        self.tx = bl * seq
        self.ta = self.tx + bl * clen

    def mrow(self, s):
        return jnp.where(s < self.nsx, s // self.spb, self.bl)

    def first_of_row(self, s):
        return jnp.logical_or(jnp.logical_and(s < self.nsx, s % self.spb == 0), s == self.nsx)

    def seq_first(self, s):
        return jnp.where(s < self.nsx, s % self.spb == 0, (s - self.nsx) % self.spc == 0)

    def seq_last(self, s):
        return jnp.where(s < self.nsx, s % self.spb == self.spb - 1, (s - self.nsx) % self.spc == self.spc - 1)


def _tok(lay, c, cb=0, clamp=None):
    if clamp is None:
        return pl.BlockSpec((lay.tt, c), lambda j, s: (s, cb + j))
    return pl.BlockSpec((lay.tt, c), lambda j, s: (jnp.minimum(s, clamp), cb + j))


def _halo_prev(lay, c, cb=0):
    u = lay.tt // HALO
    return pl.BlockSpec((HALO, c), lambda j, s: (jnp.maximum(s * u - 1, 0), cb + j))


def _halo_next(lay, c, cb=0):
    u = lay.tt // HALO
    last = lay.ta // HALO - 1
    return pl.BlockSpec((HALO, c), lambda j, s: (jnp.minimum((s + 1) * u, last), cb + j))


def _row(lay, k, c):
    return pl.BlockSpec((None, k, c), lambda j, s: (lay.mrow(s), 0, 0))


def _glob(k, c, cb=None):
    if cb is None:
        return pl.BlockSpec((k, c), lambda j, s: (0, 0))
    return pl.BlockSpec((k, c), lambda j, s: (0, cb + j))


def _tok_call(name, body, ncb, nseg, in_specs, out_specs, out_shape, inputs, scratch=()):
    return pl.pallas_call(body, grid=(ncb, nseg), in_specs=in_specs, out_specs=out_specs, out_shape=out_shape,
                          scratch_shapes=list(scratch), name=name, compiler_params=_cparams(2))(*inputs)


def _acc(ref, first, val):
    @pl.when(first)
    def _():
        ref[...] = val

    @pl.when(jnp.logical_not(first))
    def _():
        ref[...] += val


def _norm_mod_f(x, w, sh, sc):
    y = x * lax.rsqrt(jnp.mean(x * x, axis=-1, keepdims=True) + EPS) * w
    return y * (1.0 + sc) + sh


def _norm_mod_fwd(lay, nseg, x, w, modv, ksh, name):
    d = lay.d

    def body(x_ref, w_ref, m_ref, h_ref):
        h = _norm_mod_f(x_ref[...], w_ref[...], m_ref[ksh:ksh + 1, :], m_ref[ksh + 1:ksh + 2, :])
        h_ref[...] = h.astype(h_ref.dtype)

    return _tok_call(name, body, 1, nseg, [_tok(lay, d), _glob(1, d), _row(lay, N_MOD, d)], _tok(lay, d),
                     jax.ShapeDtypeStruct((nseg * lay.tt, d), BF16), (x, w, modv))


def _norm_mod_bwd(lay, nseg, nres, x, w, modv, ksh, dh, dres, name):
    d = lay.d
    nrow = lay.bl + (1 if nseg > lay.nsx else 0)

    def body(x_ref, w_ref, m_ref, dh_ref, dres_ref, dx_ref, dw_ref, dm_ref):
        s = pl.program_id(1)
        _, vjp = jax.vjp(_norm_mod_f, x_ref[...], w_ref[...], m_ref[ksh:ksh + 1, :], m_ref[ksh + 1:ksh + 2, :])
        dx, dw, dsh, dsc = vjp(dh_ref[...])
        res = jnp.where(s < nres, dres_ref[...], 0.0)
        dx_ref[...] = dx + res
        _acc(dw_ref, s == 0, dw)
        _acc(dm_ref, lay.first_of_row(s), jnp.concatenate([dsh, dsc], axis=0))

    return _tok_call(
        name, body, 1, nseg,
        [_tok(lay, d), _glob(1, d), _row(lay, N_MOD, d), _tok(lay, d), _tok(lay, d, clamp=nres - 1)],
        [_tok(lay, d), _glob(1, d), _row(lay, 2, d)],
        [jax.ShapeDtypeStruct((nseg * lay.tt, d), F32), jax.ShapeDtypeStruct((1, d), F32),
         jax.ShapeDtypeStruct((nrow, 2, d), F32)],
        (x, w, modv, dh, dres))


def _resid_fwd(lay, nseg, x, o, modv, kg, coef, name):
    d = lay.d

    def body(x_ref, o_ref, m_ref, y_ref):
        y_ref[...] = x_ref[...] + (coef * m_ref[kg:kg + 1, :]) * o_ref[...]

    return _tok_call(name, body, 1, nseg, [_tok(lay, d), _tok(lay, d), _row(lay, N_MOD, d)], _tok(lay, d),
                     jax.ShapeDtypeStruct((nseg * lay.tt, d), F32), (x, o, modv))


def _resid_bwd(lay, nseg, dy, o, modv, kg, coef, name):
    d = lay.d
    nrow = lay.bl + (1 if nseg > lay.nsx else 0)

    def body(dy_ref, o_ref, m_ref, do_ref, dg_ref):
        s = pl.program_id(1)
        dy = dy_ref[...]
        do_ref[...] = (dy * (coef * m_ref[kg:kg + 1, :])).astype(do_ref.dtype)
        _acc(dg_ref, lay.first_of_row(s), jnp.sum(dy * o_ref[...], axis=0, keepdims=True) * coef)

    return _tok_call(name, body, 1, nseg, [_tok(lay, d), _tok(lay, d), _row(lay, N_MOD, d)],
                     [_tok(lay, d), _row(lay, 1, d)],
                     [jax.ShapeDtypeStruct((nseg * lay.tt, d), BF16), jax.ShapeDtypeStruct((nrow, 1, d), F32)],
                     (dy, o, modv))


def _final_loss(lay, x, wf, target, name):
    d = lay.d

    def body(x_ref, w_ref, t_ref, loss_ref, dx_ref, dw_ref):
        s = pl.program_id(1)

        def f(xv, wv):
            return xv * lax.rsqrt(jnp.mean(xv * xv, axis=-1, keepdims=True) + EPS) * wv

        y, vjp = jax.vjp(f, x_ref[...], w_ref[...])
        err = y - t_ref[...]
        part = 0.5 * jnp.sum(jnp.sum(err * err, axis=-1, keepdims=True), axis=0, keepdims=True) / d
        dx, dw = vjp(err / d)
        dx_ref[...] = dx
        _acc(loss_ref, s == 0, part)
        _acc(dw_ref, s == 0, dw)

    return _tok_call(name, body, 1, lay.nsx, [_tok(lay, d), _glob(1, d), _tok(lay, d)],
                     [_glob(1, 1), _tok(lay, d), _glob(1, d)],
                     [jax.ShapeDtypeStruct((1, 1), F32), jax.ShapeDtypeStruct((lay.tx, d), F32),
                      jax.ShapeDtypeStruct((1, d), F32)], (x, wf, target))


class _Mix:
    def __init__(self, d, heads):
        self.d_ssm = d
        self.d_conv = d
        self.heads = heads
        assert heads * HEAD_DIM == d and heads % (2 * SSD_GROUPS) == 0 and 2 * heads <= DT_LANES
        self.gn = SSD_GROUPS * N_STATE
        self.xw = d + 2 * self.gn
        self.off_x = d
        self.off_dt = d + self.xw
        self.off_glu = self.off_dt + DT_PAD
        self.pw = self.off_glu + 2 * d
        self.off_dt_ref = d + self.xw
        self.off_glu_ref = self.off_dt_ref + 2 * heads
        self.cc = _pick(self.xw, (512, 256, 128))


def _conv5_fwd(lay, mx, proj, cw, cb, name):
    c, tt = mx.cc, lay.tt
    cb0 = mx.off_x // c
    assert mx.off_x % c == 0

    def body(prev_ref, cur_ref, next_ref, w_ref, b_ref, pre_ref, act_ref, ext_ref):
        s = pl.program_id(1)
        ext_ref[0:HALO, :] = jnp.where(lay.seq_first(s), 0.0, prev_ref[...])
        ext_ref[HALO:HALO + tt, :] = cur_ref[...]
        ext_ref[HALO + tt:, :] = jnp.where(lay.seq_last(s), 0.0, next_ref[...])
        acc = jnp.zeros((tt, c), F32) + b_ref[...]
        for k in range(5):
            acc = acc + w_ref[k:k + 1, :] * ext_ref[pl.ds(HALO + k - 2, tt), :]
        pre_ref[...] = acc
        act_ref[...] = _silu(acc)

    sh = jax.ShapeDtypeStruct((lay.ta, mx.xw), F32)
    return _tok_call(name, body, mx.xw // c, lay.ns,
                     [_halo_prev(lay, c, cb0), _tok(lay, c, cb0), _halo_next(lay, c, cb0), _glob(8, c, 0), _glob(1, c, 0)],
                     [_tok(lay, c), _tok(lay, c)], [sh, sh], (proj, proj, proj, cw, cb),
                     scratch=[pltpu.VMEM((tt + 2 * HALO, c), F32)])


def _conv5_bwd(lay, mx, proj, pre, dact_f, dact_b, cw, name):
    c, tt = mx.cc, lay.tt
    cb0 = mx.off_x // c

    def dsilu(p):
        sg = jax.nn.sigmoid(p)
        return sg * (1.0 + p * (1.0 - sg))

    def body(xp_ref, xc_ref, xn_ref, pp_ref, pc_ref, pn_ref, fp_ref, fc_ref, fn_ref, bp_ref, bc_ref, bn_ref, w_ref,
             dx_ref, dw_ref, db_ref, extx_ref, extd_ref):
        s = pl.program_id(1)
        first, last = lay.seq_first(s), lay.seq_last(s)
        dcur = (fc_ref[...] + bc_ref[...]) * dsilu(pc_ref[...])
        extd_ref[0:HALO, :] = jnp.where(first, 0.0, (fp_ref[...] + bp_ref[...]) * dsilu(pp_ref[...]))
        extd_ref[HALO:HALO + tt, :] = dcur
        extd_ref[HALO + tt:, :] = jnp.where(last, 0.0, (fn_ref[...] + bn_ref[...]) * dsilu(pn_ref[...]))
        extx_ref[0:HALO, :] = jnp.where(first, 0.0, xp_ref[...])
        extx_ref[HALO:HALO + tt, :] = xc_ref[...]
        extx_ref[HALO + tt:, :] = jnp.where(last, 0.0, xn_ref[...])
        dx = jnp.zeros((tt, c), F32)
        rows = []
        for k in range(5):
            dx = dx + w_ref[k:k + 1, :] * extd_ref[pl.ds(HALO - (k - 2), tt), :]
            rows.append(jnp.sum(dcur * extx_ref[pl.ds(HALO + k - 2, tt), :], axis=0, keepdims=True))
        dx_ref[...] = dx.astype(dx_ref.dtype)
        rows.append(jnp.zeros((3, c), F32))
        _acc(dw_ref, s == 0, jnp.concatenate(rows, axis=0))
        _acc(db_ref, s == 0, jnp.sum(dcur, axis=0, keepdims=True))

    three = lambda cbx: [_halo_prev(lay, c, cbx), _tok(lay, c, cbx), _halo_next(lay, c, cbx)]
    ext = pltpu.VMEM((tt + 2 * HALO, c), F32)
    return _tok_call(name, body, mx.xw // c, lay.ns,
                     three(cb0) + three(0) + three(0) + three(0) + [_glob(8, c, 0)],
                     [_tok(lay, c), _glob(8, c, 0), _glob(1, c, 0)],
                     [jax.ShapeDtypeStruct((lay.ta, mx.xw), BF16), jax.ShapeDtypeStruct((8, mx.xw), F32),
                      jax.ShapeDtypeStruct((1, mx.xw), F32)],
                     (proj, proj, proj, pre, pre, pre, dact_f, dact_f, dact_f, dact_b, dact_b, dact_b, cw),
                     scratch=[ext, ext])


def _softplus(v):
    return jnp.maximum(v, 0.0) + jnp.log1p(jnp.exp(-jnp.abs(v)))


def _dt_fwd(lay, mx, proj, bias, name):
    cb = mx.off_dt // DT_LANES

    def body(p_ref, b_ref, dt_ref):
        dt_ref[...] = _softplus(p_ref[...] + b_ref[...])

    return _tok_call(name, body, 1, lay.ns, [_tok(lay, DT_LANES, cb), _glob(1, DT_LANES)], _tok(lay, DT_LANES),
                     jax.ShapeDtypeStruct((lay.ta, DT_LANES), F32), (proj, bias))


def _dt_bwd(lay, mx, proj, bias, parts, name):
    cb = mx.off_dt // DT_LANES

    def body(p_ref, b_ref, a_ref, b2_ref, c_ref, d_ref, dp_ref, db_ref):
        s = pl.program_id(1)
        ddt = (a_ref[...] + b2_ref[...]) + (c_ref[...] + d_ref[...])
        draw = ddt * jax.nn.sigmoid(p_ref[...] + b_ref[...])
        dp_ref[:, 0:DT_LANES] = draw.astype(dp_ref.dtype)
        dp_ref[:, DT_LANES:] = jnp.zeros((lay.tt, DT_PAD - DT_LANES), dp_ref.dtype)
        _acc(db_ref, s == 0, jnp.sum(draw, axis=0, keepdims=True))

    t = _tok(lay, DT_LANES)
    return _tok_call(name, body, 1, lay.ns, [_tok(lay, DT_LANES, cb), _glob(1, DT_LANES), t, t, t, t],
                     [_tok(lay, DT_PAD), _glob(1, DT_LANES)],
                     [jax.ShapeDtypeStruct((lay.ta, DT_PAD), BF16), jax.ShapeDtypeStruct((1, DT_LANES), F32)],
                     (proj, bias) + tuple(parts))


def _ssd_chunk(xh_pairs, bcs, ccs, dtc, dtr, a_row, a_col, st_pairs, *, rev, heads, col0):
    n = CHUNK
    r = lax.broadcasted_iota(jnp.int32, (n, n), 0)
    c = lax.broadcasted_iota(jnp.int32, (n, n), 1)
    mask = (r <= c) if rev else (r >= c)
    tri = mask.astype(F32)
    hi = lax.Precision.HIGHEST
    da_c = dtc * a_row
    cs_c = jnp.dot(tri, da_c, precision=hi, preferred_element_type=F32)
    da_r = dtr * a_col
    cs_r = lax.dot_general(da_r, tri, (((1,), (1,)), ((), ())), precision=hi, preferred_element_type=F32)
    tot = jnp.sum(da_c, axis=0, keepdims=True)
    lane = lax.broadcasted_iota(jnp.int32, (n, DT_LANES), 1)
    sub = lax.broadcasted_iota(jnp.int32, (DT_LANES, n), 0)
    lane1 = lax.broadcasted_iota(jnp.int32, (1, DT_LANES), 1)
    left = lax.broadcasted_iota(jnp.int32, (n, 2 * HEAD_DIM), 1) < HEAD_DIM
    top = lax.broadcasted_iota(jnp.int32, (2 * HEAD_DIM, 1), 0) < HEAD_DIM
    nt = (((1,), (1,)), ((), ()))
    tn = (((0,), (0,)), ((), ()))
    scores = [lax.dot_general(ccs[g].astype(BF16), bcs[g].astype(BF16), nt, preferred_element_type=F32)
              for g in range(SSD_GROUPS)]
    ys, sts = [], []
    for p in range(heads // 2):
        g = (2 * p) // (heads // SSD_GROUPS)
        per = []
        for h in (2 * p, 2 * p + 1):
            col = col0 + h
            csv = jnp.sum(jnp.where(lane == col, cs_c, 0.0), axis=1, keepdims=True)
            csr = jnp.sum(jnp.where(sub == col, cs_r, 0.0), axis=0, keepdims=True)
            dtv = jnp.sum(jnp.where(lane == col, dtc, 0.0), axis=1, keepdims=True)
            tv = jnp.sum(jnp.where(lane1 == col, tot, 0.0), axis=1, keepdims=True)
            m = scores[g] * jnp.exp(jnp.where(mask, csv - csr, NEG_BIG))
            per.append((csv, dtv, tv, m))
        (cs1, dt1, t1, m1), (cs2, dt2, t2, m2) = per
        xs = xh_pairs[p] * jnp.where(left, dt1, dt2)
        xsb = xs.astype(BF16)
        y_diag = jnp.where(left, jnp.dot(m1.astype(BF16), xsb, preferred_element_type=F32),
                           jnp.dot(m2.astype(BF16), xsb, preferred_element_type=F32))
        y_off = lax.dot_general(ccs[g].astype(BF16), st_pairs[p].astype(BF16), nt, preferred_element_type=F32)
        y_off = y_off * jnp.where(left, jnp.exp(cs1), jnp.exp(cs2))
        wst = jnp.where(left, jnp.exp(t1 - cs1), jnp.exp(t2 - cs2))
        cst = lax.dot_general((xs * wst).astype(BF16), bcs[g].astype(BF16), tn, preferred_element_type=F32)
        ys.append(y_diag + y_off)
        sts.append(st_pairs[p] * jnp.where(top, jnp.exp(t1), jnp.exp(t2)) + cst)
    return ys, sts


class _Scan:
    def __init__(self, lay, rev):
        self.ncx, self.ncc, self.bl, self.rev = lay.seq // CHUNK, lay.clen // CHUNK, lay.bl, rev
        self.nct = self.ncx + self.ncc

    def chunk(self, b, pos):
        kc = (self.ncc - 1 - pos) if self.rev else pos
        kx = (self.ncx - 1 - (pos - self.ncc)) if self.rev else (pos - self.ncc)
        return jnp.where(pos < self.ncc, self.bl * self.ncx + b * self.ncc + kc, b * self.ncx + kx)


def _ssd_io(mx, x_ref, st_src):
    np_ = mx.heads // 2
    d = mx.d_ssm
    xh = [x_ref[:, 128 * p:128 * (p + 1)] for p in range(np_)]
    bcs = [x_ref[:, d + N_STATE * g:d + N_STATE * (g + 1)] for g in range(SSD_GROUPS)]
    ccs = [x_ref[:, d + mx.gn + N_STATE * g:d + mx.gn + N_STATE * (g + 1)] for g in range(SSD_GROUPS)]
    sts = [st_src[128 * p:128 * (p + 1), :] for p in range(np_)]
    return xh, bcs, ccs, sts


def _ssd_fwd(lay, mx, xbc, dt, dtt, a_row, a_col, rev, name):
    sc = _Scan(lay, rev)
    col0 = mx.heads if rev else 0
    hp = mx.heads * HEAD_DIM

    def body(x_ref, dt_ref, dtt_ref, ar_ref, ac_ref, y_ref, hp_ref, st_ref):
        @pl.when(pl.program_id(1) == 0)
        def _():
            st_ref[...] = jnp.zeros_like(st_ref)

        hp_ref[...] = st_ref[...]
        xh, bcs, ccs, sts = _ssd_io(mx, x_ref, st_ref)
        ys, new = _ssd_chunk(xh, bcs, ccs, dt_ref[...], dtt_ref[...], ar_ref[...], ac_ref[...], sts,
                             rev=rev, heads=mx.heads, col0=col0)
        for p in range(mx.heads // 2):
            y_ref[:, 128 * p:128 * (p + 1)] = ys[p]
            st_ref[128 * p:128 * (p + 1), :] = new[p]

    ch = sc.chunk
    return pl.pallas_call(
        body, grid=(lay.bl, sc.nct),
        in_specs=[pl.BlockSpec((CHUNK, mx.xw), lambda b, i: (ch(b, i), 0)),
                  pl.BlockSpec((CHUNK, DT_LANES), lambda b, i: (ch(b, i), 0)),
                  pl.BlockSpec((DT_LANES, CHUNK), lambda b, i: (0, ch(b, i))),
                  pl.BlockSpec((1, DT_LANES), lambda b, i: (0, 0)),
                  pl.BlockSpec((DT_LANES, 1), lambda b, i: (0, 0))],
        out_specs=[pl.BlockSpec((CHUNK, mx.d_ssm), lambda b, i: (ch(b, i), 0)),
                   pl.BlockSpec((hp, N_STATE), lambda b, i: (b * sc.nct + i, 0))],
        out_shape=[jax.ShapeDtypeStruct((lay.ta, mx.d_ssm), F32),
                   jax.ShapeDtypeStruct((lay.bl * sc.nct * hp, N_STATE), F32)],
        scratch_shapes=[pltpu.VMEM((hp, N_STATE), F32)], name=name, compiler_params=_cparams(2),
    )(xbc, dt, dtt, a_row, a_col)


def _ssd_bwd(lay, mx, xbc, dt, dtt, a_row, a_col, hprev, dy, dskip, rev, name):
    sc = _Scan(lay, rev)
    col0 = mx.heads if rev else 0
    hp = mx.heads * HEAD_DIM
    np_ = mx.heads // 2
    d = mx.d_ssm
    with_skip = dskip is not None

    def body(*refs):
        if with_skip:
            x_ref, dt_ref, dtt_ref, ar_ref, ac_ref, hp_ref, dy_ref, sk_ref = refs[:8]
            rest = refs[8:]
        else:
            x_ref, dt_ref, dtt_ref, ar_ref, ac_ref, hp_ref, dy_ref = refs[:7]
            rest = refs[7:]
        dx_ref, ddc_ref, ddr_ref, dar_ref, dac_ref, ds_ref = rest
        b, i = pl.program_id(0), pl.program_id(1)

        @pl.when(i == 0)
        def _():
            ds_ref[...] = jnp.zeros_like(ds_ref)

        xh, bcs, ccs, sts = _ssd_io(mx, x_ref, hp_ref)
        fn = functools.partial(_ssd_chunk, rev=rev, heads=mx.heads, col0=col0)
        _, vjp = jax.vjp(fn, xh, bcs, ccs, dt_ref[...], dtt_ref[...], ar_ref[...], ac_ref[...], sts)
        dys = [dy_ref[:, 128 * p:128 * (p + 1)] for p in range(np_)]
        dsts = [ds_ref[128 * p:128 * (p + 1), :] for p in range(np_)]
        dxh, dbc, dcc, ddc, ddr, dar, dac, dst = vjp((dys, dsts))
        for p in range(np_):
            v = dxh[p]
            if with_skip:
                v = v + dys[p] * sk_ref[:, 128 * p:128 * (p + 1)]
            dx_ref[:, 128 * p:128 * (p + 1)] = v
            ds_ref[128 * p:128 * (p + 1), :] = dst[p]
        for g in range(SSD_GROUPS):
            dx_ref[:, d + N_STATE * g:d + N_STATE * (g + 1)] = dbc[g]
            dx_ref[:, d + mx.gn + N_STATE * g:d + mx.gn + N_STATE * (g + 1)] = dcc[g]
        ddc_ref[...] = ddc
        ddr_ref[...] = ddr
        first = jnp.logical_and(b == 0, i == 0)
        _acc(dar_ref, first, dar)
        _acc(dac_ref, first, dac)

    ch = lambda b, i: sc.chunk(b, sc.nct - 1 - i)
    in_specs = [pl.BlockSpec((CHUNK, mx.xw), lambda b, i: (ch(b, i), 0)),
                pl.BlockSpec((CHUNK, DT_LANES), lambda b, i: (ch(b, i), 0)),
                pl.BlockSpec((DT_LANES, CHUNK), lambda b, i: (0, ch(b, i))),
                pl.BlockSpec((1, DT_LANES), lambda b, i: (0, 0)),
                pl.BlockSpec((DT_LANES, 1), lambda b, i: (0, 0)),
                pl.BlockSpec((hp, N_STATE), lambda b, i: (b * sc.nct + sc.nct - 1 - i, 0)),
                pl.BlockSpec((CHUNK, d), lambda b, i: (ch(b, i), 0))]
    inputs = [xbc, dt, dtt, a_row, a_col, hprev, dy]
    if with_skip:
        in_specs.append(pl.BlockSpec((1, d), lambda b, i: (0, 0)))
        inputs.append(dskip)
    return pl.pallas_call(
        body, grid=(lay.bl, sc.nct), in_specs=in_specs,
        out_specs=[pl.BlockSpec((CHUNK, mx.xw), lambda b, i: (ch(b, i), 0)),
                   pl.BlockSpec((CHUNK, DT_LANES), lambda b, i: (ch(b, i), 0)),
                   pl.BlockSpec((DT_LANES, CHUNK), lambda b, i: (0, ch(b, i))),
                   pl.BlockSpec((1, DT_LANES), lambda b, i: (0, 0)),
                   pl.BlockSpec((DT_LANES, 1), lambda b, i: (0, 0))],
        out_shape=[jax.ShapeDtypeStruct((lay.ta, mx.xw), F32), jax.ShapeDtypeStruct((lay.ta, DT_LANES), F32),
                   jax.ShapeDtypeStruct((DT_LANES, lay.ta), F32), jax.ShapeDtypeStruct((1, DT_LANES), F32),
                   jax.ShapeDtypeStruct((DT_LANES, 1), F32)],
        scratch_shapes=[pltpu.VMEM((hp, N_STATE), F32)], name=name, compiler_params=_cparams(2),
    )(*inputs)


def _gate_f(yf, yb, xh, z, drow, nw):
    dd = yf.shape[-1]
    half = dd // SSD_GROUPS
    yz = (yf + yb + drow * xh) * _silu(z)
    lo = lax.broadcasted_iota(jnp.int32, yz.shape, 1) < half
    sq = yz * yz
    ms1 = jnp.sum(jnp.where(lo, sq, 0.0), axis=-1, keepdims=True) / half
    ms2 = jnp.sum(jnp.where(lo, 0.0, sq), axis=-1, keepdims=True) / half
    return yz * jnp.where(lo, lax.rsqrt(ms1 + EPS), lax.rsqrt(ms2 + EPS)) * nw


def _gate_fwd(lay, mx, yf, yb, xbc, proj, drow, nw, name):
    d = mx.d_ssm

    def body(yf_ref, yb_ref, xh_ref, z_ref, d_ref, w_ref, o_ref):
        o_ref[...] = _gate_f(yf_ref[...], yb_ref[...], xh_ref[...], z_ref[...], d_ref[...], w_ref[...]).astype(o_ref.dtype)

    t = _tok(lay, d)
    return _tok_call(name, body, 1, lay.nsx, [t, t, t, t, _glob(1, d), _glob(1, d)], t,
                     jax.ShapeDtypeStruct((lay.tx, d), BF16), (yf, yb, xbc, proj, drow, nw))


def _gate_bwd(lay, mx, yf, yb, xbc, proj, drow, nw, dcat, name):
    d = mx.d_ssm
    nsx = lay.nsx

    def body(yf_ref, yb_ref, xh_ref, z_ref, d_ref, w_ref, dc_ref, dy_ref, dz_ref, dd_ref, dw_ref):
        s = pl.program_id(1)

        @pl.when(s < nsx)
        def _():
            _, vjp = jax.vjp(_gate_f, yf_ref[...], yb_ref[...], xh_ref[...], z_ref[...], d_ref[...], w_ref[...])
            dyf, _, _, dz, dd, dw = vjp(dc_ref[...])
            dy_ref[...] = dyf
            dz_ref[...] = dz.astype(dz_ref.dtype)
            _acc(dd_ref, s == 0, dd)
            _acc(dw_ref, s == 0, dw)

        @pl.when(s >= nsx)
        def _():
            dy_ref[...] = jnp.zeros_like(dy_ref)
            dz_ref[...] = jnp.zeros_like(dz_ref)

    t = _tok(lay, d)
    return _tok_call(name, body, 1, lay.ns, [t, t, t, t, _glob(1, d), _glob(1, d), _tok(lay, d, clamp=nsx - 1)],
                     [t, t, _glob(1, d), _glob(1, d)],
                     [jax.ShapeDtypeStruct((lay.ta, d), F32), jax.ShapeDtypeStruct((lay.ta, d), BF16),
                      jax.ShapeDtypeStruct((1, d), F32), jax.ShapeDtypeStruct((1, d), F32)],
                     (yf, yb, xbc, proj, drow, nw, dcat))


def _glu_fwd(lay, mx, proj, name):
    d = mx.d_conv
    c = math.gcd(mx.off_glu, d)
    cb = mx.off_glu // c

    def body(a_ref, b_ref, o_ref):
        o_ref[...] = a_ref[...] * jax.nn.sigmoid(b_ref[...])

    return _tok_call(name, body, d // c, lay.nsx, [_tok(lay, c, cb), _tok(lay, c, cb + d // c)], _tok(lay, c),
                     jax.ShapeDtypeStruct((lay.tx, d), F32), (proj, proj))


def _glu_bwd(lay, mx, proj, du, name):
    d = mx.d_conv
    c = math.gcd(mx.off_glu, d)
    cb = mx.off_glu // c
    nsx = lay.nsx

    def body(a_ref, b_ref, du_ref, da_ref, db_ref):
        s = pl.program_id(1)

        @pl.when(s < nsx)
        def _():
            sg = jax.nn.sigmoid(b_ref[...])
            duv = du_ref[...]
            da_ref[...] = (duv * sg).astype(da_ref.dtype)
            db_ref[...] = (duv * a_ref[...] * sg * (1.0 - sg)).astype(db_ref.dtype)

        @pl.when(s >= nsx)
        def _():
            da_ref[...] = jnp.zeros_like(da_ref)
            db_ref[...] = jnp.zeros_like(db_ref)

    sh = jax.ShapeDtypeStruct((lay.ta, d), BF16)
    return _tok_call(name, body, d // c, lay.ns,
                     [_tok(lay, c, cb), _tok(lay, c, cb + d // c), _tok(lay, c, clamp=nsx - 1)],
                     [_tok(lay, c), _tok(lay, c)], [sh, sh], (proj, proj, du))


def _axial(lay, mx, u, dy, cw, cb, name):
    d, seq = mx.d_conv, lay.seq
    kw = cw.shape[0]
    pad = kw // 2
    c = _pick(d // 2, (256, 128))
    ncb = d // c
    zpad = GRID_W * pad
    zpad = -(-zpad // 8) * 8
    backward = dy is not None

    def shifted(ext_ref, off):
        return ext_ref[pl.ds(zpad + off, seq), :]

    def valid_row(off):
        col = lax.broadcasted_iota(jnp.int32, (seq, c), 0) % GRID_W
        return jnp.logical_and(col + off >= 0, col + off < GRID_W)

    def fill(ext_ref, v):
        ext_ref[0:zpad, :] = jnp.zeros((zpad, c), F32)
        ext_ref[zpad:zpad + seq, :] = v
        ext_ref[zpad + seq:, :] = jnp.zeros((zpad, c), F32)

    def conv(ext_ref, w_ref, is_row, sign):
        acc = jnp.zeros((seq, c), F32)
        for k in range(kw):
            off = sign * ((k - pad) if is_row else GRID_W * (k - pad))
            v = shifted(ext_ref, off)
            if is_row:
                v = jnp.where(valid_row(off), v, 0.0)
            acc = acc + w_ref[k:k + 1, :] * v
        return acc

    def fwd_body(u_ref, w_ref, b_ref, o_ref, ext_ref):
        j = pl.program_id(0)
        fill(ext_ref, u_ref[...])

        @pl.when(j < ncb // 2)
        def _():
            o_ref[...] = conv(ext_ref, w_ref, True, 1) + b_ref[...]

        @pl.when(j >= ncb // 2)
        def _():
            o_ref[...] = conv(ext_ref, w_ref, False, 1) + b_ref[...]

    def bwd_body(u_ref, dy_ref, w_ref, du_ref, dw_ref, db_ref, extu_ref, extd_ref):
        j, b = pl.program_id(0), pl.program_id(1)
        dyv = dy_ref[...]
        fill(extu_ref, u_ref[...])
        fill(extd_ref, dyv)

        def grads(is_row):
            du_ref[...] = conv(extd_ref, w_ref, is_row, -1)
            rows = []
            for k in range(kw):
                off = (k - pad) if is_row else GRID_W * (k - pad)
                v = shifted(extu_ref, off)
                if is_row:
                    v = jnp.where(valid_row(off), v, 0.0)
                rows.append(jnp.sum(dyv * v, axis=0, keepdims=True))
            _acc(dw_ref, b == 0, jnp.concatenate(rows, axis=0))

        @pl.when(j < ncb // 2)
        def _():
            grads(True)

        @pl.when(j >= ncb // 2)
        def _():
            grads(False)

        _acc(db_ref, b == 0, jnp.sum(dyv, axis=0, keepdims=True))

    seq_spec = pl.BlockSpec((seq, c), lambda j, b: (b, j))
    w_spec = pl.BlockSpec((kw, c), lambda j, b: (0, j))
    b_spec = pl.BlockSpec((1, c), lambda j, b: (0, j))
    ext = pltpu.VMEM((seq + 2 * zpad, c), F32)
    if not backward:
        return pl.pallas_call(fwd_body, grid=(ncb, lay.bl), in_specs=[seq_spec, w_spec, b_spec], out_specs=seq_spec,
                              out_shape=jax.ShapeDtypeStruct((lay.tx, d), F32), scratch_shapes=[ext], name=name,
                              compiler_params=_cparams(2))(u, cw, cb)
    return pl.pallas_call(bwd_body, grid=(ncb, lay.bl), in_specs=[seq_spec, seq_spec, w_spec],
                          out_specs=[seq_spec, w_spec, b_spec],
                          out_shape=[jax.ShapeDtypeStruct((lay.tx, d), F32), jax.ShapeDtypeStruct((kw, d), F32),
                                     jax.ShapeDtypeStruct((1, d), F32)],
                          scratch_shapes=[ext, ext], name=name, compiler_params=_cparams(2))(u, dy, cw)


def _ln_silu_f(u, w, b):
    mu = jnp.mean(u, axis=-1, keepdims=True)
    var = jnp.mean(jnp.square(u - mu), axis=-1, keepdims=True)
    return _silu((u - mu) * lax.rsqrt(var + EPS) * w + b)


def _ln_fwd(lay, mx, u, w, b, name):
    d = mx.d_conv

    def body(u_ref, w_ref, b_ref, o_ref):
        o_ref[...] = _ln_silu_f(u_ref[...], w_ref[...], b_ref[...]).astype(o_ref.dtype)

    return _tok_call(name, body, 1, lay.nsx, [_tok(lay, d), _glob(1, d), _glob(1, d)], _tok(lay, d),
                     jax.ShapeDtypeStruct((lay.tx, d), BF16), (u, w, b))


def _ln_bwd(lay, mx, u, w, b, dcat, name):
    d = mx.d_conv

    def body(u_ref, w_ref, b_ref, dc_ref, du_ref, dw_ref, db_ref):
        s = pl.program_id(1)
        _, vjp = jax.vjp(_ln_silu_f, u_ref[...], w_ref[...], b_ref[...])
        du, dw, db = vjp(dc_ref[...])
        du_ref[...] = du
        _acc(dw_ref, s == 0, dw)
        _acc(db_ref, s == 0, db)

    return _tok_call(name, body, 1, lay.nsx, [_tok(lay, d), _glob(1, d), _glob(1, d), _tok(lay, d, 1)],
                     [_tok(lay, d), _glob(1, d), _glob(1, d)],
                     [jax.ShapeDtypeStruct((lay.tx, d), F32), jax.ShapeDtypeStruct((1, d), F32),
                      jax.ShapeDtypeStruct((1, d), F32)], (u, w, b, dcat))


def _ffn_fwd(lay, nseg, x, nw, modv, k0, wts, tag):
    wg, wu, wd = wts
    h = _norm_mod_fwd(lay, nseg, x, nw, modv, k0, tag + "_norm")
    g, u, a = _ffn_gu(h, wg, wu, tag + "_gu")
    o = _ffn_contract([a], [wd], "nn", tag + "_down")
    y = _resid_fwd(lay, nseg, x, o, modv, k0 + 2, 0.5, tag + "_res")
    return y, (x, h, g, u, a, o)


def _ffn_bwd(lay, nseg, dy, saved, nw, modv, k0, wts, tag):
    wg, wu, wd = wts
    x, h, g, u, a, o = saved
    do, dgate = _resid_bwd(lay, nseg, dy, o, modv, k0 + 2, 0.5, tag + "_dres")
    dg, du = _ffn_da(do, wd, g, u, tag + "_da")
    dwd = _ffn_dwd(a, do, tag + "_dwd")
    dh = _ffn_contract([dg, du], [wg, wu], "nt", tag + "_dh")
    dwg, dwu = _ffn_dw(h, [dg, du], tag + "_dwgu")
    dx, dnw, dss = _norm_mod_bwd(lay, nseg, nseg, x, nw, modv, k0, dh, dy, tag + "_dnorm")
    return dx, (dwg, dwu, dwd), dnw, jnp.concatenate([dss, dgate], axis=1)


def _local_step(lay, mx, xa, target, modv, w):
    d, bl = lay.d, lay.bl
    g = {}
    xa1, ffn1 = _ffn_fwd(lay, lay.ns, xa, w["norm_ffn1"], modv, 0, w["ffn1"], "ffn1")
    ha = _norm_mod_fwd(lay, lay.ns, xa1, w["norm_mix"], modv, 3, "mix_norm")
    proj = _mm(ha, w["w_in"], "nn", F32, "mix_in")
    pre, xbc = _conv5_fwd(lay, mx, proj, w["conv_w"], w["conv_b"], "mix_conv")
    dt = _dt_fwd(lay, mx, proj, w["dt_bias"], "mix_dt")
    dtt = dt.T
    yf, hpf = _ssd_fwd(lay, mx, xbc, dt, dtt, w["a_row"], w["a_col"], False, "ssd_f")
    yb, hpb = _ssd_fwd(lay, mx, xbc, dt, dtt, w["a_row"], w["a_col"], True, "ssd_b")
    cat_y = _gate_fwd(lay, mx, yf, yb, xbc, proj, w["d_row"], w["ssm_norm_w"], "mix_gate")
    u0 = _glu_fwd(lay, mx, proj, "mix_glu")
    uc = _axial(lay, mx, u0, None, w["cconv_w"], w["cconv_b"], "mix_axial")
    cat_u = _ln_fwd(lay, mx, uc, w["ln_w"], w["ln_b"], "mix_ln")
    cat = jnp.concatenate([cat_y, cat_u], axis=1)
    mix = _mm(cat, w["w_out"], "nn", F32, "mix_out")
    x2 = _resid_fwd(lay, lay.nsx, xa1, mix, modv, 5, 1.0, "mix_res")
    x3, ffn2 = _ffn_fwd(lay, lay.nsx, x2, w["norm_ffn2"], modv, 6, w["ffn2"], "ffn2")
    loss, dx3, g["final_norm"] = _final_loss(lay, x3, w["final_norm"], target, "loss")
    dx2, g["ffn2"], g["norm_ffn2"], dmod2 = _ffn_bwd(lay, lay.nsx, dx3, ffn2, w["norm_ffn2"], modv, 6, w["ffn2"], "ffn2")
    dmix, dg2 = _resid_bwd(lay, lay.nsx, dx2, mix, modv, 5, 1.0, "mix_dres")
    dcat = _mm(dmix, w["w_out"], "nt", F32, "mix_dcat")
    g["w_out"] = _mm(cat, dmix, "tn", F32, "mix_dwout")
    duc, g["ln_w"], g["ln_b"] = _ln_bwd(lay, mx, uc, w["ln_w"], w["ln_b"], dcat, "mix_dln")
    du0, g["cconv_w"], g["cconv_b"] = _axial(lay, mx, u0, duc, w["cconv_w"], None, "mix_daxial")
    dglu_a, dglu_b = _glu_bwd(lay, mx, proj, du0, "mix_dglu")
    dyssd, dz, g["d_row"], g["ssm_norm_w"] = _gate_bwd(lay, mx, yf, yb, xbc, proj, w["d_row"], w["ssm_norm_w"], dcat,
                                                       "mix_dgate")
    dxf, ddcf, ddrf, darf, dacf = _ssd_bwd(lay, mx, xbc, dt, dtt, w["a_row"], w["a_col"], hpf, dyssd, w["d_row"],
                                           False, "ssd_df")
    dxb, ddcb, ddrb, darb, dacb = _ssd_bwd(lay, mx, xbc, dt, dtt, w["a_row"], w["a_col"], hpb, dyssd, None,
                                           True, "ssd_db")
    g["a_row"] = (darf + darb) + (dacf + dacb).T
    dxbc, g["conv_w"], g["conv_b"] = _conv5_bwd(lay, mx, proj, pre, dxf, dxb, w["conv_w"], "mix_dconv")
    ddtraw, g["dt_bias"] = _dt_bwd(lay, mx, proj, w["dt_bias"], (ddcf, ddcb, ddrf.T, ddrb.T), "mix_ddt")
    dproj = jnp.concatenate([dz, dxbc, ddtraw, dglu_a, dglu_b], axis=1)
    dha = _mm(dproj, w["w_in"], "nt", F32, "mix_dha")
    g["w_in"] = _mm(ha, dproj, "tn", F32, "mix_dwin")
    dxa1, g["norm_mix"], dss_mix = _norm_mod_bwd(lay, lay.ns, lay.nsx, xa1, w["norm_mix"], modv, 3, dha, dx2, "mix_dnorm")
    dxa, g["ffn1"], g["norm_ffn1"], dmod1 = _ffn_bwd(lay, lay.ns, dxa1, ffn1, w["norm_ffn1"], modv, 0, w["ffn1"], "ffn1")
    zrow = lambda t: jnp.concatenate([t, jnp.zeros((1,) + t.shape[1:], F32)], axis=0)
    dmodv = jnp.concatenate([dmod1, dss_mix, zrow(dg2), zrow(dmod2)], axis=1)
    return loss, dxa[:lay.tx], g, dmodv


def _all_gather(xs, name, in_hbm):
    na = len(xs)

    def body(*refs):
        x_refs, out_refs = refs[:na], refs[na:2 * na]
        send_sems, recv_sems, local_sems = refs[2 * na:]
        mx_, my_, mc_ = lax.axis_index("x"), lax.axis_index("y"), lax.axis_index("c")
        me, sibling = (mx_, my_, mc_), (mx_, my_, 1 - mc_)
        chips = [(1 - mx_, my_), (mx_, 1 - my_), (1 - mx_, 1 - my_)]

        def slot(a, px, py, pc):
            return out_refs[a].at[4 * px + 2 * py + pc]

        def copy(a, k, block, to, own=False):
            return pltpu.make_async_remote_copy(
                src_ref=x_refs[a] if own else slot(a, *block), dst_ref=slot(a, *block),
                send_sem=send_sems.at[7 * a + k], recv_sem=recv_sems.at[7 * a + k], device_id=to, device_id_type=MESH)

        mine = [pltpu.make_async_copy(x_refs[a], slot(a, *me), local_sems.at[a]) for a in range(na)]
        for cp in mine:
            cp.start()
        first = []
        for a in range(na):
            first.append(copy(a, 0, me, sibling, own=True))
            first += [copy(a, 1 + j, me, (*chip, mc_), own=True) for j, chip in enumerate(chips)]
        for cp in first:
            cp.start()
        passed = []
        for j, chip in enumerate(chips):
            for a in range(na):
                copy(a, 1 + j, (*chip, mc_), me).wait_recv()
                fwd = copy(a, 4 + j, (*chip, mc_), sibling)
                fwd.start()
                passed.append(fwd)
        for a in range(na):
            copy(a, 0, sibling, me).wait_recv()
            for j, chip in enumerate(chips):
                copy(a, 4 + j, (*chip, 1 - mc_), me).wait_recv()
        for cp in first + passed:
            cp.wait_send()
        for cp in mine:
            cp.wait()

    space = pl.ANY if in_hbm else pltpu.VMEM
    return pl.pallas_call(
        body, out_shape=[jax.ShapeDtypeStruct((N_DEV,) + tuple(x.shape), x.dtype) for x in xs],
        in_specs=[pl.BlockSpec(memory_space=space)] * na, out_specs=[pl.BlockSpec(memory_space=space)] * na,
        scratch_shapes=[pltpu.SemaphoreType.DMA((7 * na,)), pltpu.SemaphoreType.DMA((7 * na,)),
                        pltpu.SemaphoreType.DMA((na,))],
        name=name,
    )(*xs)


N_CHIPS = 4


def _swap_sibling(gs, name):
    na = len(gs)

    def body(*refs):
        g_refs, out_refs, send_sems, recv_sems = refs[:na], refs[na:2 * na], refs[2 * na], refs[2 * na + 1]
        mx_, my_, mc_ = lax.axis_index("x"), lax.axis_index("y"), lax.axis_index("c")
        copies = [pltpu.make_async_remote_copy(
            src_ref=g_refs[a].at[k, 1 - mc_], dst_ref=out_refs[a].at[k], send_sem=send_sems.at[N_CHIPS * a + k],
            recv_sem=recv_sems.at[N_CHIPS * a + k], device_id=(mx_, my_, 1 - mc_), device_id_type=MESH)
            for a in range(na) for k in range(N_CHIPS)]
        for cp in copies:
            cp.start()
        for cp in copies:
            cp.wait_recv()
        for cp in copies:
            cp.wait_send()

    return pl.pallas_call(
        body, out_shape=[jax.ShapeDtypeStruct((N_CHIPS,) + tuple(g.shape[2:]), g.dtype) for g in gs],
        in_specs=[pl.BlockSpec(memory_space=pl.ANY)] * na, out_specs=[pl.BlockSpec(memory_space=pl.ANY)] * na,
        scratch_shapes=[pltpu.SemaphoreType.DMA((N_CHIPS * na,)), pltpu.SemaphoreType.DMA((N_CHIPS * na,))], name=name,
    )(*gs)


def _row_tile(r, n):
    if r * n * 4 <= (1 << 20):
        return r
    for t in (1024, 512, 256, 128, 64, 32, 16, 8):
        if r % t == 0 and t * n * 4 <= (1 << 20):
            return t
    return r


def _pair_add(place, g, got, name):
    _, _, r, n = g.shape
    tr = _row_tile(r, n)

    def body(place_ref, g_ref, got_ref, o_ref):
        o_ref[...] = g_ref[...] + got_ref[...]

    grid_spec = pltpu.PrefetchScalarGridSpec(
        num_scalar_prefetch=1, grid=(N_CHIPS, r // tr),
        in_specs=[pl.BlockSpec((None, None, tr, n), lambda k, i, pr: (k, pr[0], i, 0)),
                  pl.BlockSpec((None, tr, n), lambda k, i, pr: (k, i, 0))],
        out_specs=pl.BlockSpec((None, tr, n), lambda k, i, pr: (k, i, 0)))
    return pl.pallas_call(body, grid_spec=grid_spec, out_shape=jax.ShapeDtypeStruct((N_CHIPS, r, n), g.dtype),
                          name=name, compiler_params=_cparams(2))(place, g, got)


def _swap_chips(ps, name):
    na = len(ps)

    def body(*refs):
        p_refs, out_refs, send_sems, recv_sems = refs[:na], refs[na:2 * na], refs[2 * na], refs[2 * na + 1]
        mx_, my_, mc_ = lax.axis_index("x"), lax.axis_index("y"), lax.axis_index("c")
        chips = [(1 - mx_, my_), (mx_, 1 - my_), (1 - mx_, 1 - my_)]
        copies = [pltpu.make_async_remote_copy(
            src_ref=p_refs[a].at[2 * cx + cy], dst_ref=out_refs[a].at[j], send_sem=send_sems.at[N_CHIP_PEERS * a + j],
            recv_sem=recv_sems.at[N_CHIP_PEERS * a + j], device_id=(cx, cy, mc_), device_id_type=MESH)
            for a in range(na) for j, (cx, cy) in enumerate(chips)]
        for cp in copies:
            cp.start()
        for cp in copies:
            cp.wait_recv()
        for cp in copies:
            cp.wait_send()

    return pl.pallas_call(
        body, out_shape=[jax.ShapeDtypeStruct((N_CHIP_PEERS,) + tuple(p.shape[1:]), p.dtype) for p in ps],
        in_specs=[pl.BlockSpec(memory_space=pl.ANY)] * na, out_specs=[pl.BlockSpec(memory_space=pl.ANY)] * na,
        scratch_shapes=[pltpu.SemaphoreType.DMA((N_CHIP_PEERS * na,)), pltpu.SemaphoreType.DMA((N_CHIP_PEERS * na,))],
        name=name,
    )(*ps)


def _sum_lead(x, name):
    k, r, n = x.shape
    tr = _row_tile(r, n * k)

    def body(x_ref, o_ref):
        acc = x_ref[0]
        for i in range(1, k):
            acc = acc + x_ref[i]
        o_ref[...] = acc

    return pl.pallas_call(body, grid=(r // tr,), in_specs=[pl.BlockSpec((k, tr, n), lambda i: (0, i, 0))],
                          out_specs=pl.BlockSpec((tr, n), lambda i: (i, 0)),
                          out_shape=jax.ShapeDtypeStruct((r, n), x.dtype), name=name, compiler_params=_cparams(1))(x)


def _adamw(place, w, parts, m, v, name):
    shape = w.shape
    cols = shape[-1]
    rows = math.prod(shape[:-1])
    to2 = lambda t: t.reshape(rows, cols)
    tr = _row_tile(rows, cols) if rows * cols * 4 > (1 << 20) else rows
    npart = len(parts)
    spec = pl.BlockSpec((tr, cols), lambda i, pr: (i, 0))
    part_specs, part_args = [], []
    for piece in parts:
        if isinstance(piece, tuple):
            stack, k = piece
            part_args.append(stack.reshape(stack.shape[0], rows, cols))
            if k == "chip":
                part_specs.append(pl.BlockSpec((None, tr, cols), lambda i, pr: (pr[1], i, 0)))
            else:
                part_specs.append(pl.BlockSpec((None, tr, cols), functools.partial(lambda i, pr, kk: (kk, i, 0), kk=k)))
        else:
            part_args.append(to2(piece))
            part_specs.append(spec)

    def body(place_ref, *refs):
        w_ref, m_ref, v_ref = refs[0], refs[1 + npart], refs[2 + npart]
        g_ref, d_ref, nm_ref, nv_ref = refs[3 + npart:]
        g = refs[1][...]
        for q in range(1, npart):
            g = g + refs[1 + q][...]
        mm = ADAM_B1 * m_ref[...] + (1.0 - ADAM_B1) * g
        vv = ADAM_B2 * v_ref[...] + (1.0 - ADAM_B2) * jnp.square(g)
        m_hat = mm / (1.0 - ADAM_B1 ** ADAM_STEP)
        v_hat = vv / (1.0 - ADAM_B2 ** ADAM_STEP)
        g_ref[...] = g
        d_ref[...] = -ADAM_LR * (m_hat / (jnp.sqrt(v_hat) + ADAM_EPS) + ADAM_WD * w_ref[...])
        nm_ref[...] = mm
        nv_ref[...] = vv

    sh = jax.ShapeDtypeStruct((rows, cols), F32)
    grid_spec = pltpu.PrefetchScalarGridSpec(num_scalar_prefetch=1, grid=(rows // tr,),
                                             in_specs=[spec] + part_specs + [spec, spec], out_specs=[spec] * 4)
    outs = pl.pallas_call(body, grid_spec=grid_spec, out_shape=[sh] * 4, name=name, compiler_params=_cparams(1),
                          )(place, to2(w), *part_args, to2(m), to2(v))
    return tuple(o.reshape(shape) for o in outs)


def _pack_rows(items, width):
    rows = []
    for t in items:
        flat = t.reshape(-1)
        n = flat.shape[0]
        k = -(-n // width)
        if k * width > n:
            flat = jnp.concatenate([flat, jnp.zeros((k * width - n,), t.dtype)])
        rows.append(flat.reshape(k, width))
    out = jnp.concatenate(rows, axis=0)
    pad = -out.shape[0] % 8
    if pad:
        out = jnp.concatenate([out, jnp.zeros((pad, width), out.dtype)], axis=0)
    return out


def _unpack_rows(packed, shapes, lead=()):
    width = packed.shape[-1]
    out, r = [], 0
    for sh in shapes:
        n = math.prod(sh)
        k = -(-n // width)
        piece = packed[..., r:r + k, :].reshape(tuple(lead) + (k * width,))[..., :n]
        out.append(piece.reshape(tuple(lead) + tuple(sh)))
        r += k
    return out


def _cols_full(t):
    return jnp.transpose(t, (1, 0, 2)).reshape(t.shape[1], -1)


def _cols_shards(t):
    d = t.shape[0]
    return jnp.transpose(t.reshape(d, N_DEV, -1), (1, 0, 2))


BIG = ("ffn1_gate", "ffn1_up", "ffn1_down", "w_in", "w_out", "ffn2_gate", "ffn2_up", "ffn2_down")
ROW_SHARDED = ("ffn1_down", "w_out", "ffn2_down")


def kernel(x, c, ctx, c_ctx, w_mod, b_mod, norm_ffn1, ffn1_gate, ffn1_up, ffn1_down, norm_mix, w_in, ssm_conv_w, ssm_conv_b, dt_bias_fwd, dt_bias_bwd, a_log_fwd, a_log_bwd, ssm_d, ssm_norm_w, cconv_w, cconv_b, cconv_ln_w, cconv_ln_b, w_out, norm_ffn2, ffn2_gate, ffn2_up, ffn2_down, final_norm, loss_target, m_c_ctx, m_w_mod, m_b_mod, m_norm_ffn1, m_ffn1_gate, m_ffn1_up, m_ffn1_down, m_norm_mix, m_w_in, m_ssm_conv_w, m_ssm_conv_b, m_dt_bias_fwd, m_dt_bias_bwd, m_a_log_fwd, m_a_log_bwd, m_ssm_d, m_ssm_norm_w, m_cconv_w, m_cconv_b, m_cconv_ln_w, m_cconv_ln_b, m_w_out, m_norm_ffn2, m_ffn2_gate, m_ffn2_up, m_ffn2_down, m_final_norm, v_c_ctx, v_w_mod, v_b_mod, v_norm_ffn1, v_ffn1_gate, v_ffn1_up, v_ffn1_down, v_norm_mix, v_w_in, v_ssm_conv_w, v_ssm_conv_b, v_dt_bias_fwd, v_dt_bias_bwd, v_a_log_fwd, v_a_log_bwd, v_ssm_d, v_ssm_norm_w, v_cconv_w, v_cconv_b, v_cconv_ln_w, v_cconv_ln_b, v_w_out, v_norm_ffn2, v_ffn2_gate, v_ffn2_up, v_ffn2_down, v_final_norm):
    args = dict(locals())
    names = ("c_ctx", "w_mod", "b_mod", "norm_ffn1", "ffn1_gate", "ffn1_up", "ffn1_down", "norm_mix", "w_in",
             "ssm_conv_w", "ssm_conv_b", "dt_bias_fwd", "dt_bias_bwd", "a_log_fwd", "a_log_bwd", "ssm_d", "ssm_norm_w",
             "cconv_w", "cconv_b", "cconv_ln_w", "cconv_ln_b", "w_out", "norm_ffn2", "ffn2_gate", "ffn2_up",
             "ffn2_down", "final_norm")
    wts = {n: args[n] for n in names}
    bl, seq, d = x.shape
    clen = ctx.shape[1]
    heads = dt_bias_fwd.shape[1]
    dff = ffn1_gate.shape[2] * N_DEV
    lay = _Lay(bl, seq, clen, d)
    mx = _Mix(d, heads)
    nb = bl * N_DEV
    me = 4 * lax.axis_index("x") + 2 * lax.axis_index("y") + lax.axis_index("c")
    mcols = w_mod.shape[2]
    n_ctx_mod = 5 * d

    place = jnp.stack([lax.axis_index("c"), 2 * lax.axis_index("x") + lax.axis_index("y")]).astype(jnp.int32)

    small_shapes = [(bl, d), ssm_conv_w.shape[1:], cconv_w.shape[1:]]
    (g1,) = _all_gather([_pack_rows([c, ssm_conv_w, cconv_w], d)], "gather_small", False)
    c_g, conv_g, cconv_g = _unpack_rows(g1, small_shapes, (N_DEV,))
    c_all = c_g.reshape(nb, d)
    conv_w_full = jnp.transpose(conv_g, (1, 0, 2)).reshape(conv_g.shape[1], -1)
    cconv_w_full = jnp.transpose(cconv_g, (1, 0, 2)).reshape(cconv_g.shape[1], -1)

    s_all = jnp.concatenate([_silu(c_all), _silu(c_ctx)[None, :], jnp.zeros((7, d), F32)], axis=0)
    mod_cols = _mm(s_all, w_mod[0], "nn", F32, "mod_cols")
    (g2,) = _all_gather([mod_cols], "gather_mod", False)
    mod_all = _cols_full(g2) + b_mod
    mod_mine = jnp.concatenate([lax.dynamic_slice_in_dim(mod_all, me * bl, bl, axis=0), mod_all[nb:nb + 1]], axis=0)
    modv = mod_mine.reshape(bl + 1, N_MOD, d)

    full = dict(zip(BIG, _all_gather([wts[n][0].astype(BF16) for n in BIG], "gather_weights", True)))
    hh = 2 * heads
    w_in_f = _cols_full(full["w_in"])
    w_in_p = jnp.concatenate([w_in_f[:, :mx.off_dt_ref + hh], jnp.zeros((d, DT_PAD - hh), BF16),
                              w_in_f[:, mx.off_glu_ref:]], axis=1)
    lanes_pad = lambda a, b: jnp.concatenate([a, b, jnp.zeros((1, DT_LANES - hh), F32)], axis=1)
    a_vals = lanes_pad(-jnp.exp(a_log_fwd), -jnp.exp(a_log_bwd))
    w = {
        "norm_ffn1": norm_ffn1, "norm_mix": norm_mix, "norm_ffn2": norm_ffn2, "final_norm": final_norm[None, :],
        "ffn1": (full["ffn1_gate"], full["ffn1_up"], full["ffn1_down"]),
        "ffn2": (full["ffn2_gate"], full["ffn2_up"], full["ffn2_down"]),
        "w_in": w_in_p, "w_out": full["w_out"].reshape(-1, d),
        "conv_w": jnp.concatenate([conv_w_full, jnp.zeros((3, mx.xw), F32)], axis=0), "conv_b": ssm_conv_b,
        "dt_bias": lanes_pad(dt_bias_fwd, dt_bias_bwd), "a_row": a_vals, "a_col": a_vals.T,
        "d_row": jnp.repeat(ssm_d, HEAD_DIM, axis=1), "ssm_norm_w": ssm_norm_w,
        "cconv_w": cconv_w_full, "cconv_b": cconv_b, "ln_w": cconv_ln_w, "ln_b": cconv_ln_b,
    }

    xa = jnp.concatenate([x.reshape(bl * seq, d), ctx.reshape(bl * clen, d)], axis=0)
    loss, grad_x, g, dmodv = _local_step(lay, mx, xa, loss_target.reshape(bl * seq, d), modv, w)
    loss = lax.psum(loss[0, 0], ("x", "y", "c"))

    dw_in = jnp.concatenate([g["w_in"][:, :mx.off_dt_ref + hh], g["w_in"][:, mx.off_glu:]], axis=1)
    gbig = dict(zip(("ffn1_gate", "ffn1_up", "ffn1_down"), g["ffn1"]))
    gbig.update(zip(("ffn2_gate", "ffn2_up", "ffn2_down"), g["ffn2"]))
    gbig["w_in"] = _cols_shards(dw_in)
    gbig["w_out"] = g["w_out"].reshape((N_DEV,) + tuple(w_out.shape[1:]))
    by_dest = [gbig[n].reshape((N_CHIPS, 2) + tuple(gbig[n].shape[1:])) for n in BIG]
    got = _swap_sibling(by_dest, "rs_sibling")
    chip_sum = [_pair_add(place, t, s, "rs_pair_add_" + n) for n, t, s in zip(BIG, by_dest, got)]
    from_chips = _swap_chips(chip_sum, "rs_chips")

    n9 = N_MOD * d
    dmod_rows = dmodv.reshape(bl + 1, n9)
    ctx_row = jnp.concatenate([dmod_rows[bl, :n_ctx_mod], jnp.zeros((n9 - n_ctx_mod,), F32)])
    summed = [ctx_row, g["norm_ffn1"], g["norm_mix"], g["norm_ffn2"], g["final_norm"], g["conv_b"], g["dt_bias"],
              g["a_row"], g["d_row"], g["ssm_norm_w"], g["cconv_b"], g["ln_w"], g["ln_b"], g["conv_w"][:5], g["cconv_w"]]
    sum_shapes = [t.shape for t in summed]
    (g4,) = _all_gather([_pack_rows([dmod_rows[:bl]] + summed, d)], "gather_small_grads", False)
    nrow_batch = bl * N_MOD
    dmod_batch = g4[:, :nrow_batch].reshape(nb, n9)
    tot = _sum_lead(g4[:, nrow_batch:], "sum_small_grads")
    (dctx, g_n1, g_nm, g_n2, g_fn, g_cb, g_dtb, g_a, g_drow, g_snw, g_ccb, g_lnw, g_lnb, g_cw, g_ccw) = _unpack_rows(tot, sum_shapes)
    dmod_all = jnp.concatenate([dmod_batch, dctx[None, :], jnp.zeros((7, n9), F32)], axis=0)

    dmod_my_cols = lax.dynamic_slice_in_dim(dmod_all, me * mcols, mcols, axis=1)
    g_w_mod = _mm(s_all, dmod_my_cols, "tn", F32, "dw_mod")[None]
    g_b_mod = _sum_lead(dmod_all.reshape(nb + 8, N_MOD, d), "db_mod").reshape(1, n9)
    ds_part = _mm(dmod_my_cols[nb:nb + 8], w_mod[0], "nt", F32, "ds_ctx")
    (g5,) = _all_gather([jnp.concatenate([ds_part[0:1], jnp.zeros((7, d), F32)], axis=0)], "gather_ds_ctx", False)
    ds_ctx = _sum_lead(g5, "sum_ds_ctx")[0]
    sg = jax.nn.sigmoid(c_ctx)
    g_c_ctx = ds_ctx * (sg * (1.0 + c_ctx * (1.0 - sg)))

    a_f, a_b = a_vals[:, :heads], a_vals[:, heads:hh]
    grads = {
        "c_ctx": [g_c_ctx], "w_mod": [g_w_mod], "b_mod": [g_b_mod],
        "norm_ffn1": [g_n1], "norm_mix": [g_nm], "norm_ffn2": [g_n2], "final_norm": [g_fn.reshape(-1)],
        "ssm_conv_w": [lax.dynamic_slice_in_dim(g_cw, me * ssm_conv_w.shape[2], ssm_conv_w.shape[2], axis=1)[None]],
        "ssm_conv_b": [g_cb],
        "dt_bias_fwd": [g_dtb[:, :heads]], "dt_bias_bwd": [g_dtb[:, heads:hh]],
        "a_log_fwd": [g_a[:, :heads] * a_f], "a_log_bwd": [g_a[:, heads:hh] * a_b],
        "ssm_d": [jnp.sum(g_drow.reshape(1, heads, HEAD_DIM), axis=2)], "ssm_norm_w": [g_snw],
        "cconv_w": [lax.dynamic_slice_in_dim(g_ccw, me * cconv_w.shape[2], cconv_w.shape[2], axis=1)[None]],
        "cconv_b": [g_ccb], "cconv_ln_w": [g_lnw], "cconv_ln_b": [g_lnb],
    }
    for n, own, others in zip(BIG, chip_sum, from_chips):
        grads[n] = [(own, "chip"), (others, 0), (others, 1), (others, 2)]

    out_g, out_d, out_m, out_v = [], [], [], []
    for n in names:
        gr, de, nm, nv = _adamw(place, wts[n], grads[n], args["m_" + n], args["v_" + n], "adamw_" + n)
        out_g.append(gr)
        out_d.append(de)
        out_m.append(nm)
        out_v.append(nv)
    return (loss, grad_x.reshape(bl, seq, d), *out_g, *out_d, *out_m, *out_v)
```

```python
import functools
import math

import jax
import jax.numpy as jnp
from jax import lax
from jax.experimental import pallas as pl
from jax.experimental.pallas import tpu as pltpu

F32 = jnp.float32
BF16 = jnp.bfloat16
MESH = pl.DeviceIdType.MESH

N_DEV = 8
N_CHIP_PEERS = 3
HEAD_DIM = 64
N_STATE = 128
SSD_GROUPS = 2
CHUNK = 128
GRID_W = 64
N_MOD = 9
EPS = 1e-6
DT_PAD = 512
DT_LANES = 128
HALO = 8
ROW_TILE = 512
FINE_ROW_TILE = 256
VMEM_LIMIT = 48 * 1024 * 1024
NEG_BIG = -1e30

ADAM_LR = 0.001
ADAM_B1 = 0.9
ADAM_B2 = 0.999
ADAM_EPS = 1e-08
ADAM_WD = 0.01
ADAM_STEP = 10


def _pick(n, prefs):
    for p in prefs:
        if n % p == 0:
            return p
    return n


MM_TILE_CAP = 2560
MM_TILE_ELEMS = 2560 * 1024


def _big_tile(n, cap):
    if n <= cap:
        return n
    best = 0
    for t in range(128, cap + 1, 128):
        if n % t == 0:
            best = t
    return best or n


def _cparams(ndim):
    return pltpu.CompilerParams(dimension_semantics=("arbitrary",) * ndim, vmem_limit_bytes=VMEM_LIMIT)


def _silu(v):
    return v * jax.nn.sigmoid(v)


def _mm(a, b, mode, out_dtype, name):
    if mode == "tn":
        (K, M), (K2, N) = a.shape, b.shape
    elif mode == "nt":
        (M, K), (N, K2) = a.shape, b.shape
    else:
        (M, K), (K2, N) = a.shape, b.shape
    assert K == K2, (name, a.shape, b.shape)
    tm = M if M <= 512 else _pick(M, (512, 256, 128, 64, 32, 16, 8))
    tn = _big_tile(N, MM_TILE_CAP)
    tk = _big_tile(K, min(MM_TILE_CAP, MM_TILE_ELEMS // tn))
    nk = K // tk
    if mode == "tn":
        a_spec = pl.BlockSpec((tk, tm), lambda i, j, k: (k, i))
        dn = (((0,), (0,)), ((), ()))
    else:
        a_spec = pl.BlockSpec((tm, tk), lambda i, j, k: (i, k))
        dn = (((1,), (1,)), ((), ())) if mode == "nt" else (((1,), (0,)), ((), ()))
    if mode == "nt":
        b_spec = pl.BlockSpec((tn, tk), lambda i, j, k: (j, k))
    else:
        b_spec = pl.BlockSpec((tk, tn), lambda i, j, k: (k, j))

    def body(a_ref, b_ref, o_ref, *acc):
        part = lax.dot_general(a_ref[...].astype(BF16), b_ref[...].astype(BF16), dn, preferred_element_type=F32)
        if nk == 1:
            o_ref[...] = part.astype(o_ref.dtype)
            return
        acc_ref, k = acc[0], pl.program_id(2)
        _acc(acc_ref, k == 0, part)

        @pl.when(k == nk - 1)
        def _():
            o_ref[...] = acc_ref[...].astype(o_ref.dtype)

    return pl.pallas_call(
        body, grid=(M // tm, N // tn, nk), in_specs=[a_spec, b_spec],
        out_specs=pl.BlockSpec((tm, tn), lambda i, j, k: (i, j)),
        out_shape=jax.ShapeDtypeStruct((M, N), out_dtype),
        scratch_shapes=[pltpu.VMEM((tm, tn), F32)] if nk > 1 else [], name=name, compiler_params=_cparams(3),
    )(a, b)


def _ffn_tm(t):
    return _pick(t, (1024, 512, 256, 128, 64, 32, 16, 8))


def _dsilu_mul(g, u, da):
    sg = jax.nn.sigmoid(g)
    return da * u * (sg * (1.0 + g * (1.0 - sg))), da * (g * sg)


def _ffn_gu(h, wg, wu, name):
    t, d = h.shape
    nb, _, fb = wg.shape
    tm = _ffn_tm(t)

    def body(h_ref, wg_ref, wu_ref, g_ref, u_ref, a_ref):
        hv = h_ref[...]
        g = jnp.dot(hv, wg_ref[...], preferred_element_type=F32)
        u = jnp.dot(hv, wu_ref[...], preferred_element_type=F32)
        g_ref[...] = g
        u_ref[...] = u
        a_ref[...] = (_silu(g) * u).astype(a_ref.dtype)

    wspec = pl.BlockSpec((None, d, fb), lambda i, j: (j, 0, 0))
    ospec = pl.BlockSpec((None, tm, fb), lambda i, j: (j, i, 0))
    sh = jax.ShapeDtypeStruct((nb, t, fb), F32)
    return pl.pallas_call(body, grid=(t // tm, nb), in_specs=[pl.BlockSpec((tm, d), lambda i, j: (i, 0)), wspec, wspec],
                          out_specs=[ospec, ospec, ospec], out_shape=[sh, sh, jax.ShapeDtypeStruct((nb, t, fb), BF16)],
                          name=name, compiler_params=_cparams(2))(h, wg, wu)


def _ffn_contract(xs, ws, mode, name):
    nb, t, fb = xs[0].shape
    d = ws[0].shape[2] if mode == "nn" else ws[0].shape[1]
    tm = _ffn_tm(t)
    npair = len(xs)
    dn = (((1,), (0,)), ((), ())) if mode == "nn" else (((1,), (1,)), ((), ()))

    def body(*refs):
        o_ref, acc_ref = refs[2 * npair], refs[2 * npair + 1]
        j = pl.program_id(1)
        part = lax.dot_general(refs[0][...], refs[npair][...], dn, preferred_element_type=F32)
        for p in range(1, npair):
            part = part + lax.dot_general(refs[p][...], refs[npair + p][...], dn, preferred_element_type=F32)
        _acc(acc_ref, j == 0, part)

        @pl.when(j == nb - 1)
        def _():
            o_ref[...] = acc_ref[...]

    xspec = pl.BlockSpec((None, tm, fb), lambda i, j: (j, i, 0))
    wspec = pl.BlockSpec((None,) + tuple(ws[0].shape[1:]), lambda i, j: (j, 0, 0))
    return pl.pallas_call(body, grid=(t // tm, nb), in_specs=[xspec] * npair + [wspec] * npair,
                          out_specs=pl.BlockSpec((tm, d), lambda i, j: (i, 0)),
                          out_shape=jax.ShapeDtypeStruct((t, d), F32), scratch_shapes=[pltpu.VMEM((tm, d), F32)],
                          name=name, compiler_params=_cparams(2))(*xs, *ws)


def _ffn_da(do, wd, g, u, name):
    t, d = do.shape
    nb, fb, _ = wd.shape
    tm = _ffn_tm(t)

    def body(do_ref, wd_ref, g_ref, u_ref, dg_ref, du_ref):
        da = lax.dot_general(do_ref[...], wd_ref[...], (((1,), (1,)), ((), ())), preferred_element_type=F32)
        dg, du = _dsilu_mul(g_ref[...], u_ref[...], da)
        dg_ref[...] = dg.astype(dg_ref.dtype)
        du_ref[...] = du.astype(du_ref.dtype)

    bspec = pl.BlockSpec((None, tm, fb), lambda i, j: (j, i, 0))
    sh = jax.ShapeDtypeStruct((nb, t, fb), BF16)
    return pl.pallas_call(body, grid=(t // tm, nb),
                          in_specs=[pl.BlockSpec((tm, d), lambda i, j: (i, 0)),
                                    pl.BlockSpec((None, fb, d), lambda i, j: (j, 0, 0)), bspec, bspec],
                          out_specs=[bspec, bspec], out_shape=[sh, sh], name=name, compiler_params=_cparams(2),
                          )(do, wd, g, u)


def _ffn_dw(h, xs, name):
    t, d = h.shape
    nb, _, fb = xs[0].shape
    tk = _ffn_tm(t)
    nk = t // tk
    npair = len(xs)
    tn_dims = (((0,), (0,)), ((), ()))

    def body(*refs):
        h_ref = refs[0]
        k = pl.program_id(1)
        hv = h_ref[...]
        for p in range(npair):
            part = lax.dot_general(hv, refs[1 + p][...], tn_dims, preferred_element_type=F32)
            _acc(refs[1 + npair + p], k == 0, part)

    xspec = pl.BlockSpec((None, tk, fb), lambda j, k: (j, k, 0))
    ospec = pl.BlockSpec((None, d, fb), lambda j, k: (j, 0, 0))
    sh = jax.ShapeDtypeStruct((nb, d, fb), F32)
    return pl.pallas_call(body, grid=(nb, nk), in_specs=[pl.BlockSpec((tk, d), lambda j, k: (k, 0))] + [xspec] * npair,
                          out_specs=[ospec] * npair, out_shape=[sh] * npair, name=name, compiler_params=_cparams(2),
                          )(h, *xs)


def _ffn_dwd(a, do, name):
    nb, t, fb = a.shape
    d = do.shape[1]
    tk = _ffn_tm(t)

    def body(a_ref, do_ref, o_ref):
        part = lax.dot_general(a_ref[...], do_ref[...], (((0,), (0,)), ((), ())), preferred_element_type=F32)
        _acc(o_ref, pl.program_id(1) == 0, part)

    return pl.pallas_call(body, grid=(nb, t // tk),
                          in_specs=[pl.BlockSpec((None, tk, fb), lambda j, k: (j, k, 0)),
                                    pl.BlockSpec((tk, d), lambda j, k: (k, 0))],
                          out_specs=pl.BlockSpec((None, fb, d), lambda j, k: (j, 0, 0)),
                          out_shape=jax.ShapeDtypeStruct((nb, fb, d), F32), name=name, compiler_params=_cparams(2),
                          )(a, do)


class _Lay:
    def __init__(self, bl, seq, clen, d, tt=None):
        self.bl, self.seq, self.clen, self.d = bl, seq, clen, d
        self.tt = min(ROW_TILE, math.gcd(seq, bl * clen)) if tt is None else tt
        assert seq % self.tt == 0 and (bl * clen) % self.tt == 0 and self.tt % 8 == 0
        self.spb = seq // self.tt
        self.spc = clen // self.tt
        self.nsx = bl * self.spb
        self.nsc = bl * clen // self.tt
        self.ns = self.nsx + self.nsc
        self.tx = bl * seq
        self.ta = self.tx + bl * clen

    def fine(self):
        return _Lay(self.bl, self.seq, self.clen, self.d, min(FINE_ROW_TILE, self.clen))

    def mrow(self, s):
        return jnp.where(s < self.nsx, s // self.spb, self.bl)

    def first_of_row(self, s):
        return jnp.logical_or(jnp.logical_and(s < self.nsx, s % self.spb == 0), s == self.nsx)

    def seq_first(self, s):
        return jnp.where(s < self.nsx, s % self.spb == 0, (s - self.nsx) % self.spc == 0)

    def seq_last(self, s):
        return jnp.where(s < self.nsx, s % self.spb == self.spb - 1, (s - self.nsx) % self.spc == self.spc - 1)


def _tok(lay, c, cb=0, clamp=None):
    if clamp is None:
        return pl.BlockSpec((lay.tt, c), lambda j, s: (s, cb + j))
    return pl.BlockSpec((lay.tt, c), lambda j, s: (jnp.minimum(s, clamp), cb + j))


def _halo_prev(lay, c, cb=0):
    u = lay.tt // HALO
    return pl.BlockSpec((HALO, c), lambda j, s: (jnp.maximum(s * u - 1, 0), cb + j))


def _halo_next(lay, c, cb=0):
    u = lay.tt // HALO
    last = lay.ta // HALO - 1
    return pl.BlockSpec((HALO, c), lambda j, s: (jnp.minimum((s + 1) * u, last), cb + j))


def _row(lay, k, c):
    return pl.BlockSpec((None, k, c), lambda j, s: (lay.mrow(s), 0, 0))


def _glob(k, c, cb=None):
    if cb is None:
        return pl.BlockSpec((k, c), lambda j, s: (0, 0))
    return pl.BlockSpec((k, c), lambda j, s: (0, cb + j))


def _tok_call(name, body, ncb, nseg, in_specs, out_specs, out_shape, inputs, scratch=()):
    return pl.pallas_call(body, grid=(ncb, nseg), in_specs=in_specs, out_specs=out_specs, out_shape=out_shape,
                          scratch_shapes=list(scratch), name=name, compiler_params=_cparams(2))(*inputs)


def _acc(ref, first, val):
    @pl.when(first)
    def _():
        ref[...] = val

    @pl.when(jnp.logical_not(first))
    def _():
        ref[...] += val


def _norm_mod_f(x, w, sh, sc):
    y = x * lax.rsqrt(jnp.mean(x * x, axis=-1, keepdims=True) + EPS) * w
    return y * (1.0 + sc) + sh


def _norm_mod_fwd(lay, nseg, x, w, modv, ksh, name):
    d = lay.d

    def body(x_ref, w_ref, m_ref, h_ref):
        h = _norm_mod_f(x_ref[...], w_ref[...], m_ref[ksh:ksh + 1, :], m_ref[ksh + 1:ksh + 2, :])
        h_ref[...] = h.astype(h_ref.dtype)

    return _tok_call(name, body, 1, nseg, [_tok(lay, d), _glob(1, d), _row(lay, N_MOD, d)], _tok(lay, d),
                     jax.ShapeDtypeStruct((nseg * lay.tt, d), BF16), (x, w, modv))


def _norm_mod_bwd(lay, nseg, nres, x, w, modv, ksh, dh, dres, name):
    d = lay.d
    nrow = lay.bl + (1 if nseg > lay.nsx else 0)

    def body(x_ref, w_ref, m_ref, dh_ref, dres_ref, dx_ref, dw_ref, dm_ref):
        s = pl.program_id(1)
        _, vjp = jax.vjp(_norm_mod_f, x_ref[...], w_ref[...], m_ref[ksh:ksh + 1, :], m_ref[ksh + 1:ksh + 2, :])
        dx, dw, dsh, dsc = vjp(dh_ref[...])
        res = jnp.where(s < nres, dres_ref[...], 0.0)
        dx_ref[...] = dx + res
        _acc(dw_ref, s == 0, dw)
        _acc(dm_ref, lay.first_of_row(s), jnp.concatenate([dsh, dsc], axis=0))

    return _tok_call(
        name, body, 1, nseg,
        [_tok(lay, d), _glob(1, d), _row(lay, N_MOD, d), _tok(lay, d), _tok(lay, d, clamp=nres - 1)],
        [_tok(lay, d), _glob(1, d), _row(lay, 2, d)],
        [jax.ShapeDtypeStruct((nseg * lay.tt, d), F32), jax.ShapeDtypeStruct((1, d), F32),
         jax.ShapeDtypeStruct((nrow, 2, d), F32)],
        (x, w, modv, dh, dres))


def _resid_fwd(lay, nseg, x, o, modv, kg, coef, name):
    d = lay.d

    def body(x_ref, o_ref, m_ref, y_ref):
        y_ref[...] = x_ref[...] + (coef * m_ref[kg:kg + 1, :]) * o_ref[...]

    return _tok_call(name, body, 1, nseg, [_tok(lay, d), _tok(lay, d), _row(lay, N_MOD, d)], _tok(lay, d),
                     jax.ShapeDtypeStruct((nseg * lay.tt, d), F32), (x, o, modv))


def _resid_bwd(lay, nseg, dy, o, modv, kg, coef, name):
    d = lay.d
    nrow = lay.bl + (1 if nseg > lay.nsx else 0)

    def body(dy_ref, o_ref, m_ref, do_ref, dg_ref):
        s = pl.program_id(1)
        dy = dy_ref[...]
        do_ref[...] = (dy * (coef * m_ref[kg:kg + 1, :])).astype(do_ref.dtype)
        _acc(dg_ref, lay.first_of_row(s), jnp.sum(dy * o_ref[...], axis=0, keepdims=True) * coef)

    return _tok_call(name, body, 1, nseg, [_tok(lay, d), _tok(lay, d), _row(lay, N_MOD, d)],
                     [_tok(lay, d), _row(lay, 1, d)],
                     [jax.ShapeDtypeStruct((nseg * lay.tt, d), BF16), jax.ShapeDtypeStruct((nrow, 1, d), F32)],
                     (dy, o, modv))


def _final_loss(lay, x, wf, target, name):
    d = lay.d

    def body(x_ref, w_ref, t_ref, loss_ref, dx_ref, dw_ref):
        s = pl.program_id(1)

        def f(xv, wv):
            return xv * lax.rsqrt(jnp.mean(xv * xv, axis=-1, keepdims=True) + EPS) * wv

        y, vjp = jax.vjp(f, x_ref[...], w_ref[...])
        err = y - t_ref[...]
        part = 0.5 * jnp.sum(jnp.sum(err * err, axis=-1, keepdims=True), axis=0, keepdims=True) / d
        dx, dw = vjp(err / d)
        dx_ref[...] = dx
        _acc(loss_ref, s == 0, part)
        _acc(dw_ref, s == 0, dw)

    return _tok_call(name, body, 1, lay.nsx, [_tok(lay, d), _glob(1, d), _tok(lay, d)],
                     [_glob(1, 1), _tok(lay, d), _glob(1, d)],
                     [jax.ShapeDtypeStruct((1, 1), F32), jax.ShapeDtypeStruct((lay.tx, d), F32),
                      jax.ShapeDtypeStruct((1, d), F32)], (x, wf, target))


class _Mix:
    def __init__(self, d, heads):
        self.d_ssm = d
        self.d_conv = d
        self.heads = heads
        assert heads * HEAD_DIM == d and heads % (2 * SSD_GROUPS) == 0 and 2 * heads <= DT_LANES
        self.gn = SSD_GROUPS * N_STATE
        self.xw = d + 2 * self.gn
        self.off_x = d
        self.off_dt = d + self.xw
        self.off_glu = self.off_dt + DT_PAD
        self.pw = self.off_glu + 2 * d
        self.off_dt_ref = d + self.xw
        self.off_glu_ref = self.off_dt_ref + 2 * heads
        self.cc = _pick(self.xw, (512, 256, 128))


def _conv5_fwd(lay, mx, proj, cw, cb, name):
    c, tt = mx.cc, lay.tt
    cb0 = mx.off_x // c
    assert mx.off_x % c == 0

    def body(prev_ref, cur_ref, next_ref, w_ref, b_ref, pre_ref, act_ref, ext_ref):
        s = pl.program_id(1)
        ext_ref[0:HALO, :] = jnp.where(lay.seq_first(s), 0.0, prev_ref[...])
        ext_ref[HALO:HALO + tt, :] = cur_ref[...]
        ext_ref[HALO + tt:, :] = jnp.where(lay.seq_last(s), 0.0, next_ref[...])
        acc = jnp.zeros((tt, c), F32) + b_ref[...]
        for k in range(5):
            acc = acc + w_ref[k:k + 1, :] * ext_ref[pl.ds(HALO + k - 2, tt), :]
        pre_ref[...] = acc
        act_ref[...] = _silu(acc)

    sh = jax.ShapeDtypeStruct((lay.ta, mx.xw), F32)
    return _tok_call(name, body, mx.xw // c, lay.ns,
                     [_halo_prev(lay, c, cb0), _tok(lay, c, cb0), _halo_next(lay, c, cb0), _glob(8, c, 0), _glob(1, c, 0)],
                     [_tok(lay, c), _tok(lay, c)], [sh, sh], (proj, proj, proj, cw, cb),
                     scratch=[pltpu.VMEM((tt + 2 * HALO, c), F32)])


def _conv5_bwd(lay, mx, proj, pre, dact_f, dact_b, cw, name):
    c, tt = mx.cc, lay.tt
    cb0 = mx.off_x // c

    def dsilu(p):
        sg = jax.nn.sigmoid(p)
        return sg * (1.0 + p * (1.0 - sg))

    def body(xp_ref, xc_ref, xn_ref, pp_ref, pc_ref, pn_ref, fp_ref, fc_ref, fn_ref, bp_ref, bc_ref, bn_ref, w_ref,
             dx_ref, dw_ref, db_ref, extx_ref, extd_ref):
        s = pl.program_id(1)
        first, last = lay.seq_first(s), lay.seq_last(s)
        dcur = (fc_ref[...] + bc_ref[...]) * dsilu(pc_ref[...])
        extd_ref[0:HALO, :] = jnp.where(first, 0.0, (fp_ref[...] + bp_ref[...]) * dsilu(pp_ref[...]))
        extd_ref[HALO:HALO + tt, :] = dcur
        extd_ref[HALO + tt:, :] = jnp.where(last, 0.0, (fn_ref[...] + bn_ref[...]) * dsilu(pn_ref[...]))
        extx_ref[0:HALO, :] = jnp.where(first, 0.0, xp_ref[...])
        extx_ref[HALO:HALO + tt, :] = xc_ref[...]
        extx_ref[HALO + tt:, :] = jnp.where(last, 0.0, xn_ref[...])
        dx = jnp.zeros((tt, c), F32)
        rows = []
        for k in range(5):
            dx = dx + w_ref[k:k + 1, :] * extd_ref[pl.ds(HALO - (k - 2), tt), :]
            rows.append(jnp.sum(dcur * extx_ref[pl.ds(HALO + k - 2, tt), :], axis=0, keepdims=True))
        dx_ref[...] = dx.astype(dx_ref.dtype)
        rows.append(jnp.zeros((3, c), F32))
        _acc(dw_ref, s == 0, jnp.concatenate(rows, axis=0))
        _acc(db_ref, s == 0, jnp.sum(dcur, axis=0, keepdims=True))

    three = lambda cbx: [_halo_prev(lay, c, cbx), _tok(lay, c, cbx), _halo_next(lay, c, cbx)]
    ext = pltpu.VMEM((tt + 2 * HALO, c), F32)
    return _tok_call(name, body, mx.xw // c, lay.ns,
                     three(cb0) + three(0) + three(0) + three(0) + [_glob(8, c, 0)],
                     [_tok(lay, c), _glob(8, c, 0), _glob(1, c, 0)],
                     [jax.ShapeDtypeStruct((lay.ta, mx.xw), BF16), jax.ShapeDtypeStruct((8, mx.xw), F32),
                      jax.ShapeDtypeStruct((1, mx.xw), F32)],
                     (proj, proj, proj, pre, pre, pre, dact_f, dact_f, dact_f, dact_b, dact_b, dact_b, cw),
                     scratch=[ext, ext])


def _softplus(v):
    return jnp.maximum(v, 0.0) + jnp.log1p(jnp.exp(-jnp.abs(v)))


def _dt_fwd(lay, mx, proj, bias, name):
    cb = mx.off_dt // DT_LANES

    def body(p_ref, b_ref, dt_ref):
        dt_ref[...] = _softplus(p_ref[...] + b_ref[...])

    return _tok_call(name, body, 1, lay.ns, [_tok(lay, DT_LANES, cb), _glob(1, DT_LANES)], _tok(lay, DT_LANES),
                     jax.ShapeDtypeStruct((lay.ta, DT_LANES), F32), (proj, bias))


def _dt_bwd(lay, mx, proj, bias, parts, name):
    cb = mx.off_dt // DT_LANES

    def body(p_ref, b_ref, a_ref, b2_ref, c_ref, d_ref, dp_ref, db_ref):
        s = pl.program_id(1)
        ddt = (a_ref[...] + b2_ref[...]) + (c_ref[...] + d_ref[...])
        draw = ddt * jax.nn.sigmoid(p_ref[...] + b_ref[...])
        dp_ref[:, 0:DT_LANES] = draw.astype(dp_ref.dtype)
        dp_ref[:, DT_LANES:] = jnp.zeros((lay.tt, DT_PAD - DT_LANES), dp_ref.dtype)
        _acc(db_ref, s == 0, jnp.sum(draw, axis=0, keepdims=True))

    t = _tok(lay, DT_LANES)
    return _tok_call(name, body, 1, lay.ns, [_tok(lay, DT_LANES, cb), _glob(1, DT_LANES), t, t, t, t],
                     [_tok(lay, DT_PAD), _glob(1, DT_LANES)],
                     [jax.ShapeDtypeStruct((lay.ta, DT_PAD), BF16), jax.ShapeDtypeStruct((1, DT_LANES), F32)],
                     (proj, bias) + tuple(parts))


def _scan_mask(rev):
    r = lax.broadcasted_iota(jnp.int32, (CHUNK, CHUNK), 0)
    c = lax.broadcasted_iota(jnp.int32, (CHUNK, CHUNK), 1)
    return (r <= c) if rev else (r >= c)


def _split_bf16(x):
    hi = x.astype(BF16)
    return hi, (x - hi.astype(F32)).astype(BF16)


@functools.partial(jax.custom_vjp, nondiff_argnums=(0,))
def _cum_cols(rev, x):
    m = _scan_mask(rev).astype(BF16)
    hi, lo = _split_bf16(x)
    return jnp.dot(m, hi, preferred_element_type=F32) + jnp.dot(m, lo, preferred_element_type=F32)


_cum_cols.defvjp(lambda rev, x: (_cum_cols(rev, x), None), lambda rev, _, g: (_cum_cols(not rev, g),))


@functools.partial(jax.custom_vjp, nondiff_argnums=(0,))
def _cum_rows(rev, x):
    m = _scan_mask(not rev).astype(BF16)
    hi, lo = _split_bf16(x)
    return jnp.dot(hi, m, preferred_element_type=F32) + jnp.dot(lo, m, preferred_element_type=F32)


_cum_rows.defvjp(lambda rev, x: (_cum_rows(rev, x), None), lambda rev, _, g: (_cum_rows(not rev, g),))


@functools.partial(jax.custom_vjp, nondiff_argnums=(1,))
def _take_col(x, k):
    return x[:, k:k + 1]


def _take_col_bwd(k, _, g):
    lane = lax.broadcasted_iota(jnp.int32, (g.shape[0], DT_LANES), 1)
    return (jnp.where(lane == k, g, 0.0),)


_take_col.defvjp(lambda x, k: (x[:, k:k + 1], None), _take_col_bwd)


@functools.partial(jax.custom_vjp, nondiff_argnums=(1,))
def _take_row(x, k):
    return x[k:k + 1, :]


def _take_row_bwd(k, _, g):
    sub = lax.broadcasted_iota(jnp.int32, (DT_LANES, g.shape[1]), 0)
    return (jnp.where(sub == k, g, 0.0),)


_take_row.defvjp(lambda x, k: (x[k:k + 1, :], None), _take_row_bwd)


def _ssd_chunk(xh_pairs, bcs, ccs, dtc, dtr, a_row, a_col, st_pairs, *, rev, heads, col0):
    n = CHUNK
    mask = _scan_mask(rev)
    da_c = dtc * a_row
    cs_c = _cum_cols(rev, da_c)
    da_r = dtr * a_col
    cs_r = _cum_rows(rev, da_r)
    tot = jnp.sum(da_c, axis=0, keepdims=True)
    left = lax.broadcasted_iota(jnp.int32, (n, 2 * HEAD_DIM), 1) < HEAD_DIM
    top = lax.broadcasted_iota(jnp.int32, (2 * HEAD_DIM, 1), 0) < HEAD_DIM
    nt = (((1,), (1,)), ((), ()))
    tn = (((0,), (0,)), ((), ()))
    scores = [lax.dot_general(ccs[g].astype(BF16), bcs[g].astype(BF16), nt, preferred_element_type=F32)
              for g in range(SSD_GROUPS)]
    ys, sts = [], []
    for p in range(heads // 2):
        g = (2 * p) // (heads // SSD_GROUPS)
        per = []
        for h in (2 * p, 2 * p + 1):
            col = col0 + h
            csv = _take_col(cs_c, col)
            csr = _take_row(cs_r, col)
            dtv = _take_col(dtc, col)
            tv = _take_col(tot, col)
            m = scores[g] * jnp.exp(jnp.where(mask, csv - csr, NEG_BIG))
            per.append((csv, dtv, tv, m))
        (cs1, dt1, t1, m1), (cs2, dt2, t2, m2) = per
        xs = xh_pairs[p] * jnp.where(left, dt1, dt2)
        xsb = xs.astype(BF16)
        y_diag = jnp.where(left, jnp.dot(m1.astype(BF16), xsb, preferred_element_type=F32),
                           jnp.dot(m2.astype(BF16), xsb, preferred_element_type=F32))
        y_off = lax.dot_general(ccs[g].astype(BF16), st_pairs[p].astype(BF16), nt, preferred_element_type=F32)
        y_off = y_off * jnp.where(left, jnp.exp(cs1), jnp.exp(cs2))
        wst = jnp.where(left, jnp.exp(t1 - cs1), jnp.exp(t2 - cs2))
        cst = lax.dot_general((xs * wst).astype(BF16), bcs[g].astype(BF16), tn, preferred_element_type=F32)
        ys.append(y_diag + y_off)
        sts.append(st_pairs[p] * jnp.where(top, jnp.exp(t1), jnp.exp(t2)) + cst)
    return ys, sts


class _Scan:
    def __init__(self, lay, rev):
        self.ncx, self.ncc, self.bl, self.rev = lay.seq // CHUNK, lay.clen // CHUNK, lay.bl, rev
        self.nct = self.ncx + self.ncc

    def chunk(self, b, pos):
        kc = (self.ncc - 1 - pos) if self.rev else pos
        kx = (self.ncx - 1 - (pos - self.ncc)) if self.rev else (pos - self.ncc)
        return jnp.where(pos < self.ncc, self.bl * self.ncx + b * self.ncc + kc, b * self.ncx + kx)


def _ssd_io(mx, x_ref, st_src):
    np_ = mx.heads // 2
    d = mx.d_ssm
    xh = [x_ref[:, 128 * p:128 * (p + 1)] for p in range(np_)]
    bcs = [x_ref[:, d + N_STATE * g:d + N_STATE * (g + 1)] for g in range(SSD_GROUPS)]
    ccs = [x_ref[:, d + mx.gn + N_STATE * g:d + mx.gn + N_STATE * (g + 1)] for g in range(SSD_GROUPS)]
    sts = [st_src[128 * p:128 * (p + 1), :] for p in range(np_)]
    return xh, bcs, ccs, sts


def _ssd_fwd(lay, mx, xbc, dt, dtt, a_row, a_col, rev, name):
    sc = _Scan(lay, rev)
    col0 = mx.heads if rev else 0
    hp = mx.heads * HEAD_DIM

    def body(x_ref, dt_ref, dtt_ref, ar_ref, ac_ref, y_ref, hp_ref, st_ref):
        @pl.when(pl.program_id(1) == 0)
        def _():
            st_ref[...] = jnp.zeros_like(st_ref)

        hp_ref[...] = st_ref[...]
        xh, bcs, ccs, sts = _ssd_io(mx, x_ref, st_ref)
        ys, new = _ssd_chunk(xh, bcs, ccs, dt_ref[...], dtt_ref[...], ar_ref[...], ac_ref[...], sts,
                             rev=rev, heads=mx.heads, col0=col0)
        for p in range(mx.heads // 2):
            y_ref[:, 128 * p:128 * (p + 1)] = ys[p]
            st_ref[128 * p:128 * (p + 1), :] = new[p]

    ch = sc.chunk
    return pl.pallas_call(
        body, grid=(lay.bl, sc.nct),
        in_specs=[pl.BlockSpec((CHUNK, mx.xw), lambda b, i: (ch(b, i), 0)),
                  pl.BlockSpec((CHUNK, DT_LANES), lambda b, i: (ch(b, i), 0)),
                  pl.BlockSpec((DT_LANES, CHUNK), lambda b, i: (0, ch(b, i))),
                  pl.BlockSpec((1, DT_LANES), lambda b, i: (0, 0)),
                  pl.BlockSpec((DT_LANES, 1), lambda b, i: (0, 0))],
        out_specs=[pl.BlockSpec((CHUNK, mx.d_ssm), lambda b, i: (ch(b, i), 0)),
                   pl.BlockSpec((hp, N_STATE), lambda b, i: (b * sc.nct + i, 0))],
        out_shape=[jax.ShapeDtypeStruct((lay.ta, mx.d_ssm), F32),
                   jax.ShapeDtypeStruct((lay.bl * sc.nct * hp, N_STATE), F32)],
        scratch_shapes=[pltpu.VMEM((hp, N_STATE), F32)], name=name, compiler_params=_cparams(2),
    )(xbc, dt, dtt, a_row, a_col)


def _ssd_bwd(lay, mx, xbc, dt, dtt, a_row, a_col, hprev, dy, dskip, rev, name):
    sc = _Scan(lay, rev)
    col0 = mx.heads if rev else 0
    hp = mx.heads * HEAD_DIM
    np_ = mx.heads // 2
    d = mx.d_ssm
    with_skip = dskip is not None

    def body(*refs):
        if with_skip:
            x_ref, dt_ref, dtt_ref, ar_ref, ac_ref, hp_ref, dy_ref, sk_ref = refs[:8]
            rest = refs[8:]
        else:
            x_ref, dt_ref, dtt_ref, ar_ref, ac_ref, hp_ref, dy_ref = refs[:7]
            rest = refs[7:]
        dx_ref, ddc_ref, ddr_ref, dar_ref, dac_ref, ds_ref = rest
        b, i = pl.program_id(0), pl.program_id(1)

        @pl.when(i == 0)
        def _():
            ds_ref[...] = jnp.zeros_like(ds_ref)

        xh, bcs, ccs, sts = _ssd_io(mx, x_ref, hp_ref)
        fn = functools.partial(_ssd_chunk, rev=rev, heads=mx.heads, col0=col0)
        _, vjp = jax.vjp(fn, xh, bcs, ccs, dt_ref[...], dtt_ref[...], ar_ref[...], ac_ref[...], sts)
        dys = [dy_ref[:, 128 * p:128 * (p + 1)] for p in range(np_)]
        dsts = [ds_ref[128 * p:128 * (p + 1), :] for p in range(np_)]
        dxh, dbc, dcc, ddc, ddr, dar, dac, dst = vjp((dys, dsts))
        for p in range(np_):
            v = dxh[p]
            if with_skip:
                v = v + dys[p] * sk_ref[:, 128 * p:128 * (p + 1)]
            dx_ref[:, 128 * p:128 * (p + 1)] = v
            ds_ref[128 * p:128 * (p + 1), :] = dst[p]
        for g in range(SSD_GROUPS):
            dx_ref[:, d + N_STATE * g:d + N_STATE * (g + 1)] = dbc[g]
            dx_ref[:, d + mx.gn + N_STATE * g:d + mx.gn + N_STATE * (g + 1)] = dcc[g]
        ddc_ref[...] = ddc
        ddr_ref[...] = ddr
        first = jnp.logical_and(b == 0, i == 0)
        _acc(dar_ref, first, dar)
        _acc(dac_ref, first, dac)

    ch = lambda b, i: sc.chunk(b, sc.nct - 1 - i)
    in_specs = [pl.BlockSpec((CHUNK, mx.xw), lambda b, i: (ch(b, i), 0)),
                pl.BlockSpec((CHUNK, DT_LANES), lambda b, i: (ch(b, i), 0)),
                pl.BlockSpec((DT_LANES, CHUNK), lambda b, i: (0, ch(b, i))),
                pl.BlockSpec((1, DT_LANES), lambda b, i: (0, 0)),
                pl.BlockSpec((DT_LANES, 1), lambda b, i: (0, 0)),
                pl.BlockSpec((hp, N_STATE), lambda b, i: (b * sc.nct + sc.nct - 1 - i, 0)),
                pl.BlockSpec((CHUNK, d), lambda b, i: (ch(b, i), 0))]
    inputs = [xbc, dt, dtt, a_row, a_col, hprev, dy]
    if with_skip:
        in_specs.append(pl.BlockSpec((1, d), lambda b, i: (0, 0)))
        inputs.append(dskip)
    return pl.pallas_call(
        body, grid=(lay.bl, sc.nct), in_specs=in_specs,
        out_specs=[pl.BlockSpec((CHUNK, mx.xw), lambda b, i: (ch(b, i), 0)),
                   pl.BlockSpec((CHUNK, DT_LANES), lambda b, i: (ch(b, i), 0)),
                   pl.BlockSpec((DT_LANES, CHUNK), lambda b, i: (0, ch(b, i))),
                   pl.BlockSpec((1, DT_LANES), lambda b, i: (0, 0)),
                   pl.BlockSpec((DT_LANES, 1), lambda b, i: (0, 0))],
        out_shape=[jax.ShapeDtypeStruct((lay.ta, mx.xw), F32), jax.ShapeDtypeStruct((lay.ta, DT_LANES), F32),
                   jax.ShapeDtypeStruct((DT_LANES, lay.ta), F32), jax.ShapeDtypeStruct((1, DT_LANES), F32),
                   jax.ShapeDtypeStruct((DT_LANES, 1), F32)],
        scratch_shapes=[pltpu.VMEM((hp, N_STATE), F32)], name=name, compiler_params=_cparams(2),
    )(*inputs)


def _gate_f(yf, yb, xh, z, drow, nw):
    dd = yf.shape[-1]
    half = dd // SSD_GROUPS
    yz = (yf + yb + drow * xh) * _silu(z)
    lo = lax.broadcasted_iota(jnp.int32, yz.shape, 1) < half
    sq = yz * yz
    ms1 = jnp.sum(jnp.where(lo, sq, 0.0), axis=-1, keepdims=True) / half
    ms2 = jnp.sum(jnp.where(lo, 0.0, sq), axis=-1, keepdims=True) / half
    return yz * jnp.where(lo, lax.rsqrt(ms1 + EPS), lax.rsqrt(ms2 + EPS)) * nw


def _gate_fwd(lay, mx, yf, yb, xbc, proj, drow, nw, name):
    d = mx.d_ssm

    def body(yf_ref, yb_ref, xh_ref, z_ref, d_ref, w_ref, o_ref):
        o_ref[...] = _gate_f(yf_ref[...], yb_ref[...], xh_ref[...], z_ref[...], d_ref[...], w_ref[...]).astype(o_ref.dtype)

    t = _tok(lay, d)
    return _tok_call(name, body, 1, lay.nsx, [t, t, t, t, _glob(1, d), _glob(1, d)], t,
                     jax.ShapeDtypeStruct((lay.tx, d), BF16), (yf, yb, xbc, proj, drow, nw))


def _gate_bwd(lay, mx, yf, yb, xbc, proj, drow, nw, dcat, name):
    d = mx.d_ssm
    nsx = lay.nsx

    def body(yf_ref, yb_ref, xh_ref, z_ref, d_ref, w_ref, dc_ref, dy_ref, dz_ref, dd_ref, dw_ref):
        s = pl.program_id(1)

        @pl.when(s < nsx)
        def _():
            _, vjp = jax.vjp(_gate_f, yf_ref[...], yb_ref[...], xh_ref[...], z_ref[...], d_ref[...], w_ref[...])
            dyf, _, _, dz, dd, dw = vjp(dc_ref[...])
            dy_ref[...] = dyf
            dz_ref[...] = dz.astype(dz_ref.dtype)
            _acc(dd_ref, s == 0, dd)
            _acc(dw_ref, s == 0, dw)

        @pl.when(s >= nsx)
        def _():
            dy_ref[...] = jnp.zeros_like(dy_ref)
            dz_ref[...] = jnp.zeros_like(dz_ref)

    t = _tok(lay, d)
    return _tok_call(name, body, 1, lay.ns, [t, t, t, t, _glob(1, d), _glob(1, d), _tok(lay, d, clamp=nsx - 1)],
                     [t, t, _glob(1, d), _glob(1, d)],
                     [jax.ShapeDtypeStruct((lay.ta, d), F32), jax.ShapeDtypeStruct((lay.ta, d), BF16),
                      jax.ShapeDtypeStruct((1, d), F32), jax.ShapeDtypeStruct((1, d), F32)],
                     (yf, yb, xbc, proj, drow, nw, dcat))


def _glu_fwd(lay, mx, proj, name):
    d = mx.d_conv
    c = math.gcd(mx.off_glu, d)
    cb = mx.off_glu // c

    def body(a_ref, b_ref, o_ref):
        o_ref[...] = a_ref[...] * jax.nn.sigmoid(b_ref[...])

    return _tok_call(name, body, d // c, lay.nsx, [_tok(lay, c, cb), _tok(lay, c, cb + d // c)], _tok(lay, c),
                     jax.ShapeDtypeStruct((lay.tx, d), F32), (proj, proj))


def _glu_bwd(lay, mx, proj, du, name):
    d = mx.d_conv
    c = math.gcd(mx.off_glu, d)
    cb = mx.off_glu // c
    nsx = lay.nsx

    def body(a_ref, b_ref, du_ref, da_ref, db_ref):
        s = pl.program_id(1)

        @pl.when(s < nsx)
        def _():
            sg = jax.nn.sigmoid(b_ref[...])
            duv = du_ref[...]
            da_ref[...] = (duv * sg).astype(da_ref.dtype)
            db_ref[...] = (duv * a_ref[...] * sg * (1.0 - sg)).astype(db_ref.dtype)

        @pl.when(s >= nsx)
        def _():
            da_ref[...] = jnp.zeros_like(da_ref)
            db_ref[...] = jnp.zeros_like(db_ref)

    sh = jax.ShapeDtypeStruct((lay.ta, d), BF16)
    return _tok_call(name, body, d // c, lay.ns,
                     [_tok(lay, c, cb), _tok(lay, c, cb + d // c), _tok(lay, c, clamp=nsx - 1)],
                     [_tok(lay, c), _tok(lay, c)], [sh, sh], (proj, proj, du))


def _axial(lay, mx, u, dy, cw, cb, name):
    d, seq = mx.d_conv, lay.seq
    kw = cw.shape[0]
    pad = kw // 2
    c = _pick(d // 2, (256, 128))
    ncb = d // c
    zpad = GRID_W * pad
    zpad = -(-zpad // 8) * 8
    backward = dy is not None

    def shifted(ext_ref, off):
        return ext_ref[pl.ds(zpad + off, seq), :]

    def valid_row(off):
        col = lax.broadcasted_iota(jnp.int32, (seq, c), 0) % GRID_W
        return jnp.logical_and(col + off >= 0, col + off < GRID_W)

    def fill(ext_ref, v):
        ext_ref[0:zpad, :] = jnp.zeros((zpad, c), F32)
        ext_ref[zpad:zpad + seq, :] = v
        ext_ref[zpad + seq:, :] = jnp.zeros((zpad, c), F32)

    def conv(ext_ref, w_ref, is_row, sign):
        acc = jnp.zeros((seq, c), F32)
        for k in range(kw):
            off = sign * ((k - pad) if is_row else GRID_W * (k - pad))
            v = shifted(ext_ref, off)
            if is_row:
                v = jnp.where(valid_row(off), v, 0.0)
            acc = acc + w_ref[k:k + 1, :] * v
        return acc

    def fwd_body(u_ref, w_ref, b_ref, o_ref, ext_ref):
        j = pl.program_id(0)
        fill(ext_ref, u_ref[...])

        @pl.when(j < ncb // 2)
        def _():
            o_ref[...] = conv(ext_ref, w_ref, True, 1) + b_ref[...]

        @pl.when(j >= ncb // 2)
        def _():
            o_ref[...] = conv(ext_ref, w_ref, False, 1) + b_ref[...]

    def bwd_body(u_ref, dy_ref, w_ref, du_ref, dw_ref, db_ref, extu_ref, extd_ref):
        j, b = pl.program_id(0), pl.program_id(1)
        dyv = dy_ref[...]
        fill(extu_ref, u_ref[...])
        fill(extd_ref, dyv)

        def grads(is_row):
            du_ref[...] = conv(extd_ref, w_ref, is_row, -1)
            rows = []
            for k in range(kw):
                off = (k - pad) if is_row else GRID_W * (k - pad)
                v = shifted(extu_ref, off)
                if is_row:
                    v = jnp.where(valid_row(off), v, 0.0)
                rows.append(jnp.sum(dyv * v, axis=0, keepdims=True))
            _acc(dw_ref, b == 0, jnp.concatenate(rows, axis=0))

        @pl.when(j < ncb // 2)
        def _():
            grads(True)

        @pl.when(j >= ncb // 2)
        def _():
            grads(False)

        _acc(db_ref, b == 0, jnp.sum(dyv, axis=0, keepdims=True))

    seq_spec = pl.BlockSpec((seq, c), lambda j, b: (b, j))
    w_spec = pl.BlockSpec((kw, c), lambda j, b: (0, j))
    b_spec = pl.BlockSpec((1, c), lambda j, b: (0, j))
    ext = pltpu.VMEM((seq + 2 * zpad, c), F32)
    if not backward:
        return pl.pallas_call(fwd_body, grid=(ncb, lay.bl), in_specs=[seq_spec, w_spec, b_spec], out_specs=seq_spec,
                              out_shape=jax.ShapeDtypeStruct((lay.tx, d), F32), scratch_shapes=[ext], name=name,
                              compiler_params=_cparams(2))(u, cw, cb)
    return pl.pallas_call(bwd_body, grid=(ncb, lay.bl), in_specs=[seq_spec, seq_spec, w_spec],
                          out_specs=[seq_spec, w_spec, b_spec],
                          out_shape=[jax.ShapeDtypeStruct((lay.tx, d), F32), jax.ShapeDtypeStruct((kw, d), F32),
                                     jax.ShapeDtypeStruct((1, d), F32)],
                          scratch_shapes=[ext, ext], name=name, compiler_params=_cparams(2))(u, dy, cw)


def _ln_silu_f(u, w, b):
    mu = jnp.mean(u, axis=-1, keepdims=True)
    var = jnp.mean(jnp.square(u - mu), axis=-1, keepdims=True)
    return _silu((u - mu) * lax.rsqrt(var + EPS) * w + b)


def _ln_fwd(lay, mx, u, w, b, name):
    d = mx.d_conv

    def body(u_ref, w_ref, b_ref, o_ref):
        o_ref[...] = _ln_silu_f(u_ref[...], w_ref[...], b_ref[...]).astype(o_ref.dtype)

    return _tok_call(name, body, 1, lay.nsx, [_tok(lay, d), _glob(1, d), _glob(1, d)], _tok(lay, d),
                     jax.ShapeDtypeStruct((lay.tx, d), BF16), (u, w, b))


def _ln_bwd(lay, mx, u, w, b, dcat, name):
    d = mx.d_conv

    def body(u_ref, w_ref, b_ref, dc_ref, du_ref, dw_ref, db_ref):
        s = pl.program_id(1)
        _, vjp = jax.vjp(_ln_silu_f, u_ref[...], w_ref[...], b_ref[...])
        du, dw, db = vjp(dc_ref[...])
        du_ref[...] = du
        _acc(dw_ref, s == 0, dw)
        _acc(db_ref, s == 0, db)

    return _tok_call(name, body, 1, lay.nsx, [_tok(lay, d), _glob(1, d), _glob(1, d), _tok(lay, d, 1)],
                     [_tok(lay, d), _glob(1, d), _glob(1, d)],
                     [jax.ShapeDtypeStruct((lay.tx, d), F32), jax.ShapeDtypeStruct((1, d), F32),
                      jax.ShapeDtypeStruct((1, d), F32)], (u, w, b, dcat))


def _ffn_fwd(lay, nseg, x, nw, modv, k0, wts, tag):
    wg, wu, wd = wts
    h = _norm_mod_fwd(lay, nseg, x, nw, modv, k0, tag + "_norm")
    g, u, a = _ffn_gu(h, wg, wu, tag + "_gu")
    o = _ffn_contract([a], [wd], "nn", tag + "_down")
    y = _resid_fwd(lay, nseg, x, o, modv, k0 + 2, 0.5, tag + "_res")
    return y, (x, h, g, u, a, o)


def _ffn_bwd(lay, nseg, dy, saved, nw, modv, k0, wts, tag):
    wg, wu, wd = wts
    x, h, g, u, a, o = saved
    do, dgate = _resid_bwd(lay, nseg, dy, o, modv, k0 + 2, 0.5, tag + "_dres")
    dg, du = _ffn_da(do, wd, g, u, tag + "_da")
    dwd = _ffn_dwd(a, do, tag + "_dwd")
    dh = _ffn_contract([dg, du], [wg, wu], "nt", tag + "_dh")
    dwg, dwu = _ffn_dw(h, [dg, du], tag + "_dwgu")
    dx, dnw, dss = _norm_mod_bwd(lay, nseg, nseg, x, nw, modv, k0, dh, dy, tag + "_dnorm")
    return dx, (dwg, dwu, dwd), dnw, jnp.concatenate([dss, dgate], axis=1)


def _local_step(lay, mx, xa, target, modv, w):
    d, bl = lay.d, lay.bl
    g = {}
    xa1, ffn1 = _ffn_fwd(lay, lay.ns, xa, w["norm_ffn1"], modv, 0, w["ffn1"], "ffn1")
    ha = _norm_mod_fwd(lay, lay.ns, xa1, w["norm_mix"], modv, 3, "mix_norm")
    proj = _mm(ha, w["w_in"], "nn", F32, "mix_in")
    pre, xbc = _conv5_fwd(lay.fine(), mx, proj, w["conv_w"], w["conv_b"], "mix_conv")
    dt = _dt_fwd(lay, mx, proj, w["dt_bias"], "mix_dt")
    dtt = dt.T
    yf, hpf = _ssd_fwd(lay, mx, xbc, dt, dtt, w["a_row"], w["a_col"], False, "ssd_f")
    yb, hpb = _ssd_fwd(lay, mx, xbc, dt, dtt, w["a_row"], w["a_col"], True, "ssd_b")
    cat_y = _gate_fwd(lay, mx, yf, yb, xbc, proj, w["d_row"], w["ssm_norm_w"], "mix_gate")
    u0 = _glu_fwd(lay, mx, proj, "mix_glu")
    uc = _axial(lay, mx, u0, None, w["cconv_w"], w["cconv_b"], "mix_axial")
    cat_u = _ln_fwd(lay, mx, uc, w["ln_w"], w["ln_b"], "mix_ln")
    cat = jnp.concatenate([cat_y, cat_u], axis=1)
    mix = _mm(cat, w["w_out"], "nn", F32, "mix_out")
    x2 = _resid_fwd(lay, lay.nsx, xa1, mix, modv, 5, 1.0, "mix_res")
    x3, ffn2 = _ffn_fwd(lay, lay.nsx, x2, w["norm_ffn2"], modv, 6, w["ffn2"], "ffn2")
    loss, dx3, g["final_norm"] = _final_loss(lay, x3, w["final_norm"], target, "loss")
    dx2, g["ffn2"], g["norm_ffn2"], dmod2 = _ffn_bwd(lay, lay.nsx, dx3, ffn2, w["norm_ffn2"], modv, 6, w["ffn2"], "ffn2")
    dmix, dg2 = _resid_bwd(lay, lay.nsx, dx2, mix, modv, 5, 1.0, "mix_dres")
    dcat = _mm(dmix, w["w_out"], "nt", F32, "mix_dcat")
    g["w_out"] = _mm(cat, dmix, "tn", F32, "mix_dwout")
    duc, g["ln_w"], g["ln_b"] = _ln_bwd(lay, mx, uc, w["ln_w"], w["ln_b"], dcat, "mix_dln")
    du0, g["cconv_w"], g["cconv_b"] = _axial(lay, mx, u0, duc, w["cconv_w"], None, "mix_daxial")
    dglu_a, dglu_b = _glu_bwd(lay, mx, proj, du0, "mix_dglu")
    dyssd, dz, g["d_row"], g["ssm_norm_w"] = _gate_bwd(lay, mx, yf, yb, xbc, proj, w["d_row"], w["ssm_norm_w"], dcat,
                                                       "mix_dgate")
    dxf, ddcf, ddrf, darf, dacf = _ssd_bwd(lay, mx, xbc, dt, dtt, w["a_row"], w["a_col"], hpf, dyssd, w["d_row"],
                                           False, "ssd_df")
    dxb, ddcb, ddrb, darb, dacb = _ssd_bwd(lay, mx, xbc, dt, dtt, w["a_row"], w["a_col"], hpb, dyssd, None,
                                           True, "ssd_db")
    g["a_row"] = (darf + darb) + (dacf + dacb).T
    dxbc, g["conv_w"], g["conv_b"] = _conv5_bwd(lay.fine(), mx, proj, pre, dxf, dxb, w["conv_w"], "mix_dconv")
    ddtraw, g["dt_bias"] = _dt_bwd(lay, mx, proj, w["dt_bias"], (ddcf, ddcb, ddrf.T, ddrb.T), "mix_ddt")
    dproj = jnp.concatenate([dz, dxbc, ddtraw, dglu_a, dglu_b], axis=1)
    dha = _mm(dproj, w["w_in"], "nt", F32, "mix_dha")
    g["w_in"] = _mm(ha, dproj, "tn", F32, "mix_dwin")
    dxa1, g["norm_mix"], dss_mix = _norm_mod_bwd(lay, lay.ns, lay.nsx, xa1, w["norm_mix"], modv, 3, dha, dx2, "mix_dnorm")
    dxa, g["ffn1"], g["norm_ffn1"], dmod1 = _ffn_bwd(lay, lay.ns, dxa1, ffn1, w["norm_ffn1"], modv, 0, w["ffn1"], "ffn1")
    zrow = lambda t: jnp.concatenate([t, jnp.zeros((1,) + t.shape[1:], F32)], axis=0)
    dmodv = jnp.concatenate([dmod1, dss_mix, zrow(dg2), zrow(dmod2)], axis=1)
    return loss, dxa[:lay.tx], g, dmodv


def _all_gather(xs, name, in_hbm):
    na = len(xs)

    def body(*refs):
        x_refs, out_refs = refs[:na], refs[na:2 * na]
        send_sems, recv_sems, local_sems = refs[2 * na:]
        mx_, my_, mc_ = lax.axis_index("x"), lax.axis_index("y"), lax.axis_index("c")
        me, sibling = (mx_, my_, mc_), (mx_, my_, 1 - mc_)
        chips = [(1 - mx_, my_), (mx_, 1 - my_), (1 - mx_, 1 - my_)]

        def slot(a, px, py, pc):
            return out_refs[a].at[4 * px + 2 * py + pc]

        def copy(a, k, block, to, own=False):
            return pltpu.make_async_remote_copy(
                src_ref=x_refs[a] if own else slot(a, *block), dst_ref=slot(a, *block),
                send_sem=send_sems.at[7 * a + k], recv_sem=recv_sems.at[7 * a + k], device_id=to, device_id_type=MESH)

        mine = [pltpu.make_async_copy(x_refs[a], slot(a, *me), local_sems.at[a]) for a in range(na)]
        for cp in mine:
            cp.start()
        first = []
        for a in range(na):
            first.append(copy(a, 0, me, sibling, own=True))
            first += [copy(a, 1 + j, me, (*chip, mc_), own=True) for j, chip in enumerate(chips)]
        for cp in first:
            cp.start()
        passed = []
        for j, chip in enumerate(chips):
            for a in range(na):
                copy(a, 1 + j, (*chip, mc_), me).wait_recv()
                fwd = copy(a, 4 + j, (*chip, mc_), sibling)
                fwd.start()
                passed.append(fwd)
        for a in range(na):
            copy(a, 0, sibling, me).wait_recv()
            for j, chip in enumerate(chips):
                copy(a, 4 + j, (*chip, 1 - mc_), me).wait_recv()
        for cp in first + passed:
            cp.wait_send()
        for cp in mine:
            cp.wait()

    space = pl.ANY if in_hbm else pltpu.VMEM
    return pl.pallas_call(
        body, out_shape=[jax.ShapeDtypeStruct((N_DEV,) + tuple(x.shape), x.dtype) for x in xs],
        in_specs=[pl.BlockSpec(memory_space=space)] * na, out_specs=[pl.BlockSpec(memory_space=space)] * na,
        scratch_shapes=[pltpu.SemaphoreType.DMA((7 * na,)), pltpu.SemaphoreType.DMA((7 * na,)),
                        pltpu.SemaphoreType.DMA((na,))],
        name=name,
    )(*xs)


N_CHIPS = 4


def _swap_sibling(gs, name):
    na = len(gs)

    def body(*refs):
        g_refs, out_refs, send_sems, recv_sems = refs[:na], refs[na:2 * na], refs[2 * na], refs[2 * na + 1]
        mx_, my_, mc_ = lax.axis_index("x"), lax.axis_index("y"), lax.axis_index("c")
        copies = [pltpu.make_async_remote_copy(
            src_ref=g_refs[a].at[k, 1 - mc_], dst_ref=out_refs[a].at[k], send_sem=send_sems.at[N_CHIPS * a + k],
            recv_sem=recv_sems.at[N_CHIPS * a + k], device_id=(mx_, my_, 1 - mc_), device_id_type=MESH)
            for a in range(na) for k in range(N_CHIPS)]
        for cp in copies:
            cp.start()
        for cp in copies:
            cp.wait_recv()
        for cp in copies:
            cp.wait_send()

    return pl.pallas_call(
        body, out_shape=[jax.ShapeDtypeStruct((N_CHIPS,) + tuple(g.shape[2:]), g.dtype) for g in gs],
        in_specs=[pl.BlockSpec(memory_space=pl.ANY)] * na, out_specs=[pl.BlockSpec(memory_space=pl.ANY)] * na,
        scratch_shapes=[pltpu.SemaphoreType.DMA((N_CHIPS * na,)), pltpu.SemaphoreType.DMA((N_CHIPS * na,))], name=name,
    )(*gs)


def _row_tile(r, n):
    if r * n * 4 <= (1 << 20):
        return r
    for t in (1024, 512, 256, 128, 64, 32, 16, 8):
        if r % t == 0 and t * n * 4 <= (1 << 20):
            return t
    return r


def _pair_add(place, g, got, name):
    _, _, r, n = g.shape
    tr = _row_tile(r, n)

    def body(place_ref, g_ref, got_ref, o_ref, ob_ref):
        s = g_ref[...] + got_ref[...]
        o_ref[...] = s
        ob_ref[...] = s.astype(ob_ref.dtype)

    blk = pl.BlockSpec((None, tr, n), lambda k, i, pr: (k, i, 0))
    grid_spec = pltpu.PrefetchScalarGridSpec(
        num_scalar_prefetch=1, grid=(N_CHIPS, r // tr),
        in_specs=[pl.BlockSpec((None, None, tr, n), lambda k, i, pr: (k, pr[0], i, 0)), blk], out_specs=[blk, blk])
    return pl.pallas_call(body, grid_spec=grid_spec,
                          out_shape=[jax.ShapeDtypeStruct((N_CHIPS, r, n), F32), jax.ShapeDtypeStruct((N_CHIPS, r, n), BF16)],
                          name=name, compiler_params=_cparams(2))(place, g, got)


def _swap_chips(ps, name):
    na = len(ps)

    def body(*refs):
        p_refs, out_refs, send_sems, recv_sems = refs[:na], refs[na:2 * na], refs[2 * na], refs[2 * na + 1]
        mx_, my_, mc_ = lax.axis_index("x"), lax.axis_index("y"), lax.axis_index("c")
        chips = [(1 - mx_, my_), (mx_, 1 - my_), (1 - mx_, 1 - my_)]
        copies = [pltpu.make_async_remote_copy(
            src_ref=p_refs[a].at[2 * cx + cy], dst_ref=out_refs[a].at[j], send_sem=send_sems.at[N_CHIP_PEERS * a + j],
            recv_sem=recv_sems.at[N_CHIP_PEERS * a + j], device_id=(cx, cy, mc_), device_id_type=MESH)
            for a in range(na) for j, (cx, cy) in enumerate(chips)]
        for cp in copies:
            cp.start()
        for cp in copies:
            cp.wait_recv()
        for cp in copies:
            cp.wait_send()

    return pl.pallas_call(
        body, out_shape=[jax.ShapeDtypeStruct((N_CHIP_PEERS,) + tuple(p.shape[1:]), p.dtype) for p in ps],
        in_specs=[pl.BlockSpec(memory_space=pl.ANY)] * na, out_specs=[pl.BlockSpec(memory_space=pl.ANY)] * na,
        scratch_shapes=[pltpu.SemaphoreType.DMA((N_CHIP_PEERS * na,)), pltpu.SemaphoreType.DMA((N_CHIP_PEERS * na,))],
        name=name,
    )(*ps)


def _sum_lead(x, name):
    k, r, n = x.shape
    tr = _row_tile(r, n * k)

    def body(x_ref, o_ref):
        acc = x_ref[0]
        for i in range(1, k):
            acc = acc + x_ref[i]
        o_ref[...] = acc

    return pl.pallas_call(body, grid=(r // tr,), in_specs=[pl.BlockSpec((k, tr, n), lambda i: (0, i, 0))],
                          out_specs=pl.BlockSpec((tr, n), lambda i: (i, 0)),
                          out_shape=jax.ShapeDtypeStruct((r, n), x.dtype), name=name, compiler_params=_cparams(1))(x)


def _adamw(place, w, parts, m, v, name):
    shape = w.shape
    cols = shape[-1]
    rows = math.prod(shape[:-1])
    to2 = lambda t: t.reshape(rows, cols)
    tr = _row_tile(rows, cols) if rows * cols * 4 > (1 << 20) else rows
    npart = len(parts)
    spec = pl.BlockSpec((tr, cols), lambda i, pr: (i, 0))
    part_specs, part_args = [], []
    for piece in parts:
        if isinstance(piece, tuple):
            stack, k = piece
            part_args.append(stack.reshape(stack.shape[0], rows, cols))
            if k == "chip":
                part_specs.append(pl.BlockSpec((None, tr, cols), lambda i, pr: (pr[1], i, 0)))
            else:
                part_specs.append(pl.BlockSpec((None, tr, cols), functools.partial(lambda i, pr, kk: (kk, i, 0), kk=k)))
        else:
            part_args.append(to2(piece))
            part_specs.append(spec)

    def body(place_ref, *refs):
        w_ref, m_ref, v_ref = refs[0], refs[1 + npart], refs[2 + npart]
        g_ref, d_ref, nm_ref, nv_ref = refs[3 + npart:]
        g = refs[1][...].astype(F32)
        for q in range(1, npart):
            g = g + refs[1 + q][...].astype(F32)
        mm = ADAM_B1 * m_ref[...] + (1.0 - ADAM_B1) * g
        vv = ADAM_B2 * v_ref[...] + (1.0 - ADAM_B2) * jnp.square(g)
        m_hat = mm / (1.0 - ADAM_B1 ** ADAM_STEP)
        v_hat = vv / (1.0 - ADAM_B2 ** ADAM_STEP)
        g_ref[...] = g
        d_ref[...] = -ADAM_LR * (m_hat / (jnp.sqrt(v_hat) + ADAM_EPS) + ADAM_WD * w_ref[...])
        nm_ref[...] = mm
        nv_ref[...] = vv

    sh = jax.ShapeDtypeStruct((rows, cols), F32)
    grid_spec = pltpu.PrefetchScalarGridSpec(num_scalar_prefetch=1, grid=(rows // tr,),
                                             in_specs=[spec] + part_specs + [spec, spec], out_specs=[spec] * 4)
    outs = pl.pallas_call(body, grid_spec=grid_spec, out_shape=[sh] * 4, name=name, compiler_params=_cparams(1),
                          )(place, to2(w), *part_args, to2(m), to2(v))
    return tuple(o.reshape(shape) for o in outs)


def _pack_rows(items, width):
    rows = []
    for t in items:
        flat = t.reshape(-1)
        n = flat.shape[0]
        k = -(-n // width)
        if k * width > n:
            flat = jnp.concatenate([flat, jnp.zeros((k * width - n,), t.dtype)])
        rows.append(flat.reshape(k, width))
    out = jnp.concatenate(rows, axis=0)
    pad = -out.shape[0] % 8
    if pad:
        out = jnp.concatenate([out, jnp.zeros((pad, width), out.dtype)], axis=0)
    return out


def _unpack_rows(packed, shapes, lead=()):
    width = packed.shape[-1]
    out, r = [], 0
    for sh in shapes:
        n = math.prod(sh)
        k = -(-n // width)
        piece = packed[..., r:r + k, :].reshape(tuple(lead) + (k * width,))[..., :n]
        out.append(piece.reshape(tuple(lead) + tuple(sh)))
        r += k
    return out


def _cols_full(t):
    return jnp.transpose(t, (1, 0, 2)).reshape(t.shape[1], -1)


def _cols_shards(t):
    d = t.shape[0]
    return jnp.transpose(t.reshape(d, N_DEV, -1), (1, 0, 2))


BIG = ("ffn1_gate", "ffn1_up", "ffn1_down", "w_in", "w_out", "ffn2_gate", "ffn2_up", "ffn2_down")
ROW_SHARDED = ("ffn1_down", "w_out", "ffn2_down")


def kernel(x, c, ctx, c_ctx, w_mod, b_mod, norm_ffn1, ffn1_gate, ffn1_up, ffn1_down, norm_mix, w_in, ssm_conv_w, ssm_conv_b, dt_bias_fwd, dt_bias_bwd, a_log_fwd, a_log_bwd, ssm_d, ssm_norm_w, cconv_w, cconv_b, cconv_ln_w, cconv_ln_b, w_out, norm_ffn2, ffn2_gate, ffn2_up, ffn2_down, final_norm, loss_target, m_c_ctx, m_w_mod, m_b_mod, m_norm_ffn1, m_ffn1_gate, m_ffn1_up, m_ffn1_down, m_norm_mix, m_w_in, m_ssm_conv_w, m_ssm_conv_b, m_dt_bias_fwd, m_dt_bias_bwd, m_a_log_fwd, m_a_log_bwd, m_ssm_d, m_ssm_norm_w, m_cconv_w, m_cconv_b, m_cconv_ln_w, m_cconv_ln_b, m_w_out, m_norm_ffn2, m_ffn2_gate, m_ffn2_up, m_ffn2_down, m_final_norm, v_c_ctx, v_w_mod, v_b_mod, v_norm_ffn1, v_ffn1_gate, v_ffn1_up, v_ffn1_down, v_norm_mix, v_w_in, v_ssm_conv_w, v_ssm_conv_b, v_dt_bias_fwd, v_dt_bias_bwd, v_a_log_fwd, v_a_log_bwd, v_ssm_d, v_ssm_norm_w, v_cconv_w, v_cconv_b, v_cconv_ln_w, v_cconv_ln_b, v_w_out, v_norm_ffn2, v_ffn2_gate, v_ffn2_up, v_ffn2_down, v_final_norm):
    args = dict(locals())
    names = ("c_ctx", "w_mod", "b_mod", "norm_ffn1", "ffn1_gate", "ffn1_up", "ffn1_down", "norm_mix", "w_in",
             "ssm_conv_w", "ssm_conv_b", "dt_bias_fwd", "dt_bias_bwd", "a_log_fwd", "a_log_bwd", "ssm_d", "ssm_norm_w",
             "cconv_w", "cconv_b", "cconv_ln_w", "cconv_ln_b", "w_out", "norm_ffn2", "ffn2_gate", "ffn2_up",
             "ffn2_down", "final_norm")
    wts = {n: args[n] for n in names}
    bl, seq, d = x.shape
    clen = ctx.shape[1]
    heads = dt_bias_fwd.shape[1]
    dff = ffn1_gate.shape[2] * N_DEV
    lay = _Lay(bl, seq, clen, d)
    mx = _Mix(d, heads)
    nb = bl * N_DEV
    me = 4 * lax.axis_index("x") + 2 * lax.axis_index("y") + lax.axis_index("c")
    mcols = w_mod.shape[2]
    n_ctx_mod = 5 * d

    place = jnp.stack([lax.axis_index("c"), 2 * lax.axis_index("x") + lax.axis_index("y")]).astype(jnp.int32)

    small_shapes = [(bl, d), ssm_conv_w.shape[1:], cconv_w.shape[1:]]
    (g1,) = _all_gather([_pack_rows([c, ssm_conv_w, cconv_w], d)], "gather_small", False)
    c_g, conv_g, cconv_g = _unpack_rows(g1, small_shapes, (N_DEV,))
    c_all = c_g.reshape(nb, d)
    conv_w_full = jnp.transpose(conv_g, (1, 0, 2)).reshape(conv_g.shape[1], -1)
    cconv_w_full = jnp.transpose(cconv_g, (1, 0, 2)).reshape(cconv_g.shape[1], -1)

    s_all = jnp.concatenate([_silu(c_all), _silu(c_ctx)[None, :], jnp.zeros((7, d), F32)], axis=0)
    mod_cols = _mm(s_all, w_mod[0], "nn", F32, "mod_cols")
    (g2,) = _all_gather([mod_cols], "gather_mod", False)
    mod_all = _cols_full(g2) + b_mod
    mod_mine = jnp.concatenate([lax.dynamic_slice_in_dim(mod_all, me * bl, bl, axis=0), mod_all[nb:nb + 1]], axis=0)
    modv = mod_mine.reshape(bl + 1, N_MOD, d)

    full = dict(zip(BIG, _all_gather([wts[n][0].astype(BF16) for n in BIG], "gather_weights", True)))
    hh = 2 * heads
    w_in_f = _cols_full(full["w_in"])
    w_in_p = jnp.concatenate([w_in_f[:, :mx.off_dt_ref + hh], jnp.zeros((d, DT_PAD - hh), BF16),
                              w_in_f[:, mx.off_glu_ref:]], axis=1)
    lanes_pad = lambda a, b: jnp.concatenate([a, b, jnp.zeros((1, DT_LANES - hh), F32)], axis=1)
    a_vals = lanes_pad(-jnp.exp(a_log_fwd), -jnp.exp(a_log_bwd))
    w = {
        "norm_ffn1": norm_ffn1, "norm_mix": norm_mix, "norm_ffn2": norm_ffn2, "final_norm": final_norm[None, :],
        "ffn1": (full["ffn1_gate"], full["ffn1_up"], full["ffn1_down"]),
        "ffn2": (full["ffn2_gate"], full["ffn2_up"], full["ffn2_down"]),
        "w_in": w_in_p, "w_out": full["w_out"].reshape(-1, d),
        "conv_w": jnp.concatenate([conv_w_full, jnp.zeros((3, mx.xw), F32)], axis=0), "conv_b": ssm_conv_b,
        "dt_bias": lanes_pad(dt_bias_fwd, dt_bias_bwd), "a_row": a_vals, "a_col": a_vals.T,
        "d_row": jnp.repeat(ssm_d, HEAD_DIM, axis=1), "ssm_norm_w": ssm_norm_w,
        "cconv_w": cconv_w_full, "cconv_b": cconv_b, "ln_w": cconv_ln_w, "ln_b": cconv_ln_b,
    }

    xa = jnp.concatenate([x.reshape(bl * seq, d), ctx.reshape(bl * clen, d)], axis=0)
    loss, grad_x, g, dmodv = _local_step(lay, mx, xa, loss_target.reshape(bl * seq, d), modv, w)
    loss = lax.psum(loss[0, 0], ("x", "y", "c"))

    dw_in = jnp.concatenate([g["w_in"][:, :mx.off_dt_ref + hh], g["w_in"][:, mx.off_glu:]], axis=1)
    gbig = dict(zip(("ffn1_gate", "ffn1_up", "ffn1_down"), g["ffn1"]))
    gbig.update(zip(("ffn2_gate", "ffn2_up", "ffn2_down"), g["ffn2"]))
    gbig["w_in"] = _cols_shards(dw_in)
    gbig["w_out"] = g["w_out"].reshape((N_DEV,) + tuple(w_out.shape[1:]))
    by_dest = [gbig[n].reshape((N_CHIPS, 2) + tuple(gbig[n].shape[1:])) for n in BIG]
    got = _swap_sibling(by_dest, "rs_sibling")
    pair = [_pair_add(place, t, s, "rs_pair_add_" + n) for n, t, s in zip(BIG, by_dest, got)]
    chip_sum = [p[0] for p in pair]
    from_chips = _swap_chips([p[1] for p in pair], "rs_chips")

    n9 = N_MOD * d
    dmod_rows = dmodv.reshape(bl + 1, n9)
    ctx_row = jnp.concatenate([dmod_rows[bl, :n_ctx_mod], jnp.zeros((n9 - n_ctx_mod,), F32)])
    summed = [ctx_row, g["norm_ffn1"], g["norm_mix"], g["norm_ffn2"], g["final_norm"], g["conv_b"], g["dt_bias"],
              g["a_row"], g["d_row"], g["ssm_norm_w"], g["cconv_b"], g["ln_w"], g["ln_b"], g["conv_w"][:5], g["cconv_w"]]
    sum_shapes = [t.shape for t in summed]
    (g4,) = _all_gather([_pack_rows([dmod_rows[:bl]] + summed, d)], "gather_small_grads", False)
    nrow_batch = bl * N_MOD
    dmod_batch = g4[:, :nrow_batch].reshape(nb, n9)
    tot = _sum_lead(g4[:, nrow_batch:], "sum_small_grads")
    (dctx, g_n1, g_nm, g_n2, g_fn, g_cb, g_dtb, g_a, g_drow, g_snw, g_ccb, g_lnw, g_lnb, g_cw, g_ccw) = _unpack_rows(tot, sum_shapes)
    dmod_all = jnp.concatenate([dmod_batch, dctx[None, :], jnp.zeros((7, n9), F32)], axis=0)

    dmod_my_cols = lax.dynamic_slice_in_dim(dmod_all, me * mcols, mcols, axis=1)
    g_w_mod = _mm(s_all, dmod_my_cols, "tn", F32, "dw_mod")[None]
    g_b_mod = _sum_lead(dmod_all.reshape(nb + 8, N_MOD, d), "db_mod").reshape(1, n9)
    ds_part = _mm(dmod_my_cols[nb:nb + 8], w_mod[0], "nt", F32, "ds_ctx")
    (g5,) = _all_gather([jnp.concatenate([ds_part[0:1], jnp.zeros((7, d), F32)], axis=0)], "gather_ds_ctx", False)
    ds_ctx = _sum_lead(g5, "sum_ds_ctx")[0]
    sg = jax.nn.sigmoid(c_ctx)
    g_c_ctx = ds_ctx * (sg * (1.0 + c_ctx * (1.0 - sg)))

    a_f, a_b = a_vals[:, :heads], a_vals[:, heads:hh]
    grads = {
        "c_ctx": [g_c_ctx], "w_mod": [g_w_mod], "b_mod": [g_b_mod],
        "norm_ffn1": [g_n1], "norm_mix": [g_nm], "norm_ffn2": [g_n2], "final_norm": [g_fn.reshape(-1)],
        "ssm_conv_w": [lax.dynamic_slice_in_dim(g_cw, me * ssm_conv_w.shape[2], ssm_conv_w.shape[2], axis=1)[None]],
        "ssm_conv_b": [g_cb],
        "dt_bias_fwd": [g_dtb[:, :heads]], "dt_bias_bwd": [g_dtb[:, heads:hh]],
        "a_log_fwd": [g_a[:, :heads] * a_f], "a_log_bwd": [g_a[:, heads:hh] * a_b],
        "ssm_d": [jnp.sum(g_drow.reshape(1, heads, HEAD_DIM), axis=2)], "ssm_norm_w": [g_snw],
        "cconv_w": [lax.dynamic_slice_in_dim(g_ccw, me * cconv_w.shape[2], cconv_w.shape[2], axis=1)[None]],
        "cconv_b": [g_ccb], "cconv_ln_w": [g_lnw], "cconv_ln_b": [g_lnb],
    }
    for n, own, others in zip(BIG, chip_sum, from_chips):
        grads[n] = [(own, "chip"), (others, 0), (others, 1), (others, 2)]

    out_g, out_d, out_m, out_v = [], [], [], []
    for n in names:
        gr, de, nm, nv = _adamw(place, wts[n], grads[n], args["m_" + n], args["v_" + n], "adamw_" + n)
        out_g.append(gr)
        out_d.append(de)
        out_m.append(nm)
        out_v.append(nv)
    return (loss, grad_x.reshape(bl, seq, d), *out_g, *out_d, *out_m, *out_v)
```

```python
import functools
import math

import jax
import jax.numpy as jnp
from jax import lax
from jax.experimental import pallas as pl
from jax.experimental.pallas import tpu as pltpu

F32 = jnp.float32
BF16 = jnp.bfloat16
MESH = pl.DeviceIdType.MESH

N_DEV = 8
N_CHIP_PEERS = 3
HEAD_DIM = 64
N_STATE = 128
SSD_GROUPS = 2
CHUNK = 128
GRID_W = 64
N_MOD = 9
EPS = 1e-6
DT_PAD = 512
DT_LANES = 128
HALO = 8
ROW_TILE = 512
FINE_ROW_TILE = 256
VMEM_LIMIT = 48 * 1024 * 1024
NEG_BIG = -1e30

ADAM_LR = 0.001
ADAM_B1 = 0.9
ADAM_B2 = 0.999
ADAM_EPS = 1e-08
ADAM_WD = 0.01
ADAM_STEP = 10


def _pick(n, prefs):
    for p in prefs:
        if n % p == 0:
            return p
    return n


MM_TILE_CAP = 2816
MM_TILE_ELEMS = 3 << 20
MM_OUT_TILE_ELEMS = 3 << 19
MM_FULL_ROWS = 1024


def _big_tile(n, cap):
    if n <= cap:
        return n
    best = 0
    for t in range(128, cap + 1, 128):
        if n % t == 0:
            best = t
    return best or n


def _cparams(ndim):
    return pltpu.CompilerParams(dimension_semantics=("arbitrary",) * ndim, vmem_limit_bytes=VMEM_LIMIT)


def _silu(v):
    return v * jax.nn.sigmoid(v)


def _mm(a, b, mode, out_dtype, name, tn=None, tm=None, extras=(), epilogue=None, outs=None):
    if mode == "tn":
        (K, M), (K2, N) = a.shape, b.shape
    elif mode == "nt":
        (M, K), (N, K2) = a.shape, b.shape
    else:
        (M, K), (K2, N) = a.shape, b.shape
    assert K == K2, (name, a.shape, b.shape)
    tm = M if M <= MM_FULL_ROWS else tm
    if tn is None:
        tn = _big_tile(N, min(MM_TILE_CAP, max(128, MM_OUT_TILE_ELEMS // (tm or 512))))
    if tm is None:
        tm = _big_tile(M, max(128, MM_OUT_TILE_ELEMS // tn))
    tk = _big_tile(K, min(MM_TILE_CAP, MM_TILE_ELEMS // max(tn, tm)))
    nk = K // tk
    ni, nj = M // tm, N // tn
    swap = nk == 1 and (K * N + M * K * nj) < (M * K + K * N * ni)
    ij = (lambda g0, g1: (g1, g0)) if swap else (lambda g0, g1: (g0, g1))
    if mode == "tn":
        a_spec = pl.BlockSpec((tk, tm), lambda g0, g1, k: (k, ij(g0, g1)[0]))
        dn = (((0,), (0,)), ((), ()))
    else:
        a_spec = pl.BlockSpec((tm, tk), lambda g0, g1, k: (ij(g0, g1)[0], k))
        dn = (((1,), (1,)), ((), ())) if mode == "nt" else (((1,), (0,)), ((), ()))
    if mode == "nt":
        b_spec = pl.BlockSpec((tn, tk), lambda g0, g1, k: (ij(g0, g1)[1], k))
    else:
        b_spec = pl.BlockSpec((tk, tn), lambda g0, g1, k: (k, ij(g0, g1)[1]))
    if outs is None:
        outs = [(tn, out_dtype)]
    nx = len(extras)

    def tile(w):
        return pl.BlockSpec((tm, w), lambda g0, g1, k: ij(g0, g1))

    def finish(acc, refs):
        vals = (acc,) if epilogue is None else epilogue(acc, *[r[...] for r in refs[:nx]])
        for o_ref, v in zip(refs[nx:], vals):
            o_ref[...] = v.astype(o_ref.dtype)

    def body(a_ref, b_ref, *refs):
        part = lax.dot_general(a_ref[...].astype(BF16), b_ref[...].astype(BF16), dn, preferred_element_type=F32)
        if nk == 1:
            finish(part, refs)
            return
        acc_ref, k = refs[-1], pl.program_id(2)
        _acc(acc_ref, k == 0, part)

        @pl.when(k == nk - 1)
        def _():
            finish(acc_ref[...], refs[:-1])

    grid = (nj, ni, nk) if swap else (ni, nj, nk)
    res = pl.pallas_call(
        body, grid=grid, in_specs=[a_spec, b_spec] + [tile(w) for _, w in extras],
        out_specs=[tile(w) for w, _ in outs],
        out_shape=[jax.ShapeDtypeStruct((M, nj * w), dt) for w, dt in outs],
        scratch_shapes=[pltpu.VMEM((tm, tn), F32)] if nk > 1 else [], name=name, compiler_params=_cparams(3),
    )(a, b, *[x for x, _ in extras])
    return res[0] if epilogue is None else res


def _ffn_tm(t):
    return _pick(t, (1024, 512, 256, 128, 64, 32, 16, 8))


def _dsilu_mul(g, u, da):
    sg = jax.nn.sigmoid(g)
    return da * u * (sg * (1.0 + g * (1.0 - sg))), da * (g * sg)


def _ffn_gu(h, wg, wu, name):
    t, d = h.shape
    nb, _, fb = wg.shape
    tm = _ffn_tm(t)

    def body(h_ref, wg_ref, wu_ref, g_ref, u_ref, a_ref):
        hv = h_ref[...]
        g = jnp.dot(hv, wg_ref[...], preferred_element_type=F32)
        u = jnp.dot(hv, wu_ref[...], preferred_element_type=F32)
        g_ref[...] = g
        u_ref[...] = u
        a_ref[...] = (_silu(g) * u).astype(a_ref.dtype)

    wspec = pl.BlockSpec((None, d, fb), lambda i, j: (j, 0, 0))
    ospec = pl.BlockSpec((None, tm, fb), lambda i, j: (j, i, 0))
    sh = jax.ShapeDtypeStruct((nb, t, fb), F32)
    return pl.pallas_call(body, grid=(t // tm, nb), in_specs=[pl.BlockSpec((tm, d), lambda i, j: (i, 0)), wspec, wspec],
                          out_specs=[ospec, ospec, ospec], out_shape=[sh, sh, jax.ShapeDtypeStruct((nb, t, fb), BF16)],
                          name=name, compiler_params=_cparams(2))(h, wg, wu)


def _ffn_contract(xs, ws, mode, name):
    nb, t, fb = xs[0].shape
    d = ws[0].shape[2] if mode == "nn" else ws[0].shape[1]
    tm = _ffn_tm(t)
    npair = len(xs)
    dn = (((1,), (0,)), ((), ())) if mode == "nn" else (((1,), (1,)), ((), ()))

    def body(*refs):
        o_ref, acc_ref = refs[2 * npair], refs[2 * npair + 1]
        j = pl.program_id(1)
        part = lax.dot_general(refs[0][...], refs[npair][...], dn, preferred_element_type=F32)
        for p in range(1, npair):
            part = part + lax.dot_general(refs[p][...], refs[npair + p][...], dn, preferred_element_type=F32)
        _acc(acc_ref, j == 0, part)

        @pl.when(j == nb - 1)
        def _():
            o_ref[...] = acc_ref[...]

    xspec = pl.BlockSpec((None, tm, fb), lambda i, j: (j, i, 0))
    wspec = pl.BlockSpec((None,) + tuple(ws[0].shape[1:]), lambda i, j: (j, 0, 0))
    return pl.pallas_call(body, grid=(t // tm, nb), in_specs=[xspec] * npair + [wspec] * npair,
                          out_specs=pl.BlockSpec((tm, d), lambda i, j: (i, 0)),
                          out_shape=jax.ShapeDtypeStruct((t, d), F32), scratch_shapes=[pltpu.VMEM((tm, d), F32)],
                          name=name, compiler_params=_cparams(2))(*xs, *ws)


def _ffn_da(do, wd, g, u, name):
    t, d = do.shape
    nb, fb, _ = wd.shape
    tm = _ffn_tm(t)

    def body(do_ref, wd_ref, g_ref, u_ref, dg_ref, du_ref):
        da = lax.dot_general(do_ref[...], wd_ref[...], (((1,), (1,)), ((), ())), preferred_element_type=F32)
        dg, du = _dsilu_mul(g_ref[...], u_ref[...], da)
        dg_ref[...] = dg.astype(dg_ref.dtype)
        du_ref[...] = du.astype(du_ref.dtype)

    bspec = pl.BlockSpec((None, tm, fb), lambda i, j: (j, i, 0))
    sh = jax.ShapeDtypeStruct((nb, t, fb), BF16)
    return pl.pallas_call(body, grid=(t // tm, nb),
                          in_specs=[pl.BlockSpec((tm, d), lambda i, j: (i, 0)),
                                    pl.BlockSpec((None, fb, d), lambda i, j: (j, 0, 0)), bspec, bspec],
                          out_specs=[bspec, bspec], out_shape=[sh, sh], name=name, compiler_params=_cparams(2),
                          )(do, wd, g, u)


def _ffn_dw(h, xs, name):
    t, d = h.shape
    nb, _, fb = xs[0].shape
    tk = _ffn_tm(t)
    nk = t // tk
    npair = len(xs)
    tn_dims = (((0,), (0,)), ((), ()))

    def body(*refs):
        h_ref = refs[0]
        k = pl.program_id(1)
        hv = h_ref[...]
        for p in range(npair):
            part = lax.dot_general(hv, refs[1 + p][...], tn_dims, preferred_element_type=F32)
            _acc(refs[1 + npair + p], k == 0, part)

    xspec = pl.BlockSpec((None, tk, fb), lambda j, k: (j, k, 0))
    ospec = pl.BlockSpec((None, d, fb), lambda j, k: (j, 0, 0))
    sh = jax.ShapeDtypeStruct((nb, d, fb), F32)
    return pl.pallas_call(body, grid=(nb, nk), in_specs=[pl.BlockSpec((tk, d), lambda j, k: (k, 0))] + [xspec] * npair,
                          out_specs=[ospec] * npair, out_shape=[sh] * npair, name=name, compiler_params=_cparams(2),
                          )(h, *xs)


def _ffn_dwd(a, do, name):
    nb, t, fb = a.shape
    d = do.shape[1]
    tk = _ffn_tm(t)

    def body(a_ref, do_ref, o_ref):
        part = lax.dot_general(a_ref[...], do_ref[...], (((0,), (0,)), ((), ())), preferred_element_type=F32)
        _acc(o_ref, pl.program_id(1) == 0, part)

    return pl.pallas_call(body, grid=(nb, t // tk),
                          in_specs=[pl.BlockSpec((None, tk, fb), lambda j, k: (j, k, 0)),
                                    pl.BlockSpec((tk, d), lambda j, k: (k, 0))],
                          out_specs=pl.BlockSpec((None, fb, d), lambda j, k: (j, 0, 0)),
                          out_shape=jax.ShapeDtypeStruct((nb, fb, d), F32), name=name, compiler_params=_cparams(2),
                          )(a, do)


class _Lay:
    def __init__(self, bl, seq, clen, d, tt=None):
        self.bl, self.seq, self.clen, self.d = bl, seq, clen, d
        self.tt = min(ROW_TILE, math.gcd(seq, bl * clen)) if tt is None else tt
        assert seq % self.tt == 0 and (bl * clen) % self.tt == 0 and self.tt % 8 == 0
        self.spb = seq // self.tt
        self.spc = clen // self.tt
        self.nsx = bl * self.spb
        self.nsc = bl * clen // self.tt
        self.ns = self.nsx + self.nsc
        self.tx = bl * seq
        self.ta = self.tx + bl * clen

    def fine(self):
        return _Lay(self.bl, self.seq, self.clen, self.d, min(FINE_ROW_TILE, self.clen))

    def mrow(self, s):
        return jnp.where(s < self.nsx, s // self.spb, self.bl)

    def first_of_row(self, s):
        return jnp.logical_or(jnp.logical_and(s < self.nsx, s % self.spb == 0), s == self.nsx)

    def seq_first(self, s):
        return jnp.where(s < self.nsx, s % self.spb == 0, (s - self.nsx) % self.spc == 0)

    def seq_last(self, s):
        return jnp.where(s < self.nsx, s % self.spb == self.spb - 1, (s - self.nsx) % self.spc == self.spc - 1)


def _tok(lay, c, cb=0, clamp=None):
    if clamp is None:
        return pl.BlockSpec((lay.tt, c), lambda j, s: (s, cb + j))
    return pl.BlockSpec((lay.tt, c), lambda j, s: (jnp.minimum(s, clamp), cb + j))


def _halo_prev(lay, c, cb=0):
    u = lay.tt // HALO
    return pl.BlockSpec((HALO, c), lambda j, s: (jnp.maximum(s * u - 1, 0), cb + j))


def _halo_next(lay, c, cb=0):
    u = lay.tt // HALO
    last = lay.ta // HALO - 1
    return pl.BlockSpec((HALO, c), lambda j, s: (jnp.minimum((s + 1) * u, last), cb + j))


def _row(lay, k, c):
    return pl.BlockSpec((None, k, c), lambda j, s: (lay.mrow(s), 0, 0))


def _glob(k, c, cb=None):
    if cb is None:
        return pl.BlockSpec((k, c), lambda j, s: (0, 0))
    return pl.BlockSpec((k, c), lambda j, s: (0, cb + j))


def _tok_call(name, body, ncb, nseg, in_specs, out_specs, out_shape, inputs, scratch=()):
    return pl.pallas_call(body, grid=(ncb, nseg), in_specs=in_specs, out_specs=out_specs, out_shape=out_shape,
                          scratch_shapes=list(scratch), name=name, compiler_params=_cparams(2))(*inputs)


def _acc(ref, first, val):
    @pl.when(first)
    def _():
        ref[...] = val

    @pl.when(jnp.logical_not(first))
    def _():
        ref[...] += val


def _norm_mod_f(x, w, sh, sc):
    y = x * lax.rsqrt(jnp.mean(x * x, axis=-1, keepdims=True) + EPS) * w
    return y * (1.0 + sc) + sh


def _norm_mod_fwd(lay, nseg, x, w, modv, ksh, name):
    d = lay.d

    def body(x_ref, w_ref, m_ref, h_ref):
        h = _norm_mod_f(x_ref[...], w_ref[...], m_ref[ksh:ksh + 1, :], m_ref[ksh + 1:ksh + 2, :])
        h_ref[...] = h.astype(h_ref.dtype)

    return _tok_call(name, body, 1, nseg, [_tok(lay, d), _glob(1, d), _row(lay, N_MOD, d)], _tok(lay, d),
                     jax.ShapeDtypeStruct((nseg * lay.tt, d), BF16), (x, w, modv))


def _norm_mod_bwd(lay, nseg, nres, x, w, modv, ksh, dh, dres, name):
    d = lay.d
    nrow = lay.bl + (1 if nseg > lay.nsx else 0)

    def body(x_ref, w_ref, m_ref, dh_ref, dres_ref, dx_ref, dw_ref, dm_ref):
        s = pl.program_id(1)
        _, vjp = jax.vjp(_norm_mod_f, x_ref[...], w_ref[...], m_ref[ksh:ksh + 1, :], m_ref[ksh + 1:ksh + 2, :])
        dx, dw, dsh, dsc = vjp(dh_ref[...])
        res = jnp.where(s < nres, dres_ref[...], 0.0)
        dx_ref[...] = dx + res
        _acc(dw_ref, s == 0, dw)
        _acc(dm_ref, lay.first_of_row(s), jnp.concatenate([dsh, dsc], axis=0))

    return _tok_call(
        name, body, 1, nseg,
        [_tok(lay, d), _glob(1, d), _row(lay, N_MOD, d), _tok(lay, d), _tok(lay, d, clamp=nres - 1)],
        [_tok(lay, d), _glob(1, d), _row(lay, 2, d)],
        [jax.ShapeDtypeStruct((nseg * lay.tt, d), F32), jax.ShapeDtypeStruct((1, d), F32),
         jax.ShapeDtypeStruct((nrow, 2, d), F32)],
        (x, w, modv, dh, dres))


def _resid_fwd(lay, nseg, x, o, modv, kg, coef, name):
    d = lay.d

    def body(x_ref, o_ref, m_ref, y_ref):
        y_ref[...] = x_ref[...] + (coef * m_ref[kg:kg + 1, :]) * o_ref[...]

    return _tok_call(name, body, 1, nseg, [_tok(lay, d), _tok(lay, d), _row(lay, N_MOD, d)], _tok(lay, d),
                     jax.ShapeDtypeStruct((nseg * lay.tt, d), F32), (x, o, modv))


def _resid_bwd(lay, nseg, dy, o, modv, kg, coef, name):
    d = lay.d
    nrow = lay.bl + (1 if nseg > lay.nsx else 0)

    def body(dy_ref, o_ref, m_ref, do_ref, dg_ref):
        s = pl.program_id(1)
        dy = dy_ref[...]
        do_ref[...] = (dy * (coef * m_ref[kg:kg + 1, :])).astype(do_ref.dtype)
        _acc(dg_ref, lay.first_of_row(s), jnp.sum(dy * o_ref[...], axis=0, keepdims=True) * coef)

    return _tok_call(name, body, 1, nseg, [_tok(lay, d), _tok(lay, d), _row(lay, N_MOD, d)],
                     [_tok(lay, d), _row(lay, 1, d)],
                     [jax.ShapeDtypeStruct((nseg * lay.tt, d), BF16), jax.ShapeDtypeStruct((nrow, 1, d), F32)],
                     (dy, o, modv))


def _final_loss(lay, x, wf, target, name):
    d = lay.d

    def body(x_ref, w_ref, t_ref, loss_ref, dx_ref, dw_ref):
        s = pl.program_id(1)

        def f(xv, wv):
            return xv * lax.rsqrt(jnp.mean(xv * xv, axis=-1, keepdims=True) + EPS) * wv

        y, vjp = jax.vjp(f, x_ref[...], w_ref[...])
        err = y - t_ref[...]
        part = 0.5 * jnp.sum(jnp.sum(err * err, axis=-1, keepdims=True), axis=0, keepdims=True) / d
        dx, dw = vjp(err / d)
        dx_ref[...] = dx
        _acc(loss_ref, s == 0, part)
        _acc(dw_ref, s == 0, dw)

    return _tok_call(name, body, 1, lay.nsx, [_tok(lay, d), _glob(1, d), _tok(lay, d)],
                     [_glob(1, 1), _tok(lay, d), _glob(1, d)],
                     [jax.ShapeDtypeStruct((1, 1), F32), jax.ShapeDtypeStruct((lay.tx, d), F32),
                      jax.ShapeDtypeStruct((1, d), F32)], (x, wf, target))


class _Mix:
    def __init__(self, d, heads):
        self.d_ssm = d
        self.d_conv = d
        self.heads = heads
        assert heads * HEAD_DIM == d and heads % (2 * SSD_GROUPS) == 0 and 2 * heads <= DT_LANES
        self.gn = SSD_GROUPS * N_STATE
        self.xw = d + 2 * self.gn
        self.off_x = d
        self.off_dt = d + self.xw
        self.off_glu = self.off_dt + DT_PAD
        self.pw = self.off_glu + 2 * d
        self.off_dt_ref = d + self.xw
        self.off_glu_ref = self.off_dt_ref + 2 * heads
        self.cc = _pick(self.xw, (512, 256, 128))


def _conv5_fwd(lay, mx, proj, cw, cb, name):
    c, tt = mx.cc, lay.tt
    cb0 = mx.off_x // c
    assert mx.off_x % c == 0

    def body(prev_ref, cur_ref, next_ref, w_ref, b_ref, pre_ref, act_ref, ext_ref):
        s = pl.program_id(1)
        ext_ref[0:HALO, :] = jnp.where(lay.seq_first(s), 0.0, prev_ref[...])
        ext_ref[HALO:HALO + tt, :] = cur_ref[...]
        ext_ref[HALO + tt:, :] = jnp.where(lay.seq_last(s), 0.0, next_ref[...])
        acc = jnp.zeros((tt, c), F32) + b_ref[...]
        for k in range(5):
            acc = acc + w_ref[k:k + 1, :] * ext_ref[pl.ds(HALO + k - 2, tt), :]
        pre_ref[...] = acc
        act_ref[...] = _silu(acc)

    sh = jax.ShapeDtypeStruct((lay.ta, mx.xw), F32)
    return _tok_call(name, body, mx.xw // c, lay.ns,
                     [_halo_prev(lay, c, cb0), _tok(lay, c, cb0), _halo_next(lay, c, cb0), _glob(8, c, 0), _glob(1, c, 0)],
                     [_tok(lay, c), _tok(lay, c)], [sh, sh], (proj, proj, proj, cw, cb),
                     scratch=[pltpu.VMEM((tt + 2 * HALO, c), F32)])


def _conv5_bwd(lay, mx, proj, pre, dact_f, dact_b, cw, name):
    c, tt = mx.cc, lay.tt
    cb0 = mx.off_x // c

    def dsilu(p):
        sg = jax.nn.sigmoid(p)
        return sg * (1.0 + p * (1.0 - sg))

    def body(xp_ref, xc_ref, xn_ref, pp_ref, pc_ref, pn_ref, fp_ref, fc_ref, fn_ref, bp_ref, bc_ref, bn_ref, w_ref,
             dx_ref, dw_ref, db_ref, extx_ref, extd_ref):
        s = pl.program_id(1)
        first, last = lay.seq_first(s), lay.seq_last(s)
        dcur = (fc_ref[...] + bc_ref[...]) * dsilu(pc_ref[...])
        extd_ref[0:HALO, :] = jnp.where(first, 0.0, (fp_ref[...] + bp_ref[...]) * dsilu(pp_ref[...]))
        extd_ref[HALO:HALO + tt, :] = dcur
        extd_ref[HALO + tt:, :] = jnp.where(last, 0.0, (fn_ref[...] + bn_ref[...]) * dsilu(pn_ref[...]))
        extx_ref[0:HALO, :] = jnp.where(first, 0.0, xp_ref[...])
        extx_ref[HALO:HALO + tt, :] = xc_ref[...]
        extx_ref[HALO + tt:, :] = jnp.where(last, 0.0, xn_ref[...])
        dx = jnp.zeros((tt, c), F32)
        rows = []
        for k in range(5):
            dx = dx + w_ref[k:k + 1, :] * extd_ref[pl.ds(HALO - (k - 2), tt), :]
            rows.append(jnp.sum(dcur * extx_ref[pl.ds(HALO + k - 2, tt), :], axis=0, keepdims=True))
        dx_ref[...] = dx.astype(dx_ref.dtype)
        rows.append(jnp.zeros((3, c), F32))
        _acc(dw_ref, s == 0, jnp.concatenate(rows, axis=0))
        _acc(db_ref, s == 0, jnp.sum(dcur, axis=0, keepdims=True))

    three = lambda cbx: [_halo_prev(lay, c, cbx), _tok(lay, c, cbx), _halo_next(lay, c, cbx)]
    ext = pltpu.VMEM((tt + 2 * HALO, c), F32)
    return _tok_call(name, body, mx.xw // c, lay.ns,
                     three(cb0) + three(0) + three(0) + three(0) + [_glob(8, c, 0)],
                     [_tok(lay, c), _glob(8, c, 0), _glob(1, c, 0)],
                     [jax.ShapeDtypeStruct((lay.ta, mx.xw), BF16), jax.ShapeDtypeStruct((8, mx.xw), F32),
                      jax.ShapeDtypeStruct((1, mx.xw), F32)],
                     (proj, proj, proj, pre, pre, pre, dact_f, dact_f, dact_f, dact_b, dact_b, dact_b, cw),
                     scratch=[ext, ext])


def _softplus(v):
    return jnp.maximum(v, 0.0) + jnp.log1p(jnp.exp(-jnp.abs(v)))


def _dt_fwd(lay, mx, proj, bias, name):
    cb = mx.off_dt // DT_LANES

    def body(p_ref, b_ref, dt_ref):
        dt_ref[...] = _softplus(p_ref[...] + b_ref[...])

    return _tok_call(name, body, 1, lay.ns, [_tok(lay, DT_LANES, cb), _glob(1, DT_LANES)], _tok(lay, DT_LANES),
                     jax.ShapeDtypeStruct((lay.ta, DT_LANES), F32), (proj, bias))


def _dt_bwd(lay, mx, proj, bias, parts, name):
    cb = mx.off_dt // DT_LANES

    def body(p_ref, b_ref, a_ref, b2_ref, c_ref, d_ref, dp_ref, db_ref):
        s = pl.program_id(1)
        ddt = (a_ref[...] + b2_ref[...]) + (c_ref[...] + d_ref[...])
        draw = ddt * jax.nn.sigmoid(p_ref[...] + b_ref[...])
        dp_ref[:, 0:DT_LANES] = draw.astype(dp_ref.dtype)
        dp_ref[:, DT_LANES:] = jnp.zeros((lay.tt, DT_PAD - DT_LANES), dp_ref.dtype)
        _acc(db_ref, s == 0, jnp.sum(draw, axis=0, keepdims=True))

    t = _tok(lay, DT_LANES)
    return _tok_call(name, body, 1, lay.ns, [_tok(lay, DT_LANES, cb), _glob(1, DT_LANES), t, t, t, t],
                     [_tok(lay, DT_PAD), _glob(1, DT_LANES)],
                     [jax.ShapeDtypeStruct((lay.ta, DT_PAD), BF16), jax.ShapeDtypeStruct((1, DT_LANES), F32)],
                     (proj, bias) + tuple(parts))


def _scan_mask(rev):
    r = lax.broadcasted_iota(jnp.int32, (CHUNK, CHUNK), 0)
    c = lax.broadcasted_iota(jnp.int32, (CHUNK, CHUNK), 1)
    return (r <= c) if rev else (r >= c)


def _split_bf16(x):
    hi = x.astype(BF16)
    return hi, (x - hi.astype(F32)).astype(BF16)


@functools.partial(jax.custom_vjp, nondiff_argnums=(0,))
def _cum_cols(rev, x):
    m = _scan_mask(rev).astype(BF16)
    hi, lo = _split_bf16(x)
    return jnp.dot(m, hi, preferred_element_type=F32) + jnp.dot(m, lo, preferred_element_type=F32)


_cum_cols.defvjp(lambda rev, x: (_cum_cols(rev, x), None), lambda rev, _, g: (_cum_cols(not rev, g),))


@functools.partial(jax.custom_vjp, nondiff_argnums=(0,))
def _cum_rows(rev, x):
    m = _scan_mask(not rev).astype(BF16)
    hi, lo = _split_bf16(x)
    return jnp.dot(hi, m, preferred_element_type=F32) + jnp.dot(lo, m, preferred_element_type=F32)


_cum_rows.defvjp(lambda rev, x: (_cum_rows(rev, x), None), lambda rev, _, g: (_cum_rows(not rev, g),))


@functools.partial(jax.custom_vjp, nondiff_argnums=(1,))
def _take_col(x, k):
    return x[:, k:k + 1]


def _take_col_bwd(k, _, g):
    lane = lax.broadcasted_iota(jnp.int32, (g.shape[0], DT_LANES), 1)
    return (jnp.where(lane == k, g, 0.0),)


_take_col.defvjp(lambda x, k: (x[:, k:k + 1], None), _take_col_bwd)


@functools.partial(jax.custom_vjp, nondiff_argnums=(1,))
def _take_row(x, k):
    return x[k:k + 1, :]


def _take_row_bwd(k, _, g):
    sub = lax.broadcasted_iota(jnp.int32, (DT_LANES, g.shape[1]), 0)
    return (jnp.where(sub == k, g, 0.0),)


_take_row.defvjp(lambda x, k: (x[k:k + 1, :], None), _take_row_bwd)


def _ssd_chunk(xh_pairs, bcs, ccs, dtc, dtr, a_row, a_col, st_pairs, *, rev, heads, col0):
    n = CHUNK
    mask = _scan_mask(rev)
    da_c = dtc * a_row
    cs_c = _cum_cols(rev, da_c)
    da_r = dtr * a_col
    cs_r = _cum_rows(rev, da_r)
    tot = jnp.sum(da_c, axis=0, keepdims=True)
    lane = lax.broadcasted_iota(jnp.int32, (n, DT_LANES), 1)
    sub = lax.broadcasted_iota(jnp.int32, (DT_LANES, n), 0)
    lane1 = lax.broadcasted_iota(jnp.int32, (1, DT_LANES), 1)
    left = lax.broadcasted_iota(jnp.int32, (n, 2 * HEAD_DIM), 1) < HEAD_DIM
    top = lax.broadcasted_iota(jnp.int32, (2 * HEAD_DIM, 1), 0) < HEAD_DIM
    nt = (((1,), (1,)), ((), ()))
    tn = (((0,), (0,)), ((), ()))
    scores = [lax.dot_general(ccs[g].astype(BF16), bcs[g].astype(BF16), nt, preferred_element_type=F32)
              for g in range(SSD_GROUPS)]
    ys, sts = [], []
    for p in range(heads // 2):
        g = (2 * p) // (heads // SSD_GROUPS)
        per = []
        for h in (2 * p, 2 * p + 1):
            col = col0 + h
            csv = jnp.sum(jnp.where(lane == col, cs_c, 0.0), axis=1, keepdims=True)
            csr = jnp.sum(jnp.where(sub == col, cs_r, 0.0), axis=0, keepdims=True)
            dtv = jnp.sum(jnp.where(lane == col, dtc, 0.0), axis=1, keepdims=True)
            tv = jnp.sum(jnp.where(lane1 == col, tot, 0.0), axis=1, keepdims=True)
            m = scores[g] * jnp.exp(jnp.where(mask, csv - csr, NEG_BIG))
            per.append((csv, dtv, tv, m))
        (cs1, dt1, t1, m1), (cs2, dt2, t2, m2) = per
        xs = xh_pairs[p] * jnp.where(left, dt1, dt2)
        xsb = xs.astype(BF16)
        y_diag = jnp.where(left, jnp.dot(m1.astype(BF16), xsb, preferred_element_type=F32),
                           jnp.dot(m2.astype(BF16), xsb, preferred_element_type=F32))
        y_off = lax.dot_general(ccs[g].astype(BF16), st_pairs[p].astype(BF16), nt, preferred_element_type=F32)
        y_off = y_off * jnp.where(left, jnp.exp(cs1), jnp.exp(cs2))
        wst = jnp.where(left, jnp.exp(t1 - cs1), jnp.exp(t2 - cs2))
        cst = lax.dot_general((xs * wst).astype(BF16), bcs[g].astype(BF16), tn, preferred_element_type=F32)
        ys.append(y_diag + y_off)
        sts.append(st_pairs[p] * jnp.where(top, jnp.exp(t1), jnp.exp(t2)) + cst)
    return ys, sts


class _Scan:
    def __init__(self, lay, rev):
        self.ncx, self.ncc, self.bl, self.rev = lay.seq // CHUNK, lay.clen // CHUNK, lay.bl, rev
        self.nct = self.ncx + self.ncc

    def chunk(self, b, pos):
        kc = (self.ncc - 1 - pos) if self.rev else pos
        kx = (self.ncx - 1 - (pos - self.ncc)) if self.rev else (pos - self.ncc)
        return jnp.where(pos < self.ncc, self.bl * self.ncx + b * self.ncc + kc, b * self.ncx + kx)


def _ssd_io(mx, x_ref, st_src):
    np_ = mx.heads // 2
    d = mx.d_ssm
    xh = [x_ref[:, 128 * p:128 * (p + 1)] for p in range(np_)]
    bcs = [x_ref[:, d + N_STATE * g:d + N_STATE * (g + 1)] for g in range(SSD_GROUPS)]
    ccs = [x_ref[:, d + mx.gn + N_STATE * g:d + mx.gn + N_STATE * (g + 1)] for g in range(SSD_GROUPS)]
    sts = [st_src[128 * p:128 * (p + 1), :] for p in range(np_)]
    return xh, bcs, ccs, sts


def _ssd_fwd(lay, mx, xbc, dt, dtt, a_row, a_col, rev, name):
    sc = _Scan(lay, rev)
    col0 = mx.heads if rev else 0
    hp = mx.heads * HEAD_DIM

    def body(x_ref, dt_ref, dtt_ref, ar_ref, ac_ref, y_ref, hp_ref, st_ref):
        @pl.when(pl.program_id(1) == 0)
        def _():
            st_ref[...] = jnp.zeros_like(st_ref)

        hp_ref[...] = st_ref[...]
        xh, bcs, ccs, sts = _ssd_io(mx, x_ref, st_ref)
        ys, new = _ssd_chunk(xh, bcs, ccs, dt_ref[...], dtt_ref[...], ar_ref[...], ac_ref[...], sts,
                             rev=rev, heads=mx.heads, col0=col0)
        for p in range(mx.heads // 2):
            y_ref[:, 128 * p:128 * (p + 1)] = ys[p]
            st_ref[128 * p:128 * (p + 1), :] = new[p]

    ch = sc.chunk
    return pl.pallas_call(
        body, grid=(lay.bl, sc.nct),
        in_specs=[pl.BlockSpec((CHUNK, mx.xw), lambda b, i: (ch(b, i), 0)),
                  pl.BlockSpec((CHUNK, DT_LANES), lambda b, i: (ch(b, i), 0)),
                  pl.BlockSpec((DT_LANES, CHUNK), lambda b, i: (0, ch(b, i))),
                  pl.BlockSpec((1, DT_LANES), lambda b, i: (0, 0)),
                  pl.BlockSpec((DT_LANES, 1), lambda b, i: (0, 0))],
        out_specs=[pl.BlockSpec((CHUNK, mx.d_ssm), lambda b, i: (ch(b, i), 0)),
                   pl.BlockSpec((hp, N_STATE), lambda b, i: (b * sc.nct + i, 0))],
        out_shape=[jax.ShapeDtypeStruct((lay.ta, mx.d_ssm), F32),
                   jax.ShapeDtypeStruct((lay.bl * sc.nct * hp, N_STATE), F32)],
        scratch_shapes=[pltpu.VMEM((hp, N_STATE), F32)], name=name, compiler_params=_cparams(2),
    )(xbc, dt, dtt, a_row, a_col)


def _ssd_bwd(lay, mx, xbc, dt, dtt, a_row, a_col, hprev, dy, dskip, rev, name):
    sc = _Scan(lay, rev)
    col0 = mx.heads if rev else 0
    hp = mx.heads * HEAD_DIM
    np_ = mx.heads // 2
    d = mx.d_ssm
    with_skip = dskip is not None

    def body(*refs):
        if with_skip:
            x_ref, dt_ref, dtt_ref, ar_ref, ac_ref, hp_ref, dy_ref, sk_ref = refs[:8]
            rest = refs[8:]
        else:
            x_ref, dt_ref, dtt_ref, ar_ref, ac_ref, hp_ref, dy_ref = refs[:7]
            rest = refs[7:]
        dx_ref, ddc_ref, ddr_ref, dar_ref, dac_ref, ds_ref = rest
        b, i = pl.program_id(0), pl.program_id(1)

        @pl.when(i == 0)
        def _():
            ds_ref[...] = jnp.zeros_like(ds_ref)

        xh, bcs, ccs, sts = _ssd_io(mx, x_ref, hp_ref)
        fn = functools.partial(_ssd_chunk, rev=rev, heads=mx.heads, col0=col0)
        _, vjp = jax.vjp(fn, xh, bcs, ccs, dt_ref[...], dtt_ref[...], ar_ref[...], ac_ref[...], sts)
        dys = [dy_ref[:, 128 * p:128 * (p + 1)] for p in range(np_)]
        dsts = [ds_ref[128 * p:128 * (p + 1), :] for p in range(np_)]
        dxh, dbc, dcc, ddc, ddr, dar, dac, dst = vjp((dys, dsts))
        for p in range(np_):
            v = dxh[p]
            if with_skip:
                v = v + dys[p] * sk_ref[:, 128 * p:128 * (p + 1)]
            dx_ref[:, 128 * p:128 * (p + 1)] = v
            ds_ref[128 * p:128 * (p + 1), :] = dst[p]
        for g in range(SSD_GROUPS):
            dx_ref[:, d + N_STATE * g:d + N_STATE * (g + 1)] = dbc[g]
            dx_ref[:, d + mx.gn + N_STATE * g:d + mx.gn + N_STATE * (g + 1)] = dcc[g]
        ddc_ref[...] = ddc
        ddr_ref[...] = ddr
        first = jnp.logical_and(b == 0, i == 0)
        _acc(dar_ref, first, dar)
        _acc(dac_ref, first, dac)

    ch = lambda b, i: sc.chunk(b, sc.nct - 1 - i)
    in_specs = [pl.BlockSpec((CHUNK, mx.xw), lambda b, i: (ch(b, i), 0)),
                pl.BlockSpec((CHUNK, DT_LANES), lambda b, i: (ch(b, i), 0)),
                pl.BlockSpec((DT_LANES, CHUNK), lambda b, i: (0, ch(b, i))),
                pl.BlockSpec((1, DT_LANES), lambda b, i: (0, 0)),
                pl.BlockSpec((DT_LANES, 1), lambda b, i: (0, 0)),
                pl.BlockSpec((hp, N_STATE), lambda b, i: (b * sc.nct + sc.nct - 1 - i, 0)),
                pl.BlockSpec((CHUNK, d), lambda b, i: (ch(b, i), 0))]
    inputs = [xbc, dt, dtt, a_row, a_col, hprev, dy]
    if with_skip:
        in_specs.append(pl.BlockSpec((1, d), lambda b, i: (0, 0)))
        inputs.append(dskip)
    return pl.pallas_call(
        body, grid=(lay.bl, sc.nct), in_specs=in_specs,
        out_specs=[pl.BlockSpec((CHUNK, mx.xw), lambda b, i: (ch(b, i), 0)),
                   pl.BlockSpec((CHUNK, DT_LANES), lambda b, i: (ch(b, i), 0)),
                   pl.BlockSpec((DT_LANES, CHUNK), lambda b, i: (0, ch(b, i))),
                   pl.BlockSpec((1, DT_LANES), lambda b, i: (0, 0)),
                   pl.BlockSpec((DT_LANES, 1), lambda b, i: (0, 0))],
        out_shape=[jax.ShapeDtypeStruct((lay.ta, mx.xw), F32), jax.ShapeDtypeStruct((lay.ta, DT_LANES), F32),
                   jax.ShapeDtypeStruct((DT_LANES, lay.ta), F32), jax.ShapeDtypeStruct((1, DT_LANES), F32),
                   jax.ShapeDtypeStruct((DT_LANES, 1), F32)],
        scratch_shapes=[pltpu.VMEM((hp, N_STATE), F32)], name=name, compiler_params=_cparams(2),
    )(*inputs)


def _gate_f(yf, yb, xh, z, drow, nw):
    dd = yf.shape[-1]
    half = dd // SSD_GROUPS
    yz = (yf + yb + drow * xh) * _silu(z)
    lo = lax.broadcasted_iota(jnp.int32, yz.shape, 1) < half
    sq = yz * yz
    ms1 = jnp.sum(jnp.where(lo, sq, 0.0), axis=-1, keepdims=True) / half
    ms2 = jnp.sum(jnp.where(lo, 0.0, sq), axis=-1, keepdims=True) / half
    return yz * jnp.where(lo, lax.rsqrt(ms1 + EPS), lax.rsqrt(ms2 + EPS)) * nw


def _gate_fwd(lay, mx, yf, yb, xbc, proj, drow, nw, name):
    d = mx.d_ssm

    def body(yf_ref, yb_ref, xh_ref, z_ref, d_ref, w_ref, o_ref):
        o_ref[...] = _gate_f(yf_ref[...], yb_ref[...], xh_ref[...], z_ref[...], d_ref[...], w_ref[...]).astype(o_ref.dtype)

    t = _tok(lay, d)
    return _tok_call(name, body, 1, lay.nsx, [t, t, t, t, _glob(1, d), _glob(1, d)], t,
                     jax.ShapeDtypeStruct((lay.tx, d), BF16), (yf, yb, xbc, proj, drow, nw))


def _gate_bwd(lay, mx, yf, yb, xbc, proj, drow, nw, dcat, name):
    d = mx.d_ssm
    nsx = lay.nsx

    def body(yf_ref, yb_ref, xh_ref, z_ref, d_ref, w_ref, dc_ref, dy_ref, dz_ref, dd_ref, dw_ref):
        s = pl.program_id(1)

        @pl.when(s < nsx)
        def _():
            _, vjp = jax.vjp(_gate_f, yf_ref[...], yb_ref[...], xh_ref[...], z_ref[...], d_ref[...], w_ref[...])
            dyf, _, _, dz, dd, dw = vjp(dc_ref[...])
            dy_ref[...] = dyf
            dz_ref[...] = dz.astype(dz_ref.dtype)
            _acc(dd_ref, s == 0, dd)
            _acc(dw_ref, s == 0, dw)

        @pl.when(s >= nsx)
        def _():
            dy_ref[...] = jnp.zeros_like(dy_ref)
            dz_ref[...] = jnp.zeros_like(dz_ref)

    t = _tok(lay, d)
    return _tok_call(name, body, 1, lay.ns, [t, t, t, t, _glob(1, d), _glob(1, d), _tok(lay, d, clamp=nsx - 1)],
                     [t, t, _glob(1, d), _glob(1, d)],
                     [jax.ShapeDtypeStruct((lay.ta, d), F32), jax.ShapeDtypeStruct((lay.ta, d), BF16),
                      jax.ShapeDtypeStruct((1, d), F32), jax.ShapeDtypeStruct((1, d), F32)],
                     (yf, yb, xbc, proj, drow, nw, dcat))


def _glu_fwd(lay, mx, proj, name):
    d = mx.d_conv
    c = math.gcd(mx.off_glu, d)
    cb = mx.off_glu // c

    def body(a_ref, b_ref, o_ref):
        o_ref[...] = a_ref[...] * jax.nn.sigmoid(b_ref[...])

    return _tok_call(name, body, d // c, lay.nsx, [_tok(lay, c, cb), _tok(lay, c, cb + d // c)], _tok(lay, c),
                     jax.ShapeDtypeStruct((lay.tx, d), F32), (proj, proj))


def _glu_bwd(lay, mx, proj, du, name):
    d = mx.d_conv
    c = math.gcd(mx.off_glu, d)
    cb = mx.off_glu // c
    nsx = lay.nsx

    def body(a_ref, b_ref, du_ref, da_ref, db_ref):
        s = pl.program_id(1)

        @pl.when(s < nsx)
        def _():
            sg = jax.nn.sigmoid(b_ref[...])
            duv = du_ref[...]
            da_ref[...] = (duv * sg).astype(da_ref.dtype)
            db_ref[...] = (duv * a_ref[...] * sg * (1.0 - sg)).astype(db_ref.dtype)

        @pl.when(s >= nsx)
        def _():
            da_ref[...] = jnp.zeros_like(da_ref)
            db_ref[...] = jnp.zeros_like(db_ref)

    sh = jax.ShapeDtypeStruct((lay.ta, d), BF16)
    return _tok_call(name, body, d // c, lay.ns,
                     [_tok(lay, c, cb), _tok(lay, c, cb + d // c), _tok(lay, c, clamp=nsx - 1)],
                     [_tok(lay, c), _tok(lay, c)], [sh, sh], (proj, proj, du))


def _axial(lay, mx, u, dy, cw, cb, name):
    d, seq = mx.d_conv, lay.seq
    kw = cw.shape[0]
    pad = kw // 2
    c = _pick(d // 2, (256, 128))
    ncb = d // c
    zpad = GRID_W * pad
    zpad = -(-zpad // 8) * 8
    backward = dy is not None

    def shifted(ext_ref, off):
        return ext_ref[pl.ds(zpad + off, seq), :]

    def valid_row(off):
        col = lax.broadcasted_iota(jnp.int32, (seq, c), 0) % GRID_W
        return jnp.logical_and(col + off >= 0, col + off < GRID_W)

    def fill(ext_ref, v):
        ext_ref[0:zpad, :] = jnp.zeros((zpad, c), F32)
        ext_ref[zpad:zpad + seq, :] = v
        ext_ref[zpad + seq:, :] = jnp.zeros((zpad, c), F32)

    def conv(ext_ref, w_ref, is_row, sign):
        acc = jnp.zeros((seq, c), F32)
        for k in range(kw):
            off = sign * ((k - pad) if is_row else GRID_W * (k - pad))
            v = shifted(ext_ref, off)
            if is_row:
                v = jnp.where(valid_row(off), v, 0.0)
            acc = acc + w_ref[k:k + 1, :] * v
        return acc

    def fwd_body(u_ref, w_ref, b_ref, o_ref, ext_ref):
        j = pl.program_id(0)
        fill(ext_ref, u_ref[...])

        @pl.when(j < ncb // 2)
        def _():
            o_ref[...] = conv(ext_ref, w_ref, True, 1) + b_ref[...]

        @pl.when(j >= ncb // 2)
        def _():
            o_ref[...] = conv(ext_ref, w_ref, False, 1) + b_ref[...]

    def bwd_body(u_ref, dy_ref, w_ref, du_ref, dw_ref, db_ref, extu_ref, extd_ref):
        j, b = pl.program_id(0), pl.program_id(1)
        dyv = dy_ref[...]
        fill(extu_ref, u_ref[...])
        fill(extd_ref, dyv)

        def grads(is_row):
            du_ref[...] = conv(extd_ref, w_ref, is_row, -1)
            rows = []
            for k in range(kw):
                off = (k - pad) if is_row else GRID_W * (k - pad)
                v = shifted(extu_ref, off)
                if is_row:
                    v = jnp.where(valid_row(off), v, 0.0)
                rows.append(jnp.sum(dyv * v, axis=0, keepdims=True))
            _acc(dw_ref, b == 0, jnp.concatenate(rows, axis=0))

        @pl.when(j < ncb // 2)
        def _():
            grads(True)

        @pl.when(j >= ncb // 2)
        def _():
            grads(False)

        _acc(db_ref, b == 0, jnp.sum(dyv, axis=0, keepdims=True))

    seq_spec = pl.BlockSpec((seq, c), lambda j, b: (b, j))
    w_spec = pl.BlockSpec((kw, c), lambda j, b: (0, j))
    b_spec = pl.BlockSpec((1, c), lambda j, b: (0, j))
    ext = pltpu.VMEM((seq + 2 * zpad, c), F32)
    if not backward:
        return pl.pallas_call(fwd_body, grid=(ncb, lay.bl), in_specs=[seq_spec, w_spec, b_spec], out_specs=seq_spec,
                              out_shape=jax.ShapeDtypeStruct((lay.tx, d), F32), scratch_shapes=[ext], name=name,
                              compiler_params=_cparams(2))(u, cw, cb)
    return pl.pallas_call(bwd_body, grid=(ncb, lay.bl), in_specs=[seq_spec, seq_spec, w_spec],
                          out_specs=[seq_spec, w_spec, b_spec],
                          out_shape=[jax.ShapeDtypeStruct((lay.tx, d), F32), jax.ShapeDtypeStruct((kw, d), F32),
                                     jax.ShapeDtypeStruct((1, d), F32)],
                          scratch_shapes=[ext, ext], name=name, compiler_params=_cparams(2))(u, dy, cw)


def _ln_silu_f(u, w, b):
    mu = jnp.mean(u, axis=-1, keepdims=True)
    var = jnp.mean(jnp.square(u - mu), axis=-1, keepdims=True)
    return _silu((u - mu) * lax.rsqrt(var + EPS) * w + b)


def _ln_fwd(lay, mx, u, w, b, name):
    d = mx.d_conv

    def body(u_ref, w_ref, b_ref, o_ref):
        o_ref[...] = _ln_silu_f(u_ref[...], w_ref[...], b_ref[...]).astype(o_ref.dtype)

    return _tok_call(name, body, 1, lay.nsx, [_tok(lay, d), _glob(1, d), _glob(1, d)], _tok(lay, d),
                     jax.ShapeDtypeStruct((lay.tx, d), BF16), (u, w, b))


def _ln_bwd(lay, mx, u, w, b, dcat, name):
    d = mx.d_conv

    def body(u_ref, w_ref, b_ref, dc_ref, du_ref, dw_ref, db_ref):
        s = pl.program_id(1)
        _, vjp = jax.vjp(_ln_silu_f, u_ref[...], w_ref[...], b_ref[...])
        du, dw, db = vjp(dc_ref[...])
        du_ref[...] = du
        _acc(dw_ref, s == 0, dw)
        _acc(db_ref, s == 0, db)

    return _tok_call(name, body, 1, lay.nsx, [_tok(lay, d), _glob(1, d), _glob(1, d), _tok(lay, d, 1)],
                     [_tok(lay, d), _glob(1, d), _glob(1, d)],
                     [jax.ShapeDtypeStruct((lay.tx, d), F32), jax.ShapeDtypeStruct((1, d), F32),
                      jax.ShapeDtypeStruct((1, d), F32)], (u, w, b, dcat))


def _gate_tile(dff):
    return dff // 2 if (dff // 2) % 128 == 0 else dff


def _interleave(wg, wu, ft):
    return jnp.concatenate([t[:, k:k + ft] for k in range(0, wg.shape[1], ft) for t in (wg, wu)], axis=1)


def _deinterleave(wgu, ft):
    n = wgu.shape[1]
    gate = jnp.concatenate([wgu[:, k:k + ft] for k in range(0, n, 2 * ft)], axis=1)
    up = jnp.concatenate([wgu[:, k + ft:k + 2 * ft] for k in range(0, n, 2 * ft)], axis=1)
    return gate, up


def _ffn_fwd(lay, nseg, x, nw, modv, k0, wts, tag):
    wgu, wd, ft = wts
    h = _norm_mod_fwd(lay, nseg, x, nw, modv, k0, tag + "_norm")
    t = h.shape[0]

    def act(acc):
        g, u = acc[:, :ft], acc[:, ft:]
        sg = jax.nn.sigmoid(g)
        sl = g * sg
        return jnp.concatenate([u * (sg * (1.0 + g * (1.0 - sg))), sl], axis=1), sl * u

    s, a = _mm(h, wgu, "nn", None, tag + "_gu", tn=2 * ft, tm=256 if t % 256 == 0 else None,
               epilogue=act, outs=[(2 * ft, BF16), (ft, BF16)])
    o = _mm(a, wd, "nn", F32, tag + "_down")
    y = _resid_fwd(lay, nseg, x, o, modv, k0 + 2, 0.5, tag + "_res")
    return y, (x, h, s, a, o)


def _ffn_bwd(lay, nseg, dy, saved, nw, modv, k0, wts, tag):
    wgu, wd, ft = wts
    x, h, s, a, o = saved
    do, dgate = _resid_bwd(lay, nseg, dy, o, modv, k0 + 2, 0.5, tag + "_dres")

    def through_act(da, s_tile):
        return (jnp.concatenate([da, da], axis=1) * s_tile.astype(F32),)

    (dgu,) = _mm(do, wd.T, "nn", None, tag + "_da", tn=ft, tm=_big_tile(do.shape[0], 1024), extras=[(s, 2 * ft)],
                 epilogue=through_act, outs=[(2 * ft, BF16)])
    dwd = _mm(a, do, "tn", F32, tag + "_dwd")
    dh = _mm(dgu, wgu, "nt", F32, tag + "_dh")
    dwgu = _mm(h, dgu, "tn", F32, tag + "_dwgu")
    dx, dnw, dss = _norm_mod_bwd(lay, nseg, nseg, x, nw, modv, k0, dh, dy, tag + "_dnorm")
    return dx, (dwgu, dwd), dnw, jnp.concatenate([dss, dgate], axis=1)


def _local_step(lay, mx, xa, target, modv, w):
    d, bl = lay.d, lay.bl
    g = {}
    xa1, ffn1 = _ffn_fwd(lay, lay.ns, xa, w["norm_ffn1"], modv, 0, w["ffn1"], "ffn1")
    ha = _norm_mod_fwd(lay, lay.ns, xa1, w["norm_mix"], modv, 3, "mix_norm")
    proj = _mm(ha, w["w_in"], "nn", F32, "mix_in")
    pre, xbc = _conv5_fwd(lay.fine(), mx, proj, w["conv_w"], w["conv_b"], "mix_conv")
    dt = _dt_fwd(lay, mx, proj, w["dt_bias"], "mix_dt")
    dtt = dt.T
    yf, hpf = _ssd_fwd(lay, mx, xbc, dt, dtt, w["a_row"], w["a_col"], False, "ssd_f")
    yb, hpb = _ssd_fwd(lay, mx, xbc, dt, dtt, w["a_row"], w["a_col"], True, "ssd_b")
    cat_y = _gate_fwd(lay, mx, yf, yb, xbc, proj, w["d_row"], w["ssm_norm_w"], "mix_gate")
    u0 = _glu_fwd(lay, mx, proj, "mix_glu")
    uc = _axial(lay, mx, u0, None, w["cconv_w"], w["cconv_b"], "mix_axial")
    cat_u = _ln_fwd(lay, mx, uc, w["ln_w"], w["ln_b"], "mix_ln")
    cat = jnp.concatenate([cat_y, cat_u], axis=1)
    mix = _mm(cat, w["w_out"], "nn", F32, "mix_out")
    x2 = _resid_fwd(lay, lay.nsx, xa1, mix, modv, 5, 1.0, "mix_res")
    x3, ffn2 = _ffn_fwd(lay, lay.nsx, x2, w["norm_ffn2"], modv, 6, w["ffn2"], "ffn2")
    loss, dx3, g["final_norm"] = _final_loss(lay, x3, w["final_norm"], target, "loss")
    dx2, g["ffn2"], g["norm_ffn2"], dmod2 = _ffn_bwd(lay, lay.nsx, dx3, ffn2, w["norm_ffn2"], modv, 6, w["ffn2"], "ffn2")
    dmix, dg2 = _resid_bwd(lay, lay.nsx, dx2, mix, modv, 5, 1.0, "mix_dres")
    dcat = _mm(dmix, w["w_out"], "nt", F32, "mix_dcat")
    g["w_out"] = _mm(cat, dmix, "tn", F32, "mix_dwout")
    duc, g["ln_w"], g["ln_b"] = _ln_bwd(lay, mx, uc, w["ln_w"], w["ln_b"], dcat, "mix_dln")
    du0, g["cconv_w"], g["cconv_b"] = _axial(lay, mx, u0, duc, w["cconv_w"], None, "mix_daxial")
    dglu_a, dglu_b = _glu_bwd(lay, mx, proj, du0, "mix_dglu")
    dyssd, dz, g["d_row"], g["ssm_norm_w"] = _gate_bwd(lay, mx, yf, yb, xbc, proj, w["d_row"], w["ssm_norm_w"], dcat,
                                                       "mix_dgate")
    dxf, ddcf, ddrf, darf, dacf = _ssd_bwd(lay, mx, xbc, dt, dtt, w["a_row"], w["a_col"], hpf, dyssd, w["d_row"],
                                           False, "ssd_df")
    dxb, ddcb, ddrb, darb, dacb = _ssd_bwd(lay, mx, xbc, dt, dtt, w["a_row"], w["a_col"], hpb, dyssd, None,
                                           True, "ssd_db")
    g["a_row"] = (darf + darb) + (dacf + dacb).T
    dxbc, g["conv_w"], g["conv_b"] = _conv5_bwd(lay.fine(), mx, proj, pre, dxf, dxb, w["conv_w"], "mix_dconv")
    ddtraw, g["dt_bias"] = _dt_bwd(lay, mx, proj, w["dt_bias"], (ddcf, ddcb, ddrf.T, ddrb.T), "mix_ddt")
    dproj = jnp.concatenate([dz, dxbc, ddtraw, dglu_a, dglu_b], axis=1)
    dha = _mm(dproj, w["w_in"], "nt", F32, "mix_dha")
    g["w_in"] = _mm(ha, dproj, "tn", F32, "mix_dwin")
    dxa1, g["norm_mix"], dss_mix = _norm_mod_bwd(lay, lay.ns, lay.nsx, xa1, w["norm_mix"], modv, 3, dha, dx2, "mix_dnorm")
    dxa, g["ffn1"], g["norm_ffn1"], dmod1 = _ffn_bwd(lay, lay.ns, dxa1, ffn1, w["norm_ffn1"], modv, 0, w["ffn1"], "ffn1")
    zrow = lambda t: jnp.concatenate([t, jnp.zeros((1,) + t.shape[1:], F32)], axis=0)
    dmodv = jnp.concatenate([dmod1, dss_mix, zrow(dg2), zrow(dmod2)], axis=1)
    return loss, dxa[:lay.tx], g, dmodv


def _all_gather(xs, name, in_hbm):
    na = len(xs)

    def body(*refs):
        x_refs, out_refs = refs[:na], refs[na:2 * na]
        send_sems, recv_sems, local_sems = refs[2 * na:]
        mx_, my_, mc_ = lax.axis_index("x"), lax.axis_index("y"), lax.axis_index("c")
        me, sibling = (mx_, my_, mc_), (mx_, my_, 1 - mc_)
        chips = [(1 - mx_, my_), (mx_, 1 - my_), (1 - mx_, 1 - my_)]

        def slot(a, px, py, pc):
            return out_refs[a].at[4 * px + 2 * py + pc]

        def copy(a, k, block, to, own=False):
            return pltpu.make_async_remote_copy(
                src_ref=x_refs[a] if own else slot(a, *block), dst_ref=slot(a, *block),
                send_sem=send_sems.at[7 * a + k], recv_sem=recv_sems.at[7 * a + k], device_id=to, device_id_type=MESH)

        mine = [pltpu.make_async_copy(x_refs[a], slot(a, *me), local_sems.at[a]) for a in range(na)]
        for cp in mine:
            cp.start()
        first = []
        for a in range(na):
            first.append(copy(a, 0, me, sibling, own=True))
            first += [copy(a, 1 + j, me, (*chip, mc_), own=True) for j, chip in enumerate(chips)]
        for cp in first:
            cp.start()
        passed = []
        for j, chip in enumerate(chips):
            for a in range(na):
                copy(a, 1 + j, (*chip, mc_), me).wait_recv()
                fwd = copy(a, 4 + j, (*chip, mc_), sibling)
                fwd.start()
                passed.append(fwd)
        for a in range(na):
            copy(a, 0, sibling, me).wait_recv()
            for j, chip in enumerate(chips):
                copy(a, 4 + j, (*chip, 1 - mc_), me).wait_recv()
        for cp in first + passed:
            cp.wait_send()
        for cp in mine:
            cp.wait()

    space = pl.ANY if in_hbm else pltpu.VMEM
    return pl.pallas_call(
        body, out_shape=[jax.ShapeDtypeStruct((N_DEV,) + tuple(x.shape), x.dtype) for x in xs],
        in_specs=[pl.BlockSpec(memory_space=space)] * na, out_specs=[pl.BlockSpec(memory_space=space)] * na,
        scratch_shapes=[pltpu.SemaphoreType.DMA((7 * na,)), pltpu.SemaphoreType.DMA((7 * na,)),
                        pltpu.SemaphoreType.DMA((na,))],
        name=name,
    )(*xs)


N_CHIPS = 4


def _swap_sibling(gs, name):
    na = len(gs)

    def body(*refs):
        g_refs, out_refs, send_sems, recv_sems = refs[:na], refs[na:2 * na], refs[2 * na], refs[2 * na + 1]
        mx_, my_, mc_ = lax.axis_index("x"), lax.axis_index("y"), lax.axis_index("c")
        copies = [pltpu.make_async_remote_copy(
            src_ref=g_refs[a].at[k, 1 - mc_], dst_ref=out_refs[a].at[k], send_sem=send_sems.at[N_CHIPS * a + k],
            recv_sem=recv_sems.at[N_CHIPS * a + k], device_id=(mx_, my_, 1 - mc_), device_id_type=MESH)
            for a in range(na) for k in range(N_CHIPS)]
        for cp in copies:
            cp.start()
        for cp in copies:
            cp.wait_recv()
        for cp in copies:
            cp.wait_send()

    return pl.pallas_call(
        body, out_shape=[jax.ShapeDtypeStruct((N_CHIPS,) + tuple(g.shape[2:]), g.dtype) for g in gs],
        in_specs=[pl.BlockSpec(memory_space=pl.ANY)] * na, out_specs=[pl.BlockSpec(memory_space=pl.ANY)] * na,
        scratch_shapes=[pltpu.SemaphoreType.DMA((N_CHIPS * na,)), pltpu.SemaphoreType.DMA((N_CHIPS * na,))], name=name,
    )(*gs)


def _row_tile(r, n):
    if r * n * 4 <= (1 << 20):
        return r
    for t in (1024, 512, 256, 128, 64, 32, 16, 8):
        if r % t == 0 and t * n * 4 <= (1 << 20):
            return t
    return r


def _pair_add(place, g, got, name):
    _, _, r, n = g.shape
    tr = _row_tile(r, n)

    def body(place_ref, g_ref, got_ref, o_ref, ob_ref):
        s = g_ref[...] + got_ref[...]
        o_ref[...] = s
        ob_ref[...] = s.astype(ob_ref.dtype)

    blk = pl.BlockSpec((None, tr, n), lambda k, i, pr: (k, i, 0))
    grid_spec = pltpu.PrefetchScalarGridSpec(
        num_scalar_prefetch=1, grid=(N_CHIPS, r // tr),
        in_specs=[pl.BlockSpec((None, None, tr, n), lambda k, i, pr: (k, pr[0], i, 0)), blk], out_specs=[blk, blk])
    return pl.pallas_call(body, grid_spec=grid_spec,
                          out_shape=[jax.ShapeDtypeStruct((N_CHIPS, r, n), F32), jax.ShapeDtypeStruct((N_CHIPS, r, n), BF16)],
                          name=name, compiler_params=_cparams(2))(place, g, got)


def _swap_chips(ps, name):
    na = len(ps)

    def body(*refs):
        p_refs, out_refs, send_sems, recv_sems = refs[:na], refs[na:2 * na], refs[2 * na], refs[2 * na + 1]
        mx_, my_, mc_ = lax.axis_index("x"), lax.axis_index("y"), lax.axis_index("c")
        chips = [(1 - mx_, my_), (mx_, 1 - my_), (1 - mx_, 1 - my_)]
        copies = [pltpu.make_async_remote_copy(
            src_ref=p_refs[a].at[2 * cx + cy], dst_ref=out_refs[a].at[j], send_sem=send_sems.at[N_CHIP_PEERS * a + j],
            recv_sem=recv_sems.at[N_CHIP_PEERS * a + j], device_id=(cx, cy, mc_), device_id_type=MESH)
            for a in range(na) for j, (cx, cy) in enumerate(chips)]
        for cp in copies:
            cp.start()
        for cp in copies:
            cp.wait_recv()
        for cp in copies:
            cp.wait_send()

    return pl.pallas_call(
        body, out_shape=[jax.ShapeDtypeStruct((N_CHIP_PEERS,) + tuple(p.shape[1:]), p.dtype) for p in ps],
        in_specs=[pl.BlockSpec(memory_space=pl.ANY)] * na, out_specs=[pl.BlockSpec(memory_space=pl.ANY)] * na,
        scratch_shapes=[pltpu.SemaphoreType.DMA((N_CHIP_PEERS * na,)), pltpu.SemaphoreType.DMA((N_CHIP_PEERS * na,))],
        name=name,
    )(*ps)


def _sum_lead(x, name):
    k, r, n = x.shape
    tr = _row_tile(r, n * k)

    def body(x_ref, o_ref):
        acc = x_ref[0]
        for i in range(1, k):
            acc = acc + x_ref[i]
        o_ref[...] = acc

    return pl.pallas_call(body, grid=(r // tr,), in_specs=[pl.BlockSpec((k, tr, n), lambda i: (0, i, 0))],
                          out_specs=pl.BlockSpec((tr, n), lambda i: (i, 0)),
                          out_shape=jax.ShapeDtypeStruct((r, n), x.dtype), name=name, compiler_params=_cparams(1))(x)


def _adamw(place, w, parts, m, v, name):
    shape = w.shape
    cols = shape[-1]
    rows = math.prod(shape[:-1])
    to2 = lambda t: t.reshape(rows, cols)
    tr = _row_tile(rows, cols) if rows * cols * 4 > (1 << 20) else rows
    npart = len(parts)
    spec = pl.BlockSpec((tr, cols), lambda i, pr: (i, 0))
    part_specs, part_args = [], []
    for piece in parts:
        if isinstance(piece, tuple):
            stack, k = piece
            part_args.append(stack.reshape(stack.shape[0], rows, cols))
            if k == "chip":
                part_specs.append(pl.BlockSpec((None, tr, cols), lambda i, pr: (pr[1], i, 0)))
            else:
                part_specs.append(pl.BlockSpec((None, tr, cols), functools.partial(lambda i, pr, kk: (kk, i, 0), kk=k)))
        else:
            part_args.append(to2(piece))
            part_specs.append(spec)

    def body(place_ref, *refs):
        w_ref, m_ref, v_ref = refs[0], refs[1 + npart], refs[2 + npart]
        g_ref, d_ref, nm_ref, nv_ref = refs[3 + npart:]
        g = refs[1][...].astype(F32)
        for q in range(1, npart):
            g = g + refs[1 + q][...].astype(F32)
        mm = ADAM_B1 * m_ref[...] + (1.0 - ADAM_B1) * g
        vv = ADAM_B2 * v_ref[...] + (1.0 - ADAM_B2) * jnp.square(g)
        m_hat = mm / (1.0 - ADAM_B1 ** ADAM_STEP)
        v_hat = vv / (1.0 - ADAM_B2 ** ADAM_STEP)
        g_ref[...] = g
        d_ref[...] = -ADAM_LR * (m_hat / (jnp.sqrt(v_hat) + ADAM_EPS) + ADAM_WD * w_ref[...])
        nm_ref[...] = mm
        nv_ref[...] = vv

    sh = jax.ShapeDtypeStruct((rows, cols), F32)
    grid_spec = pltpu.PrefetchScalarGridSpec(num_scalar_prefetch=1, grid=(rows // tr,),
                                             in_specs=[spec] + part_specs + [spec, spec], out_specs=[spec] * 4)
    outs = pl.pallas_call(body, grid_spec=grid_spec, out_shape=[sh] * 4, name=name, compiler_params=_cparams(1),
                          )(place, to2(w), *part_args, to2(m), to2(v))
    return tuple(o.reshape(shape) for o in outs)


def _pack_rows(items, width):
    rows = []
    for t in items:
        flat = t.reshape(-1)
        n = flat.shape[0]
        k = -(-n // width)
        if k * width > n:
            flat = jnp.concatenate([flat, jnp.zeros((k * width - n,), t.dtype)])
        rows.append(flat.reshape(k, width))
    out = jnp.concatenate(rows, axis=0)
    pad = -out.shape[0] % 8
    if pad:
        out = jnp.concatenate([out, jnp.zeros((pad, width), out.dtype)], axis=0)
    return out


def _unpack_rows(packed, shapes, lead=()):
    width = packed.shape[-1]
    out, r = [], 0
    for sh in shapes:
        n = math.prod(sh)
        k = -(-n // width)
        piece = packed[..., r:r + k, :].reshape(tuple(lead) + (k * width,))[..., :n]
        out.append(piece.reshape(tuple(lead) + tuple(sh)))
        r += k
    return out


def _cols_full(t):
    return jnp.transpose(t, (1, 0, 2)).reshape(t.shape[1], -1)


def _cols_shards(t):
    d = t.shape[0]
    return jnp.transpose(t.reshape(d, N_DEV, -1), (1, 0, 2))


BIG = ("ffn1_gate", "ffn1_up", "ffn1_down", "w_in", "w_out", "ffn2_gate", "ffn2_up", "ffn2_down")
ROW_SHARDED = ("ffn1_down", "w_out", "ffn2_down")


def kernel(x, c, ctx, c_ctx, w_mod, b_mod, norm_ffn1, ffn1_gate, ffn1_up, ffn1_down, norm_mix, w_in, ssm_conv_w, ssm_conv_b, dt_bias_fwd, dt_bias_bwd, a_log_fwd, a_log_bwd, ssm_d, ssm_norm_w, cconv_w, cconv_b, cconv_ln_w, cconv_ln_b, w_out, norm_ffn2, ffn2_gate, ffn2_up, ffn2_down, final_norm, loss_target, m_c_ctx, m_w_mod, m_b_mod, m_norm_ffn1, m_ffn1_gate, m_ffn1_up, m_ffn1_down, m_norm_mix, m_w_in, m_ssm_conv_w, m_ssm_conv_b, m_dt_bias_fwd, m_dt_bias_bwd, m_a_log_fwd, m_a_log_bwd, m_ssm_d, m_ssm_norm_w, m_cconv_w, m_cconv_b, m_cconv_ln_w, m_cconv_ln_b, m_w_out, m_norm_ffn2, m_ffn2_gate, m_ffn2_up, m_ffn2_down, m_final_norm, v_c_ctx, v_w_mod, v_b_mod, v_norm_ffn1, v_ffn1_gate, v_ffn1_up, v_ffn1_down, v_norm_mix, v_w_in, v_ssm_conv_w, v_ssm_conv_b, v_dt_bias_fwd, v_dt_bias_bwd, v_a_log_fwd, v_a_log_bwd, v_ssm_d, v_ssm_norm_w, v_cconv_w, v_cconv_b, v_cconv_ln_w, v_cconv_ln_b, v_w_out, v_norm_ffn2, v_ffn2_gate, v_ffn2_up, v_ffn2_down, v_final_norm):
    args = dict(locals())
    names = ("c_ctx", "w_mod", "b_mod", "norm_ffn1", "ffn1_gate", "ffn1_up", "ffn1_down", "norm_mix", "w_in",
             "ssm_conv_w", "ssm_conv_b", "dt_bias_fwd", "dt_bias_bwd", "a_log_fwd", "a_log_bwd", "ssm_d", "ssm_norm_w",
             "cconv_w", "cconv_b", "cconv_ln_w", "cconv_ln_b", "w_out", "norm_ffn2", "ffn2_gate", "ffn2_up",
             "ffn2_down", "final_norm")
    wts = {n: args[n] for n in names}
    bl, seq, d = x.shape
    clen = ctx.shape[1]
    heads = dt_bias_fwd.shape[1]
    ft = _gate_tile(ffn1_gate.shape[2] * N_DEV)
    lay = _Lay(bl, seq, clen, d)
    mx = _Mix(d, heads)
    nb = bl * N_DEV
    me = 4 * lax.axis_index("x") + 2 * lax.axis_index("y") + lax.axis_index("c")
    mcols = w_mod.shape[2]
    n_ctx_mod = 5 * d

    place = jnp.stack([lax.axis_index("c"), 2 * lax.axis_index("x") + lax.axis_index("y")]).astype(jnp.int32)

    small_shapes = [(bl, d), ssm_conv_w.shape[1:], cconv_w.shape[1:]]
    (g1,) = _all_gather([_pack_rows([c, ssm_conv_w, cconv_w], d)], "gather_small", False)
    c_g, conv_g, cconv_g = _unpack_rows(g1, small_shapes, (N_DEV,))
    c_all = c_g.reshape(nb, d)
    conv_w_full = jnp.transpose(conv_g, (1, 0, 2)).reshape(conv_g.shape[1], -1)
    cconv_w_full = jnp.transpose(cconv_g, (1, 0, 2)).reshape(cconv_g.shape[1], -1)

    s_all = jnp.concatenate([_silu(c_all), _silu(c_ctx)[None, :], jnp.zeros((7, d), F32)], axis=0)
    mod_cols = _mm(s_all, w_mod[0], "nn", F32, "mod_cols")
    (g2,) = _all_gather([mod_cols], "gather_mod", False)
    mod_all = _cols_full(g2) + b_mod
    mod_mine = jnp.concatenate([lax.dynamic_slice_in_dim(mod_all, me * bl, bl, axis=0), mod_all[nb:nb + 1]], axis=0)
    modv = mod_mine.reshape(bl + 1, N_MOD, d)

    full = dict(zip(BIG, _all_gather([wts[n][0].astype(BF16) for n in BIG], "gather_weights", True)))
    hh = 2 * heads
    w_in_f = _cols_full(full["w_in"])
    w_in_p = jnp.concatenate([w_in_f[:, :mx.off_dt_ref + hh], jnp.zeros((d, DT_PAD - hh), BF16),
                              w_in_f[:, mx.off_glu_ref:]], axis=1)
    lanes_pad = lambda a, b: jnp.concatenate([a, b, jnp.zeros((1, DT_LANES - hh), F32)], axis=1)
    a_vals = lanes_pad(-jnp.exp(a_log_fwd), -jnp.exp(a_log_bwd))
    w = {
        "norm_ffn1": norm_ffn1, "norm_mix": norm_mix, "norm_ffn2": norm_ffn2, "final_norm": final_norm[None, :],
        "ffn1": (_interleave(_cols_full(full["ffn1_gate"]), _cols_full(full["ffn1_up"]), ft),
                 full["ffn1_down"].reshape(-1, d), ft),
        "ffn2": (_interleave(_cols_full(full["ffn2_gate"]), _cols_full(full["ffn2_up"]), ft),
                 full["ffn2_down"].reshape(-1, d), ft),
        "w_in": w_in_p, "w_out": full["w_out"].reshape(-1, d),
        "conv_w": jnp.concatenate([conv_w_full, jnp.zeros((3, mx.xw), F32)], axis=0), "conv_b": ssm_conv_b,
        "dt_bias": lanes_pad(dt_bias_fwd, dt_bias_bwd), "a_row": a_vals, "a_col": a_vals.T,
        "d_row": jnp.repeat(ssm_d, HEAD_DIM, axis=1), "ssm_norm_w": ssm_norm_w,
        "cconv_w": cconv_w_full, "cconv_b": cconv_b, "ln_w": cconv_ln_w, "ln_b": cconv_ln_b,
    }

    xa = jnp.concatenate([x.reshape(bl * seq, d), ctx.reshape(bl * clen, d)], axis=0)
    loss, grad_x, g, dmodv = _local_step(lay, mx, xa, loss_target.reshape(bl * seq, d), modv, w)
    loss = lax.psum(loss[0, 0], ("x", "y", "c"))

    dw_in = jnp.concatenate([g["w_in"][:, :mx.off_dt_ref + hh], g["w_in"][:, mx.off_glu:]], axis=1)
    gbig = {}
    for tag in ("ffn1", "ffn2"):
        dwgu, dwd = g[tag]
        dgate, dup = _deinterleave(dwgu, ft)
        gbig[tag + "_gate"], gbig[tag + "_up"] = _cols_shards(dgate), _cols_shards(dup)
        gbig[tag + "_down"] = dwd.reshape((N_DEV,) + tuple(ffn1_down.shape[1:]))
    gbig["w_in"] = _cols_shards(dw_in)
    gbig["w_out"] = g["w_out"].reshape((N_DEV,) + tuple(w_out.shape[1:]))
    by_dest = [gbig[n].reshape((N_CHIPS, 2) + tuple(gbig[n].shape[1:])) for n in BIG]
    got = _swap_sibling(by_dest, "rs_sibling")
    pair = [_pair_add(place, t, s, "rs_pair_add_" + n) for n, t, s in zip(BIG, by_dest, got)]
    chip_sum = [p[0] for p in pair]
    from_chips = _swap_chips([p[1] for p in pair], "rs_chips")

    n9 = N_MOD * d
    dmod_rows = dmodv.reshape(bl + 1, n9)
    ctx_row = jnp.concatenate([dmod_rows[bl, :n_ctx_mod], jnp.zeros((n9 - n_ctx_mod,), F32)])
    summed = [ctx_row, g["norm_ffn1"], g["norm_mix"], g["norm_ffn2"], g["final_norm"], g["conv_b"], g["dt_bias"],
              g["a_row"], g["d_row"], g["ssm_norm_w"], g["cconv_b"], g["ln_w"], g["ln_b"], g["conv_w"][:5], g["cconv_w"]]
    sum_shapes = [t.shape for t in summed]
    (g4,) = _all_gather([_pack_rows([dmod_rows[:bl]] + summed, d)], "gather_small_grads", False)
    nrow_batch = bl * N_MOD
    dmod_batch = g4[:, :nrow_batch].reshape(nb, n9)
    tot = _sum_lead(g4[:, nrow_batch:], "sum_small_grads")
    (dctx, g_n1, g_nm, g_n2, g_fn, g_cb, g_dtb, g_a, g_drow, g_snw, g_ccb, g_lnw, g_lnb, g_cw, g_ccw) = _unpack_rows(tot, sum_shapes)
    dmod_all = jnp.concatenate([dmod_batch, dctx[None, :], jnp.zeros((7, n9), F32)], axis=0)

    dmod_my_cols = lax.dynamic_slice_in_dim(dmod_all, me * mcols, mcols, axis=1)
    g_w_mod = _mm(s_all, dmod_my_cols, "tn", F32, "dw_mod")[None]
    g_b_mod = _sum_lead(dmod_all.reshape(nb + 8, N_MOD, d), "db_mod").reshape(1, n9)
    ds_part = _mm(dmod_my_cols[nb:nb + 8], w_mod[0], "nt", F32, "ds_ctx")
    (g5,) = _all_gather([jnp.concatenate([ds_part[0:1], jnp.zeros((7, d), F32)], axis=0)], "gather_ds_ctx", False)
    ds_ctx = _sum_lead(g5, "sum_ds_ctx")[0]
    sg = jax.nn.sigmoid(c_ctx)
    g_c_ctx = ds_ctx * (sg * (1.0 + c_ctx * (1.0 - sg)))

    a_f, a_b = a_vals[:, :heads], a_vals[:, heads:hh]
    grads = {
        "c_ctx": [g_c_ctx], "w_mod": [g_w_mod], "b_mod": [g_b_mod],
        "norm_ffn1": [g_n1], "norm_mix": [g_nm], "norm_ffn2": [g_n2], "final_norm": [g_fn.reshape(-1)],
        "ssm_conv_w": [lax.dynamic_slice_in_dim(g_cw, me * ssm_conv_w.shape[2], ssm_conv_w.shape[2], axis=1)[None]],
        "ssm_conv_b": [g_cb],
        "dt_bias_fwd": [g_dtb[:, :heads]], "dt_bias_bwd": [g_dtb[:, heads:hh]],
        "a_log_fwd": [g_a[:, :heads] * a_f], "a_log_bwd": [g_a[:, heads:hh] * a_b],
        "ssm_d": [jnp.sum(g_drow.reshape(1, heads, HEAD_DIM), axis=2)], "ssm_norm_w": [g_snw],
        "cconv_w": [lax.dynamic_slice_in_dim(g_ccw, me * cconv_w.shape[2], cconv_w.shape[2], axis=1)[None]],
        "cconv_b": [g_ccb], "cconv_ln_w": [g_lnw], "cconv_ln_b": [g_lnb],
    }
    for n, own, others in zip(BIG, chip_sum, from_chips):
        grads[n] = [(own, "chip"), (others, 0), (others, 1), (others, 2)]

    out_g, out_d, out_m, out_v = [], [], [], []
    for n in names:
        gr, de, nm, nv = _adamw(place, wts[n], grads[n], args["m_" + n], args["v_" + n], "adamw_" + n)
        out_g.append(gr)
        out_d.append(de)
        out_m.append(nm)
        out_v.append(nv)
    return (loss, grad_x.reshape(bl, seq, d), *out_g, *out_d, *out_m, *out_v)
```

```python
import functools
import math

import jax
import jax.numpy as jnp
from jax import lax
from jax.experimental import pallas as pl
from jax.experimental.pallas import tpu as pltpu

F32 = jnp.float32
BF16 = jnp.bfloat16
MESH = pl.DeviceIdType.MESH

N_DEV = 8
N_CHIP_PEERS = 3
HEAD_DIM = 64
N_STATE = 128
SSD_GROUPS = 2
CHUNK = 128
GRID_W = 64
N_MOD = 9
EPS = 1e-6
DT_PAD = 512
DT_LANES = 128
HALO = 8
ROW_TILE = 512
FINE_ROW_TILE = 256
VMEM_LIMIT = 48 * 1024 * 1024
NEG_BIG = -1e30

ADAM_LR = 0.001
ADAM_B1 = 0.9
ADAM_B2 = 0.999
ADAM_EPS = 1e-08
ADAM_WD = 0.01
ADAM_STEP = 10


def _pick(n, prefs):
    for p in prefs:
        if n % p == 0:
            return p
    return n


MM_TILE_CAP = 2816
MM_TILE_ELEMS = 3 << 20
MM_OUT_TILE_ELEMS = 3 << 19
MM_FULL_ROWS = 1024


def _big_tile(n, cap):
    if n <= cap:
        return n
    best = 0
    for t in range(128, cap + 1, 128):
        if n % t == 0:
            best = t
    return best or n


def _cparams(ndim):
    return pltpu.CompilerParams(dimension_semantics=("arbitrary",) * ndim, vmem_limit_bytes=VMEM_LIMIT)


def _silu(v):
    return v * jax.nn.sigmoid(v)


def _mm(a, b, mode, out_dtype, name, tn=None, tm=None, extras=(), epilogue=None, outs=None):
    if mode == "tn":
        (K, M), (K2, N) = a.shape, b.shape
    elif mode == "nt":
        (M, K), (N, K2) = a.shape, b.shape
    else:
        (M, K), (K2, N) = a.shape, b.shape
    assert K == K2, (name, a.shape, b.shape)
    tm = M if M <= MM_FULL_ROWS else tm
    if tn is None:
        tn = _big_tile(N, min(MM_TILE_CAP, max(128, MM_OUT_TILE_ELEMS // (tm or 512))))
    if tm is None:
        tm = _big_tile(M, max(128, MM_OUT_TILE_ELEMS // tn))
    tk = _big_tile(K, min(MM_TILE_CAP, MM_TILE_ELEMS // max(tn, tm)))
    nk = K // tk
    ni, nj = M // tm, N // tn
    swap = nk == 1 and (K * N + M * K * nj) < (M * K + K * N * ni)
    ij = (lambda g0, g1: (g1, g0)) if swap else (lambda g0, g1: (g0, g1))
    if mode == "tn":
        a_spec = pl.BlockSpec((tk, tm), lambda g0, g1, k: (k, ij(g0, g1)[0]))
        dn = (((0,), (0,)), ((), ()))
    else:
        a_spec = pl.BlockSpec((tm, tk), lambda g0, g1, k: (ij(g0, g1)[0], k))
        dn = (((1,), (1,)), ((), ())) if mode == "nt" else (((1,), (0,)), ((), ()))
    if mode == "nt":
        b_spec = pl.BlockSpec((tn, tk), lambda g0, g1, k: (ij(g0, g1)[1], k))
    else:
        b_spec = pl.BlockSpec((tk, tn), lambda g0, g1, k: (k, ij(g0, g1)[1]))
    if outs is None:
        outs = [(tn, out_dtype)]
    nx = len(extras)

    def tile(w):
        return pl.BlockSpec((tm, w), lambda g0, g1, k: ij(g0, g1))

    def finish(acc, refs):
        vals = (acc,) if epilogue is None else epilogue(acc, *[r[...] for r in refs[:nx]])
        for o_ref, v in zip(refs[nx:], vals):
            o_ref[...] = v.astype(o_ref.dtype)

    def body(a_ref, b_ref, *refs):
        part = lax.dot_general(a_ref[...].astype(BF16), b_ref[...].astype(BF16), dn, preferred_element_type=F32)
        if nk == 1:
            finish(part, refs)
            return
        acc_ref, k = refs[-1], pl.program_id(2)
        _acc(acc_ref, k == 0, part)

        @pl.when(k == nk - 1)
        def _():
            finish(acc_ref[...], refs[:-1])

    grid = (nj, ni, nk) if swap else (ni, nj, nk)
    res = pl.pallas_call(
        body, grid=grid, in_specs=[a_spec, b_spec] + [tile(w) for _, w in extras],
        out_specs=[tile(w) for w, _ in outs],
        out_shape=[jax.ShapeDtypeStruct((M, nj * w), dt) for w, dt in outs],
        scratch_shapes=[pltpu.VMEM((tm, tn), F32)] if nk > 1 else [], name=name, compiler_params=_cparams(3),
    )(a, b, *[x for x, _ in extras])
    return res[0] if epilogue is None else res


def _ffn_tm(t):
    return _pick(t, (1024, 512, 256, 128, 64, 32, 16, 8))


def _dsilu_mul(g, u, da):
    sg = jax.nn.sigmoid(g)
    return da * u * (sg * (1.0 + g * (1.0 - sg))), da * (g * sg)


def _ffn_gu(h, wg, wu, name):
    t, d = h.shape
    nb, _, fb = wg.shape
    tm = _ffn_tm(t)

    def body(h_ref, wg_ref, wu_ref, g_ref, u_ref, a_ref):
        hv = h_ref[...]
        g = jnp.dot(hv, wg_ref[...], preferred_element_type=F32)
        u = jnp.dot(hv, wu_ref[...], preferred_element_type=F32)
        g_ref[...] = g
        u_ref[...] = u
        a_ref[...] = (_silu(g) * u).astype(a_ref.dtype)

    wspec = pl.BlockSpec((None, d, fb), lambda i, j: (j, 0, 0))
    ospec = pl.BlockSpec((None, tm, fb), lambda i, j: (j, i, 0))
    sh = jax.ShapeDtypeStruct((nb, t, fb), F32)
    return pl.pallas_call(body, grid=(t // tm, nb), in_specs=[pl.BlockSpec((tm, d), lambda i, j: (i, 0)), wspec, wspec],
                          out_specs=[ospec, ospec, ospec], out_shape=[sh, sh, jax.ShapeDtypeStruct((nb, t, fb), BF16)],
                          name=name, compiler_params=_cparams(2))(h, wg, wu)


def _ffn_contract(xs, ws, mode, name):
    nb, t, fb = xs[0].shape
    d = ws[0].shape[2] if mode == "nn" else ws[0].shape[1]
    tm = _ffn_tm(t)
    npair = len(xs)
    dn = (((1,), (0,)), ((), ())) if mode == "nn" else (((1,), (1,)), ((), ()))

    def body(*refs):
        o_ref, acc_ref = refs[2 * npair], refs[2 * npair + 1]
        j = pl.program_id(1)
        part = lax.dot_general(refs[0][...], refs[npair][...], dn, preferred_element_type=F32)
        for p in range(1, npair):
            part = part + lax.dot_general(refs[p][...], refs[npair + p][...], dn, preferred_element_type=F32)
        _acc(acc_ref, j == 0, part)

        @pl.when(j == nb - 1)
        def _():
            o_ref[...] = acc_ref[...]

    xspec = pl.BlockSpec((None, tm, fb), lambda i, j: (j, i, 0))
    wspec = pl.BlockSpec((None,) + tuple(ws[0].shape[1:]), lambda i, j: (j, 0, 0))
    return pl.pallas_call(body, grid=(t // tm, nb), in_specs=[xspec] * npair + [wspec] * npair,
                          out_specs=pl.BlockSpec((tm, d), lambda i, j: (i, 0)),
                          out_shape=jax.ShapeDtypeStruct((t, d), F32), scratch_shapes=[pltpu.VMEM((tm, d), F32)],
                          name=name, compiler_params=_cparams(2))(*xs, *ws)


def _ffn_da(do, wd, g, u, name):
    t, d = do.shape
    nb, fb, _ = wd.shape
    tm = _ffn_tm(t)

    def body(do_ref, wd_ref, g_ref, u_ref, dg_ref, du_ref):
        da = lax.dot_general(do_ref[...], wd_ref[...], (((1,), (1,)), ((), ())), preferred_element_type=F32)
        dg, du = _dsilu_mul(g_ref[...], u_ref[...], da)
        dg_ref[...] = dg.astype(dg_ref.dtype)
        du_ref[...] = du.astype(du_ref.dtype)

    bspec = pl.BlockSpec((None, tm, fb), lambda i, j: (j, i, 0))
    sh = jax.ShapeDtypeStruct((nb, t, fb), BF16)
    return pl.pallas_call(body, grid=(t // tm, nb),
                          in_specs=[pl.BlockSpec((tm, d), lambda i, j: (i, 0)),
                                    pl.BlockSpec((None, fb, d), lambda i, j: (j, 0, 0)), bspec, bspec],
                          out_specs=[bspec, bspec], out_shape=[sh, sh], name=name, compiler_params=_cparams(2),
                          )(do, wd, g, u)


def _ffn_dw(h, xs, name):
    t, d = h.shape
    nb, _, fb = xs[0].shape
    tk = _ffn_tm(t)
    nk = t // tk
    npair = len(xs)
    tn_dims = (((0,), (0,)), ((), ()))

    def body(*refs):
        h_ref = refs[0]
        k = pl.program_id(1)
        hv = h_ref[...]
        for p in range(npair):
            part = lax.dot_general(hv, refs[1 + p][...], tn_dims, preferred_element_type=F32)
            _acc(refs[1 + npair + p], k == 0, part)

    xspec = pl.BlockSpec((None, tk, fb), lambda j, k: (j, k, 0))
    ospec = pl.BlockSpec((None, d, fb), lambda j, k: (j, 0, 0))
    sh = jax.ShapeDtypeStruct((nb, d, fb), F32)
    return pl.pallas_call(body, grid=(nb, nk), in_specs=[pl.BlockSpec((tk, d), lambda j, k: (k, 0))] + [xspec] * npair,
                          out_specs=[ospec] * npair, out_shape=[sh] * npair, name=name, compiler_params=_cparams(2),
                          )(h, *xs)


def _ffn_dwd(a, do, name):
    nb, t, fb = a.shape
    d = do.shape[1]
    tk = _ffn_tm(t)

    def body(a_ref, do_ref, o_ref):
        part = lax.dot_general(a_ref[...], do_ref[...], (((0,), (0,)), ((), ())), preferred_element_type=F32)
        _acc(o_ref, pl.program_id(1) == 0, part)

    return pl.pallas_call(body, grid=(nb, t // tk),
                          in_specs=[pl.BlockSpec((None, tk, fb), lambda j, k: (j, k, 0)),
                                    pl.BlockSpec((tk, d), lambda j, k: (k, 0))],
                          out_specs=pl.BlockSpec((None, fb, d), lambda j, k: (j, 0, 0)),
                          out_shape=jax.ShapeDtypeStruct((nb, fb, d), F32), name=name, compiler_params=_cparams(2),
                          )(a, do)


class _Lay:
    def __init__(self, bl, seq, clen, d, tt=None):
        self.bl, self.seq, self.clen, self.d = bl, seq, clen, d
        self.tt = min(ROW_TILE, math.gcd(seq, bl * clen)) if tt is None else tt
        assert seq % self.tt == 0 and (bl * clen) % self.tt == 0 and self.tt % 8 == 0
        self.spb = seq // self.tt
        self.spc = clen // self.tt
        self.nsx = bl * self.spb
        self.nsc = bl * clen // self.tt
        self.ns = self.nsx + self.nsc
        self.tx = bl * seq
        self.ta = self.tx + bl * clen

    def fine(self):
        return _Lay(self.bl, self.seq, self.clen, self.d, min(FINE_ROW_TILE, self.clen))

    def mrow(self, s):
        return jnp.where(s < self.nsx, s // self.spb, self.bl)

    def first_of_row(self, s):
        return jnp.logical_or(jnp.logical_and(s < self.nsx, s % self.spb == 0), s == self.nsx)

    def seq_first(self, s):
        return jnp.where(s < self.nsx, s % self.spb == 0, (s - self.nsx) % self.spc == 0)

    def seq_last(self, s):
        return jnp.where(s < self.nsx, s % self.spb == self.spb - 1, (s - self.nsx) % self.spc == self.spc - 1)


def _tok(lay, c, cb=0, clamp=None):
    if clamp is None:
        return pl.BlockSpec((lay.tt, c), lambda j, s: (s, cb + j))
    return pl.BlockSpec((lay.tt, c), lambda j, s: (jnp.minimum(s, clamp), cb + j))


def _halo_prev(lay, c, cb=0):
    u = lay.tt // HALO
    return pl.BlockSpec((HALO, c), lambda j, s: (jnp.maximum(s * u - 1, 0), cb + j))


def _halo_next(lay, c, cb=0):
    u = lay.tt // HALO
    last = lay.ta // HALO - 1
    return pl.BlockSpec((HALO, c), lambda j, s: (jnp.minimum((s + 1) * u, last), cb + j))


def _row(lay, k, c):
    return pl.BlockSpec((None, k, c), lambda j, s: (lay.mrow(s), 0, 0))


def _glob(k, c, cb=None):
    if cb is None:
        return pl.BlockSpec((k, c), lambda j, s: (0, 0))
    return pl.BlockSpec((k, c), lambda j, s: (0, cb + j))


def _tok_call(name, body, ncb, nseg, in_specs, out_specs, out_shape, inputs, scratch=()):
    return pl.pallas_call(body, grid=(ncb, nseg), in_specs=in_specs, out_specs=out_specs, out_shape=out_shape,
                          scratch_shapes=list(scratch), name=name, compiler_params=_cparams(2))(*inputs)


def _acc(ref, first, val):
    @pl.when(first)
    def _():
        ref[...] = val

    @pl.when(jnp.logical_not(first))
    def _():
        ref[...] += val


def _norm_mod_f(x, w, sh, sc):
    y = x * lax.rsqrt(jnp.mean(x * x, axis=-1, keepdims=True) + EPS) * w
    return y * (1.0 + sc) + sh


def _norm_mod_fwd(lay, nseg, x, w, modv, ksh, name):
    d = lay.d

    def body(x_ref, w_ref, m_ref, h_ref):
        h = _norm_mod_f(x_ref[...], w_ref[...], m_ref[ksh:ksh + 1, :], m_ref[ksh + 1:ksh + 2, :])
        h_ref[...] = h.astype(h_ref.dtype)

    return _tok_call(name, body, 1, nseg, [_tok(lay, d), _glob(1, d), _row(lay, N_MOD, d)], _tok(lay, d),
                     jax.ShapeDtypeStruct((nseg * lay.tt, d), BF16), (x, w, modv))


def _norm_mod_bwd(lay, nseg, nres, x, w, modv, ksh, dh, dres, name):
    d = lay.d
    nrow = lay.bl + (1 if nseg > lay.nsx else 0)

    def body(x_ref, w_ref, m_ref, dh_ref, dres_ref, dx_ref, dw_ref, dm_ref):
        s = pl.program_id(1)
        _, vjp = jax.vjp(_norm_mod_f, x_ref[...], w_ref[...], m_ref[ksh:ksh + 1, :], m_ref[ksh + 1:ksh + 2, :])
        dx, dw, dsh, dsc = vjp(dh_ref[...])
        res = jnp.where(s < nres, dres_ref[...], 0.0)
        dx_ref[...] = dx + res
        _acc(dw_ref, s == 0, dw)
        _acc(dm_ref, lay.first_of_row(s), jnp.concatenate([dsh, dsc], axis=0))

    return _tok_call(
        name, body, 1, nseg,
        [_tok(lay, d), _glob(1, d), _row(lay, N_MOD, d), _tok(lay, d), _tok(lay, d, clamp=nres - 1)],
        [_tok(lay, d), _glob(1, d), _row(lay, 2, d)],
        [jax.ShapeDtypeStruct((nseg * lay.tt, d), F32), jax.ShapeDtypeStruct((1, d), F32),
         jax.ShapeDtypeStruct((nrow, 2, d), F32)],
        (x, w, modv, dh, dres))


def _resid_fwd(lay, nseg, x, o, modv, kg, coef, name):
    d = lay.d

    def body(x_ref, o_ref, m_ref, y_ref):
        y_ref[...] = x_ref[...] + (coef * m_ref[kg:kg + 1, :]) * o_ref[...]

    return _tok_call(name, body, 1, nseg, [_tok(lay, d), _tok(lay, d), _row(lay, N_MOD, d)], _tok(lay, d),
                     jax.ShapeDtypeStruct((nseg * lay.tt, d), F32), (x, o, modv))


def _resid_bwd(lay, nseg, dy, o, modv, kg, coef, name):
    d = lay.d
    nrow = lay.bl + (1 if nseg > lay.nsx else 0)

    def body(dy_ref, o_ref, m_ref, do_ref, dg_ref):
        s = pl.program_id(1)
        dy = dy_ref[...]
        do_ref[...] = (dy * (coef * m_ref[kg:kg + 1, :])).astype(do_ref.dtype)
        _acc(dg_ref, lay.first_of_row(s), jnp.sum(dy * o_ref[...], axis=0, keepdims=True) * coef)

    return _tok_call(name, body, 1, nseg, [_tok(lay, d), _tok(lay, d), _row(lay, N_MOD, d)],
                     [_tok(lay, d), _row(lay, 1, d)],
                     [jax.ShapeDtypeStruct((nseg * lay.tt, d), BF16), jax.ShapeDtypeStruct((nrow, 1, d), F32)],
                     (dy, o, modv))


def _final_loss(lay, x, wf, target, name):
    d = lay.d

    def body(x_ref, w_ref, t_ref, loss_ref, dx_ref, dw_ref):
        s = pl.program_id(1)

        def f(xv, wv):
            return xv * lax.rsqrt(jnp.mean(xv * xv, axis=-1, keepdims=True) + EPS) * wv

        y, vjp = jax.vjp(f, x_ref[...], w_ref[...])
        err = y - t_ref[...]
        part = 0.5 * jnp.sum(jnp.sum(err * err, axis=-1, keepdims=True), axis=0, keepdims=True) / d
        dx, dw = vjp(err / d)
        dx_ref[...] = dx
        _acc(loss_ref, s == 0, part)
        _acc(dw_ref, s == 0, dw)

    return _tok_call(name, body, 1, lay.nsx, [_tok(lay, d), _glob(1, d), _tok(lay, d)],
                     [_glob(1, 1), _tok(lay, d), _glob(1, d)],
                     [jax.ShapeDtypeStruct((1, 1), F32), jax.ShapeDtypeStruct((lay.tx, d), F32),
                      jax.ShapeDtypeStruct((1, d), F32)], (x, wf, target))


class _Mix:
    def __init__(self, d, heads):
        self.d_ssm = d
        self.d_conv = d
        self.heads = heads
        assert heads * HEAD_DIM == d and heads % (2 * SSD_GROUPS) == 0 and 2 * heads <= DT_LANES
        self.gn = SSD_GROUPS * N_STATE
        self.xw = d + 2 * self.gn
        self.off_x = d
        self.off_dt = d + self.xw
        self.off_glu = self.off_dt + DT_PAD
        self.pw = self.off_glu + 2 * d
        self.off_dt_ref = d + self.xw
        self.off_glu_ref = self.off_dt_ref + 2 * heads
        self.cc = _pick(self.xw, (512, 256, 128))


def _conv5_fwd(lay, mx, proj, cw, cb, name):
    c, tt = mx.cc, lay.tt
    cb0 = mx.off_x // c
    assert mx.off_x % c == 0

    def body(prev_ref, cur_ref, next_ref, w_ref, b_ref, pre_ref, act_ref, ext_ref):
        s = pl.program_id(1)
        ext_ref[0:HALO, :] = jnp.where(lay.seq_first(s), 0.0, prev_ref[...])
        ext_ref[HALO:HALO + tt, :] = cur_ref[...]
        ext_ref[HALO + tt:, :] = jnp.where(lay.seq_last(s), 0.0, next_ref[...])
        acc = jnp.zeros((tt, c), F32) + b_ref[...]
        for k in range(5):
            acc = acc + w_ref[k:k + 1, :] * ext_ref[pl.ds(HALO + k - 2, tt), :]
        pre_ref[...] = acc
        act_ref[...] = _silu(acc)

    sh = jax.ShapeDtypeStruct((lay.ta, mx.xw), F32)
    return _tok_call(name, body, mx.xw // c, lay.ns,
                     [_halo_prev(lay, c, cb0), _tok(lay, c, cb0), _halo_next(lay, c, cb0), _glob(8, c, 0), _glob(1, c, 0)],
                     [_tok(lay, c), _tok(lay, c)], [sh, sh], (proj, proj, proj, cw, cb),
                     scratch=[pltpu.VMEM((tt + 2 * HALO, c), F32)])


def _conv5_bwd(lay, mx, proj, pre, dact_f, dact_b, cw, name):
    c, tt = mx.cc, lay.tt
    cb0 = mx.off_x // c

    def dsilu(p):
        sg = jax.nn.sigmoid(p)
        return sg * (1.0 + p * (1.0 - sg))

    def body(xp_ref, xc_ref, xn_ref, pp_ref, pc_ref, pn_ref, fp_ref, fc_ref, fn_ref, bp_ref, bc_ref, bn_ref, w_ref,
             dx_ref, dw_ref, db_ref, extx_ref, extd_ref):
        s = pl.program_id(1)
        first, last = lay.seq_first(s), lay.seq_last(s)
        dcur = (fc_ref[...] + bc_ref[...]) * dsilu(pc_ref[...])
        extd_ref[0:HALO, :] = jnp.where(first, 0.0, (fp_ref[...] + bp_ref[...]) * dsilu(pp_ref[...]))
        extd_ref[HALO:HALO + tt, :] = dcur
        extd_ref[HALO + tt:, :] = jnp.where(last, 0.0, (fn_ref[...] + bn_ref[...]) * dsilu(pn_ref[...]))
        extx_ref[0:HALO, :] = jnp.where(first, 0.0, xp_ref[...])
        extx_ref[HALO:HALO + tt, :] = xc_ref[...]
        extx_ref[HALO + tt:, :] = jnp.where(last, 0.0, xn_ref[...])
        dx = jnp.zeros((tt, c), F32)
        rows = []
        for k in range(5):
            dx = dx + w_ref[k:k + 1, :] * extd_ref[pl.ds(HALO - (k - 2), tt), :]
            rows.append(jnp.sum(dcur * extx_ref[pl.ds(HALO + k - 2, tt), :], axis=0, keepdims=True))
        dx_ref[...] = dx.astype(dx_ref.dtype)
        rows.append(jnp.zeros((3, c), F32))
        _acc(dw_ref, s == 0, jnp.concatenate(rows, axis=0))
        _acc(db_ref, s == 0, jnp.sum(dcur, axis=0, keepdims=True))

    three = lambda cbx: [_halo_prev(lay, c, cbx), _tok(lay, c, cbx), _halo_next(lay, c, cbx)]
    ext = pltpu.VMEM((tt + 2 * HALO, c), F32)
    return _tok_call(name, body, mx.xw // c, lay.ns,
                     three(cb0) + three(0) + three(0) + three(0) + [_glob(8, c, 0)],
                     [_tok(lay, c), _glob(8, c, 0), _glob(1, c, 0)],
                     [jax.ShapeDtypeStruct((lay.ta, mx.xw), BF16), jax.ShapeDtypeStruct((8, mx.xw), F32),
                      jax.ShapeDtypeStruct((1, mx.xw), F32)],
                     (proj, proj, proj, pre, pre, pre, dact_f, dact_f, dact_f, dact_b, dact_b, dact_b, cw),
                     scratch=[ext, ext])


def _softplus(v):
    return jnp.maximum(v, 0.0) + jnp.log1p(jnp.exp(-jnp.abs(v)))


def _dt_fwd(lay, mx, proj, bias, name):
    cb = mx.off_dt // DT_LANES

    def body(p_ref, b_ref, dt_ref):
        dt_ref[...] = _softplus(p_ref[...] + b_ref[...])

    return _tok_call(name, body, 1, lay.ns, [_tok(lay, DT_LANES, cb), _glob(1, DT_LANES)], _tok(lay, DT_LANES),
                     jax.ShapeDtypeStruct((lay.ta, DT_LANES), F32), (proj, bias))


def _dt_bwd(lay, mx, proj, bias, parts, name):
    cb = mx.off_dt // DT_LANES

    def body(p_ref, b_ref, a_ref, b2_ref, c_ref, d_ref, dp_ref, db_ref):
        s = pl.program_id(1)
        ddt = (a_ref[...] + b2_ref[...]) + (c_ref[...] + d_ref[...])
        draw = ddt * jax.nn.sigmoid(p_ref[...] + b_ref[...])
        dp_ref[:, 0:DT_LANES] = draw.astype(dp_ref.dtype)
        dp_ref[:, DT_LANES:] = jnp.zeros((lay.tt, DT_PAD - DT_LANES), dp_ref.dtype)
        _acc(db_ref, s == 0, jnp.sum(draw, axis=0, keepdims=True))

    t = _tok(lay, DT_LANES)
    return _tok_call(name, body, 1, lay.ns, [_tok(lay, DT_LANES, cb), _glob(1, DT_LANES), t, t, t, t],
                     [_tok(lay, DT_PAD), _glob(1, DT_LANES)],
                     [jax.ShapeDtypeStruct((lay.ta, DT_PAD), BF16), jax.ShapeDtypeStruct((1, DT_LANES), F32)],
                     (proj, bias) + tuple(parts))


def _scan_mask(rev):
    r = lax.broadcasted_iota(jnp.int32, (CHUNK, CHUNK), 0)
    c = lax.broadcasted_iota(jnp.int32, (CHUNK, CHUNK), 1)
    return (r <= c) if rev else (r >= c)


def _split_bf16(x):
    hi = x.astype(BF16)
    return hi, (x - hi.astype(F32)).astype(BF16)


@functools.partial(jax.custom_vjp, nondiff_argnums=(0,))
def _cum_cols(rev, x):
    m = _scan_mask(rev).astype(BF16)
    hi, lo = _split_bf16(x)
    return jnp.dot(m, hi, preferred_element_type=F32) + jnp.dot(m, lo, preferred_element_type=F32)


_cum_cols.defvjp(lambda rev, x: (_cum_cols(rev, x), None), lambda rev, _, g: (_cum_cols(not rev, g),))


@functools.partial(jax.custom_vjp, nondiff_argnums=(0,))
def _cum_rows(rev, x):
    m = _scan_mask(not rev).astype(BF16)
    hi, lo = _split_bf16(x)
    return jnp.dot(hi, m, preferred_element_type=F32) + jnp.dot(lo, m, preferred_element_type=F32)


_cum_rows.defvjp(lambda rev, x: (_cum_rows(rev, x), None), lambda rev, _, g: (_cum_rows(not rev, g),))


@functools.partial(jax.custom_vjp, nondiff_argnums=(1,))
def _take_col(x, k):
    return x[:, k:k + 1]


def _take_col_bwd(k, _, g):
    lane = lax.broadcasted_iota(jnp.int32, (g.shape[0], DT_LANES), 1)
    return (jnp.where(lane == k, g, 0.0),)


_take_col.defvjp(lambda x, k: (x[:, k:k + 1], None), _take_col_bwd)


@functools.partial(jax.custom_vjp, nondiff_argnums=(1,))
def _take_row(x, k):
    return x[k:k + 1, :]


def _take_row_bwd(k, _, g):
    sub = lax.broadcasted_iota(jnp.int32, (DT_LANES, g.shape[1]), 0)
    return (jnp.where(sub == k, g, 0.0),)


_take_row.defvjp(lambda x, k: (x[k:k + 1, :], None), _take_row_bwd)


def _ssd_chunk(xh_pairs, bcs, ccs, dtc, dtr, a_row, a_col, st_pairs, *, rev, heads, col0):
    cs_c, cs_r, tot, scores = _ssd_shared(bcs, ccs, dtc, dtr, a_row, a_col, rev=rev)
    ppg = heads // (2 * SSD_GROUPS)
    ys, sts = [], []
    for g in range(SSD_GROUPS):
        y, st = _ssd_group(xh_pairs[g * ppg:(g + 1) * ppg], bcs[g], ccs[g], st_pairs[g], cs_c, cs_r, tot, dtc,
                           scores[g], rev=rev, col=col0 + 2 * ppg * g)
        ys.append(y)
        sts.append(st)
    return ys, sts


_NT = (((1,), (1,)), ((), ()))
_TN = (((0,), (0,)), ((), ()))


def _ssd_shared(bcs, ccs, dtc, dtr, a_row, a_col, *, rev):
    da_c = dtc * a_row
    cs_c = _cum_cols(rev, da_c)
    cs_r = _cum_rows(rev, dtr * a_col)
    tot = jnp.sum(da_c, axis=0, keepdims=True)
    scores = [lax.dot_general(ccs[g].astype(BF16), bcs[g].astype(BF16), _NT, preferred_element_type=F32)
              for g in range(SSD_GROUPS)]
    return cs_c, cs_r, tot, scores


def _ssd_group(xh_pairs, bc, cc, st, cs_c, cs_r, tot, dtc, score, *, rev, col):
    n = CHUNK
    mask = _scan_mask(rev)
    lane = lax.broadcasted_iota(jnp.int32, (n, DT_LANES), 1)
    sub = lax.broadcasted_iota(jnp.int32, (DT_LANES, n), 0)
    lane1 = lax.broadcasted_iota(jnp.int32, (1, DT_LANES), 1)
    left = lax.broadcasted_iota(jnp.int32, (n, 2 * HEAD_DIM), 1) < HEAD_DIM
    top = lax.broadcasted_iota(jnp.int32, (2 * HEAD_DIM, 1), 0) < HEAD_DIM
    xs_all, wst_all, ecs_all, edec_all, y_diag = [], [], [], [], []
    for p, xh in enumerate(xh_pairs):
        per = []
        for c in (col + 2 * p, col + 2 * p + 1):
            csv = jnp.sum(jnp.where(lane == c, cs_c, 0.0), axis=1, keepdims=True)
            csr = jnp.sum(jnp.where(sub == c, cs_r, 0.0), axis=0, keepdims=True)
            dtv = jnp.sum(jnp.where(lane == c, dtc, 0.0), axis=1, keepdims=True)
            tv = jnp.sum(jnp.where(lane1 == c, tot, 0.0), axis=1, keepdims=True)
            m = score * jnp.exp(jnp.where(mask, csv - csr, NEG_BIG))
            per.append((csv, dtv, tv, m))
        (cs1, dt1, t1, m1), (cs2, dt2, t2, m2) = per
        xs = xh * jnp.where(left, dt1, dt2)
        both = jnp.dot(jnp.concatenate([m1, m2], axis=0).astype(BF16), xs.astype(BF16), preferred_element_type=F32)
        y_diag.append(jnp.where(left, both[:n], both[n:]))
        xs_all.append(xs)
        ecs_all.append(jnp.where(left, jnp.exp(cs1), jnp.exp(cs2)))
        wst_all.append(jnp.where(left, jnp.exp(t1 - cs1), jnp.exp(t2 - cs2)))
        edec_all.append(jnp.where(top, jnp.exp(t1), jnp.exp(t2)))
    cat = lambda parts, axis: parts[0] if len(parts) == 1 else jnp.concatenate(parts, axis=axis)
    xs, wst, ecs = cat(xs_all, 1), cat(wst_all, 1), cat(ecs_all, 1)
    y_off = lax.dot_general(cc.astype(BF16), st.astype(BF16), _NT, preferred_element_type=F32) * ecs
    cst = lax.dot_general((xs * wst).astype(BF16), bc.astype(BF16), _TN, preferred_element_type=F32)
    return cat(y_diag, 1) + y_off, st * cat(edec_all, 0) + cst


class _Scan:
    def __init__(self, lay, rev):
        self.ncx, self.ncc, self.bl, self.rev = lay.seq // CHUNK, lay.clen // CHUNK, lay.bl, rev
        self.nct = self.ncx + self.ncc

    def chunk(self, b, pos):
        kc = (self.ncc - 1 - pos) if self.rev else pos
        kx = (self.ncx - 1 - (pos - self.ncc)) if self.rev else (pos - self.ncc)
        return jnp.where(pos < self.ncc, self.bl * self.ncx + b * self.ncc + kc, b * self.ncx + kx)


def _ssd_io(mx, x_ref, st_src):
    np_ = mx.heads // 2
    d = mx.d_ssm
    xh = [x_ref[:, 128 * p:128 * (p + 1)] for p in range(np_)]
    bcs = [x_ref[:, d + N_STATE * g:d + N_STATE * (g + 1)] for g in range(SSD_GROUPS)]
    ccs = [x_ref[:, d + mx.gn + N_STATE * g:d + mx.gn + N_STATE * (g + 1)] for g in range(SSD_GROUPS)]
    gw = d // SSD_GROUPS
    sts = [st_src[gw * g:gw * (g + 1), :] for g in range(SSD_GROUPS)]
    return xh, bcs, ccs, sts


def _ssd_fwd(lay, mx, xbc, dt, dtt, a_row, a_col, rev, name):
    sc = _Scan(lay, rev)
    col0 = mx.heads if rev else 0
    hp = mx.heads * HEAD_DIM

    def body(x_ref, dt_ref, dtt_ref, ar_ref, ac_ref, y_ref, hp_ref, st_ref):
        @pl.when(pl.program_id(1) == 0)
        def _():
            st_ref[...] = jnp.zeros_like(st_ref)

        hp_ref[...] = st_ref[...]
        xh, bcs, ccs, sts = _ssd_io(mx, x_ref, st_ref)
        ys, new = _ssd_chunk(xh, bcs, ccs, dt_ref[...], dtt_ref[...], ar_ref[...], ac_ref[...], sts,
                             rev=rev, heads=mx.heads, col0=col0)
        gw = mx.d_ssm // SSD_GROUPS
        for g in range(SSD_GROUPS):
            y_ref[:, gw * g:gw * (g + 1)] = ys[g]
            st_ref[gw * g:gw * (g + 1), :] = new[g]

    ch = sc.chunk
    return pl.pallas_call(
        body, grid=(lay.bl, sc.nct),
        in_specs=[pl.BlockSpec((CHUNK, mx.xw), lambda b, i: (ch(b, i), 0)),
                  pl.BlockSpec((CHUNK, DT_LANES), lambda b, i: (ch(b, i), 0)),
                  pl.BlockSpec((DT_LANES, CHUNK), lambda b, i: (0, ch(b, i))),
                  pl.BlockSpec((1, DT_LANES), lambda b, i: (0, 0)),
                  pl.BlockSpec((DT_LANES, 1), lambda b, i: (0, 0))],
        out_specs=[pl.BlockSpec((CHUNK, mx.d_ssm), lambda b, i: (ch(b, i), 0)),
                   pl.BlockSpec((hp, N_STATE), lambda b, i: (b * sc.nct + i, 0))],
        out_shape=[jax.ShapeDtypeStruct((lay.ta, mx.d_ssm), F32),
                   jax.ShapeDtypeStruct((lay.bl * sc.nct * hp, N_STATE), F32)],
        scratch_shapes=[pltpu.VMEM((hp, N_STATE), F32)], name=name, compiler_params=_cparams(2),
    )(xbc, dt, dtt, a_row, a_col)


def _ssd_bwd(lay, mx, xbc, dt, dtt, a_row, a_col, hprev, dy, dskip, rev, name):
    sc = _Scan(lay, rev)
    col0 = mx.heads if rev else 0
    hp = mx.heads * HEAD_DIM
    np_ = mx.heads // 2
    d = mx.d_ssm
    with_skip = dskip is not None

    def body(*refs):
        if with_skip:
            x_ref, dt_ref, dtt_ref, ar_ref, ac_ref, hp_ref, dy_ref, sk_ref = refs[:8]
            rest = refs[8:]
        else:
            x_ref, dt_ref, dtt_ref, ar_ref, ac_ref, hp_ref, dy_ref = refs[:7]
            rest = refs[7:]
        dx_ref, ddc_ref, ddr_ref, dar_ref, dac_ref, ds_ref = rest
        b, i = pl.program_id(0), pl.program_id(1)

        @pl.when(i == 0)
        def _():
            ds_ref[...] = jnp.zeros_like(ds_ref)

        xh, bcs, ccs, sts = _ssd_io(mx, x_ref, hp_ref)
        dtc = dt_ref[...]
        shared, vjp_shared = jax.vjp(functools.partial(_ssd_shared, rev=rev), bcs, ccs, dtc, dtt_ref[...],
                                     ar_ref[...], ac_ref[...])
        cs_c, cs_r, tot, scores = shared
        plus = lambda acc, v: v if acc is None else acc + v
        d_cs_c = d_cs_r = d_tot = ddc = None
        d_scores, dbc, dcc = [], [], []
        ppg = np_ // SSD_GROUPS
        gw = d // SSD_GROUPS
        for g in range(SSD_GROUPS):
            dyg = dy_ref[:, gw * g:gw * (g + 1)]
            fn = functools.partial(_ssd_group, rev=rev, col=col0 + 2 * ppg * g)
            _, vjp = jax.vjp(fn, xh[g * ppg:(g + 1) * ppg], bcs[g], ccs[g], sts[g], cs_c, cs_r, tot, dtc, scores[g])
            dxh, dbc_g, dcc_g, dst, dcs_c_g, dcs_r_g, dtot_g, ddc_g, dsc_g = vjp((dyg, ds_ref[gw * g:gw * (g + 1), :]))
            for q in range(ppg):
                p = g * ppg + q
                v = dxh[q]
                if with_skip:
                    v = v + dyg[:, 128 * q:128 * (q + 1)] * sk_ref[:, 128 * p:128 * (p + 1)]
                dx_ref[:, 128 * p:128 * (p + 1)] = v
            ds_ref[gw * g:gw * (g + 1), :] = dst
            d_cs_c, d_cs_r, d_tot, ddc = plus(d_cs_c, dcs_c_g), plus(d_cs_r, dcs_r_g), plus(d_tot, dtot_g), plus(ddc, ddc_g)
            d_scores.append(dsc_g)
            dbc.append(dbc_g)
            dcc.append(dcc_g)
        dbc_s, dcc_s, ddc_s, ddr, dar, dac = vjp_shared((d_cs_c, d_cs_r, d_tot, d_scores))
        ddc = ddc + ddc_s
        dbc = [dbc[g] + dbc_s[g] for g in range(SSD_GROUPS)]
        dcc = [dcc[g] + dcc_s[g] for g in range(SSD_GROUPS)]
        for g in range(SSD_GROUPS):
            dx_ref[:, d + N_STATE * g:d + N_STATE * (g + 1)] = dbc[g]
            dx_ref[:, d + mx.gn + N_STATE * g:d + mx.gn + N_STATE * (g + 1)] = dcc[g]
        ddc_ref[...] = ddc
        ddr_ref[...] = ddr
        first = jnp.logical_and(b == 0, i == 0)
        _acc(dar_ref, first, dar)
        _acc(dac_ref, first, dac)

    ch = lambda b, i: sc.chunk(b, sc.nct - 1 - i)
    in_specs = [pl.BlockSpec((CHUNK, mx.xw), lambda b, i: (ch(b, i), 0)),
                pl.BlockSpec((CHUNK, DT_LANES), lambda b, i: (ch(b, i), 0)),
                pl.BlockSpec((DT_LANES, CHUNK), lambda b, i: (0, ch(b, i))),
                pl.BlockSpec((1, DT_LANES), lambda b, i: (0, 0)),
                pl.BlockSpec((DT_LANES, 1), lambda b, i: (0, 0)),
                pl.BlockSpec((hp, N_STATE), lambda b, i: (b * sc.nct + sc.nct - 1 - i, 0)),
                pl.BlockSpec((CHUNK, d), lambda b, i: (ch(b, i), 0))]
    inputs = [xbc, dt, dtt, a_row, a_col, hprev, dy]
    if with_skip:
        in_specs.append(pl.BlockSpec((1, d), lambda b, i: (0, 0)))
        inputs.append(dskip)
    return pl.pallas_call(
        body, grid=(lay.bl, sc.nct), in_specs=in_specs,
        out_specs=[pl.BlockSpec((CHUNK, mx.xw), lambda b, i: (ch(b, i), 0)),
                   pl.BlockSpec((CHUNK, DT_LANES), lambda b, i: (ch(b, i), 0)),
                   pl.BlockSpec((DT_LANES, CHUNK), lambda b, i: (0, ch(b, i))),
                   pl.BlockSpec((1, DT_LANES), lambda b, i: (0, 0)),
                   pl.BlockSpec((DT_LANES, 1), lambda b, i: (0, 0))],
        out_shape=[jax.ShapeDtypeStruct((lay.ta, mx.xw), F32), jax.ShapeDtypeStruct((lay.ta, DT_LANES), F32),
                   jax.ShapeDtypeStruct((DT_LANES, lay.ta), F32), jax.ShapeDtypeStruct((1, DT_LANES), F32),
                   jax.ShapeDtypeStruct((DT_LANES, 1), F32)],
        scratch_shapes=[pltpu.VMEM((hp, N_STATE), F32)], name=name, compiler_params=_cparams(2),
    )(*inputs)


def _gate_f(yf, yb, xh, z, drow, nw):
    dd = yf.shape[-1]
    half = dd // SSD_GROUPS
    yz = (yf + yb + drow * xh) * _silu(z)
    lo = lax.broadcasted_iota(jnp.int32, yz.shape, 1) < half
    sq = yz * yz
    ms1 = jnp.sum(jnp.where(lo, sq, 0.0), axis=-1, keepdims=True) / half
    ms2 = jnp.sum(jnp.where(lo, 0.0, sq), axis=-1, keepdims=True) / half
    return yz * jnp.where(lo, lax.rsqrt(ms1 + EPS), lax.rsqrt(ms2 + EPS)) * nw


def _gate_fwd(lay, mx, yf, yb, xbc, proj, drow, nw, name):
    d = mx.d_ssm

    def body(yf_ref, yb_ref, xh_ref, z_ref, d_ref, w_ref, o_ref):
        o_ref[...] = _gate_f(yf_ref[...], yb_ref[...], xh_ref[...], z_ref[...], d_ref[...], w_ref[...]).astype(o_ref.dtype)

    t = _tok(lay, d)
    return _tok_call(name, body, 1, lay.nsx, [t, t, t, t, _glob(1, d), _glob(1, d)], t,
                     jax.ShapeDtypeStruct((lay.tx, d), BF16), (yf, yb, xbc, proj, drow, nw))


def _gate_bwd(lay, mx, yf, yb, xbc, proj, drow, nw, dcat, name):
    d = mx.d_ssm
    nsx = lay.nsx

    def body(yf_ref, yb_ref, xh_ref, z_ref, d_ref, w_ref, dc_ref, dy_ref, dz_ref, dd_ref, dw_ref):
        s = pl.program_id(1)

        @pl.when(s < nsx)
        def _():
            _, vjp = jax.vjp(_gate_f, yf_ref[...], yb_ref[...], xh_ref[...], z_ref[...], d_ref[...], w_ref[...])
            dyf, _, _, dz, dd, dw = vjp(dc_ref[...])
            dy_ref[...] = dyf
            dz_ref[...] = dz.astype(dz_ref.dtype)
            _acc(dd_ref, s == 0, dd)
            _acc(dw_ref, s == 0, dw)

        @pl.when(s >= nsx)
        def _():
            dy_ref[...] = jnp.zeros_like(dy_ref)
            dz_ref[...] = jnp.zeros_like(dz_ref)

    t = _tok(lay, d)
    return _tok_call(name, body, 1, lay.ns, [t, t, t, t, _glob(1, d), _glob(1, d), _tok(lay, d, clamp=nsx - 1)],
                     [t, t, _glob(1, d), _glob(1, d)],
                     [jax.ShapeDtypeStruct((lay.ta, d), F32), jax.ShapeDtypeStruct((lay.ta, d), BF16),
                      jax.ShapeDtypeStruct((1, d), F32), jax.ShapeDtypeStruct((1, d), F32)],
                     (yf, yb, xbc, proj, drow, nw, dcat))


def _glu_fwd(lay, mx, proj, name):
    d = mx.d_conv
    c = math.gcd(mx.off_glu, d)
    cb = mx.off_glu // c

    def body(a_ref, b_ref, o_ref):
        o_ref[...] = a_ref[...] * jax.nn.sigmoid(b_ref[...])

    return _tok_call(name, body, d // c, lay.nsx, [_tok(lay, c, cb), _tok(lay, c, cb + d // c)], _tok(lay, c),
                     jax.ShapeDtypeStruct((lay.tx, d), F32), (proj, proj))


def _glu_bwd(lay, mx, proj, du, name):
    d = mx.d_conv
    c = math.gcd(mx.off_glu, d)
    cb = mx.off_glu // c
    nsx = lay.nsx

    def body(a_ref, b_ref, du_ref, da_ref, db_ref):
        s = pl.program_id(1)

        @pl.when(s < nsx)
        def _():
            sg = jax.nn.sigmoid(b_ref[...])
            duv = du_ref[...]
            da_ref[...] = (duv * sg).astype(da_ref.dtype)
            db_ref[...] = (duv * a_ref[...] * sg * (1.0 - sg)).astype(db_ref.dtype)

        @pl.when(s >= nsx)
        def _():
            da_ref[...] = jnp.zeros_like(da_ref)
            db_ref[...] = jnp.zeros_like(db_ref)

    sh = jax.ShapeDtypeStruct((lay.ta, d), BF16)
    return _tok_call(name, body, d // c, lay.ns,
                     [_tok(lay, c, cb), _tok(lay, c, cb + d // c), _tok(lay, c, clamp=nsx - 1)],
                     [_tok(lay, c), _tok(lay, c)], [sh, sh], (proj, proj, du))


def _axial(lay, mx, u, dy, cw, cb, name):
    d, seq = mx.d_conv, lay.seq
    kw = cw.shape[0]
    pad = kw // 2
    c = _pick(d // 2, (256, 128))
    ncb = d // c
    zpad = GRID_W * pad
    zpad = -(-zpad // 8) * 8
    backward = dy is not None

    def shifted(ext_ref, off):
        return ext_ref[pl.ds(zpad + off, seq), :]

    def valid_row(off):
        col = lax.broadcasted_iota(jnp.int32, (seq, c), 0) % GRID_W
        return jnp.logical_and(col + off >= 0, col + off < GRID_W)

    def fill(ext_ref, v):
        ext_ref[0:zpad, :] = jnp.zeros((zpad, c), F32)
        ext_ref[zpad:zpad + seq, :] = v
        ext_ref[zpad + seq:, :] = jnp.zeros((zpad, c), F32)

    def conv(ext_ref, w_ref, is_row, sign):
        acc = jnp.zeros((seq, c), F32)
        for k in range(kw):
            off = sign * ((k - pad) if is_row else GRID_W * (k - pad))
            v = shifted(ext_ref, off)
            if is_row:
                v = jnp.where(valid_row(off), v, 0.0)
            acc = acc + w_ref[k:k + 1, :] * v
        return acc

    def fwd_body(u_ref, w_ref, b_ref, o_ref, ext_ref):
        j = pl.program_id(0)
        fill(ext_ref, u_ref[...])

        @pl.when(j < ncb // 2)
        def _():
            o_ref[...] = conv(ext_ref, w_ref, True, 1) + b_ref[...]

        @pl.when(j >= ncb // 2)
        def _():
            o_ref[...] = conv(ext_ref, w_ref, False, 1) + b_ref[...]

    def bwd_body(u_ref, dy_ref, w_ref, du_ref, dw_ref, db_ref, extu_ref, extd_ref):
        j, b = pl.program_id(0), pl.program_id(1)
        dyv = dy_ref[...]
        fill(extu_ref, u_ref[...])
        fill(extd_ref, dyv)

        def grads(is_row):
            du_ref[...] = conv(extd_ref, w_ref, is_row, -1)
            rows = []
            for k in range(kw):
                off = (k - pad) if is_row else GRID_W * (k - pad)
                v = shifted(extu_ref, off)
                if is_row:
                    v = jnp.where(valid_row(off), v, 0.0)
                rows.append(jnp.sum(dyv * v, axis=0, keepdims=True))
            _acc(dw_ref, b == 0, jnp.concatenate(rows, axis=0))

        @pl.when(j < ncb // 2)
        def _():
            grads(True)

        @pl.when(j >= ncb // 2)
        def _():
            grads(False)

        _acc(db_ref, b == 0, jnp.sum(dyv, axis=0, keepdims=True))

    seq_spec = pl.BlockSpec((seq, c), lambda j, b: (b, j))
    w_spec = pl.BlockSpec((kw, c), lambda j, b: (0, j))
    b_spec = pl.BlockSpec((1, c), lambda j, b: (0, j))
    ext = pltpu.VMEM((seq + 2 * zpad, c), F32)
    if not backward:
        return pl.pallas_call(fwd_body, grid=(ncb, lay.bl), in_specs=[seq_spec, w_spec, b_spec], out_specs=seq_spec,
                              out_shape=jax.ShapeDtypeStruct((lay.tx, d), F32), scratch_shapes=[ext], name=name,
                              compiler_params=_cparams(2))(u, cw, cb)
    return pl.pallas_call(bwd_body, grid=(ncb, lay.bl), in_specs=[seq_spec, seq_spec, w_spec],
                          out_specs=[seq_spec, w_spec, b_spec],
                          out_shape=[jax.ShapeDtypeStruct((lay.tx, d), F32), jax.ShapeDtypeStruct((kw, d), F32),
                                     jax.ShapeDtypeStruct((1, d), F32)],
                          scratch_shapes=[ext, ext], name=name, compiler_params=_cparams(2))(u, dy, cw)


def _ln_silu_f(u, w, b):
    mu = jnp.mean(u, axis=-1, keepdims=True)
    var = jnp.mean(jnp.square(u - mu), axis=-1, keepdims=True)
    return _silu((u - mu) * lax.rsqrt(var + EPS) * w + b)


def _ln_fwd(lay, mx, u, w, b, name):
    d = mx.d_conv

    def body(u_ref, w_ref, b_ref, o_ref):
        o_ref[...] = _ln_silu_f(u_ref[...], w_ref[...], b_ref[...]).astype(o_ref.dtype)

    return _tok_call(name, body, 1, lay.nsx, [_tok(lay, d), _glob(1, d), _glob(1, d)], _tok(lay, d),
                     jax.ShapeDtypeStruct((lay.tx, d), BF16), (u, w, b))


def _ln_bwd(lay, mx, u, w, b, dcat, name):
    d = mx.d_conv

    def body(u_ref, w_ref, b_ref, dc_ref, du_ref, dw_ref, db_ref):
        s = pl.program_id(1)
        _, vjp = jax.vjp(_ln_silu_f, u_ref[...], w_ref[...], b_ref[...])
        du, dw, db = vjp(dc_ref[...])
        du_ref[...] = du
        _acc(dw_ref, s == 0, dw)
        _acc(db_ref, s == 0, db)

    return _tok_call(name, body, 1, lay.nsx, [_tok(lay, d), _glob(1, d), _glob(1, d), _tok(lay, d, 1)],
                     [_tok(lay, d), _glob(1, d), _glob(1, d)],
                     [jax.ShapeDtypeStruct((lay.tx, d), F32), jax.ShapeDtypeStruct((1, d), F32),
                      jax.ShapeDtypeStruct((1, d), F32)], (u, w, b, dcat))


def _gate_tile(dff):
    return dff // 2 if (dff // 2) % 128 == 0 else dff


def _interleave(wg, wu, ft):
    return jnp.concatenate([t[:, k:k + ft] for k in range(0, wg.shape[1], ft) for t in (wg, wu)], axis=1)


def _deinterleave(wgu, ft):
    n = wgu.shape[1]
    gate = jnp.concatenate([wgu[:, k:k + ft] for k in range(0, n, 2 * ft)], axis=1)
    up = jnp.concatenate([wgu[:, k + ft:k + 2 * ft] for k in range(0, n, 2 * ft)], axis=1)
    return gate, up


def _ffn_fwd(lay, nseg, x, nw, modv, k0, wts, tag):
    wgu, wd, ft = wts
    h = _norm_mod_fwd(lay, nseg, x, nw, modv, k0, tag + "_norm")
    t = h.shape[0]

    def act(acc):
        g, u = acc[:, :ft], acc[:, ft:]
        sg = jax.nn.sigmoid(g)
        sl = g * sg
        return jnp.concatenate([u * (sg * (1.0 + g * (1.0 - sg))), sl], axis=1), sl * u

    s, a = _mm(h, wgu, "nn", None, tag + "_gu", tn=2 * ft, tm=256 if t % 256 == 0 else None,
               epilogue=act, outs=[(2 * ft, BF16), (ft, BF16)])
    o = _mm(a, wd, "nn", F32, tag + "_down")
    y = _resid_fwd(lay, nseg, x, o, modv, k0 + 2, 0.5, tag + "_res")
    return y, (x, h, s, a, o)


def _ffn_bwd(lay, nseg, dy, saved, nw, modv, k0, wts, tag):
    wgu, wd, ft = wts
    x, h, s, a, o = saved
    do, dgate = _resid_bwd(lay, nseg, dy, o, modv, k0 + 2, 0.5, tag + "_dres")

    def through_act(da, s_tile):
        return (jnp.concatenate([da, da], axis=1) * s_tile.astype(F32),)

    (dgu,) = _mm(do, wd.T, "nn", None, tag + "_da", tn=ft, tm=_big_tile(do.shape[0], 1024), extras=[(s, 2 * ft)],
                 epilogue=through_act, outs=[(2 * ft, BF16)])
    dwd = _mm(a, do, "tn", F32, tag + "_dwd")
    dh = _mm(dgu, wgu, "nt", F32, tag + "_dh")
    dwgu = _mm(h, dgu, "tn", F32, tag + "_dwgu")
    dx, dnw, dss = _norm_mod_bwd(lay, nseg, nseg, x, nw, modv, k0, dh, dy, tag + "_dnorm")
    return dx, (dwgu, dwd), dnw, jnp.concatenate([dss, dgate], axis=1)


def _local_step(lay, mx, xa, target, modv, w):
    d, bl = lay.d, lay.bl
    g = {}
    xa1, ffn1 = _ffn_fwd(lay, lay.ns, xa, w["norm_ffn1"], modv, 0, w["ffn1"], "ffn1")
    ha = _norm_mod_fwd(lay, lay.ns, xa1, w["norm_mix"], modv, 3, "mix_norm")
    proj = _mm(ha, w["w_in"], "nn", F32, "mix_in")
    pre, xbc = _conv5_fwd(lay.fine(), mx, proj, w["conv_w"], w["conv_b"], "mix_conv")
    dt = _dt_fwd(lay, mx, proj, w["dt_bias"], "mix_dt")
    dtt = dt.T
    yf, hpf = _ssd_fwd(lay, mx, xbc, dt, dtt, w["a_row"], w["a_col"], False, "ssd_f")
    yb, hpb = _ssd_fwd(lay, mx, xbc, dt, dtt, w["a_row"], w["a_col"], True, "ssd_b")
    cat_y = _gate_fwd(lay, mx, yf, yb, xbc, proj, w["d_row"], w["ssm_norm_w"], "mix_gate")
    u0 = _glu_fwd(lay, mx, proj, "mix_glu")
    uc = _axial(lay, mx, u0, None, w["cconv_w"], w["cconv_b"], "mix_axial")
    cat_u = _ln_fwd(lay, mx, uc, w["ln_w"], w["ln_b"], "mix_ln")
    cat = jnp.concatenate([cat_y, cat_u], axis=1)
    mix = _mm(cat, w["w_out"], "nn", F32, "mix_out")
    x2 = _resid_fwd(lay, lay.nsx, xa1, mix, modv, 5, 1.0, "mix_res")
    x3, ffn2 = _ffn_fwd(lay, lay.nsx, x2, w["norm_ffn2"], modv, 6, w["ffn2"], "ffn2")
    loss, dx3, g["final_norm"] = _final_loss(lay, x3, w["final_norm"], target, "loss")
    dx2, g["ffn2"], g["norm_ffn2"], dmod2 = _ffn_bwd(lay, lay.nsx, dx3, ffn2, w["norm_ffn2"], modv, 6, w["ffn2"], "ffn2")
    dmix, dg2 = _resid_bwd(lay, lay.nsx, dx2, mix, modv, 5, 1.0, "mix_dres")
    dcat = _mm(dmix, w["w_out"], "nt", F32, "mix_dcat")
    g["w_out"] = _mm(cat, dmix, "tn", F32, "mix_dwout")
    duc, g["ln_w"], g["ln_b"] = _ln_bwd(lay, mx, uc, w["ln_w"], w["ln_b"], dcat, "mix_dln")
    du0, g["cconv_w"], g["cconv_b"] = _axial(lay, mx, u0, duc, w["cconv_w"], None, "mix_daxial")
    dglu_a, dglu_b = _glu_bwd(lay, mx, proj, du0, "mix_dglu")
    dyssd, dz, g["d_row"], g["ssm_norm_w"] = _gate_bwd(lay, mx, yf, yb, xbc, proj, w["d_row"], w["ssm_norm_w"], dcat,
                                                       "mix_dgate")
    dxf, ddcf, ddrf, darf, dacf = _ssd_bwd(lay, mx, xbc, dt, dtt, w["a_row"], w["a_col"], hpf, dyssd, w["d_row"],
                                           False, "ssd_df")
    dxb, ddcb, ddrb, darb, dacb = _ssd_bwd(lay, mx, xbc, dt, dtt, w["a_row"], w["a_col"], hpb, dyssd, None,
                                           True, "ssd_db")
    g["a_row"] = (darf + darb) + (dacf + dacb).T
    dxbc, g["conv_w"], g["conv_b"] = _conv5_bwd(lay.fine(), mx, proj, pre, dxf, dxb, w["conv_w"], "mix_dconv")
    ddtraw, g["dt_bias"] = _dt_bwd(lay, mx, proj, w["dt_bias"], (ddcf, ddcb, ddrf.T, ddrb.T), "mix_ddt")
    dproj = jnp.concatenate([dz, dxbc, ddtraw, dglu_a, dglu_b], axis=1)
    dha = _mm(dproj, w["w_in"], "nt", F32, "mix_dha")
    g["w_in"] = _mm(ha, dproj, "tn", F32, "mix_dwin")
    dxa1, g["norm_mix"], dss_mix = _norm_mod_bwd(lay, lay.ns, lay.nsx, xa1, w["norm_mix"], modv, 3, dha, dx2, "mix_dnorm")
    dxa, g["ffn1"], g["norm_ffn1"], dmod1 = _ffn_bwd(lay, lay.ns, dxa1, ffn1, w["norm_ffn1"], modv, 0, w["ffn1"], "ffn1")
    zrow = lambda t: jnp.concatenate([t, jnp.zeros((1,) + t.shape[1:], F32)], axis=0)
    dmodv = jnp.concatenate([dmod1, dss_mix, zrow(dg2), zrow(dmod2)], axis=1)
    return loss, dxa[:lay.tx], g, dmodv


def _all_gather(xs, name, in_hbm):
    na = len(xs)

    def body(*refs):
        x_refs, out_refs = refs[:na], refs[na:2 * na]
        send_sems, recv_sems, local_sems = refs[2 * na:]
        mx_, my_, mc_ = lax.axis_index("x"), lax.axis_index("y"), lax.axis_index("c")
        me, sibling = (mx_, my_, mc_), (mx_, my_, 1 - mc_)
        chips = [(1 - mx_, my_), (mx_, 1 - my_), (1 - mx_, 1 - my_)]

        def slot(a, px, py, pc):
            return out_refs[a].at[4 * px + 2 * py + pc]

        def copy(a, k, block, to, own=False):
            return pltpu.make_async_remote_copy(
                src_ref=x_refs[a] if own else slot(a, *block), dst_ref=slot(a, *block),
                send_sem=send_sems.at[7 * a + k], recv_sem=recv_sems.at[7 * a + k], device_id=to, device_id_type=MESH)

        mine = [pltpu.make_async_copy(x_refs[a], slot(a, *me), local_sems.at[a]) for a in range(na)]
        for cp in mine:
            cp.start()
        first = []
        for a in range(na):
            first.append(copy(a, 0, me, sibling, own=True))
            first += [copy(a, 1 + j, me, (*chip, mc_), own=True) for j, chip in enumerate(chips)]
        for cp in first:
            cp.start()
        passed = []
        for j, chip in enumerate(chips):
            for a in range(na):
                copy(a, 1 + j, (*chip, mc_), me).wait_recv()
                fwd = copy(a, 4 + j, (*chip, mc_), sibling)
                fwd.start()
                passed.append(fwd)
        for a in range(na):
            copy(a, 0, sibling, me).wait_recv()
            for j, chip in enumerate(chips):
                copy(a, 4 + j, (*chip, 1 - mc_), me).wait_recv()
        for cp in first + passed:
            cp.wait_send()
        for cp in mine:
            cp.wait()

    space = pl.ANY if in_hbm else pltpu.VMEM
    return pl.pallas_call(
        body, out_shape=[jax.ShapeDtypeStruct((N_DEV,) + tuple(x.shape), x.dtype) for x in xs],
        in_specs=[pl.BlockSpec(memory_space=space)] * na, out_specs=[pl.BlockSpec(memory_space=space)] * na,
        scratch_shapes=[pltpu.SemaphoreType.DMA((7 * na,)), pltpu.SemaphoreType.DMA((7 * na,)),
                        pltpu.SemaphoreType.DMA((na,))],
        name=name,
    )(*xs)


N_CHIPS = 4


def _swap_sibling(gs, name):
    na = len(gs)

    def body(*refs):
        g_refs, out_refs, send_sems, recv_sems = refs[:na], refs[na:2 * na], refs[2 * na], refs[2 * na + 1]
        mx_, my_, mc_ = lax.axis_index("x"), lax.axis_index("y"), lax.axis_index("c")
        copies = [pltpu.make_async_remote_copy(
            src_ref=g_refs[a].at[k, 1 - mc_], dst_ref=out_refs[a].at[k], send_sem=send_sems.at[N_CHIPS * a + k],
            recv_sem=recv_sems.at[N_CHIPS * a + k], device_id=(mx_, my_, 1 - mc_), device_id_type=MESH)
            for a in range(na) for k in range(N_CHIPS)]
        for cp in copies:
            cp.start()
        for cp in copies:
            cp.wait_recv()
        for cp in copies:
            cp.wait_send()

    return pl.pallas_call(
        body, out_shape=[jax.ShapeDtypeStruct((N_CHIPS,) + tuple(g.shape[2:]), g.dtype) for g in gs],
        in_specs=[pl.BlockSpec(memory_space=pl.ANY)] * na, out_specs=[pl.BlockSpec(memory_space=pl.ANY)] * na,
        scratch_shapes=[pltpu.SemaphoreType.DMA((N_CHIPS * na,)), pltpu.SemaphoreType.DMA((N_CHIPS * na,))], name=name,
    )(*gs)


def _row_tile(r, n):
    if r * n * 4 <= (1 << 20):
        return r
    for t in (1024, 512, 256, 128, 64, 32, 16, 8):
        if r % t == 0 and t * n * 4 <= (1 << 20):
            return t
    return r


def _pair_add(place, g, got, name):
    _, _, r, n = g.shape
    tr = _row_tile(r, n)

    def body(place_ref, g_ref, got_ref, o_ref, ob_ref):
        s = g_ref[...] + got_ref[...]
        o_ref[...] = s
        ob_ref[...] = s.astype(ob_ref.dtype)

    blk = pl.BlockSpec((None, tr, n), lambda k, i, pr: (k, i, 0))
    grid_spec = pltpu.PrefetchScalarGridSpec(
        num_scalar_prefetch=1, grid=(N_CHIPS, r // tr),
        in_specs=[pl.BlockSpec((None, None, tr, n), lambda k, i, pr: (k, pr[0], i, 0)), blk], out_specs=[blk, blk])
    return pl.pallas_call(body, grid_spec=grid_spec,
                          out_shape=[jax.ShapeDtypeStruct((N_CHIPS, r, n), F32), jax.ShapeDtypeStruct((N_CHIPS, r, n), BF16)],
                          name=name, compiler_params=_cparams(2))(place, g, got)


def _swap_chips(ps, name):
    na = len(ps)

    def body(*refs):
        p_refs, out_refs, send_sems, recv_sems = refs[:na], refs[na:2 * na], refs[2 * na], refs[2 * na + 1]
        mx_, my_, mc_ = lax.axis_index("x"), lax.axis_index("y"), lax.axis_index("c")
        chips = [(1 - mx_, my_), (mx_, 1 - my_), (1 - mx_, 1 - my_)]
        copies = [pltpu.make_async_remote_copy(
            src_ref=p_refs[a].at[2 * cx + cy], dst_ref=out_refs[a].at[j], send_sem=send_sems.at[N_CHIP_PEERS * a + j],
            recv_sem=recv_sems.at[N_CHIP_PEERS * a + j], device_id=(cx, cy, mc_), device_id_type=MESH)
            for a in range(na) for j, (cx, cy) in enumerate(chips)]
        for cp in copies:
            cp.start()
        for cp in copies:
            cp.wait_recv()
        for cp in copies:
            cp.wait_send()

    return pl.pallas_call(
        body, out_shape=[jax.ShapeDtypeStruct((N_CHIP_PEERS,) + tuple(p.shape[1:]), p.dtype) for p in ps],
        in_specs=[pl.BlockSpec(memory_space=pl.ANY)] * na, out_specs=[pl.BlockSpec(memory_space=pl.ANY)] * na,
        scratch_shapes=[pltpu.SemaphoreType.DMA((N_CHIP_PEERS * na,)), pltpu.SemaphoreType.DMA((N_CHIP_PEERS * na,))],
        name=name,
    )(*ps)


def _sum_lead(x, name):
    k, r, n = x.shape
    tr = _row_tile(r, n * k)

    def body(x_ref, o_ref):
        acc = x_ref[0]
        for i in range(1, k):
            acc = acc + x_ref[i]
        o_ref[...] = acc

    return pl.pallas_call(body, grid=(r // tr,), in_specs=[pl.BlockSpec((k, tr, n), lambda i: (0, i, 0))],
                          out_specs=pl.BlockSpec((tr, n), lambda i: (i, 0)),
                          out_shape=jax.ShapeDtypeStruct((r, n), x.dtype), name=name, compiler_params=_cparams(1))(x)


def _adamw(place, w, parts, m, v, name):
    shape = w.shape
    cols = shape[-1]
    rows = math.prod(shape[:-1])
    to2 = lambda t: t.reshape(rows, cols)
    tr = _row_tile(rows, cols) if rows * cols * 4 > (1 << 20) else rows
    npart = len(parts)
    spec = pl.BlockSpec((tr, cols), lambda i, pr: (i, 0))
    part_specs, part_args = [], []
    for piece in parts:
        if isinstance(piece, tuple):
            stack, k = piece
            part_args.append(stack.reshape(stack.shape[0], rows, cols))
            if k == "chip":
                part_specs.append(pl.BlockSpec((None, tr, cols), lambda i, pr: (pr[1], i, 0)))
            else:
                part_specs.append(pl.BlockSpec((None, tr, cols), functools.partial(lambda i, pr, kk: (kk, i, 0), kk=k)))
        else:
            part_args.append(to2(piece))
            part_specs.append(spec)

    def body(place_ref, *refs):
        w_ref, m_ref, v_ref = refs[0], refs[1 + npart], refs[2 + npart]
        g_ref, d_ref, nm_ref, nv_ref = refs[3 + npart:]
        g = refs[1][...].astype(F32)
        for q in range(1, npart):
            g = g + refs[1 + q][...].astype(F32)
        mm = ADAM_B1 * m_ref[...] + (1.0 - ADAM_B1) * g
        vv = ADAM_B2 * v_ref[...] + (1.0 - ADAM_B2) * jnp.square(g)
        m_hat = mm / (1.0 - ADAM_B1 ** ADAM_STEP)
        v_hat = vv / (1.0 - ADAM_B2 ** ADAM_STEP)
        g_ref[...] = g
        d_ref[...] = -ADAM_LR * (m_hat / (jnp.sqrt(v_hat) + ADAM_EPS) + ADAM_WD * w_ref[...])
        nm_ref[...] = mm
        nv_ref[...] = vv

    sh = jax.ShapeDtypeStruct((rows, cols), F32)
    grid_spec = pltpu.PrefetchScalarGridSpec(num_scalar_prefetch=1, grid=(rows // tr,),
                                             in_specs=[spec] + part_specs + [spec, spec], out_specs=[spec] * 4)
    outs = pl.pallas_call(body, grid_spec=grid_spec, out_shape=[sh] * 4, name=name, compiler_params=_cparams(1),
                          )(place, to2(w), *part_args, to2(m), to2(v))
    return tuple(o.reshape(shape) for o in outs)


def _pack_rows(items, width):
    rows = []
    for t in items:
        flat = t.reshape(-1)
        n = flat.shape[0]
        k = -(-n // width)
        if k * width > n:
            flat = jnp.concatenate([flat, jnp.zeros((k * width - n,), t.dtype)])
        rows.append(flat.reshape(k, width))
    out = jnp.concatenate(rows, axis=0)
    pad = -out.shape[0] % 8
    if pad:
        out = jnp.concatenate([out, jnp.zeros((pad, width), out.dtype)], axis=0)
    return out


def _unpack_rows(packed, shapes, lead=()):
    width = packed.shape[-1]
    out, r = [], 0
    for sh in shapes:
        n = math.prod(sh)
        k = -(-n // width)
        piece = packed[..., r:r + k, :].reshape(tuple(lead) + (k * width,))[..., :n]
        out.append(piece.reshape(tuple(lead) + tuple(sh)))
        r += k
    return out


def _cols_full(t):
    return jnp.transpose(t, (1, 0, 2)).reshape(t.shape[1], -1)


def _cols_shards(t):
    d = t.shape[0]
    return jnp.transpose(t.reshape(d, N_DEV, -1), (1, 0, 2))


BIG = ("ffn1_gate", "ffn1_up", "ffn1_down", "w_in", "w_out", "ffn2_gate", "ffn2_up", "ffn2_down")
ROW_SHARDED = ("ffn1_down", "w_out", "ffn2_down")


def kernel(x, c, ctx, c_ctx, w_mod, b_mod, norm_ffn1, ffn1_gate, ffn1_up, ffn1_down, norm_mix, w_in, ssm_conv_w, ssm_conv_b, dt_bias_fwd, dt_bias_bwd, a_log_fwd, a_log_bwd, ssm_d, ssm_norm_w, cconv_w, cconv_b, cconv_ln_w, cconv_ln_b, w_out, norm_ffn2, ffn2_gate, ffn2_up, ffn2_down, final_norm, loss_target, m_c_ctx, m_w_mod, m_b_mod, m_norm_ffn1, m_ffn1_gate, m_ffn1_up, m_ffn1_down, m_norm_mix, m_w_in, m_ssm_conv_w, m_ssm_conv_b, m_dt_bias_fwd, m_dt_bias_bwd, m_a_log_fwd, m_a_log_bwd, m_ssm_d, m_ssm_norm_w, m_cconv_w, m_cconv_b, m_cconv_ln_w, m_cconv_ln_b, m_w_out, m_norm_ffn2, m_ffn2_gate, m_ffn2_up, m_ffn2_down, m_final_norm, v_c_ctx, v_w_mod, v_b_mod, v_norm_ffn1, v_ffn1_gate, v_ffn1_up, v_ffn1_down, v_norm_mix, v_w_in, v_ssm_conv_w, v_ssm_conv_b, v_dt_bias_fwd, v_dt_bias_bwd, v_a_log_fwd, v_a_log_bwd, v_ssm_d, v_ssm_norm_w, v_cconv_w, v_cconv_b, v_cconv_ln_w, v_cconv_ln_b, v_w_out, v_norm_ffn2, v_ffn2_gate, v_ffn2_up, v_ffn2_down, v_final_norm):
    args = dict(locals())
    names = ("c_ctx", "w_mod", "b_mod", "norm_ffn1", "ffn1_gate", "ffn1_up", "ffn1_down", "norm_mix", "w_in",
             "ssm_conv_w", "ssm_conv_b", "dt_bias_fwd", "dt_bias_bwd", "a_log_fwd", "a_log_bwd", "ssm_d", "ssm_norm_w",
             "cconv_w", "cconv_b", "cconv_ln_w", "cconv_ln_b", "w_out", "norm_ffn2", "ffn2_gate", "ffn2_up",
             "ffn2_down", "final_norm")
    wts = {n: args[n] for n in names}
    bl, seq, d = x.shape
    clen = ctx.shape[1]
    heads = dt_bias_fwd.shape[1]
    ft = _gate_tile(ffn1_gate.shape[2] * N_DEV)
    lay = _Lay(bl, seq, clen, d)
    mx = _Mix(d, heads)
    nb = bl * N_DEV
    me = 4 * lax.axis_index("x") + 2 * lax.axis_index("y") + lax.axis_index("c")
    mcols = w_mod.shape[2]
    n_ctx_mod = 5 * d

    place = jnp.stack([lax.axis_index("c"), 2 * lax.axis_index("x") + lax.axis_index("y")]).astype(jnp.int32)

    small_shapes = [(bl, d), ssm_conv_w.shape[1:], cconv_w.shape[1:]]
    (g1,) = _all_gather([_pack_rows([c, ssm_conv_w, cconv_w], d)], "gather_small", False)
    c_g, conv_g, cconv_g = _unpack_rows(g1, small_shapes, (N_DEV,))
    c_all = c_g.reshape(nb, d)
    conv_w_full = jnp.transpose(conv_g, (1, 0, 2)).reshape(conv_g.shape[1], -1)
    cconv_w_full = jnp.transpose(cconv_g, (1, 0, 2)).reshape(cconv_g.shape[1], -1)

    s_all = jnp.concatenate([_silu(c_all), _silu(c_ctx)[None, :], jnp.zeros((7, d), F32)], axis=0)
    mod_cols = _mm(s_all, w_mod[0], "nn", F32, "mod_cols")
    (g2,) = _all_gather([mod_cols], "gather_mod", False)
    mod_all = _cols_full(g2) + b_mod
    mod_mine = jnp.concatenate([lax.dynamic_slice_in_dim(mod_all, me * bl, bl, axis=0), mod_all[nb:nb + 1]], axis=0)
    modv = mod_mine.reshape(bl + 1, N_MOD, d)

    full = dict(zip(BIG, _all_gather([wts[n][0].astype(BF16) for n in BIG], "gather_weights", True)))
    hh = 2 * heads
    w_in_f = _cols_full(full["w_in"])
    w_in_p = jnp.concatenate([w_in_f[:, :mx.off_dt_ref + hh], jnp.zeros((d, DT_PAD - hh), BF16),
                              w_in_f[:, mx.off_glu_ref:]], axis=1)
    lanes_pad = lambda a, b: jnp.concatenate([a, b, jnp.zeros((1, DT_LANES - hh), F32)], axis=1)
    a_vals = lanes_pad(-jnp.exp(a_log_fwd), -jnp.exp(a_log_bwd))
    w = {
        "norm_ffn1": norm_ffn1, "norm_mix": norm_mix, "norm_ffn2": norm_ffn2, "final_norm": final_norm[None, :],
        "ffn1": (_interleave(_cols_full(full["ffn1_gate"]), _cols_full(full["ffn1_up"]), ft),
                 full["ffn1_down"].reshape(-1, d), ft),
        "ffn2": (_interleave(_cols_full(full["ffn2_gate"]), _cols_full(full["ffn2_up"]), ft),
                 full["ffn2_down"].reshape(-1, d), ft),
        "w_in": w_in_p, "w_out": full["w_out"].reshape(-1, d),
        "conv_w": jnp.concatenate([conv_w_full, jnp.zeros((3, mx.xw), F32)], axis=0), "conv_b": ssm_conv_b,
        "dt_bias": lanes_pad(dt_bias_fwd, dt_bias_bwd), "a_row": a_vals, "a_col": a_vals.T,
        "d_row": jnp.repeat(ssm_d, HEAD_DIM, axis=1), "ssm_norm_w": ssm_norm_w,
        "cconv_w": cconv_w_full, "cconv_b": cconv_b, "ln_w": cconv_ln_w, "ln_b": cconv_ln_b,
    }

    xa = jnp.concatenate([x.reshape(bl * seq, d), ctx.reshape(bl * clen, d)], axis=0)
    loss, grad_x, g, dmodv = _local_step(lay, mx, xa, loss_target.reshape(bl * seq, d), modv, w)
    loss = lax.psum(loss[0, 0], ("x", "y", "c"))

    dw_in = jnp.concatenate([g["w_in"][:, :mx.off_dt_ref + hh], g["w_in"][:, mx.off_glu:]], axis=1)
    gbig = {}
    for tag in ("ffn1", "ffn2"):
        dwgu, dwd = g[tag]
        dgate, dup = _deinterleave(dwgu, ft)
        gbig[tag + "_gate"], gbig[tag + "_up"] = _cols_shards(dgate), _cols_shards(dup)
        gbig[tag + "_down"] = dwd.reshape((N_DEV,) + tuple(ffn1_down.shape[1:]))
    gbig["w_in"] = _cols_shards(dw_in)
    gbig["w_out"] = g["w_out"].reshape((N_DEV,) + tuple(w_out.shape[1:]))
    by_dest = [gbig[n].reshape((N_CHIPS, 2) + tuple(gbig[n].shape[1:])) for n in BIG]
    got = _swap_sibling(by_dest, "rs_sibling")
    pair = [_pair_add(place, t, s, "rs_pair_add_" + n) for n, t, s in zip(BIG, by_dest, got)]
    chip_sum = [p[0] for p in pair]
    from_chips = _swap_chips([p[1] for p in pair], "rs_chips")

    n9 = N_MOD * d
    dmod_rows = dmodv.reshape(bl + 1, n9)
    ctx_row = jnp.concatenate([dmod_rows[bl, :n_ctx_mod], jnp.zeros((n9 - n_ctx_mod,), F32)])
    summed = [ctx_row, g["norm_ffn1"], g["norm_mix"], g["norm_ffn2"], g["final_norm"], g["conv_b"], g["dt_bias"],
              g["a_row"], g["d_row"], g["ssm_norm_w"], g["cconv_b"], g["ln_w"], g["ln_b"], g["conv_w"][:5], g["cconv_w"]]
    sum_shapes = [t.shape for t in summed]
    (g4,) = _all_gather([_pack_rows([dmod_rows[:bl]] + summed, d)], "gather_small_grads", False)
    nrow_batch = bl * N_MOD
    dmod_batch = g4[:, :nrow_batch].reshape(nb, n9)
    tot = _sum_lead(g4[:, nrow_batch:], "sum_small_grads")
    (dctx, g_n1, g_nm, g_n2, g_fn, g_cb, g_dtb, g_a, g_drow, g_snw, g_ccb, g_lnw, g_lnb, g_cw, g_ccw) = _unpack_rows(tot, sum_shapes)
    dmod_all = jnp.concatenate([dmod_batch, dctx[None, :], jnp.zeros((7, n9), F32)], axis=0)

    dmod_my_cols = lax.dynamic_slice_in_dim(dmod_all, me * mcols, mcols, axis=1)
    g_w_mod = _mm(s_all, dmod_my_cols, "tn", F32, "dw_mod")[None]
    g_b_mod = _sum_lead(dmod_all.reshape(nb + 8, N_MOD, d), "db_mod").reshape(1, n9)
    ds_part = _mm(dmod_my_cols[nb:nb + 8], w_mod[0], "nt", F32, "ds_ctx")
    (g5,) = _all_gather([jnp.concatenate([ds_part[0:1], jnp.zeros((7, d), F32)], axis=0)], "gather_ds_ctx", False)
    ds_ctx = _sum_lead(g5, "sum_ds_ctx")[0]
    sg = jax.nn.sigmoid(c_ctx)
    g_c_ctx = ds_ctx * (sg * (1.0 + c_ctx * (1.0 - sg)))

    a_f, a_b = a_vals[:, :heads], a_vals[:, heads:hh]
    grads = {
        "c_ctx": [g_c_ctx], "w_mod": [g_w_mod], "b_mod": [g_b_mod],
        "norm_ffn1": [g_n1], "norm_mix": [g_nm], "norm_ffn2": [g_n2], "final_norm": [g_fn.reshape(-1)],
        "ssm_conv_w": [lax.dynamic_slice_in_dim(g_cw, me * ssm_conv_w.shape[2], ssm_conv_w.shape[2], axis=1)[None]],
        "ssm_conv_b": [g_cb],
        "dt_bias_fwd": [g_dtb[:, :heads]], "dt_bias_bwd": [g_dtb[:, heads:hh]],
        "a_log_fwd": [g_a[:, :heads] * a_f], "a_log_bwd": [g_a[:, heads:hh] * a_b],
        "ssm_d": [jnp.sum(g_drow.reshape(1, heads, HEAD_DIM), axis=2)], "ssm_norm_w": [g_snw],
        "cconv_w": [lax.dynamic_slice_in_dim(g_ccw, me * cconv_w.shape[2], cconv_w.shape[2], axis=1)[None]],
        "cconv_b": [g_ccb], "cconv_ln_w": [g_lnw], "cconv_ln_b": [g_lnb],
    }
    for n, own, others in zip(BIG, chip_sum, from_chips):
        grads[n] = [(own, "chip"), (others, 0), (others, 1), (others, 2)]

    out_g, out_d, out_m, out_v = [], [], [], []
    for n in names:
        gr, de, nm, nv = _adamw(place, wts[n], grads[n], args["m_" + n], args["v_" + n], "adamw_" + n)
        out_g.append(gr)
        out_d.append(de)
        out_m.append(nm)
        out_v.append(nv)
    return (loss, grad_x.reshape(bl, seq, d), *out_g, *out_d, *out_m, *out_v)
```

```python
import functools
import math

import jax
import jax.numpy as jnp
from jax import lax
from jax.experimental import pallas as pl
from jax.experimental.pallas import tpu as pltpu

F32 = jnp.float32
BF16 = jnp.bfloat16
MESH = pl.DeviceIdType.MESH

N_DEV = 8
N_CHIP_PEERS = 3
HEAD_DIM = 64
N_STATE = 128
SSD_GROUPS = 2
CHUNK = 128
GRID_W = 64
N_MOD = 9
EPS = 1e-6
DT_PAD = 512
DT_LANES = 128
HALO = 8
ROW_TILE = 512
FINE_ROW_TILE = 256
VMEM_LIMIT = 48 * 1024 * 1024
NEG_BIG = -1e30

ADAM_LR = 0.001
ADAM_B1 = 0.9
ADAM_B2 = 0.999
ADAM_EPS = 1e-08
ADAM_WD = 0.01
ADAM_STEP = 10


def _pick(n, prefs):
    for p in prefs:
        if n % p == 0:
            return p
    return n


MM_TILE_CAP = 2816
MM_TILE_ELEMS = 3 << 20
MM_OUT_TILE_ELEMS = 3 << 19
MM_FULL_ROWS = 1024


def _big_tile(n, cap):
    if n <= cap:
        return n
    best = 0
    for t in range(128, cap + 1, 128):
        if n % t == 0:
            best = t
    return best or n


def _cparams(ndim):
    return pltpu.CompilerParams(dimension_semantics=("arbitrary",) * ndim, vmem_limit_bytes=VMEM_LIMIT)


def _silu(v):
    return v * jax.nn.sigmoid(v)


def _mm(a, b, mode, out_dtype, name, tn=None, tm=None, extras=(), epilogue=None, outs=None, ride=None):
    if mode == "tn":
        (K, M), (K2, N) = a.shape, b.shape
    elif mode == "nt":
        (M, K), (N, K2) = a.shape, b.shape
    else:
        (M, K), (K2, N) = a.shape, b.shape
    assert K == K2, (name, a.shape, b.shape)
    tm = M if M <= MM_FULL_ROWS else tm
    if tn is None:
        tn = _big_tile(N, min(MM_TILE_CAP, max(128, MM_OUT_TILE_ELEMS // (tm or 512))))
    if tm is None:
        tm = _big_tile(M, max(128, MM_OUT_TILE_ELEMS // tn))
    tk = _big_tile(K, min(MM_TILE_CAP, MM_TILE_ELEMS // max(tn, tm)))
    nk = K // tk
    ni, nj = M // tm, N // tn
    swap = nk == 1 and (K * N + M * K * nj) < (M * K + K * N * ni)
    ij = (lambda g0, g1: (g1, g0)) if swap else (lambda g0, g1: (g0, g1))
    if mode == "tn":
        a_spec = pl.BlockSpec((tk, tm), lambda g0, g1, k: (k, ij(g0, g1)[0]))
        dn = (((0,), (0,)), ((), ()))
    else:
        a_spec = pl.BlockSpec((tm, tk), lambda g0, g1, k: (ij(g0, g1)[0], k))
        dn = (((1,), (1,)), ((), ())) if mode == "nt" else (((1,), (0,)), ((), ()))
    if mode == "nt":
        b_spec = pl.BlockSpec((tn, tk), lambda g0, g1, k: (ij(g0, g1)[1], k))
    else:
        b_spec = pl.BlockSpec((tk, tn), lambda g0, g1, k: (k, ij(g0, g1)[1]))
    if outs is None:
        outs = [(tn, out_dtype)]
    nx = len(extras)

    def tile(w):
        return pl.BlockSpec((tm, w), lambda g0, g1, k: ij(g0, g1))

    def finish(acc, refs):
        vals = (acc,) if epilogue is None else epilogue(acc, *[r[...] for r in refs[:nx]])
        for o_ref, v in zip(refs[nx:], vals):
            o_ref[...] = v.astype(o_ref.dtype)

    grid = (nj, ni, nk) if swap else (ni, nj, nk)
    nout = len(outs)
    r_in = len(ride.arrays) if ride else 0
    r_out = len(ride.out_shapes) if ride else 0

    def compute(a_ref, b_ref, refs):
        part = lax.dot_general(a_ref[...].astype(BF16), b_ref[...].astype(BF16), dn, preferred_element_type=F32)
        if nk == 1:
            finish(part, refs)
            return
        acc_ref, k = refs[-1], pl.program_id(2)
        _acc(acc_ref, k == 0, part)

        @pl.when(k == nk - 1)
        def _():
            finish(acc_ref[...], refs[:-1])

    def body(a_ref, b_ref, *refs):
        if ride is None:
            compute(a_ref, b_ref, refs)
            return
        x_refs, rin = refs[:nx], refs[nx:nx + r_in]
        o_refs, rout = refs[nx + r_in:nx + r_in + nout], refs[nx + r_in + nout:nx + r_in + nout + r_out]
        tail = refs[nx + r_in + nout + r_out:]
        nacc = 1 if nk > 1 else 0
        sems = tail[nacc:]
        ids = [pl.program_id(q) for q in range(3)]
        first = functools.reduce(jnp.logical_and, [i == 0 for i in ids])
        last = functools.reduce(jnp.logical_and, [i == n - 1 for i, n in zip(ids, grid)])
        pl.when(first)(lambda: ride.start(rin, rout, *sems))
        compute(a_ref, b_ref, tuple(x_refs) + tuple(o_refs) + tuple(tail[:nacc]))
        pl.when(last)(lambda: ride.finish(rin, rout, *sems))

    hbm = pl.BlockSpec(memory_space=pl.ANY)
    res = pl.pallas_call(
        body, grid=grid, in_specs=[a_spec, b_spec] + [tile(w) for _, w in extras] + [hbm] * r_in,
        out_specs=[tile(w) for w, _ in outs] + [hbm] * r_out,
        out_shape=[jax.ShapeDtypeStruct((M, nj * w), dt) for w, dt in outs] + (list(ride.out_shapes) if ride else []),
        scratch_shapes=([pltpu.VMEM((tm, tn), F32)] if nk > 1 else []) + (list(ride.sems) if ride else []),
        name=name, compiler_params=_cparams(3),
    )(a, b, *[x for x, _ in extras], *(ride.arrays if ride else []))
    main = res[0] if epilogue is None else res[:nout]
    return (main, res[nout:]) if ride else main


def _ffn_tm(t):
    return _pick(t, (1024, 512, 256, 128, 64, 32, 16, 8))


def _dsilu_mul(g, u, da):
    sg = jax.nn.sigmoid(g)
    return da * u * (sg * (1.0 + g * (1.0 - sg))), da * (g * sg)


def _ffn_gu(h, wg, wu, name):
    t, d = h.shape
    nb, _, fb = wg.shape
    tm = _ffn_tm(t)

    def body(h_ref, wg_ref, wu_ref, g_ref, u_ref, a_ref):
        hv = h_ref[...]
        g = jnp.dot(hv, wg_ref[...], preferred_element_type=F32)
        u = jnp.dot(hv, wu_ref[...], preferred_element_type=F32)
        g_ref[...] = g
        u_ref[...] = u
        a_ref[...] = (_silu(g) * u).astype(a_ref.dtype)

    wspec = pl.BlockSpec((None, d, fb), lambda i, j: (j, 0, 0))
    ospec = pl.BlockSpec((None, tm, fb), lambda i, j: (j, i, 0))
    sh = jax.ShapeDtypeStruct((nb, t, fb), F32)
    return pl.pallas_call(body, grid=(t // tm, nb), in_specs=[pl.BlockSpec((tm, d), lambda i, j: (i, 0)), wspec, wspec],
                          out_specs=[ospec, ospec, ospec], out_shape=[sh, sh, jax.ShapeDtypeStruct((nb, t, fb), BF16)],
                          name=name, compiler_params=_cparams(2))(h, wg, wu)


def _ffn_contract(xs, ws, mode, name):
    nb, t, fb = xs[0].shape
    d = ws[0].shape[2] if mode == "nn" else ws[0].shape[1]
    tm = _ffn_tm(t)
    npair = len(xs)
    dn = (((1,), (0,)), ((), ())) if mode == "nn" else (((1,), (1,)), ((), ()))

    def body(*refs):
        o_ref, acc_ref = refs[2 * npair], refs[2 * npair + 1]
        j = pl.program_id(1)
        part = lax.dot_general(refs[0][...], refs[npair][...], dn, preferred_element_type=F32)
        for p in range(1, npair):
            part = part + lax.dot_general(refs[p][...], refs[npair + p][...], dn, preferred_element_type=F32)
        _acc(acc_ref, j == 0, part)

        @pl.when(j == nb - 1)
        def _():
            o_ref[...] = acc_ref[...]

    xspec = pl.BlockSpec((None, tm, fb), lambda i, j: (j, i, 0))
    wspec = pl.BlockSpec((None,) + tuple(ws[0].shape[1:]), lambda i, j: (j, 0, 0))
    return pl.pallas_call(body, grid=(t // tm, nb), in_specs=[xspec] * npair + [wspec] * npair,
                          out_specs=pl.BlockSpec((tm, d), lambda i, j: (i, 0)),
                          out_shape=jax.ShapeDtypeStruct((t, d), F32), scratch_shapes=[pltpu.VMEM((tm, d), F32)],
                          name=name, compiler_params=_cparams(2))(*xs, *ws)


def _ffn_da(do, wd, g, u, name):
    t, d = do.shape
    nb, fb, _ = wd.shape
    tm = _ffn_tm(t)

    def body(do_ref, wd_ref, g_ref, u_ref, dg_ref, du_ref):
        da = lax.dot_general(do_ref[...], wd_ref[...], (((1,), (1,)), ((), ())), preferred_element_type=F32)
        dg, du = _dsilu_mul(g_ref[...], u_ref[...], da)
        dg_ref[...] = dg.astype(dg_ref.dtype)
        du_ref[...] = du.astype(du_ref.dtype)

    bspec = pl.BlockSpec((None, tm, fb), lambda i, j: (j, i, 0))
    sh = jax.ShapeDtypeStruct((nb, t, fb), BF16)
    return pl.pallas_call(body, grid=(t // tm, nb),
                          in_specs=[pl.BlockSpec((tm, d), lambda i, j: (i, 0)),
                                    pl.BlockSpec((None, fb, d), lambda i, j: (j, 0, 0)), bspec, bspec],
                          out_specs=[bspec, bspec], out_shape=[sh, sh], name=name, compiler_params=_cparams(2),
                          )(do, wd, g, u)


def _ffn_dw(h, xs, name):
    t, d = h.shape
    nb, _, fb = xs[0].shape
    tk = _ffn_tm(t)
    nk = t // tk
    npair = len(xs)
    tn_dims = (((0,), (0,)), ((), ()))

    def body(*refs):
        h_ref = refs[0]
        k = pl.program_id(1)
        hv = h_ref[...]
        for p in range(npair):
            part = lax.dot_general(hv, refs[1 + p][...], tn_dims, preferred_element_type=F32)
            _acc(refs[1 + npair + p], k == 0, part)

    xspec = pl.BlockSpec((None, tk, fb), lambda j, k: (j, k, 0))
    ospec = pl.BlockSpec((None, d, fb), lambda j, k: (j, 0, 0))
    sh = jax.ShapeDtypeStruct((nb, d, fb), F32)
    return pl.pallas_call(body, grid=(nb, nk), in_specs=[pl.BlockSpec((tk, d), lambda j, k: (k, 0))] + [xspec] * npair,
                          out_specs=[ospec] * npair, out_shape=[sh] * npair, name=name, compiler_params=_cparams(2),
                          )(h, *xs)


def _ffn_dwd(a, do, name):
    nb, t, fb = a.shape
    d = do.shape[1]
    tk = _ffn_tm(t)

    def body(a_ref, do_ref, o_ref):
        part = lax.dot_general(a_ref[...], do_ref[...], (((0,), (0,)), ((), ())), preferred_element_type=F32)
        _acc(o_ref, pl.program_id(1) == 0, part)

    return pl.pallas_call(body, grid=(nb, t // tk),
                          in_specs=[pl.BlockSpec((None, tk, fb), lambda j, k: (j, k, 0)),
                                    pl.BlockSpec((tk, d), lambda j, k: (k, 0))],
                          out_specs=pl.BlockSpec((None, fb, d), lambda j, k: (j, 0, 0)),
                          out_shape=jax.ShapeDtypeStruct((nb, fb, d), F32), name=name, compiler_params=_cparams(2),
                          )(a, do)


class _Lay:
    def __init__(self, bl, seq, clen, d, tt=None):
        self.bl, self.seq, self.clen, self.d = bl, seq, clen, d
        self.tt = min(ROW_TILE, math.gcd(seq, bl * clen)) if tt is None else tt
        assert seq % self.tt == 0 and (bl * clen) % self.tt == 0 and self.tt % 8 == 0
        self.spb = seq // self.tt
        self.spc = clen // self.tt
        self.nsx = bl * self.spb
        self.nsc = bl * clen // self.tt
        self.ns = self.nsx + self.nsc
        self.tx = bl * seq
        self.ta = self.tx + bl * clen

    def fine(self):
        return _Lay(self.bl, self.seq, self.clen, self.d, min(FINE_ROW_TILE, self.clen))

    def mrow(self, s):
        return jnp.where(s < self.nsx, s // self.spb, self.bl)

    def first_of_row(self, s):
        return jnp.logical_or(jnp.logical_and(s < self.nsx, s % self.spb == 0), s == self.nsx)

    def seq_first(self, s):
        return jnp.where(s < self.nsx, s % self.spb == 0, (s - self.nsx) % self.spc == 0)

    def seq_last(self, s):
        return jnp.where(s < self.nsx, s % self.spb == self.spb - 1, (s - self.nsx) % self.spc == self.spc - 1)


def _tok(lay, c, cb=0, clamp=None):
    if clamp is None:
        return pl.BlockSpec((lay.tt, c), lambda j, s: (s, cb + j))
    return pl.BlockSpec((lay.tt, c), lambda j, s: (jnp.minimum(s, clamp), cb + j))


def _halo_prev(lay, c, cb=0):
    u = lay.tt // HALO
    return pl.BlockSpec((HALO, c), lambda j, s: (jnp.maximum(s * u - 1, 0), cb + j))


def _halo_next(lay, c, cb=0):
    u = lay.tt // HALO
    last = lay.ta // HALO - 1
    return pl.BlockSpec((HALO, c), lambda j, s: (jnp.minimum((s + 1) * u, last), cb + j))


def _row(lay, k, c):
    return pl.BlockSpec((None, k, c), lambda j, s: (lay.mrow(s), 0, 0))


def _glob(k, c, cb=None):
    if cb is None:
        return pl.BlockSpec((k, c), lambda j, s: (0, 0))
    return pl.BlockSpec((k, c), lambda j, s: (0, cb + j))


def _tok_call(name, body, ncb, nseg, in_specs, out_specs, out_shape, inputs, scratch=()):
    return pl.pallas_call(body, grid=(ncb, nseg), in_specs=in_specs, out_specs=out_specs, out_shape=out_shape,
                          scratch_shapes=list(scratch), name=name, compiler_params=_cparams(2))(*inputs)


def _acc(ref, first, val):
    @pl.when(first)
    def _():
        ref[...] = val

    @pl.when(jnp.logical_not(first))
    def _():
        ref[...] += val


def _norm_mod_f(x, w, sh, sc):
    y = x * lax.rsqrt(jnp.mean(x * x, axis=-1, keepdims=True) + EPS) * w
    return y * (1.0 + sc) + sh


def _norm_mod_fwd(lay, nseg, x, w, modv, ksh, name):
    d = lay.d

    def body(x_ref, w_ref, m_ref, h_ref):
        h = _norm_mod_f(x_ref[...], w_ref[...], m_ref[ksh:ksh + 1, :], m_ref[ksh + 1:ksh + 2, :])
        h_ref[...] = h.astype(h_ref.dtype)

    return _tok_call(name, body, 1, nseg, [_tok(lay, d), _glob(1, d), _row(lay, N_MOD, d)], _tok(lay, d),
                     jax.ShapeDtypeStruct((nseg * lay.tt, d), BF16), (x, w, modv))


def _norm_mod_bwd(lay, nseg, nres, x, w, modv, ksh, dh, dres, name, nout=None):
    d = lay.d
    nrow = lay.bl + (1 if nseg > lay.nsx else 0)
    nout = nseg if nout is None else nout

    def body(x_ref, w_ref, m_ref, dh_ref, dres_ref, dx_ref, dw_ref, dm_ref):
        s = pl.program_id(1)
        _, vjp = jax.vjp(_norm_mod_f, x_ref[...], w_ref[...], m_ref[ksh:ksh + 1, :], m_ref[ksh + 1:ksh + 2, :])
        dx, dw, dsh, dsc = vjp(dh_ref[...])

        @pl.when(s < nout)
        def _():
            dx_ref[...] = dx + jnp.where(s < nres, dres_ref[...], 0.0)

        _acc(dw_ref, s == 0, dw)
        _acc(dm_ref, lay.first_of_row(s), jnp.concatenate([dsh, dsc], axis=0))

    return _tok_call(
        name, body, 1, nseg,
        [_tok(lay, d), _glob(1, d), _row(lay, N_MOD, d), _tok(lay, d), _tok(lay, d, clamp=nres - 1)],
        [_tok(lay, d, clamp=nout - 1), _glob(1, d), _row(lay, 2, d)],
        [jax.ShapeDtypeStruct((nout * lay.tt, d), F32), jax.ShapeDtypeStruct((1, d), F32),
         jax.ShapeDtypeStruct((nrow, 2, d), F32)],
        (x, w, modv, dh, dres))


def _resid_fwd(lay, nseg, x, o, modv, kg, coef, name):
    d = lay.d

    def body(x_ref, o_ref, m_ref, y_ref):
        y_ref[...] = x_ref[...] + (coef * m_ref[kg:kg + 1, :]) * o_ref[...]

    return _tok_call(name, body, 1, nseg, [_tok(lay, d), _tok(lay, d), _row(lay, N_MOD, d)], _tok(lay, d),
                     jax.ShapeDtypeStruct((nseg * lay.tt, d), F32), (x, o, modv))


def _resid_bwd(lay, nseg, dy, o, modv, kg, coef, name):
    d = lay.d
    nrow = lay.bl + (1 if nseg > lay.nsx else 0)

    def body(dy_ref, o_ref, m_ref, do_ref, dg_ref):
        s = pl.program_id(1)
        dy = dy_ref[...]
        do_ref[...] = (dy * (coef * m_ref[kg:kg + 1, :])).astype(do_ref.dtype)
        _acc(dg_ref, lay.first_of_row(s), jnp.sum(dy * o_ref[...], axis=0, keepdims=True) * coef)

    return _tok_call(name, body, 1, nseg, [_tok(lay, d), _tok(lay, d), _row(lay, N_MOD, d)],
                     [_tok(lay, d), _row(lay, 1, d)],
                     [jax.ShapeDtypeStruct((nseg * lay.tt, d), BF16), jax.ShapeDtypeStruct((nrow, 1, d), F32)],
                     (dy, o, modv))


def _final_loss(lay, x, wf, target, name):
    d = lay.d

    def body(x_ref, w_ref, t_ref, loss_ref, dx_ref, dw_ref):
        s = pl.program_id(1)

        def f(xv, wv):
            return xv * lax.rsqrt(jnp.mean(xv * xv, axis=-1, keepdims=True) + EPS) * wv

        y, vjp = jax.vjp(f, x_ref[...], w_ref[...])
        err = y - t_ref[...]
        part = 0.5 * jnp.sum(jnp.sum(err * err, axis=-1, keepdims=True), axis=0, keepdims=True) / d
        dx, dw = vjp(err / d)
        dx_ref[...] = dx
        _acc(loss_ref, s == 0, part)
        _acc(dw_ref, s == 0, dw)

    return _tok_call(name, body, 1, lay.nsx, [_tok(lay, d), _glob(1, d), _tok(lay, d)],
                     [_glob(1, 1), _tok(lay, d), _glob(1, d)],
                     [jax.ShapeDtypeStruct((1, 1), F32), jax.ShapeDtypeStruct((lay.tx, d), F32),
                      jax.ShapeDtypeStruct((1, d), F32)], (x, wf, target))


class _Mix:
    def __init__(self, d, heads):
        self.d_ssm = d
        self.d_conv = d
        self.heads = heads
        assert heads * HEAD_DIM == d and heads % (2 * SSD_GROUPS) == 0 and 2 * heads <= DT_LANES
        self.gn = SSD_GROUPS * N_STATE
        self.xw = d + 2 * self.gn
        self.off_dt = d
        self.off_x = d + DT_PAD
        self.off_glu = self.off_x + self.xw
        self.pw = self.off_glu + 2 * d
        self.ref_x = d
        self.ref_dt = d + self.xw
        self.ref_glu = self.ref_dt + 2 * heads
        self.cc = self.xw if self.off_x % self.xw == 0 else _pick(self.xw, (512, 256, 128))


def _conv5_fwd(lay, mx, proj, cw, cb, name):
    c, tt = mx.cc, lay.tt
    cb0 = mx.off_x // c
    assert mx.off_x % c == 0

    def body(prev_ref, cur_ref, next_ref, w_ref, b_ref, pre_ref, act_ref, ext_ref):
        s = pl.program_id(1)
        ext_ref[0:HALO, :] = jnp.where(lay.seq_first(s), 0.0, prev_ref[...])
        ext_ref[HALO:HALO + tt, :] = cur_ref[...]
        ext_ref[HALO + tt:, :] = jnp.where(lay.seq_last(s), 0.0, next_ref[...])
        acc = jnp.zeros((tt, c), F32) + b_ref[...]
        for k in range(5):
            acc = acc + w_ref[k:k + 1, :] * ext_ref[pl.ds(HALO + k - 2, tt), :]
        pre_ref[...] = acc
        act_ref[...] = _silu(acc)

    sh = jax.ShapeDtypeStruct((lay.ta, mx.xw), F32)
    return _tok_call(name, body, mx.xw // c, lay.ns,
                     [_halo_prev(lay, c, cb0), _tok(lay, c, cb0), _halo_next(lay, c, cb0), _glob(8, c, 0), _glob(1, c, 0)],
                     [_tok(lay, c), _tok(lay, c)], [sh, sh], (proj, proj, proj, cw, cb),
                     scratch=[pltpu.VMEM((tt + 2 * HALO, c), F32)])


def _conv5_bwd(lay, mx, proj, pre, dact_f, dact_b, cw, name):
    c, tt = mx.cc, lay.tt
    cb0 = mx.off_x // c

    def dsilu(p):
        sg = jax.nn.sigmoid(p)
        return sg * (1.0 + p * (1.0 - sg))

    def body(xp_ref, xc_ref, xn_ref, pp_ref, pc_ref, pn_ref, fp_ref, fc_ref, fn_ref, bp_ref, bc_ref, bn_ref, w_ref,
             dx_ref, dw_ref, db_ref, extx_ref, extd_ref):
        s = pl.program_id(1)
        first, last = lay.seq_first(s), lay.seq_last(s)
        dcur = (fc_ref[...] + bc_ref[...]) * dsilu(pc_ref[...])
        extd_ref[0:HALO, :] = jnp.where(first, 0.0, (fp_ref[...] + bp_ref[...]) * dsilu(pp_ref[...]))
        extd_ref[HALO:HALO + tt, :] = dcur
        extd_ref[HALO + tt:, :] = jnp.where(last, 0.0, (fn_ref[...] + bn_ref[...]) * dsilu(pn_ref[...]))
        extx_ref[0:HALO, :] = jnp.where(first, 0.0, xp_ref[...])
        extx_ref[HALO:HALO + tt, :] = xc_ref[...]
        extx_ref[HALO + tt:, :] = jnp.where(last, 0.0, xn_ref[...])
        dx = jnp.zeros((tt, c), F32)
        rows = []
        for k in range(5):
            dx = dx + w_ref[k:k + 1, :] * extd_ref[pl.ds(HALO - (k - 2), tt), :]
            rows.append(jnp.sum(dcur * extx_ref[pl.ds(HALO + k - 2, tt), :], axis=0, keepdims=True))
        dx_ref[...] = dx.astype(dx_ref.dtype)
        rows.append(jnp.zeros((3, c), F32))
        _acc(dw_ref, s == 0, jnp.concatenate(rows, axis=0))
        _acc(db_ref, s == 0, jnp.sum(dcur, axis=0, keepdims=True))

    three = lambda cbx: [_halo_prev(lay, c, cbx), _tok(lay, c, cbx), _halo_next(lay, c, cbx)]
    ext = pltpu.VMEM((tt + 2 * HALO, c), F32)
    return _tok_call(name, body, mx.xw // c, lay.ns,
                     three(cb0) + three(0) + three(0) + three(0) + [_glob(8, c, 0)],
                     [_tok(lay, c), _glob(8, c, 0), _glob(1, c, 0)],
                     [jax.ShapeDtypeStruct((lay.ta, mx.xw), BF16), jax.ShapeDtypeStruct((8, mx.xw), F32),
                      jax.ShapeDtypeStruct((1, mx.xw), F32)],
                     (proj, proj, proj, pre, pre, pre, dact_f, dact_f, dact_f, dact_b, dact_b, dact_b, cw),
                     scratch=[ext, ext])


def _softplus(v):
    return jnp.maximum(v, 0.0) + jnp.log1p(jnp.exp(-jnp.abs(v)))


def _dt_fwd(lay, mx, proj, bias, name):
    cb = mx.off_dt // DT_LANES

    def body(p_ref, b_ref, dt_ref):
        dt_ref[...] = _softplus(p_ref[...] + b_ref[...])

    return _tok_call(name, body, 1, lay.ns, [_tok(lay, DT_LANES, cb), _glob(1, DT_LANES)], _tok(lay, DT_LANES),
                     jax.ShapeDtypeStruct((lay.ta, DT_LANES), F32), (proj, bias))


def _dt_bwd(lay, mx, proj, bias, parts, name):
    cb = mx.off_dt // DT_LANES

    def body(p_ref, b_ref, a_ref, b2_ref, c_ref, d_ref, dp_ref, db_ref):
        s = pl.program_id(1)
        ddt = (a_ref[...] + b2_ref[...]) + (c_ref[...] + d_ref[...])
        draw = ddt * jax.nn.sigmoid(p_ref[...] + b_ref[...])
        dp_ref[:, 0:DT_LANES] = draw.astype(dp_ref.dtype)
        dp_ref[:, DT_LANES:] = jnp.zeros((lay.tt, DT_PAD - DT_LANES), dp_ref.dtype)
        _acc(db_ref, s == 0, jnp.sum(draw, axis=0, keepdims=True))

    t = _tok(lay, DT_LANES)
    return _tok_call(name, body, 1, lay.ns, [_tok(lay, DT_LANES, cb), _glob(1, DT_LANES), t, t, t, t],
                     [_tok(lay, DT_PAD), _glob(1, DT_LANES)],
                     [jax.ShapeDtypeStruct((lay.ta, DT_PAD), BF16), jax.ShapeDtypeStruct((1, DT_LANES), F32)],
                     (proj, bias) + tuple(parts))


def _scan_mask(rev):
    r = lax.broadcasted_iota(jnp.int32, (CHUNK, CHUNK), 0)
    c = lax.broadcasted_iota(jnp.int32, (CHUNK, CHUNK), 1)
    return (r <= c) if rev else (r >= c)


def _split_bf16(x):
    hi = x.astype(BF16)
    return hi, (x - hi.astype(F32)).astype(BF16)


@functools.partial(jax.custom_vjp, nondiff_argnums=(0,))
def _cum_cols(rev, x):
    m = _scan_mask(rev).astype(BF16)
    hi, lo = _split_bf16(x)
    return jnp.dot(m, hi, preferred_element_type=F32) + jnp.dot(m, lo, preferred_element_type=F32)


_cum_cols.defvjp(lambda rev, x: (_cum_cols(rev, x), None), lambda rev, _, g: (_cum_cols(not rev, g),))


@functools.partial(jax.custom_vjp, nondiff_argnums=(0,))
def _cum_rows(rev, x):
    m = _scan_mask(not rev).astype(BF16)
    hi, lo = _split_bf16(x)
    return jnp.dot(hi, m, preferred_element_type=F32) + jnp.dot(lo, m, preferred_element_type=F32)


_cum_rows.defvjp(lambda rev, x: (_cum_rows(rev, x), None), lambda rev, _, g: (_cum_rows(not rev, g),))


@functools.partial(jax.custom_vjp, nondiff_argnums=(1,))
def _take_col(x, k):
    return x[:, k:k + 1]


def _take_col_bwd(k, _, g):
    lane = lax.broadcasted_iota(jnp.int32, (g.shape[0], DT_LANES), 1)
    return (jnp.where(lane == k, g, 0.0),)


_take_col.defvjp(lambda x, k: (x[:, k:k + 1], None), _take_col_bwd)


@functools.partial(jax.custom_vjp, nondiff_argnums=(1,))
def _take_row(x, k):
    return x[k:k + 1, :]


def _take_row_bwd(k, _, g):
    sub = lax.broadcasted_iota(jnp.int32, (DT_LANES, g.shape[1]), 0)
    return (jnp.where(sub == k, g, 0.0),)


_take_row.defvjp(lambda x, k: (x[k:k + 1, :], None), _take_row_bwd)


def _ssd_chunk(xh_pairs, bcs, ccs, dtc, dtr, a_row, a_col, st_pairs, *, rev, heads, col0):
    cs_c, cs_r, tot, scores = _ssd_shared(bcs, ccs, dtc, dtr, a_row, a_col, rev=rev)
    ppg = heads // (2 * SSD_GROUPS)
    ys, sts = [], []
    for g in range(SSD_GROUPS):
        y, st = _ssd_group(xh_pairs[g * ppg:(g + 1) * ppg], bcs[g], ccs[g], st_pairs[g], cs_c, cs_r, tot, dtc,
                           scores[g], rev=rev, col=col0 + 2 * ppg * g)
        ys.append(y)
        sts.append(st)
    return ys, sts


_NT = (((1,), (1,)), ((), ()))
_TN = (((0,), (0,)), ((), ()))


def _ssd_shared(bcs, ccs, dtc, dtr, a_row, a_col, *, rev):
    da_c = dtc * a_row
    cs_c = _cum_cols(rev, da_c)
    cs_r = _cum_rows(rev, dtr * a_col)
    tot = jnp.sum(da_c, axis=0, keepdims=True)
    scores = [lax.dot_general(ccs[g].astype(BF16), bcs[g].astype(BF16), _NT, preferred_element_type=F32)
              for g in range(SSD_GROUPS)]
    return cs_c, cs_r, tot, scores


def _ssd_group(xh_pairs, bc, cc, st, cs_c, cs_r, tot, dtc, score, *, rev, col):
    n = CHUNK
    mask = _scan_mask(rev)
    lane = lax.broadcasted_iota(jnp.int32, (n, DT_LANES), 1)
    sub = lax.broadcasted_iota(jnp.int32, (DT_LANES, n), 0)
    lane1 = lax.broadcasted_iota(jnp.int32, (1, DT_LANES), 1)
    left = lax.broadcasted_iota(jnp.int32, (n, 2 * HEAD_DIM), 1) < HEAD_DIM
    top = lax.broadcasted_iota(jnp.int32, (2 * HEAD_DIM, 1), 0) < HEAD_DIM
    xs_all, wst_all, ecs_all, edec_all, y_diag = [], [], [], [], []
    for p, xh in enumerate(xh_pairs):
        per = []
        for c in (col + 2 * p, col + 2 * p + 1):
            csv = jnp.sum(jnp.where(lane == c, cs_c, 0.0), axis=1, keepdims=True)
            csr = jnp.sum(jnp.where(sub == c, cs_r, 0.0), axis=0, keepdims=True)
            dtv = jnp.sum(jnp.where(lane == c, dtc, 0.0), axis=1, keepdims=True)
            tv = jnp.sum(jnp.where(lane1 == c, tot, 0.0), axis=1, keepdims=True)
            m = score * jnp.exp(jnp.where(mask, csv - csr, NEG_BIG))
            per.append((csv, dtv, tv, m))
        (cs1, dt1, t1, m1), (cs2, dt2, t2, m2) = per
        xs = xh * jnp.where(left, dt1, dt2)
        both = jnp.dot(jnp.concatenate([m1, m2], axis=0).astype(BF16), xs.astype(BF16), preferred_element_type=F32)
        y_diag.append(jnp.where(left, both[:n], both[n:]))
        xs_all.append(xs)
        ecs_all.append(jnp.where(left, jnp.exp(cs1), jnp.exp(cs2)))
        wst_all.append(jnp.where(left, jnp.exp(t1 - cs1), jnp.exp(t2 - cs2)))
        edec_all.append(jnp.where(top, jnp.exp(t1), jnp.exp(t2)))
    cat = lambda parts, axis: parts[0] if len(parts) == 1 else jnp.concatenate(parts, axis=axis)
    xs, wst, ecs = cat(xs_all, 1), cat(wst_all, 1), cat(ecs_all, 1)
    y_off = lax.dot_general(cc.astype(BF16), st.astype(BF16), _NT, preferred_element_type=F32) * ecs
    cst = lax.dot_general((xs * wst).astype(BF16), bc.astype(BF16), _TN, preferred_element_type=F32)
    return cat(y_diag, 1) + y_off, st * cat(edec_all, 0) + cst


class _Scan:
    def __init__(self, lay, rev):
        self.ncx, self.ncc, self.bl, self.rev = lay.seq // CHUNK, lay.clen // CHUNK, lay.bl, rev
        self.nct = self.ncx + self.ncc

    def chunk(self, b, pos):
        kc = (self.ncc - 1 - pos) if self.rev else pos
        kx = (self.ncx - 1 - (pos - self.ncc)) if self.rev else (pos - self.ncc)
        return jnp.where(pos < self.ncc, self.bl * self.ncx + b * self.ncc + kc, b * self.ncx + kx)


def _ssd_io(mx, x_ref, st_src):
    np_ = mx.heads // 2
    d = mx.d_ssm
    xh = [x_ref[:, 128 * p:128 * (p + 1)] for p in range(np_)]
    bcs = [x_ref[:, d + N_STATE * g:d + N_STATE * (g + 1)] for g in range(SSD_GROUPS)]
    ccs = [x_ref[:, d + mx.gn + N_STATE * g:d + mx.gn + N_STATE * (g + 1)] for g in range(SSD_GROUPS)]
    gw = d // SSD_GROUPS
    sts = [st_src[gw * g:gw * (g + 1), :] for g in range(SSD_GROUPS)]
    return xh, bcs, ccs, sts


def _ssd_fwd(lay, mx, xbc, dt, dtt, a_row, a_col, rev, name):
    sc = _Scan(lay, rev)
    col0 = mx.heads if rev else 0
    hp = mx.heads * HEAD_DIM

    def body(x_ref, dt_ref, dtt_ref, ar_ref, ac_ref, y_ref, hp_ref, st_ref):
        @pl.when(pl.program_id(1) == 0)
        def _():
            st_ref[...] = jnp.zeros_like(st_ref)

        hp_ref[...] = st_ref[...]
        xh, bcs, ccs, sts = _ssd_io(mx, x_ref, st_ref)
        ys, new = _ssd_chunk(xh, bcs, ccs, dt_ref[...], dtt_ref[...], ar_ref[...], ac_ref[...], sts,
                             rev=rev, heads=mx.heads, col0=col0)
        gw = mx.d_ssm // SSD_GROUPS
        for g in range(SSD_GROUPS):
            y_ref[:, gw * g:gw * (g + 1)] = ys[g]
            st_ref[gw * g:gw * (g + 1), :] = new[g]

    ch = sc.chunk
    return pl.pallas_call(
        body, grid=(lay.bl, sc.nct),
        in_specs=[pl.BlockSpec((CHUNK, mx.xw), lambda b, i: (ch(b, i), 0)),
                  pl.BlockSpec((CHUNK, DT_LANES), lambda b, i: (ch(b, i), 0)),
                  pl.BlockSpec((DT_LANES, CHUNK), lambda b, i: (0, ch(b, i))),
                  pl.BlockSpec((1, DT_LANES), lambda b, i: (0, 0)),
                  pl.BlockSpec((DT_LANES, 1), lambda b, i: (0, 0))],
        out_specs=[pl.BlockSpec((CHUNK, mx.d_ssm), lambda b, i: (ch(b, i), 0)),
                   pl.BlockSpec((hp, N_STATE), lambda b, i: (b * sc.nct + i, 0))],
        out_shape=[jax.ShapeDtypeStruct((lay.ta, mx.d_ssm), F32),
                   jax.ShapeDtypeStruct((lay.bl * sc.nct * hp, N_STATE), F32)],
        scratch_shapes=[pltpu.VMEM((hp, N_STATE), F32)], name=name, compiler_params=_cparams(2),
    )(xbc, dt, dtt, a_row, a_col)


def _ssd_bwd(lay, mx, xbc, dt, dtt, a_row, a_col, hprev, dy, dskip, rev, name):
    sc = _Scan(lay, rev)
    col0 = mx.heads if rev else 0
    hp = mx.heads * HEAD_DIM
    np_ = mx.heads // 2
    d = mx.d_ssm
    with_skip = dskip is not None

    def body(*refs):
        if with_skip:
            x_ref, dt_ref, dtt_ref, ar_ref, ac_ref, hp_ref, dy_ref, sk_ref = refs[:8]
            rest = refs[8:]
        else:
            x_ref, dt_ref, dtt_ref, ar_ref, ac_ref, hp_ref, dy_ref = refs[:7]
            rest = refs[7:]
        dx_ref, ddc_ref, ddr_ref, dar_ref, dac_ref, ds_ref = rest
        b, i = pl.program_id(0), pl.program_id(1)

        @pl.when(i == 0)
        def _():
            ds_ref[...] = jnp.zeros_like(ds_ref)

        xh, bcs, ccs, sts = _ssd_io(mx, x_ref, hp_ref)
        dtc = dt_ref[...]
        shared, vjp_shared = jax.vjp(functools.partial(_ssd_shared, rev=rev), bcs, ccs, dtc, dtt_ref[...],
                                     ar_ref[...], ac_ref[...])
        cs_c, cs_r, tot, scores = shared
        plus = lambda acc, v: v if acc is None else acc + v
        d_cs_c = d_cs_r = d_tot = ddc = None
        d_scores, dbc, dcc = [], [], []
        ppg = np_ // SSD_GROUPS
        gw = d // SSD_GROUPS
        for g in range(SSD_GROUPS):
            dyg = dy_ref[:, gw * g:gw * (g + 1)]
            fn = functools.partial(_ssd_group, rev=rev, col=col0 + 2 * ppg * g)
            _, vjp = jax.vjp(fn, xh[g * ppg:(g + 1) * ppg], bcs[g], ccs[g], sts[g], cs_c, cs_r, tot, dtc, scores[g])
            dxh, dbc_g, dcc_g, dst, dcs_c_g, dcs_r_g, dtot_g, ddc_g, dsc_g = vjp((dyg, ds_ref[gw * g:gw * (g + 1), :]))
            for q in range(ppg):
                p = g * ppg + q
                v = dxh[q]
                if with_skip:
                    v = v + dyg[:, 128 * q:128 * (q + 1)] * sk_ref[:, 128 * p:128 * (p + 1)]
                dx_ref[:, 128 * p:128 * (p + 1)] = v
            ds_ref[gw * g:gw * (g + 1), :] = dst
            d_cs_c, d_cs_r, d_tot, ddc = plus(d_cs_c, dcs_c_g), plus(d_cs_r, dcs_r_g), plus(d_tot, dtot_g), plus(ddc, ddc_g)
            d_scores.append(dsc_g)
            dbc.append(dbc_g)
            dcc.append(dcc_g)
        dbc_s, dcc_s, ddc_s, ddr, dar, dac = vjp_shared((d_cs_c, d_cs_r, d_tot, d_scores))
        ddc = ddc + ddc_s
        dbc = [dbc[g] + dbc_s[g] for g in range(SSD_GROUPS)]
        dcc = [dcc[g] + dcc_s[g] for g in range(SSD_GROUPS)]
        for g in range(SSD_GROUPS):
            dx_ref[:, d + N_STATE * g:d + N_STATE * (g + 1)] = dbc[g]
            dx_ref[:, d + mx.gn + N_STATE * g:d + mx.gn + N_STATE * (g + 1)] = dcc[g]
        ddc_ref[...] = ddc
        ddr_ref[...] = ddr
        first = jnp.logical_and(b == 0, i == 0)
        _acc(dar_ref, first, dar)
        _acc(dac_ref, first, dac)

    ch = lambda b, i: sc.chunk(b, sc.nct - 1 - i)
    in_specs = [pl.BlockSpec((CHUNK, mx.xw), lambda b, i: (ch(b, i), 0)),
                pl.BlockSpec((CHUNK, DT_LANES), lambda b, i: (ch(b, i), 0)),
                pl.BlockSpec((DT_LANES, CHUNK), lambda b, i: (0, ch(b, i))),
                pl.BlockSpec((1, DT_LANES), lambda b, i: (0, 0)),
                pl.BlockSpec((DT_LANES, 1), lambda b, i: (0, 0)),
                pl.BlockSpec((hp, N_STATE), lambda b, i: (b * sc.nct + sc.nct - 1 - i, 0)),
                pl.BlockSpec((CHUNK, d), lambda b, i: (ch(b, i), 0))]
    inputs = [xbc, dt, dtt, a_row, a_col, hprev, dy]
    if with_skip:
        in_specs.append(pl.BlockSpec((1, d), lambda b, i: (0, 0)))
        inputs.append(dskip)
    return pl.pallas_call(
        body, grid=(lay.bl, sc.nct), in_specs=in_specs,
        out_specs=[pl.BlockSpec((CHUNK, mx.xw), lambda b, i: (ch(b, i), 0)),
                   pl.BlockSpec((CHUNK, DT_LANES), lambda b, i: (ch(b, i), 0)),
                   pl.BlockSpec((DT_LANES, CHUNK), lambda b, i: (0, ch(b, i))),
                   pl.BlockSpec((1, DT_LANES), lambda b, i: (0, 0)),
                   pl.BlockSpec((DT_LANES, 1), lambda b, i: (0, 0))],
        out_shape=[jax.ShapeDtypeStruct((lay.ta, mx.xw), F32), jax.ShapeDtypeStruct((lay.ta, DT_LANES), F32),
                   jax.ShapeDtypeStruct((DT_LANES, lay.ta), F32), jax.ShapeDtypeStruct((1, DT_LANES), F32),
                   jax.ShapeDtypeStruct((DT_LANES, 1), F32)],
        scratch_shapes=[pltpu.VMEM((hp, N_STATE), F32)], name=name, compiler_params=_cparams(2),
    )(*inputs)


def _gate_f(yf, yb, xh, z, drow, nw):
    dd = yf.shape[-1]
    half = dd // SSD_GROUPS
    yz = (yf + yb + drow * xh) * _silu(z)
    lo = lax.broadcasted_iota(jnp.int32, yz.shape, 1) < half
    sq = yz * yz
    ms1 = jnp.sum(jnp.where(lo, sq, 0.0), axis=-1, keepdims=True) / half
    ms2 = jnp.sum(jnp.where(lo, 0.0, sq), axis=-1, keepdims=True) / half
    return yz * jnp.where(lo, lax.rsqrt(ms1 + EPS), lax.rsqrt(ms2 + EPS)) * nw


def _gate_fwd(lay, mx, yf, yb, xbc, proj, drow, nw, name):
    d = mx.d_ssm

    def body(yf_ref, yb_ref, xh_ref, z_ref, d_ref, w_ref, o_ref):
        o_ref[...] = _gate_f(yf_ref[...], yb_ref[...], xh_ref[...], z_ref[...], d_ref[...], w_ref[...]).astype(o_ref.dtype)

    t = _tok(lay, d)
    return _tok_call(name, body, 1, lay.nsx, [t, t, t, t, _glob(1, d), _glob(1, d)], t,
                     jax.ShapeDtypeStruct((lay.tx, d), BF16), (yf, yb, xbc, proj, drow, nw))


def _gate_bwd(lay, mx, yf, yb, xbc, proj, drow, nw, dcat, name):
    d = mx.d_ssm
    nsx = lay.nsx

    def body(yf_ref, yb_ref, xh_ref, z_ref, d_ref, w_ref, dc_ref, dy_ref, dz_ref, dd_ref, dw_ref):
        s = pl.program_id(1)

        @pl.when(s < nsx)
        def _():
            _, vjp = jax.vjp(_gate_f, yf_ref[...], yb_ref[...], xh_ref[...], z_ref[...], d_ref[...], w_ref[...])
            dyf, _, _, dz, dd, dw = vjp(dc_ref[...])
            dy_ref[...] = dyf
            dz_ref[...] = dz.astype(dz_ref.dtype)
            _acc(dd_ref, s == 0, dd)
            _acc(dw_ref, s == 0, dw)

        @pl.when(s >= nsx)
        def _():
            dy_ref[...] = jnp.zeros_like(dy_ref)
            dz_ref[...] = jnp.zeros_like(dz_ref)

    t = _tok(lay, d)
    return _tok_call(name, body, 1, lay.ns, [t, t, t, t, _glob(1, d), _glob(1, d), _tok(lay, d, clamp=nsx - 1)],
                     [t, t, _glob(1, d), _glob(1, d)],
                     [jax.ShapeDtypeStruct((lay.ta, d), F32), jax.ShapeDtypeStruct((lay.ta, d), BF16),
                      jax.ShapeDtypeStruct((1, d), F32), jax.ShapeDtypeStruct((1, d), F32)],
                     (yf, yb, xbc, proj, drow, nw, dcat))


def _glu_fwd(lay, mx, proj, name):
    d = mx.d_conv
    c = math.gcd(mx.off_glu, d)
    cb = mx.off_glu // c

    def body(a_ref, b_ref, o_ref):
        o_ref[...] = a_ref[...] * jax.nn.sigmoid(b_ref[...])

    return _tok_call(name, body, d // c, lay.nsx, [_tok(lay, c, cb), _tok(lay, c, cb + d // c)], _tok(lay, c),
                     jax.ShapeDtypeStruct((lay.tx, d), F32), (proj, proj))


def _glu_bwd(lay, mx, proj, du, name):
    d = mx.d_conv
    c = math.gcd(mx.off_glu, d)
    cb = mx.off_glu // c
    nsx = lay.nsx

    def body(a_ref, b_ref, du_ref, da_ref, db_ref):
        s = pl.program_id(1)

        @pl.when(s < nsx)
        def _():
            sg = jax.nn.sigmoid(b_ref[...])
            duv = du_ref[...]
            da_ref[...] = (duv * sg).astype(da_ref.dtype)
            db_ref[...] = (duv * a_ref[...] * sg * (1.0 - sg)).astype(db_ref.dtype)

        @pl.when(s >= nsx)
        def _():
            da_ref[...] = jnp.zeros_like(da_ref)
            db_ref[...] = jnp.zeros_like(db_ref)

    sh = jax.ShapeDtypeStruct((lay.ta, d), BF16)
    return _tok_call(name, body, d // c, lay.ns,
                     [_tok(lay, c, cb), _tok(lay, c, cb + d // c), _tok(lay, c, clamp=nsx - 1)],
                     [_tok(lay, c), _tok(lay, c)], [sh, sh], (proj, proj, du))


def _axial(lay, mx, u, dy, cw, cb, name):
    d, seq = mx.d_conv, lay.seq
    kw = cw.shape[0]
    pad = kw // 2
    c = _pick(d // 2, (256, 128))
    ncb = d // c
    zpad = GRID_W * pad
    zpad = -(-zpad // 8) * 8
    backward = dy is not None

    def shifted(ext_ref, off):
        return ext_ref[pl.ds(zpad + off, seq), :]

    def valid_row(off):
        col = lax.broadcasted_iota(jnp.int32, (seq, c), 0) % GRID_W
        return jnp.logical_and(col + off >= 0, col + off < GRID_W)

    def fill(ext_ref, v):
        ext_ref[0:zpad, :] = jnp.zeros((zpad, c), F32)
        ext_ref[zpad:zpad + seq, :] = v
        ext_ref[zpad + seq:, :] = jnp.zeros((zpad, c), F32)

    def conv(ext_ref, w_ref, is_row, sign):
        acc = jnp.zeros((seq, c), F32)
        for k in range(kw):
            off = sign * ((k - pad) if is_row else GRID_W * (k - pad))
            v = shifted(ext_ref, off)
            if is_row:
                v = jnp.where(valid_row(off), v, 0.0)
            acc = acc + w_ref[k:k + 1, :] * v
        return acc

    def fwd_body(u_ref, w_ref, b_ref, o_ref, ext_ref):
        j = pl.program_id(0)
        fill(ext_ref, u_ref[...])

        @pl.when(j < ncb // 2)
        def _():
            o_ref[...] = conv(ext_ref, w_ref, True, 1) + b_ref[...]

        @pl.when(j >= ncb // 2)
        def _():
            o_ref[...] = conv(ext_ref, w_ref, False, 1) + b_ref[...]

    def bwd_body(u_ref, dy_ref, w_ref, du_ref, dw_ref, db_ref, extu_ref, extd_ref):
        j, b = pl.program_id(0), pl.program_id(1)
        dyv = dy_ref[...]
        fill(extu_ref, u_ref[...])
        fill(extd_ref, dyv)

        def grads(is_row):
            du_ref[...] = conv(extd_ref, w_ref, is_row, -1)
            rows = []
            for k in range(kw):
                off = (k - pad) if is_row else GRID_W * (k - pad)
                v = shifted(extu_ref, off)
                if is_row:
                    v = jnp.where(valid_row(off), v, 0.0)
                rows.append(jnp.sum(dyv * v, axis=0, keepdims=True))
            _acc(dw_ref, b == 0, jnp.concatenate(rows, axis=0))

        @pl.when(j < ncb // 2)
        def _():
            grads(True)

        @pl.when(j >= ncb // 2)
        def _():
            grads(False)

        _acc(db_ref, b == 0, jnp.sum(dyv, axis=0, keepdims=True))

    seq_spec = pl.BlockSpec((seq, c), lambda j, b: (b, j))
    w_spec = pl.BlockSpec((kw, c), lambda j, b: (0, j))
    b_spec = pl.BlockSpec((1, c), lambda j, b: (0, j))
    ext = pltpu.VMEM((seq + 2 * zpad, c), F32)
    if not backward:
        return pl.pallas_call(fwd_body, grid=(ncb, lay.bl), in_specs=[seq_spec, w_spec, b_spec], out_specs=seq_spec,
                              out_shape=jax.ShapeDtypeStruct((lay.tx, d), F32), scratch_shapes=[ext], name=name,
                              compiler_params=_cparams(2))(u, cw, cb)
    return pl.pallas_call(bwd_body, grid=(ncb, lay.bl), in_specs=[seq_spec, seq_spec, w_spec],
                          out_specs=[seq_spec, w_spec, b_spec],
                          out_shape=[jax.ShapeDtypeStruct((lay.tx, d), F32), jax.ShapeDtypeStruct((kw, d), F32),
                                     jax.ShapeDtypeStruct((1, d), F32)],
                          scratch_shapes=[ext, ext], name=name, compiler_params=_cparams(2))(u, dy, cw)


def _ln_silu_f(u, w, b):
    mu = jnp.mean(u, axis=-1, keepdims=True)
    var = jnp.mean(jnp.square(u - mu), axis=-1, keepdims=True)
    return _silu((u - mu) * lax.rsqrt(var + EPS) * w + b)


def _ln_fwd(lay, mx, u, w, b, name):
    d = mx.d_conv

    def body(u_ref, w_ref, b_ref, o_ref):
        o_ref[...] = _ln_silu_f(u_ref[...], w_ref[...], b_ref[...]).astype(o_ref.dtype)

    return _tok_call(name, body, 1, lay.nsx, [_tok(lay, d), _glob(1, d), _glob(1, d)], _tok(lay, d),
                     jax.ShapeDtypeStruct((lay.tx, d), BF16), (u, w, b))


def _ln_bwd(lay, mx, u, w, b, dcat, name):
    d = mx.d_conv

    def body(u_ref, w_ref, b_ref, dc_ref, du_ref, dw_ref, db_ref):
        s = pl.program_id(1)
        _, vjp = jax.vjp(_ln_silu_f, u_ref[...], w_ref[...], b_ref[...])
        du, dw, db = vjp(dc_ref[...])
        du_ref[...] = du
        _acc(dw_ref, s == 0, dw)
        _acc(db_ref, s == 0, db)

    return _tok_call(name, body, 1, lay.nsx, [_tok(lay, d), _glob(1, d), _glob(1, d), _tok(lay, d, 1)],
                     [_tok(lay, d), _glob(1, d), _glob(1, d)],
                     [jax.ShapeDtypeStruct((lay.tx, d), F32), jax.ShapeDtypeStruct((1, d), F32),
                      jax.ShapeDtypeStruct((1, d), F32)], (u, w, b, dcat))


def _gate_tile(dff):
    return dff // 2 if (dff // 2) % 128 == 0 else dff


def _interleave(wg, wu, ft):
    return jnp.concatenate([t[:, k:k + ft] for k in range(0, wg.shape[1], ft) for t in (wg, wu)], axis=1)


def _deinterleave(wgu, ft):
    n = wgu.shape[1]
    gate = jnp.concatenate([wgu[:, k:k + ft] for k in range(0, n, 2 * ft)], axis=1)
    up = jnp.concatenate([wgu[:, k + ft:k + 2 * ft] for k in range(0, n, 2 * ft)], axis=1)
    return gate, up


def _ffn_fwd(lay, nseg, x, nw, modv, k0, wts, tag, ride=None):
    wgu, wd, ft = wts
    h = _norm_mod_fwd(lay, nseg, x, nw, modv, k0, tag + "_norm")
    t = h.shape[0]

    def act(acc):
        g, u = acc[:, :ft], acc[:, ft:]
        sg = jax.nn.sigmoid(g)
        sl = g * sg
        return jnp.concatenate([u * (sg * (1.0 + g * (1.0 - sg))), sl], axis=1), sl * u

    res = _mm(h, wgu, "nn", None, tag + "_gu", tn=2 * ft, tm=256 if t % 256 == 0 else None,
              epilogue=act, outs=[(2 * ft, BF16), (ft, BF16)], ride=ride)
    (s, a), rode = res if ride else (res, None)
    o = _mm(a, wd, "nn", F32, tag + "_down")
    y = _resid_fwd(lay, nseg, x, o, modv, k0 + 2, 0.5, tag + "_res")
    return y, (x, h, s, a, o), rode


def _ffn_bwd(lay, nseg, dy, saved, nw, modv, k0, wts, tag, nout=None, rides=(None, None)):
    wgu, wd, ft = wts
    x, h, s, a, o = saved
    do, dgate = _resid_bwd(lay, nseg, dy, o, modv, k0 + 2, 0.5, tag + "_dres")

    def through_act(da, s_tile):
        return (jnp.concatenate([da, da], axis=1) * s_tile.astype(F32),)

    (dgu,) = _mm(do, wd.T, "nn", None, tag + "_da", tn=ft, tm=_big_tile(do.shape[0], 1024), extras=[(s, 2 * ft)],
                 epilogue=through_act, outs=[(2 * ft, BF16)])
    dwd = _mm(a, do, "tn", F32, tag + "_dwd")
    dh = _mm(dgu, wgu, "nt", F32, tag + "_dh", ride=rides[0])
    dwgu = _mm(h, dgu, "tn", F32, tag + "_dwgu", ride=rides[1])
    (dh, rode_a), (dwgu, rode_b) = (dh if rides[0] else (dh, None)), (dwgu if rides[1] else (dwgu, None))
    dx, dnw, dss = _norm_mod_bwd(lay, nseg, nseg, x, nw, modv, k0, dh, dy, tag + "_dnorm", nout=nout)
    return dx, (dwgu, dwd), dnw, jnp.concatenate([dss, dgate], axis=1), (rode_a, rode_b)


def _local_step(lay, mx, xa, target, modv, w, exch=None):
    d, bl = lay.d, lay.bl
    g = {}
    xa1, ffn1, late = _ffn_fwd(lay, lay.ns, xa, w["norm_ffn1"], modv, 0, w["ffn1"], "ffn1",
                               ride=exch.late_weights if exch else None)
    if exch:
        w = {**w, **exch.unpack_late(late)}
    ha = _norm_mod_fwd(lay, lay.ns, xa1, w["norm_mix"], modv, 3, "mix_norm")
    proj = _mm(ha, w["w_in"], "nn", F32, "mix_in")
    pre, xbc = _conv5_fwd(lay.fine(), mx, proj, w["conv_w"], w["conv_b"], "mix_conv")
    dt = _dt_fwd(lay, mx, proj, w["dt_bias"], "mix_dt")
    dtt = dt.T
    yf, hpf = _ssd_fwd(lay, mx, xbc, dt, dtt, w["a_row"], w["a_col"], False, "ssd_f")
    yb, hpb = _ssd_fwd(lay, mx, xbc, dt, dtt, w["a_row"], w["a_col"], True, "ssd_b")
    cat_y = _gate_fwd(lay, mx, yf, yb, xbc, proj, w["d_row"], w["ssm_norm_w"], "mix_gate")
    u0 = _glu_fwd(lay, mx, proj, "mix_glu")
    uc = _axial(lay, mx, u0, None, w["cconv_w"], w["cconv_b"], "mix_axial")
    cat_u = _ln_fwd(lay, mx, uc, w["ln_w"], w["ln_b"], "mix_ln")
    cat = jnp.concatenate([cat_y, cat_u], axis=1)
    mix = _mm(cat, w["w_out"], "nn", F32, "mix_out")
    x2 = _resid_fwd(lay, lay.nsx, xa1, mix, modv, 5, 1.0, "mix_res")
    x3, ffn2, _ = _ffn_fwd(lay, lay.nsx, x2, w["norm_ffn2"], modv, 6, w["ffn2"], "ffn2")
    loss, dx3, g["final_norm"] = _final_loss(lay, x3, w["final_norm"], target, "loss")
    dx2, g["ffn2"], g["norm_ffn2"], dmod2, _ = _ffn_bwd(lay, lay.nsx, dx3, ffn2, w["norm_ffn2"], modv, 6, w["ffn2"], "ffn2")
    dmix, dg2 = _resid_bwd(lay, lay.nsx, dx2, mix, modv, 5, 1.0, "mix_dres")
    dcat = _mm(dmix, w["w_out"], "nt", F32, "mix_dcat")
    g["w_out"] = _mm(cat, dmix, "tn", F32, "mix_dwout")
    duc, g["ln_w"], g["ln_b"] = _ln_bwd(lay, mx, uc, w["ln_w"], w["ln_b"], dcat, "mix_dln")
    du0, g["cconv_w"], g["cconv_b"] = _axial(lay, mx, u0, duc, w["cconv_w"], None, "mix_daxial")
    dglu_a, dglu_b = _glu_bwd(lay, mx, proj, du0, "mix_dglu")
    dyssd, dz, g["d_row"], g["ssm_norm_w"] = _gate_bwd(lay, mx, yf, yb, xbc, proj, w["d_row"], w["ssm_norm_w"], dcat,
                                                       "mix_dgate")
    dxf, ddcf, ddrf, darf, dacf = _ssd_bwd(lay, mx, xbc, dt, dtt, w["a_row"], w["a_col"], hpf, dyssd, w["d_row"],
                                           False, "ssd_df")
    dxb, ddcb, ddrb, darb, dacb = _ssd_bwd(lay, mx, xbc, dt, dtt, w["a_row"], w["a_col"], hpb, dyssd, None,
                                           True, "ssd_db")
    g["a_row"] = (darf + darb) + (dacf + dacb).T
    dxbc, g["conv_w"], g["conv_b"] = _conv5_bwd(lay.fine(), mx, proj, pre, dxf, dxb, w["conv_w"], "mix_dconv")
    ddtraw, g["dt_bias"] = _dt_bwd(lay, mx, proj, w["dt_bias"], (ddcf, ddcb, ddrf.T, ddrb.T), "mix_ddt")
    dproj = jnp.concatenate([dz, ddtraw, dxbc, dglu_a, dglu_b], axis=1)
    dha = _mm(dproj, w["w_in"], "nt", F32, "mix_dha")
    g["w_in"] = _mm(ha, dproj, "tn", F32, "mix_dwin")
    dxa1, g["norm_mix"], dss_mix = _norm_mod_bwd(lay, lay.ns, lay.nsx, xa1, w["norm_mix"], modv, 3, dha, dx2, "mix_dnorm")
    rides = exch.early_grads(g) if exch else (None, None)
    dx, g["ffn1"], g["norm_ffn1"], dmod1, rode = _ffn_bwd(lay, lay.ns, dxa1, ffn1, w["norm_ffn1"], modv, 0, w["ffn1"],
                                                         "ffn1", nout=lay.nsx, rides=rides)
    if exch:
        exch.take_early(rode)
    zrow = lambda t: jnp.concatenate([t, jnp.zeros((1,) + t.shape[1:], F32)], axis=0)
    dmodv = jnp.concatenate([dmod1, dss_mix, zrow(dg2), zrow(dmod2)], axis=1)
    return loss, dx, g, dmodv


class _GatherRide:
    def __init__(self, xs):
        self.arrays = list(xs)
        self.na = len(xs)
        self.out_shapes = [jax.ShapeDtypeStruct((N_DEV,) + tuple(x.shape), x.dtype) for x in xs]
        self.sems = [pltpu.SemaphoreType.DMA((7 * self.na,)), pltpu.SemaphoreType.DMA((7 * self.na,)),
                     pltpu.SemaphoreType.DMA((self.na,))]

    def _plan(self, x_refs, out_refs, send_sems, recv_sems, local_sems):
        mx_, my_, mc_ = lax.axis_index("x"), lax.axis_index("y"), lax.axis_index("c")
        me, sibling = (mx_, my_, mc_), (mx_, my_, 1 - mc_)
        chips = [(1 - mx_, my_), (mx_, 1 - my_), (1 - mx_, 1 - my_)]

        def slot(a, px, py, pc):
            return out_refs[a].at[4 * px + 2 * py + pc]

        def copy(a, k, block, to, own=False):
            return pltpu.make_async_remote_copy(
                src_ref=x_refs[a] if own else slot(a, *block), dst_ref=slot(a, *block),
                send_sem=send_sems.at[7 * a + k], recv_sem=recv_sems.at[7 * a + k], device_id=to, device_id_type=MESH)

        mine = [pltpu.make_async_copy(x_refs[a], slot(a, *me), local_sems.at[a]) for a in range(self.na)]
        first = []
        for a in range(self.na):
            first.append(copy(a, 0, me, sibling, own=True))
            first += [copy(a, 1 + j, me, (*chip, mc_), own=True) for j, chip in enumerate(chips)]
        return me, sibling, chips, mc_, copy, mine, first

    def start(self, x_refs, out_refs, send_sems, recv_sems, local_sems):
        *_, mine, first = self._plan(x_refs, out_refs, send_sems, recv_sems, local_sems)
        for cp in mine + first:
            cp.start()

    def finish(self, x_refs, out_refs, send_sems, recv_sems, local_sems):
        me, sibling, chips, mc_, copy, mine, first = self._plan(x_refs, out_refs, send_sems, recv_sems, local_sems)
        passed = []
        for j, chip in enumerate(chips):
            for a in range(self.na):
                copy(a, 1 + j, (*chip, mc_), me).wait_recv()
                fwd = copy(a, 4 + j, (*chip, mc_), sibling)
                fwd.start()
                passed.append(fwd)
        for a in range(self.na):
            copy(a, 0, sibling, me).wait_recv()
            for j, chip in enumerate(chips):
                copy(a, 4 + j, (*chip, 1 - mc_), me).wait_recv()
        for cp in first + passed:
            cp.wait_send()
        for cp in mine:
            cp.wait()


def _exchange(ride, name, in_hbm=True):
    n_in, n_out = len(ride.arrays), len(ride.out_shapes)

    def body(*refs):
        ins, outs, sems = refs[:n_in], refs[n_in:n_in + n_out], refs[n_in + n_out:]
        ride.start(ins, outs, *sems)
        ride.finish(ins, outs, *sems)

    space = pl.BlockSpec(memory_space=pl.ANY if in_hbm else pltpu.VMEM)
    return pl.pallas_call(body, out_shape=list(ride.out_shapes), in_specs=[space] * n_in, out_specs=[space] * n_out,
                          scratch_shapes=list(ride.sems), name=name)(*ride.arrays)


def _all_gather(xs, name, in_hbm):
    return _exchange(_GatherRide(xs), name, in_hbm)


N_CHIPS = 4


def _swap_sibling(gs, name):
    na = len(gs)

    def body(*refs):
        g_refs, out_refs, send_sems, recv_sems = refs[:na], refs[na:2 * na], refs[2 * na], refs[2 * na + 1]
        mx_, my_, mc_ = lax.axis_index("x"), lax.axis_index("y"), lax.axis_index("c")
        copies = [pltpu.make_async_remote_copy(
            src_ref=g_refs[a].at[k, 1 - mc_], dst_ref=out_refs[a].at[k], send_sem=send_sems.at[N_CHIPS * a + k],
            recv_sem=recv_sems.at[N_CHIPS * a + k], device_id=(mx_, my_, 1 - mc_), device_id_type=MESH)
            for a in range(na) for k in range(N_CHIPS)]
        for cp in copies:
            cp.start()
        for cp in copies:
            cp.wait_recv()
        for cp in copies:
            cp.wait_send()

    return pl.pallas_call(
        body, out_shape=[jax.ShapeDtypeStruct((N_CHIPS,) + tuple(g.shape[2:]), g.dtype) for g in gs],
        in_specs=[pl.BlockSpec(memory_space=pl.ANY)] * na, out_specs=[pl.BlockSpec(memory_space=pl.ANY)] * na,
        scratch_shapes=[pltpu.SemaphoreType.DMA((N_CHIPS * na,)), pltpu.SemaphoreType.DMA((N_CHIPS * na,))], name=name,
    )(*gs)


def _row_tile(r, n):
    if r * n * 4 <= (1 << 20):
        return r
    for t in (1024, 512, 256, 128, 64, 32, 16, 8):
        if r % t == 0 and t * n * 4 <= (1 << 20):
            return t
    return r


def _pair_add(place, g, got, name):
    _, _, r, n = g.shape
    tr = _row_tile(r, n)

    def body(place_ref, g_ref, got_ref, o_ref, ob_ref):
        s = g_ref[...] + got_ref[...]
        o_ref[...] = s
        ob_ref[...] = s.astype(ob_ref.dtype)

    blk = pl.BlockSpec((None, tr, n), lambda k, i, pr: (k, i, 0))
    grid_spec = pltpu.PrefetchScalarGridSpec(
        num_scalar_prefetch=1, grid=(N_CHIPS, r // tr),
        in_specs=[pl.BlockSpec((None, None, tr, n), lambda k, i, pr: (k, pr[0], i, 0)), blk], out_specs=[blk, blk])
    return pl.pallas_call(body, grid_spec=grid_spec,
                          out_shape=[jax.ShapeDtypeStruct((N_CHIPS, r, n), F32), jax.ShapeDtypeStruct((N_CHIPS, r, n), BF16)],
                          name=name, compiler_params=_cparams(2))(place, g, got)


class _ChipSwapRide:
    def __init__(self, ps):
        self.arrays = list(ps)
        self.na = len(ps)
        self.out_shapes = [jax.ShapeDtypeStruct((N_CHIP_PEERS,) + tuple(p.shape[1:]), p.dtype) for p in ps]
        self.sems = [pltpu.SemaphoreType.DMA((N_CHIP_PEERS * self.na,)), pltpu.SemaphoreType.DMA((N_CHIP_PEERS * self.na,))]

    def _copies(self, p_refs, out_refs, send_sems, recv_sems):
        mx_, my_, mc_ = lax.axis_index("x"), lax.axis_index("y"), lax.axis_index("c")
        chips = [(1 - mx_, my_), (mx_, 1 - my_), (1 - mx_, 1 - my_)]
        return [pltpu.make_async_remote_copy(
            src_ref=p_refs[a].at[2 * cx + cy], dst_ref=out_refs[a].at[j], send_sem=send_sems.at[N_CHIP_PEERS * a + j],
            recv_sem=recv_sems.at[N_CHIP_PEERS * a + j], device_id=(cx, cy, mc_), device_id_type=MESH)
            for a in range(self.na) for j, (cx, cy) in enumerate(chips)]

    def start(self, p_refs, out_refs, send_sems, recv_sems):
        for cp in self._copies(p_refs, out_refs, send_sems, recv_sems):
            cp.start()

    def finish(self, p_refs, out_refs, send_sems, recv_sems):
        copies = self._copies(p_refs, out_refs, send_sems, recv_sems)
        for cp in copies:
            cp.wait_recv()
        for cp in copies:
            cp.wait_send()


def _sum_lead(x, name):
    k, r, n = x.shape
    tr = _row_tile(r, n * k)

    def body(x_ref, o_ref):
        acc = x_ref[0]
        for i in range(1, k):
            acc = acc + x_ref[i]
        o_ref[...] = acc

    return pl.pallas_call(body, grid=(r // tr,), in_specs=[pl.BlockSpec((k, tr, n), lambda i: (0, i, 0))],
                          out_specs=pl.BlockSpec((tr, n), lambda i: (i, 0)),
                          out_shape=jax.ShapeDtypeStruct((r, n), x.dtype), name=name, compiler_params=_cparams(1))(x)


def _adamw(place, w, parts, m, v, name):
    shape = w.shape
    cols = shape[-1]
    rows = math.prod(shape[:-1])
    to2 = lambda t: t.reshape(rows, cols)
    tr = _row_tile(rows, cols) if rows * cols * 4 > (1 << 20) else rows
    npart = len(parts)
    spec = pl.BlockSpec((tr, cols), lambda i, pr: (i, 0))
    part_specs, part_args = [], []
    for piece in parts:
        if isinstance(piece, tuple):
            stack, k = piece
            part_args.append(stack.reshape(stack.shape[0], rows, cols))
            if k == "chip":
                part_specs.append(pl.BlockSpec((None, tr, cols), lambda i, pr: (pr[1], i, 0)))
            else:
                part_specs.append(pl.BlockSpec((None, tr, cols), functools.partial(lambda i, pr, kk: (kk, i, 0), kk=k)))
        else:
            part_args.append(to2(piece))
            part_specs.append(spec)

    def body(place_ref, *refs):
        w_ref, m_ref, v_ref = refs[0], refs[1 + npart], refs[2 + npart]
        g_ref, d_ref, nm_ref, nv_ref = refs[3 + npart:]
        g = refs[1][...].astype(F32)
        for q in range(1, npart):
            g = g + refs[1 + q][...].astype(F32)
        mm = ADAM_B1 * m_ref[...] + (1.0 - ADAM_B1) * g
        vv = ADAM_B2 * v_ref[...] + (1.0 - ADAM_B2) * jnp.square(g)
        m_hat = mm / (1.0 - ADAM_B1 ** ADAM_STEP)
        v_hat = vv / (1.0 - ADAM_B2 ** ADAM_STEP)
        g_ref[...] = g
        d_ref[...] = -ADAM_LR * (m_hat / (jnp.sqrt(v_hat) + ADAM_EPS) + ADAM_WD * w_ref[...])
        nm_ref[...] = mm
        nv_ref[...] = vv

    sh = jax.ShapeDtypeStruct((rows, cols), F32)
    grid_spec = pltpu.PrefetchScalarGridSpec(num_scalar_prefetch=1, grid=(rows // tr,),
                                             in_specs=[spec] + part_specs + [spec, spec], out_specs=[spec] * 4)
    outs = pl.pallas_call(body, grid_spec=grid_spec, out_shape=[sh] * 4, name=name, compiler_params=_cparams(1),
                          )(place, to2(w), *part_args, to2(m), to2(v))
    return tuple(o.reshape(shape) for o in outs)


def _packed_rows(n, width):
    return -(-n // (8 * width)) * 8


def _pack_rows(items, width):
    rows = []
    for t in items:
        flat = t.reshape(-1)
        n = flat.shape[0]
        k = _packed_rows(n, width)
        if k * width > n:
            flat = jnp.concatenate([flat, jnp.zeros((k * width - n,), t.dtype)])
        rows.append(flat.reshape(k, width))
    return jnp.concatenate(rows, axis=0)


def _unpack_rows(packed, shapes, lead=()):
    width = packed.shape[-1]
    out, r = [], 0
    for sh in shapes:
        n = math.prod(sh)
        k = _packed_rows(n, width)
        piece = packed[..., r:r + k, :].reshape(tuple(lead) + (k * width,))[..., :n]
        out.append(piece.reshape(tuple(lead) + tuple(sh)))
        r += k
    return out


def _cols_full(t):
    return jnp.transpose(t, (1, 0, 2)).reshape(t.shape[1], -1)


def _cols_shards(t):
    d = t.shape[0]
    return jnp.transpose(t.reshape(d, N_DEV, -1), (1, 0, 2))


BIG = ("ffn1_gate", "ffn1_up", "ffn1_down", "w_in", "w_out", "ffn2_gate", "ffn2_up", "ffn2_down")
ROW_SHARDED = ("ffn1_down", "w_out", "ffn2_down")


def kernel(x, c, ctx, c_ctx, w_mod, b_mod, norm_ffn1, ffn1_gate, ffn1_up, ffn1_down, norm_mix, w_in, ssm_conv_w, ssm_conv_b, dt_bias_fwd, dt_bias_bwd, a_log_fwd, a_log_bwd, ssm_d, ssm_norm_w, cconv_w, cconv_b, cconv_ln_w, cconv_ln_b, w_out, norm_ffn2, ffn2_gate, ffn2_up, ffn2_down, final_norm, loss_target, m_c_ctx, m_w_mod, m_b_mod, m_norm_ffn1, m_ffn1_gate, m_ffn1_up, m_ffn1_down, m_norm_mix, m_w_in, m_ssm_conv_w, m_ssm_conv_b, m_dt_bias_fwd, m_dt_bias_bwd, m_a_log_fwd, m_a_log_bwd, m_ssm_d, m_ssm_norm_w, m_cconv_w, m_cconv_b, m_cconv_ln_w, m_cconv_ln_b, m_w_out, m_norm_ffn2, m_ffn2_gate, m_ffn2_up, m_ffn2_down, m_final_norm, v_c_ctx, v_w_mod, v_b_mod, v_norm_ffn1, v_ffn1_gate, v_ffn1_up, v_ffn1_down, v_norm_mix, v_w_in, v_ssm_conv_w, v_ssm_conv_b, v_dt_bias_fwd, v_dt_bias_bwd, v_a_log_fwd, v_a_log_bwd, v_ssm_d, v_ssm_norm_w, v_cconv_w, v_cconv_b, v_cconv_ln_w, v_cconv_ln_b, v_w_out, v_norm_ffn2, v_ffn2_gate, v_ffn2_up, v_ffn2_down, v_final_norm):
    args = dict(locals())
    names = ("c_ctx", "w_mod", "b_mod", "norm_ffn1", "ffn1_gate", "ffn1_up", "ffn1_down", "norm_mix", "w_in",
             "ssm_conv_w", "ssm_conv_b", "dt_bias_fwd", "dt_bias_bwd", "a_log_fwd", "a_log_bwd", "ssm_d", "ssm_norm_w",
             "cconv_w", "cconv_b", "cconv_ln_w", "cconv_ln_b", "w_out", "norm_ffn2", "ffn2_gate", "ffn2_up",
             "ffn2_down", "final_norm")
    wts = {n: args[n] for n in names}
    bl, seq, d = x.shape
    clen = ctx.shape[1]
    heads = dt_bias_fwd.shape[1]
    ft = _gate_tile(ffn1_gate.shape[2] * N_DEV)
    lay = _Lay(bl, seq, clen, d)
    mx = _Mix(d, heads)
    nb = bl * N_DEV
    me = 4 * lax.axis_index("x") + 2 * lax.axis_index("y") + lax.axis_index("c")
    mcols = w_mod.shape[2]
    n_ctx_mod = 5 * d

    place = jnp.stack([lax.axis_index("c"), 2 * lax.axis_index("x") + lax.axis_index("y")]).astype(jnp.int32)

    small_shapes = [(bl, d), ssm_conv_w.shape[1:], cconv_w.shape[1:]]
    (g1,) = _all_gather([_pack_rows([c, ssm_conv_w, cconv_w], d)], "gather_small", False)
    c_g, conv_g, cconv_g = _unpack_rows(g1, small_shapes, (N_DEV,))
    c_all = c_g.reshape(nb, d)
    conv_w_full = jnp.transpose(conv_g, (1, 0, 2)).reshape(conv_g.shape[1], -1)
    cconv_w_full = jnp.transpose(cconv_g, (1, 0, 2)).reshape(cconv_g.shape[1], -1)

    s_all = jnp.concatenate([_silu(c_all), _silu(c_ctx)[None, :], jnp.zeros((7, d), F32)], axis=0)
    mod_cols = _mm(s_all, w_mod[0], "nn", F32, "mod_cols")
    (g2,) = _all_gather([mod_cols], "gather_mod", False)
    mod_all = _cols_full(g2) + b_mod
    mod_mine = jnp.concatenate([lax.dynamic_slice_in_dim(mod_all, me * bl, bl, axis=0), mod_all[nb:nb + 1]], axis=0)
    modv = mod_mine.reshape(bl + 1, N_MOD, d)

    hh = 2 * heads
    shard16 = lambda n: wts[n][0].astype(BF16)

    def ffn_weights(gate, up, down):
        return _interleave(_cols_full(gate), _cols_full(up), ft), down.reshape(-1, d), ft

    def grads_by_dest(name, grad):
        if name == "w_in":
            grad = _cols_shards(jnp.concatenate([grad[:, :d], grad[:, mx.off_x:mx.off_glu],
                                                 grad[:, mx.off_dt:mx.off_dt + hh], grad[:, mx.off_glu:]], axis=1))
        elif name.endswith("_down") or name == "w_out":
            grad = grad.reshape((N_DEV,) + tuple(wts[name].shape[1:]))
        else:
            grad = _cols_shards(grad)
        return grad.reshape((N_CHIPS, 2) + tuple(grad.shape[1:]))

    def ffn_grads(tag, pair):
        gate, up = _deinterleave(pair[0], ft)
        return {tag + "_gate": gate, tag + "_up": up, tag + "_down": pair[1]}

    def to_chip_sums(named):
        names_ = list(named)
        by_dest = [grads_by_dest(n, named[n]) for n in names_]
        got = _swap_sibling(by_dest, "rs_sibling_" + names_[0])
        return {n: _pair_add(place, t, s, "rs_pair_add_" + n) for n, t, s in zip(names_, by_dest, got)}

    class _Overlap:
        late_names = ("w_in", "w_out", "ffn2_gate", "ffn2_up", "ffn2_down")
        late_weights = _GatherRide([shard16(n) for n in late_names])
        sums, arrived = {}, {}

        def unpack_late(self, outs):
            full = dict(zip(self.late_names, outs))
            w_in_f = _cols_full(full["w_in"])
            w_in_p = jnp.concatenate([w_in_f[:, :d], w_in_f[:, mx.ref_dt:mx.ref_glu], jnp.zeros((d, DT_PAD - hh), BF16),
                                      w_in_f[:, mx.ref_x:mx.ref_dt], w_in_f[:, mx.ref_glu:]], axis=1)
            return {"w_in": w_in_p, "w_out": full["w_out"].reshape(-1, d),
                    "ffn2": ffn_weights(full["ffn2_gate"], full["ffn2_up"], full["ffn2_down"])}

        def early_grads(self, g):
            self.sums = to_chip_sums({**ffn_grads("ffn2", g["ffn2"]), "w_in": g["w_in"], "w_out": g["w_out"]})
            self.groups = (("ffn2_gate", "ffn2_up", "ffn2_down"), ("w_in", "w_out"))
            return tuple(_ChipSwapRide([self.sums[n][1] for n in grp]) for grp in self.groups)

        def take_early(self, rode):
            for grp, outs in zip(self.groups, rode):
                self.arrived.update(zip(grp, outs))

    exch = _Overlap()
    full = dict(zip(BIG[:3], _all_gather([shard16(n) for n in BIG[:3]], "gather_weights", True)))
    lanes_pad = lambda a, b: jnp.concatenate([a, b, jnp.zeros((1, DT_LANES - hh), F32)], axis=1)
    a_vals = lanes_pad(-jnp.exp(a_log_fwd), -jnp.exp(a_log_bwd))
    w = {
        "norm_ffn1": norm_ffn1, "norm_mix": norm_mix, "norm_ffn2": norm_ffn2, "final_norm": final_norm[None, :],
        "ffn1": ffn_weights(full["ffn1_gate"], full["ffn1_up"], full["ffn1_down"]),
        "conv_w": jnp.concatenate([conv_w_full, jnp.zeros((3, mx.xw), F32)], axis=0), "conv_b": ssm_conv_b,
        "dt_bias": lanes_pad(dt_bias_fwd, dt_bias_bwd), "a_row": a_vals, "a_col": a_vals.T,
        "d_row": jnp.repeat(ssm_d, HEAD_DIM, axis=1), "ssm_norm_w": ssm_norm_w,
        "cconv_w": cconv_w_full, "cconv_b": cconv_b, "ln_w": cconv_ln_w, "ln_b": cconv_ln_b,
    }

    xa = jnp.concatenate([x.reshape(bl * seq, d), ctx.reshape(bl * clen, d)], axis=0)
    loss, grad_x, g, dmodv = _local_step(lay, mx, xa, loss_target.reshape(bl * seq, d), modv, w, exch)
    loss = lax.psum(loss[0, 0], ("x", "y", "c"))

    sums = {**exch.sums, **to_chip_sums(ffn_grads("ffn1", g["ffn1"]))}
    arrived = dict(exch.arrived)
    arrived.update(zip(BIG[:3], _exchange(_ChipSwapRide([sums[n][1] for n in BIG[:3]]), "rs_chips")))
    chip_sum = [sums[n][0] for n in BIG]
    from_chips = [arrived[n] for n in BIG]

    n9 = N_MOD * d
    dmod_rows = dmodv.reshape(bl + 1, n9)
    ctx_row = jnp.concatenate([dmod_rows[bl, :n_ctx_mod], jnp.zeros((n9 - n_ctx_mod,), F32)])
    summed = [ctx_row, g["norm_ffn1"], g["norm_mix"], g["norm_ffn2"], g["final_norm"], g["conv_b"], g["dt_bias"],
              g["a_row"], g["d_row"], g["ssm_norm_w"], g["cconv_b"], g["ln_w"], g["ln_b"], g["conv_w"][:5], g["cconv_w"]]
    sum_shapes = [t.shape for t in summed]
    (g4,) = _all_gather([_pack_rows([dmod_rows[:bl]] + summed, d)], "gather_small_grads", False)
    dmod_batch = g4[:, :bl * N_MOD].reshape(nb, n9)
    tot = _sum_lead(g4[:, _packed_rows(bl * n9, d):], "sum_small_grads")
    (dctx, g_n1, g_nm, g_n2, g_fn, g_cb, g_dtb, g_a, g_drow, g_snw, g_ccb, g_lnw, g_lnb, g_cw, g_ccw) = _unpack_rows(tot, sum_shapes)
    dmod_all = jnp.concatenate([dmod_batch, dctx[None, :], jnp.zeros((7, n9), F32)], axis=0)

    dmod_my_cols = lax.dynamic_slice_in_dim(dmod_all, me * mcols, mcols, axis=1)
    g_w_mod = _mm(s_all, dmod_my_cols, "tn", F32, "dw_mod")[None]
    g_b_mod = _sum_lead(dmod_all.reshape(nb + 8, N_MOD, d), "db_mod").reshape(1, n9)
    ds_part = _mm(dmod_my_cols[nb:nb + 8], w_mod[0], "nt", F32, "ds_ctx")
    (g5,) = _all_gather([jnp.concatenate([ds_part[0:1], jnp.zeros((7, d), F32)], axis=0)], "gather_ds_ctx", False)
    ds_ctx = _sum_lead(g5, "sum_ds_ctx")[0]
    sg = jax.nn.sigmoid(c_ctx)
    g_c_ctx = ds_ctx * (sg * (1.0 + c_ctx * (1.0 - sg)))

    a_f, a_b = a_vals[:, :heads], a_vals[:, heads:hh]
    grads = {
        "c_ctx": [g_c_ctx], "w_mod": [g_w_mod], "b_mod": [g_b_mod],
        "norm_ffn1": [g_n1], "norm_mix": [g_nm], "norm_ffn2": [g_n2], "final_norm": [g_fn.reshape(-1)],
        "ssm_conv_w": [lax.dynamic_slice_in_dim(g_cw, me * ssm_conv_w.shape[2], ssm_conv_w.shape[2], axis=1)[None]],
        "ssm_conv_b": [g_cb],
        "dt_bias_fwd": [g_dtb[:, :heads]], "dt_bias_bwd": [g_dtb[:, heads:hh]],
        "a_log_fwd": [g_a[:, :heads] * a_f], "a_log_bwd": [g_a[:, heads:hh] * a_b],
        "ssm_d": [jnp.sum(g_drow.reshape(1, heads, HEAD_DIM), axis=2)], "ssm_norm_w": [g_snw],
        "cconv_w": [lax.dynamic_slice_in_dim(g_ccw, me * cconv_w.shape[2], cconv_w.shape[2], axis=1)[None]],
        "cconv_b": [g_ccb], "cconv_ln_w": [g_lnw], "cconv_ln_b": [g_lnb],
    }
    for n, own, others in zip(BIG, chip_sum, from_chips):
        grads[n] = [(own, "chip"), (others, 0), (others, 1), (others, 2)]

    out_g, out_d, out_m, out_v = [], [], [], []
    for n in names:
        gr, de, nm, nv = _adamw(place, wts[n], grads[n], args["m_" + n], args["v_" + n], "adamw_" + n)
        out_g.append(gr)
        out_d.append(de)
        out_m.append(nm)
        out_v.append(nv)
    return (loss, grad_x.reshape(bl, seq, d), *out_g, *out_d, *out_m, *out_v)
```

```python
import functools
import math

import jax
import jax.numpy as jnp
from jax import lax
from jax.experimental import pallas as pl
from jax.experimental.pallas import tpu as pltpu

F32 = jnp.float32
BF16 = jnp.bfloat16
MESH = pl.DeviceIdType.MESH

N_DEV = 8
N_CHIP_PEERS = 3
HEAD_DIM = 64
N_STATE = 128
SSD_GROUPS = 2
CHUNK = 128
GRID_W = 64
N_MOD = 9
EPS = 1e-6
DT_PAD = 512
DT_LANES = 128
HALO = 8
ROW_TILE = 512
FINE_ROW_TILE = 256
VMEM_LIMIT = 48 * 1024 * 1024
NEG_BIG = -1e30

ADAM_LR = 0.001
ADAM_B1 = 0.9
ADAM_B2 = 0.999
ADAM_EPS = 1e-08
ADAM_WD = 0.01
ADAM_STEP = 10


def _pick(n, prefs):
    for p in prefs:
        if n % p == 0:
            return p
    return n


MM_TILE_CAP = 2816
MM_TILE_ELEMS = 3 << 20
MM_OUT_TILE_ELEMS = 3 << 19
MM_FULL_ROWS = 1024


def _big_tile(n, cap):
    if n <= cap:
        return n
    best = 0
    for t in range(128, cap + 1, 128):
        if n % t == 0:
            best = t
    return best or n


def _cparams(ndim):
    return pltpu.CompilerParams(dimension_semantics=("arbitrary",) * ndim, vmem_limit_bytes=VMEM_LIMIT)


def _silu(v):
    return v * jax.nn.sigmoid(v)


def _mm(a, b, mode, out_dtype, name, tn=None, tm=None, extras=(), epilogue=None, outs=None, ride=None):
    if mode == "tn":
        (K, M), (K2, N) = a.shape, b.shape
    elif mode == "nt":
        (M, K), (N, K2) = a.shape, b.shape
    else:
        (M, K), (K2, N) = a.shape, b.shape
    assert K == K2, (name, a.shape, b.shape)
    tm = M if M <= MM_FULL_ROWS else tm
    if tn is None:
        tn = _big_tile(N, min(MM_TILE_CAP, max(128, MM_OUT_TILE_ELEMS // (tm or 512))))
    if tm is None:
        tm = _big_tile(M, max(128, MM_OUT_TILE_ELEMS // tn))
    tk = _big_tile(K, min(MM_TILE_CAP, MM_TILE_ELEMS // max(tn, tm)))
    nk = K // tk
    ni, nj = M // tm, N // tn
    swap = nk == 1 and (K * N + M * K * nj) < (M * K + K * N * ni)
    ij = (lambda g0, g1: (g1, g0)) if swap else (lambda g0, g1: (g0, g1))
    if mode == "tn":
        a_spec = pl.BlockSpec((tk, tm), lambda g0, g1, k: (k, ij(g0, g1)[0]))
        dn = (((0,), (0,)), ((), ()))
    else:
        a_spec = pl.BlockSpec((tm, tk), lambda g0, g1, k: (ij(g0, g1)[0], k))
        dn = (((1,), (1,)), ((), ())) if mode == "nt" else (((1,), (0,)), ((), ()))
    if mode == "nt":
        b_spec = pl.BlockSpec((tn, tk), lambda g0, g1, k: (ij(g0, g1)[1], k))
    else:
        b_spec = pl.BlockSpec((tk, tn), lambda g0, g1, k: (k, ij(g0, g1)[1]))
    if outs is None:
        outs = [(tn, out_dtype)]
    nx = len(extras)

    def tile(w):
        return pl.BlockSpec((tm, w), lambda g0, g1, k: ij(g0, g1))

    def finish(acc, refs):
        vals = (acc,) if epilogue is None else epilogue(acc, *[r[...] for r in refs[:nx]])
        for o_ref, v in zip(refs[nx:], vals):
            o_ref[...] = v.astype(o_ref.dtype)

    grid = (nj, ni, nk) if swap else (ni, nj, nk)
    nout = len(outs)
    r_in = len(ride.arrays) if ride else 0
    r_out = len(ride.out_shapes) if ride else 0

    def compute(a_ref, b_ref, refs):
        part = lax.dot_general(a_ref[...].astype(BF16), b_ref[...].astype(BF16), dn, preferred_element_type=F32)
        if nk == 1:
            finish(part, refs)
            return
        acc_ref, k = refs[-1], pl.program_id(2)
        _acc(acc_ref, k == 0, part)

        @pl.when(k == nk - 1)
        def _():
            finish(acc_ref[...], refs[:-1])

    def body(a_ref, b_ref, *refs):
        if ride is None:
            compute(a_ref, b_ref, refs)
            return
        x_refs, rin = refs[:nx], refs[nx:nx + r_in]
        o_refs, rout = refs[nx + r_in:nx + r_in + nout], refs[nx + r_in + nout:nx + r_in + nout + r_out]
        tail = refs[nx + r_in + nout + r_out:]
        nacc = 1 if nk > 1 else 0
        sems = tail[nacc:]
        ids = [pl.program_id(q) for q in range(3)]
        first = functools.reduce(jnp.logical_and, [i == 0 for i in ids])
        last = functools.reduce(jnp.logical_and, [i == n - 1 for i, n in zip(ids, grid)])
        pl.when(first)(lambda: ride.start(rin, rout, *sems))
        compute(a_ref, b_ref, tuple(x_refs) + tuple(o_refs) + tuple(tail[:nacc]))
        pl.when(last)(lambda: ride.finish(rin, rout, *sems))

    hbm = pl.BlockSpec(memory_space=pl.ANY)
    res = pl.pallas_call(
        body, grid=grid, in_specs=[a_spec, b_spec] + [tile(w) for _, w in extras] + [hbm] * r_in,
        out_specs=[tile(w) for w, _ in outs] + [hbm] * r_out,
        out_shape=[jax.ShapeDtypeStruct((M, nj * w), dt) for w, dt in outs] + (list(ride.out_shapes) if ride else []),
        scratch_shapes=([pltpu.VMEM((tm, tn), F32)] if nk > 1 else []) + (list(ride.sems) if ride else []),
        name=name, compiler_params=_cparams(3),
    )(a, b, *[x for x, _ in extras], *(ride.arrays if ride else []))
    main = res[0] if epilogue is None else res[:nout]
    return (main, res[nout:]) if ride else main


def _ffn_tm(t):
    return _pick(t, (1024, 512, 256, 128, 64, 32, 16, 8))


def _dsilu_mul(g, u, da):
    sg = jax.nn.sigmoid(g)
    return da * u * (sg * (1.0 + g * (1.0 - sg))), da * (g * sg)


def _ffn_gu(h, wg, wu, name):
    t, d = h.shape
    nb, _, fb = wg.shape
    tm = _ffn_tm(t)

    def body(h_ref, wg_ref, wu_ref, g_ref, u_ref, a_ref):
        hv = h_ref[...]
        g = jnp.dot(hv, wg_ref[...], preferred_element_type=F32)
        u = jnp.dot(hv, wu_ref[...], preferred_element_type=F32)
        g_ref[...] = g
        u_ref[...] = u
        a_ref[...] = (_silu(g) * u).astype(a_ref.dtype)

    wspec = pl.BlockSpec((None, d, fb), lambda i, j: (j, 0, 0))
    ospec = pl.BlockSpec((None, tm, fb), lambda i, j: (j, i, 0))
    sh = jax.ShapeDtypeStruct((nb, t, fb), F32)
    return pl.pallas_call(body, grid=(t // tm, nb), in_specs=[pl.BlockSpec((tm, d), lambda i, j: (i, 0)), wspec, wspec],
                          out_specs=[ospec, ospec, ospec], out_shape=[sh, sh, jax.ShapeDtypeStruct((nb, t, fb), BF16)],
                          name=name, compiler_params=_cparams(2))(h, wg, wu)


def _ffn_contract(xs, ws, mode, name):
    nb, t, fb = xs[0].shape
    d = ws[0].shape[2] if mode == "nn" else ws[0].shape[1]
    tm = _ffn_tm(t)
    npair = len(xs)
    dn = (((1,), (0,)), ((), ())) if mode == "nn" else (((1,), (1,)), ((), ()))

    def body(*refs):
        o_ref, acc_ref = refs[2 * npair], refs[2 * npair + 1]
        j = pl.program_id(1)
        part = lax.dot_general(refs[0][...], refs[npair][...], dn, preferred_element_type=F32)
        for p in range(1, npair):
            part = part + lax.dot_general(refs[p][...], refs[npair + p][...], dn, preferred_element_type=F32)
        _acc(acc_ref, j == 0, part)

        @pl.when(j == nb - 1)
        def _():
            o_ref[...] = acc_ref[...]

    xspec = pl.BlockSpec((None, tm, fb), lambda i, j: (j, i, 0))
    wspec = pl.BlockSpec((None,) + tuple(ws[0].shape[1:]), lambda i, j: (j, 0, 0))
    return pl.pallas_call(body, grid=(t // tm, nb), in_specs=[xspec] * npair + [wspec] * npair,
                          out_specs=pl.BlockSpec((tm, d), lambda i, j: (i, 0)),
                          out_shape=jax.ShapeDtypeStruct((t, d), F32), scratch_shapes=[pltpu.VMEM((tm, d), F32)],
                          name=name, compiler_params=_cparams(2))(*xs, *ws)


def _ffn_da(do, wd, g, u, name):
    t, d = do.shape
    nb, fb, _ = wd.shape
    tm = _ffn_tm(t)

    def body(do_ref, wd_ref, g_ref, u_ref, dg_ref, du_ref):
        da = lax.dot_general(do_ref[...], wd_ref[...], (((1,), (1,)), ((), ())), preferred_element_type=F32)
        dg, du = _dsilu_mul(g_ref[...], u_ref[...], da)
        dg_ref[...] = dg.astype(dg_ref.dtype)
        du_ref[...] = du.astype(du_ref.dtype)

    bspec = pl.BlockSpec((None, tm, fb), lambda i, j: (j, i, 0))
    sh = jax.ShapeDtypeStruct((nb, t, fb), BF16)
    return pl.pallas_call(body, grid=(t // tm, nb),
                          in_specs=[pl.BlockSpec((tm, d), lambda i, j: (i, 0)),
                                    pl.BlockSpec((None, fb, d), lambda i, j: (j, 0, 0)), bspec, bspec],
                          out_specs=[bspec, bspec], out_shape=[sh, sh], name=name, compiler_params=_cparams(2),
                          )(do, wd, g, u)


def _ffn_dw(h, xs, name):
    t, d = h.shape
    nb, _, fb = xs[0].shape
    tk = _ffn_tm(t)
    nk = t // tk
    npair = len(xs)
    tn_dims = (((0,), (0,)), ((), ()))

    def body(*refs):
        h_ref = refs[0]
        k = pl.program_id(1)
        hv = h_ref[...]
        for p in range(npair):
            part = lax.dot_general(hv, refs[1 + p][...], tn_dims, preferred_element_type=F32)
            _acc(refs[1 + npair + p], k == 0, part)

    xspec = pl.BlockSpec((None, tk, fb), lambda j, k: (j, k, 0))
    ospec = pl.BlockSpec((None, d, fb), lambda j, k: (j, 0, 0))
    sh = jax.ShapeDtypeStruct((nb, d, fb), F32)
    return pl.pallas_call(body, grid=(nb, nk), in_specs=[pl.BlockSpec((tk, d), lambda j, k: (k, 0))] + [xspec] * npair,
                          out_specs=[ospec] * npair, out_shape=[sh] * npair, name=name, compiler_params=_cparams(2),
                          )(h, *xs)


def _ffn_dwd(a, do, name):
    nb, t, fb = a.shape
    d = do.shape[1]
    tk = _ffn_tm(t)

    def body(a_ref, do_ref, o_ref):
        part = lax.dot_general(a_ref[...], do_ref[...], (((0,), (0,)), ((), ())), preferred_element_type=F32)
        _acc(o_ref, pl.program_id(1) == 0, part)

    return pl.pallas_call(body, grid=(nb, t // tk),
                          in_specs=[pl.BlockSpec((None, tk, fb), lambda j, k: (j, k, 0)),
                                    pl.BlockSpec((tk, d), lambda j, k: (k, 0))],
                          out_specs=pl.BlockSpec((None, fb, d), lambda j, k: (j, 0, 0)),
                          out_shape=jax.ShapeDtypeStruct((nb, fb, d), F32), name=name, compiler_params=_cparams(2),
                          )(a, do)


class _Lay:
    def __init__(self, bl, seq, clen, d, tt=None):
        self.bl, self.seq, self.clen, self.d = bl, seq, clen, d
        self.tt = min(ROW_TILE, math.gcd(seq, bl * clen)) if tt is None else tt
        assert seq % self.tt == 0 and (bl * clen) % self.tt == 0 and self.tt % 8 == 0
        self.spb = seq // self.tt
        self.spc = clen // self.tt
        self.nsx = bl * self.spb
        self.nsc = bl * clen // self.tt
        self.ns = self.nsx + self.nsc
        self.tx = bl * seq
        self.ta = self.tx + bl * clen

    def fine(self):
        return _Lay(self.bl, self.seq, self.clen, self.d, min(FINE_ROW_TILE, self.clen))

    def mrow(self, s):
        return jnp.where(s < self.nsx, s // self.spb, self.bl)

    def first_of_row(self, s):
        return jnp.logical_or(jnp.logical_and(s < self.nsx, s % self.spb == 0), s == self.nsx)

    def seq_first(self, s):
        return jnp.where(s < self.nsx, s % self.spb == 0, (s - self.nsx) % self.spc == 0)

    def seq_last(self, s):
        return jnp.where(s < self.nsx, s % self.spb == self.spb - 1, (s - self.nsx) % self.spc == self.spc - 1)


def _tok(lay, c, cb=0, clamp=None):
    if clamp is None:
        return pl.BlockSpec((lay.tt, c), lambda j, s: (s, cb + j))
    return pl.BlockSpec((lay.tt, c), lambda j, s: (jnp.minimum(s, clamp), cb + j))


def _halo_prev(lay, c, cb=0):
    u = lay.tt // HALO
    return pl.BlockSpec((HALO, c), lambda j, s: (jnp.maximum(s * u - 1, 0), cb + j))


def _halo_next(lay, c, cb=0):
    u = lay.tt // HALO
    last = lay.ta // HALO - 1
    return pl.BlockSpec((HALO, c), lambda j, s: (jnp.minimum((s + 1) * u, last), cb + j))


def _row(lay, k, c):
    return pl.BlockSpec((None, k, c), lambda j, s: (lay.mrow(s), 0, 0))


def _glob(k, c, cb=None):
    if cb is None:
        return pl.BlockSpec((k, c), lambda j, s: (0, 0))
    return pl.BlockSpec((k, c), lambda j, s: (0, cb + j))


def _tok_call(name, body, ncb, nseg, in_specs, out_specs, out_shape, inputs, scratch=()):
    return pl.pallas_call(body, grid=(ncb, nseg), in_specs=in_specs, out_specs=out_specs, out_shape=out_shape,
                          scratch_shapes=list(scratch), name=name, compiler_params=_cparams(2))(*inputs)


def _acc(ref, first, val):
    @pl.when(first)
    def _():
        ref[...] = val

    @pl.when(jnp.logical_not(first))
    def _():
        ref[...] += val


def _norm_mod_f(x, w, sh, sc):
    y = x * lax.rsqrt(jnp.mean(x * x, axis=-1, keepdims=True) + EPS) * w
    return y * (1.0 + sc) + sh


def _norm_mod_fwd(lay, nseg, x, w, modv, ksh, name):
    d = lay.d

    def body(x_ref, w_ref, m_ref, h_ref):
        h = _norm_mod_f(x_ref[...], w_ref[...], m_ref[ksh:ksh + 1, :], m_ref[ksh + 1:ksh + 2, :])
        h_ref[...] = h.astype(h_ref.dtype)

    return _tok_call(name, body, 1, nseg, [_tok(lay, d), _glob(1, d), _row(lay, N_MOD, d)], _tok(lay, d),
                     jax.ShapeDtypeStruct((nseg * lay.tt, d), BF16), (x, w, modv))


def _norm_mod_bwd(lay, nseg, nres, x, w, modv, ksh, dh, dres, name, nout=None):
    d = lay.d
    nrow = lay.bl + (1 if nseg > lay.nsx else 0)
    nout = nseg if nout is None else nout

    def body(x_ref, w_ref, m_ref, dh_ref, dres_ref, dx_ref, dw_ref, dm_ref):
        s = pl.program_id(1)
        _, vjp = jax.vjp(_norm_mod_f, x_ref[...], w_ref[...], m_ref[ksh:ksh + 1, :], m_ref[ksh + 1:ksh + 2, :])
        dx, dw, dsh, dsc = vjp(dh_ref[...])

        @pl.when(s < nout)
        def _():
            dx_ref[...] = dx + jnp.where(s < nres, dres_ref[...], 0.0)

        _acc(dw_ref, s == 0, dw)
        _acc(dm_ref, lay.first_of_row(s), jnp.concatenate([dsh, dsc], axis=0))

    return _tok_call(
        name, body, 1, nseg,
        [_tok(lay, d), _glob(1, d), _row(lay, N_MOD, d), _tok(lay, d), _tok(lay, d, clamp=nres - 1)],
        [_tok(lay, d, clamp=nout - 1), _glob(1, d), _row(lay, 2, d)],
        [jax.ShapeDtypeStruct((nout * lay.tt, d), F32), jax.ShapeDtypeStruct((1, d), F32),
         jax.ShapeDtypeStruct((nrow, 2, d), F32)],
        (x, w, modv, dh, dres))


def _resid_fwd(lay, nseg, x, o, modv, kg, coef, name):
    d = lay.d

    def body(x_ref, o_ref, m_ref, y_ref):
        y_ref[...] = x_ref[...] + (coef * m_ref[kg:kg + 1, :]) * o_ref[...]

    return _tok_call(name, body, 1, nseg, [_tok(lay, d), _tok(lay, d), _row(lay, N_MOD, d)], _tok(lay, d),
                     jax.ShapeDtypeStruct((nseg * lay.tt, d), F32), (x, o, modv))


def _resid_bwd(lay, nseg, dy, o, modv, kg, coef, name):
    d = lay.d
    nrow = lay.bl + (1 if nseg > lay.nsx else 0)

    def body(dy_ref, o_ref, m_ref, do_ref, dg_ref):
        s = pl.program_id(1)
        dy = dy_ref[...]
        do_ref[...] = (dy * (coef * m_ref[kg:kg + 1, :])).astype(do_ref.dtype)
        _acc(dg_ref, lay.first_of_row(s), jnp.sum(dy * o_ref[...], axis=0, keepdims=True) * coef)

    return _tok_call(name, body, 1, nseg, [_tok(lay, d), _tok(lay, d), _row(lay, N_MOD, d)],
                     [_tok(lay, d), _row(lay, 1, d)],
                     [jax.ShapeDtypeStruct((nseg * lay.tt, d), BF16), jax.ShapeDtypeStruct((nrow, 1, d), F32)],
                     (dy, o, modv))


def _final_loss(lay, x, wf, target, name):
    d = lay.d

    def body(x_ref, w_ref, t_ref, loss_ref, dx_ref, dw_ref):
        s = pl.program_id(1)

        def f(xv, wv):
            return xv * lax.rsqrt(jnp.mean(xv * xv, axis=-1, keepdims=True) + EPS) * wv

        y, vjp = jax.vjp(f, x_ref[...], w_ref[...])
        err = y - t_ref[...]
        part = 0.5 * jnp.sum(jnp.sum(err * err, axis=-1, keepdims=True), axis=0, keepdims=True) / d
        dx, dw = vjp(err / d)
        dx_ref[...] = dx
        _acc(loss_ref, s == 0, part)
        _acc(dw_ref, s == 0, dw)

    return _tok_call(name, body, 1, lay.nsx, [_tok(lay, d), _glob(1, d), _tok(lay, d)],
                     [_glob(1, 1), _tok(lay, d), _glob(1, d)],
                     [jax.ShapeDtypeStruct((1, 1), F32), jax.ShapeDtypeStruct((lay.tx, d), F32),
                      jax.ShapeDtypeStruct((1, d), F32)], (x, wf, target))


class _Mix:
    def __init__(self, d, heads):
        self.d_ssm = d
        self.d_conv = d
        self.heads = heads
        assert heads * HEAD_DIM == d and heads % (2 * SSD_GROUPS) == 0 and 2 * heads <= DT_LANES
        self.gn = SSD_GROUPS * N_STATE
        self.xw = d + 2 * self.gn
        self.off_dt = d
        self.off_x = d + DT_PAD
        self.off_glu = self.off_x + self.xw
        self.pw = self.off_glu + 2 * d
        self.ref_x = d
        self.ref_dt = d + self.xw
        self.ref_glu = self.ref_dt + 2 * heads
        self.cc = self.xw if self.off_x % self.xw == 0 else _pick(self.xw, (512, 256, 128))


def _conv5_fwd(lay, mx, proj, cw, cb, name):
    c, tt = mx.cc, lay.tt
    cb0 = mx.off_x // c
    assert mx.off_x % c == 0

    def body(prev_ref, cur_ref, next_ref, w_ref, b_ref, pre_ref, act_ref, ext_ref):
        s = pl.program_id(1)
        ext_ref[0:HALO, :] = jnp.where(lay.seq_first(s), 0.0, prev_ref[...])
        ext_ref[HALO:HALO + tt, :] = cur_ref[...]
        ext_ref[HALO + tt:, :] = jnp.where(lay.seq_last(s), 0.0, next_ref[...])
        acc = jnp.zeros((tt, c), F32) + b_ref[...]
        for k in range(5):
            acc = acc + w_ref[k:k + 1, :] * ext_ref[pl.ds(HALO + k - 2, tt), :]
        pre_ref[...] = acc
        act_ref[...] = _silu(acc)

    sh = jax.ShapeDtypeStruct((lay.ta, mx.xw), F32)
    return _tok_call(name, body, mx.xw // c, lay.ns,
                     [_halo_prev(lay, c, cb0), _tok(lay, c, cb0), _halo_next(lay, c, cb0), _glob(8, c, 0), _glob(1, c, 0)],
                     [_tok(lay, c), _tok(lay, c)], [sh, sh], (proj, proj, proj, cw, cb),
                     scratch=[pltpu.VMEM((tt + 2 * HALO, c), F32)])


def _conv5_bwd(lay, mx, proj, pre, dact_f, dact_b, cw, name):
    c, tt = mx.cc, lay.tt
    cb0 = mx.off_x // c

    def dsilu(p):
        sg = jax.nn.sigmoid(p)
        return sg * (1.0 + p * (1.0 - sg))

    def body(xp_ref, xc_ref, xn_ref, pp_ref, pc_ref, pn_ref, fp_ref, fc_ref, fn_ref, bp_ref, bc_ref, bn_ref, w_ref,
             dx_ref, dw_ref, db_ref, extx_ref, extd_ref):
        s = pl.program_id(1)
        first, last = lay.seq_first(s), lay.seq_last(s)
        dcur = (fc_ref[...] + bc_ref[...]) * dsilu(pc_ref[...])
        extd_ref[0:HALO, :] = jnp.where(first, 0.0, (fp_ref[...] + bp_ref[...]) * dsilu(pp_ref[...]))
        extd_ref[HALO:HALO + tt, :] = dcur
        extd_ref[HALO + tt:, :] = jnp.where(last, 0.0, (fn_ref[...] + bn_ref[...]) * dsilu(pn_ref[...]))
        extx_ref[0:HALO, :] = jnp.where(first, 0.0, xp_ref[...])
        extx_ref[HALO:HALO + tt, :] = xc_ref[...]
        extx_ref[HALO + tt:, :] = jnp.where(last, 0.0, xn_ref[...])
        dx = jnp.zeros((tt, c), F32)
        rows = []
        for k in range(5):
            dx = dx + w_ref[k:k + 1, :] * extd_ref[pl.ds(HALO - (k - 2), tt), :]
            rows.append(jnp.sum(dcur * extx_ref[pl.ds(HALO + k - 2, tt), :], axis=0, keepdims=True))
        dx_ref[...] = dx.astype(dx_ref.dtype)
        rows.append(jnp.zeros((3, c), F32))
        _acc(dw_ref, s == 0, jnp.concatenate(rows, axis=0))
        _acc(db_ref, s == 0, jnp.sum(dcur, axis=0, keepdims=True))

    three = lambda cbx: [_halo_prev(lay, c, cbx), _tok(lay, c, cbx), _halo_next(lay, c, cbx)]
    ext = pltpu.VMEM((tt + 2 * HALO, c), F32)
    return _tok_call(name, body, mx.xw // c, lay.ns,
                     three(cb0) + three(0) + three(0) + three(0) + [_glob(8, c, 0)],
                     [_tok(lay, c), _glob(8, c, 0), _glob(1, c, 0)],
                     [jax.ShapeDtypeStruct((lay.ta, mx.xw), BF16), jax.ShapeDtypeStruct((8, mx.xw), F32),
                      jax.ShapeDtypeStruct((1, mx.xw), F32)],
                     (proj, proj, proj, pre, pre, pre, dact_f, dact_f, dact_f, dact_b, dact_b, dact_b, cw),
                     scratch=[ext, ext])


def _softplus(v):
    return jnp.maximum(v, 0.0) + jnp.log1p(jnp.exp(-jnp.abs(v)))


def _dt_fwd(lay, mx, proj, bias, name):
    cb = mx.off_dt // DT_LANES

    def body(p_ref, b_ref, dt_ref):
        dt_ref[...] = _softplus(p_ref[...] + b_ref[...])

    return _tok_call(name, body, 1, lay.ns, [_tok(lay, DT_LANES, cb), _glob(1, DT_LANES)], _tok(lay, DT_LANES),
                     jax.ShapeDtypeStruct((lay.ta, DT_LANES), F32), (proj, bias))


def _dt_bwd(lay, mx, proj, bias, parts, name):
    cb = mx.off_dt // DT_LANES

    def body(p_ref, b_ref, a_ref, b2_ref, c_ref, d_ref, dp_ref, db_ref):
        s = pl.program_id(1)
        ddt = (a_ref[...] + b2_ref[...]) + (c_ref[...] + d_ref[...])
        draw = ddt * jax.nn.sigmoid(p_ref[...] + b_ref[...])
        dp_ref[:, 0:DT_LANES] = draw.astype(dp_ref.dtype)
        dp_ref[:, DT_LANES:] = jnp.zeros((lay.tt, DT_PAD - DT_LANES), dp_ref.dtype)
        _acc(db_ref, s == 0, jnp.sum(draw, axis=0, keepdims=True))

    t = _tok(lay, DT_LANES)
    return _tok_call(name, body, 1, lay.ns, [_tok(lay, DT_LANES, cb), _glob(1, DT_LANES), t, t, t, t],
                     [_tok(lay, DT_PAD), _glob(1, DT_LANES)],
                     [jax.ShapeDtypeStruct((lay.ta, DT_PAD), BF16), jax.ShapeDtypeStruct((1, DT_LANES), F32)],
                     (proj, bias) + tuple(parts))


def _scan_mask(rev):
    r = lax.broadcasted_iota(jnp.int32, (CHUNK, CHUNK), 0)
    c = lax.broadcasted_iota(jnp.int32, (CHUNK, CHUNK), 1)
    return (r <= c) if rev else (r >= c)


def _split_bf16(x):
    hi = x.astype(BF16)
    return hi, (x - hi.astype(F32)).astype(BF16)


@functools.partial(jax.custom_vjp, nondiff_argnums=(0,))
def _cum_cols(rev, x):
    m = _scan_mask(rev).astype(BF16)
    hi, lo = _split_bf16(x)
    return jnp.dot(m, hi, preferred_element_type=F32) + jnp.dot(m, lo, preferred_element_type=F32)


_cum_cols.defvjp(lambda rev, x: (_cum_cols(rev, x), None), lambda rev, _, g: (_cum_cols(not rev, g),))


@functools.partial(jax.custom_vjp, nondiff_argnums=(0,))
def _cum_rows(rev, x):
    m = _scan_mask(not rev).astype(BF16)
    hi, lo = _split_bf16(x)
    return jnp.dot(hi, m, preferred_element_type=F32) + jnp.dot(lo, m, preferred_element_type=F32)


_cum_rows.defvjp(lambda rev, x: (_cum_rows(rev, x), None), lambda rev, _, g: (_cum_rows(not rev, g),))


@functools.partial(jax.custom_vjp, nondiff_argnums=(1,))
def _take_col(x, k):
    return x[:, k:k + 1]


def _take_col_bwd(k, _, g):
    lane = lax.broadcasted_iota(jnp.int32, (g.shape[0], DT_LANES), 1)
    return (jnp.where(lane == k, g, 0.0),)


_take_col.defvjp(lambda x, k: (x[:, k:k + 1], None), _take_col_bwd)


@functools.partial(jax.custom_vjp, nondiff_argnums=(1,))
def _take_row(x, k):
    return x[k:k + 1, :]


def _take_row_bwd(k, _, g):
    sub = lax.broadcasted_iota(jnp.int32, (DT_LANES, g.shape[1]), 0)
    return (jnp.where(sub == k, g, 0.0),)


_take_row.defvjp(lambda x, k: (x[k:k + 1, :], None), _take_row_bwd)


def _ssd_chunk(xh_pairs, bcs, ccs, dtc, dtr, a_row, a_col, st_pairs, *, rev, heads, col0):
    cs_c, cs_r, tot, scores = _ssd_shared(bcs, ccs, dtc, dtr, a_row, a_col, rev=rev)
    ppg = heads // (2 * SSD_GROUPS)
    ys, sts = [], []
    for g in range(SSD_GROUPS):
        y, st = _ssd_group(xh_pairs[g * ppg:(g + 1) * ppg], bcs[g], ccs[g], st_pairs[g], cs_c, cs_r, tot, dtc,
                           scores[g], rev=rev, col=col0 + 2 * ppg * g)
        ys.append(y)
        sts.append(st)
    return ys, sts


_NT = (((1,), (1,)), ((), ()))
_TN = (((0,), (0,)), ((), ()))


def _ssd_shared(bcs, ccs, dtc, dtr, a_row, a_col, *, rev):
    da_c = dtc * a_row
    cs_c = _cum_cols(rev, da_c)
    cs_r = _cum_rows(rev, dtr * a_col)
    tot = jnp.sum(da_c, axis=0, keepdims=True)
    scores = [lax.dot_general(ccs[g].astype(BF16), bcs[g].astype(BF16), _NT, preferred_element_type=F32)
              for g in range(SSD_GROUPS)]
    return cs_c, cs_r, tot, scores


def _ssd_group(xh_pairs, bc, cc, st, cs_c, cs_r, tot, dtc, score, *, rev, col):
    n = CHUNK
    mask = _scan_mask(rev)
    lane = lax.broadcasted_iota(jnp.int32, (n, DT_LANES), 1)
    sub = lax.broadcasted_iota(jnp.int32, (DT_LANES, n), 0)
    lane1 = lax.broadcasted_iota(jnp.int32, (1, DT_LANES), 1)
    left = lax.broadcasted_iota(jnp.int32, (n, 2 * HEAD_DIM), 1) < HEAD_DIM
    top = lax.broadcasted_iota(jnp.int32, (2 * HEAD_DIM, 1), 0) < HEAD_DIM
    xs_all, wst_all, ecs_all, edec_all, y_diag = [], [], [], [], []
    for p, xh in enumerate(xh_pairs):
        per = []
        for c in (col + 2 * p, col + 2 * p + 1):
            csv = jnp.sum(jnp.where(lane == c, cs_c, 0.0), axis=1, keepdims=True)
            csr = jnp.sum(jnp.where(sub == c, cs_r, 0.0), axis=0, keepdims=True)
            dtv = jnp.sum(jnp.where(lane == c, dtc, 0.0), axis=1, keepdims=True)
            tv = jnp.sum(jnp.where(lane1 == c, tot, 0.0), axis=1, keepdims=True)
            m = score * jnp.exp(jnp.where(mask, csv - csr, NEG_BIG))
            per.append((csv, dtv, tv, m))
        (cs1, dt1, t1, m1), (cs2, dt2, t2, m2) = per
        xs = xh * jnp.where(left, dt1, dt2)
        both = jnp.dot(jnp.concatenate([m1, m2], axis=0).astype(BF16), xs.astype(BF16), preferred_element_type=F32)
        y_diag.append(jnp.where(left, both[:n], both[n:]))
        xs_all.append(xs)
        ecs_all.append(jnp.where(left, jnp.exp(cs1), jnp.exp(cs2)))
        wst_all.append(jnp.where(left, jnp.exp(t1 - cs1), jnp.exp(t2 - cs2)))
        edec_all.append(jnp.where(top, jnp.exp(t1), jnp.exp(t2)))
    cat = lambda parts, axis: parts[0] if len(parts) == 1 else jnp.concatenate(parts, axis=axis)
    xs, wst, ecs = cat(xs_all, 1), cat(wst_all, 1), cat(ecs_all, 1)
    y_off = lax.dot_general(cc.astype(BF16), st.astype(BF16), _NT, preferred_element_type=F32) * ecs
    cst = lax.dot_general((xs * wst).astype(BF16), bc.astype(BF16), _TN, preferred_element_type=F32)
    return cat(y_diag, 1) + y_off, st * cat(edec_all, 0) + cst


class _Scan:
    def __init__(self, lay, rev):
        self.ncx, self.ncc, self.bl, self.rev = lay.seq // CHUNK, lay.clen // CHUNK, lay.bl, rev
        self.nct = self.ncx + self.ncc

    def chunk(self, b, pos):
        kc = (self.ncc - 1 - pos) if self.rev else pos
        kx = (self.ncx - 1 - (pos - self.ncc)) if self.rev else (pos - self.ncc)
        return jnp.where(pos < self.ncc, self.bl * self.ncx + b * self.ncc + kc, b * self.ncx + kx)


def _ssd_io(mx, x_ref, st_src):
    np_ = mx.heads // 2
    d = mx.d_ssm
    xh = [x_ref[:, 128 * p:128 * (p + 1)] for p in range(np_)]
    bcs = [x_ref[:, d + N_STATE * g:d + N_STATE * (g + 1)] for g in range(SSD_GROUPS)]
    ccs = [x_ref[:, d + mx.gn + N_STATE * g:d + mx.gn + N_STATE * (g + 1)] for g in range(SSD_GROUPS)]
    gw = d // SSD_GROUPS
    sts = [st_src[gw * g:gw * (g + 1), :] for g in range(SSD_GROUPS)]
    return xh, bcs, ccs, sts


def _ssd_fwd(lay, mx, xbc, dt, dtt, a_row, a_col, rev, name):
    sc = _Scan(lay, rev)
    col0 = mx.heads if rev else 0
    hp = mx.heads * HEAD_DIM

    def body(x_ref, dt_ref, dtt_ref, ar_ref, ac_ref, y_ref, hp_ref, st_ref):
        @pl.when(pl.program_id(1) == 0)
        def _():
            st_ref[...] = jnp.zeros_like(st_ref)

        hp_ref[...] = st_ref[...]
        xh, bcs, ccs, sts = _ssd_io(mx, x_ref, st_ref)
        ys, new = _ssd_chunk(xh, bcs, ccs, dt_ref[...], dtt_ref[...], ar_ref[...], ac_ref[...], sts,
                             rev=rev, heads=mx.heads, col0=col0)
        gw = mx.d_ssm // SSD_GROUPS
        for g in range(SSD_GROUPS):
            y_ref[:, gw * g:gw * (g + 1)] = ys[g]
            st_ref[gw * g:gw * (g + 1), :] = new[g]

    ch = sc.chunk
    return pl.pallas_call(
        body, grid=(lay.bl, sc.nct),
        in_specs=[pl.BlockSpec((CHUNK, mx.xw), lambda b, i: (ch(b, i), 0)),
                  pl.BlockSpec((CHUNK, DT_LANES), lambda b, i: (ch(b, i), 0)),
                  pl.BlockSpec((DT_LANES, CHUNK), lambda b, i: (0, ch(b, i))),
                  pl.BlockSpec((1, DT_LANES), lambda b, i: (0, 0)),
                  pl.BlockSpec((DT_LANES, 1), lambda b, i: (0, 0))],
        out_specs=[pl.BlockSpec((CHUNK, mx.d_ssm), lambda b, i: (ch(b, i), 0)),
                   pl.BlockSpec((hp, N_STATE), lambda b, i: (b * sc.nct + i, 0))],
        out_shape=[jax.ShapeDtypeStruct((lay.ta, mx.d_ssm), F32),
                   jax.ShapeDtypeStruct((lay.bl * sc.nct * hp, N_STATE), F32)],
        scratch_shapes=[pltpu.VMEM((hp, N_STATE), F32)], name=name, compiler_params=_cparams(2),
    )(xbc, dt, dtt, a_row, a_col)


def _ssd_bwd(lay, mx, xbc, dt, dtt, a_row, a_col, hprev, dy, dskip, rev, name):
    sc = _Scan(lay, rev)
    col0 = mx.heads if rev else 0
    hp = mx.heads * HEAD_DIM
    np_ = mx.heads // 2
    d = mx.d_ssm
    with_skip = dskip is not None

    def body(*refs):
        if with_skip:
            x_ref, dt_ref, dtt_ref, ar_ref, ac_ref, hp_ref, dy_ref, sk_ref = refs[:8]
            rest = refs[8:]
        else:
            x_ref, dt_ref, dtt_ref, ar_ref, ac_ref, hp_ref, dy_ref = refs[:7]
            rest = refs[7:]
        dx_ref, ddc_ref, ddr_ref, dar_ref, dac_ref, ds_ref = rest
        b, i = pl.program_id(0), pl.program_id(1)

        @pl.when(i == 0)
        def _():
            ds_ref[...] = jnp.zeros_like(ds_ref)

        xh, bcs, ccs, sts = _ssd_io(mx, x_ref, hp_ref)
        dtc = dt_ref[...]
        shared, vjp_shared = jax.vjp(functools.partial(_ssd_shared, rev=rev), bcs, ccs, dtc, dtt_ref[...],
                                     ar_ref[...], ac_ref[...])
        cs_c, cs_r, tot, scores = shared
        plus = lambda acc, v: v if acc is None else acc + v
        d_cs_c = d_cs_r = d_tot = ddc = None
        d_scores, dbc, dcc = [], [], []
        ppg = np_ // SSD_GROUPS
        gw = d // SSD_GROUPS
        for g in range(SSD_GROUPS):
            dyg = dy_ref[:, gw * g:gw * (g + 1)]
            fn = functools.partial(_ssd_group, rev=rev, col=col0 + 2 * ppg * g)
            _, vjp = jax.vjp(fn, xh[g * ppg:(g + 1) * ppg], bcs[g], ccs[g], sts[g], cs_c, cs_r, tot, dtc, scores[g])
            dxh, dbc_g, dcc_g, dst, dcs_c_g, dcs_r_g, dtot_g, ddc_g, dsc_g = vjp((dyg, ds_ref[gw * g:gw * (g + 1), :]))
            for q in range(ppg):
                p = g * ppg + q
                v = dxh[q]
                if with_skip:
                    v = v + dyg[:, 128 * q:128 * (q + 1)] * sk_ref[:, 128 * p:128 * (p + 1)]
                dx_ref[:, 128 * p:128 * (p + 1)] = v
            ds_ref[gw * g:gw * (g + 1), :] = dst
            d_cs_c, d_cs_r, d_tot, ddc = plus(d_cs_c, dcs_c_g), plus(d_cs_r, dcs_r_g), plus(d_tot, dtot_g), plus(ddc, ddc_g)
            d_scores.append(dsc_g)
            dbc.append(dbc_g)
            dcc.append(dcc_g)
        dbc_s, dcc_s, ddc_s, ddr, dar, dac = vjp_shared((d_cs_c, d_cs_r, d_tot, d_scores))
        ddc = ddc + ddc_s
        dbc = [dbc[g] + dbc_s[g] for g in range(SSD_GROUPS)]
        dcc = [dcc[g] + dcc_s[g] for g in range(SSD_GROUPS)]
        for g in range(SSD_GROUPS):
            dx_ref[:, d + N_STATE * g:d + N_STATE * (g + 1)] = dbc[g]
            dx_ref[:, d + mx.gn + N_STATE * g:d + mx.gn + N_STATE * (g + 1)] = dcc[g]
        ddc_ref[...] = ddc
        ddr_ref[...] = ddr
        first = jnp.logical_and(b == 0, i == 0)
        _acc(dar_ref, first, dar)
        _acc(dac_ref, first, dac)

    ch = lambda b, i: sc.chunk(b, sc.nct - 1 - i)
    in_specs = [pl.BlockSpec((CHUNK, mx.xw), lambda b, i: (ch(b, i), 0)),
                pl.BlockSpec((CHUNK, DT_LANES), lambda b, i: (ch(b, i), 0)),
                pl.BlockSpec((DT_LANES, CHUNK), lambda b, i: (0, ch(b, i))),
                pl.BlockSpec((1, DT_LANES), lambda b, i: (0, 0)),
                pl.BlockSpec((DT_LANES, 1), lambda b, i: (0, 0)),
                pl.BlockSpec((hp, N_STATE), lambda b, i: (b * sc.nct + sc.nct - 1 - i, 0)),
                pl.BlockSpec((CHUNK, d), lambda b, i: (ch(b, i), 0))]
    inputs = [xbc, dt, dtt, a_row, a_col, hprev, dy]
    if with_skip:
        in_specs.append(pl.BlockSpec((1, d), lambda b, i: (0, 0)))
        inputs.append(dskip)
    return pl.pallas_call(
        body, grid=(lay.bl, sc.nct), in_specs=in_specs,
        out_specs=[pl.BlockSpec((CHUNK, mx.xw), lambda b, i: (ch(b, i), 0)),
                   pl.BlockSpec((CHUNK, DT_LANES), lambda b, i: (ch(b, i), 0)),
                   pl.BlockSpec((DT_LANES, CHUNK), lambda b, i: (0, ch(b, i))),
                   pl.BlockSpec((1, DT_LANES), lambda b, i: (0, 0)),
                   pl.BlockSpec((DT_LANES, 1), lambda b, i: (0, 0))],
        out_shape=[jax.ShapeDtypeStruct((lay.ta, mx.xw), F32), jax.ShapeDtypeStruct((lay.ta, DT_LANES), F32),
                   jax.ShapeDtypeStruct((DT_LANES, lay.ta), F32), jax.ShapeDtypeStruct((1, DT_LANES), F32),
                   jax.ShapeDtypeStruct((DT_LANES, 1), F32)],
        scratch_shapes=[pltpu.VMEM((hp, N_STATE), F32)], name=name, compiler_params=_cparams(2),
    )(*inputs)


def _gate_f(yf, yb, xh, z, drow, nw):
    dd = yf.shape[-1]
    half = dd // SSD_GROUPS
    yz = (yf + yb + drow * xh) * _silu(z)
    lo = lax.broadcasted_iota(jnp.int32, yz.shape, 1) < half
    sq = yz * yz
    ms1 = jnp.sum(jnp.where(lo, sq, 0.0), axis=-1, keepdims=True) / half
    ms2 = jnp.sum(jnp.where(lo, 0.0, sq), axis=-1, keepdims=True) / half
    return yz * jnp.where(lo, lax.rsqrt(ms1 + EPS), lax.rsqrt(ms2 + EPS)) * nw


def _gate_fwd(lay, mx, yf, yb, xbc, proj, drow, nw, name):
    d = mx.d_ssm

    def body(yf_ref, yb_ref, xh_ref, z_ref, d_ref, w_ref, o_ref):
        o_ref[...] = _gate_f(yf_ref[...], yb_ref[...], xh_ref[...], z_ref[...], d_ref[...], w_ref[...]).astype(o_ref.dtype)

    t = _tok(lay, d)
    return _tok_call(name, body, 1, lay.nsx, [t, t, t, t, _glob(1, d), _glob(1, d)], t,
                     jax.ShapeDtypeStruct((lay.tx, d), BF16), (yf, yb, xbc, proj, drow, nw))


def _gate_bwd(lay, mx, yf, yb, xbc, proj, drow, nw, dcat, name):
    d = mx.d_ssm
    nsx = lay.nsx

    def body(yf_ref, yb_ref, xh_ref, z_ref, d_ref, w_ref, dc_ref, dy_ref, dz_ref, dd_ref, dw_ref):
        s = pl.program_id(1)

        @pl.when(s < nsx)
        def _():
            _, vjp = jax.vjp(_gate_f, yf_ref[...], yb_ref[...], xh_ref[...], z_ref[...], d_ref[...], w_ref[...])
            dyf, _, _, dz, dd, dw = vjp(dc_ref[...])
            dy_ref[...] = dyf
            dz_ref[...] = dz.astype(dz_ref.dtype)
            _acc(dd_ref, s == 0, dd)
            _acc(dw_ref, s == 0, dw)

        @pl.when(s >= nsx)
        def _():
            dy_ref[...] = jnp.zeros_like(dy_ref)
            dz_ref[...] = jnp.zeros_like(dz_ref)

    t = _tok(lay, d)
    return _tok_call(name, body, 1, lay.ns, [t, t, t, t, _glob(1, d), _glob(1, d), _tok(lay, d, clamp=nsx - 1)],
                     [t, t, _glob(1, d), _glob(1, d)],
                     [jax.ShapeDtypeStruct((lay.ta, d), F32), jax.ShapeDtypeStruct((lay.ta, d), BF16),
                      jax.ShapeDtypeStruct((1, d), F32), jax.ShapeDtypeStruct((1, d), F32)],
                     (yf, yb, xbc, proj, drow, nw, dcat))


def _glu_fwd(lay, mx, proj, name):
    d = mx.d_conv
    c = math.gcd(mx.off_glu, d)
    cb = mx.off_glu // c

    def body(a_ref, b_ref, o_ref):
        o_ref[...] = a_ref[...] * jax.nn.sigmoid(b_ref[...])

    return _tok_call(name, body, d // c, lay.nsx, [_tok(lay, c, cb), _tok(lay, c, cb + d // c)], _tok(lay, c),
                     jax.ShapeDtypeStruct((lay.tx, d), F32), (proj, proj))


def _glu_bwd(lay, mx, proj, du, name):
    d = mx.d_conv
    c = math.gcd(mx.off_glu, d)
    cb = mx.off_glu // c
    nsx = lay.nsx

    def body(a_ref, b_ref, du_ref, da_ref, db_ref):
        s = pl.program_id(1)

        @pl.when(s < nsx)
        def _():
            sg = jax.nn.sigmoid(b_ref[...])
            duv = du_ref[...]
            da_ref[...] = (duv * sg).astype(da_ref.dtype)
            db_ref[...] = (duv * a_ref[...] * sg * (1.0 - sg)).astype(db_ref.dtype)

        @pl.when(s >= nsx)
        def _():
            da_ref[...] = jnp.zeros_like(da_ref)
            db_ref[...] = jnp.zeros_like(db_ref)

    sh = jax.ShapeDtypeStruct((lay.ta, d), BF16)
    return _tok_call(name, body, d // c, lay.ns,
                     [_tok(lay, c, cb), _tok(lay, c, cb + d // c), _tok(lay, c, clamp=nsx - 1)],
                     [_tok(lay, c), _tok(lay, c)], [sh, sh], (proj, proj, du))


def _axial(lay, mx, u, dy, cw, cb, name):
    d, seq = mx.d_conv, lay.seq
    kw = cw.shape[0]
    pad = kw // 2
    c = _pick(d // 2, (256, 128))
    ncb = d // c
    zpad = GRID_W * pad
    zpad = -(-zpad // 8) * 8
    backward = dy is not None

    def shifted(ext_ref, off):
        return ext_ref[pl.ds(zpad + off, seq), :]

    def valid_row(off):
        col = lax.broadcasted_iota(jnp.int32, (seq, c), 0) % GRID_W
        return jnp.logical_and(col + off >= 0, col + off < GRID_W)

    def fill(ext_ref, v):
        ext_ref[0:zpad, :] = jnp.zeros((zpad, c), F32)
        ext_ref[zpad:zpad + seq, :] = v
        ext_ref[zpad + seq:, :] = jnp.zeros((zpad, c), F32)

    def conv(ext_ref, w_ref, is_row, sign):
        acc = jnp.zeros((seq, c), F32)
        for k in range(kw):
            off = sign * ((k - pad) if is_row else GRID_W * (k - pad))
            v = shifted(ext_ref, off)
            if is_row:
                v = jnp.where(valid_row(off), v, 0.0)
            acc = acc + w_ref[k:k + 1, :] * v
        return acc

    def fwd_body(u_ref, w_ref, b_ref, o_ref, ext_ref):
        j = pl.program_id(0)
        fill(ext_ref, u_ref[...])

        @pl.when(j < ncb // 2)
        def _():
            o_ref[...] = conv(ext_ref, w_ref, True, 1) + b_ref[...]

        @pl.when(j >= ncb // 2)
        def _():
            o_ref[...] = conv(ext_ref, w_ref, False, 1) + b_ref[...]

    def bwd_body(u_ref, dy_ref, w_ref, du_ref, dw_ref, db_ref, extu_ref, extd_ref):
        j, b = pl.program_id(0), pl.program_id(1)
        dyv = dy_ref[...]
        fill(extu_ref, u_ref[...])
        fill(extd_ref, dyv)

        def grads(is_row):
            du_ref[...] = conv(extd_ref, w_ref, is_row, -1)
            rows = []
            for k in range(kw):
                off = (k - pad) if is_row else GRID_W * (k - pad)
                v = shifted(extu_ref, off)
                if is_row:
                    v = jnp.where(valid_row(off), v, 0.0)
                rows.append(jnp.sum(dyv * v, axis=0, keepdims=True))
            _acc(dw_ref, b == 0, jnp.concatenate(rows, axis=0))

        @pl.when(j < ncb // 2)
        def _():
            grads(True)

        @pl.when(j >= ncb // 2)
        def _():
            grads(False)

        _acc(db_ref, b == 0, jnp.sum(dyv, axis=0, keepdims=True))

    seq_spec = pl.BlockSpec((seq, c), lambda j, b: (b, j))
    w_spec = pl.BlockSpec((kw, c), lambda j, b: (0, j))
    b_spec = pl.BlockSpec((1, c), lambda j, b: (0, j))
    ext = pltpu.VMEM((seq + 2 * zpad, c), F32)
    if not backward:
        return pl.pallas_call(fwd_body, grid=(ncb, lay.bl), in_specs=[seq_spec, w_spec, b_spec], out_specs=seq_spec,
                              out_shape=jax.ShapeDtypeStruct((lay.tx, d), F32), scratch_shapes=[ext], name=name,
                              compiler_params=_cparams(2))(u, cw, cb)
    return pl.pallas_call(bwd_body, grid=(ncb, lay.bl), in_specs=[seq_spec, seq_spec, w_spec],
                          out_specs=[seq_spec, w_spec, b_spec],
                          out_shape=[jax.ShapeDtypeStruct((lay.tx, d), F32), jax.ShapeDtypeStruct((kw, d), F32),
                                     jax.ShapeDtypeStruct((1, d), F32)],
                          scratch_shapes=[ext, ext], name=name, compiler_params=_cparams(2))(u, dy, cw)


def _ln_silu_f(u, w, b):
    mu = jnp.mean(u, axis=-1, keepdims=True)
    var = jnp.mean(jnp.square(u - mu), axis=-1, keepdims=True)
    return _silu((u - mu) * lax.rsqrt(var + EPS) * w + b)


def _ln_fwd(lay, mx, u, w, b, name):
    d = mx.d_conv

    def body(u_ref, w_ref, b_ref, o_ref):
        o_ref[...] = _ln_silu_f(u_ref[...], w_ref[...], b_ref[...]).astype(o_ref.dtype)

    return _tok_call(name, body, 1, lay.nsx, [_tok(lay, d), _glob(1, d), _glob(1, d)], _tok(lay, d),
                     jax.ShapeDtypeStruct((lay.tx, d), BF16), (u, w, b))


def _ln_bwd(lay, mx, u, w, b, dcat, name):
    d = mx.d_conv

    def body(u_ref, w_ref, b_ref, dc_ref, du_ref, dw_ref, db_ref):
        s = pl.program_id(1)
        _, vjp = jax.vjp(_ln_silu_f, u_ref[...], w_ref[...], b_ref[...])
        du, dw, db = vjp(dc_ref[...])
        du_ref[...] = du
        _acc(dw_ref, s == 0, dw)
        _acc(db_ref, s == 0, db)

    return _tok_call(name, body, 1, lay.nsx, [_tok(lay, d), _glob(1, d), _glob(1, d), _tok(lay, d, 1)],
                     [_tok(lay, d), _glob(1, d), _glob(1, d)],
                     [jax.ShapeDtypeStruct((lay.tx, d), F32), jax.ShapeDtypeStruct((1, d), F32),
                      jax.ShapeDtypeStruct((1, d), F32)], (u, w, b, dcat))


def _gate_tile(dff):
    return dff // 2 if (dff // 2) % 128 == 0 else dff


def _interleave(wg, wu, ft):
    return jnp.concatenate([t[:, k:k + ft] for k in range(0, wg.shape[1], ft) for t in (wg, wu)], axis=1)


def _deinterleave(wgu, ft):
    n = wgu.shape[1]
    gate = jnp.concatenate([wgu[:, k:k + ft] for k in range(0, n, 2 * ft)], axis=1)
    up = jnp.concatenate([wgu[:, k + ft:k + 2 * ft] for k in range(0, n, 2 * ft)], axis=1)
    return gate, up


def _ffn_fwd(lay, nseg, x, nw, modv, k0, wts, tag, ride=None):
    wgu, wd, ft = wts
    h = _norm_mod_fwd(lay, nseg, x, nw, modv, k0, tag + "_norm")
    t = h.shape[0]

    def act(acc):
        g, u = acc[:, :ft], acc[:, ft:]
        sg = jax.nn.sigmoid(g)
        sl = g * sg
        return jnp.concatenate([u * (sg * (1.0 + g * (1.0 - sg))), sl], axis=1), sl * u

    res = _mm(h, wgu, "nn", None, tag + "_gu", tn=2 * ft, tm=256 if t % 256 == 0 else None,
              epilogue=act, outs=[(2 * ft, BF16), (ft, BF16)], ride=ride)
    (s, a), rode = res if ride else (res, None)
    o = _mm(a, wd, "nn", F32, tag + "_down")
    y = _resid_fwd(lay, nseg, x, o, modv, k0 + 2, 0.5, tag + "_res")
    return y, (x, h, s, a, o), rode


def _ffn_bwd(lay, nseg, dy, saved, nw, modv, k0, wts, tag, nout=None, rides=(None, None)):
    wgu, wd, ft = wts
    x, h, s, a, o = saved
    do, dgate = _resid_bwd(lay, nseg, dy, o, modv, k0 + 2, 0.5, tag + "_dres")

    def through_act(da, s_tile):
        return (jnp.concatenate([da, da], axis=1) * s_tile.astype(F32),)

    (dgu,) = _mm(do, wd.T, "nn", None, tag + "_da", tn=ft, tm=_big_tile(do.shape[0], 1024), extras=[(s, 2 * ft)],
                 epilogue=through_act, outs=[(2 * ft, BF16)])
    dwd = _mm(a, do, "tn", F32, tag + "_dwd")
    dh = _mm(dgu, wgu, "nt", F32, tag + "_dh", ride=rides[0])
    dwgu = _mm(h, dgu, "tn", F32, tag + "_dwgu", ride=rides[1])
    (dh, rode_a), (dwgu, rode_b) = (dh if rides[0] else (dh, None)), (dwgu if rides[1] else (dwgu, None))
    dx, dnw, dss = _norm_mod_bwd(lay, nseg, nseg, x, nw, modv, k0, dh, dy, tag + "_dnorm", nout=nout)
    return dx, (dwgu, dwd), dnw, jnp.concatenate([dss, dgate], axis=1), (rode_a, rode_b)


def _local_step(lay, mx, xa, target, modv, w, exch=None):
    d, bl = lay.d, lay.bl
    g = {}
    xa1, ffn1, late = _ffn_fwd(lay, lay.ns, xa, w["norm_ffn1"], modv, 0, w["ffn1"], "ffn1",
                               ride=exch.late_weights if exch else None)
    if exch:
        w = {**w, **exch.unpack_late(late)}
    ha = _norm_mod_fwd(lay, lay.ns, xa1, w["norm_mix"], modv, 3, "mix_norm")
    proj = _mm(ha, w["w_in"], "nn", F32, "mix_in")
    pre, xbc = _conv5_fwd(lay.fine(), mx, proj, w["conv_w"], w["conv_b"], "mix_conv")
    dt = _dt_fwd(lay, mx, proj, w["dt_bias"], "mix_dt")
    dtt = dt.T
    yf, hpf = _ssd_fwd(lay, mx, xbc, dt, dtt, w["a_row"], w["a_col"], False, "ssd_f")
    yb, hpb = _ssd_fwd(lay, mx, xbc, dt, dtt, w["a_row"], w["a_col"], True, "ssd_b")
    cat_y = _gate_fwd(lay, mx, yf, yb, xbc, proj, w["d_row"], w["ssm_norm_w"], "mix_gate")
    u0 = _glu_fwd(lay, mx, proj, "mix_glu")
    uc = _axial(lay, mx, u0, None, w["cconv_w"], w["cconv_b"], "mix_axial")
    cat_u = _ln_fwd(lay, mx, uc, w["ln_w"], w["ln_b"], "mix_ln")
    cat = jnp.concatenate([cat_y, cat_u], axis=1)
    mix = _mm(cat, w["w_out"], "nn", F32, "mix_out")
    x2 = _resid_fwd(lay, lay.nsx, xa1, mix, modv, 5, 1.0, "mix_res")
    x3, ffn2, _ = _ffn_fwd(lay, lay.nsx, x2, w["norm_ffn2"], modv, 6, w["ffn2"], "ffn2")
    loss, dx3, g["final_norm"] = _final_loss(lay, x3, w["final_norm"], target, "loss")
    dx2, g["ffn2"], g["norm_ffn2"], dmod2, _ = _ffn_bwd(lay, lay.nsx, dx3, ffn2, w["norm_ffn2"], modv, 6, w["ffn2"], "ffn2")
    dmix, dg2 = _resid_bwd(lay, lay.nsx, dx2, mix, modv, 5, 1.0, "mix_dres")
    dcat = _mm(dmix, w["w_out"], "nt", F32, "mix_dcat")
    g["w_out"] = _mm(cat, dmix, "tn", F32, "mix_dwout")
    duc, g["ln_w"], g["ln_b"] = _ln_bwd(lay, mx, uc, w["ln_w"], w["ln_b"], dcat, "mix_dln")
    du0, g["cconv_w"], g["cconv_b"] = _axial(lay, mx, u0, duc, w["cconv_w"], None, "mix_daxial")
    dglu_a, dglu_b = _glu_bwd(lay, mx, proj, du0, "mix_dglu")
    dyssd, dz, g["d_row"], g["ssm_norm_w"] = _gate_bwd(lay, mx, yf, yb, xbc, proj, w["d_row"], w["ssm_norm_w"], dcat,
                                                       "mix_dgate")
    dxf, ddcf, ddrf, darf, dacf = _ssd_bwd(lay, mx, xbc, dt, dtt, w["a_row"], w["a_col"], hpf, dyssd, w["d_row"],
                                           False, "ssd_df")
    dxb, ddcb, ddrb, darb, dacb = _ssd_bwd(lay, mx, xbc, dt, dtt, w["a_row"], w["a_col"], hpb, dyssd, None,
                                           True, "ssd_db")
    g["a_row"] = (darf + darb) + (dacf + dacb).T
    dxbc, g["conv_w"], g["conv_b"] = _conv5_bwd(lay.fine(), mx, proj, pre, dxf, dxb, w["conv_w"], "mix_dconv")
    ddtraw, g["dt_bias"] = _dt_bwd(lay, mx, proj, w["dt_bias"], (ddcf, ddcb, ddrf.T, ddrb.T), "mix_ddt")
    dproj = jnp.concatenate([dz, ddtraw, dxbc, dglu_a, dglu_b], axis=1)
    dha = _mm(dproj, w["w_in"], "nt", F32, "mix_dha")
    g["w_in"] = _mm(ha, dproj, "tn", F32, "mix_dwin")
    dxa1, g["norm_mix"], dss_mix = _norm_mod_bwd(lay, lay.ns, lay.nsx, xa1, w["norm_mix"], modv, 3, dha, dx2, "mix_dnorm")
    rides = exch.early_grads(g) if exch else (None, None)
    dx, g["ffn1"], g["norm_ffn1"], dmod1, rode = _ffn_bwd(lay, lay.ns, dxa1, ffn1, w["norm_ffn1"], modv, 0, w["ffn1"],
                                                         "ffn1", nout=lay.nsx, rides=rides)
    if exch:
        exch.take_early(rode)
    zrow = lambda t: jnp.concatenate([t, jnp.zeros((1,) + t.shape[1:], F32)], axis=0)
    dmodv = jnp.concatenate([dmod1, dss_mix, zrow(dg2), zrow(dmod2)], axis=1)
    return loss, dx, g, dmodv


class _GatherRide:
    def __init__(self, xs):
        self.arrays = list(xs)
        self.na = len(xs)
        self.out_shapes = [jax.ShapeDtypeStruct((N_DEV,) + tuple(x.shape), x.dtype) for x in xs]
        self.sems = [pltpu.SemaphoreType.DMA((7 * self.na,)), pltpu.SemaphoreType.DMA((7 * self.na,)),
                     pltpu.SemaphoreType.DMA((self.na,))]

    def _plan(self, x_refs, out_refs, send_sems, recv_sems, local_sems):
        mx_, my_, mc_ = lax.axis_index("x"), lax.axis_index("y"), lax.axis_index("c")
        me, sibling = (mx_, my_, mc_), (mx_, my_, 1 - mc_)
        chips = [(1 - mx_, my_), (mx_, 1 - my_), (1 - mx_, 1 - my_)]

        def slot(a, px, py, pc):
            return out_refs[a].at[4 * px + 2 * py + pc]

        def copy(a, k, block, to, own=False):
            return pltpu.make_async_remote_copy(
                src_ref=x_refs[a] if own else slot(a, *block), dst_ref=slot(a, *block),
                send_sem=send_sems.at[7 * a + k], recv_sem=recv_sems.at[7 * a + k], device_id=to, device_id_type=MESH)

        mine = [pltpu.make_async_copy(x_refs[a], slot(a, *me), local_sems.at[a]) for a in range(self.na)]
        first = []
        for a in range(self.na):
            first.append(copy(a, 0, me, sibling, own=True))
            first += [copy(a, 1 + j, me, (*chip, mc_), own=True) for j, chip in enumerate(chips)]
        return me, sibling, chips, mc_, copy, mine, first

    def start(self, x_refs, out_refs, send_sems, recv_sems, local_sems):
        *_, mine, first = self._plan(x_refs, out_refs, send_sems, recv_sems, local_sems)
        for cp in mine + first:
            cp.start()

    def finish(self, x_refs, out_refs, send_sems, recv_sems, local_sems):
        me, sibling, chips, mc_, copy, mine, first = self._plan(x_refs, out_refs, send_sems, recv_sems, local_sems)
        passed = []
        for j, chip in enumerate(chips):
            for a in range(self.na):
                copy(a, 1 + j, (*chip, mc_), me).wait_recv()
                fwd = copy(a, 4 + j, (*chip, mc_), sibling)
                fwd.start()
                passed.append(fwd)
        for a in range(self.na):
            copy(a, 0, sibling, me).wait_recv()
            for j, chip in enumerate(chips):
                copy(a, 4 + j, (*chip, 1 - mc_), me).wait_recv()
        for cp in first + passed:
            cp.wait_send()
        for cp in mine:
            cp.wait()


def _exchange(ride, name, in_hbm=True):
    n_in, n_out = len(ride.arrays), len(ride.out_shapes)

    def body(*refs):
        ins, outs, sems = refs[:n_in], refs[n_in:n_in + n_out], refs[n_in + n_out:]
        ride.start(ins, outs, *sems)
        ride.finish(ins, outs, *sems)

    space = pl.BlockSpec(memory_space=pl.ANY if in_hbm else pltpu.VMEM)
    return pl.pallas_call(body, out_shape=list(ride.out_shapes), in_specs=[space] * n_in, out_specs=[space] * n_out,
                          scratch_shapes=list(ride.sems), name=name)(*ride.arrays)


def _all_gather(xs, name, in_hbm):
    return _exchange(_GatherRide(xs), name, in_hbm)


N_CHIPS = 4


def _swap_sibling(gs, name):
    na = len(gs)

    def body(*refs):
        g_refs, out_refs, send_sems, recv_sems = refs[:na], refs[na:2 * na], refs[2 * na], refs[2 * na + 1]
        mx_, my_, mc_ = lax.axis_index("x"), lax.axis_index("y"), lax.axis_index("c")
        copies = [pltpu.make_async_remote_copy(
            src_ref=g_refs[a].at[k, 1 - mc_], dst_ref=out_refs[a].at[k], send_sem=send_sems.at[N_CHIPS * a + k],
            recv_sem=recv_sems.at[N_CHIPS * a + k], device_id=(mx_, my_, 1 - mc_), device_id_type=MESH)
            for a in range(na) for k in range(N_CHIPS)]
        for cp in copies:
            cp.start()
        for cp in copies:
            cp.wait_recv()
        for cp in copies:
            cp.wait_send()

    return pl.pallas_call(
        body, out_shape=[jax.ShapeDtypeStruct((N_CHIPS,) + tuple(g.shape[2:]), g.dtype) for g in gs],
        in_specs=[pl.BlockSpec(memory_space=pl.ANY)] * na, out_specs=[pl.BlockSpec(memory_space=pl.ANY)] * na,
        scratch_shapes=[pltpu.SemaphoreType.DMA((N_CHIPS * na,)), pltpu.SemaphoreType.DMA((N_CHIPS * na,))], name=name,
    )(*gs)


def _row_tile(r, n):
    if r * n * 4 <= (1 << 20):
        return r
    for t in (1024, 512, 256, 128, 64, 32, 16, 8):
        if r % t == 0 and t * n * 4 <= (1 << 20):
            return t
    return r


def _pair_add(place, g, got, name):
    _, _, r, n = g.shape
    tr = _row_tile(r, n)

    def body(place_ref, g_ref, got_ref, o_ref, ob_ref):
        s = g_ref[...] + got_ref[...]
        o_ref[...] = s
        ob_ref[...] = s.astype(ob_ref.dtype)

    blk = pl.BlockSpec((None, tr, n), lambda k, i, pr: (k, i, 0))
    grid_spec = pltpu.PrefetchScalarGridSpec(
        num_scalar_prefetch=1, grid=(N_CHIPS, r // tr),
        in_specs=[pl.BlockSpec((None, None, tr, n), lambda k, i, pr: (k, pr[0], i, 0)), blk], out_specs=[blk, blk])
    return pl.pallas_call(body, grid_spec=grid_spec,
                          out_shape=[jax.ShapeDtypeStruct((N_CHIPS, r, n), F32), jax.ShapeDtypeStruct((N_CHIPS, r, n), BF16)],
                          name=name, compiler_params=_cparams(2))(place, g, got)


class _ChipSwapRide:
    def __init__(self, ps):
        self.arrays = list(ps)
        self.na = len(ps)
        self.out_shapes = [jax.ShapeDtypeStruct((N_CHIP_PEERS,) + tuple(p.shape[1:]), p.dtype) for p in ps]
        self.sems = [pltpu.SemaphoreType.DMA((N_CHIP_PEERS * self.na,)), pltpu.SemaphoreType.DMA((N_CHIP_PEERS * self.na,))]

    def _copies(self, p_refs, out_refs, send_sems, recv_sems):
        mx_, my_, mc_ = lax.axis_index("x"), lax.axis_index("y"), lax.axis_index("c")
        chips = [(1 - mx_, my_), (mx_, 1 - my_), (1 - mx_, 1 - my_)]
        return [pltpu.make_async_remote_copy(
            src_ref=p_refs[a].at[2 * cx + cy], dst_ref=out_refs[a].at[j], send_sem=send_sems.at[N_CHIP_PEERS * a + j],
            recv_sem=recv_sems.at[N_CHIP_PEERS * a + j], device_id=(cx, cy, mc_), device_id_type=MESH)
            for a in range(self.na) for j, (cx, cy) in enumerate(chips)]

    def start(self, p_refs, out_refs, send_sems, recv_sems):
        for cp in self._copies(p_refs, out_refs, send_sems, recv_sems):
            cp.start()

    def finish(self, p_refs, out_refs, send_sems, recv_sems):
        copies = self._copies(p_refs, out_refs, send_sems, recv_sems)
        for cp in copies:
            cp.wait_recv()
        for cp in copies:
            cp.wait_send()


def _sum_lead(x, name):
    k, r, n = x.shape
    tr = _row_tile(r, n * k)

    def body(x_ref, o_ref):
        acc = x_ref[0]
        for i in range(1, k):
            acc = acc + x_ref[i]
        o_ref[...] = acc

    return pl.pallas_call(body, grid=(r // tr,), in_specs=[pl.BlockSpec((k, tr, n), lambda i: (0, i, 0))],
                          out_specs=pl.BlockSpec((tr, n), lambda i: (i, 0)),
                          out_shape=jax.ShapeDtypeStruct((r, n), x.dtype), name=name, compiler_params=_cparams(1))(x)


def _adamw(place, w, parts, m, v, name):
    shape = w.shape
    cols = shape[-1]
    rows = math.prod(shape[:-1])
    to2 = lambda t: t.reshape(rows, cols)
    tr = _row_tile(rows, cols) if rows * cols * 4 > (1 << 20) else rows
    npart = len(parts)
    spec = pl.BlockSpec((tr, cols), lambda i, pr: (i, 0))
    part_specs, part_args = [], []
    for piece in parts:
        if isinstance(piece, tuple):
            stack, k, row0 = piece
            part_args.append(stack.reshape(stack.shape[0], -1, cols))
            assert row0 % tr == 0
            if k == "chip":
                part_specs.append(pl.BlockSpec((None, tr, cols), functools.partial(lambda i, pr, b0: (pr[1], i + b0, 0),
                                                                                   b0=row0 // tr)))
            else:
                part_specs.append(pl.BlockSpec((None, tr, cols), functools.partial(
                    lambda i, pr, kk, b0: (kk, i + b0, 0), kk=k, b0=row0 // tr)))
        else:
            part_args.append(to2(piece))
            part_specs.append(spec)

    def body(place_ref, *refs):
        w_ref, m_ref, v_ref = refs[0], refs[1 + npart], refs[2 + npart]
        g_ref, d_ref, nm_ref, nv_ref = refs[3 + npart:]
        g = refs[1][...].astype(F32)
        for q in range(1, npart):
            g = g + refs[1 + q][...].astype(F32)
        mm = ADAM_B1 * m_ref[...] + (1.0 - ADAM_B1) * g
        vv = ADAM_B2 * v_ref[...] + (1.0 - ADAM_B2) * jnp.square(g)
        m_hat = mm / (1.0 - ADAM_B1 ** ADAM_STEP)
        v_hat = vv / (1.0 - ADAM_B2 ** ADAM_STEP)
        g_ref[...] = g
        d_ref[...] = -ADAM_LR * (m_hat / (jnp.sqrt(v_hat) + ADAM_EPS) + ADAM_WD * w_ref[...])
        nm_ref[...] = mm
        nv_ref[...] = vv

    sh = jax.ShapeDtypeStruct((rows, cols), F32)
    grid_spec = pltpu.PrefetchScalarGridSpec(num_scalar_prefetch=1, grid=(rows // tr,),
                                             in_specs=[spec] + part_specs + [spec, spec], out_specs=[spec] * 4)
    outs = pl.pallas_call(body, grid_spec=grid_spec, out_shape=[sh] * 4, name=name, compiler_params=_cparams(1),
                          )(place, to2(w), *part_args, to2(m), to2(v))
    return tuple(o.reshape(shape) for o in outs)


def _packed_rows(n, width):
    return -(-n // (8 * width)) * 8


def _pack_rows(items, width):
    rows = []
    for t in items:
        flat = t.reshape(-1)
        n = flat.shape[0]
        k = _packed_rows(n, width)
        if k * width > n:
            flat = jnp.concatenate([flat, jnp.zeros((k * width - n,), t.dtype)])
        rows.append(flat.reshape(k, width))
    return jnp.concatenate(rows, axis=0)


def _unpack_rows(packed, shapes, lead=()):
    width = packed.shape[-1]
    out, r = [], 0
    for sh in shapes:
        n = math.prod(sh)
        k = _packed_rows(n, width)
        piece = packed[..., r:r + k, :].reshape(tuple(lead) + (k * width,))[..., :n]
        out.append(piece.reshape(tuple(lead) + tuple(sh)))
        r += k
    return out


def _cols_full(t):
    return jnp.transpose(t, (1, 0, 2)).reshape(t.shape[1], -1)


def _cols_shards(t):
    d = t.shape[0]
    return jnp.transpose(t.reshape(d, N_DEV, -1), (1, 0, 2))


BIG = ("ffn1_gate", "ffn1_up", "ffn1_down", "w_in", "w_out", "ffn2_gate", "ffn2_up", "ffn2_down")
ROW_SHARDED = ("ffn1_down", "w_out", "ffn2_down")


def kernel(x, c, ctx, c_ctx, w_mod, b_mod, norm_ffn1, ffn1_gate, ffn1_up, ffn1_down, norm_mix, w_in, ssm_conv_w, ssm_conv_b, dt_bias_fwd, dt_bias_bwd, a_log_fwd, a_log_bwd, ssm_d, ssm_norm_w, cconv_w, cconv_b, cconv_ln_w, cconv_ln_b, w_out, norm_ffn2, ffn2_gate, ffn2_up, ffn2_down, final_norm, loss_target, m_c_ctx, m_w_mod, m_b_mod, m_norm_ffn1, m_ffn1_gate, m_ffn1_up, m_ffn1_down, m_norm_mix, m_w_in, m_ssm_conv_w, m_ssm_conv_b, m_dt_bias_fwd, m_dt_bias_bwd, m_a_log_fwd, m_a_log_bwd, m_ssm_d, m_ssm_norm_w, m_cconv_w, m_cconv_b, m_cconv_ln_w, m_cconv_ln_b, m_w_out, m_norm_ffn2, m_ffn2_gate, m_ffn2_up, m_ffn2_down, m_final_norm, v_c_ctx, v_w_mod, v_b_mod, v_norm_ffn1, v_ffn1_gate, v_ffn1_up, v_ffn1_down, v_norm_mix, v_w_in, v_ssm_conv_w, v_ssm_conv_b, v_dt_bias_fwd, v_dt_bias_bwd, v_a_log_fwd, v_a_log_bwd, v_ssm_d, v_ssm_norm_w, v_cconv_w, v_cconv_b, v_cconv_ln_w, v_cconv_ln_b, v_w_out, v_norm_ffn2, v_ffn2_gate, v_ffn2_up, v_ffn2_down, v_final_norm):
    args = dict(locals())
    names = ("c_ctx", "w_mod", "b_mod", "norm_ffn1", "ffn1_gate", "ffn1_up", "ffn1_down", "norm_mix", "w_in",
             "ssm_conv_w", "ssm_conv_b", "dt_bias_fwd", "dt_bias_bwd", "a_log_fwd", "a_log_bwd", "ssm_d", "ssm_norm_w",
             "cconv_w", "cconv_b", "cconv_ln_w", "cconv_ln_b", "w_out", "norm_ffn2", "ffn2_gate", "ffn2_up",
             "ffn2_down", "final_norm")
    wts = {n: args[n] for n in names}
    bl, seq, d = x.shape
    clen = ctx.shape[1]
    heads = dt_bias_fwd.shape[1]
    ft = _gate_tile(ffn1_gate.shape[2] * N_DEV)
    lay = _Lay(bl, seq, clen, d)
    mx = _Mix(d, heads)
    nb = bl * N_DEV
    me = 4 * lax.axis_index("x") + 2 * lax.axis_index("y") + lax.axis_index("c")
    mcols = w_mod.shape[2]
    n_ctx_mod = 5 * d

    place = jnp.stack([lax.axis_index("c"), 2 * lax.axis_index("x") + lax.axis_index("y")]).astype(jnp.int32)

    small_shapes = [(bl, d), ssm_conv_w.shape[1:], cconv_w.shape[1:]]
    (g1,) = _all_gather([_pack_rows([c, ssm_conv_w, cconv_w], d)], "gather_small", False)
    c_g, conv_g, cconv_g = _unpack_rows(g1, small_shapes, (N_DEV,))
    c_all = c_g.reshape(nb, d)
    conv_w_full = jnp.transpose(conv_g, (1, 0, 2)).reshape(conv_g.shape[1], -1)
    cconv_w_full = jnp.transpose(cconv_g, (1, 0, 2)).reshape(cconv_g.shape[1], -1)

    s_all = jnp.concatenate([_silu(c_all), _silu(c_ctx)[None, :], jnp.zeros((7, d), F32)], axis=0)
    mod_cols = _mm(s_all, w_mod[0], "nn", F32, "mod_cols")
    (g2,) = _all_gather([mod_cols], "gather_mod", False)
    mod_all = _cols_full(g2) + b_mod
    mod_mine = jnp.concatenate([lax.dynamic_slice_in_dim(mod_all, me * bl, bl, axis=0), mod_all[nb:nb + 1]], axis=0)
    modv = mod_mine.reshape(bl + 1, N_MOD, d)

    hh = 2 * heads
    shard16 = lambda n: wts[n][0].astype(BF16)

    nl = ffn1_gate.shape[2]
    spt = ft // nl
    assert ft % nl == 0 and N_DEV % spt == 0

    def ffn_weights(gate, up, down):
        both = jnp.stack([gate, up], axis=1).reshape(N_DEV // spt, spt, 2, d, nl)
        return jnp.transpose(both, (3, 0, 2, 1, 4)).reshape(d, -1), down.reshape(-1, d), ft

    def grads_by_dest(name, grad):
        if name == "w_in":
            grad = _cols_shards(jnp.concatenate([grad[:, :d], grad[:, mx.off_x:mx.off_glu],
                                                 grad[:, mx.off_dt:mx.off_dt + hh], grad[:, mx.off_glu:]], axis=1))
        elif name.endswith("_gu"):
            grad = jnp.transpose(grad.reshape(d, N_DEV // spt, 2, spt, nl), (1, 3, 2, 0, 4)).reshape(N_DEV, 2 * d, nl)
        else:
            grad = grad.reshape((N_DEV,) + tuple(wts[name].shape[1:]))
        return grad.reshape((N_CHIPS, 2) + tuple(grad.shape[1:]))

    def ffn_grads(tag, pair):
        return {tag + "_gu": pair[0], tag + "_down": pair[1]}

    def to_chip_sums(named):
        names_ = list(named)
        by_dest = [grads_by_dest(n, named[n]) for n in names_]
        got = _swap_sibling(by_dest, "rs_sibling_" + names_[0])
        return {n: _pair_add(place, t, s, "rs_pair_add_" + n) for n, t, s in zip(names_, by_dest, got)}

    class _Overlap:
        late_names = ("w_in", "w_out", "ffn2_gate", "ffn2_up", "ffn2_down")
        late_weights = _GatherRide([shard16(n) for n in late_names])
        sums, arrived = {}, {}

        def unpack_late(self, outs):
            full = dict(zip(self.late_names, outs))
            w_in_f = _cols_full(full["w_in"])
            w_in_p = jnp.concatenate([w_in_f[:, :d], w_in_f[:, mx.ref_dt:mx.ref_glu], jnp.zeros((d, DT_PAD - hh), BF16),
                                      w_in_f[:, mx.ref_x:mx.ref_dt], w_in_f[:, mx.ref_glu:]], axis=1)
            return {"w_in": w_in_p, "w_out": full["w_out"].reshape(-1, d),
                    "ffn2": ffn_weights(full["ffn2_gate"], full["ffn2_up"], full["ffn2_down"])}

        def early_grads(self, g):
            self.sums = to_chip_sums({**ffn_grads("ffn2", g["ffn2"]), "w_in": g["w_in"], "w_out": g["w_out"]})
            self.groups = (("ffn2_gu", "ffn2_down"), ("w_in", "w_out"))
            return tuple(_ChipSwapRide([self.sums[n][1] for n in grp]) for grp in self.groups)

        def take_early(self, rode):
            for grp, outs in zip(self.groups, rode):
                self.arrived.update(zip(grp, outs))

    exch = _Overlap()
    full = dict(zip(BIG[:3], _all_gather([shard16(n) for n in BIG[:3]], "gather_weights", True)))
    lanes_pad = lambda a, b: jnp.concatenate([a, b, jnp.zeros((1, DT_LANES - hh), F32)], axis=1)
    a_vals = lanes_pad(-jnp.exp(a_log_fwd), -jnp.exp(a_log_bwd))
    w = {
        "norm_ffn1": norm_ffn1, "norm_mix": norm_mix, "norm_ffn2": norm_ffn2, "final_norm": final_norm[None, :],
        "ffn1": ffn_weights(full["ffn1_gate"], full["ffn1_up"], full["ffn1_down"]),
        "conv_w": jnp.concatenate([conv_w_full, jnp.zeros((3, mx.xw), F32)], axis=0), "conv_b": ssm_conv_b,
        "dt_bias": lanes_pad(dt_bias_fwd, dt_bias_bwd), "a_row": a_vals, "a_col": a_vals.T,
        "d_row": jnp.repeat(ssm_d, HEAD_DIM, axis=1), "ssm_norm_w": ssm_norm_w,
        "cconv_w": cconv_w_full, "cconv_b": cconv_b, "ln_w": cconv_ln_w, "ln_b": cconv_ln_b,
    }

    xa = jnp.concatenate([x.reshape(bl * seq, d), ctx.reshape(bl * clen, d)], axis=0)
    loss, grad_x, g, dmodv = _local_step(lay, mx, xa, loss_target.reshape(bl * seq, d), modv, w, exch)
    loss = lax.psum(loss[0, 0], ("x", "y", "c"))

    sums = {**exch.sums, **to_chip_sums(ffn_grads("ffn1", g["ffn1"]))}
    arrived = dict(exch.arrived)
    last = ("ffn1_gu", "ffn1_down")
    arrived.update(zip(last, _exchange(_ChipSwapRide([sums[n][1] for n in last]), "rs_chips")))

    def big_parts(n):
        key, row0 = (n[:4] + "_gu", d if n.endswith("_up") else 0) if n.endswith(("_gate", "_up")) else (n, 0)
        return [(sums[key][0], "chip", row0)] + [(arrived[key], k, row0) for k in range(N_CHIP_PEERS)]

    n9 = N_MOD * d
    dmod_rows = dmodv.reshape(bl + 1, n9)
    ctx_row = jnp.concatenate([dmod_rows[bl, :n_ctx_mod], jnp.zeros((n9 - n_ctx_mod,), F32)])
    summed = [ctx_row, g["norm_ffn1"], g["norm_mix"], g["norm_ffn2"], g["final_norm"], g["conv_b"], g["dt_bias"],
              g["a_row"], g["d_row"], g["ssm_norm_w"], g["cconv_b"], g["ln_w"], g["ln_b"], g["conv_w"][:5], g["cconv_w"]]
    sum_shapes = [t.shape for t in summed]
    (g4,) = _all_gather([_pack_rows([dmod_rows[:bl]] + summed, d)], "gather_small_grads", False)
    dmod_batch = g4[:, :bl * N_MOD].reshape(nb, n9)
    tot = _sum_lead(g4[:, _packed_rows(bl * n9, d):], "sum_small_grads")
    (dctx, g_n1, g_nm, g_n2, g_fn, g_cb, g_dtb, g_a, g_drow, g_snw, g_ccb, g_lnw, g_lnb, g_cw, g_ccw) = _unpack_rows(tot, sum_shapes)
    dmod_all = jnp.concatenate([dmod_batch, dctx[None, :], jnp.zeros((7, n9), F32)], axis=0)

    dmod_my_cols = lax.dynamic_slice_in_dim(dmod_all, me * mcols, mcols, axis=1)
    g_w_mod = _mm(s_all, dmod_my_cols, "tn", F32, "dw_mod")[None]
    g_b_mod = _sum_lead(dmod_all.reshape(nb + 8, N_MOD, d), "db_mod").reshape(1, n9)
    ds_part = _mm(dmod_my_cols[nb:nb + 8], w_mod[0], "nt", F32, "ds_ctx")
    (g5,) = _all_gather([jnp.concatenate([ds_part[0:1], jnp.zeros((7, d), F32)], axis=0)], "gather_ds_ctx", False)
    ds_ctx = _sum_lead(g5, "sum_ds_ctx")[0]
    sg = jax.nn.sigmoid(c_ctx)
    g_c_ctx = ds_ctx * (sg * (1.0 + c_ctx * (1.0 - sg)))

    a_f, a_b = a_vals[:, :heads], a_vals[:, heads:hh]
    grads = {
        "c_ctx": [g_c_ctx], "w_mod": [g_w_mod], "b_mod": [g_b_mod],
        "norm_ffn1": [g_n1], "norm_mix": [g_nm], "norm_ffn2": [g_n2], "final_norm": [g_fn.reshape(-1)],
        "ssm_conv_w": [lax.dynamic_slice_in_dim(g_cw, me * ssm_conv_w.shape[2], ssm_conv_w.shape[2], axis=1)[None]],
        "ssm_conv_b": [g_cb],
        "dt_bias_fwd": [g_dtb[:, :heads]], "dt_bias_bwd": [g_dtb[:, heads:hh]],
        "a_log_fwd": [g_a[:, :heads] * a_f], "a_log_bwd": [g_a[:, heads:hh] * a_b],
        "ssm_d": [jnp.sum(g_drow.reshape(1, heads, HEAD_DIM), axis=2)], "ssm_norm_w": [g_snw],
        "cconv_w": [lax.dynamic_slice_in_dim(g_ccw, me * cconv_w.shape[2], cconv_w.shape[2], axis=1)[None]],
        "cconv_b": [g_ccb], "cconv_ln_w": [g_lnw], "cconv_ln_b": [g_lnb],
    }
    for n in BIG:
        grads[n] = big_parts(n)

    out_g, out_d, out_m, out_v = [], [], [], []
    for n in names:
        gr, de, nm, nv = _adamw(place, wts[n], grads[n], args["m_" + n], args["v_" + n], "adamw_" + n)
        out_g.append(gr)
        out_d.append(de)
        out_m.append(nm)
        out_v.append(nv)
    return (loss, grad_x.reshape(bl, seq, d), *out_g, *out_d, *out_m, *out_v)
```

```python
import functools
import math

import jax
import jax.numpy as jnp
from jax import lax
from jax.experimental import pallas as pl
from jax.experimental.pallas import tpu as pltpu

F32 = jnp.float32
BF16 = jnp.bfloat16
MESH = pl.DeviceIdType.MESH

N_DEV = 8
N_CHIP_PEERS = 3
HEAD_DIM = 64
N_STATE = 128
SSD_GROUPS = 2
CHUNK = 128
GRID_W = 64
N_MOD = 9
EPS = 1e-6
DT_PAD = 512
DT_LANES = 128
HALO = 8
ROW_TILE = 512
FINE_ROW_TILE = 256
VMEM_LIMIT = 48 * 1024 * 1024
NEG_BIG = -1e30

ADAM_LR = 0.001
ADAM_B1 = 0.9
ADAM_B2 = 0.999
ADAM_EPS = 1e-08
ADAM_WD = 0.01
ADAM_STEP = 10


def _pick(n, prefs):
    for p in prefs:
        if n % p == 0:
            return p
    return n


MM_TILE_CAP = 2816
MM_TILE_ELEMS = 3 << 20
MM_OUT_TILE_ELEMS = 3 << 19
MM_FULL_ROWS = 1024


def _big_tile(n, cap):
    if n <= cap:
        return n
    best = 0
    for t in range(128, cap + 1, 128):
        if n % t == 0:
            best = t
    return best or n


def _cparams(ndim):
    return pltpu.CompilerParams(dimension_semantics=("arbitrary",) * ndim, vmem_limit_bytes=VMEM_LIMIT)


def _silu(v):
    return v * jax.nn.sigmoid(v)


def _mm(a, b, mode, out_dtype, name, tn=None, tm=None, extras=(), epilogue=None, outs=None, ride=None):
    if mode == "tn":
        (K, M), (K2, N) = a.shape, b.shape
    elif mode == "nt":
        (M, K), (N, K2) = a.shape, b.shape
    else:
        (M, K), (K2, N) = a.shape, b.shape
    assert K == K2, (name, a.shape, b.shape)
    tm = M if M <= MM_FULL_ROWS else tm
    if tn is None:
        tn = _big_tile(N, min(MM_TILE_CAP, max(128, MM_OUT_TILE_ELEMS // (tm or 512))))
    if tm is None:
        tm = _big_tile(M, max(128, MM_OUT_TILE_ELEMS // tn))
    tk = _big_tile(K, min(MM_TILE_CAP, MM_TILE_ELEMS // max(tn, tm)))
    nk = K // tk
    ni, nj = M // tm, N // tn
    swap = nk == 1 and (K * N + M * K * nj) < (M * K + K * N * ni)
    ij = (lambda g0, g1: (g1, g0)) if swap else (lambda g0, g1: (g0, g1))
    if mode == "tn":
        a_spec = pl.BlockSpec((tk, tm), lambda g0, g1, k: (k, ij(g0, g1)[0]))
        dn = (((0,), (0,)), ((), ()))
    else:
        a_spec = pl.BlockSpec((tm, tk), lambda g0, g1, k: (ij(g0, g1)[0], k))
        dn = (((1,), (1,)), ((), ())) if mode == "nt" else (((1,), (0,)), ((), ()))
    if mode == "nt":
        b_spec = pl.BlockSpec((tn, tk), lambda g0, g1, k: (ij(g0, g1)[1], k))
    else:
        b_spec = pl.BlockSpec((tk, tn), lambda g0, g1, k: (k, ij(g0, g1)[1]))
    if outs is None:
        outs = [(tn, out_dtype)]
    nx = len(extras)

    def tile(w):
        return pl.BlockSpec((tm, w), lambda g0, g1, k: ij(g0, g1))

    def finish(acc, refs):
        vals = (acc,) if epilogue is None else epilogue(acc, *[r[...] for r in refs[:nx]])
        for o_ref, v in zip(refs[nx:], vals):
            o_ref[...] = v.astype(o_ref.dtype)

    grid = (nj, ni, nk) if swap else (ni, nj, nk)
    nout = len(outs)
    r_in = len(ride.arrays) if ride else 0
    r_out = len(ride.out_shapes) if ride else 0

    def compute(a_ref, b_ref, refs):
        part = lax.dot_general(a_ref[...].astype(BF16), b_ref[...].astype(BF16), dn, preferred_element_type=F32)
        if nk == 1:
            finish(part, refs)
            return
        acc_ref, k = refs[-1], pl.program_id(2)
        _acc(acc_ref, k == 0, part)

        @pl.when(k == nk - 1)
        def _():
            finish(acc_ref[...], refs[:-1])

    def body(a_ref, b_ref, *refs):
        if ride is None:
            compute(a_ref, b_ref, refs)
            return
        x_refs, rin = refs[:nx], refs[nx:nx + r_in]
        o_refs, rout = refs[nx + r_in:nx + r_in + nout], refs[nx + r_in + nout:nx + r_in + nout + r_out]
        tail = refs[nx + r_in + nout + r_out:]
        nacc = 1 if nk > 1 else 0
        sems = tail[nacc:]
        ids = [pl.program_id(q) for q in range(3)]
        first = functools.reduce(jnp.logical_and, [i == 0 for i in ids])
        last = functools.reduce(jnp.logical_and, [i == n - 1 for i, n in zip(ids, grid)])
        pl.when(first)(lambda: ride.start(rin, rout, *sems))
        compute(a_ref, b_ref, tuple(x_refs) + tuple(o_refs) + tuple(tail[:nacc]))
        pl.when(last)(lambda: ride.finish(rin, rout, *sems))

    hbm = pl.BlockSpec(memory_space=pl.ANY)
    res = pl.pallas_call(
        body, grid=grid, in_specs=[a_spec, b_spec] + [tile(w) for _, w in extras] + [hbm] * r_in,
        out_specs=[tile(w) for w, _ in outs] + [hbm] * r_out,
        out_shape=[jax.ShapeDtypeStruct((M, nj * w), dt) for w, dt in outs] + (list(ride.out_shapes) if ride else []),
        scratch_shapes=([pltpu.VMEM((tm, tn), F32)] if nk > 1 else []) + (list(ride.sems) if ride else []),
        name=name, compiler_params=_cparams(3),
    )(a, b, *[x for x, _ in extras], *(ride.arrays if ride else []))
    main = res[0] if epilogue is None else res[:nout]
    return (main, res[nout:]) if ride else main


def _ffn_tm(t):
    return _pick(t, (1024, 512, 256, 128, 64, 32, 16, 8))


def _dsilu_mul(g, u, da):
    sg = jax.nn.sigmoid(g)
    return da * u * (sg * (1.0 + g * (1.0 - sg))), da * (g * sg)


def _ffn_gu(h, wg, wu, name):
    t, d = h.shape
    nb, _, fb = wg.shape
    tm = _ffn_tm(t)

    def body(h_ref, wg_ref, wu_ref, g_ref, u_ref, a_ref):
        hv = h_ref[...]
        g = jnp.dot(hv, wg_ref[...], preferred_element_type=F32)
        u = jnp.dot(hv, wu_ref[...], preferred_element_type=F32)
        g_ref[...] = g
        u_ref[...] = u
        a_ref[...] = (_silu(g) * u).astype(a_ref.dtype)

    wspec = pl.BlockSpec((None, d, fb), lambda i, j: (j, 0, 0))
    ospec = pl.BlockSpec((None, tm, fb), lambda i, j: (j, i, 0))
    sh = jax.ShapeDtypeStruct((nb, t, fb), F32)
    return pl.pallas_call(body, grid=(t // tm, nb), in_specs=[pl.BlockSpec((tm, d), lambda i, j: (i, 0)), wspec, wspec],
                          out_specs=[ospec, ospec, ospec], out_shape=[sh, sh, jax.ShapeDtypeStruct((nb, t, fb), BF16)],
                          name=name, compiler_params=_cparams(2))(h, wg, wu)


def _ffn_contract(xs, ws, mode, name):
    nb, t, fb = xs[0].shape
    d = ws[0].shape[2] if mode == "nn" else ws[0].shape[1]
    tm = _ffn_tm(t)
    npair = len(xs)
    dn = (((1,), (0,)), ((), ())) if mode == "nn" else (((1,), (1,)), ((), ()))

    def body(*refs):
        o_ref, acc_ref = refs[2 * npair], refs[2 * npair + 1]
        j = pl.program_id(1)
        part = lax.dot_general(refs[0][...], refs[npair][...], dn, preferred_element_type=F32)
        for p in range(1, npair):
            part = part + lax.dot_general(refs[p][...], refs[npair + p][...], dn, preferred_element_type=F32)
        _acc(acc_ref, j == 0, part)

        @pl.when(j == nb - 1)
        def _():
            o_ref[...] = acc_ref[...]

    xspec = pl.BlockSpec((None, tm, fb), lambda i, j: (j, i, 0))
    wspec = pl.BlockSpec((None,) + tuple(ws[0].shape[1:]), lambda i, j: (j, 0, 0))
    return pl.pallas_call(body, grid=(t // tm, nb), in_specs=[xspec] * npair + [wspec] * npair,
                          out_specs=pl.BlockSpec((tm, d), lambda i, j: (i, 0)),
                          out_shape=jax.ShapeDtypeStruct((t, d), F32), scratch_shapes=[pltpu.VMEM((tm, d), F32)],
                          name=name, compiler_params=_cparams(2))(*xs, *ws)


def _ffn_da(do, wd, g, u, name):
    t, d = do.shape
    nb, fb, _ = wd.shape
    tm = _ffn_tm(t)

    def body(do_ref, wd_ref, g_ref, u_ref, dg_ref, du_ref):
        da = lax.dot_general(do_ref[...], wd_ref[...], (((1,), (1,)), ((), ())), preferred_element_type=F32)
        dg, du = _dsilu_mul(g_ref[...], u_ref[...], da)
        dg_ref[...] = dg.astype(dg_ref.dtype)
        du_ref[...] = du.astype(du_ref.dtype)

    bspec = pl.BlockSpec((None, tm, fb), lambda i, j: (j, i, 0))
    sh = jax.ShapeDtypeStruct((nb, t, fb), BF16)
    return pl.pallas_call(body, grid=(t // tm, nb),
                          in_specs=[pl.BlockSpec((tm, d), lambda i, j: (i, 0)),
                                    pl.BlockSpec((None, fb, d), lambda i, j: (j, 0, 0)), bspec, bspec],
                          out_specs=[bspec, bspec], out_shape=[sh, sh], name=name, compiler_params=_cparams(2),
                          )(do, wd, g, u)


def _ffn_dw(h, xs, name):
    t, d = h.shape
    nb, _, fb = xs[0].shape
    tk = _ffn_tm(t)
    nk = t // tk
    npair = len(xs)
    tn_dims = (((0,), (0,)), ((), ()))

    def body(*refs):
        h_ref = refs[0]
        k = pl.program_id(1)
        hv = h_ref[...]
        for p in range(npair):
            part = lax.dot_general(hv, refs[1 + p][...], tn_dims, preferred_element_type=F32)
            _acc(refs[1 + npair + p], k == 0, part)

    xspec = pl.BlockSpec((None, tk, fb), lambda j, k: (j, k, 0))
    ospec = pl.BlockSpec((None, d, fb), lambda j, k: (j, 0, 0))
    sh = jax.ShapeDtypeStruct((nb, d, fb), F32)
    return pl.pallas_call(body, grid=(nb, nk), in_specs=[pl.BlockSpec((tk, d), lambda j, k: (k, 0))] + [xspec] * npair,
                          out_specs=[ospec] * npair, out_shape=[sh] * npair, name=name, compiler_params=_cparams(2),
                          )(h, *xs)


def _ffn_dwd(a, do, name):
    nb, t, fb = a.shape
    d = do.shape[1]
    tk = _ffn_tm(t)

    def body(a_ref, do_ref, o_ref):
        part = lax.dot_general(a_ref[...], do_ref[...], (((0,), (0,)), ((), ())), preferred_element_type=F32)
        _acc(o_ref, pl.program_id(1) == 0, part)

    return pl.pallas_call(body, grid=(nb, t // tk),
                          in_specs=[pl.BlockSpec((None, tk, fb), lambda j, k: (j, k, 0)),
                                    pl.BlockSpec((tk, d), lambda j, k: (k, 0))],
                          out_specs=pl.BlockSpec((None, fb, d), lambda j, k: (j, 0, 0)),
                          out_shape=jax.ShapeDtypeStruct((nb, fb, d), F32), name=name, compiler_params=_cparams(2),
                          )(a, do)


class _Lay:
    def __init__(self, bl, seq, clen, d, tt=None):
        self.bl, self.seq, self.clen, self.d = bl, seq, clen, d
        self.tt = min(ROW_TILE, math.gcd(seq, bl * clen)) if tt is None else tt
        assert seq % self.tt == 0 and (bl * clen) % self.tt == 0 and self.tt % 8 == 0
        self.spb = seq // self.tt
        self.spc = clen // self.tt
        self.nsx = bl * self.spb
        self.nsc = bl * clen // self.tt
        self.ns = self.nsx + self.nsc
        self.tx = bl * seq
        self.ta = self.tx + bl * clen

    def fine(self):
        return _Lay(self.bl, self.seq, self.clen, self.d, min(FINE_ROW_TILE, self.clen))

    def mrow(self, s):
        return jnp.where(s < self.nsx, s // self.spb, self.bl)

    def first_of_row(self, s):
        return jnp.logical_or(jnp.logical_and(s < self.nsx, s % self.spb == 0), s == self.nsx)

    def seq_first(self, s):
        return jnp.where(s < self.nsx, s % self.spb == 0, (s - self.nsx) % self.spc == 0)

    def seq_last(self, s):
        return jnp.where(s < self.nsx, s % self.spb == self.spb - 1, (s - self.nsx) % self.spc == self.spc - 1)


def _tok(lay, c, cb=0, clamp=None):
    if clamp is None:
        return pl.BlockSpec((lay.tt, c), lambda j, s: (s, cb + j))
    return pl.BlockSpec((lay.tt, c), lambda j, s: (jnp.minimum(s, clamp), cb + j))


def _halo_prev(lay, c, cb=0):
    u = lay.tt // HALO
    return pl.BlockSpec((HALO, c), lambda j, s: (jnp.maximum(s * u - 1, 0), cb + j))


def _halo_next(lay, c, cb=0):
    u = lay.tt // HALO
    last = lay.ta // HALO - 1
    return pl.BlockSpec((HALO, c), lambda j, s: (jnp.minimum((s + 1) * u, last), cb + j))


def _row(lay, k, c):
    return pl.BlockSpec((None, k, c), lambda j, s: (lay.mrow(s), 0, 0))


def _glob(k, c, cb=None):
    if cb is None:
        return pl.BlockSpec((k, c), lambda j, s: (0, 0))
    return pl.BlockSpec((k, c), lambda j, s: (0, cb + j))


def _tok_call(name, body, ncb, nseg, in_specs, out_specs, out_shape, inputs, scratch=(), aliases=None):
    return pl.pallas_call(body, grid=(ncb, nseg), in_specs=in_specs, out_specs=out_specs, out_shape=out_shape,
                          scratch_shapes=list(scratch), name=name, compiler_params=_cparams(2),
                          input_output_aliases=aliases or {})(*inputs)


def _acc(ref, first, val):
    @pl.when(first)
    def _():
        ref[...] = val

    @pl.when(jnp.logical_not(first))
    def _():
        ref[...] += val


def _norm_mod_f(x, w, sh, sc):
    y = x * lax.rsqrt(jnp.mean(x * x, axis=-1, keepdims=True) + EPS) * w
    return y * (1.0 + sc) + sh


def _norm_mod_fwd(lay, nseg, x, w, modv, ksh, name):
    d = lay.d

    def body(x_ref, w_ref, m_ref, h_ref):
        h = _norm_mod_f(x_ref[...], w_ref[...], m_ref[ksh:ksh + 1, :], m_ref[ksh + 1:ksh + 2, :])
        h_ref[...] = h.astype(h_ref.dtype)

    return _tok_call(name, body, 1, nseg, [_tok(lay, d), _glob(1, d), _row(lay, N_MOD, d)], _tok(lay, d),
                     jax.ShapeDtypeStruct((nseg * lay.tt, d), BF16), (x, w, modv))


def _norm_mod_bwd(lay, nseg, nres, x, w, modv, ksh, dh, dres, name, nout=None):
    d = lay.d
    nrow = lay.bl + (1 if nseg > lay.nsx else 0)
    nout = nseg if nout is None else nout

    def body(x_ref, w_ref, m_ref, dh_ref, dres_ref, dx_ref, dw_ref, dm_ref):
        s = pl.program_id(1)
        _, vjp = jax.vjp(_norm_mod_f, x_ref[...], w_ref[...], m_ref[ksh:ksh + 1, :], m_ref[ksh + 1:ksh + 2, :])
        dx, dw, dsh, dsc = vjp(dh_ref[...])

        @pl.when(s < nout)
        def _():
            dx_ref[...] = dx + jnp.where(s < nres, dres_ref[...], 0.0)

        _acc(dw_ref, s == 0, dw)
        _acc(dm_ref, lay.first_of_row(s), jnp.concatenate([dsh, dsc], axis=0))

    return _tok_call(
        name, body, 1, nseg,
        [_tok(lay, d), _glob(1, d), _row(lay, N_MOD, d), _tok(lay, d), _tok(lay, d, clamp=nres - 1)],
        [_tok(lay, d, clamp=nout - 1), _glob(1, d), _row(lay, 2, d)],
        [jax.ShapeDtypeStruct((nout * lay.tt, d), F32), jax.ShapeDtypeStruct((1, d), F32),
         jax.ShapeDtypeStruct((nrow, 2, d), F32)],
        (x, w, modv, dh, dres))


def _resid_fwd(lay, nseg, x, o, modv, kg, coef, name):
    d = lay.d

    def body(x_ref, o_ref, m_ref, y_ref):
        y_ref[...] = x_ref[...] + (coef * m_ref[kg:kg + 1, :]) * o_ref[...]

    return _tok_call(name, body, 1, nseg, [_tok(lay, d), _tok(lay, d), _row(lay, N_MOD, d)], _tok(lay, d),
                     jax.ShapeDtypeStruct((nseg * lay.tt, d), F32), (x, o, modv))


def _resid_bwd(lay, nseg, dy, o, modv, kg, coef, name):
    d = lay.d
    nrow = lay.bl + (1 if nseg > lay.nsx else 0)

    def body(dy_ref, o_ref, m_ref, do_ref, dg_ref):
        s = pl.program_id(1)
        dy = dy_ref[...]
        do_ref[...] = (dy * (coef * m_ref[kg:kg + 1, :])).astype(do_ref.dtype)
        _acc(dg_ref, lay.first_of_row(s), jnp.sum(dy * o_ref[...], axis=0, keepdims=True) * coef)

    return _tok_call(name, body, 1, nseg, [_tok(lay, d), _tok(lay, d), _row(lay, N_MOD, d)],
                     [_tok(lay, d), _row(lay, 1, d)],
                     [jax.ShapeDtypeStruct((nseg * lay.tt, d), BF16), jax.ShapeDtypeStruct((nrow, 1, d), F32)],
                     (dy, o, modv))


def _final_loss(lay, x, wf, target, name):
    d = lay.d

    def body(x_ref, w_ref, t_ref, loss_ref, dx_ref, dw_ref):
        s = pl.program_id(1)

        def f(xv, wv):
            return xv * lax.rsqrt(jnp.mean(xv * xv, axis=-1, keepdims=True) + EPS) * wv

        y, vjp = jax.vjp(f, x_ref[...], w_ref[...])
        err = y - t_ref[...]
        part = 0.5 * jnp.sum(jnp.sum(err * err, axis=-1, keepdims=True), axis=0, keepdims=True) / d
        dx, dw = vjp(err / d)
        dx_ref[...] = dx
        _acc(loss_ref, s == 0, part)
        _acc(dw_ref, s == 0, dw)

    return _tok_call(name, body, 1, lay.nsx, [_tok(lay, d), _glob(1, d), _tok(lay, d)],
                     [_glob(1, 1), _tok(lay, d), _glob(1, d)],
                     [jax.ShapeDtypeStruct((1, 1), F32), jax.ShapeDtypeStruct((lay.tx, d), F32),
                      jax.ShapeDtypeStruct((1, d), F32)], (x, wf, target))


class _Mix:
    def __init__(self, d, heads):
        self.d_ssm = d
        self.d_conv = d
        self.heads = heads
        assert heads * HEAD_DIM == d and heads % (2 * SSD_GROUPS) == 0 and 2 * heads <= DT_LANES
        self.gn = SSD_GROUPS * N_STATE
        self.xw = d + 2 * self.gn
        self.off_dt = d
        self.off_x = d + DT_PAD
        self.off_glu = self.off_x + self.xw
        self.pw = self.off_glu + 2 * d
        self.ref_x = d
        self.ref_dt = d + self.xw
        self.ref_glu = self.ref_dt + 2 * heads
        self.cc = self.xw if self.off_x % self.xw == 0 else _pick(self.xw, (512, 256, 128))


def _conv5_fwd(lay, mx, proj, cw, cb, name):
    c, tt = mx.cc, lay.tt
    cb0 = mx.off_x // c
    assert mx.off_x % c == 0

    def body(prev_ref, cur_ref, next_ref, w_ref, b_ref, pre_ref, act_ref, ext_ref):
        s = pl.program_id(1)
        ext_ref[0:HALO, :] = jnp.where(lay.seq_first(s), 0.0, prev_ref[...])
        ext_ref[HALO:HALO + tt, :] = cur_ref[...]
        ext_ref[HALO + tt:, :] = jnp.where(lay.seq_last(s), 0.0, next_ref[...])
        acc = jnp.zeros((tt, c), F32) + b_ref[...]
        for k in range(5):
            acc = acc + w_ref[k:k + 1, :] * ext_ref[pl.ds(HALO + k - 2, tt), :]
        pre_ref[...] = acc
        act_ref[...] = _silu(acc)

    sh = jax.ShapeDtypeStruct((lay.ta, mx.xw), F32)
    return _tok_call(name, body, mx.xw // c, lay.ns,
                     [_halo_prev(lay, c, cb0), _tok(lay, c, cb0), _halo_next(lay, c, cb0), _glob(8, c, 0), _glob(1, c, 0)],
                     [_tok(lay, c), _tok(lay, c)], [sh, sh], (proj, proj, proj, cw, cb),
                     scratch=[pltpu.VMEM((tt + 2 * HALO, c), F32)])


def _conv5_bwd(lay, mx, proj, pre, dact_f, dact_b, cw, dproj, name):
    c, tt = mx.cc, lay.tt
    cb0 = mx.off_x // c

    def dsilu(p):
        sg = jax.nn.sigmoid(p)
        return sg * (1.0 + p * (1.0 - sg))

    def body(xp_ref, xc_ref, xn_ref, pp_ref, pc_ref, pn_ref, fp_ref, fc_ref, fn_ref, bp_ref, bc_ref, bn_ref, w_ref,
             buf_ref, dx_ref, dw_ref, db_ref, extx_ref, extd_ref):
        s = pl.program_id(1)
        first, last = lay.seq_first(s), lay.seq_last(s)
        dcur = (fc_ref[...] + bc_ref[...]) * dsilu(pc_ref[...])
        extd_ref[0:HALO, :] = jnp.where(first, 0.0, (fp_ref[...] + bp_ref[...]) * dsilu(pp_ref[...]))
        extd_ref[HALO:HALO + tt, :] = dcur
        extd_ref[HALO + tt:, :] = jnp.where(last, 0.0, (fn_ref[...] + bn_ref[...]) * dsilu(pn_ref[...]))
        extx_ref[0:HALO, :] = jnp.where(first, 0.0, xp_ref[...])
        extx_ref[HALO:HALO + tt, :] = xc_ref[...]
        extx_ref[HALO + tt:, :] = jnp.where(last, 0.0, xn_ref[...])
        dx = jnp.zeros((tt, c), F32)
        rows = []
        for k in range(5):
            dx = dx + w_ref[k:k + 1, :] * extd_ref[pl.ds(HALO - (k - 2), tt), :]
            rows.append(jnp.sum(dcur * extx_ref[pl.ds(HALO + k - 2, tt), :], axis=0, keepdims=True))
        dx_ref[...] = dx.astype(dx_ref.dtype)
        rows.append(jnp.zeros((3, c), F32))
        _acc(dw_ref, s == 0, jnp.concatenate(rows, axis=0))
        _acc(db_ref, s == 0, jnp.sum(dcur, axis=0, keepdims=True))

    three = lambda cbx: [_halo_prev(lay, c, cbx), _tok(lay, c, cbx), _halo_next(lay, c, cbx)]
    ext = pltpu.VMEM((tt + 2 * HALO, c), F32)
    return _tok_call(name, body, mx.xw // c, lay.ns,
                     three(cb0) + three(0) + three(0) + three(0) + [_glob(8, c, 0), pl.BlockSpec(memory_space=pl.ANY)],
                     [_tok(lay, c, cb0), _glob(8, c, 0), _glob(1, c, 0)],
                     [jax.ShapeDtypeStruct(dproj.shape, dproj.dtype), jax.ShapeDtypeStruct((8, mx.xw), F32),
                      jax.ShapeDtypeStruct((1, mx.xw), F32)],
                     (proj, proj, proj, pre, pre, pre, dact_f, dact_f, dact_f, dact_b, dact_b, dact_b, cw, dproj),
                     scratch=[ext, ext], aliases={13: 0})


def _softplus(v):
    return jnp.maximum(v, 0.0) + jnp.log1p(jnp.exp(-jnp.abs(v)))


def _dt_fwd(lay, mx, proj, bias, name):
    cb = mx.off_dt // DT_LANES

    def body(p_ref, b_ref, dt_ref):
        dt_ref[...] = _softplus(p_ref[...] + b_ref[...])

    return _tok_call(name, body, 1, lay.ns, [_tok(lay, DT_LANES, cb), _glob(1, DT_LANES)], _tok(lay, DT_LANES),
                     jax.ShapeDtypeStruct((lay.ta, DT_LANES), F32), (proj, bias))


def _dt_bwd(lay, mx, proj, bias, parts, dproj, name):
    cb = mx.off_dt // DT_LANES
    ncb = DT_PAD // DT_LANES

    def body(p_ref, b_ref, a_ref, b2_ref, c_ref, d_ref, buf_ref, dp_ref, db_ref):
        j, s = pl.program_id(0), pl.program_id(1)

        @pl.when(j == 0)
        def _():
            ddt = (a_ref[...] + b2_ref[...]) + (c_ref[...] + d_ref[...])
            draw = ddt * jax.nn.sigmoid(p_ref[...] + b_ref[...])
            dp_ref[...] = draw.astype(dp_ref.dtype)
            _acc(db_ref, s == 0, jnp.sum(draw, axis=0, keepdims=True))

        @pl.when(j > 0)
        def _():
            dp_ref[...] = jnp.zeros_like(dp_ref)

    t = pl.BlockSpec((lay.tt, DT_LANES), lambda j, s: (s, 0))
    return _tok_call(name, body, ncb, lay.ns,
                     [pl.BlockSpec((lay.tt, DT_LANES), lambda j, s: (s, cb)), _glob(1, DT_LANES), t, t, t, t,
                      pl.BlockSpec(memory_space=pl.ANY)],
                     [_tok(lay, DT_LANES, cb), _glob(1, DT_LANES)],
                     [jax.ShapeDtypeStruct(dproj.shape, dproj.dtype), jax.ShapeDtypeStruct((1, DT_LANES), F32)],
                     (proj, bias) + tuple(parts) + (dproj,), aliases={6: 0})


def _scan_mask(rev):
    r = lax.broadcasted_iota(jnp.int32, (CHUNK, CHUNK), 0)
    c = lax.broadcasted_iota(jnp.int32, (CHUNK, CHUNK), 1)
    return (r <= c) if rev else (r >= c)


def _split_bf16(x):
    hi = x.astype(BF16)
    return hi, (x - hi.astype(F32)).astype(BF16)


@functools.partial(jax.custom_vjp, nondiff_argnums=(0,))
def _cum_cols(rev, x):
    m = _scan_mask(rev).astype(BF16)
    hi, lo = _split_bf16(x)
    return jnp.dot(m, hi, preferred_element_type=F32) + jnp.dot(m, lo, preferred_element_type=F32)


_cum_cols.defvjp(lambda rev, x: (_cum_cols(rev, x), None), lambda rev, _, g: (_cum_cols(not rev, g),))


@functools.partial(jax.custom_vjp, nondiff_argnums=(0,))
def _cum_rows(rev, x):
    m = _scan_mask(not rev).astype(BF16)
    hi, lo = _split_bf16(x)
    return jnp.dot(hi, m, preferred_element_type=F32) + jnp.dot(lo, m, preferred_element_type=F32)


_cum_rows.defvjp(lambda rev, x: (_cum_rows(rev, x), None), lambda rev, _, g: (_cum_rows(not rev, g),))


@functools.partial(jax.custom_vjp, nondiff_argnums=(1,))
def _take_col(x, k):
    return x[:, k:k + 1]


def _take_col_bwd(k, _, g):
    lane = lax.broadcasted_iota(jnp.int32, (g.shape[0], DT_LANES), 1)
    return (jnp.where(lane == k, g, 0.0),)


_take_col.defvjp(lambda x, k: (x[:, k:k + 1], None), _take_col_bwd)


@functools.partial(jax.custom_vjp, nondiff_argnums=(1,))
def _take_row(x, k):
    return x[k:k + 1, :]


def _take_row_bwd(k, _, g):
    sub = lax.broadcasted_iota(jnp.int32, (DT_LANES, g.shape[1]), 0)
    return (jnp.where(sub == k, g, 0.0),)


_take_row.defvjp(lambda x, k: (x[k:k + 1, :], None), _take_row_bwd)


def _ssd_chunk(xh_pairs, bcs, ccs, dtc, dtr, a_row, a_col, st_pairs, *, rev, heads, col0):
    cs_c, cs_r, tot, scores = _ssd_shared(bcs, ccs, dtc, dtr, a_row, a_col, rev=rev)
    ppg = heads // (2 * SSD_GROUPS)
    ys, sts = [], []
    for g in range(SSD_GROUPS):
        y, st = _ssd_group(xh_pairs[g * ppg:(g + 1) * ppg], bcs[g], ccs[g], st_pairs[g], cs_c, cs_r, tot, dtc,
                           scores[g], rev=rev, col=col0 + 2 * ppg * g)
        ys.append(y)
        sts.append(st)
    return ys, sts


_NT = (((1,), (1,)), ((), ()))
_TN = (((0,), (0,)), ((), ()))


def _ssd_shared(bcs, ccs, dtc, dtr, a_row, a_col, *, rev):
    da_c = dtc * a_row
    cs_c = _cum_cols(rev, da_c)
    cs_r = _cum_rows(rev, dtr * a_col)
    tot = jnp.sum(da_c, axis=0, keepdims=True)
    scores = [lax.dot_general(ccs[g].astype(BF16), bcs[g].astype(BF16), _NT, preferred_element_type=F32)
              for g in range(SSD_GROUPS)]
    return cs_c, cs_r, tot, scores


def _ssd_group(xh_pairs, bc, cc, st, cs_c, cs_r, tot, dtc, score, *, rev, col):
    n = CHUNK
    mask = _scan_mask(rev)
    lane = lax.broadcasted_iota(jnp.int32, (n, DT_LANES), 1)
    sub = lax.broadcasted_iota(jnp.int32, (DT_LANES, n), 0)
    lane1 = lax.broadcasted_iota(jnp.int32, (1, DT_LANES), 1)
    left = lax.broadcasted_iota(jnp.int32, (n, 2 * HEAD_DIM), 1) < HEAD_DIM
    top = lax.broadcasted_iota(jnp.int32, (2 * HEAD_DIM, 1), 0) < HEAD_DIM
    xs_all, wst_all, ecs_all, edec_all, y_diag = [], [], [], [], []
    for p, xh in enumerate(xh_pairs):
        per = []
        for c in (col + 2 * p, col + 2 * p + 1):
            csv = jnp.sum(jnp.where(lane == c, cs_c, 0.0), axis=1, keepdims=True)
            csr = jnp.sum(jnp.where(sub == c, cs_r, 0.0), axis=0, keepdims=True)
            dtv = jnp.sum(jnp.where(lane == c, dtc, 0.0), axis=1, keepdims=True)
            tv = jnp.sum(jnp.where(lane1 == c, tot, 0.0), axis=1, keepdims=True)
            m = score * jnp.exp(jnp.where(mask, csv - csr, NEG_BIG))
            per.append((csv, dtv, tv, m))
        (cs1, dt1, t1, m1), (cs2, dt2, t2, m2) = per
        xs = xh * jnp.where(left, dt1, dt2)
        both = jnp.dot(jnp.concatenate([m1, m2], axis=0).astype(BF16), xs.astype(BF16), preferred_element_type=F32)
        y_diag.append(jnp.where(left, both[:n], both[n:]))
        xs_all.append(xs)
        ecs_all.append(jnp.where(left, jnp.exp(cs1), jnp.exp(cs2)))
        wst_all.append(jnp.where(left, jnp.exp(t1 - cs1), jnp.exp(t2 - cs2)))
        edec_all.append(jnp.where(top, jnp.exp(t1), jnp.exp(t2)))
    cat = lambda parts, axis: parts[0] if len(parts) == 1 else jnp.concatenate(parts, axis=axis)
    xs, wst, ecs = cat(xs_all, 1), cat(wst_all, 1), cat(ecs_all, 1)
    y_off = lax.dot_general(cc.astype(BF16), st.astype(BF16), _NT, preferred_element_type=F32) * ecs
    cst = lax.dot_general((xs * wst).astype(BF16), bc.astype(BF16), _TN, preferred_element_type=F32)
    return cat(y_diag, 1) + y_off, st * cat(edec_all, 0) + cst


class _Scan:
    def __init__(self, lay, rev):
        self.ncx, self.ncc, self.bl, self.rev = lay.seq // CHUNK, lay.clen // CHUNK, lay.bl, rev
        self.nct = self.ncx + self.ncc

    def chunk(self, b, pos):
        kc = (self.ncc - 1 - pos) if self.rev else pos
        kx = (self.ncx - 1 - (pos - self.ncc)) if self.rev else (pos - self.ncc)
        return jnp.where(pos < self.ncc, self.bl * self.ncx + b * self.ncc + kc, b * self.ncx + kx)


def _ssd_io(mx, x_ref, st_src):
    np_ = mx.heads // 2
    d = mx.d_ssm
    xh = [x_ref[:, 128 * p:128 * (p + 1)] for p in range(np_)]
    bcs = [x_ref[:, d + N_STATE * g:d + N_STATE * (g + 1)] for g in range(SSD_GROUPS)]
    ccs = [x_ref[:, d + mx.gn + N_STATE * g:d + mx.gn + N_STATE * (g + 1)] for g in range(SSD_GROUPS)]
    gw = d // SSD_GROUPS
    sts = [st_src[gw * g:gw * (g + 1), :] for g in range(SSD_GROUPS)]
    return xh, bcs, ccs, sts


def _ssd_fwd(lay, mx, xbc, dt, dtt, a_row, a_col, rev, name):
    sc = _Scan(lay, rev)
    col0 = mx.heads if rev else 0
    hp = mx.heads * HEAD_DIM

    def body(x_ref, dt_ref, dtt_ref, ar_ref, ac_ref, y_ref, hp_ref, st_ref):
        @pl.when(pl.program_id(1) == 0)
        def _():
            st_ref[...] = jnp.zeros_like(st_ref)

        hp_ref[...] = st_ref[...]
        xh, bcs, ccs, sts = _ssd_io(mx, x_ref, st_ref)
        ys, new = _ssd_chunk(xh, bcs, ccs, dt_ref[...], dtt_ref[...], ar_ref[...], ac_ref[...], sts,
                             rev=rev, heads=mx.heads, col0=col0)
        gw = mx.d_ssm // SSD_GROUPS
        for g in range(SSD_GROUPS):
            y_ref[:, gw * g:gw * (g + 1)] = ys[g]
            st_ref[gw * g:gw * (g + 1), :] = new[g]

    ch = sc.chunk
    return pl.pallas_call(
        body, grid=(lay.bl, sc.nct),
        in_specs=[pl.BlockSpec((CHUNK, mx.xw), lambda b, i: (ch(b, i), 0)),
                  pl.BlockSpec((CHUNK, DT_LANES), lambda b, i: (ch(b, i), 0)),
                  pl.BlockSpec((DT_LANES, CHUNK), lambda b, i: (0, ch(b, i))),
                  pl.BlockSpec((1, DT_LANES), lambda b, i: (0, 0)),
                  pl.BlockSpec((DT_LANES, 1), lambda b, i: (0, 0))],
        out_specs=[pl.BlockSpec((CHUNK, mx.d_ssm), lambda b, i: (ch(b, i), 0)),
                   pl.BlockSpec((hp, N_STATE), lambda b, i: (b * sc.nct + i, 0))],
        out_shape=[jax.ShapeDtypeStruct((lay.ta, mx.d_ssm), F32),
                   jax.ShapeDtypeStruct((lay.bl * sc.nct * hp, N_STATE), F32)],
        scratch_shapes=[pltpu.VMEM((hp, N_STATE), F32)], name=name, compiler_params=_cparams(2),
    )(xbc, dt, dtt, a_row, a_col)


def _ssd_bwd(lay, mx, xbc, dt, dtt, a_row, a_col, hprev, dy, dskip, rev, name):
    sc = _Scan(lay, rev)
    col0 = mx.heads if rev else 0
    hp = mx.heads * HEAD_DIM
    np_ = mx.heads // 2
    d = mx.d_ssm
    with_skip = dskip is not None

    def body(*refs):
        if with_skip:
            x_ref, dt_ref, dtt_ref, ar_ref, ac_ref, hp_ref, dy_ref, sk_ref = refs[:8]
            rest = refs[8:]
        else:
            x_ref, dt_ref, dtt_ref, ar_ref, ac_ref, hp_ref, dy_ref = refs[:7]
            rest = refs[7:]
        dx_ref, ddc_ref, ddr_ref, dar_ref, dac_ref, ds_ref = rest
        b, i = pl.program_id(0), pl.program_id(1)

        @pl.when(i == 0)
        def _():
            ds_ref[...] = jnp.zeros_like(ds_ref)

        xh, bcs, ccs, sts = _ssd_io(mx, x_ref, hp_ref)
        dtc = dt_ref[...]
        shared, vjp_shared = jax.vjp(functools.partial(_ssd_shared, rev=rev), bcs, ccs, dtc, dtt_ref[...],
                                     ar_ref[...], ac_ref[...])
        cs_c, cs_r, tot, scores = shared
        plus = lambda acc, v: v if acc is None else acc + v
        d_cs_c = d_cs_r = d_tot = ddc = None
        d_scores, dbc, dcc = [], [], []
        ppg = np_ // SSD_GROUPS
        gw = d // SSD_GROUPS
        for g in range(SSD_GROUPS):
            dyg = dy_ref[:, gw * g:gw * (g + 1)]
            fn = functools.partial(_ssd_group, rev=rev, col=col0 + 2 * ppg * g)
            _, vjp = jax.vjp(fn, xh[g * ppg:(g + 1) * ppg], bcs[g], ccs[g], sts[g], cs_c, cs_r, tot, dtc, scores[g])
            dxh, dbc_g, dcc_g, dst, dcs_c_g, dcs_r_g, dtot_g, ddc_g, dsc_g = vjp((dyg, ds_ref[gw * g:gw * (g + 1), :]))
            for q in range(ppg):
                p = g * ppg + q
                v = dxh[q]
                if with_skip:
                    v = v + dyg[:, 128 * q:128 * (q + 1)] * sk_ref[:, 128 * p:128 * (p + 1)]
                dx_ref[:, 128 * p:128 * (p + 1)] = v
            ds_ref[gw * g:gw * (g + 1), :] = dst
            d_cs_c, d_cs_r, d_tot, ddc = plus(d_cs_c, dcs_c_g), plus(d_cs_r, dcs_r_g), plus(d_tot, dtot_g), plus(ddc, ddc_g)
            d_scores.append(dsc_g)
            dbc.append(dbc_g)
            dcc.append(dcc_g)
        dbc_s, dcc_s, ddc_s, ddr, dar, dac = vjp_shared((d_cs_c, d_cs_r, d_tot, d_scores))
        ddc = ddc + ddc_s
        dbc = [dbc[g] + dbc_s[g] for g in range(SSD_GROUPS)]
        dcc = [dcc[g] + dcc_s[g] for g in range(SSD_GROUPS)]
        for g in range(SSD_GROUPS):
            dx_ref[:, d + N_STATE * g:d + N_STATE * (g + 1)] = dbc[g]
            dx_ref[:, d + mx.gn + N_STATE * g:d + mx.gn + N_STATE * (g + 1)] = dcc[g]
        ddc_ref[...] = ddc
        ddr_ref[...] = ddr
        first = jnp.logical_and(b == 0, i == 0)
        _acc(dar_ref, first, dar)
        _acc(dac_ref, first, dac)

    ch = lambda b, i: sc.chunk(b, sc.nct - 1 - i)
    in_specs = [pl.BlockSpec((CHUNK, mx.xw), lambda b, i: (ch(b, i), 0)),
                pl.BlockSpec((CHUNK, DT_LANES), lambda b, i: (ch(b, i), 0)),
                pl.BlockSpec((DT_LANES, CHUNK), lambda b, i: (0, ch(b, i))),
                pl.BlockSpec((1, DT_LANES), lambda b, i: (0, 0)),
                pl.BlockSpec((DT_LANES, 1), lambda b, i: (0, 0)),
                pl.BlockSpec((hp, N_STATE), lambda b, i: (b * sc.nct + sc.nct - 1 - i, 0)),
                pl.BlockSpec((CHUNK, d), lambda b, i: (ch(b, i), 0))]
    inputs = [xbc, dt, dtt, a_row, a_col, hprev, dy]
    if with_skip:
        in_specs.append(pl.BlockSpec((1, d), lambda b, i: (0, 0)))
        inputs.append(dskip)
    return pl.pallas_call(
        body, grid=(lay.bl, sc.nct), in_specs=in_specs,
        out_specs=[pl.BlockSpec((CHUNK, mx.xw), lambda b, i: (ch(b, i), 0)),
                   pl.BlockSpec((CHUNK, DT_LANES), lambda b, i: (ch(b, i), 0)),
                   pl.BlockSpec((DT_LANES, CHUNK), lambda b, i: (0, ch(b, i))),
                   pl.BlockSpec((1, DT_LANES), lambda b, i: (0, 0)),
                   pl.BlockSpec((DT_LANES, 1), lambda b, i: (0, 0))],
        out_shape=[jax.ShapeDtypeStruct((lay.ta, mx.xw), F32), jax.ShapeDtypeStruct((lay.ta, DT_LANES), F32),
                   jax.ShapeDtypeStruct((DT_LANES, lay.ta), F32), jax.ShapeDtypeStruct((1, DT_LANES), F32),
                   jax.ShapeDtypeStruct((DT_LANES, 1), F32)],
        scratch_shapes=[pltpu.VMEM((hp, N_STATE), F32)], name=name, compiler_params=_cparams(2),
    )(*inputs)


def _gate_f(yf, yb, xh, z, drow, nw):
    dd = yf.shape[-1]
    half = dd // SSD_GROUPS
    yz = (yf + yb + drow * xh) * _silu(z)
    lo = lax.broadcasted_iota(jnp.int32, yz.shape, 1) < half
    sq = yz * yz
    ms1 = jnp.sum(jnp.where(lo, sq, 0.0), axis=-1, keepdims=True) / half
    ms2 = jnp.sum(jnp.where(lo, 0.0, sq), axis=-1, keepdims=True) / half
    return yz * jnp.where(lo, lax.rsqrt(ms1 + EPS), lax.rsqrt(ms2 + EPS)) * nw


def _gate_fwd(lay, mx, yf, yb, xbc, proj, drow, nw, name):
    d = mx.d_ssm

    def body(yf_ref, yb_ref, xh_ref, z_ref, d_ref, w_ref, o_ref):
        o_ref[...] = _gate_f(yf_ref[...], yb_ref[...], xh_ref[...], z_ref[...], d_ref[...], w_ref[...]).astype(o_ref.dtype)

    t = _tok(lay, d)
    return _tok_call(name, body, 1, lay.nsx, [t, t, t, t, _glob(1, d), _glob(1, d)], t,
                     jax.ShapeDtypeStruct((lay.tx, d + mx.d_conv), BF16), (yf, yb, xbc, proj, drow, nw))


def _gate_bwd(lay, mx, yf, yb, xbc, proj, drow, nw, dcat, name):
    d = mx.d_ssm
    nsx = lay.nsx

    def body(yf_ref, yb_ref, xh_ref, z_ref, d_ref, w_ref, dc_ref, dy_ref, dz_ref, dd_ref, dw_ref):
        s = pl.program_id(1)

        @pl.when(s < nsx)
        def _():
            _, vjp = jax.vjp(_gate_f, yf_ref[...], yb_ref[...], xh_ref[...], z_ref[...], d_ref[...], w_ref[...])
            dyf, _, _, dz, dd, dw = vjp(dc_ref[...])
            dy_ref[...] = dyf
            dz_ref[...] = dz.astype(dz_ref.dtype)
            _acc(dd_ref, s == 0, dd)
            _acc(dw_ref, s == 0, dw)

        @pl.when(s >= nsx)
        def _():
            dy_ref[...] = jnp.zeros_like(dy_ref)
            dz_ref[...] = jnp.zeros_like(dz_ref)

    t = _tok(lay, d)
    return _tok_call(name, body, 1, lay.ns, [t, t, t, t, _glob(1, d), _glob(1, d), _tok(lay, d, clamp=nsx - 1)],
                     [t, t, _glob(1, d), _glob(1, d)],
                     [jax.ShapeDtypeStruct((lay.ta, d), F32), jax.ShapeDtypeStruct((lay.ta, mx.pw), BF16),
                      jax.ShapeDtypeStruct((1, d), F32), jax.ShapeDtypeStruct((1, d), F32)],
                     (yf, yb, xbc, proj, drow, nw, dcat))


def _glu_fwd(lay, mx, proj, name):
    d = mx.d_conv
    c = math.gcd(mx.off_glu, d)
    cb = mx.off_glu // c

    def body(a_ref, b_ref, o_ref):
        o_ref[...] = a_ref[...] * jax.nn.sigmoid(b_ref[...])

    return _tok_call(name, body, d // c, lay.nsx, [_tok(lay, c, cb), _tok(lay, c, cb + d // c)], _tok(lay, c),
                     jax.ShapeDtypeStruct((lay.tx, d), F32), (proj, proj))


def _glu_bwd(lay, mx, proj, du, dproj, name):
    d = mx.d_conv
    c = math.gcd(mx.off_glu, d)
    cb = mx.off_glu // c
    nc = d // c
    nsx = lay.nsx

    def body(a_ref, b_ref, du_ref, buf_ref, o_ref):
        j, s = pl.program_id(0), pl.program_id(1)

        @pl.when(s < nsx)
        def _():
            sg = jax.nn.sigmoid(b_ref[...])
            da = du_ref[...] * sg
            o_ref[...] = jnp.where(j < nc, da, da * a_ref[...] * (1.0 - sg)).astype(o_ref.dtype)

        @pl.when(s >= nsx)
        def _():
            o_ref[...] = jnp.zeros_like(o_ref)

    win = lambda half: pl.BlockSpec((lay.tt, c), lambda j, s: (s, cb + half * nc + j % nc))
    return _tok_call(name, body, 2 * nc, lay.ns,
                     [win(0), win(1), pl.BlockSpec((lay.tt, c), lambda j, s: (jnp.minimum(s, nsx - 1), j % nc)),
                      pl.BlockSpec(memory_space=pl.ANY)],
                     _tok(lay, c, cb), jax.ShapeDtypeStruct(dproj.shape, dproj.dtype), (proj, proj, du, dproj),
                     aliases={3: 0})


def _axial(lay, mx, u, dy, cw, cb, name):
    d, seq = mx.d_conv, lay.seq
    kw = cw.shape[0]
    pad = kw // 2
    c = _pick(d // 2, (256, 128))
    ncb = d // c
    zpad = GRID_W * pad
    zpad = -(-zpad // 8) * 8
    backward = dy is not None

    def shifted(ext_ref, off):
        return ext_ref[pl.ds(zpad + off, seq), :]

    def valid_row(off):
        col = lax.broadcasted_iota(jnp.int32, (seq, c), 0) % GRID_W
        return jnp.logical_and(col + off >= 0, col + off < GRID_W)

    def fill(ext_ref, v):
        ext_ref[0:zpad, :] = jnp.zeros((zpad, c), F32)
        ext_ref[zpad:zpad + seq, :] = v
        ext_ref[zpad + seq:, :] = jnp.zeros((zpad, c), F32)

    def conv(ext_ref, w_ref, is_row, sign):
        acc = jnp.zeros((seq, c), F32)
        for k in range(kw):
            off = sign * ((k - pad) if is_row else GRID_W * (k - pad))
            v = shifted(ext_ref, off)
            if is_row:
                v = jnp.where(valid_row(off), v, 0.0)
            acc = acc + w_ref[k:k + 1, :] * v
        return acc

    def fwd_body(u_ref, w_ref, b_ref, o_ref, ext_ref):
        j = pl.program_id(0)
        fill(ext_ref, u_ref[...])

        @pl.when(j < ncb // 2)
        def _():
            o_ref[...] = conv(ext_ref, w_ref, True, 1) + b_ref[...]

        @pl.when(j >= ncb // 2)
        def _():
            o_ref[...] = conv(ext_ref, w_ref, False, 1) + b_ref[...]

    def bwd_body(u_ref, dy_ref, w_ref, du_ref, dw_ref, db_ref, extu_ref, extd_ref):
        j, b = pl.program_id(0), pl.program_id(1)
        dyv = dy_ref[...]
        fill(extu_ref, u_ref[...])
        fill(extd_ref, dyv)

        def grads(is_row):
            du_ref[...] = conv(extd_ref, w_ref, is_row, -1)
            rows = []
            for k in range(kw):
                off = (k - pad) if is_row else GRID_W * (k - pad)
                v = shifted(extu_ref, off)
                if is_row:
                    v = jnp.where(valid_row(off), v, 0.0)
                rows.append(jnp.sum(dyv * v, axis=0, keepdims=True))
            _acc(dw_ref, b == 0, jnp.concatenate(rows, axis=0))

        @pl.when(j < ncb // 2)
        def _():
            grads(True)

        @pl.when(j >= ncb // 2)
        def _():
            grads(False)

        _acc(db_ref, b == 0, jnp.sum(dyv, axis=0, keepdims=True))

    seq_spec = pl.BlockSpec((seq, c), lambda j, b: (b, j))
    w_spec = pl.BlockSpec((kw, c), lambda j, b: (0, j))
    b_spec = pl.BlockSpec((1, c), lambda j, b: (0, j))
    ext = pltpu.VMEM((seq + 2 * zpad, c), F32)
    if not backward:
        return pl.pallas_call(fwd_body, grid=(ncb, lay.bl), in_specs=[seq_spec, w_spec, b_spec], out_specs=seq_spec,
                              out_shape=jax.ShapeDtypeStruct((lay.tx, d), F32), scratch_shapes=[ext], name=name,
                              compiler_params=_cparams(2))(u, cw, cb)
    return pl.pallas_call(bwd_body, grid=(ncb, lay.bl), in_specs=[seq_spec, seq_spec, w_spec],
                          out_specs=[seq_spec, w_spec, b_spec],
                          out_shape=[jax.ShapeDtypeStruct((lay.tx, d), F32), jax.ShapeDtypeStruct((kw, d), F32),
                                     jax.ShapeDtypeStruct((1, d), F32)],
                          scratch_shapes=[ext, ext], name=name, compiler_params=_cparams(2))(u, dy, cw)


def _ln_silu_f(u, w, b):
    mu = jnp.mean(u, axis=-1, keepdims=True)
    var = jnp.mean(jnp.square(u - mu), axis=-1, keepdims=True)
    return _silu((u - mu) * lax.rsqrt(var + EPS) * w + b)


def _ln_fwd(lay, mx, u, w, b, cat, name):
    d = mx.d_conv
    assert mx.d_ssm % d == 0

    def body(u_ref, w_ref, b_ref, cat_ref, o_ref):
        o_ref[...] = _ln_silu_f(u_ref[...], w_ref[...], b_ref[...]).astype(o_ref.dtype)

    return _tok_call(name, body, 1, lay.nsx,
                     [_tok(lay, d), _glob(1, d), _glob(1, d), pl.BlockSpec(memory_space=pl.ANY)],
                     _tok(lay, d, mx.d_ssm // d), jax.ShapeDtypeStruct(cat.shape, cat.dtype), (u, w, b, cat),
                     aliases={3: 0})


def _ln_bwd(lay, mx, u, w, b, dcat, name):
    d = mx.d_conv

    def body(u_ref, w_ref, b_ref, dc_ref, du_ref, dw_ref, db_ref):
        s = pl.program_id(1)
        _, vjp = jax.vjp(_ln_silu_f, u_ref[...], w_ref[...], b_ref[...])
        du, dw, db = vjp(dc_ref[...])
        du_ref[...] = du
        _acc(dw_ref, s == 0, dw)
        _acc(db_ref, s == 0, db)

    return _tok_call(name, body, 1, lay.nsx, [_tok(lay, d), _glob(1, d), _glob(1, d), _tok(lay, d, 1)],
                     [_tok(lay, d), _glob(1, d), _glob(1, d)],
                     [jax.ShapeDtypeStruct((lay.tx, d), F32), jax.ShapeDtypeStruct((1, d), F32),
                      jax.ShapeDtypeStruct((1, d), F32)], (u, w, b, dcat))


def _gate_tile(dff):
    return dff // 2 if (dff // 2) % 128 == 0 else dff


def _interleave(wg, wu, ft):
    return jnp.concatenate([t[:, k:k + ft] for k in range(0, wg.shape[1], ft) for t in (wg, wu)], axis=1)


def _deinterleave(wgu, ft):
    n = wgu.shape[1]
    gate = jnp.concatenate([wgu[:, k:k + ft] for k in range(0, n, 2 * ft)], axis=1)
    up = jnp.concatenate([wgu[:, k + ft:k + 2 * ft] for k in range(0, n, 2 * ft)], axis=1)
    return gate, up


def _ffn_fwd(lay, nseg, x, nw, modv, k0, wts, tag, ride=None):
    wgu, wd, ft = wts
    h = _norm_mod_fwd(lay, nseg, x, nw, modv, k0, tag + "_norm")
    t = h.shape[0]

    def act(acc):
        g, u = acc[:, :ft], acc[:, ft:]
        sg = jax.nn.sigmoid(g)
        sl = g * sg
        return jnp.concatenate([u * (sg * (1.0 + g * (1.0 - sg))), sl], axis=1), sl * u

    res = _mm(h, wgu, "nn", None, tag + "_gu", tn=2 * ft, tm=256 if t % 256 == 0 else None,
              epilogue=act, outs=[(2 * ft, BF16), (ft, BF16)], ride=ride)
    (s, a), rode = res if ride else (res, None)
    o = _mm(a, wd, "nn", F32, tag + "_down")
    y = _resid_fwd(lay, nseg, x, o, modv, k0 + 2, 0.5, tag + "_res")
    return y, (x, h, s, a, o), rode


def _ffn_bwd(lay, nseg, dy, saved, nw, modv, k0, wts, tag, nout=None, rides=(None, None)):
    wgu, wd, ft = wts
    x, h, s, a, o = saved
    do, dgate = _resid_bwd(lay, nseg, dy, o, modv, k0 + 2, 0.5, tag + "_dres")

    def through_act(da, s_tile):
        return (jnp.concatenate([da, da], axis=1) * s_tile.astype(F32),)

    (dgu,) = _mm(do, wd.T, "nn", None, tag + "_da", tn=ft, tm=_big_tile(do.shape[0], 1024), extras=[(s, 2 * ft)],
                 epilogue=through_act, outs=[(2 * ft, BF16)])
    dwd = _mm(a, do, "tn", F32, tag + "_dwd")
    dh = _mm(dgu, wgu, "nt", F32, tag + "_dh", ride=rides[0])
    dwgu = _mm(h, dgu, "tn", F32, tag + "_dwgu", ride=rides[1])
    (dh, rode_a), (dwgu, rode_b) = (dh if rides[0] else (dh, None)), (dwgu if rides[1] else (dwgu, None))
    dx, dnw, dss = _norm_mod_bwd(lay, nseg, nseg, x, nw, modv, k0, dh, dy, tag + "_dnorm", nout=nout)
    return dx, (dwgu, dwd), dnw, jnp.concatenate([dss, dgate], axis=1), (rode_a, rode_b)


def _local_step(lay, mx, xa, target, modv, w, exch=None):
    d, bl = lay.d, lay.bl
    g = {}
    xa1, ffn1, late = _ffn_fwd(lay, lay.ns, xa, w["norm_ffn1"], modv, 0, w["ffn1"], "ffn1",
                               ride=exch.late_weights if exch else None)
    if exch:
        w = {**w, **exch.unpack_late(late)}
    ha = _norm_mod_fwd(lay, lay.ns, xa1, w["norm_mix"], modv, 3, "mix_norm")
    proj = _mm(ha, w["w_in"], "nn", F32, "mix_in")
    pre, xbc = _conv5_fwd(lay.fine(), mx, proj, w["conv_w"], w["conv_b"], "mix_conv")
    dt = _dt_fwd(lay, mx, proj, w["dt_bias"], "mix_dt")
    dtt = dt.T
    yf, hpf = _ssd_fwd(lay, mx, xbc, dt, dtt, w["a_row"], w["a_col"], False, "ssd_f")
    yb, hpb = _ssd_fwd(lay, mx, xbc, dt, dtt, w["a_row"], w["a_col"], True, "ssd_b")
    cat_y = _gate_fwd(lay, mx, yf, yb, xbc, proj, w["d_row"], w["ssm_norm_w"], "mix_gate")
    u0 = _glu_fwd(lay, mx, proj, "mix_glu")
    uc = _axial(lay, mx, u0, None, w["cconv_w"], w["cconv_b"], "mix_axial")
    cat = _ln_fwd(lay, mx, uc, w["ln_w"], w["ln_b"], cat_y, "mix_ln")
    mix = _mm(cat, w["w_out"], "nn", F32, "mix_out")
    x2 = _resid_fwd(lay, lay.nsx, xa1, mix, modv, 5, 1.0, "mix_res")
    x3, ffn2, _ = _ffn_fwd(lay, lay.nsx, x2, w["norm_ffn2"], modv, 6, w["ffn2"], "ffn2")
    loss, dx3, g["final_norm"] = _final_loss(lay, x3, w["final_norm"], target, "loss")
    dx2, g["ffn2"], g["norm_ffn2"], dmod2, _ = _ffn_bwd(lay, lay.nsx, dx3, ffn2, w["norm_ffn2"], modv, 6, w["ffn2"], "ffn2")
    dmix, dg2 = _resid_bwd(lay, lay.nsx, dx2, mix, modv, 5, 1.0, "mix_dres")
    dcat = _mm(dmix, w["w_out"], "nt", F32, "mix_dcat")
    g["w_out"] = _mm(cat, dmix, "tn", F32, "mix_dwout")
    duc, g["ln_w"], g["ln_b"] = _ln_bwd(lay, mx, uc, w["ln_w"], w["ln_b"], dcat, "mix_dln")
    du0, g["cconv_w"], g["cconv_b"] = _axial(lay, mx, u0, duc, w["cconv_w"], None, "mix_daxial")
    dyssd, dproj, g["d_row"], g["ssm_norm_w"] = _gate_bwd(lay, mx, yf, yb, xbc, proj, w["d_row"], w["ssm_norm_w"], dcat,
                                                          "mix_dgate")
    dproj = _glu_bwd(lay, mx, proj, du0, dproj, "mix_dglu")
    dxf, ddcf, ddrf, darf, dacf = _ssd_bwd(lay, mx, xbc, dt, dtt, w["a_row"], w["a_col"], hpf, dyssd, w["d_row"],
                                           False, "ssd_df")
    dxb, ddcb, ddrb, darb, dacb = _ssd_bwd(lay, mx, xbc, dt, dtt, w["a_row"], w["a_col"], hpb, dyssd, None,
                                           True, "ssd_db")
    g["a_row"] = (darf + darb) + (dacf + dacb).T
    dproj, g["conv_w"], g["conv_b"] = _conv5_bwd(lay.fine(), mx, proj, pre, dxf, dxb, w["conv_w"], dproj, "mix_dconv")
    dproj, g["dt_bias"] = _dt_bwd(lay, mx, proj, w["dt_bias"], (ddcf, ddcb, ddrf.T, ddrb.T), dproj, "mix_ddt")
    dha = _mm(dproj, w["w_in"], "nt", F32, "mix_dha")
    g["w_in"] = _mm(ha, dproj, "tn", F32, "mix_dwin")
    dxa1, g["norm_mix"], dss_mix = _norm_mod_bwd(lay, lay.ns, lay.nsx, xa1, w["norm_mix"], modv, 3, dha, dx2, "mix_dnorm")
    rides = exch.early_grads(g) if exch else (None, None)
    dx, g["ffn1"], g["norm_ffn1"], dmod1, rode = _ffn_bwd(lay, lay.ns, dxa1, ffn1, w["norm_ffn1"], modv, 0, w["ffn1"],
                                                         "ffn1", nout=lay.nsx, rides=rides)
    if exch:
        exch.take_early(rode)
    zrow = lambda t: jnp.concatenate([t, jnp.zeros((1,) + t.shape[1:], F32)], axis=0)
    dmodv = jnp.concatenate([dmod1, dss_mix, zrow(dg2), zrow(dmod2)], axis=1)
    return loss, dx, g, dmodv


class _GatherRide:
    def __init__(self, xs):
        self.arrays = list(xs)
        self.na = len(xs)
        self.out_shapes = [jax.ShapeDtypeStruct((N_DEV,) + tuple(x.shape), x.dtype) for x in xs]
        self.sems = [pltpu.SemaphoreType.DMA((7 * self.na,)), pltpu.SemaphoreType.DMA((7 * self.na,)),
                     pltpu.SemaphoreType.DMA((self.na,))]

    def _plan(self, x_refs, out_refs, send_sems, recv_sems, local_sems):
        mx_, my_, mc_ = lax.axis_index("x"), lax.axis_index("y"), lax.axis_index("c")
        me, sibling = (mx_, my_, mc_), (mx_, my_, 1 - mc_)
        chips = [(1 - mx_, my_), (mx_, 1 - my_), (1 - mx_, 1 - my_)]

        def slot(a, px, py, pc):
            return out_refs[a].at[4 * px + 2 * py + pc]

        def copy(a, k, block, to, own=False):
            return pltpu.make_async_remote_copy(
                src_ref=x_refs[a] if own else slot(a, *block), dst_ref=slot(a, *block),
                send_sem=send_sems.at[7 * a + k], recv_sem=recv_sems.at[7 * a + k], device_id=to, device_id_type=MESH)

        mine = [pltpu.make_async_copy(x_refs[a], slot(a, *me), local_sems.at[a]) for a in range(self.na)]
        first = []
        for a in range(self.na):
            first.append(copy(a, 0, me, sibling, own=True))
            first += [copy(a, 1 + j, me, (*chip, mc_), own=True) for j, chip in enumerate(chips)]
        return me, sibling, chips, mc_, copy, mine, first

    def start(self, x_refs, out_refs, send_sems, recv_sems, local_sems):
        *_, mine, first = self._plan(x_refs, out_refs, send_sems, recv_sems, local_sems)
        for cp in mine + first:
            cp.start()

    def finish(self, x_refs, out_refs, send_sems, recv_sems, local_sems):
        me, sibling, chips, mc_, copy, mine, first = self._plan(x_refs, out_refs, send_sems, recv_sems, local_sems)
        passed = []
        for j, chip in enumerate(chips):
            for a in range(self.na):
                copy(a, 1 + j, (*chip, mc_), me).wait_recv()
                fwd = copy(a, 4 + j, (*chip, mc_), sibling)
                fwd.start()
                passed.append(fwd)
        for a in range(self.na):
            copy(a, 0, sibling, me).wait_recv()
            for j, chip in enumerate(chips):
                copy(a, 4 + j, (*chip, 1 - mc_), me).wait_recv()
        for cp in first + passed:
            cp.wait_send()
        for cp in mine:
            cp.wait()


def _exchange(ride, name, in_hbm=True):
    n_in, n_out = len(ride.arrays), len(ride.out_shapes)

    def body(*refs):
        ins, outs, sems = refs[:n_in], refs[n_in:n_in + n_out], refs[n_in + n_out:]
        ride.start(ins, outs, *sems)
        ride.finish(ins, outs, *sems)

    space = pl.BlockSpec(memory_space=pl.ANY if in_hbm else pltpu.VMEM)
    return pl.pallas_call(body, out_shape=list(ride.out_shapes), in_specs=[space] * n_in, out_specs=[space] * n_out,
                          scratch_shapes=list(ride.sems), name=name)(*ride.arrays)


def _all_gather(xs, name, in_hbm):
    return _exchange(_GatherRide(xs), name, in_hbm)


N_CHIPS = 4


def _swap_sibling(gs, name):
    na = len(gs)

    def body(*refs):
        g_refs, out_refs, send_sems, recv_sems = refs[:na], refs[na:2 * na], refs[2 * na], refs[2 * na + 1]
        mx_, my_, mc_ = lax.axis_index("x"), lax.axis_index("y"), lax.axis_index("c")
        copies = [pltpu.make_async_remote_copy(
            src_ref=g_refs[a].at[k, 1 - mc_], dst_ref=out_refs[a].at[k], send_sem=send_sems.at[N_CHIPS * a + k],
            recv_sem=recv_sems.at[N_CHIPS * a + k], device_id=(mx_, my_, 1 - mc_), device_id_type=MESH)
            for a in range(na) for k in range(N_CHIPS)]
        for cp in copies:
            cp.start()
        for cp in copies:
            cp.wait_recv()
        for cp in copies:
            cp.wait_send()

    return pl.pallas_call(
        body, out_shape=[jax.ShapeDtypeStruct((N_CHIPS,) + tuple(g.shape[2:]), g.dtype) for g in gs],
        in_specs=[pl.BlockSpec(memory_space=pl.ANY)] * na, out_specs=[pl.BlockSpec(memory_space=pl.ANY)] * na,
        scratch_shapes=[pltpu.SemaphoreType.DMA((N_CHIPS * na,)), pltpu.SemaphoreType.DMA((N_CHIPS * na,))], name=name,
    )(*gs)


def _row_tile(r, n):
    if r * n * 4 <= (1 << 20):
        return r
    for t in (1024, 512, 256, 128, 64, 32, 16, 8):
        if r % t == 0 and t * n * 4 <= (1 << 20):
            return t
    return r


def _pair_add(place, g, got, name):
    _, _, r, n = g.shape
    tr = r if r * n * 4 <= (3 << 19) else _row_tile(r, n)

    def body(place_ref, g_ref, got_ref, o_ref, ob_ref):
        s = g_ref[...] + got_ref[...]
        o_ref[...] = s
        ob_ref[...] = s.astype(ob_ref.dtype)

    blk = pl.BlockSpec((None, tr, n), lambda k, i, pr: (k, i, 0))
    grid_spec = pltpu.PrefetchScalarGridSpec(
        num_scalar_prefetch=1, grid=(N_CHIPS, r // tr),
        in_specs=[pl.BlockSpec((None, None, tr, n), lambda k, i, pr: (k, pr[0], i, 0)), blk], out_specs=[blk, blk])
    return pl.pallas_call(body, grid_spec=grid_spec,
                          out_shape=[jax.ShapeDtypeStruct((N_CHIPS, r, n), F32), jax.ShapeDtypeStruct((N_CHIPS, r, n), BF16)],
                          name=name, compiler_params=_cparams(2))(place, g, got)


class _ChipSwapRide:
    def __init__(self, ps):
        self.arrays = list(ps)
        self.na = len(ps)
        self.out_shapes = [jax.ShapeDtypeStruct((N_CHIP_PEERS,) + tuple(p.shape[1:]), p.dtype) for p in ps]
        self.sems = [pltpu.SemaphoreType.DMA((N_CHIP_PEERS * self.na,)), pltpu.SemaphoreType.DMA((N_CHIP_PEERS * self.na,))]

    def _copies(self, p_refs, out_refs, send_sems, recv_sems):
        mx_, my_, mc_ = lax.axis_index("x"), lax.axis_index("y"), lax.axis_index("c")
        chips = [(1 - mx_, my_), (mx_, 1 - my_), (1 - mx_, 1 - my_)]
        return [pltpu.make_async_remote_copy(
            src_ref=p_refs[a].at[2 * cx + cy], dst_ref=out_refs[a].at[j], send_sem=send_sems.at[N_CHIP_PEERS * a + j],
            recv_sem=recv_sems.at[N_CHIP_PEERS * a + j], device_id=(cx, cy, mc_), device_id_type=MESH)
            for a in range(self.na) for j, (cx, cy) in enumerate(chips)]

    def start(self, p_refs, out_refs, send_sems, recv_sems):
        for cp in self._copies(p_refs, out_refs, send_sems, recv_sems):
            cp.start()

    def finish(self, p_refs, out_refs, send_sems, recv_sems):
        copies = self._copies(p_refs, out_refs, send_sems, recv_sems)
        for cp in copies:
            cp.wait_recv()
        for cp in copies:
            cp.wait_send()


def _sum_lead(x, name):
    k, r, n = x.shape
    tr = _row_tile(r, n * k)

    def body(x_ref, o_ref):
        acc = x_ref[0]
        for i in range(1, k):
            acc = acc + x_ref[i]
        o_ref[...] = acc

    return pl.pallas_call(body, grid=(r // tr,), in_specs=[pl.BlockSpec((k, tr, n), lambda i: (0, i, 0))],
                          out_specs=pl.BlockSpec((tr, n), lambda i: (i, 0)),
                          out_shape=jax.ShapeDtypeStruct((r, n), x.dtype), name=name, compiler_params=_cparams(1))(x)


def _adamw(place, w, parts, m, v, name):
    shape = w.shape
    cols = shape[-1]
    rows = math.prod(shape[:-1])
    to2 = lambda t: t.reshape(rows, cols)
    tr = _row_tile(rows, cols) if rows * cols * 4 > (1 << 20) else rows
    npart = len(parts)
    spec = pl.BlockSpec((tr, cols), lambda i, pr: (i, 0))
    native = len(shape) == 3 and shape[0] == 1
    own = pl.BlockSpec((None, tr, cols), lambda i, pr: (0, i, 0)) if native else spec
    as_own = (lambda t: t) if native else to2
    part_specs, part_args = [], []
    for piece in parts:
        if isinstance(piece, tuple):
            stack, k, row0 = piece
            part_args.append(stack.reshape(stack.shape[0], -1, cols))
            assert row0 % tr == 0
            if k == "chip":
                part_specs.append(pl.BlockSpec((None, tr, cols), functools.partial(lambda i, pr, b0: (pr[1], i + b0, 0),
                                                                                   b0=row0 // tr)))
            else:
                part_specs.append(pl.BlockSpec((None, tr, cols), functools.partial(
                    lambda i, pr, kk, b0: (kk, i + b0, 0), kk=k, b0=row0 // tr)))
        else:
            part_args.append(to2(piece))
            part_specs.append(spec)

    def body(place_ref, *refs):
        w_ref, m_ref, v_ref = refs[0], refs[1 + npart], refs[2 + npart]
        g_ref, d_ref, nm_ref, nv_ref = refs[3 + npart:]
        g = refs[1][...].astype(F32)
        for q in range(1, npart):
            g = g + refs[1 + q][...].astype(F32)
        mm = ADAM_B1 * m_ref[...] + (1.0 - ADAM_B1) * g
        vv = ADAM_B2 * v_ref[...] + (1.0 - ADAM_B2) * jnp.square(g)
        m_hat = mm / (1.0 - ADAM_B1 ** ADAM_STEP)
        v_hat = vv / (1.0 - ADAM_B2 ** ADAM_STEP)
        g_ref[...] = g
        d_ref[...] = -ADAM_LR * (m_hat / (jnp.sqrt(v_hat) + ADAM_EPS) + ADAM_WD * w_ref[...])
        nm_ref[...] = mm
        nv_ref[...] = vv

    sh = jax.ShapeDtypeStruct(shape if native else (rows, cols), F32)
    grid_spec = pltpu.PrefetchScalarGridSpec(num_scalar_prefetch=1, grid=(rows // tr,),
                                             in_specs=[own] + part_specs + [own, own], out_specs=[own] * 4)
    outs = pl.pallas_call(body, grid_spec=grid_spec, out_shape=[sh] * 4, name=name, compiler_params=_cparams(1),
                          )(place, as_own(w), *part_args, as_own(m), as_own(v))
    return tuple(o.reshape(shape) for o in outs)


def _packed_rows(n, width):
    return -(-n // (8 * width)) * 8


def _pack_rows(items, width):
    rows = []
    for t in items:
        flat = t.reshape(-1)
        n = flat.shape[0]
        k = _packed_rows(n, width)
        if k * width > n:
            flat = jnp.concatenate([flat, jnp.zeros((k * width - n,), t.dtype)])
        rows.append(flat.reshape(k, width))
    return jnp.concatenate(rows, axis=0)


def _unpack_rows(packed, shapes, lead=()):
    width = packed.shape[-1]
    out, r = [], 0
    for sh in shapes:
        n = math.prod(sh)
        k = _packed_rows(n, width)
        piece = packed[..., r:r + k, :].reshape(tuple(lead) + (k * width,))[..., :n]
        out.append(piece.reshape(tuple(lead) + tuple(sh)))
        r += k
    return out


def _cols_full(t):
    return jnp.transpose(t, (1, 0, 2)).reshape(t.shape[1], -1)


def _cols_shards(t):
    d = t.shape[0]
    return jnp.transpose(t.reshape(d, N_DEV, -1), (1, 0, 2))


BIG = ("ffn1_gate", "ffn1_up", "ffn1_down", "w_in", "w_out", "ffn2_gate", "ffn2_up", "ffn2_down")
ROW_SHARDED = ("ffn1_down", "w_out", "ffn2_down")


def kernel(x, c, ctx, c_ctx, w_mod, b_mod, norm_ffn1, ffn1_gate, ffn1_up, ffn1_down, norm_mix, w_in, ssm_conv_w, ssm_conv_b, dt_bias_fwd, dt_bias_bwd, a_log_fwd, a_log_bwd, ssm_d, ssm_norm_w, cconv_w, cconv_b, cconv_ln_w, cconv_ln_b, w_out, norm_ffn2, ffn2_gate, ffn2_up, ffn2_down, final_norm, loss_target, m_c_ctx, m_w_mod, m_b_mod, m_norm_ffn1, m_ffn1_gate, m_ffn1_up, m_ffn1_down, m_norm_mix, m_w_in, m_ssm_conv_w, m_ssm_conv_b, m_dt_bias_fwd, m_dt_bias_bwd, m_a_log_fwd, m_a_log_bwd, m_ssm_d, m_ssm_norm_w, m_cconv_w, m_cconv_b, m_cconv_ln_w, m_cconv_ln_b, m_w_out, m_norm_ffn2, m_ffn2_gate, m_ffn2_up, m_ffn2_down, m_final_norm, v_c_ctx, v_w_mod, v_b_mod, v_norm_ffn1, v_ffn1_gate, v_ffn1_up, v_ffn1_down, v_norm_mix, v_w_in, v_ssm_conv_w, v_ssm_conv_b, v_dt_bias_fwd, v_dt_bias_bwd, v_a_log_fwd, v_a_log_bwd, v_ssm_d, v_ssm_norm_w, v_cconv_w, v_cconv_b, v_cconv_ln_w, v_cconv_ln_b, v_w_out, v_norm_ffn2, v_ffn2_gate, v_ffn2_up, v_ffn2_down, v_final_norm):
    args = dict(locals())
    names = ("c_ctx", "w_mod", "b_mod", "norm_ffn1", "ffn1_gate", "ffn1_up", "ffn1_down", "norm_mix", "w_in",
             "ssm_conv_w", "ssm_conv_b", "dt_bias_fwd", "dt_bias_bwd", "a_log_fwd", "a_log_bwd", "ssm_d", "ssm_norm_w",
             "cconv_w", "cconv_b", "cconv_ln_w", "cconv_ln_b", "w_out", "norm_ffn2", "ffn2_gate", "ffn2_up",
             "ffn2_down", "final_norm")
    wts = {n: args[n] for n in names}
    bl, seq, d = x.shape
    clen = ctx.shape[1]
    heads = dt_bias_fwd.shape[1]
    ft = _gate_tile(ffn1_gate.shape[2] * N_DEV)
    lay = _Lay(bl, seq, clen, d)
    mx = _Mix(d, heads)
    nb = bl * N_DEV
    me = 4 * lax.axis_index("x") + 2 * lax.axis_index("y") + lax.axis_index("c")
    mcols = w_mod.shape[2]
    n_ctx_mod = 5 * d

    place = jnp.stack([lax.axis_index("c"), 2 * lax.axis_index("x") + lax.axis_index("y")]).astype(jnp.int32)

    small_shapes = [(bl, d), ssm_conv_w.shape[1:], cconv_w.shape[1:]]
    (g1,) = _all_gather([_pack_rows([c, ssm_conv_w, cconv_w], d)], "gather_small", False)
    c_g, conv_g, cconv_g = _unpack_rows(g1, small_shapes, (N_DEV,))
    c_all = c_g.reshape(nb, d)
    conv_w_full = jnp.transpose(conv_g, (1, 0, 2)).reshape(conv_g.shape[1], -1)
    cconv_w_full = jnp.transpose(cconv_g, (1, 0, 2)).reshape(cconv_g.shape[1], -1)

    s_all = jnp.concatenate([_silu(c_all), _silu(c_ctx)[None, :], jnp.zeros((7, d), F32)], axis=0)
    mod_cols = _mm(s_all, w_mod[0], "nn", F32, "mod_cols")
    (g2,) = _all_gather([mod_cols], "gather_mod", False)
    mod_all = _cols_full(g2) + b_mod
    mod_mine = jnp.concatenate([lax.dynamic_slice_in_dim(mod_all, me * bl, bl, axis=0), mod_all[nb:nb + 1]], axis=0)
    modv = mod_mine.reshape(bl + 1, N_MOD, d)

    hh = 2 * heads
    shard16 = lambda n: wts[n][0].astype(BF16)

    nl = ffn1_gate.shape[2]
    spt = ft // nl
    assert ft % nl == 0 and N_DEV % spt == 0

    def ffn_weights(gate, up, down):
        both = jnp.stack([gate, up], axis=1).reshape(N_DEV // spt, spt, 2, d, nl)
        return jnp.transpose(both, (3, 0, 2, 1, 4)).reshape(d, -1), down.reshape(-1, d), ft

    def grads_by_dest(name, grad):
        if name == "w_in":
            grad = _cols_shards(jnp.concatenate([grad[:, :d], grad[:, mx.off_x:mx.off_glu],
                                                 grad[:, mx.off_dt:mx.off_dt + hh], grad[:, mx.off_glu:]], axis=1))
        elif name.endswith("_gu"):
            grad = jnp.transpose(grad.reshape(d, N_DEV // spt, 2, spt, nl), (1, 3, 2, 0, 4)).reshape(N_DEV, 2 * d, nl)
        else:
            grad = grad.reshape((N_DEV,) + tuple(wts[name].shape[1:]))
        return grad.reshape((N_CHIPS, 2) + tuple(grad.shape[1:]))

    def ffn_grads(tag, pair):
        return {tag + "_gu": pair[0], tag + "_down": pair[1]}

    def to_chip_sums(named):
        names_ = list(named)
        by_dest = [grads_by_dest(n, named[n]) for n in names_]
        got = _swap_sibling(by_dest, "rs_sibling_" + names_[0])
        return {n: _pair_add(place, t, s, "rs_pair_add_" + n) for n, t, s in zip(names_, by_dest, got)}

    class _Overlap:
        late_names = ("w_in", "w_out", "ffn2_gate", "ffn2_up", "ffn2_down")
        late_weights = _GatherRide([shard16(n) for n in late_names])
        sums, arrived = {}, {}

        def unpack_late(self, outs):
            full = dict(zip(self.late_names, outs))
            w_in_f = _cols_full(full["w_in"])
            w_in_p = jnp.concatenate([w_in_f[:, :d], w_in_f[:, mx.ref_dt:mx.ref_glu], jnp.zeros((d, DT_PAD - hh), BF16),
                                      w_in_f[:, mx.ref_x:mx.ref_dt], w_in_f[:, mx.ref_glu:]], axis=1)
            return {"w_in": w_in_p, "w_out": full["w_out"].reshape(-1, d),
                    "ffn2": ffn_weights(full["ffn2_gate"], full["ffn2_up"], full["ffn2_down"])}

        def early_grads(self, g):
            self.sums = to_chip_sums({**ffn_grads("ffn2", g["ffn2"]), "w_in": g["w_in"], "w_out": g["w_out"]})
            self.groups = (("ffn2_gu", "ffn2_down"), ("w_in", "w_out"))
            return tuple(_ChipSwapRide([self.sums[n][1] for n in grp]) for grp in self.groups)

        def take_early(self, rode):
            for grp, outs in zip(self.groups, rode):
                self.arrived.update(zip(grp, outs))

    exch = _Overlap()
    full = dict(zip(BIG[:3], _all_gather([shard16(n) for n in BIG[:3]], "gather_weights", True)))
    lanes_pad = lambda a, b: jnp.concatenate([a, b, jnp.zeros((1, DT_LANES - hh), F32)], axis=1)
    a_vals = lanes_pad(-jnp.exp(a_log_fwd), -jnp.exp(a_log_bwd))
    w = {
        "norm_ffn1": norm_ffn1, "norm_mix": norm_mix, "norm_ffn2": norm_ffn2, "final_norm": final_norm[None, :],
        "ffn1": ffn_weights(full["ffn1_gate"], full["ffn1_up"], full["ffn1_down"]),
        "conv_w": jnp.concatenate([conv_w_full, jnp.zeros((3, mx.xw), F32)], axis=0), "conv_b": ssm_conv_b,
        "dt_bias": lanes_pad(dt_bias_fwd, dt_bias_bwd), "a_row": a_vals, "a_col": a_vals.T,
        "d_row": jnp.repeat(ssm_d, HEAD_DIM, axis=1), "ssm_norm_w": ssm_norm_w,
        "cconv_w": cconv_w_full, "cconv_b": cconv_b, "ln_w": cconv_ln_w, "ln_b": cconv_ln_b,
    }

    xa = jnp.concatenate([x.reshape(bl * seq, d), ctx.reshape(bl * clen, d)], axis=0)
    loss, grad_x, g, dmodv = _local_step(lay, mx, xa, loss_target.reshape(bl * seq, d), modv, w, exch)
    loss = lax.psum(loss[0, 0], ("x", "y", "c"))

    sums = {**exch.sums, **to_chip_sums(ffn_grads("ffn1", g["ffn1"]))}
    arrived = dict(exch.arrived)
    last = ("ffn1_gu", "ffn1_down")
    arrived.update(zip(last, _exchange(_ChipSwapRide([sums[n][1] for n in last]), "rs_chips")))

    def big_parts(n):
        key, row0 = (n[:4] + "_gu", d if n.endswith("_up") else 0) if n.endswith(("_gate", "_up")) else (n, 0)
        return [(sums[key][0], "chip", row0)] + [(arrived[key], k, row0) for k in range(N_CHIP_PEERS)]

    n9 = N_MOD * d
    dmod_rows = dmodv.reshape(bl + 1, n9)
    ctx_row = jnp.concatenate([dmod_rows[bl, :n_ctx_mod], jnp.zeros((n9 - n_ctx_mod,), F32)])
    summed = [ctx_row, g["norm_ffn1"], g["norm_mix"], g["norm_ffn2"], g["final_norm"], g["conv_b"], g["dt_bias"],
              g["a_row"], g["d_row"], g["ssm_norm_w"], g["cconv_b"], g["ln_w"], g["ln_b"], g["conv_w"][:5], g["cconv_w"]]
    sum_shapes = [t.shape for t in summed]
    (g4,) = _all_gather([_pack_rows([dmod_rows[:bl]] + summed, d)], "gather_small_grads", False)
    dmod_batch = g4[:, :bl * N_MOD].reshape(nb, n9)
    tot = _sum_lead(g4[:, _packed_rows(bl * n9, d):], "sum_small_grads")
    (dctx, g_n1, g_nm, g_n2, g_fn, g_cb, g_dtb, g_a, g_drow, g_snw, g_ccb, g_lnw, g_lnb, g_cw, g_ccw) = _unpack_rows(tot, sum_shapes)
    dmod_all = jnp.concatenate([dmod_batch, dctx[None, :], jnp.zeros((7, n9), F32)], axis=0)

    dmod_my_cols = lax.dynamic_slice_in_dim(dmod_all, me * mcols, mcols, axis=1)
    g_w_mod = _mm(s_all, dmod_my_cols, "tn", F32, "dw_mod")[None]
    g_b_mod = _sum_lead(dmod_all.reshape(nb + 8, N_MOD, d), "db_mod").reshape(1, n9)
    ds_part = _mm(dmod_my_cols[nb:nb + 8], w_mod[0], "nt", F32, "ds_ctx")
    (g5,) = _all_gather([jnp.concatenate([ds_part[0:1], jnp.zeros((7, d), F32)], axis=0)], "gather_ds_ctx", False)
    ds_ctx = _sum_lead(g5, "sum_ds_ctx")[0]
    sg = jax.nn.sigmoid(c_ctx)
    g_c_ctx = ds_ctx * (sg * (1.0 + c_ctx * (1.0 - sg)))

    a_f, a_b = a_vals[:, :heads], a_vals[:, heads:hh]
    grads = {
        "c_ctx": [g_c_ctx], "w_mod": [g_w_mod], "b_mod": [g_b_mod],
        "norm_ffn1": [g_n1], "norm_mix": [g_nm], "norm_ffn2": [g_n2], "final_norm": [g_fn.reshape(-1)],
        "ssm_conv_w": [lax.dynamic_slice_in_dim(g_cw, me * ssm_conv_w.shape[2], ssm_conv_w.shape[2], axis=1)[None]],
        "ssm_conv_b": [g_cb],
        "dt_bias_fwd": [g_dtb[:, :heads]], "dt_bias_bwd": [g_dtb[:, heads:hh]],
        "a_log_fwd": [g_a[:, :heads] * a_f], "a_log_bwd": [g_a[:, heads:hh] * a_b],
        "ssm_d": [jnp.sum(g_drow.reshape(1, heads, HEAD_DIM), axis=2)], "ssm_norm_w": [g_snw],
        "cconv_w": [lax.dynamic_slice_in_dim(g_ccw, me * cconv_w.shape[2], cconv_w.shape[2], axis=1)[None]],
        "cconv_b": [g_ccb], "cconv_ln_w": [g_lnw], "cconv_ln_b": [g_lnb],
    }
    for n in BIG:
        grads[n] = big_parts(n)

    out_g, out_d, out_m, out_v = [], [], [], []
    for n in names:
        gr, de, nm, nv = _adamw(place, wts[n], grads[n], args["m_" + n], args["v_" + n], "adamw_" + n)
        out_g.append(gr)
        out_d.append(de)
        out_m.append(nm)
        out_v.append(nv)
    return (loss, grad_x.reshape(bl, seq, d), *out_g, *out_d, *out_m, *out_v)
```

```python
import functools
import math

import jax
import jax.numpy as jnp
from jax import lax
from jax.experimental import pallas as pl
from jax.experimental.pallas import tpu as pltpu

F32 = jnp.float32
BF16 = jnp.bfloat16
MESH = pl.DeviceIdType.MESH

N_DEV = 8
N_CHIP_PEERS = 3
HEAD_DIM = 64
N_STATE = 128
SSD_GROUPS = 2
CHUNK = 128
GRID_W = 64
N_MOD = 9
EPS = 1e-6
DT_PAD = 512
DT_LANES = 128
HALO = 8
ROW_TILE = 512
FINE_ROW_TILE = 256
VMEM_LIMIT = 48 * 1024 * 1024
NEG_BIG = -1e30

ADAM_LR = 0.001
ADAM_B1 = 0.9
ADAM_B2 = 0.999
ADAM_EPS = 1e-08
ADAM_WD = 0.01
ADAM_STEP = 10


def _pick(n, prefs):
    for p in prefs:
        if n % p == 0:
            return p
    return n


MM_TILE_CAP = 2816
MM_TILE_ELEMS = 3 << 20
MM_OUT_TILE_ELEMS = 3 << 19
MM_FULL_ROWS = 1024


def _big_tile(n, cap):
    if n <= cap:
        return n
    best = 0
    for t in range(128, cap + 1, 128):
        if n % t == 0:
            best = t
    return best or n


def _cparams(ndim):
    return pltpu.CompilerParams(dimension_semantics=("arbitrary",) * ndim, vmem_limit_bytes=VMEM_LIMIT)


def _silu(v):
    return v * jax.nn.sigmoid(v)


def _mm(a, b, mode, out_dtype, name, tn=None, tm=None, extras=(), epilogue=None, outs=None, ride=None):
    if mode == "tn":
        (K, M), (K2, N) = a.shape, b.shape
    elif mode == "nt":
        (M, K), (N, K2) = a.shape, b.shape
    else:
        (M, K), (K2, N) = a.shape, b.shape
    assert K == K2, (name, a.shape, b.shape)
    tm = tm or (M if M <= MM_FULL_ROWS else None)
    if tn is None:
        tn = _big_tile(N, min(MM_TILE_CAP, max(128, MM_OUT_TILE_ELEMS // (tm or 512))))
    if tm is None:
        tm = _big_tile(M, max(128, MM_OUT_TILE_ELEMS // tn))
    tk = _big_tile(K, min(MM_TILE_CAP, MM_TILE_ELEMS // max(tn, tm)))
    nk = K // tk
    ni, nj = M // tm, N // tn
    swap = nk == 1 and (K * N + M * K * nj) < (M * K + K * N * ni)
    ij = (lambda g0, g1: (g1, g0)) if swap else (lambda g0, g1: (g0, g1))
    if mode == "tn":
        a_spec = pl.BlockSpec((tk, tm), lambda g0, g1, k: (k, ij(g0, g1)[0]))
        dn = (((0,), (0,)), ((), ()))
    else:
        a_spec = pl.BlockSpec((tm, tk), lambda g0, g1, k: (ij(g0, g1)[0], k))
        dn = (((1,), (1,)), ((), ())) if mode == "nt" else (((1,), (0,)), ((), ()))
    if mode == "nt":
        b_spec = pl.BlockSpec((tn, tk), lambda g0, g1, k: (ij(g0, g1)[1], k))
    else:
        b_spec = pl.BlockSpec((tk, tn), lambda g0, g1, k: (k, ij(g0, g1)[1]))
    if outs is None:
        outs = [(tn, out_dtype)]
    nx = len(extras)

    def tile(w):
        return pl.BlockSpec((tm, w), lambda g0, g1, k: ij(g0, g1))

    def extra_spec(item):
        if len(item) == 3:
            return pl.BlockSpec((None,) + tuple(item[0].shape[1:]), lambda g0, g1, k: (item[1](ij(g0, g1)[0]), 0, 0))
        return tile(item[1])

    def finish(acc, refs):
        vals = (acc,) if epilogue is None else epilogue(acc, *[r[...] for r in refs[:nx]])
        for o_ref, v in zip(refs[nx:], vals):
            o_ref[...] = v.astype(o_ref.dtype)

    grid = (nj, ni, nk) if swap else (ni, nj, nk)
    nout = len(outs)
    r_in = len(ride.arrays) if ride else 0
    r_out = len(ride.out_shapes) if ride else 0

    def compute(a_ref, b_ref, refs):
        part = lax.dot_general(a_ref[...].astype(BF16), b_ref[...].astype(BF16), dn, preferred_element_type=F32)
        if nk == 1:
            finish(part, refs)
            return
        acc_ref, k = refs[-1], pl.program_id(2)
        _acc(acc_ref, k == 0, part)

        @pl.when(k == nk - 1)
        def _():
            finish(acc_ref[...], refs[:-1])

    def body(a_ref, b_ref, *refs):
        if ride is None:
            compute(a_ref, b_ref, refs)
            return
        x_refs, rin = refs[:nx], refs[nx:nx + r_in]
        o_refs, rout = refs[nx + r_in:nx + r_in + nout], refs[nx + r_in + nout:nx + r_in + nout + r_out]
        tail = refs[nx + r_in + nout + r_out:]
        nacc = 1 if nk > 1 else 0
        sems = tail[nacc:]
        ids = [pl.program_id(q) for q in range(3)]
        first = functools.reduce(jnp.logical_and, [i == 0 for i in ids])
        last = functools.reduce(jnp.logical_and, [i == n - 1 for i, n in zip(ids, grid)])
        pl.when(first)(lambda: ride.start(rin, rout, *sems))
        compute(a_ref, b_ref, tuple(x_refs) + tuple(o_refs) + tuple(tail[:nacc]))
        pl.when(last)(lambda: ride.finish(rin, rout, *sems))

    hbm = pl.BlockSpec(memory_space=pl.ANY)
    res = pl.pallas_call(
        body, grid=grid, in_specs=[a_spec, b_spec] + [extra_spec(x) for x in extras] + [hbm] * r_in,
        out_specs=[tile(w) for w, _ in outs] + [hbm] * r_out,
        out_shape=[jax.ShapeDtypeStruct((M, nj * w), dt) for w, dt in outs] + (list(ride.out_shapes) if ride else []),
        scratch_shapes=([pltpu.VMEM((tm, tn), F32)] if nk > 1 else []) + (list(ride.sems) if ride else []),
        name=name, compiler_params=_cparams(3),
    )(a, b, *[x[0] for x in extras], *(ride.arrays if ride else []))
    main = res[0] if epilogue is None else res[:nout]
    return (main, res[nout:]) if ride else main


class _Lay:
    def __init__(self, bl, seq, clen, d, tt=None):
        self.bl, self.seq, self.clen, self.d = bl, seq, clen, d
        self.tt = min(ROW_TILE, math.gcd(seq, bl * clen)) if tt is None else tt
        assert seq % self.tt == 0 and (bl * clen) % self.tt == 0 and self.tt % 8 == 0
        self.spb = seq // self.tt
        self.spc = clen // self.tt
        self.nsx = bl * self.spb
        self.nsc = bl * clen // self.tt
        self.ns = self.nsx + self.nsc
        self.tx = bl * seq
        self.ta = self.tx + bl * clen

    def fine(self):
        return _Lay(self.bl, self.seq, self.clen, self.d, min(FINE_ROW_TILE, self.clen))

    def mrow(self, s):
        return jnp.where(s < self.nsx, s // self.spb, self.bl)

    def first_of_row(self, s):
        return jnp.logical_or(jnp.logical_and(s < self.nsx, s % self.spb == 0), s == self.nsx)

    def seq_first(self, s):
        return jnp.where(s < self.nsx, s % self.spb == 0, (s - self.nsx) % self.spc == 0)

    def seq_last(self, s):
        return jnp.where(s < self.nsx, s % self.spb == self.spb - 1, (s - self.nsx) % self.spc == self.spc - 1)


def _tok(lay, c, cb=0, clamp=None):
    if clamp is None:
        return pl.BlockSpec((lay.tt, c), lambda j, s: (s, cb + j))
    return pl.BlockSpec((lay.tt, c), lambda j, s: (jnp.minimum(s, clamp), cb + j))


def _halo_prev(lay, c, cb=0):
    u = lay.tt // HALO
    return pl.BlockSpec((HALO, c), lambda j, s: (jnp.maximum(s * u - 1, 0), cb + j))


def _halo_next(lay, c, cb=0):
    u = lay.tt // HALO
    last = lay.ta // HALO - 1
    return pl.BlockSpec((HALO, c), lambda j, s: (jnp.minimum((s + 1) * u, last), cb + j))


def _row(lay, k, c):
    return pl.BlockSpec((None, k, c), lambda j, s: (lay.mrow(s), 0, 0))


def _glob(k, c, cb=None):
    if cb is None:
        return pl.BlockSpec((k, c), lambda j, s: (0, 0))
    return pl.BlockSpec((k, c), lambda j, s: (0, cb + j))


def _tok_call(name, body, ncb, nseg, in_specs, out_specs, out_shape, inputs, scratch=(), aliases=None):
    return pl.pallas_call(body, grid=(ncb, nseg), in_specs=in_specs, out_specs=out_specs, out_shape=out_shape,
                          scratch_shapes=list(scratch), name=name, compiler_params=_cparams(2),
                          input_output_aliases=aliases or {})(*inputs)


def _acc(ref, first, val):
    @pl.when(first)
    def _():
        ref[...] = val

    @pl.when(jnp.logical_not(first))
    def _():
        ref[...] += val


def _norm_mod_f(x, w, sh, sc):
    y = x * lax.rsqrt(jnp.mean(x * x, axis=-1, keepdims=True) + EPS) * w
    return y * (1.0 + sc) + sh


def _norm_mod_fwd(lay, nseg, x, w, modv, ksh, name):
    d = lay.d

    def body(x_ref, w_ref, m_ref, h_ref):
        h = _norm_mod_f(x_ref[...], w_ref[...], m_ref[ksh:ksh + 1, :], m_ref[ksh + 1:ksh + 2, :])
        h_ref[...] = h.astype(h_ref.dtype)

    return _tok_call(name, body, 1, nseg, [_tok(lay, d), _glob(1, d), _row(lay, N_MOD, d)], _tok(lay, d),
                     jax.ShapeDtypeStruct((nseg * lay.tt, d), BF16), (x, w, modv))


def _norm_mod_bwd(lay, nseg, nres, x, w, modv, ksh, dh, dres, name, nout=None):
    d = lay.d
    nrow = lay.bl + (1 if nseg > lay.nsx else 0)
    nout = nseg if nout is None else nout

    def body(x_ref, w_ref, m_ref, dh_ref, dres_ref, dx_ref, dw_ref, dm_ref):
        s = pl.program_id(1)
        _, vjp = jax.vjp(_norm_mod_f, x_ref[...], w_ref[...], m_ref[ksh:ksh + 1, :], m_ref[ksh + 1:ksh + 2, :])
        dx, dw, dsh, dsc = vjp(dh_ref[...])

        @pl.when(s < nout)
        def _():
            dx_ref[...] = dx + jnp.where(s < nres, dres_ref[...], 0.0)

        _acc(dw_ref, s == 0, dw)
        _acc(dm_ref, lay.first_of_row(s), jnp.concatenate([dsh, dsc], axis=0))

    return _tok_call(
        name, body, 1, nseg,
        [_tok(lay, d), _glob(1, d), _row(lay, N_MOD, d), _tok(lay, d), _tok(lay, d, clamp=nres - 1)],
        [_tok(lay, d, clamp=nout - 1), _glob(1, d), _row(lay, 2, d)],
        [jax.ShapeDtypeStruct((nout * lay.tt, d), F32), jax.ShapeDtypeStruct((1, d), F32),
         jax.ShapeDtypeStruct((nrow, 2, d), F32)],
        (x, w, modv, dh, dres))


def _mm_resid(lay, a, b, x, modv, kg, coef, name):
    d = lay.d
    tm = min(ROW_TILE, math.gcd(lay.seq, lay.bl * lay.clen))
    assert a.shape[0] % tm == 0 and b.shape[1] == d

    def row_of(i):
        return jnp.where(i * tm < lay.tx, (i * tm) // lay.seq, lay.bl)

    def add(acc, x_tile, m_blk):
        return acc, x_tile + (coef * m_blk[kg:kg + 1, :]) * acc

    return _mm(a, b, "nn", None, name, tm=tm, tn=d, extras=[(x, d), (modv, row_of, "rows")], epilogue=add,
               outs=[(d, F32), (d, F32)])


def _resid_bwd(lay, nseg, dy, o, modv, kg, coef, name):
    d = lay.d
    nrow = lay.bl + (1 if nseg > lay.nsx else 0)

    def body(dy_ref, o_ref, m_ref, do_ref, dg_ref):
        s = pl.program_id(1)
        dy = dy_ref[...]
        do_ref[...] = (dy * (coef * m_ref[kg:kg + 1, :])).astype(do_ref.dtype)
        _acc(dg_ref, lay.first_of_row(s), jnp.sum(dy * o_ref[...], axis=0, keepdims=True) * coef)

    return _tok_call(name, body, 1, nseg, [_tok(lay, d), _tok(lay, d), _row(lay, N_MOD, d)],
                     [_tok(lay, d), _row(lay, 1, d)],
                     [jax.ShapeDtypeStruct((nseg * lay.tt, d), BF16), jax.ShapeDtypeStruct((nrow, 1, d), F32)],
                     (dy, o, modv))


def _final_loss(lay, x, wf, target, name):
    d = lay.d

    def body(x_ref, w_ref, t_ref, loss_ref, dx_ref, dw_ref):
        s = pl.program_id(1)

        def f(xv, wv):
            return xv * lax.rsqrt(jnp.mean(xv * xv, axis=-1, keepdims=True) + EPS) * wv

        y, vjp = jax.vjp(f, x_ref[...], w_ref[...])
        err = y - t_ref[...]
        part = 0.5 * jnp.sum(jnp.sum(err * err, axis=-1, keepdims=True), axis=0, keepdims=True) / d
        dx, dw = vjp(err / d)
        dx_ref[...] = dx
        _acc(loss_ref, s == 0, part)
        _acc(dw_ref, s == 0, dw)

    return _tok_call(name, body, 1, lay.nsx, [_tok(lay, d), _glob(1, d), _tok(lay, d)],
                     [_glob(1, 1), _tok(lay, d), _glob(1, d)],
                     [jax.ShapeDtypeStruct((1, 1), F32), jax.ShapeDtypeStruct((lay.tx, d), F32),
                      jax.ShapeDtypeStruct((1, d), F32)], (x, wf, target))


class _Mix:
    def __init__(self, d, heads):
        self.d_ssm = d
        self.d_conv = d
        self.heads = heads
        assert heads * HEAD_DIM == d and heads % (2 * SSD_GROUPS) == 0 and 2 * heads <= DT_LANES
        self.gn = SSD_GROUPS * N_STATE
        self.xw = d + 2 * self.gn
        self.off_dt = d
        self.off_x = d + DT_PAD
        self.off_glu = self.off_x + self.xw
        self.pw = self.off_glu + 2 * d
        self.ref_x = d
        self.ref_dt = d + self.xw
        self.ref_glu = self.ref_dt + 2 * heads
        self.cc = self.xw if self.off_x % self.xw == 0 else _pick(self.xw, (512, 256, 128))


def _conv5_fwd(lay, mx, proj, cw, cb, name):
    c, tt = mx.cc, lay.tt
    cb0 = mx.off_x // c
    assert mx.off_x % c == 0

    def body(prev_ref, cur_ref, next_ref, w_ref, b_ref, pre_ref, act_ref, ext_ref):
        s = pl.program_id(1)
        ext_ref[0:HALO, :] = jnp.where(lay.seq_first(s), 0.0, prev_ref[...])
        ext_ref[HALO:HALO + tt, :] = cur_ref[...]
        ext_ref[HALO + tt:, :] = jnp.where(lay.seq_last(s), 0.0, next_ref[...])
        acc = jnp.zeros((tt, c), F32) + b_ref[...]
        for k in range(5):
            acc = acc + w_ref[k:k + 1, :] * ext_ref[pl.ds(HALO + k - 2, tt), :]
        pre_ref[...] = acc
        act_ref[...] = _silu(acc)

    sh = jax.ShapeDtypeStruct((lay.ta, mx.xw), F32)
    return _tok_call(name, body, mx.xw // c, lay.ns,
                     [_halo_prev(lay, c, cb0), _tok(lay, c, cb0), _halo_next(lay, c, cb0), _glob(8, c, 0), _glob(1, c, 0)],
                     [_tok(lay, c), _tok(lay, c)], [sh, sh], (proj, proj, proj, cw, cb),
                     scratch=[pltpu.VMEM((tt + 2 * HALO, c), F32)])


def _conv5_bwd(lay, mx, proj, pre, dact_f, dact_b, cw, dproj, name):
    c, tt = mx.cc, lay.tt
    cb0 = mx.off_x // c

    def dsilu(p):
        sg = jax.nn.sigmoid(p)
        return sg * (1.0 + p * (1.0 - sg))

    def body(xp_ref, xc_ref, xn_ref, pp_ref, pc_ref, pn_ref, fp_ref, fc_ref, fn_ref, bp_ref, bc_ref, bn_ref, w_ref,
             buf_ref, dx_ref, dw_ref, db_ref, extx_ref, extd_ref):
        s = pl.program_id(1)
        first, last = lay.seq_first(s), lay.seq_last(s)
        dcur = (fc_ref[...] + bc_ref[...]) * dsilu(pc_ref[...])
        extd_ref[0:HALO, :] = jnp.where(first, 0.0, (fp_ref[...] + bp_ref[...]) * dsilu(pp_ref[...]))
        extd_ref[HALO:HALO + tt, :] = dcur
        extd_ref[HALO + tt:, :] = jnp.where(last, 0.0, (fn_ref[...] + bn_ref[...]) * dsilu(pn_ref[...]))
        extx_ref[0:HALO, :] = jnp.where(first, 0.0, xp_ref[...])
        extx_ref[HALO:HALO + tt, :] = xc_ref[...]
        extx_ref[HALO + tt:, :] = jnp.where(last, 0.0, xn_ref[...])
        dx = jnp.zeros((tt, c), F32)
        rows = []
        for k in range(5):
            dx = dx + w_ref[k:k + 1, :] * extd_ref[pl.ds(HALO - (k - 2), tt), :]
            rows.append(jnp.sum(dcur * extx_ref[pl.ds(HALO + k - 2, tt), :], axis=0, keepdims=True))
        dx_ref[...] = dx.astype(dx_ref.dtype)
        rows.append(jnp.zeros((3, c), F32))
        _acc(dw_ref, s == 0, jnp.concatenate(rows, axis=0))
        _acc(db_ref, s == 0, jnp.sum(dcur, axis=0, keepdims=True))

    three = lambda cbx: [_halo_prev(lay, c, cbx), _tok(lay, c, cbx), _halo_next(lay, c, cbx)]
    ext = pltpu.VMEM((tt + 2 * HALO, c), F32)
    return _tok_call(name, body, mx.xw // c, lay.ns,
                     three(cb0) + three(0) + three(0) + three(0) + [_glob(8, c, 0), pl.BlockSpec(memory_space=pl.ANY)],
                     [_tok(lay, c, cb0), _glob(8, c, 0), _glob(1, c, 0)],
                     [jax.ShapeDtypeStruct(dproj.shape, dproj.dtype), jax.ShapeDtypeStruct((8, mx.xw), F32),
                      jax.ShapeDtypeStruct((1, mx.xw), F32)],
                     (proj, proj, proj, pre, pre, pre, dact_f, dact_f, dact_f, dact_b, dact_b, dact_b, cw, dproj),
                     scratch=[ext, ext], aliases={13: 0})


def _softplus(v):
    return jnp.maximum(v, 0.0) + jnp.log1p(jnp.exp(-jnp.abs(v)))


def _dt_fwd(lay, mx, proj, bias, name):
    cb = mx.off_dt // DT_LANES

    def body(p_ref, b_ref, dt_ref):
        dt_ref[...] = _softplus(p_ref[...] + b_ref[...])

    return _tok_call(name, body, 1, lay.ns, [_tok(lay, DT_LANES, cb), _glob(1, DT_LANES)], _tok(lay, DT_LANES),
                     jax.ShapeDtypeStruct((lay.ta, DT_LANES), F32), (proj, bias))


def _dt_bwd(lay, mx, proj, bias, parts, dproj, name):
    cb = mx.off_dt // DT_LANES
    ncb = DT_PAD // DT_LANES

    def body(p_ref, b_ref, a_ref, b2_ref, c_ref, d_ref, buf_ref, dp_ref, db_ref):
        j, s = pl.program_id(0), pl.program_id(1)

        @pl.when(j == 0)
        def _():
            ddt = (a_ref[...] + b2_ref[...]) + (c_ref[...] + d_ref[...])
            draw = ddt * jax.nn.sigmoid(p_ref[...] + b_ref[...])
            dp_ref[...] = draw.astype(dp_ref.dtype)
            _acc(db_ref, s == 0, jnp.sum(draw, axis=0, keepdims=True))

        @pl.when(j > 0)
        def _():
            dp_ref[...] = jnp.zeros_like(dp_ref)

    t = pl.BlockSpec((lay.tt, DT_LANES), lambda j, s: (s, 0))
    return _tok_call(name, body, ncb, lay.ns,
                     [pl.BlockSpec((lay.tt, DT_LANES), lambda j, s: (s, cb)), _glob(1, DT_LANES), t, t, t, t,
                      pl.BlockSpec(memory_space=pl.ANY)],
                     [_tok(lay, DT_LANES, cb), _glob(1, DT_LANES)],
                     [jax.ShapeDtypeStruct(dproj.shape, dproj.dtype), jax.ShapeDtypeStruct((1, DT_LANES), F32)],
                     (proj, bias) + tuple(parts) + (dproj,), aliases={6: 0})


def _scan_mask(rev):
    r = lax.broadcasted_iota(jnp.int32, (CHUNK, CHUNK), 0)
    c = lax.broadcasted_iota(jnp.int32, (CHUNK, CHUNK), 1)
    return (r <= c) if rev else (r >= c)


def _split_bf16(x):
    hi = x.astype(BF16)
    return hi, (x - hi.astype(F32)).astype(BF16)


@functools.partial(jax.custom_vjp, nondiff_argnums=(0,))
def _cum_cols(rev, x):
    m = _scan_mask(rev).astype(BF16)
    hi, lo = _split_bf16(x)
    return jnp.dot(m, hi, preferred_element_type=F32) + jnp.dot(m, lo, preferred_element_type=F32)


_cum_cols.defvjp(lambda rev, x: (_cum_cols(rev, x), None), lambda rev, _, g: (_cum_cols(not rev, g),))


@functools.partial(jax.custom_vjp, nondiff_argnums=(0,))
def _cum_rows(rev, x):
    m = _scan_mask(not rev).astype(BF16)
    hi, lo = _split_bf16(x)
    return jnp.dot(hi, m, preferred_element_type=F32) + jnp.dot(lo, m, preferred_element_type=F32)


_cum_rows.defvjp(lambda rev, x: (_cum_rows(rev, x), None), lambda rev, _, g: (_cum_rows(not rev, g),))


def _ssd_chunk(xh_pairs, bcs, ccs, dtc, dtr, a_row, a_col, st_pairs, *, rev, heads, col0):
    cs_c, cs_r, tot, scores = _ssd_shared(bcs, ccs, dtc, dtr, a_row, a_col, rev=rev)
    ppg = heads // (2 * SSD_GROUPS)
    ys, sts = [], []
    for g in range(SSD_GROUPS):
        y, st = _ssd_group(xh_pairs[g * ppg:(g + 1) * ppg], bcs[g], ccs[g], st_pairs[g], cs_c, cs_r, tot, dtc,
                           scores[g], rev=rev, col=col0 + 2 * ppg * g)
        ys.append(y)
        sts.append(st)
    return ys, sts


_NT = (((1,), (1,)), ((), ()))
_TN = (((0,), (0,)), ((), ()))


def _ssd_shared(bcs, ccs, dtc, dtr, a_row, a_col, *, rev):
    da_c = dtc * a_row
    cs_c = _cum_cols(rev, da_c)
    cs_r = _cum_rows(rev, dtr * a_col)
    tot = jnp.sum(da_c, axis=0, keepdims=True)
    scores = [lax.dot_general(ccs[g].astype(BF16), bcs[g].astype(BF16), _NT, preferred_element_type=F32)
              for g in range(SSD_GROUPS)]
    return cs_c, cs_r, tot, scores


def _ssd_group(xh_pairs, bc, cc, st, cs_c, cs_r, tot, dtc, score, *, rev, col):
    n = CHUNK
    mask = _scan_mask(rev)
    lane = lax.broadcasted_iota(jnp.int32, (n, DT_LANES), 1)
    sub = lax.broadcasted_iota(jnp.int32, (DT_LANES, n), 0)
    lane1 = lax.broadcasted_iota(jnp.int32, (1, DT_LANES), 1)
    left = lax.broadcasted_iota(jnp.int32, (n, 2 * HEAD_DIM), 1) < HEAD_DIM
    top = lax.broadcasted_iota(jnp.int32, (2 * HEAD_DIM, 1), 0) < HEAD_DIM
    xs_all, wst_all, ecs_all, edec_all, y_diag = [], [], [], [], []
    for p, xh in enumerate(xh_pairs):
        per = []
        for c in (col + 2 * p, col + 2 * p + 1):
            csv = jnp.sum(jnp.where(lane == c, cs_c, 0.0), axis=1, keepdims=True)
            csr = jnp.sum(jnp.where(sub == c, cs_r, 0.0), axis=0, keepdims=True)
            dtv = jnp.sum(jnp.where(lane == c, dtc, 0.0), axis=1, keepdims=True)
            tv = jnp.sum(jnp.where(lane1 == c, tot, 0.0), axis=1, keepdims=True)
            m = score * jnp.exp(jnp.where(mask, csv - csr, NEG_BIG))
            per.append((csv, dtv, tv, m))
        (cs1, dt1, t1, m1), (cs2, dt2, t2, m2) = per
        xs = xh * jnp.where(left, dt1, dt2)
        both = jnp.dot(jnp.concatenate([m1, m2], axis=0).astype(BF16), xs.astype(BF16), preferred_element_type=F32)
        y_diag.append(jnp.where(left, both[:n], both[n:]))
        xs_all.append(xs)
        ecs_all.append(jnp.where(left, jnp.exp(cs1), jnp.exp(cs2)))
        wst_all.append(jnp.where(left, jnp.exp(t1 - cs1), jnp.exp(t2 - cs2)))
        edec_all.append(jnp.where(top, jnp.exp(t1), jnp.exp(t2)))
    cat = lambda parts, axis: parts[0] if len(parts) == 1 else jnp.concatenate(parts, axis=axis)
    xs, wst, ecs = cat(xs_all, 1), cat(wst_all, 1), cat(ecs_all, 1)
    y_off = lax.dot_general(cc.astype(BF16), st.astype(BF16), _NT, preferred_element_type=F32) * ecs
    cst = lax.dot_general((xs * wst).astype(BF16), bc.astype(BF16), _TN, preferred_element_type=F32)
    return cat(y_diag, 1) + y_off, st * cat(edec_all, 0) + cst


class _Scan:
    def __init__(self, lay, rev):
        self.ncx, self.ncc, self.bl, self.rev = lay.seq // CHUNK, lay.clen // CHUNK, lay.bl, rev
        self.nct = self.ncx + self.ncc

    def chunk(self, b, pos):
        kc = (self.ncc - 1 - pos) if self.rev else pos
        kx = (self.ncx - 1 - (pos - self.ncc)) if self.rev else (pos - self.ncc)
        return jnp.where(pos < self.ncc, self.bl * self.ncx + b * self.ncc + kc, b * self.ncx + kx)


def _ssd_io(mx, x_ref, st_src):
    np_ = mx.heads // 2
    d = mx.d_ssm
    xh = [x_ref[:, 128 * p:128 * (p + 1)] for p in range(np_)]
    bcs = [x_ref[:, d + N_STATE * g:d + N_STATE * (g + 1)] for g in range(SSD_GROUPS)]
    ccs = [x_ref[:, d + mx.gn + N_STATE * g:d + mx.gn + N_STATE * (g + 1)] for g in range(SSD_GROUPS)]
    gw = d // SSD_GROUPS
    sts = [st_src[gw * g:gw * (g + 1), :] for g in range(SSD_GROUPS)]
    return xh, bcs, ccs, sts


def _ssd_fwd(lay, mx, xbc, dt, dtt, a_row, a_col, rev, name):
    sc = _Scan(lay, rev)
    col0 = mx.heads if rev else 0
    hp = mx.heads * HEAD_DIM

    def body(x_ref, dt_ref, dtt_ref, ar_ref, ac_ref, y_ref, hp_ref, st_ref):
        @pl.when(pl.program_id(1) == 0)
        def _():
            st_ref[...] = jnp.zeros_like(st_ref)

        hp_ref[...] = st_ref[...]
        xh, bcs, ccs, sts = _ssd_io(mx, x_ref, st_ref)
        ys, new = _ssd_chunk(xh, bcs, ccs, dt_ref[...], dtt_ref[...], ar_ref[...], ac_ref[...], sts,
                             rev=rev, heads=mx.heads, col0=col0)
        gw = mx.d_ssm // SSD_GROUPS
        for g in range(SSD_GROUPS):
            y_ref[:, gw * g:gw * (g + 1)] = ys[g]
            st_ref[gw * g:gw * (g + 1), :] = new[g]

    ch = sc.chunk
    return pl.pallas_call(
        body, grid=(lay.bl, sc.nct),
        in_specs=[pl.BlockSpec((CHUNK, mx.xw), lambda b, i: (ch(b, i), 0)),
                  pl.BlockSpec((CHUNK, DT_LANES), lambda b, i: (ch(b, i), 0)),
                  pl.BlockSpec((DT_LANES, CHUNK), lambda b, i: (0, ch(b, i))),
                  pl.BlockSpec((1, DT_LANES), lambda b, i: (0, 0)),
                  pl.BlockSpec((DT_LANES, 1), lambda b, i: (0, 0))],
        out_specs=[pl.BlockSpec((CHUNK, mx.d_ssm), lambda b, i: (ch(b, i), 0)),
                   pl.BlockSpec((hp, N_STATE), lambda b, i: (b * sc.nct + i, 0))],
        out_shape=[jax.ShapeDtypeStruct((lay.ta, mx.d_ssm), F32),
                   jax.ShapeDtypeStruct((lay.bl * sc.nct * hp, N_STATE), F32)],
        scratch_shapes=[pltpu.VMEM((hp, N_STATE), F32)], name=name, compiler_params=_cparams(2),
    )(xbc, dt, dtt, a_row, a_col)


def _ssd_bwd(lay, mx, xbc, dt, dtt, a_row, a_col, hprev, dy, dskip, rev, name):
    sc = _Scan(lay, rev)
    col0 = mx.heads if rev else 0
    hp = mx.heads * HEAD_DIM
    np_ = mx.heads // 2
    d = mx.d_ssm
    with_skip = dskip is not None

    def body(*refs):
        if with_skip:
            x_ref, dt_ref, dtt_ref, ar_ref, ac_ref, hp_ref, dy_ref, sk_ref = refs[:8]
            rest = refs[8:]
        else:
            x_ref, dt_ref, dtt_ref, ar_ref, ac_ref, hp_ref, dy_ref = refs[:7]
            rest = refs[7:]
        dx_ref, ddc_ref, ddr_ref, dar_ref, dac_ref, ds_ref = rest
        b, i = pl.program_id(0), pl.program_id(1)

        @pl.when(i == 0)
        def _():
            ds_ref[...] = jnp.zeros_like(ds_ref)

        xh, bcs, ccs, sts = _ssd_io(mx, x_ref, hp_ref)
        dtc = dt_ref[...]
        shared, vjp_shared = jax.vjp(functools.partial(_ssd_shared, rev=rev), bcs, ccs, dtc, dtt_ref[...],
                                     ar_ref[...], ac_ref[...])
        cs_c, cs_r, tot, scores = shared
        plus = lambda acc, v: v if acc is None else acc + v
        d_cs_c = d_cs_r = d_tot = ddc = None
        d_scores, dbc, dcc = [], [], []
        ppg = np_ // SSD_GROUPS
        gw = d // SSD_GROUPS
        for g in range(SSD_GROUPS):
            dyg = dy_ref[:, gw * g:gw * (g + 1)]
            fn = functools.partial(_ssd_group, rev=rev, col=col0 + 2 * ppg * g)
            _, vjp = jax.vjp(fn, xh[g * ppg:(g + 1) * ppg], bcs[g], ccs[g], sts[g], cs_c, cs_r, tot, dtc, scores[g])
            dxh, dbc_g, dcc_g, dst, dcs_c_g, dcs_r_g, dtot_g, ddc_g, dsc_g = vjp((dyg, ds_ref[gw * g:gw * (g + 1), :]))
            for q in range(ppg):
                p = g * ppg + q
                v = dxh[q]
                if with_skip:
                    v = v + dyg[:, 128 * q:128 * (q + 1)] * sk_ref[:, 128 * p:128 * (p + 1)]
                dx_ref[:, 128 * p:128 * (p + 1)] = v
            ds_ref[gw * g:gw * (g + 1), :] = dst
            d_cs_c, d_cs_r, d_tot, ddc = plus(d_cs_c, dcs_c_g), plus(d_cs_r, dcs_r_g), plus(d_tot, dtot_g), plus(ddc, ddc_g)
            d_scores.append(dsc_g)
            dbc.append(dbc_g)
            dcc.append(dcc_g)
        dbc_s, dcc_s, ddc_s, ddr, dar, dac = vjp_shared((d_cs_c, d_cs_r, d_tot, d_scores))
        ddc = ddc + ddc_s
        dbc = [dbc[g] + dbc_s[g] for g in range(SSD_GROUPS)]
        dcc = [dcc[g] + dcc_s[g] for g in range(SSD_GROUPS)]
        for g in range(SSD_GROUPS):
            dx_ref[:, d + N_STATE * g:d + N_STATE * (g + 1)] = dbc[g]
            dx_ref[:, d + mx.gn + N_STATE * g:d + mx.gn + N_STATE * (g + 1)] = dcc[g]
        ddc_ref[...] = ddc
        ddr_ref[...] = ddr
        first = jnp.logical_and(b == 0, i == 0)
        _acc(dar_ref, first, dar)
        _acc(dac_ref, first, dac)

    ch = lambda b, i: sc.chunk(b, sc.nct - 1 - i)
    in_specs = [pl.BlockSpec((CHUNK, mx.xw), lambda b, i: (ch(b, i), 0)),
                pl.BlockSpec((CHUNK, DT_LANES), lambda b, i: (ch(b, i), 0)),
                pl.BlockSpec((DT_LANES, CHUNK), lambda b, i: (0, ch(b, i))),
                pl.BlockSpec((1, DT_LANES), lambda b, i: (0, 0)),
                pl.BlockSpec((DT_LANES, 1), lambda b, i: (0, 0)),
                pl.BlockSpec((hp, N_STATE), lambda b, i: (b * sc.nct + sc.nct - 1 - i, 0)),
                pl.BlockSpec((CHUNK, d), lambda b, i: (ch(b, i), 0))]
    inputs = [xbc, dt, dtt, a_row, a_col, hprev, dy]
    if with_skip:
        in_specs.append(pl.BlockSpec((1, d), lambda b, i: (0, 0)))
        inputs.append(dskip)
    return pl.pallas_call(
        body, grid=(lay.bl, sc.nct), in_specs=in_specs,
        out_specs=[pl.BlockSpec((CHUNK, mx.xw), lambda b, i: (ch(b, i), 0)),
                   pl.BlockSpec((CHUNK, DT_LANES), lambda b, i: (ch(b, i), 0)),
                   pl.BlockSpec((DT_LANES, CHUNK), lambda b, i: (0, ch(b, i))),
                   pl.BlockSpec((1, DT_LANES), lambda b, i: (0, 0)),
                   pl.BlockSpec((DT_LANES, 1), lambda b, i: (0, 0))],
        out_shape=[jax.ShapeDtypeStruct((lay.ta, mx.xw), F32), jax.ShapeDtypeStruct((lay.ta, DT_LANES), F32),
                   jax.ShapeDtypeStruct((DT_LANES, lay.ta), F32), jax.ShapeDtypeStruct((1, DT_LANES), F32),
                   jax.ShapeDtypeStruct((DT_LANES, 1), F32)],
        scratch_shapes=[pltpu.VMEM((hp, N_STATE), F32)], name=name, compiler_params=_cparams(2),
    )(*inputs)


def _gate_f(yf, yb, xh, z, drow, nw):
    dd = yf.shape[-1]
    half = dd // SSD_GROUPS
    yz = (yf + yb + drow * xh) * _silu(z)
    lo = lax.broadcasted_iota(jnp.int32, yz.shape, 1) < half
    sq = yz * yz
    ms1 = jnp.sum(jnp.where(lo, sq, 0.0), axis=-1, keepdims=True) / half
    ms2 = jnp.sum(jnp.where(lo, 0.0, sq), axis=-1, keepdims=True) / half
    return yz * jnp.where(lo, lax.rsqrt(ms1 + EPS), lax.rsqrt(ms2 + EPS)) * nw


def _gate_fwd(lay, mx, yf, yb, xbc, proj, drow, nw, name):
    d = mx.d_ssm

    def body(yf_ref, yb_ref, xh_ref, z_ref, d_ref, w_ref, o_ref):
        o_ref[...] = _gate_f(yf_ref[...], yb_ref[...], xh_ref[...], z_ref[...], d_ref[...], w_ref[...]).astype(o_ref.dtype)

    t = _tok(lay, d)
    return _tok_call(name, body, 1, lay.nsx, [t, t, t, t, _glob(1, d), _glob(1, d)], t,
                     jax.ShapeDtypeStruct((lay.tx, d + mx.d_conv), BF16), (yf, yb, xbc, proj, drow, nw))


def _gate_bwd(lay, mx, yf, yb, xbc, proj, drow, nw, dcat, name):
    d = mx.d_ssm
    nsx = lay.nsx

    def body(yf_ref, yb_ref, xh_ref, z_ref, d_ref, w_ref, dc_ref, dy_ref, dz_ref, dd_ref, dw_ref):
        s = pl.program_id(1)

        @pl.when(s < nsx)
        def _():
            _, vjp = jax.vjp(_gate_f, yf_ref[...], yb_ref[...], xh_ref[...], z_ref[...], d_ref[...], w_ref[...])
            dyf, _, _, dz, dd, dw = vjp(dc_ref[...])
            dy_ref[...] = dyf
            dz_ref[...] = dz.astype(dz_ref.dtype)
            _acc(dd_ref, s == 0, dd)
            _acc(dw_ref, s == 0, dw)

        @pl.when(s >= nsx)
        def _():
            dy_ref[...] = jnp.zeros_like(dy_ref)
            dz_ref[...] = jnp.zeros_like(dz_ref)

    t = _tok(lay, d)
    return _tok_call(name, body, 1, lay.ns, [t, t, t, t, _glob(1, d), _glob(1, d), _tok(lay, d, clamp=nsx - 1)],
                     [t, t, _glob(1, d), _glob(1, d)],
                     [jax.ShapeDtypeStruct((lay.ta, d), F32), jax.ShapeDtypeStruct((lay.ta, mx.pw), BF16),
                      jax.ShapeDtypeStruct((1, d), F32), jax.ShapeDtypeStruct((1, d), F32)],
                     (yf, yb, xbc, proj, drow, nw, dcat))


def _glu_fwd(lay, mx, proj, name):
    d = mx.d_conv
    c = math.gcd(mx.off_glu, d)
    cb = mx.off_glu // c

    def body(a_ref, b_ref, o_ref):
        o_ref[...] = a_ref[...] * jax.nn.sigmoid(b_ref[...])

    return _tok_call(name, body, d // c, lay.nsx, [_tok(lay, c, cb), _tok(lay, c, cb + d // c)], _tok(lay, c),
                     jax.ShapeDtypeStruct((lay.tx, d), F32), (proj, proj))


def _glu_bwd(lay, mx, proj, du, dproj, name):
    d = mx.d_conv
    c = math.gcd(mx.off_glu, d)
    cb = mx.off_glu // c
    nc = d // c
    nsx = lay.nsx

    def body(a_ref, b_ref, du_ref, buf_ref, o_ref):
        j, s = pl.program_id(0), pl.program_id(1)

        @pl.when(s < nsx)
        def _():
            sg = jax.nn.sigmoid(b_ref[...])
            da = du_ref[...] * sg
            o_ref[...] = jnp.where(j < nc, da, da * a_ref[...] * (1.0 - sg)).astype(o_ref.dtype)

        @pl.when(s >= nsx)
        def _():
            o_ref[...] = jnp.zeros_like(o_ref)

    win = lambda half: pl.BlockSpec((lay.tt, c), lambda j, s: (s, cb + half * nc + j % nc))
    return _tok_call(name, body, 2 * nc, lay.ns,
                     [win(0), win(1), pl.BlockSpec((lay.tt, c), lambda j, s: (jnp.minimum(s, nsx - 1), j % nc)),
                      pl.BlockSpec(memory_space=pl.ANY)],
                     _tok(lay, c, cb), jax.ShapeDtypeStruct(dproj.shape, dproj.dtype), (proj, proj, du, dproj),
                     aliases={3: 0})


def _axial(lay, mx, u, dy, cw, cb, name):
    d, seq = mx.d_conv, lay.seq
    kw = cw.shape[0]
    pad = kw // 2
    c = _pick(d // 2, (256, 128))
    ncb = d // c
    zpad = GRID_W * pad
    zpad = -(-zpad // 8) * 8
    backward = dy is not None

    def shifted(ext_ref, off):
        return ext_ref[pl.ds(zpad + off, seq), :]

    def valid_row(off):
        col = lax.broadcasted_iota(jnp.int32, (seq, c), 0) % GRID_W
        return jnp.logical_and(col + off >= 0, col + off < GRID_W)

    def fill(ext_ref, v):
        ext_ref[0:zpad, :] = jnp.zeros((zpad, c), F32)
        ext_ref[zpad:zpad + seq, :] = v
        ext_ref[zpad + seq:, :] = jnp.zeros((zpad, c), F32)

    def conv(ext_ref, w_ref, is_row, sign):
        acc = jnp.zeros((seq, c), F32)
        for k in range(kw):
            off = sign * ((k - pad) if is_row else GRID_W * (k - pad))
            v = shifted(ext_ref, off)
            if is_row:
                v = jnp.where(valid_row(off), v, 0.0)
            acc = acc + w_ref[k:k + 1, :] * v
        return acc

    def fwd_body(u_ref, w_ref, b_ref, o_ref, ext_ref):
        j = pl.program_id(0)
        fill(ext_ref, u_ref[...])

        @pl.when(j < ncb // 2)
        def _():
            o_ref[...] = conv(ext_ref, w_ref, True, 1) + b_ref[...]

        @pl.when(j >= ncb // 2)
        def _():
            o_ref[...] = conv(ext_ref, w_ref, False, 1) + b_ref[...]

    def bwd_body(u_ref, dy_ref, w_ref, du_ref, dw_ref, db_ref, extu_ref, extd_ref):
        j, b = pl.program_id(0), pl.program_id(1)
        dyv = dy_ref[...]
        fill(extu_ref, u_ref[...])
        fill(extd_ref, dyv)

        def grads(is_row):
            du_ref[...] = conv(extd_ref, w_ref, is_row, -1)
            rows = []
            for k in range(kw):
                off = (k - pad) if is_row else GRID_W * (k - pad)
                v = shifted(extu_ref, off)
                if is_row:
                    v = jnp.where(valid_row(off), v, 0.0)
                rows.append(jnp.sum(dyv * v, axis=0, keepdims=True))
            _acc(dw_ref, b == 0, jnp.concatenate(rows, axis=0))

        @pl.when(j < ncb // 2)
        def _():
            grads(True)

        @pl.when(j >= ncb // 2)
        def _():
            grads(False)

        _acc(db_ref, b == 0, jnp.sum(dyv, axis=0, keepdims=True))

    seq_spec = pl.BlockSpec((seq, c), lambda j, b: (b, j))
    w_spec = pl.BlockSpec((kw, c), lambda j, b: (0, j))
    b_spec = pl.BlockSpec((1, c), lambda j, b: (0, j))
    ext = pltpu.VMEM((seq + 2 * zpad, c), F32)
    if not backward:
        return pl.pallas_call(fwd_body, grid=(ncb, lay.bl), in_specs=[seq_spec, w_spec, b_spec], out_specs=seq_spec,
                              out_shape=jax.ShapeDtypeStruct((lay.tx, d), F32), scratch_shapes=[ext], name=name,
                              compiler_params=_cparams(2))(u, cw, cb)
    return pl.pallas_call(bwd_body, grid=(ncb, lay.bl), in_specs=[seq_spec, seq_spec, w_spec],
                          out_specs=[seq_spec, w_spec, b_spec],
                          out_shape=[jax.ShapeDtypeStruct((lay.tx, d), F32), jax.ShapeDtypeStruct((kw, d), F32),
                                     jax.ShapeDtypeStruct((1, d), F32)],
                          scratch_shapes=[ext, ext], name=name, compiler_params=_cparams(2))(u, dy, cw)


def _ln_silu_f(u, w, b):
    mu = jnp.mean(u, axis=-1, keepdims=True)
    var = jnp.mean(jnp.square(u - mu), axis=-1, keepdims=True)
    return _silu((u - mu) * lax.rsqrt(var + EPS) * w + b)


def _ln_fwd(lay, mx, u, w, b, cat, name):
    d = mx.d_conv
    assert mx.d_ssm % d == 0

    def body(u_ref, w_ref, b_ref, cat_ref, o_ref):
        o_ref[...] = _ln_silu_f(u_ref[...], w_ref[...], b_ref[...]).astype(o_ref.dtype)

    return _tok_call(name, body, 1, lay.nsx,
                     [_tok(lay, d), _glob(1, d), _glob(1, d), pl.BlockSpec(memory_space=pl.ANY)],
                     _tok(lay, d, mx.d_ssm // d), jax.ShapeDtypeStruct(cat.shape, cat.dtype), (u, w, b, cat),
                     aliases={3: 0})


def _ln_bwd(lay, mx, u, w, b, dcat, name):
    d = mx.d_conv

    def body(u_ref, w_ref, b_ref, dc_ref, du_ref, dw_ref, db_ref):
        s = pl.program_id(1)
        _, vjp = jax.vjp(_ln_silu_f, u_ref[...], w_ref[...], b_ref[...])
        du, dw, db = vjp(dc_ref[...])
        du_ref[...] = du
        _acc(dw_ref, s == 0, dw)
        _acc(db_ref, s == 0, db)

    return _tok_call(name, body, 1, lay.nsx, [_tok(lay, d), _glob(1, d), _glob(1, d), _tok(lay, d, 1)],
                     [_tok(lay, d), _glob(1, d), _glob(1, d)],
                     [jax.ShapeDtypeStruct((lay.tx, d), F32), jax.ShapeDtypeStruct((1, d), F32),
                      jax.ShapeDtypeStruct((1, d), F32)], (u, w, b, dcat))


def _gate_tile(dff):
    return dff // 2 if (dff // 2) % 128 == 0 else dff


def _ffn_fwd(lay, nseg, x, nw, modv, k0, wts, tag, ride=None):
    wgu, wd, ft = wts
    h = _norm_mod_fwd(lay, nseg, x, nw, modv, k0, tag + "_norm")
    t = h.shape[0]

    def act(acc):
        g, u = acc[:, :ft], acc[:, ft:]
        sg = jax.nn.sigmoid(g)
        sl = g * sg
        return jnp.concatenate([u * (sg * (1.0 + g * (1.0 - sg))), sl], axis=1), sl * u

    res = _mm(h, wgu, "nn", None, tag + "_gu", tn=2 * ft, tm=256 if t % 256 == 0 else None,
              epilogue=act, outs=[(2 * ft, BF16), (ft, BF16)], ride=ride)
    (s, a), rode = res if ride else (res, None)
    o, y = _mm_resid(lay, a, wd, x, modv, k0 + 2, 0.5, tag + "_down")
    return y, (x, h, s, a, o), rode


def _ffn_bwd(lay, nseg, dy, saved, nw, modv, k0, wts, tag, nout=None, rides=(None, None)):
    wgu, wd, ft = wts
    x, h, s, a, o = saved
    do, dgate = _resid_bwd(lay, nseg, dy, o, modv, k0 + 2, 0.5, tag + "_dres")

    def through_act(da, s_tile):
        return (jnp.concatenate([da, da], axis=1) * s_tile.astype(F32),)

    (dgu,) = _mm(do, wd.T, "nn", None, tag + "_da", tn=ft, tm=_big_tile(do.shape[0], 1024), extras=[(s, 2 * ft)],
                 epilogue=through_act, outs=[(2 * ft, BF16)])
    dwd = _mm(a, do, "tn", F32, tag + "_dwd")
    dh = _mm(dgu, wgu, "nt", F32, tag + "_dh", ride=rides[0])
    dwgu = _mm(h, dgu, "tn", F32, tag + "_dwgu", ride=rides[1])
    (dh, rode_a), (dwgu, rode_b) = (dh if rides[0] else (dh, None)), (dwgu if rides[1] else (dwgu, None))
    dx, dnw, dss = _norm_mod_bwd(lay, nseg, nseg, x, nw, modv, k0, dh, dy, tag + "_dnorm", nout=nout)
    return dx, (dwgu, dwd), dnw, jnp.concatenate([dss, dgate], axis=1), (rode_a, rode_b)


def _local_step(lay, mx, xa, target, modv, w, exch=None):
    d, bl = lay.d, lay.bl
    g = {}
    xa1, ffn1, late = _ffn_fwd(lay, lay.ns, xa, w["norm_ffn1"], modv, 0, w["ffn1"], "ffn1",
                               ride=exch.late_weights if exch else None)
    if exch:
        w = {**w, **exch.unpack_late(late)}
    ha = _norm_mod_fwd(lay, lay.ns, xa1, w["norm_mix"], modv, 3, "mix_norm")
    proj = _mm(ha, w["w_in"], "nn", F32, "mix_in")
    pre, xbc = _conv5_fwd(lay.fine(), mx, proj, w["conv_w"], w["conv_b"], "mix_conv")
    dt = _dt_fwd(lay, mx, proj, w["dt_bias"], "mix_dt")
    dtt = dt.T
    yf, hpf = _ssd_fwd(lay, mx, xbc, dt, dtt, w["a_row"], w["a_col"], False, "ssd_f")
    yb, hpb = _ssd_fwd(lay, mx, xbc, dt, dtt, w["a_row"], w["a_col"], True, "ssd_b")
    cat_y = _gate_fwd(lay, mx, yf, yb, xbc, proj, w["d_row"], w["ssm_norm_w"], "mix_gate")
    u0 = _glu_fwd(lay, mx, proj, "mix_glu")
    uc = _axial(lay, mx, u0, None, w["cconv_w"], w["cconv_b"], "mix_axial")
    cat = _ln_fwd(lay, mx, uc, w["ln_w"], w["ln_b"], cat_y, "mix_ln")
    mix, x2 = _mm_resid(lay, cat, w["w_out"], xa1, modv, 5, 1.0, "mix_out")
    x3, ffn2, _ = _ffn_fwd(lay, lay.nsx, x2, w["norm_ffn2"], modv, 6, w["ffn2"], "ffn2")
    loss, dx3, g["final_norm"] = _final_loss(lay, x3, w["final_norm"], target, "loss")
    dx2, g["ffn2"], g["norm_ffn2"], dmod2, _ = _ffn_bwd(lay, lay.nsx, dx3, ffn2, w["norm_ffn2"], modv, 6, w["ffn2"], "ffn2")
    dmix, dg2 = _resid_bwd(lay, lay.nsx, dx2, mix, modv, 5, 1.0, "mix_dres")
    dcat = _mm(dmix, w["w_out"], "nt", F32, "mix_dcat")
    g["w_out"] = _mm(cat, dmix, "tn", F32, "mix_dwout")
    duc, g["ln_w"], g["ln_b"] = _ln_bwd(lay, mx, uc, w["ln_w"], w["ln_b"], dcat, "mix_dln")
    du0, g["cconv_w"], g["cconv_b"] = _axial(lay, mx, u0, duc, w["cconv_w"], None, "mix_daxial")
    dyssd, dproj, g["d_row"], g["ssm_norm_w"] = _gate_bwd(lay, mx, yf, yb, xbc, proj, w["d_row"], w["ssm_norm_w"], dcat,
                                                          "mix_dgate")
    dproj = _glu_bwd(lay, mx, proj, du0, dproj, "mix_dglu")
    dxf, ddcf, ddrf, darf, dacf = _ssd_bwd(lay, mx, xbc, dt, dtt, w["a_row"], w["a_col"], hpf, dyssd, w["d_row"],
                                           False, "ssd_df")
    dxb, ddcb, ddrb, darb, dacb = _ssd_bwd(lay, mx, xbc, dt, dtt, w["a_row"], w["a_col"], hpb, dyssd, None,
                                           True, "ssd_db")
    g["a_row"] = (darf + darb) + (dacf + dacb).T
    dproj, g["conv_w"], g["conv_b"] = _conv5_bwd(lay.fine(), mx, proj, pre, dxf, dxb, w["conv_w"], dproj, "mix_dconv")
    dproj, g["dt_bias"] = _dt_bwd(lay, mx, proj, w["dt_bias"], (ddcf, ddcb, ddrf.T, ddrb.T), dproj, "mix_ddt")
    dha = _mm(dproj, w["w_in"], "nt", F32, "mix_dha")
    g["w_in"] = _mm(ha, dproj, "tn", F32, "mix_dwin")
    dxa1, g["norm_mix"], dss_mix = _norm_mod_bwd(lay, lay.ns, lay.nsx, xa1, w["norm_mix"], modv, 3, dha, dx2, "mix_dnorm")
    rides = exch.early_grads(g) if exch else (None, None)
    dx, g["ffn1"], g["norm_ffn1"], dmod1, rode = _ffn_bwd(lay, lay.ns, dxa1, ffn1, w["norm_ffn1"], modv, 0, w["ffn1"],
                                                         "ffn1", nout=lay.nsx, rides=rides)
    if exch:
        exch.take_early(rode)
    zrow = lambda t: jnp.concatenate([t, jnp.zeros((1,) + t.shape[1:], F32)], axis=0)
    dmodv = jnp.concatenate([dmod1, dss_mix, zrow(dg2), zrow(dmod2)], axis=1)
    return loss, dx, g, dmodv


class _GatherRide:
    def __init__(self, xs):
        self.arrays = list(xs)
        self.na = len(xs)
        self.out_shapes = [jax.ShapeDtypeStruct((N_DEV,) + tuple(x.shape), x.dtype) for x in xs]
        self.sems = [pltpu.SemaphoreType.DMA((7 * self.na,)), pltpu.SemaphoreType.DMA((7 * self.na,)),
                     pltpu.SemaphoreType.DMA((self.na,))]

    def _plan(self, x_refs, out_refs, send_sems, recv_sems, local_sems):
        mx_, my_, mc_ = lax.axis_index("x"), lax.axis_index("y"), lax.axis_index("c")
        me, sibling = (mx_, my_, mc_), (mx_, my_, 1 - mc_)
        chips = [(1 - mx_, my_), (mx_, 1 - my_), (1 - mx_, 1 - my_)]

        def slot(a, px, py, pc):
            return out_refs[a].at[4 * px + 2 * py + pc]

        def copy(a, k, block, to, own=False):
            return pltpu.make_async_remote_copy(
                src_ref=x_refs[a] if own else slot(a, *block), dst_ref=slot(a, *block),
                send_sem=send_sems.at[7 * a + k], recv_sem=recv_sems.at[7 * a + k], device_id=to, device_id_type=MESH)

        mine = [pltpu.make_async_copy(x_refs[a], slot(a, *me), local_sems.at[a]) for a in range(self.na)]
        first = []
        for a in range(self.na):
            first.append(copy(a, 0, me, sibling, own=True))
            first += [copy(a, 1 + j, me, (*chip, mc_), own=True) for j, chip in enumerate(chips)]
        return me, sibling, chips, mc_, copy, mine, first

    def start(self, x_refs, out_refs, send_sems, recv_sems, local_sems):
        *_, mine, first = self._plan(x_refs, out_refs, send_sems, recv_sems, local_sems)
        for cp in mine + first:
            cp.start()

    def finish(self, x_refs, out_refs, send_sems, recv_sems, local_sems):
        me, sibling, chips, mc_, copy, mine, first = self._plan(x_refs, out_refs, send_sems, recv_sems, local_sems)
        passed = []
        for j, chip in enumerate(chips):
            for a in range(self.na):
                copy(a, 1 + j, (*chip, mc_), me).wait_recv()
                fwd = copy(a, 4 + j, (*chip, mc_), sibling)
                fwd.start()
                passed.append(fwd)
        for a in range(self.na):
            copy(a, 0, sibling, me).wait_recv()
            for j, chip in enumerate(chips):
                copy(a, 4 + j, (*chip, 1 - mc_), me).wait_recv()
        for cp in first + passed:
            cp.wait_send()
        for cp in mine:
            cp.wait()


def _exchange(ride, name, in_hbm=True):
    n_in, n_out = len(ride.arrays), len(ride.out_shapes)

    def body(*refs):
        ins, outs, sems = refs[:n_in], refs[n_in:n_in + n_out], refs[n_in + n_out:]
        ride.start(ins, outs, *sems)
        ride.finish(ins, outs, *sems)

    space = pl.BlockSpec(memory_space=pl.ANY if in_hbm else pltpu.VMEM)
    return pl.pallas_call(body, out_shape=list(ride.out_shapes), in_specs=[space] * n_in, out_specs=[space] * n_out,
                          scratch_shapes=list(ride.sems), name=name)(*ride.arrays)


def _all_gather(xs, name, in_hbm):
    return _exchange(_GatherRide(xs), name, in_hbm)


N_CHIPS = 4


def _swap_sibling(gs, name):
    na = len(gs)

    def body(*refs):
        g_refs, out_refs, send_sems, recv_sems = refs[:na], refs[na:2 * na], refs[2 * na], refs[2 * na + 1]
        mx_, my_, mc_ = lax.axis_index("x"), lax.axis_index("y"), lax.axis_index("c")
        copies = [pltpu.make_async_remote_copy(
            src_ref=g_refs[a].at[k, 1 - mc_], dst_ref=out_refs[a].at[k], send_sem=send_sems.at[N_CHIPS * a + k],
            recv_sem=recv_sems.at[N_CHIPS * a + k], device_id=(mx_, my_, 1 - mc_), device_id_type=MESH)
            for a in range(na) for k in range(N_CHIPS)]
        for cp in copies:
            cp.start()
        for cp in copies:
            cp.wait_recv()
        for cp in copies:
            cp.wait_send()

    return pl.pallas_call(
        body, out_shape=[jax.ShapeDtypeStruct((N_CHIPS,) + tuple(g.shape[2:]), g.dtype) for g in gs],
        in_specs=[pl.BlockSpec(memory_space=pl.ANY)] * na, out_specs=[pl.BlockSpec(memory_space=pl.ANY)] * na,
        scratch_shapes=[pltpu.SemaphoreType.DMA((N_CHIPS * na,)), pltpu.SemaphoreType.DMA((N_CHIPS * na,))], name=name,
    )(*gs)


def _row_tile(r, n):
    if r * n * 4 <= (1 << 20):
        return r
    for t in (1024, 512, 256, 128, 64, 32, 16, 8):
        if r % t == 0 and t * n * 4 <= (1 << 20):
            return t
    return r


def _pair_add(place, g, got, name):
    _, _, r, n = g.shape
    tr = r if r * n * 4 <= (3 << 19) else _row_tile(r, n)

    def body(place_ref, g_ref, got_ref, o_ref, ob_ref):
        s = g_ref[...] + got_ref[...]
        o_ref[...] = s
        ob_ref[...] = s.astype(ob_ref.dtype)

    blk = pl.BlockSpec((None, tr, n), lambda k, i, pr: (k, i, 0))
    grid_spec = pltpu.PrefetchScalarGridSpec(
        num_scalar_prefetch=1, grid=(N_CHIPS, r // tr),
        in_specs=[pl.BlockSpec((None, None, tr, n), lambda k, i, pr: (k, pr[0], i, 0)), blk], out_specs=[blk, blk])
    return pl.pallas_call(body, grid_spec=grid_spec,
                          out_shape=[jax.ShapeDtypeStruct((N_CHIPS, r, n), F32), jax.ShapeDtypeStruct((N_CHIPS, r, n), BF16)],
                          name=name, compiler_params=_cparams(2))(place, g, got)


class _ChipSwapRide:
    def __init__(self, ps):
        self.arrays = list(ps)
        self.na = len(ps)
        self.out_shapes = [jax.ShapeDtypeStruct((N_CHIP_PEERS,) + tuple(p.shape[1:]), p.dtype) for p in ps]
        self.sems = [pltpu.SemaphoreType.DMA((N_CHIP_PEERS * self.na,)), pltpu.SemaphoreType.DMA((N_CHIP_PEERS * self.na,))]

    def _copies(self, p_refs, out_refs, send_sems, recv_sems):
        mx_, my_, mc_ = lax.axis_index("x"), lax.axis_index("y"), lax.axis_index("c")
        chips = [(1 - mx_, my_), (mx_, 1 - my_), (1 - mx_, 1 - my_)]
        return [pltpu.make_async_remote_copy(
            src_ref=p_refs[a].at[2 * cx + cy], dst_ref=out_refs[a].at[j], send_sem=send_sems.at[N_CHIP_PEERS * a + j],
            recv_sem=recv_sems.at[N_CHIP_PEERS * a + j], device_id=(cx, cy, mc_), device_id_type=MESH)
            for a in range(self.na) for j, (cx, cy) in enumerate(chips)]

    def start(self, p_refs, out_refs, send_sems, recv_sems):
        for cp in self._copies(p_refs, out_refs, send_sems, recv_sems):
            cp.start()

    def finish(self, p_refs, out_refs, send_sems, recv_sems):
        copies = self._copies(p_refs, out_refs, send_sems, recv_sems)
        for cp in copies:
            cp.wait_recv()
        for cp in copies:
            cp.wait_send()


def _sum_lead(x, name):
    k, r, n = x.shape
    tr = _row_tile(r, n * k)

    def body(x_ref, o_ref):
        acc = x_ref[0]
        for i in range(1, k):
            acc = acc + x_ref[i]
        o_ref[...] = acc

    return pl.pallas_call(body, grid=(r // tr,), in_specs=[pl.BlockSpec((k, tr, n), lambda i: (0, i, 0))],
                          out_specs=pl.BlockSpec((tr, n), lambda i: (i, 0)),
                          out_shape=jax.ShapeDtypeStruct((r, n), x.dtype), name=name, compiler_params=_cparams(1))(x)


def _adamw(place, w, parts, m, v, name):
    shape = w.shape
    cols = shape[-1]
    rows = math.prod(shape[:-1])
    to2 = lambda t: t.reshape(rows, cols)
    tr = _row_tile(rows, cols) if rows * cols * 4 > (1 << 20) else rows
    npart = len(parts)
    spec = pl.BlockSpec((tr, cols), lambda i, pr: (i, 0))
    native = len(shape) == 3 and shape[0] == 1
    own = pl.BlockSpec((None, tr, cols), lambda i, pr: (0, i, 0)) if native else spec
    as_own = (lambda t: t) if native else to2
    part_specs, part_args = [], []
    for piece in parts:
        if isinstance(piece, tuple):
            stack, k, row0 = piece
            part_args.append(stack.reshape(stack.shape[0], -1, cols))
            assert row0 % tr == 0
            if k == "chip":
                part_specs.append(pl.BlockSpec((None, tr, cols), functools.partial(lambda i, pr, b0: (pr[1], i + b0, 0),
                                                                                   b0=row0 // tr)))
            else:
                part_specs.append(pl.BlockSpec((None, tr, cols), functools.partial(
                    lambda i, pr, kk, b0: (kk, i + b0, 0), kk=k, b0=row0 // tr)))
        else:
            part_args.append(to2(piece))
            part_specs.append(spec)

    def body(place_ref, *refs):
        w_ref, m_ref, v_ref = refs[0], refs[1 + npart], refs[2 + npart]
        g_ref, d_ref, nm_ref, nv_ref = refs[3 + npart:]
        g = refs[1][...].astype(F32)
        for q in range(1, npart):
            g = g + refs[1 + q][...].astype(F32)
        mm = ADAM_B1 * m_ref[...] + (1.0 - ADAM_B1) * g
        vv = ADAM_B2 * v_ref[...] + (1.0 - ADAM_B2) * jnp.square(g)
        m_hat = mm / (1.0 - ADAM_B1 ** ADAM_STEP)
        v_hat = vv / (1.0 - ADAM_B2 ** ADAM_STEP)
        g_ref[...] = g
        d_ref[...] = -ADAM_LR * (m_hat / (jnp.sqrt(v_hat) + ADAM_EPS) + ADAM_WD * w_ref[...])
        nm_ref[...] = mm
        nv_ref[...] = vv

    sh = jax.ShapeDtypeStruct(shape if native else (rows, cols), F32)
    grid_spec = pltpu.PrefetchScalarGridSpec(num_scalar_prefetch=1, grid=(rows // tr,),
                                             in_specs=[own] + part_specs + [own, own], out_specs=[own] * 4)
    outs = pl.pallas_call(body, grid_spec=grid_spec, out_shape=[sh] * 4, name=name, compiler_params=_cparams(1),
                          )(place, as_own(w), *part_args, as_own(m), as_own(v))
    return tuple(o.reshape(shape) for o in outs)


def _packed_rows(n, width):
    return -(-n // (8 * width)) * 8


def _pack_rows(items, width):
    rows = []
    for t in items:
        flat = t.reshape(-1)
        n = flat.shape[0]
        k = _packed_rows(n, width)
        if k * width > n:
            flat = jnp.concatenate([flat, jnp.zeros((k * width - n,), t.dtype)])
        rows.append(flat.reshape(k, width))
    return jnp.concatenate(rows, axis=0)


def _unpack_rows(packed, shapes, lead=()):
    width = packed.shape[-1]
    out, r = [], 0
    for sh in shapes:
        n = math.prod(sh)
        k = _packed_rows(n, width)
        piece = packed[..., r:r + k, :].reshape(tuple(lead) + (k * width,))[..., :n]
        out.append(piece.reshape(tuple(lead) + tuple(sh)))
        r += k
    return out


def _cols_full(t):
    return jnp.transpose(t, (1, 0, 2)).reshape(t.shape[1], -1)


def _cols_shards(t):
    d = t.shape[0]
    return jnp.transpose(t.reshape(d, N_DEV, -1), (1, 0, 2))


BIG = ("ffn1_gate", "ffn1_up", "ffn1_down", "w_in", "w_out", "ffn2_gate", "ffn2_up", "ffn2_down")


def kernel(x, c, ctx, c_ctx, w_mod, b_mod, norm_ffn1, ffn1_gate, ffn1_up, ffn1_down, norm_mix, w_in, ssm_conv_w, ssm_conv_b, dt_bias_fwd, dt_bias_bwd, a_log_fwd, a_log_bwd, ssm_d, ssm_norm_w, cconv_w, cconv_b, cconv_ln_w, cconv_ln_b, w_out, norm_ffn2, ffn2_gate, ffn2_up, ffn2_down, final_norm, loss_target, m_c_ctx, m_w_mod, m_b_mod, m_norm_ffn1, m_ffn1_gate, m_ffn1_up, m_ffn1_down, m_norm_mix, m_w_in, m_ssm_conv_w, m_ssm_conv_b, m_dt_bias_fwd, m_dt_bias_bwd, m_a_log_fwd, m_a_log_bwd, m_ssm_d, m_ssm_norm_w, m_cconv_w, m_cconv_b, m_cconv_ln_w, m_cconv_ln_b, m_w_out, m_norm_ffn2, m_ffn2_gate, m_ffn2_up, m_ffn2_down, m_final_norm, v_c_ctx, v_w_mod, v_b_mod, v_norm_ffn1, v_ffn1_gate, v_ffn1_up, v_ffn1_down, v_norm_mix, v_w_in, v_ssm_conv_w, v_ssm_conv_b, v_dt_bias_fwd, v_dt_bias_bwd, v_a_log_fwd, v_a_log_bwd, v_ssm_d, v_ssm_norm_w, v_cconv_w, v_cconv_b, v_cconv_ln_w, v_cconv_ln_b, v_w_out, v_norm_ffn2, v_ffn2_gate, v_ffn2_up, v_ffn2_down, v_final_norm):
    args = dict(locals())
    names = ("c_ctx", "w_mod", "b_mod", "norm_ffn1", "ffn1_gate", "ffn1_up", "ffn1_down", "norm_mix", "w_in",
             "ssm_conv_w", "ssm_conv_b", "dt_bias_fwd", "dt_bias_bwd", "a_log_fwd", "a_log_bwd", "ssm_d", "ssm_norm_w",
             "cconv_w", "cconv_b", "cconv_ln_w", "cconv_ln_b", "w_out", "norm_ffn2", "ffn2_gate", "ffn2_up",
             "ffn2_down", "final_norm")
    wts = {n: args[n] for n in names}
    bl, seq, d = x.shape
    clen = ctx.shape[1]
    heads = dt_bias_fwd.shape[1]
    ft = _gate_tile(ffn1_gate.shape[2] * N_DEV)
    lay = _Lay(bl, seq, clen, d)
    mx = _Mix(d, heads)
    nb = bl * N_DEV
    me = 4 * lax.axis_index("x") + 2 * lax.axis_index("y") + lax.axis_index("c")
    mcols = w_mod.shape[2]
    n_ctx_mod = 5 * d

    place = jnp.stack([lax.axis_index("c"), 2 * lax.axis_index("x") + lax.axis_index("y")]).astype(jnp.int32)

    small_shapes = [(bl, d), ssm_conv_w.shape[1:], cconv_w.shape[1:]]
    (g1,) = _all_gather([_pack_rows([c, ssm_conv_w, cconv_w], d)], "gather_small", False)
    c_g, conv_g, cconv_g = _unpack_rows(g1, small_shapes, (N_DEV,))
    c_all = c_g.reshape(nb, d)
    conv_w_full = jnp.transpose(conv_g, (1, 0, 2)).reshape(conv_g.shape[1], -1)
    cconv_w_full = jnp.transpose(cconv_g, (1, 0, 2)).reshape(cconv_g.shape[1], -1)

    s_all = jnp.concatenate([_silu(c_all), _silu(c_ctx)[None, :], jnp.zeros((7, d), F32)], axis=0)
    mod_cols = _mm(s_all, w_mod[0], "nn", F32, "mod_cols")
    (g2,) = _all_gather([mod_cols], "gather_mod", False)
    mod_all = _cols_full(g2) + b_mod
    mod_mine = jnp.concatenate([lax.dynamic_slice_in_dim(mod_all, me * bl, bl, axis=0), mod_all[nb:nb + 1]], axis=0)
    modv = mod_mine.reshape(bl + 1, N_MOD, d)

    hh = 2 * heads
    shard16 = lambda n: wts[n][0].astype(BF16)

    nl = ffn1_gate.shape[2]
    spt = ft // nl
    assert ft % nl == 0 and N_DEV % spt == 0

    def ffn_weights(gate, up, down):
        both = jnp.stack([gate, up], axis=1).reshape(N_DEV // spt, spt, 2, d, nl)
        return jnp.transpose(both, (3, 0, 2, 1, 4)).reshape(d, -1), down.reshape(-1, d), ft

    def grads_by_dest(name, grad):
        if name == "w_in":
            grad = _cols_shards(jnp.concatenate([grad[:, :d], grad[:, mx.off_x:mx.off_glu],
                                                 grad[:, mx.off_dt:mx.off_dt + hh], grad[:, mx.off_glu:]], axis=1))
        elif name.endswith("_gu"):
            grad = jnp.transpose(grad.reshape(d, N_DEV // spt, 2, spt, nl), (1, 3, 2, 0, 4)).reshape(N_DEV, 2 * d, nl)
        else:
            grad = grad.reshape((N_DEV,) + tuple(wts[name].shape[1:]))
        return grad.reshape((N_CHIPS, 2) + tuple(grad.shape[1:]))

    def ffn_grads(tag, pair):
        return {tag + "_gu": pair[0], tag + "_down": pair[1]}

    def to_chip_sums(named):
        names_ = list(named)
        by_dest = [grads_by_dest(n, named[n]) for n in names_]
        got = _swap_sibling(by_dest, "rs_sibling_" + names_[0])
        return {n: _pair_add(place, t, s, "rs_pair_add_" + n) for n, t, s in zip(names_, by_dest, got)}

    class _Overlap:
        late_names = ("w_in", "w_out", "ffn2_gate", "ffn2_up", "ffn2_down")
        late_weights = _GatherRide([shard16(n) for n in late_names])
        sums, arrived = {}, {}

        def unpack_late(self, outs):
            full = dict(zip(self.late_names, outs))
            w_in_f = _cols_full(full["w_in"])
            w_in_p = jnp.concatenate([w_in_f[:, :d], w_in_f[:, mx.ref_dt:mx.ref_glu], jnp.zeros((d, DT_PAD - hh), BF16),
                                      w_in_f[:, mx.ref_x:mx.ref_dt], w_in_f[:, mx.ref_glu:]], axis=1)
            return {"w_in": w_in_p, "w_out": full["w_out"].reshape(-1, d),
                    "ffn2": ffn_weights(full["ffn2_gate"], full["ffn2_up"], full["ffn2_down"])}

        def early_grads(self, g):
            self.sums = to_chip_sums({**ffn_grads("ffn2", g["ffn2"]), "w_in": g["w_in"], "w_out": g["w_out"]})
            self.groups = (("ffn2_gu", "ffn2_down"), ("w_in", "w_out"))
            return tuple(_ChipSwapRide([self.sums[n][1] for n in grp]) for grp in self.groups)

        def take_early(self, rode):
            for grp, outs in zip(self.groups, rode):
                self.arrived.update(zip(grp, outs))

    exch = _Overlap()
    full = dict(zip(BIG[:3], _all_gather([shard16(n) for n in BIG[:3]], "gather_weights", True)))
    lanes_pad = lambda a, b: jnp.concatenate([a, b, jnp.zeros((1, DT_LANES - hh), F32)], axis=1)
    a_vals = lanes_pad(-jnp.exp(a_log_fwd), -jnp.exp(a_log_bwd))
    w = {
        "norm_ffn1": norm_ffn1, "norm_mix": norm_mix, "norm_ffn2": norm_ffn2, "final_norm": final_norm[None, :],
        "ffn1": ffn_weights(full["ffn1_gate"], full["ffn1_up"], full["ffn1_down"]),
        "conv_w": jnp.concatenate([conv_w_full, jnp.zeros((3, mx.xw), F32)], axis=0), "conv_b": ssm_conv_b,
        "dt_bias": lanes_pad(dt_bias_fwd, dt_bias_bwd), "a_row": a_vals, "a_col": a_vals.T,
        "d_row": jnp.repeat(ssm_d, HEAD_DIM, axis=1), "ssm_norm_w": ssm_norm_w,
        "cconv_w": cconv_w_full, "cconv_b": cconv_b, "ln_w": cconv_ln_w, "ln_b": cconv_ln_b,
    }

    xa = jnp.concatenate([x.reshape(bl * seq, d), ctx.reshape(bl * clen, d)], axis=0)
    loss, grad_x, g, dmodv = _local_step(lay, mx, xa, loss_target.reshape(bl * seq, d), modv, w, exch)
    loss = lax.psum(loss[0, 0], ("x", "y", "c"))

    sums = {**exch.sums, **to_chip_sums(ffn_grads("ffn1", g["ffn1"]))}
    arrived = dict(exch.arrived)
    last = ("ffn1_gu", "ffn1_down")
    arrived.update(zip(last, _exchange(_ChipSwapRide([sums[n][1] for n in last]), "rs_chips")))

    def big_parts(n):
        key, row0 = (n[:4] + "_gu", d if n.endswith("_up") else 0) if n.endswith(("_gate", "_up")) else (n, 0)
        return [(sums[key][0], "chip", row0)] + [(arrived[key], k, row0) for k in range(N_CHIP_PEERS)]

    n9 = N_MOD * d
    dmod_rows = dmodv.reshape(bl + 1, n9)
    ctx_row = jnp.concatenate([dmod_rows[bl, :n_ctx_mod], jnp.zeros((n9 - n_ctx_mod,), F32)])
    summed = [ctx_row, g["norm_ffn1"], g["norm_mix"], g["norm_ffn2"], g["final_norm"], g["conv_b"], g["dt_bias"],
              g["a_row"], g["d_row"], g["ssm_norm_w"], g["cconv_b"], g["ln_w"], g["ln_b"], g["conv_w"][:5], g["cconv_w"]]
    sum_shapes = [t.shape for t in summed]
    (g4,) = _all_gather([_pack_rows([dmod_rows[:bl]] + summed, d)], "gather_small_grads", False)
    dmod_batch = g4[:, :bl * N_MOD].reshape(nb, n9)
    tot = _sum_lead(g4[:, _packed_rows(bl * n9, d):], "sum_small_grads")
    (dctx, g_n1, g_nm, g_n2, g_fn, g_cb, g_dtb, g_a, g_drow, g_snw, g_ccb, g_lnw, g_lnb, g_cw, g_ccw) = _unpack_rows(tot, sum_shapes)
    dmod_all = jnp.concatenate([dmod_batch, dctx[None, :], jnp.zeros((7, n9), F32)], axis=0)

    dmod_my_cols = lax.dynamic_slice_in_dim(dmod_all, me * mcols, mcols, axis=1)
    g_w_mod = _mm(s_all, dmod_my_cols, "tn", F32, "dw_mod")[None]
    g_b_mod = _sum_lead(dmod_all.reshape(nb + 8, N_MOD, d), "db_mod").reshape(1, n9)
    ds_part = _mm(dmod_my_cols[nb:nb + 8], w_mod[0], "nt", F32, "ds_ctx")
    (g5,) = _all_gather([jnp.concatenate([ds_part[0:1], jnp.zeros((7, d), F32)], axis=0)], "gather_ds_ctx", False)
    ds_ctx = _sum_lead(g5, "sum_ds_ctx")[0]
    sg = jax.nn.sigmoid(c_ctx)
    g_c_ctx = ds_ctx * (sg * (1.0 + c_ctx * (1.0 - sg)))

    a_f, a_b = a_vals[:, :heads], a_vals[:, heads:hh]
    grads = {
        "c_ctx": [g_c_ctx], "w_mod": [g_w_mod], "b_mod": [g_b_mod],
        "norm_ffn1": [g_n1], "norm_mix": [g_nm], "norm_ffn2": [g_n2], "final_norm": [g_fn.reshape(-1)],
        "ssm_conv_w": [lax.dynamic_slice_in_dim(g_cw, me * ssm_conv_w.shape[2], ssm_conv_w.shape[2], axis=1)[None]],
        "ssm_conv_b": [g_cb],
        "dt_bias_fwd": [g_dtb[:, :heads]], "dt_bias_bwd": [g_dtb[:, heads:hh]],
        "a_log_fwd": [g_a[:, :heads] * a_f], "a_log_bwd": [g_a[:, heads:hh] * a_b],
        "ssm_d": [jnp.sum(g_drow.reshape(1, heads, HEAD_DIM), axis=2)], "ssm_norm_w": [g_snw],
        "cconv_w": [lax.dynamic_slice_in_dim(g_ccw, me * cconv_w.shape[2], cconv_w.shape[2], axis=1)[None]],
        "cconv_b": [g_ccb], "cconv_ln_w": [g_lnw], "cconv_ln_b": [g_lnb],
    }
    for n in BIG:
        grads[n] = big_parts(n)

    out_g, out_d, out_m, out_v = [], [], [], []
    for n in names:
        gr, de, nm, nv = _adamw(place, wts[n], grads[n], args["m_" + n], args["v_" + n], "adamw_" + n)
        out_g.append(gr)
        out_d.append(de)
        out_m.append(nm)
        out_v.append(nv)
    return (loss, grad_x.reshape(bl, seq, d), *out_g, *out_d, *out_m, *out_v)
```

```python
import functools
import math

import jax
import jax.numpy as jnp
from jax import lax
from jax.experimental import pallas as pl
from jax.experimental.pallas import tpu as pltpu

F32 = jnp.float32
BF16 = jnp.bfloat16
MESH = pl.DeviceIdType.MESH

N_DEV = 8
N_CHIP_PEERS = 3
HEAD_DIM = 64
N_STATE = 128
SSD_GROUPS = 2
CHUNK = 128
GRID_W = 64
N_MOD = 9
EPS = 1e-6
DT_PAD = 512
DT_LANES = 128
HALO = 8
ROW_TILE = 512
FINE_ROW_TILE = 256
VMEM_LIMIT = 48 * 1024 * 1024
NEG_BIG = -1e30

ADAM_LR = 0.001
ADAM_B1 = 0.9
ADAM_B2 = 0.999
ADAM_EPS = 1e-08
ADAM_WD = 0.01
ADAM_STEP = 10


def _pick(n, prefs):
    for p in prefs:
        if n % p == 0:
            return p
    return n


MM_TILE_CAP = 2816
MM_TILE_ELEMS = 3 << 20
MM_OUT_TILE_ELEMS = 3 << 19
MM_FULL_ROWS = 1024


def _big_tile(n, cap):
    if n <= cap:
        return n
    best = 0
    for t in range(128, cap + 1, 128):
        if n % t == 0:
            best = t
    return best or n


def _cparams(ndim):
    return pltpu.CompilerParams(dimension_semantics=("arbitrary",) * ndim, vmem_limit_bytes=VMEM_LIMIT)


def _silu(v):
    return v * jax.nn.sigmoid(v)


def _mm(a, b, mode, out_dtype, name, tn=None, tm=None, extras=(), epilogue=None, outs=None, ride=None):
    if mode == "tn":
        (K, M), (K2, N) = a.shape, b.shape
    elif mode == "nt":
        (M, K), (N, K2) = a.shape, b.shape
    else:
        (M, K), (K2, N) = a.shape, b.shape
    assert K == K2, (name, a.shape, b.shape)
    tm = tm or (M if M <= MM_FULL_ROWS else None)
    if tn is None:
        tn = _big_tile(N, min(MM_TILE_CAP, max(128, MM_OUT_TILE_ELEMS // (tm or 512))))
    if tm is None:
        tm = _big_tile(M, max(128, MM_OUT_TILE_ELEMS // tn))
    tk = _big_tile(K, min(MM_TILE_CAP, MM_TILE_ELEMS // max(tn, tm)))
    nk = K // tk
    ni, nj = M // tm, N // tn
    swap = nk == 1 and (K * N + M * K * nj) < (M * K + K * N * ni)
    ij = (lambda g0, g1: (g1, g0)) if swap else (lambda g0, g1: (g0, g1))
    if mode == "tn":
        a_spec = pl.BlockSpec((tk, tm), lambda g0, g1, k: (k, ij(g0, g1)[0]))
        dn = (((0,), (0,)), ((), ()))
    else:
        a_spec = pl.BlockSpec((tm, tk), lambda g0, g1, k: (ij(g0, g1)[0], k))
        dn = (((1,), (1,)), ((), ())) if mode == "nt" else (((1,), (0,)), ((), ()))
    if mode == "nt":
        b_spec = pl.BlockSpec((tn, tk), lambda g0, g1, k: (ij(g0, g1)[1], k))
    else:
        b_spec = pl.BlockSpec((tk, tn), lambda g0, g1, k: (k, ij(g0, g1)[1]))
    if outs is None:
        outs = [(tn, out_dtype)]
    nx = len(extras)

    def tile(w):
        return pl.BlockSpec((tm, w), lambda g0, g1, k: ij(g0, g1))

    def extra_spec(item):
        if len(item) == 3:
            return pl.BlockSpec((None,) + tuple(item[0].shape[1:]), lambda g0, g1, k: (item[1](ij(g0, g1)[0]), 0, 0))
        return tile(item[1])

    def finish(acc, refs):
        vals = (acc,) if epilogue is None else epilogue(acc, *[r[...] for r in refs[:nx]])
        for o_ref, v in zip(refs[nx:], vals):
            o_ref[...] = v.astype(o_ref.dtype)

    grid = (nj, ni, nk) if swap else (ni, nj, nk)
    nout = len(outs)
    r_in = len(ride.arrays) if ride else 0
    r_out = len(ride.out_shapes) if ride else 0

    def compute(a_ref, b_ref, refs):
        part = lax.dot_general(a_ref[...].astype(BF16), b_ref[...].astype(BF16), dn, preferred_element_type=F32)
        if nk == 1:
            finish(part, refs)
            return
        acc_ref, k = refs[-1], pl.program_id(2)
        _acc(acc_ref, k == 0, part)

        @pl.when(k == nk - 1)
        def _():
            finish(acc_ref[...], refs[:-1])

    def body(a_ref, b_ref, *refs):
        if ride is None:
            compute(a_ref, b_ref, refs)
            return
        x_refs, rin = refs[:nx], refs[nx:nx + r_in]
        o_refs, rout = refs[nx + r_in:nx + r_in + nout], refs[nx + r_in + nout:nx + r_in + nout + r_out]
        tail = refs[nx + r_in + nout + r_out:]
        nacc = 1 if nk > 1 else 0
        sems = tail[nacc:]
        ids = [pl.program_id(q) for q in range(3)]
        first = functools.reduce(jnp.logical_and, [i == 0 for i in ids])
        last = functools.reduce(jnp.logical_and, [i == n - 1 for i, n in zip(ids, grid)])
        pl.when(first)(lambda: ride.start(rin, rout, *sems))
        compute(a_ref, b_ref, tuple(x_refs) + tuple(o_refs) + tuple(tail[:nacc]))
        pl.when(last)(lambda: ride.finish(rin, rout, *sems))

    hbm = pl.BlockSpec(memory_space=pl.ANY)
    res = pl.pallas_call(
        body, grid=grid, in_specs=[a_spec, b_spec] + [extra_spec(x) for x in extras] + [hbm] * r_in,
        out_specs=[tile(w) for w, _ in outs] + [hbm] * r_out,
        out_shape=[jax.ShapeDtypeStruct((M, nj * w), dt) for w, dt in outs] + (list(ride.out_shapes) if ride else []),
        scratch_shapes=([pltpu.VMEM((tm, tn), F32)] if nk > 1 else []) + (list(ride.sems) if ride else []),
        name=name, compiler_params=_cparams(3),
    )(a, b, *[x[0] for x in extras], *(ride.arrays if ride else []))
    main = res[0] if epilogue is None else res[:nout]
    return (main, res[nout:]) if ride else main


class _Lay:
    def __init__(self, bl, seq, clen, d, tt=None):
        self.bl, self.seq, self.clen, self.d = bl, seq, clen, d
        self.tt = min(ROW_TILE, math.gcd(seq, bl * clen)) if tt is None else tt
        assert seq % self.tt == 0 and (bl * clen) % self.tt == 0 and self.tt % 8 == 0
        self.spb = seq // self.tt
        self.spc = clen // self.tt
        self.nsx = bl * self.spb
        self.nsc = bl * clen // self.tt
        self.ns = self.nsx + self.nsc
        self.tx = bl * seq
        self.ta = self.tx + bl * clen

    def fine(self):
        return _Lay(self.bl, self.seq, self.clen, self.d, min(FINE_ROW_TILE, self.clen))

    def mrow(self, s):
        return jnp.where(s < self.nsx, s // self.spb, self.bl)

    def first_of_row(self, s):
        return jnp.logical_or(jnp.logical_and(s < self.nsx, s % self.spb == 0), s == self.nsx)

    def seq_first(self, s):
        return jnp.where(s < self.nsx, s % self.spb == 0, (s - self.nsx) % self.spc == 0)

    def seq_last(self, s):
        return jnp.where(s < self.nsx, s % self.spb == self.spb - 1, (s - self.nsx) % self.spc == self.spc - 1)


def _tok(lay, c, cb=0, clamp=None):
    if clamp is None:
        return pl.BlockSpec((lay.tt, c), lambda j, s: (s, cb + j))
    return pl.BlockSpec((lay.tt, c), lambda j, s: (jnp.minimum(s, clamp), cb + j))


def _halo_prev(lay, c, cb=0):
    u = lay.tt // HALO
    return pl.BlockSpec((HALO, c), lambda j, s: (jnp.maximum(s * u - 1, 0), cb + j))


def _halo_next(lay, c, cb=0):
    u = lay.tt // HALO
    last = lay.ta // HALO - 1
    return pl.BlockSpec((HALO, c), lambda j, s: (jnp.minimum((s + 1) * u, last), cb + j))


def _row(lay, k, c):
    return pl.BlockSpec((None, k, c), lambda j, s: (lay.mrow(s), 0, 0))


def _glob(k, c, cb=None):
    if cb is None:
        return pl.BlockSpec((k, c), lambda j, s: (0, 0))
    return pl.BlockSpec((k, c), lambda j, s: (0, cb + j))


def _tok_call(name, body, ncb, nseg, in_specs, out_specs, out_shape, inputs, scratch=(), aliases=None):
    return pl.pallas_call(body, grid=(ncb, nseg), in_specs=in_specs, out_specs=out_specs, out_shape=out_shape,
                          scratch_shapes=list(scratch), name=name, compiler_params=_cparams(2),
                          input_output_aliases=aliases or {})(*inputs)


def _acc(ref, first, val):
    @pl.when(first)
    def _():
        ref[...] = val

    @pl.when(jnp.logical_not(first))
    def _():
        ref[...] += val


def _norm_mod_f(x, w, sh, sc):
    y = x * lax.rsqrt(jnp.mean(x * x, axis=-1, keepdims=True) + EPS) * w
    return y * (1.0 + sc) + sh


def _norm_mod_fwd(lay, nseg, x, w, modv, ksh, name):
    d = lay.d

    def body(x_ref, w_ref, m_ref, h_ref):
        h = _norm_mod_f(x_ref[...], w_ref[...], m_ref[ksh:ksh + 1, :], m_ref[ksh + 1:ksh + 2, :])
        h_ref[...] = h.astype(h_ref.dtype)

    return _tok_call(name, body, 1, nseg, [_tok(lay, d), _glob(1, d), _row(lay, N_MOD, d)], _tok(lay, d),
                     jax.ShapeDtypeStruct((nseg * lay.tt, d), BF16), (x, w, modv))


def _norm_mod_bwd(lay, nseg, nres, x, w, modv, ksh, dh, dres, name, nout=None):
    d = lay.d
    nrow = lay.bl + (1 if nseg > lay.nsx else 0)
    nout = nseg if nout is None else nout

    def body(x_ref, w_ref, m_ref, dh_ref, dres_ref, dx_ref, dw_ref, dm_ref):
        s = pl.program_id(1)
        _, vjp = jax.vjp(_norm_mod_f, x_ref[...], w_ref[...], m_ref[ksh:ksh + 1, :], m_ref[ksh + 1:ksh + 2, :])
        dx, dw, dsh, dsc = vjp(dh_ref[...])

        @pl.when(s < nout)
        def _():
            dx_ref[...] = dx + jnp.where(s < nres, dres_ref[...], 0.0)

        _acc(dw_ref, s == 0, dw)
        _acc(dm_ref, lay.first_of_row(s), jnp.concatenate([dsh, dsc], axis=0))

    return _tok_call(
        name, body, 1, nseg,
        [_tok(lay, d), _glob(1, d), _row(lay, N_MOD, d), _tok(lay, d), _tok(lay, d, clamp=nres - 1)],
        [_tok(lay, d, clamp=nout - 1), _glob(1, d), _row(lay, 2, d)],
        [jax.ShapeDtypeStruct((nout * lay.tt, d), F32), jax.ShapeDtypeStruct((1, d), F32),
         jax.ShapeDtypeStruct((nrow, 2, d), F32)],
        (x, w, modv, dh, dres))


def _mm_resid(lay, a, b, x, modv, kg, coef, name, then_norm=None):
    d = lay.d
    tm = min(ROW_TILE, math.gcd(lay.seq, lay.bl * lay.clen))
    assert a.shape[0] % tm == 0 and b.shape[1] == d

    def row_of(i):
        return jnp.where(i * tm < lay.tx, (i * tm) // lay.seq, lay.bl)

    def add(acc, x_tile, m_blk, *nw):
        y = x_tile + (coef * m_blk[kg:kg + 1, :]) * acc
        if then_norm is None:
            return acc, y
        k = then_norm[1]
        return acc, y, _norm_mod_f(y, nw[0], m_blk[k:k + 1, :], m_blk[k + 1:k + 2, :])

    extras = [(x, d), (modv, row_of, "rows")]
    outs = [(d, F32), (d, F32)]
    if then_norm is not None:
        extras.append((then_norm[0].reshape(1, 1, d), lambda i: 0, "rows"))
        outs.append((d, BF16))
    return _mm(a, b, "nn", None, name, tm=tm, tn=d, extras=extras, epilogue=add, outs=outs)


def _resid_bwd(lay, nseg, dy, o, modv, kg, coef, name):
    d = lay.d
    nrow = lay.bl + (1 if nseg > lay.nsx else 0)

    def body(dy_ref, o_ref, m_ref, do_ref, dg_ref):
        s = pl.program_id(1)
        dy = dy_ref[...]
        do_ref[...] = (dy * (coef * m_ref[kg:kg + 1, :])).astype(do_ref.dtype)
        _acc(dg_ref, lay.first_of_row(s), jnp.sum(dy * o_ref[...], axis=0, keepdims=True) * coef)

    return _tok_call(name, body, 1, nseg, [_tok(lay, d), _tok(lay, d), _row(lay, N_MOD, d)],
                     [_tok(lay, d), _row(lay, 1, d)],
                     [jax.ShapeDtypeStruct((nseg * lay.tt, d), BF16), jax.ShapeDtypeStruct((nrow, 1, d), F32)],
                     (dy, o, modv))


def _final_loss(lay, x, wf, target, name):
    d = lay.d

    def body(x_ref, w_ref, t_ref, loss_ref, dx_ref, dw_ref):
        s = pl.program_id(1)

        def f(xv, wv):
            return xv * lax.rsqrt(jnp.mean(xv * xv, axis=-1, keepdims=True) + EPS) * wv

        y, vjp = jax.vjp(f, x_ref[...], w_ref[...])
        err = y - t_ref[...]
        part = 0.5 * jnp.sum(jnp.sum(err * err, axis=-1, keepdims=True), axis=0, keepdims=True) / d
        dx, dw = vjp(err / d)
        dx_ref[...] = dx
        _acc(loss_ref, s == 0, part)
        _acc(dw_ref, s == 0, dw)

    return _tok_call(name, body, 1, lay.nsx, [_tok(lay, d), _glob(1, d), _tok(lay, d)],
                     [_glob(1, 1), _tok(lay, d), _glob(1, d)],
                     [jax.ShapeDtypeStruct((1, 1), F32), jax.ShapeDtypeStruct((lay.tx, d), F32),
                      jax.ShapeDtypeStruct((1, d), F32)], (x, wf, target))


class _Mix:
    def __init__(self, d, heads):
        self.d_ssm = d
        self.d_conv = d
        self.heads = heads
        assert heads * HEAD_DIM == d and heads % (2 * SSD_GROUPS) == 0 and 2 * heads <= DT_LANES
        self.gn = SSD_GROUPS * N_STATE
        self.xw = d + 2 * self.gn
        self.off_x = 0
        self.off_dt = self.xw
        self.off_glu = self.xw + DT_PAD
        self.off_z = self.off_glu + 2 * d
        self.pw = self.off_z + d
        assert self.off_z % d == 0
        self.ref_x = d
        self.ref_dt = d + self.xw
        self.ref_glu = self.ref_dt + 2 * heads
        self.cc = self.xw if self.off_x % self.xw == 0 else _pick(self.xw, (512, 256, 128))


def _conv5_fwd(lay, mx, proj, cw, cb, name):
    c, tt = mx.cc, lay.tt
    cb0 = mx.off_x // c
    assert mx.off_x % c == 0

    def body(prev_ref, cur_ref, next_ref, w_ref, b_ref, pre_ref, act_ref, ext_ref):
        s = pl.program_id(1)
        ext_ref[0:HALO, :] = jnp.where(lay.seq_first(s), 0.0, prev_ref[...])
        ext_ref[HALO:HALO + tt, :] = cur_ref[...]
        ext_ref[HALO + tt:, :] = jnp.where(lay.seq_last(s), 0.0, next_ref[...])
        acc = jnp.zeros((tt, c), F32) + b_ref[...]
        for k in range(5):
            acc = acc + w_ref[k:k + 1, :] * ext_ref[pl.ds(HALO + k - 2, tt), :]
        pre_ref[...] = acc
        act_ref[...] = _silu(acc)

    sh = jax.ShapeDtypeStruct((lay.ta, mx.xw), F32)
    return _tok_call(name, body, mx.xw // c, lay.ns,
                     [_halo_prev(lay, c, cb0), _tok(lay, c, cb0), _halo_next(lay, c, cb0), _glob(8, c, 0), _glob(1, c, 0)],
                     [_tok(lay, c), _tok(lay, c)], [sh, sh], (proj, proj, proj, cw, cb),
                     scratch=[pltpu.VMEM((tt + 2 * HALO, c), F32)])


def _conv5_bwd(lay, mx, proj, pre, dact_f, dact_b, cw, dproj, name):
    c, tt = mx.cc, lay.tt
    cb0 = mx.off_x // c

    def dsilu(p):
        sg = jax.nn.sigmoid(p)
        return sg * (1.0 + p * (1.0 - sg))

    def body(xp_ref, xc_ref, xn_ref, pp_ref, pc_ref, pn_ref, fp_ref, fc_ref, fn_ref, bp_ref, bc_ref, bn_ref, w_ref,
             buf_ref, dx_ref, dw_ref, db_ref, extx_ref, extd_ref):
        s = pl.program_id(1)
        first, last = lay.seq_first(s), lay.seq_last(s)
        dcur = (fc_ref[...] + bc_ref[...]) * dsilu(pc_ref[...])
        extd_ref[0:HALO, :] = jnp.where(first, 0.0, (fp_ref[...] + bp_ref[...]) * dsilu(pp_ref[...]))
        extd_ref[HALO:HALO + tt, :] = dcur
        extd_ref[HALO + tt:, :] = jnp.where(last, 0.0, (fn_ref[...] + bn_ref[...]) * dsilu(pn_ref[...]))
        extx_ref[0:HALO, :] = jnp.where(first, 0.0, xp_ref[...])
        extx_ref[HALO:HALO + tt, :] = xc_ref[...]
        extx_ref[HALO + tt:, :] = jnp.where(last, 0.0, xn_ref[...])
        dx = jnp.zeros((tt, c), F32)
        rows = []
        for k in range(5):
            dx = dx + w_ref[k:k + 1, :] * extd_ref[pl.ds(HALO - (k - 2), tt), :]
            rows.append(jnp.sum(dcur * extx_ref[pl.ds(HALO + k - 2, tt), :], axis=0, keepdims=True))
        dx_ref[...] = dx.astype(dx_ref.dtype)
        rows.append(jnp.zeros((3, c), F32))
        _acc(dw_ref, s == 0, jnp.concatenate(rows, axis=0))
        _acc(db_ref, s == 0, jnp.sum(dcur, axis=0, keepdims=True))

    three = lambda cbx: [_halo_prev(lay, c, cbx), _tok(lay, c, cbx), _halo_next(lay, c, cbx)]
    ext = pltpu.VMEM((tt + 2 * HALO, c), F32)
    return _tok_call(name, body, mx.xw // c, lay.ns,
                     three(cb0) + three(0) + three(0) + three(0) + [_glob(8, c, 0), pl.BlockSpec(memory_space=pl.ANY)],
                     [_tok(lay, c, cb0), _glob(8, c, 0), _glob(1, c, 0)],
                     [jax.ShapeDtypeStruct(dproj.shape, dproj.dtype), jax.ShapeDtypeStruct((8, mx.xw), F32),
                      jax.ShapeDtypeStruct((1, mx.xw), F32)],
                     (proj, proj, proj, pre, pre, pre, dact_f, dact_f, dact_f, dact_b, dact_b, dact_b, cw, dproj),
                     scratch=[ext, ext], aliases={13: 0})


def _softplus(v):
    return jnp.maximum(v, 0.0) + jnp.log1p(jnp.exp(-jnp.abs(v)))


def _dt_fwd(lay, mx, proj, bias, name):
    cb = mx.off_dt // DT_LANES

    def body(p_ref, b_ref, dt_ref):
        dt_ref[...] = _softplus(p_ref[...] + b_ref[...])

    return _tok_call(name, body, 1, lay.ns, [_tok(lay, DT_LANES, cb), _glob(1, DT_LANES)], _tok(lay, DT_LANES),
                     jax.ShapeDtypeStruct((lay.ta, DT_LANES), F32), (proj, bias))


def _dt_bwd(lay, mx, proj, bias, parts, dproj, name):
    cb = mx.off_dt // DT_LANES
    wb = DT_PAD if mx.off_dt % DT_PAD == 0 else DT_LANES
    ncb = DT_PAD // wb

    def body(p_ref, b_ref, a_ref, b2_ref, c_ref, d_ref, buf_ref, dp_ref, db_ref):
        j, s = pl.program_id(0), pl.program_id(1)

        @pl.when(j == 0)
        def _():
            ddt = (a_ref[...] + b2_ref[...]) + (c_ref[...] + d_ref[...])
            draw = ddt * jax.nn.sigmoid(p_ref[...] + b_ref[...])
            dp_ref[:, 0:DT_LANES] = draw.astype(dp_ref.dtype)
            if wb > DT_LANES:
                dp_ref[:, DT_LANES:] = jnp.zeros((lay.tt, wb - DT_LANES), dp_ref.dtype)
            _acc(db_ref, s == 0, jnp.sum(draw, axis=0, keepdims=True))

        @pl.when(j > 0)
        def _():
            dp_ref[...] = jnp.zeros_like(dp_ref)

    t = pl.BlockSpec((lay.tt, DT_LANES), lambda j, s: (s, 0))
    return _tok_call(name, body, ncb, lay.ns,
                     [pl.BlockSpec((lay.tt, DT_LANES), lambda j, s: (s, cb)), _glob(1, DT_LANES), t, t, t, t,
                      pl.BlockSpec(memory_space=pl.ANY)],
                     [_tok(lay, wb, mx.off_dt // wb), _glob(1, DT_LANES)],
                     [jax.ShapeDtypeStruct(dproj.shape, dproj.dtype), jax.ShapeDtypeStruct((1, DT_LANES), F32)],
                     (proj, bias) + tuple(parts) + (dproj,), aliases={6: 0})


def _scan_mask(rev):
    r = lax.broadcasted_iota(jnp.int32, (CHUNK, CHUNK), 0)
    c = lax.broadcasted_iota(jnp.int32, (CHUNK, CHUNK), 1)
    return (r <= c) if rev else (r >= c)


def _split_bf16(x):
    hi = x.astype(BF16)
    return hi, (x - hi.astype(F32)).astype(BF16)


@functools.partial(jax.custom_vjp, nondiff_argnums=(0,))
def _cum_cols(rev, x):
    m = _scan_mask(rev).astype(BF16)
    hi, lo = _split_bf16(x)
    return jnp.dot(m, hi, preferred_element_type=F32) + jnp.dot(m, lo, preferred_element_type=F32)


_cum_cols.defvjp(lambda rev, x: (_cum_cols(rev, x), None), lambda rev, _, g: (_cum_cols(not rev, g),))


@functools.partial(jax.custom_vjp, nondiff_argnums=(0,))
def _cum_rows(rev, x):
    m = _scan_mask(not rev).astype(BF16)
    hi, lo = _split_bf16(x)
    return jnp.dot(hi, m, preferred_element_type=F32) + jnp.dot(lo, m, preferred_element_type=F32)


_cum_rows.defvjp(lambda rev, x: (_cum_rows(rev, x), None), lambda rev, _, g: (_cum_rows(not rev, g),))


def _ssd_chunk(xh_pairs, bcs, ccs, dtc, dtr, a_row, a_col, st_pairs, *, rev, heads, col0):
    cs_c, cs_r, tot, scores = _ssd_shared(bcs, ccs, dtc, dtr, a_row, a_col, rev=rev)
    ppg = heads // (2 * SSD_GROUPS)
    ys, sts = [], []
    for g in range(SSD_GROUPS):
        y, st = _ssd_group(xh_pairs[g * ppg:(g + 1) * ppg], bcs[g], ccs[g], st_pairs[g], cs_c, cs_r, tot, dtc,
                           scores[g], rev=rev, col=col0 + 2 * ppg * g)
        ys.append(y)
        sts.append(st)
    return ys, sts


_NT = (((1,), (1,)), ((), ()))
_TN = (((0,), (0,)), ((), ()))


def _ssd_shared(bcs, ccs, dtc, dtr, a_row, a_col, *, rev):
    da_c = dtc * a_row
    cs_c = _cum_cols(rev, da_c)
    cs_r = _cum_rows(rev, dtr * a_col)
    tot = jnp.sum(da_c, axis=0, keepdims=True)
    scores = [lax.dot_general(ccs[g].astype(BF16), bcs[g].astype(BF16), _NT, preferred_element_type=F32)
              for g in range(SSD_GROUPS)]
    return cs_c, cs_r, tot, scores


def _ssd_group(xh_pairs, bc, cc, st, cs_c, cs_r, tot, dtc, score, *, rev, col):
    n = CHUNK
    mask = _scan_mask(rev)
    lane = lax.broadcasted_iota(jnp.int32, (n, DT_LANES), 1)
    sub = lax.broadcasted_iota(jnp.int32, (DT_LANES, n), 0)
    lane1 = lax.broadcasted_iota(jnp.int32, (1, DT_LANES), 1)
    left = lax.broadcasted_iota(jnp.int32, (n, 2 * HEAD_DIM), 1) < HEAD_DIM
    top = lax.broadcasted_iota(jnp.int32, (2 * HEAD_DIM, 1), 0) < HEAD_DIM
    xs_all, wst_all, ecs_all, edec_all, y_diag = [], [], [], [], []
    for p, xh in enumerate(xh_pairs):
        per = []
        for c in (col + 2 * p, col + 2 * p + 1):
            csv = jnp.sum(jnp.where(lane == c, cs_c, 0.0), axis=1, keepdims=True)
            csr = jnp.sum(jnp.where(sub == c, cs_r, 0.0), axis=0, keepdims=True)
            dtv = jnp.sum(jnp.where(lane == c, dtc, 0.0), axis=1, keepdims=True)
            tv = jnp.sum(jnp.where(lane1 == c, tot, 0.0), axis=1, keepdims=True)
            m = score * jnp.exp(jnp.where(mask, csv - csr, NEG_BIG))
            per.append((csv, dtv, tv, m))
        (cs1, dt1, t1, m1), (cs2, dt2, t2, m2) = per
        xs = xh * jnp.where(left, dt1, dt2)
        both = jnp.dot(jnp.concatenate([m1, m2], axis=0).astype(BF16), xs.astype(BF16), preferred_element_type=F32)
        y_diag.append(jnp.where(left, both[:n], both[n:]))
        xs_all.append(xs)
        ecs_all.append(jnp.where(left, jnp.exp(cs1), jnp.exp(cs2)))
        wst_all.append(jnp.where(left, jnp.exp(t1 - cs1), jnp.exp(t2 - cs2)))
        edec_all.append(jnp.where(top, jnp.exp(t1), jnp.exp(t2)))
    cat = lambda parts, axis: parts[0] if len(parts) == 1 else jnp.concatenate(parts, axis=axis)
    xs, wst, ecs = cat(xs_all, 1), cat(wst_all, 1), cat(ecs_all, 1)
    y_off = lax.dot_general(cc.astype(BF16), st.astype(BF16), _NT, preferred_element_type=F32) * ecs
    cst = lax.dot_general((xs * wst).astype(BF16), bc.astype(BF16), _TN, preferred_element_type=F32)
    return cat(y_diag, 1) + y_off, st * cat(edec_all, 0) + cst


class _Scan:
    def __init__(self, lay, rev):
        self.ncx, self.ncc, self.bl, self.rev = lay.seq // CHUNK, lay.clen // CHUNK, lay.bl, rev
        self.nct = self.ncx + self.ncc

    def chunk(self, b, pos):
        kc = (self.ncc - 1 - pos) if self.rev else pos
        kx = (self.ncx - 1 - (pos - self.ncc)) if self.rev else (pos - self.ncc)
        return jnp.where(pos < self.ncc, self.bl * self.ncx + b * self.ncc + kc, b * self.ncx + kx)


def _ssd_io(mx, x_ref, st_src):
    np_ = mx.heads // 2
    d = mx.d_ssm
    xh = [x_ref[:, 128 * p:128 * (p + 1)] for p in range(np_)]
    bcs = [x_ref[:, d + N_STATE * g:d + N_STATE * (g + 1)] for g in range(SSD_GROUPS)]
    ccs = [x_ref[:, d + mx.gn + N_STATE * g:d + mx.gn + N_STATE * (g + 1)] for g in range(SSD_GROUPS)]
    gw = d // SSD_GROUPS
    sts = [st_src[gw * g:gw * (g + 1), :] for g in range(SSD_GROUPS)]
    return xh, bcs, ccs, sts


def _ssd_fwd(lay, mx, xbc, dt, dtt, a_row, a_col, rev, name):
    sc = _Scan(lay, rev)
    col0 = mx.heads if rev else 0
    hp = mx.heads * HEAD_DIM

    def body(x_ref, dt_ref, dtt_ref, ar_ref, ac_ref, y_ref, hp_ref, st_ref):
        @pl.when(pl.program_id(1) == 0)
        def _():
            st_ref[...] = jnp.zeros_like(st_ref)

        hp_ref[...] = st_ref[...]
        xh, bcs, ccs, sts = _ssd_io(mx, x_ref, st_ref)
        ys, new = _ssd_chunk(xh, bcs, ccs, dt_ref[...], dtt_ref[...], ar_ref[...], ac_ref[...], sts,
                             rev=rev, heads=mx.heads, col0=col0)
        gw = mx.d_ssm // SSD_GROUPS
        for g in range(SSD_GROUPS):
            y_ref[:, gw * g:gw * (g + 1)] = ys[g]
            st_ref[gw * g:gw * (g + 1), :] = new[g]

    ch = sc.chunk
    return pl.pallas_call(
        body, grid=(lay.bl, sc.nct),
        in_specs=[pl.BlockSpec((CHUNK, mx.xw), lambda b, i: (ch(b, i), 0)),
                  pl.BlockSpec((CHUNK, DT_LANES), lambda b, i: (ch(b, i), 0)),
                  pl.BlockSpec((DT_LANES, CHUNK), lambda b, i: (0, ch(b, i))),
                  pl.BlockSpec((1, DT_LANES), lambda b, i: (0, 0)),
                  pl.BlockSpec((DT_LANES, 1), lambda b, i: (0, 0))],
        out_specs=[pl.BlockSpec((CHUNK, mx.d_ssm), lambda b, i: (ch(b, i), 0)),
                   pl.BlockSpec((hp, N_STATE), lambda b, i: (b * sc.nct + i, 0))],
        out_shape=[jax.ShapeDtypeStruct((lay.ta, mx.d_ssm), F32),
                   jax.ShapeDtypeStruct((lay.bl * sc.nct * hp, N_STATE), F32)],
        scratch_shapes=[pltpu.VMEM((hp, N_STATE), F32)], name=name, compiler_params=_cparams(2),
    )(xbc, dt, dtt, a_row, a_col)


def _ssd_bwd(lay, mx, xbc, dt, dtt, a_row, a_col, hprev, dy, dskip, rev, name):
    sc = _Scan(lay, rev)
    col0 = mx.heads if rev else 0
    hp = mx.heads * HEAD_DIM
    np_ = mx.heads // 2
    d = mx.d_ssm
    with_skip = dskip is not None

    def body(*refs):
        if with_skip:
            x_ref, dt_ref, dtt_ref, ar_ref, ac_ref, hp_ref, dy_ref, sk_ref = refs[:8]
            rest = refs[8:]
        else:
            x_ref, dt_ref, dtt_ref, ar_ref, ac_ref, hp_ref, dy_ref = refs[:7]
            rest = refs[7:]
        dx_ref, ddc_ref, ddr_ref, dar_ref, dac_ref, ds_ref = rest
        b, i = pl.program_id(0), pl.program_id(1)

        @pl.when(i == 0)
        def _():
            ds_ref[...] = jnp.zeros_like(ds_ref)

        xh, bcs, ccs, sts = _ssd_io(mx, x_ref, hp_ref)
        dtc = dt_ref[...]
        shared, vjp_shared = jax.vjp(functools.partial(_ssd_shared, rev=rev), bcs, ccs, dtc, dtt_ref[...],
                                     ar_ref[...], ac_ref[...])
        cs_c, cs_r, tot, scores = shared
        plus = lambda acc, v: v if acc is None else acc + v
        d_cs_c = d_cs_r = d_tot = ddc = None
        d_scores, dbc, dcc = [], [], []
        ppg = np_ // SSD_GROUPS
        gw = d // SSD_GROUPS
        for g in range(SSD_GROUPS):
            dyg = dy_ref[:, gw * g:gw * (g + 1)]
            fn = functools.partial(_ssd_group, rev=rev, col=col0 + 2 * ppg * g)
            _, vjp = jax.vjp(fn, xh[g * ppg:(g + 1) * ppg], bcs[g], ccs[g], sts[g], cs_c, cs_r, tot, dtc, scores[g])
            dxh, dbc_g, dcc_g, dst, dcs_c_g, dcs_r_g, dtot_g, ddc_g, dsc_g = vjp((dyg, ds_ref[gw * g:gw * (g + 1), :]))
            for q in range(ppg):
                p = g * ppg + q
                v = dxh[q]
                if with_skip:
                    v = v + dyg[:, 128 * q:128 * (q + 1)] * sk_ref[:, 128 * p:128 * (p + 1)]
                dx_ref[:, 128 * p:128 * (p + 1)] = v
            ds_ref[gw * g:gw * (g + 1), :] = dst
            d_cs_c, d_cs_r, d_tot, ddc = plus(d_cs_c, dcs_c_g), plus(d_cs_r, dcs_r_g), plus(d_tot, dtot_g), plus(ddc, ddc_g)
            d_scores.append(dsc_g)
            dbc.append(dbc_g)
            dcc.append(dcc_g)
        dbc_s, dcc_s, ddc_s, ddr, dar, dac = vjp_shared((d_cs_c, d_cs_r, d_tot, d_scores))
        ddc = ddc + ddc_s
        dbc = [dbc[g] + dbc_s[g] for g in range(SSD_GROUPS)]
        dcc = [dcc[g] + dcc_s[g] for g in range(SSD_GROUPS)]
        for g in range(SSD_GROUPS):
            dx_ref[:, d + N_STATE * g:d + N_STATE * (g + 1)] = dbc[g]
            dx_ref[:, d + mx.gn + N_STATE * g:d + mx.gn + N_STATE * (g + 1)] = dcc[g]
        ddc_ref[...] = ddc
        ddr_ref[...] = ddr
        first = jnp.logical_and(b == 0, i == 0)
        _acc(dar_ref, first, dar)
        _acc(dac_ref, first, dac)

    ch = lambda b, i: sc.chunk(b, sc.nct - 1 - i)
    in_specs = [pl.BlockSpec((CHUNK, mx.xw), lambda b, i: (ch(b, i), 0)),
                pl.BlockSpec((CHUNK, DT_LANES), lambda b, i: (ch(b, i), 0)),
                pl.BlockSpec((DT_LANES, CHUNK), lambda b, i: (0, ch(b, i))),
                pl.BlockSpec((1, DT_LANES), lambda b, i: (0, 0)),
                pl.BlockSpec((DT_LANES, 1), lambda b, i: (0, 0)),
                pl.BlockSpec((hp, N_STATE), lambda b, i: (b * sc.nct + sc.nct - 1 - i, 0)),
                pl.BlockSpec((CHUNK, d), lambda b, i: (ch(b, i), 0))]
    inputs = [xbc, dt, dtt, a_row, a_col, hprev, dy]
    if with_skip:
        in_specs.append(pl.BlockSpec((1, d), lambda b, i: (0, 0)))
        inputs.append(dskip)
    return pl.pallas_call(
        body, grid=(lay.bl, sc.nct), in_specs=in_specs,
        out_specs=[pl.BlockSpec((CHUNK, mx.xw), lambda b, i: (ch(b, i), 0)),
                   pl.BlockSpec((CHUNK, DT_LANES), lambda b, i: (ch(b, i), 0)),
                   pl.BlockSpec((DT_LANES, CHUNK), lambda b, i: (0, ch(b, i))),
                   pl.BlockSpec((1, DT_LANES), lambda b, i: (0, 0)),
                   pl.BlockSpec((DT_LANES, 1), lambda b, i: (0, 0))],
        out_shape=[jax.ShapeDtypeStruct((lay.ta, mx.xw), F32), jax.ShapeDtypeStruct((lay.ta, DT_LANES), F32),
                   jax.ShapeDtypeStruct((DT_LANES, lay.ta), F32), jax.ShapeDtypeStruct((1, DT_LANES), F32),
                   jax.ShapeDtypeStruct((DT_LANES, 1), F32)],
        scratch_shapes=[pltpu.VMEM((hp, N_STATE), F32)], name=name, compiler_params=_cparams(2),
    )(*inputs)


def _gate_f(yf, yb, xh, z, drow, nw):
    dd = yf.shape[-1]
    half = dd // SSD_GROUPS
    yz = (yf + yb + drow * xh) * _silu(z)
    lo = lax.broadcasted_iota(jnp.int32, yz.shape, 1) < half
    sq = yz * yz
    ms1 = jnp.sum(jnp.where(lo, sq, 0.0), axis=-1, keepdims=True) / half
    ms2 = jnp.sum(jnp.where(lo, 0.0, sq), axis=-1, keepdims=True) / half
    return yz * jnp.where(lo, lax.rsqrt(ms1 + EPS), lax.rsqrt(ms2 + EPS)) * nw


def _gate_fwd(lay, mx, yf, yb, xbc, proj, drow, nw, name):
    d = mx.d_ssm

    def body(yf_ref, yb_ref, xh_ref, z_ref, d_ref, w_ref, o_ref):
        o_ref[...] = _gate_f(yf_ref[...], yb_ref[...], xh_ref[...], z_ref[...], d_ref[...], w_ref[...]).astype(o_ref.dtype)

    t, z = _tok(lay, d), _tok(lay, d, mx.off_z // d)
    return _tok_call(name, body, 1, lay.nsx, [t, t, t, z, _glob(1, d), _glob(1, d)], t,
                     jax.ShapeDtypeStruct((lay.tx, d + mx.d_conv), BF16), (yf, yb, xbc, proj, drow, nw))


def _gate_bwd(lay, mx, yf, yb, xbc, proj, drow, nw, dcat, name):
    d = mx.d_ssm
    nsx = lay.nsx

    def body(yf_ref, yb_ref, xh_ref, z_ref, d_ref, w_ref, dc_ref, dy_ref, dz_ref, dd_ref, dw_ref):
        s = pl.program_id(1)

        @pl.when(s < nsx)
        def _():
            _, vjp = jax.vjp(_gate_f, yf_ref[...], yb_ref[...], xh_ref[...], z_ref[...], d_ref[...], w_ref[...])
            dyf, _, _, dz, dd, dw = vjp(dc_ref[...])
            dy_ref[...] = dyf
            dz_ref[...] = dz.astype(dz_ref.dtype)
            _acc(dd_ref, s == 0, dd)
            _acc(dw_ref, s == 0, dw)

        @pl.when(s >= nsx)
        def _():
            dy_ref[...] = jnp.zeros_like(dy_ref)
            dz_ref[...] = jnp.zeros_like(dz_ref)

    t, z = _tok(lay, d), _tok(lay, d, mx.off_z // d)
    return _tok_call(name, body, 1, lay.ns, [t, t, t, z, _glob(1, d), _glob(1, d), _tok(lay, d, clamp=nsx - 1)],
                     [t, z, _glob(1, d), _glob(1, d)],
                     [jax.ShapeDtypeStruct((lay.ta, d), F32), jax.ShapeDtypeStruct((lay.ta, mx.pw), BF16),
                      jax.ShapeDtypeStruct((1, d), F32), jax.ShapeDtypeStruct((1, d), F32)],
                     (yf, yb, xbc, proj, drow, nw, dcat))


def _glu_fwd(lay, mx, proj, name):
    d = mx.d_conv
    c = math.gcd(mx.off_glu, d)
    cb = mx.off_glu // c

    def body(a_ref, b_ref, o_ref):
        o_ref[...] = a_ref[...] * jax.nn.sigmoid(b_ref[...])

    return _tok_call(name, body, d // c, lay.nsx, [_tok(lay, c, cb), _tok(lay, c, cb + d // c)], _tok(lay, c),
                     jax.ShapeDtypeStruct((lay.tx, d), F32), (proj, proj))


def _glu_bwd(lay, mx, proj, du, dproj, name):
    d = mx.d_conv
    nsx = lay.nsx
    whole = mx.off_glu % (2 * d) == 0
    c = d if whole else math.gcd(mx.off_glu, d)
    cb = mx.off_glu // c
    nc = d // c

    def body(a_ref, b_ref, du_ref, buf_ref, o_ref):
        j, s = pl.program_id(0), pl.program_id(1)

        @pl.when(s < nsx)
        def _():
            sg = jax.nn.sigmoid(b_ref[...])
            da = du_ref[...] * sg
            db = da * a_ref[...] * (1.0 - sg)
            if whole:
                o_ref[:, 0:d] = da.astype(o_ref.dtype)
                o_ref[:, d:] = db.astype(o_ref.dtype)
            else:
                o_ref[...] = jnp.where(j < nc, da, db).astype(o_ref.dtype)

        @pl.when(s >= nsx)
        def _():
            o_ref[...] = jnp.zeros_like(o_ref)

    win = lambda half: pl.BlockSpec((lay.tt, c), lambda j, s: (s, cb + half * nc + j % nc))
    out = _tok(lay, 2 * d, mx.off_glu // (2 * d)) if whole else _tok(lay, c, cb)
    return _tok_call(name, body, 1 if whole else 2 * nc, lay.ns,
                     [win(0), win(1), pl.BlockSpec((lay.tt, c), lambda j, s: (jnp.minimum(s, nsx - 1), j % nc)),
                      pl.BlockSpec(memory_space=pl.ANY)],
                     out, jax.ShapeDtypeStruct(dproj.shape, dproj.dtype), (proj, proj, du, dproj), aliases={3: 0})


def _axial(lay, mx, u, dy, cw, cb, name):
    d, seq = mx.d_conv, lay.seq
    kw = cw.shape[0]
    pad = kw // 2
    c = _pick(d // 2, (256, 128))
    ncb = d // c
    zpad = GRID_W * pad
    zpad = -(-zpad // 8) * 8
    backward = dy is not None

    def shifted(ext_ref, off):
        return ext_ref[pl.ds(zpad + off, seq), :]

    def valid_row(off):
        col = lax.broadcasted_iota(jnp.int32, (seq, c), 0) % GRID_W
        return jnp.logical_and(col + off >= 0, col + off < GRID_W)

    def fill(ext_ref, v):
        ext_ref[0:zpad, :] = jnp.zeros((zpad, c), F32)
        ext_ref[zpad:zpad + seq, :] = v
        ext_ref[zpad + seq:, :] = jnp.zeros((zpad, c), F32)

    def conv(ext_ref, w_ref, is_row, sign):
        acc = jnp.zeros((seq, c), F32)
        for k in range(kw):
            off = sign * ((k - pad) if is_row else GRID_W * (k - pad))
            v = shifted(ext_ref, off)
            if is_row:
                v = jnp.where(valid_row(off), v, 0.0)
            acc = acc + w_ref[k:k + 1, :] * v
        return acc

    def fwd_body(u_ref, w_ref, b_ref, o_ref, ext_ref):
        j = pl.program_id(0)
        fill(ext_ref, u_ref[...])

        @pl.when(j < ncb // 2)
        def _():
            o_ref[...] = conv(ext_ref, w_ref, True, 1) + b_ref[...]

        @pl.when(j >= ncb // 2)
        def _():
            o_ref[...] = conv(ext_ref, w_ref, False, 1) + b_ref[...]

    def bwd_body(u_ref, dy_ref, w_ref, du_ref, dw_ref, db_ref, extu_ref, extd_ref):
        j, b = pl.program_id(0), pl.program_id(1)
        dyv = dy_ref[...]
        fill(extu_ref, u_ref[...])
        fill(extd_ref, dyv)

        def grads(is_row):
            du_ref[...] = conv(extd_ref, w_ref, is_row, -1)
            rows = []
            for k in range(kw):
                off = (k - pad) if is_row else GRID_W * (k - pad)
                v = shifted(extu_ref, off)
                if is_row:
                    v = jnp.where(valid_row(off), v, 0.0)
                rows.append(jnp.sum(dyv * v, axis=0, keepdims=True))
            _acc(dw_ref, b == 0, jnp.concatenate(rows, axis=0))

        @pl.when(j < ncb // 2)
        def _():
            grads(True)

        @pl.when(j >= ncb // 2)
        def _():
            grads(False)

        _acc(db_ref, b == 0, jnp.sum(dyv, axis=0, keepdims=True))

    seq_spec = pl.BlockSpec((seq, c), lambda j, b: (b, j))
    w_spec = pl.BlockSpec((kw, c), lambda j, b: (0, j))
    b_spec = pl.BlockSpec((1, c), lambda j, b: (0, j))
    ext = pltpu.VMEM((seq + 2 * zpad, c), F32)
    if not backward:
        return pl.pallas_call(fwd_body, grid=(ncb, lay.bl), in_specs=[seq_spec, w_spec, b_spec], out_specs=seq_spec,
                              out_shape=jax.ShapeDtypeStruct((lay.tx, d), F32), scratch_shapes=[ext], name=name,
                              compiler_params=_cparams(2))(u, cw, cb)
    return pl.pallas_call(bwd_body, grid=(ncb, lay.bl), in_specs=[seq_spec, seq_spec, w_spec],
                          out_specs=[seq_spec, w_spec, b_spec],
                          out_shape=[jax.ShapeDtypeStruct((lay.tx, d), F32), jax.ShapeDtypeStruct((kw, d), F32),
                                     jax.ShapeDtypeStruct((1, d), F32)],
                          scratch_shapes=[ext, ext], name=name, compiler_params=_cparams(2))(u, dy, cw)


def _ln_silu_f(u, w, b):
    mu = jnp.mean(u, axis=-1, keepdims=True)
    var = jnp.mean(jnp.square(u - mu), axis=-1, keepdims=True)
    return _silu((u - mu) * lax.rsqrt(var + EPS) * w + b)


def _ln_fwd(lay, mx, u, w, b, cat, name):
    d = mx.d_conv
    assert mx.d_ssm % d == 0

    def body(u_ref, w_ref, b_ref, cat_ref, o_ref):
        o_ref[...] = _ln_silu_f(u_ref[...], w_ref[...], b_ref[...]).astype(o_ref.dtype)

    return _tok_call(name, body, 1, lay.nsx,
                     [_tok(lay, d), _glob(1, d), _glob(1, d), pl.BlockSpec(memory_space=pl.ANY)],
                     _tok(lay, d, mx.d_ssm // d), jax.ShapeDtypeStruct(cat.shape, cat.dtype), (u, w, b, cat),
                     aliases={3: 0})


def _ln_bwd(lay, mx, u, w, b, dcat, name):
    d = mx.d_conv

    def body(u_ref, w_ref, b_ref, dc_ref, du_ref, dw_ref, db_ref):
        s = pl.program_id(1)
        _, vjp = jax.vjp(_ln_silu_f, u_ref[...], w_ref[...], b_ref[...])
        du, dw, db = vjp(dc_ref[...])
        du_ref[...] = du
        _acc(dw_ref, s == 0, dw)
        _acc(db_ref, s == 0, db)

    return _tok_call(name, body, 1, lay.nsx, [_tok(lay, d), _glob(1, d), _glob(1, d), _tok(lay, d, 1)],
                     [_tok(lay, d), _glob(1, d), _glob(1, d)],
                     [jax.ShapeDtypeStruct((lay.tx, d), F32), jax.ShapeDtypeStruct((1, d), F32),
                      jax.ShapeDtypeStruct((1, d), F32)], (u, w, b, dcat))


def _gate_tile(dff):
    return dff // 2 if (dff // 2) % 128 == 0 else dff


def _ffn_fwd(lay, nseg, x, nw, modv, k0, wts, tag, ride=None, h=None, then_norm=None):
    wgu, wd, ft = wts
    if h is None:
        h = _norm_mod_fwd(lay, nseg, x, nw, modv, k0, tag + "_norm")
    t = h.shape[0]

    def act(acc):
        g, u = acc[:, :ft], acc[:, ft:]
        sg = jax.nn.sigmoid(g)
        sl = g * sg
        return jnp.concatenate([u * (sg * (1.0 + g * (1.0 - sg))), sl], axis=1), sl * u

    res = _mm(h, wgu, "nn", None, tag + "_gu", tn=2 * ft, tm=256 if t % 256 == 0 else None,
              epilogue=act, outs=[(2 * ft, BF16), (ft, BF16)], ride=ride)
    (s, a), rode = res if ride else (res, None)
    o, y, *h_next = _mm_resid(lay, a, wd, x, modv, k0 + 2, 0.5, tag + "_down", then_norm=then_norm)
    return y, (x, h, s, a, o), rode, (h_next[0] if h_next else None)


def _ffn_bwd(lay, nseg, dy, saved, nw, modv, k0, wts, tag, nout=None, rides=(None, None)):
    wgu, wd, ft = wts
    x, h, s, a, o = saved
    do, dgate = _resid_bwd(lay, nseg, dy, o, modv, k0 + 2, 0.5, tag + "_dres")

    def through_act(da, s_tile):
        return (jnp.concatenate([da, da], axis=1) * s_tile.astype(F32),)

    (dgu,) = _mm(do, wd.T, "nn", None, tag + "_da", tn=ft, tm=_big_tile(do.shape[0], 1024), extras=[(s, 2 * ft)],
                 epilogue=through_act, outs=[(2 * ft, BF16)])
    dwd = _mm(a, do, "tn", F32, tag + "_dwd")
    dh = _mm(dgu, wgu, "nt", F32, tag + "_dh", ride=rides[0])
    dwgu = _mm(h, dgu, "tn", F32, tag + "_dwgu", ride=rides[1])
    (dh, rode_a), (dwgu, rode_b) = (dh if rides[0] else (dh, None)), (dwgu if rides[1] else (dwgu, None))
    dx, dnw, dss = _norm_mod_bwd(lay, nseg, nseg, x, nw, modv, k0, dh, dy, tag + "_dnorm", nout=nout)
    return dx, (dwgu, dwd), dnw, jnp.concatenate([dss, dgate], axis=1), (rode_a, rode_b)


def _local_step(lay, mx, xa, target, modv, w, exch=None):
    d, bl = lay.d, lay.bl
    g = {}
    xa1, ffn1, late, ha = _ffn_fwd(lay, lay.ns, xa, w["norm_ffn1"], modv, 0, w["ffn1"], "ffn1",
                                   ride=exch.late_weights if exch else None, then_norm=(w["norm_mix"], 3))
    if exch:
        w = {**w, **exch.unpack_late(late)}
    proj = _mm(ha, w["w_in"], "nn", F32, "mix_in")
    pre, xbc = _conv5_fwd(lay.fine(), mx, proj, w["conv_w"], w["conv_b"], "mix_conv")
    dt = _dt_fwd(lay, mx, proj, w["dt_bias"], "mix_dt")
    dtt = dt.T
    yf, hpf = _ssd_fwd(lay, mx, xbc, dt, dtt, w["a_row"], w["a_col"], False, "ssd_f")
    yb, hpb = _ssd_fwd(lay, mx, xbc, dt, dtt, w["a_row"], w["a_col"], True, "ssd_b")
    cat_y = _gate_fwd(lay, mx, yf, yb, xbc, proj, w["d_row"], w["ssm_norm_w"], "mix_gate")
    u0 = _glu_fwd(lay, mx, proj, "mix_glu")
    uc = _axial(lay, mx, u0, None, w["cconv_w"], w["cconv_b"], "mix_axial")
    cat = _ln_fwd(lay, mx, uc, w["ln_w"], w["ln_b"], cat_y, "mix_ln")
    mix, x2, h2 = _mm_resid(lay, cat, w["w_out"], xa1, modv, 5, 1.0, "mix_out", then_norm=(w["norm_ffn2"], 6))
    x3, ffn2, _, _ = _ffn_fwd(lay, lay.nsx, x2, w["norm_ffn2"], modv, 6, w["ffn2"], "ffn2", h=h2)
    loss, dx3, g["final_norm"] = _final_loss(lay, x3, w["final_norm"], target, "loss")
    dx2, g["ffn2"], g["norm_ffn2"], dmod2, _ = _ffn_bwd(lay, lay.nsx, dx3, ffn2, w["norm_ffn2"], modv, 6, w["ffn2"], "ffn2")
    dmix, dg2 = _resid_bwd(lay, lay.nsx, dx2, mix, modv, 5, 1.0, "mix_dres")
    dcat = _mm(dmix, w["w_out"], "nt", F32, "mix_dcat")
    g["w_out"] = _mm(cat, dmix, "tn", F32, "mix_dwout")
    duc, g["ln_w"], g["ln_b"] = _ln_bwd(lay, mx, uc, w["ln_w"], w["ln_b"], dcat, "mix_dln")
    du0, g["cconv_w"], g["cconv_b"] = _axial(lay, mx, u0, duc, w["cconv_w"], None, "mix_daxial")
    dyssd, dproj, g["d_row"], g["ssm_norm_w"] = _gate_bwd(lay, mx, yf, yb, xbc, proj, w["d_row"], w["ssm_norm_w"], dcat,
                                                          "mix_dgate")
    dproj = _glu_bwd(lay, mx, proj, du0, dproj, "mix_dglu")
    dxf, ddcf, ddrf, darf, dacf = _ssd_bwd(lay, mx, xbc, dt, dtt, w["a_row"], w["a_col"], hpf, dyssd, w["d_row"],
                                           False, "ssd_df")
    dxb, ddcb, ddrb, darb, dacb = _ssd_bwd(lay, mx, xbc, dt, dtt, w["a_row"], w["a_col"], hpb, dyssd, None,
                                           True, "ssd_db")
    g["a_row"] = (darf + darb) + (dacf + dacb).T
    dproj, g["conv_w"], g["conv_b"] = _conv5_bwd(lay.fine(), mx, proj, pre, dxf, dxb, w["conv_w"], dproj, "mix_dconv")
    dproj, g["dt_bias"] = _dt_bwd(lay, mx, proj, w["dt_bias"], (ddcf, ddcb, ddrf.T, ddrb.T), dproj, "mix_ddt")
    dha = _mm(dproj, w["w_in"], "nt", F32, "mix_dha")
    g["w_in"] = _mm(ha, dproj, "tn", F32, "mix_dwin")
    dxa1, g["norm_mix"], dss_mix = _norm_mod_bwd(lay, lay.ns, lay.nsx, xa1, w["norm_mix"], modv, 3, dha, dx2, "mix_dnorm")
    rides = exch.early_grads(g) if exch else (None, None)
    dx, g["ffn1"], g["norm_ffn1"], dmod1, rode = _ffn_bwd(lay, lay.ns, dxa1, ffn1, w["norm_ffn1"], modv, 0, w["ffn1"],
                                                         "ffn1", nout=lay.nsx, rides=rides)
    if exch:
        exch.take_early(rode)
    zrow = lambda t: jnp.concatenate([t, jnp.zeros((1,) + t.shape[1:], F32)], axis=0)
    dmodv = jnp.concatenate([dmod1, dss_mix, zrow(dg2), zrow(dmod2)], axis=1)
    return loss, dx, g, dmodv


class _GatherRide:
    def __init__(self, xs):
        self.arrays = list(xs)
        self.na = len(xs)
        self.out_shapes = [jax.ShapeDtypeStruct((N_DEV,) + tuple(x.shape), x.dtype) for x in xs]
        self.sems = [pltpu.SemaphoreType.DMA((7 * self.na,)), pltpu.SemaphoreType.DMA((7 * self.na,)),
                     pltpu.SemaphoreType.DMA((self.na,))]

    def _plan(self, x_refs, out_refs, send_sems, recv_sems, local_sems):
        mx_, my_, mc_ = lax.axis_index("x"), lax.axis_index("y"), lax.axis_index("c")
        me, sibling = (mx_, my_, mc_), (mx_, my_, 1 - mc_)
        chips = [(1 - mx_, my_), (mx_, 1 - my_), (1 - mx_, 1 - my_)]

        def slot(a, px, py, pc):
            return out_refs[a].at[4 * px + 2 * py + pc]

        def copy(a, k, block, to, own=False):
            return pltpu.make_async_remote_copy(
                src_ref=x_refs[a] if own else slot(a, *block), dst_ref=slot(a, *block),
                send_sem=send_sems.at[7 * a + k], recv_sem=recv_sems.at[7 * a + k], device_id=to, device_id_type=MESH)

        mine = [pltpu.make_async_copy(x_refs[a], slot(a, *me), local_sems.at[a]) for a in range(self.na)]
        first = []
        for a in range(self.na):
            first.append(copy(a, 0, me, sibling, own=True))
            first += [copy(a, 1 + j, me, (*chip, mc_), own=True) for j, chip in enumerate(chips)]
        return me, sibling, chips, mc_, copy, mine, first

    def start(self, x_refs, out_refs, send_sems, recv_sems, local_sems):
        *_, mine, first = self._plan(x_refs, out_refs, send_sems, recv_sems, local_sems)
        for cp in mine + first:
            cp.start()

    def finish(self, x_refs, out_refs, send_sems, recv_sems, local_sems):
        me, sibling, chips, mc_, copy, mine, first = self._plan(x_refs, out_refs, send_sems, recv_sems, local_sems)
        passed = []
        for j, chip in enumerate(chips):
            for a in range(self.na):
                copy(a, 1 + j, (*chip, mc_), me).wait_recv()
                fwd = copy(a, 4 + j, (*chip, mc_), sibling)
                fwd.start()
                passed.append(fwd)
        for a in range(self.na):
            copy(a, 0, sibling, me).wait_recv()
            for j, chip in enumerate(chips):
                copy(a, 4 + j, (*chip, 1 - mc_), me).wait_recv()
        for cp in first + passed:
            cp.wait_send()
        for cp in mine:
            cp.wait()


def _exchange(ride, name, in_hbm=True):
    n_in, n_out = len(ride.arrays), len(ride.out_shapes)

    def body(*refs):
        ins, outs, sems = refs[:n_in], refs[n_in:n_in + n_out], refs[n_in + n_out:]
        ride.start(ins, outs, *sems)
        ride.finish(ins, outs, *sems)

    space = pl.BlockSpec(memory_space=pl.ANY if in_hbm else pltpu.VMEM)
    return pl.pallas_call(body, out_shape=list(ride.out_shapes), in_specs=[space] * n_in, out_specs=[space] * n_out,
                          scratch_shapes=list(ride.sems), name=name)(*ride.arrays)


def _all_gather(xs, name, in_hbm):
    return _exchange(_GatherRide(xs), name, in_hbm)


N_CHIPS = 4


def _swap_sibling(gs, name):
    na = len(gs)

    def body(*refs):
        g_refs, out_refs, send_sems, recv_sems = refs[:na], refs[na:2 * na], refs[2 * na], refs[2 * na + 1]
        mx_, my_, mc_ = lax.axis_index("x"), lax.axis_index("y"), lax.axis_index("c")
        copies = [pltpu.make_async_remote_copy(
            src_ref=g_refs[a].at[k, 1 - mc_], dst_ref=out_refs[a].at[k], send_sem=send_sems.at[N_CHIPS * a + k],
            recv_sem=recv_sems.at[N_CHIPS * a + k], device_id=(mx_, my_, 1 - mc_), device_id_type=MESH)
            for a in range(na) for k in range(N_CHIPS)]
        for cp in copies:
            cp.start()
        for cp in copies:
            cp.wait_recv()
        for cp in copies:
            cp.wait_send()

    return pl.pallas_call(
        body, out_shape=[jax.ShapeDtypeStruct((N_CHIPS,) + tuple(g.shape[2:]), g.dtype) for g in gs],
        in_specs=[pl.BlockSpec(memory_space=pl.ANY)] * na, out_specs=[pl.BlockSpec(memory_space=pl.ANY)] * na,
        scratch_shapes=[pltpu.SemaphoreType.DMA((N_CHIPS * na,)), pltpu.SemaphoreType.DMA((N_CHIPS * na,))], name=name,
    )(*gs)


def _row_tile(r, n):
    if r * n * 4 <= (1 << 20):
        return r
    for t in (1024, 512, 256, 128, 64, 32, 16, 8):
        if r % t == 0 and t * n * 4 <= (1 << 20):
            return t
    return r


def _pair_add(place, g, got, name):
    _, _, r, n = g.shape
    tr = r if r * n * 4 <= (3 << 19) else _row_tile(r, n)

    def body(place_ref, g_ref, got_ref, o_ref, ob_ref):
        s = g_ref[...] + got_ref[...]
        o_ref[...] = s
        ob_ref[...] = s.astype(ob_ref.dtype)

    blk = pl.BlockSpec((None, tr, n), lambda k, i, pr: (k, i, 0))
    grid_spec = pltpu.PrefetchScalarGridSpec(
        num_scalar_prefetch=1, grid=(N_CHIPS, r // tr),
        in_specs=[pl.BlockSpec((None, None, tr, n), lambda k, i, pr: (k, pr[0], i, 0)), blk], out_specs=[blk, blk])
    return pl.pallas_call(body, grid_spec=grid_spec,
                          out_shape=[jax.ShapeDtypeStruct((N_CHIPS, r, n), F32), jax.ShapeDtypeStruct((N_CHIPS, r, n), BF16)],
                          name=name, compiler_params=_cparams(2))(place, g, got)


class _ChipSwapRide:
    def __init__(self, ps):
        self.arrays = list(ps)
        self.na = len(ps)
        self.out_shapes = [jax.ShapeDtypeStruct((N_CHIP_PEERS,) + tuple(p.shape[1:]), p.dtype) for p in ps]
        self.sems = [pltpu.SemaphoreType.DMA((N_CHIP_PEERS * self.na,)), pltpu.SemaphoreType.DMA((N_CHIP_PEERS * self.na,))]

    def _copies(self, p_refs, out_refs, send_sems, recv_sems):
        mx_, my_, mc_ = lax.axis_index("x"), lax.axis_index("y"), lax.axis_index("c")
        chips = [(1 - mx_, my_), (mx_, 1 - my_), (1 - mx_, 1 - my_)]
        return [pltpu.make_async_remote_copy(
            src_ref=p_refs[a].at[2 * cx + cy], dst_ref=out_refs[a].at[j], send_sem=send_sems.at[N_CHIP_PEERS * a + j],
            recv_sem=recv_sems.at[N_CHIP_PEERS * a + j], device_id=(cx, cy, mc_), device_id_type=MESH)
            for a in range(self.na) for j, (cx, cy) in enumerate(chips)]

    def start(self, p_refs, out_refs, send_sems, recv_sems):
        for cp in self._copies(p_refs, out_refs, send_sems, recv_sems):
            cp.start()

    def finish(self, p_refs, out_refs, send_sems, recv_sems):
        copies = self._copies(p_refs, out_refs, send_sems, recv_sems)
        for cp in copies:
            cp.wait_recv()
        for cp in copies:
            cp.wait_send()


def _sum_lead(x, name):
    k, r, n = x.shape
    tr = _row_tile(r, n * k)

    def body(x_ref, o_ref):
        acc = x_ref[0]
        for i in range(1, k):
            acc = acc + x_ref[i]
        o_ref[...] = acc

    return pl.pallas_call(body, grid=(r // tr,), in_specs=[pl.BlockSpec((k, tr, n), lambda i: (0, i, 0))],
                          out_specs=pl.BlockSpec((tr, n), lambda i: (i, 0)),
                          out_shape=jax.ShapeDtypeStruct((r, n), x.dtype), name=name, compiler_params=_cparams(1))(x)


def _adamw(place, w, parts, m, v, name):
    shape = w.shape
    cols = shape[-1]
    rows = math.prod(shape[:-1])
    to2 = lambda t: t.reshape(rows, cols)
    tr = _row_tile(rows, cols) if rows * cols * 4 > (1 << 20) else rows
    npart = len(parts)
    spec = pl.BlockSpec((tr, cols), lambda i, pr: (i, 0))
    native = len(shape) == 3 and shape[0] == 1
    own = pl.BlockSpec((None, tr, cols), lambda i, pr: (0, i, 0)) if native else spec
    as_own = (lambda t: t) if native else to2
    part_specs, part_args = [], []
    for piece in parts:
        if isinstance(piece, tuple):
            stack, k, row0 = piece
            part_args.append(stack.reshape(stack.shape[0], -1, cols))
            assert row0 % tr == 0
            if k == "chip":
                part_specs.append(pl.BlockSpec((None, tr, cols), functools.partial(lambda i, pr, b0: (pr[1], i + b0, 0),
                                                                                   b0=row0 // tr)))
            else:
                part_specs.append(pl.BlockSpec((None, tr, cols), functools.partial(
                    lambda i, pr, kk, b0: (kk, i + b0, 0), kk=k, b0=row0 // tr)))
        else:
            part_args.append(to2(piece))
            part_specs.append(spec)

    def body(place_ref, *refs):
        w_ref, m_ref, v_ref = refs[0], refs[1 + npart], refs[2 + npart]
        g_ref, d_ref, nm_ref, nv_ref = refs[3 + npart:]
        g = refs[1][...].astype(F32)
        for q in range(1, npart):
            g = g + refs[1 + q][...].astype(F32)
        mm = ADAM_B1 * m_ref[...] + (1.0 - ADAM_B1) * g
        vv = ADAM_B2 * v_ref[...] + (1.0 - ADAM_B2) * jnp.square(g)
        m_hat = mm / (1.0 - ADAM_B1 ** ADAM_STEP)
        v_hat = vv / (1.0 - ADAM_B2 ** ADAM_STEP)
        g_ref[...] = g
        d_ref[...] = -ADAM_LR * (m_hat / (jnp.sqrt(v_hat) + ADAM_EPS) + ADAM_WD * w_ref[...])
        nm_ref[...] = mm
        nv_ref[...] = vv

    sh = jax.ShapeDtypeStruct(shape if native else (rows, cols), F32)
    grid_spec = pltpu.PrefetchScalarGridSpec(num_scalar_prefetch=1, grid=(rows // tr,),
                                             in_specs=[own] + part_specs + [own, own], out_specs=[own] * 4)
    outs = pl.pallas_call(body, grid_spec=grid_spec, out_shape=[sh] * 4, name=name, compiler_params=_cparams(1),
                          )(place, as_own(w), *part_args, as_own(m), as_own(v))
    return tuple(o.reshape(shape) for o in outs)


def _packed_rows(n, width):
    return -(-n // (8 * width)) * 8


def _pack_rows(items, width):
    rows = []
    for t in items:
        flat = t.reshape(-1)
        n = flat.shape[0]
        k = _packed_rows(n, width)
        if k * width > n:
            flat = jnp.concatenate([flat, jnp.zeros((k * width - n,), t.dtype)])
        rows.append(flat.reshape(k, width))
    return jnp.concatenate(rows, axis=0)


def _unpack_rows(packed, shapes, lead=()):
    width = packed.shape[-1]
    out, r = [], 0
    for sh in shapes:
        n = math.prod(sh)
        k = _packed_rows(n, width)
        piece = packed[..., r:r + k, :].reshape(tuple(lead) + (k * width,))[..., :n]
        out.append(piece.reshape(tuple(lead) + tuple(sh)))
        r += k
    return out


def _cols_full(t):
    return jnp.transpose(t, (1, 0, 2)).reshape(t.shape[1], -1)


def _cols_shards(t):
    d = t.shape[0]
    return jnp.transpose(t.reshape(d, N_DEV, -1), (1, 0, 2))


BIG = ("ffn1_gate", "ffn1_up", "ffn1_down", "w_in", "w_out", "ffn2_gate", "ffn2_up", "ffn2_down")


def kernel(x, c, ctx, c_ctx, w_mod, b_mod, norm_ffn1, ffn1_gate, ffn1_up, ffn1_down, norm_mix, w_in, ssm_conv_w, ssm_conv_b, dt_bias_fwd, dt_bias_bwd, a_log_fwd, a_log_bwd, ssm_d, ssm_norm_w, cconv_w, cconv_b, cconv_ln_w, cconv_ln_b, w_out, norm_ffn2, ffn2_gate, ffn2_up, ffn2_down, final_norm, loss_target, m_c_ctx, m_w_mod, m_b_mod, m_norm_ffn1, m_ffn1_gate, m_ffn1_up, m_ffn1_down, m_norm_mix, m_w_in, m_ssm_conv_w, m_ssm_conv_b, m_dt_bias_fwd, m_dt_bias_bwd, m_a_log_fwd, m_a_log_bwd, m_ssm_d, m_ssm_norm_w, m_cconv_w, m_cconv_b, m_cconv_ln_w, m_cconv_ln_b, m_w_out, m_norm_ffn2, m_ffn2_gate, m_ffn2_up, m_ffn2_down, m_final_norm, v_c_ctx, v_w_mod, v_b_mod, v_norm_ffn1, v_ffn1_gate, v_ffn1_up, v_ffn1_down, v_norm_mix, v_w_in, v_ssm_conv_w, v_ssm_conv_b, v_dt_bias_fwd, v_dt_bias_bwd, v_a_log_fwd, v_a_log_bwd, v_ssm_d, v_ssm_norm_w, v_cconv_w, v_cconv_b, v_cconv_ln_w, v_cconv_ln_b, v_w_out, v_norm_ffn2, v_ffn2_gate, v_ffn2_up, v_ffn2_down, v_final_norm):
    args = dict(locals())
    names = ("c_ctx", "w_mod", "b_mod", "norm_ffn1", "ffn1_gate", "ffn1_up", "ffn1_down", "norm_mix", "w_in",
             "ssm_conv_w", "ssm_conv_b", "dt_bias_fwd", "dt_bias_bwd", "a_log_fwd", "a_log_bwd", "ssm_d", "ssm_norm_w",
             "cconv_w", "cconv_b", "cconv_ln_w", "cconv_ln_b", "w_out", "norm_ffn2", "ffn2_gate", "ffn2_up",
             "ffn2_down", "final_norm")
    wts = {n: args[n] for n in names}
    bl, seq, d = x.shape
    clen = ctx.shape[1]
    heads = dt_bias_fwd.shape[1]
    ft = _gate_tile(ffn1_gate.shape[2] * N_DEV)
    lay = _Lay(bl, seq, clen, d)
    mx = _Mix(d, heads)
    nb = bl * N_DEV
    me = 4 * lax.axis_index("x") + 2 * lax.axis_index("y") + lax.axis_index("c")
    mcols = w_mod.shape[2]
    n_ctx_mod = 5 * d

    place = jnp.stack([lax.axis_index("c"), 2 * lax.axis_index("x") + lax.axis_index("y")]).astype(jnp.int32)

    small_shapes = [(bl, d), ssm_conv_w.shape[1:], cconv_w.shape[1:]]
    (g1,) = _all_gather([_pack_rows([c, ssm_conv_w, cconv_w], d)], "gather_small", False)
    c_g, conv_g, cconv_g = _unpack_rows(g1, small_shapes, (N_DEV,))
    c_all = c_g.reshape(nb, d)
    conv_w_full = jnp.transpose(conv_g, (1, 0, 2)).reshape(conv_g.shape[1], -1)
    cconv_w_full = jnp.transpose(cconv_g, (1, 0, 2)).reshape(cconv_g.shape[1], -1)

    s_all = jnp.concatenate([_silu(c_all), _silu(c_ctx)[None, :], jnp.zeros((7, d), F32)], axis=0)
    mod_cols = _mm(s_all, w_mod[0], "nn", F32, "mod_cols")
    (g2,) = _all_gather([mod_cols], "gather_mod", False)
    mod_all = _cols_full(g2) + b_mod
    mod_mine = jnp.concatenate([lax.dynamic_slice_in_dim(mod_all, me * bl, bl, axis=0), mod_all[nb:nb + 1]], axis=0)
    modv = mod_mine.reshape(bl + 1, N_MOD, d)

    hh = 2 * heads
    shard16 = lambda n: wts[n][0].astype(BF16)

    nl = ffn1_gate.shape[2]
    spt = ft // nl
    assert ft % nl == 0 and N_DEV % spt == 0

    def ffn_weights(gate, up, down):
        both = jnp.stack([gate, up], axis=1).reshape(N_DEV // spt, spt, 2, d, nl)
        return jnp.transpose(both, (3, 0, 2, 1, 4)).reshape(d, -1), down.reshape(-1, d), ft

    def grads_by_dest(name, grad):
        if name == "w_in":
            grad = _cols_shards(jnp.concatenate([grad[:, mx.off_z:], grad[:, :mx.off_dt + hh],
                                                 grad[:, mx.off_glu:mx.off_z]], axis=1))
        elif name.endswith("_gu"):
            grad = jnp.transpose(grad.reshape(d, N_DEV // spt, 2, spt, nl), (1, 3, 2, 0, 4)).reshape(N_DEV, 2 * d, nl)
        else:
            grad = grad.reshape((N_DEV,) + tuple(wts[name].shape[1:]))
        return grad.reshape((N_CHIPS, 2) + tuple(grad.shape[1:]))

    def ffn_grads(tag, pair):
        return {tag + "_gu": pair[0], tag + "_down": pair[1]}

    def to_chip_sums(named):
        names_ = list(named)
        by_dest = [grads_by_dest(n, named[n]) for n in names_]
        got = _swap_sibling(by_dest, "rs_sibling_" + names_[0])
        return {n: _pair_add(place, t, s, "rs_pair_add_" + n) for n, t, s in zip(names_, by_dest, got)}

    class _Overlap:
        late_names = ("w_in", "w_out", "ffn2_gate", "ffn2_up", "ffn2_down")
        late_weights = _GatherRide([shard16(n) for n in late_names])
        sums, arrived = {}, {}

        def unpack_late(self, outs):
            full = dict(zip(self.late_names, outs))
            w_in_f = _cols_full(full["w_in"])
            w_in_p = jnp.concatenate([w_in_f[:, mx.ref_x:mx.ref_glu], jnp.zeros((d, DT_PAD - hh), BF16),
                                      w_in_f[:, mx.ref_glu:], w_in_f[:, :d]], axis=1)
            return {"w_in": w_in_p, "w_out": full["w_out"].reshape(-1, d),
                    "ffn2": ffn_weights(full["ffn2_gate"], full["ffn2_up"], full["ffn2_down"])}

        def early_grads(self, g):
            self.sums = to_chip_sums({**ffn_grads("ffn2", g["ffn2"]), "w_in": g["w_in"], "w_out": g["w_out"]})
            self.groups = (("ffn2_gu", "ffn2_down"), ("w_in", "w_out"))
            return tuple(_ChipSwapRide([self.sums[n][1] for n in grp]) for grp in self.groups)

        def take_early(self, rode):
            for grp, outs in zip(self.groups, rode):
                self.arrived.update(zip(grp, outs))

    exch = _Overlap()
    full = dict(zip(BIG[:3], _all_gather([shard16(n) for n in BIG[:3]], "gather_weights", True)))
    lanes_pad = lambda a, b: jnp.concatenate([a, b, jnp.zeros((1, DT_LANES - hh), F32)], axis=1)
    a_vals = lanes_pad(-jnp.exp(a_log_fwd), -jnp.exp(a_log_bwd))
    w = {
        "norm_ffn1": norm_ffn1, "norm_mix": norm_mix, "norm_ffn2": norm_ffn2, "final_norm": final_norm[None, :],
        "ffn1": ffn_weights(full["ffn1_gate"], full["ffn1_up"], full["ffn1_down"]),
        "conv_w": jnp.concatenate([conv_w_full, jnp.zeros((3, mx.xw), F32)], axis=0), "conv_b": ssm_conv_b,
        "dt_bias": lanes_pad(dt_bias_fwd, dt_bias_bwd), "a_row": a_vals, "a_col": a_vals.T,
        "d_row": jnp.repeat(ssm_d, HEAD_DIM, axis=1), "ssm_norm_w": ssm_norm_w,
        "cconv_w": cconv_w_full, "cconv_b": cconv_b, "ln_w": cconv_ln_w, "ln_b": cconv_ln_b,
    }

    xa = jnp.concatenate([x.reshape(bl * seq, d), ctx.reshape(bl * clen, d)], axis=0)
    loss, grad_x, g, dmodv = _local_step(lay, mx, xa, loss_target.reshape(bl * seq, d), modv, w, exch)
    loss = lax.psum(loss[0, 0], ("x", "y", "c"))

    sums = {**exch.sums, **to_chip_sums(ffn_grads("ffn1", g["ffn1"]))}
    arrived = dict(exch.arrived)
    last = ("ffn1_gu", "ffn1_down")
    arrived.update(zip(last, _exchange(_ChipSwapRide([sums[n][1] for n in last]), "rs_chips")))

    def big_parts(n):
        key, row0 = (n[:4] + "_gu", d if n.endswith("_up") else 0) if n.endswith(("_gate", "_up")) else (n, 0)
        return [(sums[key][0], "chip", row0)] + [(arrived[key], k, row0) for k in range(N_CHIP_PEERS)]

    n9 = N_MOD * d
    dmod_rows = dmodv.reshape(bl + 1, n9)
    ctx_row = jnp.concatenate([dmod_rows[bl, :n_ctx_mod], jnp.zeros((n9 - n_ctx_mod,), F32)])
    summed = [ctx_row, g["norm_ffn1"], g["norm_mix"], g["norm_ffn2"], g["final_norm"], g["conv_b"], g["dt_bias"],
              g["a_row"], g["d_row"], g["ssm_norm_w"], g["cconv_b"], g["ln_w"], g["ln_b"], g["conv_w"][:5], g["cconv_w"]]
    sum_shapes = [t.shape for t in summed]
    (g4,) = _all_gather([_pack_rows([dmod_rows[:bl]] + summed, d)], "gather_small_grads", False)
    dmod_batch = g4[:, :bl * N_MOD].reshape(nb, n9)
    tot = _sum_lead(g4[:, _packed_rows(bl * n9, d):], "sum_small_grads")
    (dctx, g_n1, g_nm, g_n2, g_fn, g_cb, g_dtb, g_a, g_drow, g_snw, g_ccb, g_lnw, g_lnb, g_cw, g_ccw) = _unpack_rows(tot, sum_shapes)
    dmod_all = jnp.concatenate([dmod_batch, dctx[None, :], jnp.zeros((7, n9), F32)], axis=0)

    dmod_my_cols = lax.dynamic_slice_in_dim(dmod_all, me * mcols, mcols, axis=1)
    g_w_mod = _mm(s_all, dmod_my_cols, "tn", F32, "dw_mod")[None]
    g_b_mod = _sum_lead(dmod_all.reshape(nb + 8, N_MOD, d), "db_mod").reshape(1, n9)
    ds_part = _mm(dmod_my_cols[nb:nb + 8], w_mod[0], "nt", F32, "ds_ctx")
    (g5,) = _all_gather([jnp.concatenate([ds_part[0:1], jnp.zeros((7, d), F32)], axis=0)], "gather_ds_ctx", False)
    ds_ctx = _sum_lead(g5, "sum_ds_ctx")[0]
    sg = jax.nn.sigmoid(c_ctx)
    g_c_ctx = ds_ctx * (sg * (1.0 + c_ctx * (1.0 - sg)))

    a_f, a_b = a_vals[:, :heads], a_vals[:, heads:hh]
    grads = {
        "c_ctx": [g_c_ctx], "w_mod": [g_w_mod], "b_mod": [g_b_mod],
        "norm_ffn1": [g_n1], "norm_mix": [g_nm], "norm_ffn2": [g_n2], "final_norm": [g_fn.reshape(-1)],
        "ssm_conv_w": [lax.dynamic_slice_in_dim(g_cw, me * ssm_conv_w.shape[2], ssm_conv_w.shape[2], axis=1)[None]],
        "ssm_conv_b": [g_cb],
        "dt_bias_fwd": [g_dtb[:, :heads]], "dt_bias_bwd": [g_dtb[:, heads:hh]],
        "a_log_fwd": [g_a[:, :heads] * a_f], "a_log_bwd": [g_a[:, heads:hh] * a_b],
        "ssm_d": [jnp.sum(g_drow.reshape(1, heads, HEAD_DIM), axis=2)], "ssm_norm_w": [g_snw],
        "cconv_w": [lax.dynamic_slice_in_dim(g_ccw, me * cconv_w.shape[2], cconv_w.shape[2], axis=1)[None]],
        "cconv_b": [g_ccb], "cconv_ln_w": [g_lnw], "cconv_ln_b": [g_lnb],
    }
    for n in BIG:
        grads[n] = big_parts(n)

    out_g, out_d, out_m, out_v = [], [], [], []
    for n in names:
        gr, de, nm, nv = _adamw(place, wts[n], grads[n], args["m_" + n], args["v_" + n], "adamw_" + n)
        out_g.append(gr)
        out_d.append(de)
        out_m.append(nm)
        out_v.append(nv)
    return (loss, grad_x.reshape(bl, seq, d), *out_g, *out_d, *out_m, *out_v)
```

```python
import functools
import math

import jax
import jax.numpy as jnp
from jax import lax
from jax.experimental import pallas as pl
from jax.experimental.pallas import tpu as pltpu

F32 = jnp.float32
BF16 = jnp.bfloat16
MESH = pl.DeviceIdType.MESH

N_DEV = 8
N_CHIP_PEERS = 3
HEAD_DIM = 64
N_STATE = 128
SSD_GROUPS = 2
CHUNK = 128
GRID_W = 64
N_MOD = 9
EPS = 1e-6
DT_PAD = 512
DT_LANES = 128
HALO = 8
ROW_TILE = 512
FINE_ROW_TILE = 256
VMEM_LIMIT = 48 * 1024 * 1024
NEG_BIG = -1e30

ADAM_LR = 0.001
ADAM_B1 = 0.9
ADAM_B2 = 0.999
ADAM_EPS = 1e-08
ADAM_WD = 0.01
ADAM_STEP = 10


def _pick(n, prefs):
    for p in prefs:
        if n % p == 0:
            return p
    return n


MM_TILE_CAP = 2816
MM_TILE_ELEMS = 3 << 20
MM_OUT_TILE_ELEMS = 3 << 19
MM_FULL_ROWS = 1024


def _big_tile(n, cap):
    if n <= cap:
        return n
    best = 0
    for t in range(128, cap + 1, 128):
        if n % t == 0:
            best = t
    return best or n


def _cparams(ndim):
    return pltpu.CompilerParams(dimension_semantics=("arbitrary",) * ndim, vmem_limit_bytes=VMEM_LIMIT)


def _silu(v):
    return v * jax.nn.sigmoid(v)


def _mm(a, b, mode, out_dtype, name, tn=None, tm=None, extras=(), epilogue=None, outs=None, ride=None):
    if mode == "tn":
        (K, M), (K2, N) = a.shape, b.shape
    elif mode == "nt":
        (M, K), (N, K2) = a.shape, b.shape
    else:
        (M, K), (K2, N) = a.shape, b.shape
    assert K == K2, (name, a.shape, b.shape)
    tm = tm or (M if M <= MM_FULL_ROWS else None)
    if tn is None:
        tn = _big_tile(N, min(MM_TILE_CAP, max(128, MM_OUT_TILE_ELEMS // (tm or 512))))
    if tm is None:
        tm = _big_tile(M, max(128, MM_OUT_TILE_ELEMS // tn))
    tk = _big_tile(K, min(MM_TILE_CAP, MM_TILE_ELEMS // max(tn, tm)))
    nk = K // tk
    ni, nj = M // tm, N // tn
    swap = nk == 1 and (K * N + M * K * nj) < (M * K + K * N * ni)
    ij = (lambda g0, g1: (g1, g0)) if swap else (lambda g0, g1: (g0, g1))
    if mode == "tn":
        a_spec = pl.BlockSpec((tk, tm), lambda g0, g1, k: (k, ij(g0, g1)[0]))
        dn = (((0,), (0,)), ((), ()))
    else:
        a_spec = pl.BlockSpec((tm, tk), lambda g0, g1, k: (ij(g0, g1)[0], k))
        dn = (((1,), (1,)), ((), ())) if mode == "nt" else (((1,), (0,)), ((), ()))
    if mode == "nt":
        b_spec = pl.BlockSpec((tn, tk), lambda g0, g1, k: (ij(g0, g1)[1], k))
    else:
        b_spec = pl.BlockSpec((tk, tn), lambda g0, g1, k: (k, ij(g0, g1)[1]))
    if outs is None:
        outs = [(tn, out_dtype)]
    nx = len(extras)

    def tile(w):
        return pl.BlockSpec((tm, w), lambda g0, g1, k: ij(g0, g1))

    def extra_spec(item):
        if len(item) == 3:
            return pl.BlockSpec((None,) + tuple(item[0].shape[1:]), lambda g0, g1, k: (item[1](ij(g0, g1)[0]), 0, 0))
        return tile(item[1])

    def finish(acc, refs):
        vals = (acc,) if epilogue is None else epilogue(acc, *[r[...] for r in refs[:nx]])
        for o_ref, v in zip(refs[nx:], vals):
            o_ref[...] = v.astype(o_ref.dtype)

    grid = (nj, ni, nk) if swap else (ni, nj, nk)
    nout = len(outs)
    r_in = len(ride.arrays) if ride else 0
    r_out = len(ride.out_shapes) if ride else 0

    def compute(a_ref, b_ref, refs):
        part = lax.dot_general(a_ref[...].astype(BF16), b_ref[...].astype(BF16), dn, preferred_element_type=F32)
        if nk == 1:
            finish(part, refs)
            return
        acc_ref, k = refs[-1], pl.program_id(2)
        _acc(acc_ref, k == 0, part)

        @pl.when(k == nk - 1)
        def _():
            finish(acc_ref[...], refs[:-1])

    def body(a_ref, b_ref, *refs):
        if ride is None:
            compute(a_ref, b_ref, refs)
            return
        x_refs, rin = refs[:nx], refs[nx:nx + r_in]
        o_refs, rout = refs[nx + r_in:nx + r_in + nout], refs[nx + r_in + nout:nx + r_in + nout + r_out]
        tail = refs[nx + r_in + nout + r_out:]
        nacc = 1 if nk > 1 else 0
        sems = tail[nacc:]
        ids = [pl.program_id(q) for q in range(3)]
        first = functools.reduce(jnp.logical_and, [i == 0 for i in ids])
        last = functools.reduce(jnp.logical_and, [i == n - 1 for i, n in zip(ids, grid)])
        pl.when(first)(lambda: ride.start(rin, rout, *sems))
        compute(a_ref, b_ref, tuple(x_refs) + tuple(o_refs) + tuple(tail[:nacc]))
        pl.when(last)(lambda: ride.finish(rin, rout, *sems))

    hbm = pl.BlockSpec(memory_space=pl.ANY)
    res = pl.pallas_call(
        body, grid=grid, in_specs=[a_spec, b_spec] + [extra_spec(x) for x in extras] + [hbm] * r_in,
        out_specs=[tile(w) for w, _ in outs] + [hbm] * r_out,
        out_shape=[jax.ShapeDtypeStruct((M, nj * w), dt) for w, dt in outs] + (list(ride.out_shapes) if ride else []),
        scratch_shapes=([pltpu.VMEM((tm, tn), F32)] if nk > 1 else []) + (list(ride.sems) if ride else []),
        name=name, compiler_params=_cparams(3),
    )(a, b, *[x[0] for x in extras], *(ride.arrays if ride else []))
    main = res[0] if epilogue is None else res[:nout]
    return (main, res[nout:]) if ride else main


class _Lay:
    def __init__(self, bl, seq, clen, d, tt=None):
        self.bl, self.seq, self.clen, self.d = bl, seq, clen, d
        self.tt = min(ROW_TILE, math.gcd(seq, bl * clen)) if tt is None else tt
        assert seq % self.tt == 0 and (bl * clen) % self.tt == 0 and self.tt % 8 == 0
        self.spb = seq // self.tt
        self.spc = clen // self.tt
        self.nsx = bl * self.spb
        self.nsc = bl * clen // self.tt
        self.ns = self.nsx + self.nsc
        self.tx = bl * seq
        self.ta = self.tx + bl * clen

    def fine(self):
        return _Lay(self.bl, self.seq, self.clen, self.d, min(FINE_ROW_TILE, self.clen))

    def mrow(self, s):
        return jnp.where(s < self.nsx, s // self.spb, self.bl)

    def first_of_row(self, s):
        return jnp.logical_or(jnp.logical_and(s < self.nsx, s % self.spb == 0), s == self.nsx)

    def seq_first(self, s):
        return jnp.where(s < self.nsx, s % self.spb == 0, (s - self.nsx) % self.spc == 0)

    def seq_last(self, s):
        return jnp.where(s < self.nsx, s % self.spb == self.spb - 1, (s - self.nsx) % self.spc == self.spc - 1)


def _tok(lay, c, cb=0, clamp=None):
    if clamp is None:
        return pl.BlockSpec((lay.tt, c), lambda j, s: (s, cb + j))
    return pl.BlockSpec((lay.tt, c), lambda j, s: (jnp.minimum(s, clamp), cb + j))


def _halo_prev(lay, c, cb=0):
    u = lay.tt // HALO
    return pl.BlockSpec((HALO, c), lambda j, s: (jnp.maximum(s * u - 1, 0), cb + j))


def _halo_next(lay, c, cb=0):
    u = lay.tt // HALO
    last = lay.ta // HALO - 1
    return pl.BlockSpec((HALO, c), lambda j, s: (jnp.minimum((s + 1) * u, last), cb + j))


def _row(lay, k, c):
    return pl.BlockSpec((None, k, c), lambda j, s: (lay.mrow(s), 0, 0))


def _glob(k, c, cb=None):
    if cb is None:
        return pl.BlockSpec((k, c), lambda j, s: (0, 0))
    return pl.BlockSpec((k, c), lambda j, s: (0, cb + j))


def _tok_call(name, body, ncb, nseg, in_specs, out_specs, out_shape, inputs, scratch=(), aliases=None):
    return pl.pallas_call(body, grid=(ncb, nseg), in_specs=in_specs, out_specs=out_specs, out_shape=out_shape,
                          scratch_shapes=list(scratch), name=name, compiler_params=_cparams(2),
                          input_output_aliases=aliases or {})(*inputs)


def _acc(ref, first, val):
    @pl.when(first)
    def _():
        ref[...] = val

    @pl.when(jnp.logical_not(first))
    def _():
        ref[...] += val


def _norm_mod_f(x, w, sh, sc):
    y = x * lax.rsqrt(jnp.mean(x * x, axis=-1, keepdims=True) + EPS) * w
    return y * (1.0 + sc) + sh


def _norm_mod_fwd(lay, nseg, x, w, modv, ksh, name):
    d = lay.d

    def body(x_ref, w_ref, m_ref, h_ref):
        h = _norm_mod_f(x_ref[...], w_ref[...], m_ref[ksh:ksh + 1, :], m_ref[ksh + 1:ksh + 2, :])
        h_ref[...] = h.astype(h_ref.dtype)

    return _tok_call(name, body, 1, nseg, [_tok(lay, d), _glob(1, d), _row(lay, N_MOD, d)], _tok(lay, d),
                     jax.ShapeDtypeStruct((nseg * lay.tt, d), BF16), (x, w, modv))


def _norm_mod_bwd(lay, nseg, nres, x, w, modv, ksh, dh, dres, name, nout=None):
    d = lay.d
    nrow = lay.bl + (1 if nseg > lay.nsx else 0)
    nout = nseg if nout is None else nout

    def body(x_ref, w_ref, m_ref, dh_ref, dres_ref, dx_ref, dw_ref, dm_ref):
        s = pl.program_id(1)
        _, vjp = jax.vjp(_norm_mod_f, x_ref[...], w_ref[...], m_ref[ksh:ksh + 1, :], m_ref[ksh + 1:ksh + 2, :])
        dx, dw, dsh, dsc = vjp(dh_ref[...])

        @pl.when(s < nout)
        def _():
            dx_ref[...] = dx + jnp.where(s < nres, dres_ref[...], 0.0)

        _acc(dw_ref, s == 0, dw)
        _acc(dm_ref, lay.first_of_row(s), jnp.concatenate([dsh, dsc], axis=0))

    return _tok_call(
        name, body, 1, nseg,
        [_tok(lay, d), _glob(1, d), _row(lay, N_MOD, d), _tok(lay, d), _tok(lay, d, clamp=nres - 1)],
        [_tok(lay, d, clamp=nout - 1), _glob(1, d), _row(lay, 2, d)],
        [jax.ShapeDtypeStruct((nout * lay.tt, d), F32), jax.ShapeDtypeStruct((1, d), F32),
         jax.ShapeDtypeStruct((nrow, 2, d), F32)],
        (x, w, modv, dh, dres))


def _mm_resid(lay, a, b, x, modv, kg, coef, name, then_norm=None, ride=None):
    d = lay.d
    tm = min(ROW_TILE, math.gcd(lay.seq, lay.bl * lay.clen))
    assert a.shape[0] % tm == 0 and b.shape[1] == d

    def row_of(i):
        return jnp.where(i * tm < lay.tx, (i * tm) // lay.seq, lay.bl)

    def add(acc, x_tile, m_blk, *nw):
        y = x_tile + (coef * m_blk[kg:kg + 1, :]) * acc
        if then_norm is None:
            return acc, y
        k = then_norm[1]
        return acc, y, _norm_mod_f(y, nw[0], m_blk[k:k + 1, :], m_blk[k + 1:k + 2, :])

    extras = [(x, d), (modv, row_of, "rows")]
    outs = [(d, F32), (d, F32)]
    if then_norm is not None:
        extras.append((then_norm[0].reshape(1, 1, d), lambda i: 0, "rows"))
        outs.append((d, BF16))
    return _mm(a, b, "nn", None, name, tm=tm, tn=d, extras=extras, epilogue=add, outs=outs, ride=ride)


def _resid_bwd(lay, nseg, dy, o, modv, kg, coef, name):
    d = lay.d
    nrow = lay.bl + (1 if nseg > lay.nsx else 0)

    def body(dy_ref, o_ref, m_ref, do_ref, dg_ref):
        s = pl.program_id(1)
        dy = dy_ref[...]
        do_ref[...] = (dy * (coef * m_ref[kg:kg + 1, :])).astype(do_ref.dtype)
        _acc(dg_ref, lay.first_of_row(s), jnp.sum(dy * o_ref[...], axis=0, keepdims=True) * coef)

    return _tok_call(name, body, 1, nseg, [_tok(lay, d), _tok(lay, d), _row(lay, N_MOD, d)],
                     [_tok(lay, d), _row(lay, 1, d)],
                     [jax.ShapeDtypeStruct((nseg * lay.tt, d), BF16), jax.ShapeDtypeStruct((nrow, 1, d), F32)],
                     (dy, o, modv))


def _final_loss(lay, x, wf, target, name):
    d = lay.d

    def body(x_ref, w_ref, t_ref, loss_ref, dx_ref, dw_ref):
        s = pl.program_id(1)

        def f(xv, wv):
            return xv * lax.rsqrt(jnp.mean(xv * xv, axis=-1, keepdims=True) + EPS) * wv

        y, vjp = jax.vjp(f, x_ref[...], w_ref[...])
        err = y - t_ref[...]
        part = 0.5 * jnp.sum(jnp.sum(err * err, axis=-1, keepdims=True), axis=0, keepdims=True) / d
        dx, dw = vjp(err / d)
        dx_ref[...] = dx
        _acc(loss_ref, s == 0, part)
        _acc(dw_ref, s == 0, dw)

    return _tok_call(name, body, 1, lay.nsx, [_tok(lay, d), _glob(1, d), _tok(lay, d)],
                     [_glob(1, 1), _tok(lay, d), _glob(1, d)],
                     [jax.ShapeDtypeStruct((1, 1), F32), jax.ShapeDtypeStruct((lay.tx, d), F32),
                      jax.ShapeDtypeStruct((1, d), F32)], (x, wf, target))


class _Mix:
    def __init__(self, d, heads):
        self.d_ssm = d
        self.d_conv = d
        self.heads = heads
        assert heads * HEAD_DIM == d and heads % (2 * SSD_GROUPS) == 0 and 2 * heads <= DT_LANES
        self.gn = SSD_GROUPS * N_STATE
        self.xw = d + 2 * self.gn
        self.off_x = 0
        self.off_dt = self.xw
        self.off_glu = self.xw + DT_PAD
        self.off_z = self.off_glu + 2 * d
        self.pw = self.off_z + d
        assert self.off_z % d == 0
        self.ref_x = d
        self.ref_dt = d + self.xw
        self.ref_glu = self.ref_dt + 2 * heads
        self.cc = self.xw if self.off_x % self.xw == 0 else _pick(self.xw, (512, 256, 128))


def _conv5_fwd(lay, mx, proj, cw, cb, name):
    c, tt = mx.cc, lay.tt
    cb0 = mx.off_x // c
    assert mx.off_x % c == 0

    def body(prev_ref, cur_ref, next_ref, w_ref, b_ref, pre_ref, act_ref, ext_ref):
        s = pl.program_id(1)
        ext_ref[0:HALO, :] = jnp.where(lay.seq_first(s), 0.0, prev_ref[...])
        ext_ref[HALO:HALO + tt, :] = cur_ref[...]
        ext_ref[HALO + tt:, :] = jnp.where(lay.seq_last(s), 0.0, next_ref[...])
        acc = jnp.zeros((tt, c), F32) + b_ref[...]
        for k in range(5):
            acc = acc + w_ref[k:k + 1, :] * ext_ref[pl.ds(HALO + k - 2, tt), :]
        pre_ref[...] = acc
        act_ref[...] = _silu(acc)

    sh = jax.ShapeDtypeStruct((lay.ta, mx.xw), F32)
    return _tok_call(name, body, mx.xw // c, lay.ns,
                     [_halo_prev(lay, c, cb0), _tok(lay, c, cb0), _halo_next(lay, c, cb0), _glob(8, c, 0), _glob(1, c, 0)],
                     [_tok(lay, c), _tok(lay, c)], [sh, sh], (proj, proj, proj, cw, cb),
                     scratch=[pltpu.VMEM((tt + 2 * HALO, c), F32)])


def _conv5_bwd(lay, mx, proj, pre, dact_f, dact_b, cw, dproj, name):
    c, tt = mx.cc, lay.tt
    cb0 = mx.off_x // c

    def dsilu(p):
        sg = jax.nn.sigmoid(p)
        return sg * (1.0 + p * (1.0 - sg))

    def body(xp_ref, xc_ref, xn_ref, pp_ref, pc_ref, pn_ref, fp_ref, fc_ref, fn_ref, bp_ref, bc_ref, bn_ref, w_ref,
             buf_ref, dx_ref, dw_ref, db_ref, extx_ref, extd_ref):
        s = pl.program_id(1)
        first, last = lay.seq_first(s), lay.seq_last(s)
        dcur = (fc_ref[...] + bc_ref[...]) * dsilu(pc_ref[...])
        extd_ref[0:HALO, :] = jnp.where(first, 0.0, (fp_ref[...] + bp_ref[...]) * dsilu(pp_ref[...]))
        extd_ref[HALO:HALO + tt, :] = dcur
        extd_ref[HALO + tt:, :] = jnp.where(last, 0.0, (fn_ref[...] + bn_ref[...]) * dsilu(pn_ref[...]))
        extx_ref[0:HALO, :] = jnp.where(first, 0.0, xp_ref[...])
        extx_ref[HALO:HALO + tt, :] = xc_ref[...]
        extx_ref[HALO + tt:, :] = jnp.where(last, 0.0, xn_ref[...])
        dx = jnp.zeros((tt, c), F32)
        rows = []
        for k in range(5):
            dx = dx + w_ref[k:k + 1, :] * extd_ref[pl.ds(HALO - (k - 2), tt), :]
            rows.append(jnp.sum(dcur * extx_ref[pl.ds(HALO + k - 2, tt), :], axis=0, keepdims=True))
        dx_ref[...] = dx.astype(dx_ref.dtype)
        rows.append(jnp.zeros((3, c), F32))
        _acc(dw_ref, s == 0, jnp.concatenate(rows, axis=0))
        _acc(db_ref, s == 0, jnp.sum(dcur, axis=0, keepdims=True))

    three = lambda cbx: [_halo_prev(lay, c, cbx), _tok(lay, c, cbx), _halo_next(lay, c, cbx)]
    ext = pltpu.VMEM((tt + 2 * HALO, c), F32)
    return _tok_call(name, body, mx.xw // c, lay.ns,
                     three(cb0) + three(0) + three(0) + three(0) + [_glob(8, c, 0), pl.BlockSpec(memory_space=pl.ANY)],
                     [_tok(lay, c, cb0), _glob(8, c, 0), _glob(1, c, 0)],
                     [jax.ShapeDtypeStruct(dproj.shape, dproj.dtype), jax.ShapeDtypeStruct((8, mx.xw), F32),
                      jax.ShapeDtypeStruct((1, mx.xw), F32)],
                     (proj, proj, proj, pre, pre, pre, dact_f, dact_f, dact_f, dact_b, dact_b, dact_b, cw, dproj),
                     scratch=[ext, ext], aliases={13: 0})


def _softplus(v):
    return jnp.maximum(v, 0.0) + jnp.log1p(jnp.exp(-jnp.abs(v)))


def _dt_fwd(lay, mx, proj, bias, name):
    cb = mx.off_dt // DT_LANES

    def body(p_ref, b_ref, dt_ref):
        dt_ref[...] = _softplus(p_ref[...] + b_ref[...])

    return _tok_call(name, body, 1, lay.ns, [_tok(lay, DT_LANES, cb), _glob(1, DT_LANES)], _tok(lay, DT_LANES),
                     jax.ShapeDtypeStruct((lay.ta, DT_LANES), F32), (proj, bias))


def _dt_bwd(lay, mx, proj, bias, parts, dproj, name):
    cb = mx.off_dt // DT_LANES
    wb = DT_PAD if mx.off_dt % DT_PAD == 0 else DT_LANES
    ncb = DT_PAD // wb

    def body(p_ref, b_ref, a_ref, b2_ref, c_ref, d_ref, buf_ref, dp_ref, db_ref):
        j, s = pl.program_id(0), pl.program_id(1)

        @pl.when(j == 0)
        def _():
            ddt = (a_ref[...] + b2_ref[...]) + (c_ref[...] + d_ref[...])
            draw = ddt * jax.nn.sigmoid(p_ref[...] + b_ref[...])
            dp_ref[:, 0:DT_LANES] = draw.astype(dp_ref.dtype)
            if wb > DT_LANES:
                dp_ref[:, DT_LANES:] = jnp.zeros((lay.tt, wb - DT_LANES), dp_ref.dtype)
            _acc(db_ref, s == 0, jnp.sum(draw, axis=0, keepdims=True))

        @pl.when(j > 0)
        def _():
            dp_ref[...] = jnp.zeros_like(dp_ref)

    t = pl.BlockSpec((lay.tt, DT_LANES), lambda j, s: (s, 0))
    return _tok_call(name, body, ncb, lay.ns,
                     [pl.BlockSpec((lay.tt, DT_LANES), lambda j, s: (s, cb)), _glob(1, DT_LANES), t, t, t, t,
                      pl.BlockSpec(memory_space=pl.ANY)],
                     [_tok(lay, wb, mx.off_dt // wb), _glob(1, DT_LANES)],
                     [jax.ShapeDtypeStruct(dproj.shape, dproj.dtype), jax.ShapeDtypeStruct((1, DT_LANES), F32)],
                     (proj, bias) + tuple(parts) + (dproj,), aliases={6: 0})


def _scan_mask(rev):
    r = lax.broadcasted_iota(jnp.int32, (CHUNK, CHUNK), 0)
    c = lax.broadcasted_iota(jnp.int32, (CHUNK, CHUNK), 1)
    return (r <= c) if rev else (r >= c)


def _split_bf16(x):
    hi = x.astype(BF16)
    return hi, (x - hi.astype(F32)).astype(BF16)


@functools.partial(jax.custom_vjp, nondiff_argnums=(0,))
def _cum_cols(rev, x):
    m = _scan_mask(rev).astype(BF16)
    hi, lo = _split_bf16(x)
    return jnp.dot(m, hi, preferred_element_type=F32) + jnp.dot(m, lo, preferred_element_type=F32)


_cum_cols.defvjp(lambda rev, x: (_cum_cols(rev, x), None), lambda rev, _, g: (_cum_cols(not rev, g),))


@functools.partial(jax.custom_vjp, nondiff_argnums=(0,))
def _cum_rows(rev, x):
    m = _scan_mask(not rev).astype(BF16)
    hi, lo = _split_bf16(x)
    return jnp.dot(hi, m, preferred_element_type=F32) + jnp.dot(lo, m, preferred_element_type=F32)


_cum_rows.defvjp(lambda rev, x: (_cum_rows(rev, x), None), lambda rev, _, g: (_cum_rows(not rev, g),))


def _ssd_chunk(xh_pairs, bcs, ccs, dtc, dtr, a_row, a_col, st_pairs, *, rev, heads, col0):
    cs_c, cs_r, tot, scores = _ssd_shared(bcs, ccs, dtc, dtr, a_row, a_col, rev=rev)
    ppg = heads // (2 * SSD_GROUPS)
    ys, sts = [], []
    for g in range(SSD_GROUPS):
        y, st = _ssd_group(xh_pairs[g * ppg:(g + 1) * ppg], bcs[g], ccs[g], st_pairs[g], cs_c, cs_r, tot, dtc,
                           scores[g], rev=rev, col=col0 + 2 * ppg * g)
        ys.append(y)
        sts.append(st)
    return ys, sts


_NT = (((1,), (1,)), ((), ()))
_TN = (((0,), (0,)), ((), ()))


def _ssd_shared(bcs, ccs, dtc, dtr, a_row, a_col, *, rev):
    da_c = dtc * a_row
    cs_c = _cum_cols(rev, da_c)
    cs_r = _cum_rows(rev, dtr * a_col)
    tot = jnp.sum(da_c, axis=0, keepdims=True)
    scores = [lax.dot_general(ccs[g].astype(BF16), bcs[g].astype(BF16), _NT, preferred_element_type=F32)
              for g in range(SSD_GROUPS)]
    return cs_c, cs_r, tot, scores


def _ssd_group(xh_pairs, bc, cc, st, cs_c, cs_r, tot, dtc, score, *, rev, col):
    n = CHUNK
    mask = _scan_mask(rev)
    lane = lax.broadcasted_iota(jnp.int32, (n, DT_LANES), 1)
    sub = lax.broadcasted_iota(jnp.int32, (DT_LANES, n), 0)
    lane1 = lax.broadcasted_iota(jnp.int32, (1, DT_LANES), 1)
    left = lax.broadcasted_iota(jnp.int32, (n, 2 * HEAD_DIM), 1) < HEAD_DIM
    top = lax.broadcasted_iota(jnp.int32, (2 * HEAD_DIM, 1), 0) < HEAD_DIM
    xs_all, wst_all, ecs_all, edec_all, y_diag = [], [], [], [], []
    for p, xh in enumerate(xh_pairs):
        per = []
        for c in (col + 2 * p, col + 2 * p + 1):
            csv = jnp.sum(jnp.where(lane == c, cs_c, 0.0), axis=1, keepdims=True)
            csr = jnp.sum(jnp.where(sub == c, cs_r, 0.0), axis=0, keepdims=True)
            dtv = jnp.sum(jnp.where(lane == c, dtc, 0.0), axis=1, keepdims=True)
            tv = jnp.sum(jnp.where(lane1 == c, tot, 0.0), axis=1, keepdims=True)
            m = score * jnp.exp(jnp.where(mask, csv - csr, NEG_BIG))
            per.append((csv, dtv, tv, m))
        (cs1, dt1, t1, m1), (cs2, dt2, t2, m2) = per
        xs = xh * jnp.where(left, dt1, dt2)
        both = jnp.dot(jnp.concatenate([m1, m2], axis=0).astype(BF16), xs.astype(BF16), preferred_element_type=F32)
        y_diag.append(jnp.where(left, both[:n], both[n:]))
        xs_all.append(xs)
        ecs_all.append(jnp.where(left, jnp.exp(cs1), jnp.exp(cs2)))
        wst_all.append(jnp.where(left, jnp.exp(t1 - cs1), jnp.exp(t2 - cs2)))
        edec_all.append(jnp.where(top, jnp.exp(t1), jnp.exp(t2)))
    cat = lambda parts, axis: parts[0] if len(parts) == 1 else jnp.concatenate(parts, axis=axis)
    xs, wst, ecs = cat(xs_all, 1), cat(wst_all, 1), cat(ecs_all, 1)
    y_off = lax.dot_general(cc.astype(BF16), st.astype(BF16), _NT, preferred_element_type=F32) * ecs
    cst = lax.dot_general((xs * wst).astype(BF16), bc.astype(BF16), _TN, preferred_element_type=F32)
    return cat(y_diag, 1) + y_off, st * cat(edec_all, 0) + cst


class _Scan:
    def __init__(self, lay, rev):
        self.ncx, self.ncc, self.bl, self.rev = lay.seq // CHUNK, lay.clen // CHUNK, lay.bl, rev
        self.nct = self.ncx + self.ncc

    def chunk(self, b, pos):
        kc = (self.ncc - 1 - pos) if self.rev else pos
        kx = (self.ncx - 1 - (pos - self.ncc)) if self.rev else (pos - self.ncc)
        return jnp.where(pos < self.ncc, self.bl * self.ncx + b * self.ncc + kc, b * self.ncx + kx)


def _ssd_io(mx, x_ref, st_src):
    np_ = mx.heads // 2
    d = mx.d_ssm
    xh = [x_ref[:, 128 * p:128 * (p + 1)] for p in range(np_)]
    bcs = [x_ref[:, d + N_STATE * g:d + N_STATE * (g + 1)] for g in range(SSD_GROUPS)]
    ccs = [x_ref[:, d + mx.gn + N_STATE * g:d + mx.gn + N_STATE * (g + 1)] for g in range(SSD_GROUPS)]
    gw = d // SSD_GROUPS
    sts = [st_src[gw * g:gw * (g + 1), :] for g in range(SSD_GROUPS)]
    return xh, bcs, ccs, sts


def _ssd_fwd(lay, mx, xbc, dt, dtt, a_row, a_col, rev, name):
    sc = _Scan(lay, rev)
    col0 = mx.heads if rev else 0
    hp = mx.heads * HEAD_DIM

    def body(x_ref, dt_ref, dtt_ref, ar_ref, ac_ref, y_ref, hp_ref, st_ref):
        @pl.when(pl.program_id(1) == 0)
        def _():
            st_ref[...] = jnp.zeros_like(st_ref)

        hp_ref[...] = st_ref[...]
        xh, bcs, ccs, sts = _ssd_io(mx, x_ref, st_ref)
        ys, new = _ssd_chunk(xh, bcs, ccs, dt_ref[...], dtt_ref[...], ar_ref[...], ac_ref[...], sts,
                             rev=rev, heads=mx.heads, col0=col0)
        gw = mx.d_ssm // SSD_GROUPS
        for g in range(SSD_GROUPS):
            y_ref[:, gw * g:gw * (g + 1)] = ys[g]
            st_ref[gw * g:gw * (g + 1), :] = new[g]

    ch = sc.chunk
    return pl.pallas_call(
        body, grid=(lay.bl, sc.nct),
        in_specs=[pl.BlockSpec((CHUNK, mx.xw), lambda b, i: (ch(b, i), 0)),
                  pl.BlockSpec((CHUNK, DT_LANES), lambda b, i: (ch(b, i), 0)),
                  pl.BlockSpec((DT_LANES, CHUNK), lambda b, i: (0, ch(b, i))),
                  pl.BlockSpec((1, DT_LANES), lambda b, i: (0, 0)),
                  pl.BlockSpec((DT_LANES, 1), lambda b, i: (0, 0))],
        out_specs=[pl.BlockSpec((CHUNK, mx.d_ssm), lambda b, i: (ch(b, i), 0)),
                   pl.BlockSpec((hp, N_STATE), lambda b, i: (b * sc.nct + i, 0))],
        out_shape=[jax.ShapeDtypeStruct((lay.ta, mx.d_ssm), F32),
                   jax.ShapeDtypeStruct((lay.bl * sc.nct * hp, N_STATE), F32)],
        scratch_shapes=[pltpu.VMEM((hp, N_STATE), F32)], name=name, compiler_params=_cparams(2),
    )(xbc, dt, dtt, a_row, a_col)


def _ssd_bwd(lay, mx, xbc, dt, dtt, a_row, a_col, hprev, dy, dskip, rev, name):
    sc = _Scan(lay, rev)
    col0 = mx.heads if rev else 0
    hp = mx.heads * HEAD_DIM
    np_ = mx.heads // 2
    d = mx.d_ssm
    with_skip = dskip is not None

    def body(*refs):
        if with_skip:
            x_ref, dt_ref, dtt_ref, ar_ref, ac_ref, hp_ref, dy_ref, sk_ref = refs[:8]
            rest = refs[8:]
        else:
            x_ref, dt_ref, dtt_ref, ar_ref, ac_ref, hp_ref, dy_ref = refs[:7]
            rest = refs[7:]
        dx_ref, ddc_ref, ddr_ref, dar_ref, dac_ref, ds_ref = rest
        b, i = pl.program_id(0), pl.program_id(1)

        @pl.when(i == 0)
        def _():
            ds_ref[...] = jnp.zeros_like(ds_ref)

        xh, bcs, ccs, sts = _ssd_io(mx, x_ref, hp_ref)
        dtc = dt_ref[...]
        shared, vjp_shared = jax.vjp(functools.partial(_ssd_shared, rev=rev), bcs, ccs, dtc, dtt_ref[...],
                                     ar_ref[...], ac_ref[...])
        cs_c, cs_r, tot, scores = shared
        plus = lambda acc, v: v if acc is None else acc + v
        d_cs_c = d_cs_r = d_tot = ddc = None
        d_scores, dbc, dcc = [], [], []
        ppg = np_ // SSD_GROUPS
        gw = d // SSD_GROUPS
        for g in range(SSD_GROUPS):
            dyg = dy_ref[:, gw * g:gw * (g + 1)]
            fn = functools.partial(_ssd_group, rev=rev, col=col0 + 2 * ppg * g)
            _, vjp = jax.vjp(fn, xh[g * ppg:(g + 1) * ppg], bcs[g], ccs[g], sts[g], cs_c, cs_r, tot, dtc, scores[g])
            dxh, dbc_g, dcc_g, dst, dcs_c_g, dcs_r_g, dtot_g, ddc_g, dsc_g = vjp((dyg, ds_ref[gw * g:gw * (g + 1), :]))
            for q in range(ppg):
                p = g * ppg + q
                v = dxh[q]
                if with_skip:
                    v = v + dyg[:, 128 * q:128 * (q + 1)] * sk_ref[:, 128 * p:128 * (p + 1)]
                dx_ref[:, 128 * p:128 * (p + 1)] = v
            ds_ref[gw * g:gw * (g + 1), :] = dst
            d_cs_c, d_cs_r, d_tot, ddc = plus(d_cs_c, dcs_c_g), plus(d_cs_r, dcs_r_g), plus(d_tot, dtot_g), plus(ddc, ddc_g)
            d_scores.append(dsc_g)
            dbc.append(dbc_g)
            dcc.append(dcc_g)
        dbc_s, dcc_s, ddc_s, ddr, dar, dac = vjp_shared((d_cs_c, d_cs_r, d_tot, d_scores))
        ddc = ddc + ddc_s
        dbc = [dbc[g] + dbc_s[g] for g in range(SSD_GROUPS)]
        dcc = [dcc[g] + dcc_s[g] for g in range(SSD_GROUPS)]
        for g in range(SSD_GROUPS):
            dx_ref[:, d + N_STATE * g:d + N_STATE * (g + 1)] = dbc[g]
            dx_ref[:, d + mx.gn + N_STATE * g:d + mx.gn + N_STATE * (g + 1)] = dcc[g]
        ddc_ref[...] = ddc
        ddr_ref[...] = ddr
        first = jnp.logical_and(b == 0, i == 0)
        _acc(dar_ref, first, dar)
        _acc(dac_ref, first, dac)

    ch = lambda b, i: sc.chunk(b, sc.nct - 1 - i)
    in_specs = [pl.BlockSpec((CHUNK, mx.xw), lambda b, i: (ch(b, i), 0)),
                pl.BlockSpec((CHUNK, DT_LANES), lambda b, i: (ch(b, i), 0)),
                pl.BlockSpec((DT_LANES, CHUNK), lambda b, i: (0, ch(b, i))),
                pl.BlockSpec((1, DT_LANES), lambda b, i: (0, 0)),
                pl.BlockSpec((DT_LANES, 1), lambda b, i: (0, 0)),
                pl.BlockSpec((hp, N_STATE), lambda b, i: (b * sc.nct + sc.nct - 1 - i, 0)),
                pl.BlockSpec((CHUNK, d), lambda b, i: (ch(b, i), 0))]
    inputs = [xbc, dt, dtt, a_row, a_col, hprev, dy]
    if with_skip:
        in_specs.append(pl.BlockSpec((1, d), lambda b, i: (0, 0)))
        inputs.append(dskip)
    return pl.pallas_call(
        body, grid=(lay.bl, sc.nct), in_specs=in_specs,
        out_specs=[pl.BlockSpec((CHUNK, mx.xw), lambda b, i: (ch(b, i), 0)),
                   pl.BlockSpec((CHUNK, DT_LANES), lambda b, i: (ch(b, i), 0)),
                   pl.BlockSpec((DT_LANES, CHUNK), lambda b, i: (0, ch(b, i))),
                   pl.BlockSpec((1, DT_LANES), lambda b, i: (0, 0)),
                   pl.BlockSpec((DT_LANES, 1), lambda b, i: (0, 0))],
        out_shape=[jax.ShapeDtypeStruct((lay.ta, mx.xw), F32), jax.ShapeDtypeStruct((lay.ta, DT_LANES), F32),
                   jax.ShapeDtypeStruct((DT_LANES, lay.ta), F32), jax.ShapeDtypeStruct((1, DT_LANES), F32),
                   jax.ShapeDtypeStruct((DT_LANES, 1), F32)],
        scratch_shapes=[pltpu.VMEM((hp, N_STATE), F32)], name=name, compiler_params=_cparams(2),
    )(*inputs)


def _gate_f(yf, yb, xh, z, drow, nw):
    dd = yf.shape[-1]
    half = dd // SSD_GROUPS
    yz = (yf + yb + drow * xh) * _silu(z)
    lo = lax.broadcasted_iota(jnp.int32, yz.shape, 1) < half
    sq = yz * yz
    ms1 = jnp.sum(jnp.where(lo, sq, 0.0), axis=-1, keepdims=True) / half
    ms2 = jnp.sum(jnp.where(lo, 0.0, sq), axis=-1, keepdims=True) / half
    return yz * jnp.where(lo, lax.rsqrt(ms1 + EPS), lax.rsqrt(ms2 + EPS)) * nw


def _gate_fwd(lay, mx, yf, yb, xbc, proj, drow, nw, name):
    d = mx.d_ssm

    def body(yf_ref, yb_ref, xh_ref, z_ref, d_ref, w_ref, o_ref):
        o_ref[...] = _gate_f(yf_ref[...], yb_ref[...], xh_ref[...], z_ref[...], d_ref[...], w_ref[...]).astype(o_ref.dtype)

    t, z = _tok(lay, d), _tok(lay, d, mx.off_z // d)
    return _tok_call(name, body, 1, lay.nsx, [t, t, t, z, _glob(1, d), _glob(1, d)], t,
                     jax.ShapeDtypeStruct((lay.tx, d + mx.d_conv), BF16), (yf, yb, xbc, proj, drow, nw))


def _gate_bwd(lay, mx, yf, yb, xbc, proj, drow, nw, dcat, name):
    d = mx.d_ssm
    nsx = lay.nsx

    def body(yf_ref, yb_ref, xh_ref, z_ref, d_ref, w_ref, dc_ref, dy_ref, dz_ref, dd_ref, dw_ref):
        s = pl.program_id(1)

        @pl.when(s < nsx)
        def _():
            _, vjp = jax.vjp(_gate_f, yf_ref[...], yb_ref[...], xh_ref[...], z_ref[...], d_ref[...], w_ref[...])
            dyf, _, _, dz, dd, dw = vjp(dc_ref[...])
            dy_ref[...] = dyf
            dz_ref[...] = dz.astype(dz_ref.dtype)
            _acc(dd_ref, s == 0, dd)
            _acc(dw_ref, s == 0, dw)

        @pl.when(s >= nsx)
        def _():
            dy_ref[...] = jnp.zeros_like(dy_ref)
            dz_ref[...] = jnp.zeros_like(dz_ref)

    t, z = _tok(lay, d), _tok(lay, d, mx.off_z // d)
    return _tok_call(name, body, 1, lay.ns, [t, t, t, z, _glob(1, d), _glob(1, d), _tok(lay, d, clamp=nsx - 1)],
                     [t, z, _glob(1, d), _glob(1, d)],
                     [jax.ShapeDtypeStruct((lay.ta, d), F32), jax.ShapeDtypeStruct((lay.ta, mx.pw), BF16),
                      jax.ShapeDtypeStruct((1, d), F32), jax.ShapeDtypeStruct((1, d), F32)],
                     (yf, yb, xbc, proj, drow, nw, dcat))


def _glu_fwd(lay, mx, proj, name):
    d = mx.d_conv
    c = math.gcd(mx.off_glu, d)
    cb = mx.off_glu // c

    def body(a_ref, b_ref, o_ref):
        o_ref[...] = a_ref[...] * jax.nn.sigmoid(b_ref[...])

    return _tok_call(name, body, d // c, lay.nsx, [_tok(lay, c, cb), _tok(lay, c, cb + d // c)], _tok(lay, c),
                     jax.ShapeDtypeStruct((lay.tx, d), F32), (proj, proj))


def _glu_bwd(lay, mx, proj, du, dproj, name):
    d = mx.d_conv
    nsx = lay.nsx
    whole = mx.off_glu % (2 * d) == 0
    c = d if whole else math.gcd(mx.off_glu, d)
    cb = mx.off_glu // c
    nc = d // c

    def body(a_ref, b_ref, du_ref, buf_ref, o_ref):
        j, s = pl.program_id(0), pl.program_id(1)

        @pl.when(s < nsx)
        def _():
            sg = jax.nn.sigmoid(b_ref[...])
            da = du_ref[...] * sg
            db = da * a_ref[...] * (1.0 - sg)
            if whole:
                o_ref[:, 0:d] = da.astype(o_ref.dtype)
                o_ref[:, d:] = db.astype(o_ref.dtype)
            else:
                o_ref[...] = jnp.where(j < nc, da, db).astype(o_ref.dtype)

        @pl.when(s >= nsx)
        def _():
            o_ref[...] = jnp.zeros_like(o_ref)

    win = lambda half: pl.BlockSpec((lay.tt, c), lambda j, s: (s, cb + half * nc + j % nc))
    out = _tok(lay, 2 * d, mx.off_glu // (2 * d)) if whole else _tok(lay, c, cb)
    return _tok_call(name, body, 1 if whole else 2 * nc, lay.ns,
                     [win(0), win(1), pl.BlockSpec((lay.tt, c), lambda j, s: (jnp.minimum(s, nsx - 1), j % nc)),
                      pl.BlockSpec(memory_space=pl.ANY)],
                     out, jax.ShapeDtypeStruct(dproj.shape, dproj.dtype), (proj, proj, du, dproj), aliases={3: 0})


def _axial(lay, mx, u, dy, cw, cb, name):
    d, seq = mx.d_conv, lay.seq
    kw = cw.shape[0]
    pad = kw // 2
    c = _pick(d // 2, (256, 128))
    ncb = d // c
    zpad = GRID_W * pad
    zpad = -(-zpad // 8) * 8
    backward = dy is not None

    def shifted(ext_ref, off):
        return ext_ref[pl.ds(zpad + off, seq), :]

    def valid_row(off):
        col = lax.broadcasted_iota(jnp.int32, (seq, c), 0) % GRID_W
        return jnp.logical_and(col + off >= 0, col + off < GRID_W)

    def fill(ext_ref, v):
        ext_ref[0:zpad, :] = jnp.zeros((zpad, c), F32)
        ext_ref[zpad:zpad + seq, :] = v
        ext_ref[zpad + seq:, :] = jnp.zeros((zpad, c), F32)

    def conv(ext_ref, w_ref, is_row, sign):
        acc = jnp.zeros((seq, c), F32)
        for k in range(kw):
            off = sign * ((k - pad) if is_row else GRID_W * (k - pad))
            v = shifted(ext_ref, off)
            if is_row:
                v = jnp.where(valid_row(off), v, 0.0)
            acc = acc + w_ref[k:k + 1, :] * v
        return acc

    def fwd_body(u_ref, w_ref, b_ref, o_ref, ext_ref):
        j = pl.program_id(0)
        fill(ext_ref, u_ref[...])

        @pl.when(j < ncb // 2)
        def _():
            o_ref[...] = conv(ext_ref, w_ref, True, 1) + b_ref[...]

        @pl.when(j >= ncb // 2)
        def _():
            o_ref[...] = conv(ext_ref, w_ref, False, 1) + b_ref[...]

    def bwd_body(u_ref, dy_ref, w_ref, du_ref, dw_ref, db_ref, extu_ref, extd_ref):
        j, b = pl.program_id(0), pl.program_id(1)
        dyv = dy_ref[...]
        fill(extu_ref, u_ref[...])
        fill(extd_ref, dyv)

        def grads(is_row):
            du_ref[...] = conv(extd_ref, w_ref, is_row, -1)
            rows = []
            for k in range(kw):
                off = (k - pad) if is_row else GRID_W * (k - pad)
                v = shifted(extu_ref, off)
                if is_row:
                    v = jnp.where(valid_row(off), v, 0.0)
                rows.append(jnp.sum(dyv * v, axis=0, keepdims=True))
            _acc(dw_ref, b == 0, jnp.concatenate(rows, axis=0))

        @pl.when(j < ncb // 2)
        def _():
            grads(True)

        @pl.when(j >= ncb // 2)
        def _():
            grads(False)

        _acc(db_ref, b == 0, jnp.sum(dyv, axis=0, keepdims=True))

    seq_spec = pl.BlockSpec((seq, c), lambda j, b: (b, j))
    w_spec = pl.BlockSpec((kw, c), lambda j, b: (0, j))
    b_spec = pl.BlockSpec((1, c), lambda j, b: (0, j))
    ext = pltpu.VMEM((seq + 2 * zpad, c), F32)
    if not backward:
        return pl.pallas_call(fwd_body, grid=(ncb, lay.bl), in_specs=[seq_spec, w_spec, b_spec], out_specs=seq_spec,
                              out_shape=jax.ShapeDtypeStruct((lay.tx, d), F32), scratch_shapes=[ext], name=name,
                              compiler_params=_cparams(2))(u, cw, cb)
    return pl.pallas_call(bwd_body, grid=(ncb, lay.bl), in_specs=[seq_spec, seq_spec, w_spec],
                          out_specs=[seq_spec, w_spec, b_spec],
                          out_shape=[jax.ShapeDtypeStruct((lay.tx, d), F32), jax.ShapeDtypeStruct((kw, d), F32),
                                     jax.ShapeDtypeStruct((1, d), F32)],
                          scratch_shapes=[ext, ext], name=name, compiler_params=_cparams(2))(u, dy, cw)


def _ln_silu_f(u, w, b):
    mu = jnp.mean(u, axis=-1, keepdims=True)
    var = jnp.mean(jnp.square(u - mu), axis=-1, keepdims=True)
    return _silu((u - mu) * lax.rsqrt(var + EPS) * w + b)


def _ln_fwd(lay, mx, u, w, b, cat, name):
    d = mx.d_conv
    assert mx.d_ssm % d == 0

    def body(u_ref, w_ref, b_ref, cat_ref, o_ref):
        o_ref[...] = _ln_silu_f(u_ref[...], w_ref[...], b_ref[...]).astype(o_ref.dtype)

    return _tok_call(name, body, 1, lay.nsx,
                     [_tok(lay, d), _glob(1, d), _glob(1, d), pl.BlockSpec(memory_space=pl.ANY)],
                     _tok(lay, d, mx.d_ssm // d), jax.ShapeDtypeStruct(cat.shape, cat.dtype), (u, w, b, cat),
                     aliases={3: 0})


def _ln_bwd(lay, mx, u, w, b, dcat, name):
    d = mx.d_conv

    def body(u_ref, w_ref, b_ref, dc_ref, du_ref, dw_ref, db_ref):
        s = pl.program_id(1)
        _, vjp = jax.vjp(_ln_silu_f, u_ref[...], w_ref[...], b_ref[...])
        du, dw, db = vjp(dc_ref[...])
        du_ref[...] = du
        _acc(dw_ref, s == 0, dw)
        _acc(db_ref, s == 0, db)

    return _tok_call(name, body, 1, lay.nsx, [_tok(lay, d), _glob(1, d), _glob(1, d), _tok(lay, d, 1)],
                     [_tok(lay, d), _glob(1, d), _glob(1, d)],
                     [jax.ShapeDtypeStruct((lay.tx, d), F32), jax.ShapeDtypeStruct((1, d), F32),
                      jax.ShapeDtypeStruct((1, d), F32)], (u, w, b, dcat))


def _gate_tile(dff):
    return dff // 2 if (dff // 2) % 128 == 0 else dff


def _ffn_fwd(lay, nseg, x, nw, modv, k0, wts, tag, ride=None, ride_down=None, h=None, then_norm=None):
    wgu, wd, ft = wts
    if h is None:
        h = _norm_mod_fwd(lay, nseg, x, nw, modv, k0, tag + "_norm")
    t = h.shape[0]

    def act(acc):
        g, u = acc[:, :ft], acc[:, ft:]
        sg = jax.nn.sigmoid(g)
        sl = g * sg
        return jnp.concatenate([u * (sg * (1.0 + g * (1.0 - sg))), sl], axis=1), sl * u

    res = _mm(h, wgu, "nn", None, tag + "_gu", tn=2 * ft, tm=256 if t % 256 == 0 else None,
              epilogue=act, outs=[(2 * ft, BF16), (ft, BF16)], ride=ride)
    (s, a), rode = res if ride else (res, None)
    if callable(wd):
        wd = wd(rode)
    res = _mm_resid(lay, a, wd, x, modv, k0 + 2, 0.5, tag + "_down", then_norm=then_norm, ride=ride_down)
    (o, y, *h_next), rode_down = res if ride_down else (res, None)
    return y, (x, h, s, a, o), (rode, rode_down), (h_next[0] if h_next else None)


def _ffn_bwd(lay, nseg, dy, saved, nw, modv, k0, wts, tag, nout=None, rides=(None, None)):
    wgu, wd, ft = wts
    x, h, s, a, o = saved
    do, dgate = _resid_bwd(lay, nseg, dy, o, modv, k0 + 2, 0.5, tag + "_dres")

    def through_act(da, s_tile):
        return (jnp.concatenate([da, da], axis=1) * s_tile.astype(F32),)

    (dgu,) = _mm(do, wd.T, "nn", None, tag + "_da", tn=ft, tm=_big_tile(do.shape[0], 1024), extras=[(s, 2 * ft)],
                 epilogue=through_act, outs=[(2 * ft, BF16)])
    dwd = _mm(a, do, "tn", F32, tag + "_dwd")
    dh = _mm(dgu, wgu, "nt", F32, tag + "_dh", ride=rides[0])
    dwgu = _mm(h, dgu, "tn", F32, tag + "_dwgu", ride=rides[1])
    (dh, rode_a), (dwgu, rode_b) = (dh if rides[0] else (dh, None)), (dwgu if rides[1] else (dwgu, None))
    dx, dnw, dss = _norm_mod_bwd(lay, nseg, nseg, x, nw, modv, k0, dh, dy, tag + "_dnorm", nout=nout)
    return dx, (dwgu, dwd), dnw, jnp.concatenate([dss, dgate], axis=1), (rode_a, rode_b)


def _local_step(lay, mx, xa, target, modv, w, exch=None):
    d, bl = lay.d, lay.bl
    g = {}
    xa1, ffn1, late, ha = _ffn_fwd(lay, lay.ns, xa, w["norm_ffn1"], modv, 0, w["ffn1"], "ffn1",
                                   ride=exch.late_a if exch else None, ride_down=exch.late_b if exch else None,
                                   then_norm=(w["norm_mix"], 3))
    if exch:
        w = {**w, **exch.unpack_late(w["ffn1"], *late)}
    proj = _mm(ha, w["w_in"], "nn", F32, "mix_in")
    pre, xbc = _conv5_fwd(lay.fine(), mx, proj, w["conv_w"], w["conv_b"], "mix_conv")
    dt = _dt_fwd(lay, mx, proj, w["dt_bias"], "mix_dt")
    dtt = dt.T
    yf, hpf = _ssd_fwd(lay, mx, xbc, dt, dtt, w["a_row"], w["a_col"], False, "ssd_f")
    yb, hpb = _ssd_fwd(lay, mx, xbc, dt, dtt, w["a_row"], w["a_col"], True, "ssd_b")
    cat_y = _gate_fwd(lay, mx, yf, yb, xbc, proj, w["d_row"], w["ssm_norm_w"], "mix_gate")
    u0 = _glu_fwd(lay, mx, proj, "mix_glu")
    uc = _axial(lay, mx, u0, None, w["cconv_w"], w["cconv_b"], "mix_axial")
    cat = _ln_fwd(lay, mx, uc, w["ln_w"], w["ln_b"], cat_y, "mix_ln")
    mix, x2, h2 = _mm_resid(lay, cat, w["w_out"], xa1, modv, 5, 1.0, "mix_out", then_norm=(w["norm_ffn2"], 6))
    x3, ffn2, _, _ = _ffn_fwd(lay, lay.nsx, x2, w["norm_ffn2"], modv, 6, w["ffn2"], "ffn2", h=h2)
    loss, dx3, g["final_norm"] = _final_loss(lay, x3, w["final_norm"], target, "loss")
    dx2, g["ffn2"], g["norm_ffn2"], dmod2, _ = _ffn_bwd(lay, lay.nsx, dx3, ffn2, w["norm_ffn2"], modv, 6, w["ffn2"], "ffn2")
    dmix, dg2 = _resid_bwd(lay, lay.nsx, dx2, mix, modv, 5, 1.0, "mix_dres")
    dcat = _mm(dmix, w["w_out"], "nt", F32, "mix_dcat")
    g["w_out"] = _mm(cat, dmix, "tn", F32, "mix_dwout")
    duc, g["ln_w"], g["ln_b"] = _ln_bwd(lay, mx, uc, w["ln_w"], w["ln_b"], dcat, "mix_dln")
    du0, g["cconv_w"], g["cconv_b"] = _axial(lay, mx, u0, duc, w["cconv_w"], None, "mix_daxial")
    dyssd, dproj, g["d_row"], g["ssm_norm_w"] = _gate_bwd(lay, mx, yf, yb, xbc, proj, w["d_row"], w["ssm_norm_w"], dcat,
                                                          "mix_dgate")
    dproj = _glu_bwd(lay, mx, proj, du0, dproj, "mix_dglu")
    dxf, ddcf, ddrf, darf, dacf = _ssd_bwd(lay, mx, xbc, dt, dtt, w["a_row"], w["a_col"], hpf, dyssd, w["d_row"],
                                           False, "ssd_df")
    dxb, ddcb, ddrb, darb, dacb = _ssd_bwd(lay, mx, xbc, dt, dtt, w["a_row"], w["a_col"], hpb, dyssd, None,
                                           True, "ssd_db")
    g["a_row"] = (darf + darb) + (dacf + dacb).T
    dproj, g["conv_w"], g["conv_b"] = _conv5_bwd(lay.fine(), mx, proj, pre, dxf, dxb, w["conv_w"], dproj, "mix_dconv")
    dproj, g["dt_bias"] = _dt_bwd(lay, mx, proj, w["dt_bias"], (ddcf, ddcb, ddrf.T, ddrb.T), dproj, "mix_ddt")
    dha = _mm(dproj, w["w_in"], "nt", F32, "mix_dha")
    g["w_in"] = _mm(ha, dproj, "tn", F32, "mix_dwin")
    dxa1, g["norm_mix"], dss_mix = _norm_mod_bwd(lay, lay.ns, lay.nsx, xa1, w["norm_mix"], modv, 3, dha, dx2, "mix_dnorm")
    rides = exch.early_grads(g) if exch else (None, None)
    dx, g["ffn1"], g["norm_ffn1"], dmod1, rode = _ffn_bwd(lay, lay.ns, dxa1, ffn1, w["norm_ffn1"], modv, 0, w["ffn1"],
                                                         "ffn1", nout=lay.nsx, rides=rides)
    if exch:
        exch.take_early(rode)
    zrow = lambda t: jnp.concatenate([t, jnp.zeros((1,) + t.shape[1:], F32)], axis=0)
    dmodv = jnp.concatenate([dmod1, dss_mix, zrow(dg2), zrow(dmod2)], axis=1)
    return loss, dx, g, dmodv


class _GatherRide:
    def __init__(self, xs):
        self.arrays = list(xs)
        self.na = len(xs)
        self.out_shapes = [jax.ShapeDtypeStruct((N_DEV,) + tuple(x.shape), x.dtype) for x in xs]
        self.sems = [pltpu.SemaphoreType.DMA((7 * self.na,)), pltpu.SemaphoreType.DMA((7 * self.na,)),
                     pltpu.SemaphoreType.DMA((self.na,))]

    def _plan(self, x_refs, out_refs, send_sems, recv_sems, local_sems):
        mx_, my_, mc_ = lax.axis_index("x"), lax.axis_index("y"), lax.axis_index("c")
        me, sibling = (mx_, my_, mc_), (mx_, my_, 1 - mc_)
        chips = [(1 - mx_, my_), (mx_, 1 - my_), (1 - mx_, 1 - my_)]

        def slot(a, px, py, pc):
            return out_refs[a].at[4 * px + 2 * py + pc]

        def copy(a, k, block, to, own=False):
            return pltpu.make_async_remote_copy(
                src_ref=x_refs[a] if own else slot(a, *block), dst_ref=slot(a, *block),
                send_sem=send_sems.at[7 * a + k], recv_sem=recv_sems.at[7 * a + k], device_id=to, device_id_type=MESH)

        mine = [pltpu.make_async_copy(x_refs[a], slot(a, *me), local_sems.at[a]) for a in range(self.na)]
        first = []
        for a in range(self.na):
            first.append(copy(a, 0, me, sibling, own=True))
            first += [copy(a, 1 + j, me, (*chip, mc_), own=True) for j, chip in enumerate(chips)]
        return me, sibling, chips, mc_, copy, mine, first

    def start(self, x_refs, out_refs, send_sems, recv_sems, local_sems):
        *_, mine, first = self._plan(x_refs, out_refs, send_sems, recv_sems, local_sems)
        for cp in mine + first:
            cp.start()

    def finish(self, x_refs, out_refs, send_sems, recv_sems, local_sems):
        me, sibling, chips, mc_, copy, mine, first = self._plan(x_refs, out_refs, send_sems, recv_sems, local_sems)
        passed = []
        for j, chip in enumerate(chips):
            for a in range(self.na):
                copy(a, 1 + j, (*chip, mc_), me).wait_recv()
                fwd = copy(a, 4 + j, (*chip, mc_), sibling)
                fwd.start()
                passed.append(fwd)
        for a in range(self.na):
            copy(a, 0, sibling, me).wait_recv()
            for j, chip in enumerate(chips):
                copy(a, 4 + j, (*chip, 1 - mc_), me).wait_recv()
        for cp in first + passed:
            cp.wait_send()
        for cp in mine:
            cp.wait()


def _exchange(ride, name, in_hbm=True):
    n_in, n_out = len(ride.arrays), len(ride.out_shapes)

    def body(*refs):
        ins, outs, sems = refs[:n_in], refs[n_in:n_in + n_out], refs[n_in + n_out:]
        ride.start(ins, outs, *sems)
        ride.finish(ins, outs, *sems)

    space = pl.BlockSpec(memory_space=pl.ANY if in_hbm else pltpu.VMEM)
    return pl.pallas_call(body, out_shape=list(ride.out_shapes), in_specs=[space] * n_in, out_specs=[space] * n_out,
                          scratch_shapes=list(ride.sems), name=name)(*ride.arrays)


def _all_gather(xs, name, in_hbm):
    return _exchange(_GatherRide(xs), name, in_hbm)


N_CHIPS = 4


def _swap_sibling(gs, name):
    na = len(gs)

    def body(*refs):
        g_refs, out_refs, send_sems, recv_sems = refs[:na], refs[na:2 * na], refs[2 * na], refs[2 * na + 1]
        mx_, my_, mc_ = lax.axis_index("x"), lax.axis_index("y"), lax.axis_index("c")
        copies = [pltpu.make_async_remote_copy(
            src_ref=g_refs[a].at[k, 1 - mc_], dst_ref=out_refs[a].at[k], send_sem=send_sems.at[N_CHIPS * a + k],
            recv_sem=recv_sems.at[N_CHIPS * a + k], device_id=(mx_, my_, 1 - mc_), device_id_type=MESH)
            for a in range(na) for k in range(N_CHIPS)]
        for cp in copies:
            cp.start()
        for cp in copies:
            cp.wait_recv()
        for cp in copies:
            cp.wait_send()

    return pl.pallas_call(
        body, out_shape=[jax.ShapeDtypeStruct((N_CHIPS,) + tuple(g.shape[2:]), g.dtype) for g in gs],
        in_specs=[pl.BlockSpec(memory_space=pl.ANY)] * na, out_specs=[pl.BlockSpec(memory_space=pl.ANY)] * na,
        scratch_shapes=[pltpu.SemaphoreType.DMA((N_CHIPS * na,)), pltpu.SemaphoreType.DMA((N_CHIPS * na,))], name=name,
    )(*gs)


def _row_tile(r, n):
    if r * n * 4 <= (1 << 20):
        return r
    for t in (1024, 512, 256, 128, 64, 32, 16, 8):
        if r % t == 0 and t * n * 4 <= (1 << 20):
            return t
    return r


def _pair_add(place, g, got, name):
    _, _, r, n = g.shape
    tr = r if r * n * 4 <= (3 << 19) else _row_tile(r, n)

    def body(place_ref, g_ref, got_ref, o_ref, ob_ref):
        s = g_ref[...] + got_ref[...]
        o_ref[...] = s
        ob_ref[...] = s.astype(ob_ref.dtype)

    blk = pl.BlockSpec((None, tr, n), lambda k, i, pr: (k, i, 0))
    grid_spec = pltpu.PrefetchScalarGridSpec(
        num_scalar_prefetch=1, grid=(N_CHIPS, r // tr),
        in_specs=[pl.BlockSpec((None, None, tr, n), lambda k, i, pr: (k, pr[0], i, 0)), blk], out_specs=[blk, blk])
    return pl.pallas_call(body, grid_spec=grid_spec,
                          out_shape=[jax.ShapeDtypeStruct((N_CHIPS, r, n), F32), jax.ShapeDtypeStruct((N_CHIPS, r, n), BF16)],
                          name=name, compiler_params=_cparams(2))(place, g, got)


class _ChipSwapRide:
    def __init__(self, ps):
        self.arrays = list(ps)
        self.na = len(ps)
        self.out_shapes = [jax.ShapeDtypeStruct((N_CHIP_PEERS,) + tuple(p.shape[1:]), p.dtype) for p in ps]
        self.sems = [pltpu.SemaphoreType.DMA((N_CHIP_PEERS * self.na,)), pltpu.SemaphoreType.DMA((N_CHIP_PEERS * self.na,))]

    def _copies(self, p_refs, out_refs, send_sems, recv_sems):
        mx_, my_, mc_ = lax.axis_index("x"), lax.axis_index("y"), lax.axis_index("c")
        chips = [(1 - mx_, my_), (mx_, 1 - my_), (1 - mx_, 1 - my_)]
        return [pltpu.make_async_remote_copy(
            src_ref=p_refs[a].at[2 * cx + cy], dst_ref=out_refs[a].at[j], send_sem=send_sems.at[N_CHIP_PEERS * a + j],
            recv_sem=recv_sems.at[N_CHIP_PEERS * a + j], device_id=(cx, cy, mc_), device_id_type=MESH)
            for a in range(self.na) for j, (cx, cy) in enumerate(chips)]

    def start(self, p_refs, out_refs, send_sems, recv_sems):
        for cp in self._copies(p_refs, out_refs, send_sems, recv_sems):
            cp.start()

    def finish(self, p_refs, out_refs, send_sems, recv_sems):
        copies = self._copies(p_refs, out_refs, send_sems, recv_sems)
        for cp in copies:
            cp.wait_recv()
        for cp in copies:
            cp.wait_send()


def _sum_lead(x, name):
    k, r, n = x.shape
    tr = _row_tile(r, n * k)

    def body(x_ref, o_ref):
        acc = x_ref[0]
        for i in range(1, k):
            acc = acc + x_ref[i]
        o_ref[...] = acc

    return pl.pallas_call(body, grid=(r // tr,), in_specs=[pl.BlockSpec((k, tr, n), lambda i: (0, i, 0))],
                          out_specs=pl.BlockSpec((tr, n), lambda i: (i, 0)),
                          out_shape=jax.ShapeDtypeStruct((r, n), x.dtype), name=name, compiler_params=_cparams(1))(x)


def _adamw(place, w, parts, m, v, name):
    shape = w.shape
    cols = shape[-1]
    rows = math.prod(shape[:-1])
    to2 = lambda t: t.reshape(rows, cols)
    tr = _row_tile(rows, cols) if rows * cols * 4 > (1 << 20) else rows
    npart = len(parts)
    spec = pl.BlockSpec((tr, cols), lambda i, pr: (i, 0))
    native = len(shape) == 3 and shape[0] == 1
    own = pl.BlockSpec((None, tr, cols), lambda i, pr: (0, i, 0)) if native else spec
    as_own = (lambda t: t) if native else to2
    part_specs, part_args = [], []
    for piece in parts:
        if isinstance(piece, tuple):
            stack, k, row0 = piece
            part_args.append(stack.reshape(stack.shape[0], -1, cols))
            assert row0 % tr == 0
            if k == "chip":
                part_specs.append(pl.BlockSpec((None, tr, cols), functools.partial(lambda i, pr, b0: (pr[1], i + b0, 0),
                                                                                   b0=row0 // tr)))
            else:
                part_specs.append(pl.BlockSpec((None, tr, cols), functools.partial(
                    lambda i, pr, kk, b0: (kk, i + b0, 0), kk=k, b0=row0 // tr)))
        else:
            part_args.append(to2(piece))
            part_specs.append(spec)

    def body(place_ref, *refs):
        w_ref, m_ref, v_ref = refs[0], refs[1 + npart], refs[2 + npart]
        g_ref, d_ref, nm_ref, nv_ref = refs[3 + npart:]
        g = refs[1][...].astype(F32)
        for q in range(1, npart):
            g = g + refs[1 + q][...].astype(F32)
        mm = ADAM_B1 * m_ref[...] + (1.0 - ADAM_B1) * g
        vv = ADAM_B2 * v_ref[...] + (1.0 - ADAM_B2) * jnp.square(g)
        m_hat = mm / (1.0 - ADAM_B1 ** ADAM_STEP)
        v_hat = vv / (1.0 - ADAM_B2 ** ADAM_STEP)
        g_ref[...] = g
        d_ref[...] = -ADAM_LR * (m_hat / (jnp.sqrt(v_hat) + ADAM_EPS) + ADAM_WD * w_ref[...])
        nm_ref[...] = mm
        nv_ref[...] = vv

    sh = jax.ShapeDtypeStruct(shape if native else (rows, cols), F32)
    grid_spec = pltpu.PrefetchScalarGridSpec(num_scalar_prefetch=1, grid=(rows // tr,),
                                             in_specs=[own] + part_specs + [own, own], out_specs=[own] * 4)
    outs = pl.pallas_call(body, grid_spec=grid_spec, out_shape=[sh] * 4, name=name, compiler_params=_cparams(1),
                          )(place, as_own(w), *part_args, as_own(m), as_own(v))
    return tuple(o.reshape(shape) for o in outs)


def _packed_rows(n, width):
    return -(-n // (8 * width)) * 8


def _pack_rows(items, width):
    rows = []
    for t in items:
        flat = t.reshape(-1)
        n = flat.shape[0]
        k = _packed_rows(n, width)
        if k * width > n:
            flat = jnp.concatenate([flat, jnp.zeros((k * width - n,), t.dtype)])
        rows.append(flat.reshape(k, width))
    return jnp.concatenate(rows, axis=0)


def _unpack_rows(packed, shapes, lead=()):
    width = packed.shape[-1]
    out, r = [], 0
    for sh in shapes:
        n = math.prod(sh)
        k = _packed_rows(n, width)
        piece = packed[..., r:r + k, :].reshape(tuple(lead) + (k * width,))[..., :n]
        out.append(piece.reshape(tuple(lead) + tuple(sh)))
        r += k
    return out


def _cols_full(t):
    return jnp.transpose(t, (1, 0, 2)).reshape(t.shape[1], -1)


def _cols_shards(t):
    d = t.shape[0]
    return jnp.transpose(t.reshape(d, N_DEV, -1), (1, 0, 2))


BIG = ("ffn1_gate", "ffn1_up", "ffn1_down", "w_in", "w_out", "ffn2_gate", "ffn2_up", "ffn2_down")


def kernel(x, c, ctx, c_ctx, w_mod, b_mod, norm_ffn1, ffn1_gate, ffn1_up, ffn1_down, norm_mix, w_in, ssm_conv_w, ssm_conv_b, dt_bias_fwd, dt_bias_bwd, a_log_fwd, a_log_bwd, ssm_d, ssm_norm_w, cconv_w, cconv_b, cconv_ln_w, cconv_ln_b, w_out, norm_ffn2, ffn2_gate, ffn2_up, ffn2_down, final_norm, loss_target, m_c_ctx, m_w_mod, m_b_mod, m_norm_ffn1, m_ffn1_gate, m_ffn1_up, m_ffn1_down, m_norm_mix, m_w_in, m_ssm_conv_w, m_ssm_conv_b, m_dt_bias_fwd, m_dt_bias_bwd, m_a_log_fwd, m_a_log_bwd, m_ssm_d, m_ssm_norm_w, m_cconv_w, m_cconv_b, m_cconv_ln_w, m_cconv_ln_b, m_w_out, m_norm_ffn2, m_ffn2_gate, m_ffn2_up, m_ffn2_down, m_final_norm, v_c_ctx, v_w_mod, v_b_mod, v_norm_ffn1, v_ffn1_gate, v_ffn1_up, v_ffn1_down, v_norm_mix, v_w_in, v_ssm_conv_w, v_ssm_conv_b, v_dt_bias_fwd, v_dt_bias_bwd, v_a_log_fwd, v_a_log_bwd, v_ssm_d, v_ssm_norm_w, v_cconv_w, v_cconv_b, v_cconv_ln_w, v_cconv_ln_b, v_w_out, v_norm_ffn2, v_ffn2_gate, v_ffn2_up, v_ffn2_down, v_final_norm):
    args = dict(locals())
    names = ("c_ctx", "w_mod", "b_mod", "norm_ffn1", "ffn1_gate", "ffn1_up", "ffn1_down", "norm_mix", "w_in",
             "ssm_conv_w", "ssm_conv_b", "dt_bias_fwd", "dt_bias_bwd", "a_log_fwd", "a_log_bwd", "ssm_d", "ssm_norm_w",
             "cconv_w", "cconv_b", "cconv_ln_w", "cconv_ln_b", "w_out", "norm_ffn2", "ffn2_gate", "ffn2_up",
             "ffn2_down", "final_norm")
    wts = {n: args[n] for n in names}
    bl, seq, d = x.shape
    clen = ctx.shape[1]
    heads = dt_bias_fwd.shape[1]
    ft = _gate_tile(ffn1_gate.shape[2] * N_DEV)
    lay = _Lay(bl, seq, clen, d)
    mx = _Mix(d, heads)
    nb = bl * N_DEV
    me = 4 * lax.axis_index("x") + 2 * lax.axis_index("y") + lax.axis_index("c")
    mcols = w_mod.shape[2]
    n_ctx_mod = 5 * d

    place = jnp.stack([lax.axis_index("c"), 2 * lax.axis_index("x") + lax.axis_index("y")]).astype(jnp.int32)

    small_shapes = [(bl, d), ssm_conv_w.shape[1:], cconv_w.shape[1:]]
    (g1,) = _all_gather([_pack_rows([c, ssm_conv_w, cconv_w], d)], "gather_small", False)
    c_g, conv_g, cconv_g = _unpack_rows(g1, small_shapes, (N_DEV,))
    c_all = c_g.reshape(nb, d)
    conv_w_full = jnp.transpose(conv_g, (1, 0, 2)).reshape(conv_g.shape[1], -1)
    cconv_w_full = jnp.transpose(cconv_g, (1, 0, 2)).reshape(cconv_g.shape[1], -1)

    s_all = jnp.concatenate([_silu(c_all), _silu(c_ctx)[None, :], jnp.zeros((7, d), F32)], axis=0)
    mod_cols = _mm(s_all, w_mod[0], "nn", F32, "mod_cols")
    (g2,) = _all_gather([mod_cols], "gather_mod", False)
    mod_all = _cols_full(g2) + b_mod
    mod_mine = jnp.concatenate([lax.dynamic_slice_in_dim(mod_all, me * bl, bl, axis=0), mod_all[nb:nb + 1]], axis=0)
    modv = mod_mine.reshape(bl + 1, N_MOD, d)

    hh = 2 * heads
    shard16 = lambda n: wts[n][0].astype(BF16)

    nl = ffn1_gate.shape[2]
    spt = ft // nl
    assert ft % nl == 0 and N_DEV % spt == 0

    def ffn_weights(gate, up, down):
        both = jnp.stack([gate, up], axis=1).reshape(N_DEV // spt, spt, 2, d, nl)
        return jnp.transpose(both, (3, 0, 2, 1, 4)).reshape(d, -1), None if down is None else down.reshape(-1, d), ft

    def grads_by_dest(name, grad):
        if name == "w_in":
            grad = _cols_shards(jnp.concatenate([grad[:, mx.off_z:], grad[:, :mx.off_dt + hh],
                                                 grad[:, mx.off_glu:mx.off_z]], axis=1))
        elif name.endswith("_gu"):
            grad = jnp.transpose(grad.reshape(d, N_DEV // spt, 2, spt, nl), (1, 3, 2, 0, 4)).reshape(N_DEV, 2 * d, nl)
        else:
            grad = grad.reshape((N_DEV,) + tuple(wts[name].shape[1:]))
        return grad.reshape((N_CHIPS, 2) + tuple(grad.shape[1:]))

    def ffn_grads(tag, pair):
        return {tag + "_gu": pair[0], tag + "_down": pair[1]}

    def to_chip_sums(named):
        names_ = list(named)
        by_dest = [grads_by_dest(n, named[n]) for n in names_]
        got = _swap_sibling(by_dest, "rs_sibling_" + names_[0])
        return {n: _pair_add(place, t, s, "rs_pair_add_" + n) for n, t, s in zip(names_, by_dest, got)}

    class _Overlap:
        names_a = ("ffn1_down", "w_in", "ffn2_gate")
        names_b = ("w_out", "ffn2_up", "ffn2_down")
        late_a = _GatherRide([shard16(n) for n in names_a])
        late_b = _GatherRide([shard16(n) for n in names_b])
        sums, arrived = {}, {}

        def unpack_late(self, ffn1, outs_a, outs_b):
            full = {**dict(zip(self.names_a, outs_a)), **dict(zip(self.names_b, outs_b))}
            w_in_f = _cols_full(full["w_in"])
            w_in_p = jnp.concatenate([w_in_f[:, mx.ref_x:mx.ref_glu], jnp.zeros((d, DT_PAD - hh), BF16),
                                      w_in_f[:, mx.ref_glu:], w_in_f[:, :d]], axis=1)
            return {"w_in": w_in_p, "w_out": full["w_out"].reshape(-1, d),
                    "ffn1": (ffn1[0], full["ffn1_down"].reshape(-1, d), ffn1[2]),
                    "ffn2": ffn_weights(full["ffn2_gate"], full["ffn2_up"], full["ffn2_down"])}

        def early_grads(self, g):
            self.sums = to_chip_sums({**ffn_grads("ffn2", g["ffn2"]), "w_in": g["w_in"], "w_out": g["w_out"]})
            self.groups = (("ffn2_gu", "ffn2_down"), ("w_in", "w_out"))
            return tuple(_ChipSwapRide([self.sums[n][1] for n in grp]) for grp in self.groups)

        def take_early(self, rode):
            for grp, outs in zip(self.groups, rode):
                self.arrived.update(zip(grp, outs))

    exch = _Overlap()
    gate1, up1 = _all_gather([shard16("ffn1_gate"), shard16("ffn1_up")], "gather_weights", True)
    wgu1 = ffn_weights(gate1, up1, None)[0]
    lanes_pad = lambda a, b: jnp.concatenate([a, b, jnp.zeros((1, DT_LANES - hh), F32)], axis=1)
    a_vals = lanes_pad(-jnp.exp(a_log_fwd), -jnp.exp(a_log_bwd))
    w = {
        "norm_ffn1": norm_ffn1, "norm_mix": norm_mix, "norm_ffn2": norm_ffn2, "final_norm": final_norm[None, :],
        "ffn1": (wgu1, lambda outs_a: outs_a[0].reshape(-1, d), ft),
        "conv_w": jnp.concatenate([conv_w_full, jnp.zeros((3, mx.xw), F32)], axis=0), "conv_b": ssm_conv_b,
        "dt_bias": lanes_pad(dt_bias_fwd, dt_bias_bwd), "a_row": a_vals, "a_col": a_vals.T,
        "d_row": jnp.repeat(ssm_d, HEAD_DIM, axis=1), "ssm_norm_w": ssm_norm_w,
        "cconv_w": cconv_w_full, "cconv_b": cconv_b, "ln_w": cconv_ln_w, "ln_b": cconv_ln_b,
    }

    xa = jnp.concatenate([x.reshape(bl * seq, d), ctx.reshape(bl * clen, d)], axis=0)
    loss, grad_x, g, dmodv = _local_step(lay, mx, xa, loss_target.reshape(bl * seq, d), modv, w, exch)
    loss = lax.psum(loss[0, 0], ("x", "y", "c"))

    sums = {**exch.sums, **to_chip_sums(ffn_grads("ffn1", g["ffn1"]))}
    arrived = dict(exch.arrived)
    last = ("ffn1_gu", "ffn1_down")
    arrived.update(zip(last, _exchange(_ChipSwapRide([sums[n][1] for n in last]), "rs_chips")))

    def big_parts(n):
        key, row0 = (n[:4] + "_gu", d if n.endswith("_up") else 0) if n.endswith(("_gate", "_up")) else (n, 0)
        return [(sums[key][0], "chip", row0)] + [(arrived[key], k, row0) for k in range(N_CHIP_PEERS)]

    n9 = N_MOD * d
    dmod_rows = dmodv.reshape(bl + 1, n9)
    ctx_row = jnp.concatenate([dmod_rows[bl, :n_ctx_mod], jnp.zeros((n9 - n_ctx_mod,), F32)])
    summed = [ctx_row, g["norm_ffn1"], g["norm_mix"], g["norm_ffn2"], g["final_norm"], g["conv_b"], g["dt_bias"],
              g["a_row"], g["d_row"], g["ssm_norm_w"], g["cconv_b"], g["ln_w"], g["ln_b"], g["conv_w"][:5], g["cconv_w"]]
    sum_shapes = [t.shape for t in summed]
    (g4,) = _all_gather([_pack_rows([dmod_rows[:bl]] + summed, d)], "gather_small_grads", False)
    dmod_batch = g4[:, :bl * N_MOD].reshape(nb, n9)
    tot = _sum_lead(g4[:, _packed_rows(bl * n9, d):], "sum_small_grads")
    (dctx, g_n1, g_nm, g_n2, g_fn, g_cb, g_dtb, g_a, g_drow, g_snw, g_ccb, g_lnw, g_lnb, g_cw, g_ccw) = _unpack_rows(tot, sum_shapes)
    dmod_all = jnp.concatenate([dmod_batch, dctx[None, :], jnp.zeros((7, n9), F32)], axis=0)

    dmod_my_cols = lax.dynamic_slice_in_dim(dmod_all, me * mcols, mcols, axis=1)
    g_w_mod = _mm(s_all, dmod_my_cols, "tn", F32, "dw_mod")[None]
    g_b_mod = _sum_lead(dmod_all.reshape(nb + 8, N_MOD, d), "db_mod").reshape(1, n9)
    ds_part = _mm(dmod_my_cols[nb:nb + 8], w_mod[0], "nt", F32, "ds_ctx")
    (g5,) = _all_gather([jnp.concatenate([ds_part[0:1], jnp.zeros((7, d), F32)], axis=0)], "gather_ds_ctx", False)
    ds_ctx = _sum_lead(g5, "sum_ds_ctx")[0]
    sg = jax.nn.sigmoid(c_ctx)
    g_c_ctx = ds_ctx * (sg * (1.0 + c_ctx * (1.0 - sg)))

    a_f, a_b = a_vals[:, :heads], a_vals[:, heads:hh]
    grads = {
        "c_ctx": [g_c_ctx], "w_mod": [g_w_mod], "b_mod": [g_b_mod],
        "norm_ffn1": [g_n1], "norm_mix": [g_nm], "norm_ffn2": [g_n2], "final_norm": [g_fn.reshape(-1)],
        "ssm_conv_w": [lax.dynamic_slice_in_dim(g_cw, me * ssm_conv_w.shape[2], ssm_conv_w.shape[2], axis=1)[None]],
        "ssm_conv_b": [g_cb],
        "dt_bias_fwd": [g_dtb[:, :heads]], "dt_bias_bwd": [g_dtb[:, heads:hh]],
        "a_log_fwd": [g_a[:, :heads] * a_f], "a_log_bwd": [g_a[:, heads:hh] * a_b],
        "ssm_d": [jnp.sum(g_drow.reshape(1, heads, HEAD_DIM), axis=2)], "ssm_norm_w": [g_snw],
        "cconv_w": [lax.dynamic_slice_in_dim(g_ccw, me * cconv_w.shape[2], cconv_w.shape[2], axis=1)[None]],
        "cconv_b": [g_ccb], "cconv_ln_w": [g_lnw], "cconv_ln_b": [g_lnb],
    }
    for n in BIG:
        grads[n] = big_parts(n)

    out_g, out_d, out_m, out_v = [], [], [], []
    for n in names:
        gr, de, nm, nv = _adamw(place, wts[n], grads[n], args["m_" + n], args["v_" + n], "adamw_" + n)
        out_g.append(gr)
        out_d.append(de)
        out_m.append(nm)
        out_v.append(nv)
    return (loss, grad_x.reshape(bl, seq, d), *out_g, *out_d, *out_m, *out_v)
```

```python
import functools
import math

import jax
import jax.numpy as jnp
from jax import lax
from jax.experimental import pallas as pl
from jax.experimental.pallas import tpu as pltpu

F32 = jnp.float32
BF16 = jnp.bfloat16
MESH = pl.DeviceIdType.MESH

N_DEV = 8
N_CHIP_PEERS = 3
HEAD_DIM = 64
N_STATE = 128
SSD_GROUPS = 2
CHUNK = 128
GRID_W = 64
N_MOD = 9
EPS = 1e-6
DT_PAD = 512
DT_LANES = 128
HALO = 8
ROW_TILE = 512
FINE_ROW_TILE = 256
VMEM_LIMIT = 48 * 1024 * 1024
NEG_BIG = -1e30

ADAM_LR = 0.001
ADAM_B1 = 0.9
ADAM_B2 = 0.999
ADAM_EPS = 1e-08
ADAM_WD = 0.01
ADAM_STEP = 10


def _pick(n, prefs):
    for p in prefs:
        if n % p == 0:
            return p
    return n


MM_TILE_CAP = 2816
MM_TILE_ELEMS = 3 << 20
MM_OUT_TILE_ELEMS = 3 << 19
MM_FULL_ROWS = 1024


def _big_tile(n, cap):
    if n <= cap:
        return n
    best = 0
    for t in range(128, cap + 1, 128):
        if n % t == 0:
            best = t
    return best or n


def _cparams(ndim):
    return pltpu.CompilerParams(dimension_semantics=("arbitrary",) * ndim, vmem_limit_bytes=VMEM_LIMIT)


def _silu(v):
    return v * jax.nn.sigmoid(v)


def _mm(a, b, mode, out_dtype, name, tn=None, tm=None, extras=(), epilogue=None, outs=None, ride=None):
    if mode == "tn":
        (K, M), (K2, N) = a.shape, b.shape
    elif mode == "nt":
        (M, K), (N, K2) = a.shape, b.shape
    else:
        (M, K), (K2, N) = a.shape, b.shape
    assert K == K2, (name, a.shape, b.shape)
    tm = tm or (M if M <= MM_FULL_ROWS else None)
    if tn is None:
        tn = _big_tile(N, min(MM_TILE_CAP, max(128, MM_OUT_TILE_ELEMS // (tm or 512))))
    if tm is None:
        tm = _big_tile(M, max(128, MM_OUT_TILE_ELEMS // tn))
    tk = _big_tile(K, min(MM_TILE_CAP, MM_TILE_ELEMS // max(tn, tm)))
    nk = K // tk
    ni, nj = M // tm, N // tn
    swap = nk == 1 and (K * N + M * K * nj) < (M * K + K * N * ni)
    ij = (lambda g0, g1: (g1, g0)) if swap else (lambda g0, g1: (g0, g1))
    if mode == "tn":
        a_spec = pl.BlockSpec((tk, tm), lambda g0, g1, k: (k, ij(g0, g1)[0]))
        dn = (((0,), (0,)), ((), ()))
    else:
        a_spec = pl.BlockSpec((tm, tk), lambda g0, g1, k: (ij(g0, g1)[0], k))
        dn = (((1,), (1,)), ((), ())) if mode == "nt" else (((1,), (0,)), ((), ()))
    if mode == "nt":
        b_spec = pl.BlockSpec((tn, tk), lambda g0, g1, k: (ij(g0, g1)[1], k))
    else:
        b_spec = pl.BlockSpec((tk, tn), lambda g0, g1, k: (k, ij(g0, g1)[1]))
    if outs is None:
        outs = [(tn, out_dtype)]
    nx = len(extras)

    def tile(w):
        return pl.BlockSpec((tm, w), lambda g0, g1, k: ij(g0, g1))

    def extra_spec(item):
        if len(item) == 3:
            return pl.BlockSpec((None,) + tuple(item[0].shape[1:]), lambda g0, g1, k: (item[1](ij(g0, g1)[0]), 0, 0))
        return tile(item[1])

    def finish(acc, refs):
        vals = (acc,) if epilogue is None else epilogue(acc, *[r[...] for r in refs[:nx]])
        for o_ref, v in zip(refs[nx:], vals):
            o_ref[...] = v.astype(o_ref.dtype)

    grid = (nj, ni, nk) if swap else (ni, nj, nk)
    nout = len(outs)
    r_in = len(ride.arrays) if ride else 0
    r_out = len(ride.out_shapes) if ride else 0

    def compute(a_ref, b_ref, refs):
        part = lax.dot_general(a_ref[...].astype(BF16), b_ref[...].astype(BF16), dn, preferred_element_type=F32)
        if nk == 1:
            finish(part, refs)
            return
        acc_ref, k = refs[-1], pl.program_id(2)
        _acc(acc_ref, k == 0, part)

        @pl.when(k == nk - 1)
        def _():
            finish(acc_ref[...], refs[:-1])

    def body(a_ref, b_ref, *refs):
        if ride is None:
            compute(a_ref, b_ref, refs)
            return
        x_refs, rin = refs[:nx], refs[nx:nx + r_in]
        o_refs, rout = refs[nx + r_in:nx + r_in + nout], refs[nx + r_in + nout:nx + r_in + nout + r_out]
        tail = refs[nx + r_in + nout + r_out:]
        nacc = 1 if nk > 1 else 0
        sems = tail[nacc:]
        ids = [pl.program_id(q) for q in range(3)]
        first = functools.reduce(jnp.logical_and, [i == 0 for i in ids])
        last = functools.reduce(jnp.logical_and, [i == n - 1 for i, n in zip(ids, grid)])
        pl.when(first)(lambda: ride.start(rin, rout, *sems))
        compute(a_ref, b_ref, tuple(x_refs) + tuple(o_refs) + tuple(tail[:nacc]))
        pl.when(last)(lambda: ride.finish(rin, rout, *sems))

    hbm = pl.BlockSpec(memory_space=pl.ANY)
    res = pl.pallas_call(
        body, grid=grid, in_specs=[a_spec, b_spec] + [extra_spec(x) for x in extras] + [hbm] * r_in,
        out_specs=[tile(w) for w, _ in outs] + [hbm] * r_out,
        out_shape=[jax.ShapeDtypeStruct((M, nj * w), dt) for w, dt in outs] + (list(ride.out_shapes) if ride else []),
        scratch_shapes=([pltpu.VMEM((tm, tn), F32)] if nk > 1 else []) + (list(ride.sems) if ride else []),
        name=name, compiler_params=_cparams(3),
    )(a, b, *[x[0] for x in extras], *(ride.arrays if ride else []))
    main = res[0] if epilogue is None else res[:nout]
    return (main, res[nout:]) if ride else main


class _Lay:
    def __init__(self, bl, seq, clen, d, tt=None):
        self.bl, self.seq, self.clen, self.d = bl, seq, clen, d
        self.tt = min(ROW_TILE, math.gcd(seq, bl * clen)) if tt is None else tt
        assert seq % self.tt == 0 and (bl * clen) % self.tt == 0 and self.tt % 8 == 0
        self.spb = seq // self.tt
        self.spc = clen // self.tt
        self.nsx = bl * self.spb
        self.nsc = bl * clen // self.tt
        self.ns = self.nsx + self.nsc
        self.tx = bl * seq
        self.ta = self.tx + bl * clen

    def fine(self):
        return _Lay(self.bl, self.seq, self.clen, self.d, min(FINE_ROW_TILE, self.clen))

    def mrow(self, s):
        return jnp.where(s < self.nsx, s // self.spb, self.bl)

    def first_of_row(self, s):
        return jnp.logical_or(jnp.logical_and(s < self.nsx, s % self.spb == 0), s == self.nsx)

    def seq_first(self, s):
        return jnp.where(s < self.nsx, s % self.spb == 0, (s - self.nsx) % self.spc == 0)

    def seq_last(self, s):
        return jnp.where(s < self.nsx, s % self.spb == self.spb - 1, (s - self.nsx) % self.spc == self.spc - 1)


def _tok(lay, c, cb=0, clamp=None):
    if clamp is None:
        return pl.BlockSpec((lay.tt, c), lambda j, s: (s, cb + j))
    return pl.BlockSpec((lay.tt, c), lambda j, s: (jnp.minimum(s, clamp), cb + j))


def _halo_prev(lay, c, cb=0):
    u = lay.tt // HALO
    return pl.BlockSpec((HALO, c), lambda j, s: (jnp.maximum(s * u - 1, 0), cb + j))


def _halo_next(lay, c, cb=0):
    u = lay.tt // HALO
    last = lay.ta // HALO - 1
    return pl.BlockSpec((HALO, c), lambda j, s: (jnp.minimum((s + 1) * u, last), cb + j))


def _row(lay, k, c):
    return pl.BlockSpec((None, k, c), lambda j, s: (lay.mrow(s), 0, 0))


def _glob(k, c, cb=None):
    if cb is None:
        return pl.BlockSpec((k, c), lambda j, s: (0, 0))
    return pl.BlockSpec((k, c), lambda j, s: (0, cb + j))


def _tok_call(name, body, ncb, nseg, in_specs, out_specs, out_shape, inputs, scratch=(), aliases=None):
    return pl.pallas_call(body, grid=(ncb, nseg), in_specs=in_specs, out_specs=out_specs, out_shape=out_shape,
                          scratch_shapes=list(scratch), name=name, compiler_params=_cparams(2),
                          input_output_aliases=aliases or {})(*inputs)


def _acc(ref, first, val):
    @pl.when(first)
    def _():
        ref[...] = val

    @pl.when(jnp.logical_not(first))
    def _():
        ref[...] += val


def _norm_mod_f(x, w, sh, sc):
    y = x * lax.rsqrt(jnp.mean(x * x, axis=-1, keepdims=True) + EPS) * w
    return y * (1.0 + sc) + sh


def _norm_mod_fwd(lay, nseg, x, w, modv, ksh, name):
    d = lay.d

    def body(x_ref, w_ref, m_ref, h_ref):
        h = _norm_mod_f(x_ref[...], w_ref[...], m_ref[ksh:ksh + 1, :], m_ref[ksh + 1:ksh + 2, :])
        h_ref[...] = h.astype(h_ref.dtype)

    return _tok_call(name, body, 1, nseg, [_tok(lay, d), _glob(1, d), _row(lay, N_MOD, d)], _tok(lay, d),
                     jax.ShapeDtypeStruct((nseg * lay.tt, d), BF16), (x, w, modv))


def _norm_mod_bwd(lay, nseg, nres, x, w, modv, ksh, dh, dres, name, nout=None):
    d = lay.d
    nrow = lay.bl + (1 if nseg > lay.nsx else 0)
    nout = nseg if nout is None else nout

    def body(x_ref, w_ref, m_ref, dh_ref, dres_ref, dx_ref, dw_ref, dm_ref):
        s = pl.program_id(1)
        _, vjp = jax.vjp(_norm_mod_f, x_ref[...], w_ref[...], m_ref[ksh:ksh + 1, :], m_ref[ksh + 1:ksh + 2, :])
        dx, dw, dsh, dsc = vjp(dh_ref[...])

        @pl.when(s < nout)
        def _():
            dx_ref[...] = dx + jnp.where(s < nres, dres_ref[...], 0.0)

        _acc(dw_ref, s == 0, dw)
        _acc(dm_ref, lay.first_of_row(s), jnp.concatenate([dsh, dsc], axis=0))

    return _tok_call(
        name, body, 1, nseg,
        [_tok(lay, d), _glob(1, d), _row(lay, N_MOD, d), _tok(lay, d), _tok(lay, d, clamp=nres - 1)],
        [_tok(lay, d, clamp=nout - 1), _glob(1, d), _row(lay, 2, d)],
        [jax.ShapeDtypeStruct((nout * lay.tt, d), F32), jax.ShapeDtypeStruct((1, d), F32),
         jax.ShapeDtypeStruct((nrow, 2, d), F32)],
        (x, w, modv, dh, dres))


def _mm_resid(lay, a, b, x, modv, kg, coef, name, then_norm=None, ride=None):
    d = lay.d
    tm = min(ROW_TILE, math.gcd(lay.seq, lay.bl * lay.clen))
    assert a.shape[0] % tm == 0 and b.shape[1] == d

    def row_of(i):
        return jnp.where(i * tm < lay.tx, (i * tm) // lay.seq, lay.bl)

    def add(acc, x_tile, m_blk, *nw):
        y = x_tile + (coef * m_blk[kg:kg + 1, :]) * acc
        if then_norm is None:
            return acc, y
        k = then_norm[1]
        return acc, y, _norm_mod_f(y, nw[0], m_blk[k:k + 1, :], m_blk[k + 1:k + 2, :])

    extras = [(x, d), (modv, row_of, "rows")]
    outs = [(d, F32), (d, F32)]
    if then_norm is not None:
        extras.append((then_norm[0].reshape(1, 1, d), lambda i: 0, "rows"))
        outs.append((d, BF16))
    return _mm(a, b, "nn", None, name, tm=tm, tn=d, extras=extras, epilogue=add, outs=outs, ride=ride)


def _resid_bwd(lay, nseg, dy, o, modv, kg, coef, name):
    d = lay.d
    nrow = lay.bl + (1 if nseg > lay.nsx else 0)

    def body(dy_ref, o_ref, m_ref, do_ref, dg_ref):
        s = pl.program_id(1)
        dy = dy_ref[...]
        do_ref[...] = (dy * (coef * m_ref[kg:kg + 1, :])).astype(do_ref.dtype)
        _acc(dg_ref, lay.first_of_row(s), jnp.sum(dy * o_ref[...], axis=0, keepdims=True) * coef)

    return _tok_call(name, body, 1, nseg, [_tok(lay, d), _tok(lay, d), _row(lay, N_MOD, d)],
                     [_tok(lay, d), _row(lay, 1, d)],
                     [jax.ShapeDtypeStruct((nseg * lay.tt, d), BF16), jax.ShapeDtypeStruct((nrow, 1, d), F32)],
                     (dy, o, modv))


def _final_loss(lay, x, wf, target, name):
    d = lay.d

    def body(x_ref, w_ref, t_ref, loss_ref, dx_ref, dw_ref):
        s = pl.program_id(1)

        def f(xv, wv):
            return xv * lax.rsqrt(jnp.mean(xv * xv, axis=-1, keepdims=True) + EPS) * wv

        y, vjp = jax.vjp(f, x_ref[...], w_ref[...])
        err = y - t_ref[...]
        part = 0.5 * jnp.sum(jnp.sum(err * err, axis=-1, keepdims=True), axis=0, keepdims=True) / d
        dx, dw = vjp(err / d)
        dx_ref[...] = dx
        _acc(loss_ref, s == 0, part)
        _acc(dw_ref, s == 0, dw)

    return _tok_call(name, body, 1, lay.nsx, [_tok(lay, d), _glob(1, d), _tok(lay, d)],
                     [_glob(1, 1), _tok(lay, d), _glob(1, d)],
                     [jax.ShapeDtypeStruct((1, 1), F32), jax.ShapeDtypeStruct((lay.tx, d), F32),
                      jax.ShapeDtypeStruct((1, d), F32)], (x, wf, target))


class _Mix:
    def __init__(self, d, heads):
        self.d_ssm = d
        self.d_conv = d
        self.heads = heads
        assert heads * HEAD_DIM == d and heads % (2 * SSD_GROUPS) == 0 and 2 * heads <= DT_LANES
        self.gn = SSD_GROUPS * N_STATE
        self.xw = d + 2 * self.gn
        self.off_x = 0
        self.off_dt = self.xw
        self.off_glu = self.xw + DT_PAD
        self.off_z = self.off_glu + 2 * d
        self.pw = self.off_z + d
        assert self.off_z % d == 0
        self.ref_x = d
        self.ref_dt = d + self.xw
        self.ref_glu = self.ref_dt + 2 * heads
        self.cc = self.xw if self.off_x % self.xw == 0 else _pick(self.xw, (512, 256, 128))


def _conv5_fwd(lay, mx, proj, cw, cb, name):
    c, tt = mx.cc, lay.tt
    cb0 = mx.off_x // c
    assert mx.off_x % c == 0

    def body(prev_ref, cur_ref, next_ref, w_ref, b_ref, pre_ref, act_ref, ext_ref):
        s = pl.program_id(1)
        ext_ref[0:HALO, :] = jnp.where(lay.seq_first(s), 0.0, prev_ref[...])
        ext_ref[HALO:HALO + tt, :] = cur_ref[...]
        ext_ref[HALO + tt:, :] = jnp.where(lay.seq_last(s), 0.0, next_ref[...])
        acc = jnp.zeros((tt, c), F32) + b_ref[...]
        for k in range(5):
            acc = acc + w_ref[k:k + 1, :] * ext_ref[pl.ds(HALO + k - 2, tt), :]
        pre_ref[...] = acc
        act_ref[...] = _silu(acc)

    sh = jax.ShapeDtypeStruct((lay.ta, mx.xw), F32)
    return _tok_call(name, body, mx.xw // c, lay.ns,
                     [_halo_prev(lay, c, cb0), _tok(lay, c, cb0), _halo_next(lay, c, cb0), _glob(8, c, 0), _glob(1, c, 0)],
                     [_tok(lay, c), _tok(lay, c)], [sh, sh], (proj, proj, proj, cw, cb),
                     scratch=[pltpu.VMEM((tt + 2 * HALO, c), F32)])


def _conv5_bwd(lay, mx, proj, pre, dact_f, dact_b, cw, dproj, name):
    c, tt = mx.cc, lay.tt
    cb0 = mx.off_x // c

    def dsilu(p):
        sg = jax.nn.sigmoid(p)
        return sg * (1.0 + p * (1.0 - sg))

    def body(xp_ref, xc_ref, xn_ref, pp_ref, pc_ref, pn_ref, fp_ref, fc_ref, fn_ref, bp_ref, bc_ref, bn_ref, w_ref,
             buf_ref, dx_ref, dw_ref, db_ref, extx_ref, extd_ref):
        s = pl.program_id(1)
        first, last = lay.seq_first(s), lay.seq_last(s)
        dcur = (fc_ref[...] + bc_ref[...]) * dsilu(pc_ref[...])
        extd_ref[0:HALO, :] = jnp.where(first, 0.0, (fp_ref[...] + bp_ref[...]) * dsilu(pp_ref[...]))
        extd_ref[HALO:HALO + tt, :] = dcur
        extd_ref[HALO + tt:, :] = jnp.where(last, 0.0, (fn_ref[...] + bn_ref[...]) * dsilu(pn_ref[...]))
        extx_ref[0:HALO, :] = jnp.where(first, 0.0, xp_ref[...])
        extx_ref[HALO:HALO + tt, :] = xc_ref[...]
        extx_ref[HALO + tt:, :] = jnp.where(last, 0.0, xn_ref[...])
        dx = jnp.zeros((tt, c), F32)
        rows = []
        for k in range(5):
            dx = dx + w_ref[k:k + 1, :] * extd_ref[pl.ds(HALO - (k - 2), tt), :]
            rows.append(jnp.sum(dcur * extx_ref[pl.ds(HALO + k - 2, tt), :], axis=0, keepdims=True))
        dx_ref[...] = dx.astype(dx_ref.dtype)
        rows.append(jnp.zeros((3, c), F32))
        _acc(dw_ref, s == 0, jnp.concatenate(rows, axis=0))
        _acc(db_ref, s == 0, jnp.sum(dcur, axis=0, keepdims=True))

    three = lambda cbx: [_halo_prev(lay, c, cbx), _tok(lay, c, cbx), _halo_next(lay, c, cbx)]
    ext = pltpu.VMEM((tt + 2 * HALO, c), F32)
    return _tok_call(name, body, mx.xw // c, lay.ns,
                     three(cb0) + three(0) + three(0) + three(0) + [_glob(8, c, 0), pl.BlockSpec(memory_space=pl.ANY)],
                     [_tok(lay, c, cb0), _glob(8, c, 0), _glob(1, c, 0)],
                     [jax.ShapeDtypeStruct(dproj.shape, dproj.dtype), jax.ShapeDtypeStruct((8, mx.xw), F32),
                      jax.ShapeDtypeStruct((1, mx.xw), F32)],
                     (proj, proj, proj, pre, pre, pre, dact_f, dact_f, dact_f, dact_b, dact_b, dact_b, cw, dproj),
                     scratch=[ext, ext], aliases={13: 0})


def _softplus(v):
    return jnp.maximum(v, 0.0) + jnp.log1p(jnp.exp(-jnp.abs(v)))


def _dt_fwd(lay, mx, proj, bias, name):
    cb = mx.off_dt // DT_LANES

    def body(p_ref, b_ref, dt_ref):
        dt_ref[...] = _softplus(p_ref[...] + b_ref[...])

    return _tok_call(name, body, 1, lay.ns, [_tok(lay, DT_LANES, cb), _glob(1, DT_LANES)], _tok(lay, DT_LANES),
                     jax.ShapeDtypeStruct((lay.ta, DT_LANES), F32), (proj, bias))


def _dt_bwd(lay, mx, proj, bias, parts, dproj, name):
    cb = mx.off_dt // DT_LANES
    wb = DT_PAD if mx.off_dt % DT_PAD == 0 else DT_LANES
    ncb = DT_PAD // wb

    def body(p_ref, b_ref, a_ref, b2_ref, c_ref, d_ref, buf_ref, dp_ref, db_ref):
        j, s = pl.program_id(0), pl.program_id(1)

        @pl.when(j == 0)
        def _():
            ddt = (a_ref[...] + b2_ref[...]) + (c_ref[...] + d_ref[...])
            draw = ddt * jax.nn.sigmoid(p_ref[...] + b_ref[...])
            dp_ref[:, 0:DT_LANES] = draw.astype(dp_ref.dtype)
            if wb > DT_LANES:
                dp_ref[:, DT_LANES:] = jnp.zeros((lay.tt, wb - DT_LANES), dp_ref.dtype)
            _acc(db_ref, s == 0, jnp.sum(draw, axis=0, keepdims=True))

        @pl.when(j > 0)
        def _():
            dp_ref[...] = jnp.zeros_like(dp_ref)

    t = pl.BlockSpec((lay.tt, DT_LANES), lambda j, s: (s, 0))
    return _tok_call(name, body, ncb, lay.ns,
                     [pl.BlockSpec((lay.tt, DT_LANES), lambda j, s: (s, cb)), _glob(1, DT_LANES), t, t, t, t,
                      pl.BlockSpec(memory_space=pl.ANY)],
                     [_tok(lay, wb, mx.off_dt // wb), _glob(1, DT_LANES)],
                     [jax.ShapeDtypeStruct(dproj.shape, dproj.dtype), jax.ShapeDtypeStruct((1, DT_LANES), F32)],
                     (proj, bias) + tuple(parts) + (dproj,), aliases={6: 0})


def _scan_mask(rev):
    r = lax.broadcasted_iota(jnp.int32, (CHUNK, CHUNK), 0)
    c = lax.broadcasted_iota(jnp.int32, (CHUNK, CHUNK), 1)
    return (r <= c) if rev else (r >= c)


def _split_bf16(x):
    hi = x.astype(BF16)
    return hi, (x - hi.astype(F32)).astype(BF16)


@functools.partial(jax.custom_vjp, nondiff_argnums=(0,))
def _cum_cols(rev, x):
    m = _scan_mask(rev).astype(BF16)
    hi, lo = _split_bf16(x)
    return jnp.dot(m, hi, preferred_element_type=F32) + jnp.dot(m, lo, preferred_element_type=F32)


_cum_cols.defvjp(lambda rev, x: (_cum_cols(rev, x), None), lambda rev, _, g: (_cum_cols(not rev, g),))


@functools.partial(jax.custom_vjp, nondiff_argnums=(0,))
def _cum_rows(rev, x):
    m = _scan_mask(not rev).astype(BF16)
    hi, lo = _split_bf16(x)
    return jnp.dot(hi, m, preferred_element_type=F32) + jnp.dot(lo, m, preferred_element_type=F32)


_cum_rows.defvjp(lambda rev, x: (_cum_rows(rev, x), None), lambda rev, _, g: (_cum_rows(not rev, g),))


def _ssd_chunk(xh_pairs, bcs, ccs, dtc, dtr, a_row, a_col, st_pairs, *, rev, heads, col0):
    cs_c, cs_r, tot, scores = _ssd_shared(bcs, ccs, dtc, dtr, a_row, a_col, rev=rev)
    ppg = heads // (2 * SSD_GROUPS)
    ys, sts = [], []
    for g in range(SSD_GROUPS):
        y, st = _ssd_group(xh_pairs[g * ppg:(g + 1) * ppg], bcs[g], ccs[g], st_pairs[g], cs_c, cs_r, tot, dtc,
                           scores[g], rev=rev, col=col0 + 2 * ppg * g)
        ys.append(y)
        sts.append(st)
    return ys, sts


_NT = (((1,), (1,)), ((), ()))
_TN = (((0,), (0,)), ((), ()))


def _ssd_shared(bcs, ccs, dtc, dtr, a_row, a_col, *, rev):
    da_c = dtc * a_row
    cs_c = _cum_cols(rev, da_c)
    cs_r = _cum_rows(rev, dtr * a_col)
    tot = jnp.sum(da_c, axis=0, keepdims=True)
    scores = [lax.dot_general(ccs[g].astype(BF16), bcs[g].astype(BF16), _NT, preferred_element_type=F32)
              for g in range(SSD_GROUPS)]
    return cs_c, cs_r, tot, scores


def _ssd_group(xh_pairs, bc, cc, st, cs_c, cs_r, tot, dtc, score, *, rev, col):
    n = CHUNK
    mask = _scan_mask(rev)
    lane = lax.broadcasted_iota(jnp.int32, (n, DT_LANES), 1)
    sub = lax.broadcasted_iota(jnp.int32, (DT_LANES, n), 0)
    lane1 = lax.broadcasted_iota(jnp.int32, (1, DT_LANES), 1)
    left = lax.broadcasted_iota(jnp.int32, (n, 2 * HEAD_DIM), 1) < HEAD_DIM
    top = lax.broadcasted_iota(jnp.int32, (2 * HEAD_DIM, 1), 0) < HEAD_DIM
    xs_all, wst_all, ecs_all, edec_all, y_diag = [], [], [], [], []
    for p, xh in enumerate(xh_pairs):
        per = []
        for c in (col + 2 * p, col + 2 * p + 1):
            csv = jnp.sum(jnp.where(lane == c, cs_c, 0.0), axis=1, keepdims=True)
            csr = jnp.sum(jnp.where(sub == c, cs_r, 0.0), axis=0, keepdims=True)
            dtv = jnp.sum(jnp.where(lane == c, dtc, 0.0), axis=1, keepdims=True)
            tv = jnp.sum(jnp.where(lane1 == c, tot, 0.0), axis=1, keepdims=True)
            m = score * jnp.exp(jnp.where(mask, csv - csr, NEG_BIG))
            per.append((csv, dtv, tv, m))
        (cs1, dt1, t1, m1), (cs2, dt2, t2, m2) = per
        xs = xh * jnp.where(left, dt1, dt2)
        both = jnp.dot(jnp.concatenate([m1, m2], axis=0).astype(BF16), xs.astype(BF16), preferred_element_type=F32)
        y_diag.append(jnp.where(left, both[:n], both[n:]))
        xs_all.append(xs)
        ecs_all.append(jnp.where(left, jnp.exp(cs1), jnp.exp(cs2)))
        wst_all.append(jnp.where(left, jnp.exp(t1 - cs1), jnp.exp(t2 - cs2)))
        edec_all.append(jnp.where(top, jnp.exp(t1), jnp.exp(t2)))
    cat = lambda parts, axis: parts[0] if len(parts) == 1 else jnp.concatenate(parts, axis=axis)
    xs, wst, ecs = cat(xs_all, 1), cat(wst_all, 1), cat(ecs_all, 1)
    y_off = lax.dot_general(cc.astype(BF16), st.astype(BF16), _NT, preferred_element_type=F32) * ecs
    cst = lax.dot_general((xs * wst).astype(BF16), bc.astype(BF16), _TN, preferred_element_type=F32)
    return cat(y_diag, 1) + y_off, st * cat(edec_all, 0) + cst


class _Scan:
    def __init__(self, lay, rev):
        self.ncx, self.ncc, self.bl, self.rev = lay.seq // CHUNK, lay.clen // CHUNK, lay.bl, rev
        self.nct = self.ncx + self.ncc

    def chunk(self, b, pos):
        kc = (self.ncc - 1 - pos) if self.rev else pos
        kx = (self.ncx - 1 - (pos - self.ncc)) if self.rev else (pos - self.ncc)
        return jnp.where(pos < self.ncc, self.bl * self.ncx + b * self.ncc + kc, b * self.ncx + kx)


def _ssd_io(mx, x_ref, st_src):
    np_ = mx.heads // 2
    d = mx.d_ssm
    xh = [x_ref[:, 128 * p:128 * (p + 1)] for p in range(np_)]
    bcs = [x_ref[:, d + N_STATE * g:d + N_STATE * (g + 1)] for g in range(SSD_GROUPS)]
    ccs = [x_ref[:, d + mx.gn + N_STATE * g:d + mx.gn + N_STATE * (g + 1)] for g in range(SSD_GROUPS)]
    gw = d // SSD_GROUPS
    sts = [st_src[gw * g:gw * (g + 1), :] for g in range(SSD_GROUPS)]
    return xh, bcs, ccs, sts


def _ssd_fwd(lay, mx, xbc, dt, dtt, a_row, a_col, rev, name):
    sc = _Scan(lay, rev)
    col0 = mx.heads if rev else 0
    hp = mx.heads * HEAD_DIM

    def body(x_ref, dt_ref, dtt_ref, ar_ref, ac_ref, y_ref, hp_ref, st_ref):
        @pl.when(pl.program_id(1) == 0)
        def _():
            st_ref[...] = jnp.zeros_like(st_ref)

        hp_ref[...] = st_ref[...]
        xh, bcs, ccs, sts = _ssd_io(mx, x_ref, st_ref)
        ys, new = _ssd_chunk(xh, bcs, ccs, dt_ref[...], dtt_ref[...], ar_ref[...], ac_ref[...], sts,
                             rev=rev, heads=mx.heads, col0=col0)
        gw = mx.d_ssm // SSD_GROUPS
        for g in range(SSD_GROUPS):
            y_ref[:, gw * g:gw * (g + 1)] = ys[g]
            st_ref[gw * g:gw * (g + 1), :] = new[g]

    ch = sc.chunk
    return pl.pallas_call(
        body, grid=(lay.bl, sc.nct),
        in_specs=[pl.BlockSpec((CHUNK, mx.xw), lambda b, i: (ch(b, i), 0)),
                  pl.BlockSpec((CHUNK, DT_LANES), lambda b, i: (ch(b, i), 0)),
                  pl.BlockSpec((DT_LANES, CHUNK), lambda b, i: (0, ch(b, i))),
                  pl.BlockSpec((1, DT_LANES), lambda b, i: (0, 0)),
                  pl.BlockSpec((DT_LANES, 1), lambda b, i: (0, 0))],
        out_specs=[pl.BlockSpec((CHUNK, mx.d_ssm), lambda b, i: (ch(b, i), 0)),
                   pl.BlockSpec((hp, N_STATE), lambda b, i: (b * sc.nct + i, 0))],
        out_shape=[jax.ShapeDtypeStruct((lay.ta, mx.d_ssm), F32),
                   jax.ShapeDtypeStruct((lay.bl * sc.nct * hp, N_STATE), F32)],
        scratch_shapes=[pltpu.VMEM((hp, N_STATE), F32)], name=name, compiler_params=_cparams(2),
    )(xbc, dt, dtt, a_row, a_col)


def _ssd_bwd(lay, mx, xbc, dt, dtt, a_row, a_col, hprev, dy, dskip, rev, name):
    sc = _Scan(lay, rev)
    col0 = mx.heads if rev else 0
    hp = mx.heads * HEAD_DIM
    np_ = mx.heads // 2
    d = mx.d_ssm
    with_skip = dskip is not None

    def body(*refs):
        if with_skip:
            x_ref, dt_ref, dtt_ref, ar_ref, ac_ref, hp_ref, dy_ref, sk_ref = refs[:8]
            rest = refs[8:]
        else:
            x_ref, dt_ref, dtt_ref, ar_ref, ac_ref, hp_ref, dy_ref = refs[:7]
            rest = refs[7:]
        dx_ref, ddc_ref, ddr_ref, dar_ref, dac_ref, ds_ref = rest
        b, i = pl.program_id(0), pl.program_id(1)

        @pl.when(i == 0)
        def _():
            ds_ref[...] = jnp.zeros_like(ds_ref)

        xh, bcs, ccs, sts = _ssd_io(mx, x_ref, hp_ref)
        dtc = dt_ref[...]
        shared, vjp_shared = jax.vjp(functools.partial(_ssd_shared, rev=rev), bcs, ccs, dtc, dtt_ref[...],
                                     ar_ref[...], ac_ref[...])
        cs_c, cs_r, tot, scores = shared
        plus = lambda acc, v: v if acc is None else acc + v
        d_cs_c = d_cs_r = d_tot = ddc = None
        d_scores, dbc, dcc = [], [], []
        ppg = np_ // SSD_GROUPS
        gw = d // SSD_GROUPS
        for g in range(SSD_GROUPS):
            dyg = dy_ref[:, gw * g:gw * (g + 1)]
            fn = functools.partial(_ssd_group, rev=rev, col=col0 + 2 * ppg * g)
            _, vjp = jax.vjp(fn, xh[g * ppg:(g + 1) * ppg], bcs[g], ccs[g], sts[g], cs_c, cs_r, tot, dtc, scores[g])
            dxh, dbc_g, dcc_g, dst, dcs_c_g, dcs_r_g, dtot_g, ddc_g, dsc_g = vjp((dyg, ds_ref[gw * g:gw * (g + 1), :]))
            for q in range(ppg):
                p = g * ppg + q
                v = dxh[q]
                if with_skip:
                    v = v + dyg[:, 128 * q:128 * (q + 1)] * sk_ref[:, 128 * p:128 * (p + 1)]
                dx_ref[:, 128 * p:128 * (p + 1)] = v
            ds_ref[gw * g:gw * (g + 1), :] = dst
            d_cs_c, d_cs_r, d_tot, ddc = plus(d_cs_c, dcs_c_g), plus(d_cs_r, dcs_r_g), plus(d_tot, dtot_g), plus(ddc, ddc_g)
            d_scores.append(dsc_g)
            dbc.append(dbc_g)
            dcc.append(dcc_g)
        dbc_s, dcc_s, ddc_s, ddr, dar, dac = vjp_shared((d_cs_c, d_cs_r, d_tot, d_scores))
        ddc = ddc + ddc_s
        dbc = [dbc[g] + dbc_s[g] for g in range(SSD_GROUPS)]
        dcc = [dcc[g] + dcc_s[g] for g in range(SSD_GROUPS)]
        for g in range(SSD_GROUPS):
            dx_ref[:, d + N_STATE * g:d + N_STATE * (g + 1)] = dbc[g]
            dx_ref[:, d + mx.gn + N_STATE * g:d + mx.gn + N_STATE * (g + 1)] = dcc[g]
        ddc_ref[...] = ddc
        ddr_ref[...] = ddr
        first = jnp.logical_and(b == 0, i == 0)
        _acc(dar_ref, first, dar)
        _acc(dac_ref, first, dac)

    ch = lambda b, i: sc.chunk(b, sc.nct - 1 - i)
    in_specs = [pl.BlockSpec((CHUNK, mx.xw), lambda b, i: (ch(b, i), 0)),
                pl.BlockSpec((CHUNK, DT_LANES), lambda b, i: (ch(b, i), 0)),
                pl.BlockSpec((DT_LANES, CHUNK), lambda b, i: (0, ch(b, i))),
                pl.BlockSpec((1, DT_LANES), lambda b, i: (0, 0)),
                pl.BlockSpec((DT_LANES, 1), lambda b, i: (0, 0)),
                pl.BlockSpec((hp, N_STATE), lambda b, i: (b * sc.nct + sc.nct - 1 - i, 0)),
                pl.BlockSpec((CHUNK, d), lambda b, i: (ch(b, i), 0))]
    inputs = [xbc, dt, dtt, a_row, a_col, hprev, dy]
    if with_skip:
        in_specs.append(pl.BlockSpec((1, d), lambda b, i: (0, 0)))
        inputs.append(dskip)
    return pl.pallas_call(
        body, grid=(lay.bl, sc.nct), in_specs=in_specs,
        out_specs=[pl.BlockSpec((CHUNK, mx.xw), lambda b, i: (ch(b, i), 0)),
                   pl.BlockSpec((CHUNK, DT_LANES), lambda b, i: (ch(b, i), 0)),
                   pl.BlockSpec((DT_LANES, CHUNK), lambda b, i: (0, ch(b, i))),
                   pl.BlockSpec((1, DT_LANES), lambda b, i: (0, 0)),
                   pl.BlockSpec((DT_LANES, 1), lambda b, i: (0, 0))],
        out_shape=[jax.ShapeDtypeStruct((lay.ta, mx.xw), F32), jax.ShapeDtypeStruct((lay.ta, DT_LANES), F32),
                   jax.ShapeDtypeStruct((DT_LANES, lay.ta), F32), jax.ShapeDtypeStruct((1, DT_LANES), F32),
                   jax.ShapeDtypeStruct((DT_LANES, 1), F32)],
        scratch_shapes=[pltpu.VMEM((hp, N_STATE), F32)], name=name, compiler_params=_cparams(2),
    )(*inputs)


def _gate_f(yf, yb, xh, z, drow, nw):
    dd = yf.shape[-1]
    half = dd // SSD_GROUPS
    yz = (yf + yb + drow * xh) * _silu(z)
    lo = lax.broadcasted_iota(jnp.int32, yz.shape, 1) < half
    sq = yz * yz
    ms1 = jnp.sum(jnp.where(lo, sq, 0.0), axis=-1, keepdims=True) / half
    ms2 = jnp.sum(jnp.where(lo, 0.0, sq), axis=-1, keepdims=True) / half
    return yz * jnp.where(lo, lax.rsqrt(ms1 + EPS), lax.rsqrt(ms2 + EPS)) * nw


def _gate_fwd(lay, mx, yf, yb, xbc, proj, drow, nw, name):
    d = mx.d_ssm

    def body(yf_ref, yb_ref, xh_ref, z_ref, d_ref, w_ref, o_ref):
        o_ref[...] = _gate_f(yf_ref[...], yb_ref[...], xh_ref[...], z_ref[...], d_ref[...], w_ref[...]).astype(o_ref.dtype)

    t, z = _tok(lay, d), _tok(lay, d, mx.off_z // d)
    return _tok_call(name, body, 1, lay.nsx, [t, t, t, z, _glob(1, d), _glob(1, d)], t,
                     jax.ShapeDtypeStruct((lay.tx, d + mx.d_conv), BF16), (yf, yb, xbc, proj, drow, nw))


def _gate_bwd(lay, mx, yf, yb, xbc, proj, drow, nw, dcat, name):
    d = mx.d_ssm
    nsx = lay.nsx

    def body(yf_ref, yb_ref, xh_ref, z_ref, d_ref, w_ref, dc_ref, dy_ref, dz_ref, dd_ref, dw_ref):
        s = pl.program_id(1)

        @pl.when(s < nsx)
        def _():
            _, vjp = jax.vjp(_gate_f, yf_ref[...], yb_ref[...], xh_ref[...], z_ref[...], d_ref[...], w_ref[...])
            dyf, _, _, dz, dd, dw = vjp(dc_ref[...])
            dy_ref[...] = dyf
            dz_ref[...] = dz.astype(dz_ref.dtype)
            _acc(dd_ref, s == 0, dd)
            _acc(dw_ref, s == 0, dw)

        @pl.when(s >= nsx)
        def _():
            dy_ref[...] = jnp.zeros_like(dy_ref)
            dz_ref[...] = jnp.zeros_like(dz_ref)

    t, z = _tok(lay, d), _tok(lay, d, mx.off_z // d)
    return _tok_call(name, body, 1, lay.ns, [t, t, t, z, _glob(1, d), _glob(1, d), _tok(lay, d, clamp=nsx - 1)],
                     [t, z, _glob(1, d), _glob(1, d)],
                     [jax.ShapeDtypeStruct((lay.ta, d), F32), jax.ShapeDtypeStruct((lay.ta, mx.pw), BF16),
                      jax.ShapeDtypeStruct((1, d), F32), jax.ShapeDtypeStruct((1, d), F32)],
                     (yf, yb, xbc, proj, drow, nw, dcat))


def _glu_fwd(lay, mx, proj, name):
    d = mx.d_conv
    c = math.gcd(mx.off_glu, d)
    cb = mx.off_glu // c

    def body(a_ref, b_ref, o_ref):
        o_ref[...] = a_ref[...] * jax.nn.sigmoid(b_ref[...])

    return _tok_call(name, body, d // c, lay.nsx, [_tok(lay, c, cb), _tok(lay, c, cb + d // c)], _tok(lay, c),
                     jax.ShapeDtypeStruct((lay.tx, d), F32), (proj, proj))


def _glu_bwd(lay, mx, proj, du, dproj, name):
    d = mx.d_conv
    nsx = lay.nsx
    whole = mx.off_glu % (2 * d) == 0
    c = d if whole else math.gcd(mx.off_glu, d)
    cb = mx.off_glu // c
    nc = d // c

    def body(a_ref, b_ref, du_ref, buf_ref, o_ref):
        j, s = pl.program_id(0), pl.program_id(1)

        @pl.when(s < nsx)
        def _():
            sg = jax.nn.sigmoid(b_ref[...])
            da = du_ref[...] * sg
            db = da * a_ref[...] * (1.0 - sg)
            if whole:
                o_ref[:, 0:d] = da.astype(o_ref.dtype)
                o_ref[:, d:] = db.astype(o_ref.dtype)
            else:
                o_ref[...] = jnp.where(j < nc, da, db).astype(o_ref.dtype)

        @pl.when(s >= nsx)
        def _():
            o_ref[...] = jnp.zeros_like(o_ref)

    win = lambda half: pl.BlockSpec((lay.tt, c), lambda j, s: (s, cb + half * nc + j % nc))
    out = _tok(lay, 2 * d, mx.off_glu // (2 * d)) if whole else _tok(lay, c, cb)
    return _tok_call(name, body, 1 if whole else 2 * nc, lay.ns,
                     [win(0), win(1), pl.BlockSpec((lay.tt, c), lambda j, s: (jnp.minimum(s, nsx - 1), j % nc)),
                      pl.BlockSpec(memory_space=pl.ANY)],
                     out, jax.ShapeDtypeStruct(dproj.shape, dproj.dtype), (proj, proj, du, dproj), aliases={3: 0})


def _axial(lay, mx, u, dy, cw, cb, name):
    d, seq = mx.d_conv, lay.seq
    kw = cw.shape[0]
    pad = kw // 2
    c = _pick(d // 2, (256, 128))
    ncb = d // c
    zpad = GRID_W * pad
    zpad = -(-zpad // 8) * 8
    backward = dy is not None

    def shifted(ext_ref, off):
        return ext_ref[pl.ds(zpad + off, seq), :]

    def valid_row(off):
        col = lax.broadcasted_iota(jnp.int32, (seq, c), 0) % GRID_W
        return jnp.logical_and(col + off >= 0, col + off < GRID_W)

    def fill(ext_ref, v):
        ext_ref[0:zpad, :] = jnp.zeros((zpad, c), F32)
        ext_ref[zpad:zpad + seq, :] = v
        ext_ref[zpad + seq:, :] = jnp.zeros((zpad, c), F32)

    def conv(ext_ref, w_ref, is_row, sign):
        acc = jnp.zeros((seq, c), F32)
        for k in range(kw):
            off = sign * ((k - pad) if is_row else GRID_W * (k - pad))
            v = shifted(ext_ref, off)
            if is_row:
                v = jnp.where(valid_row(off), v, 0.0)
            acc = acc + w_ref[k:k + 1, :] * v
        return acc

    def fwd_body(u_ref, w_ref, b_ref, o_ref, ext_ref):
        j = pl.program_id(0)
        fill(ext_ref, u_ref[...])

        @pl.when(j < ncb // 2)
        def _():
            o_ref[...] = conv(ext_ref, w_ref, True, 1) + b_ref[...]

        @pl.when(j >= ncb // 2)
        def _():
            o_ref[...] = conv(ext_ref, w_ref, False, 1) + b_ref[...]

    def bwd_body(u_ref, dy_ref, w_ref, du_ref, dw_ref, db_ref, extu_ref, extd_ref):
        j, b = pl.program_id(0), pl.program_id(1)
        dyv = dy_ref[...]
        fill(extu_ref, u_ref[...])
        fill(extd_ref, dyv)

        def grads(is_row):
            du_ref[...] = conv(extd_ref, w_ref, is_row, -1)
            rows = []
            for k in range(kw):
                off = (k - pad) if is_row else GRID_W * (k - pad)
                v = shifted(extu_ref, off)
                if is_row:
                    v = jnp.where(valid_row(off), v, 0.0)
                rows.append(jnp.sum(dyv * v, axis=0, keepdims=True))
            _acc(dw_ref, b == 0, jnp.concatenate(rows, axis=0))

        @pl.when(j < ncb // 2)
        def _():
            grads(True)

        @pl.when(j >= ncb // 2)
        def _():
            grads(False)

        _acc(db_ref, b == 0, jnp.sum(dyv, axis=0, keepdims=True))

    seq_spec = pl.BlockSpec((seq, c), lambda j, b: (b, j))
    w_spec = pl.BlockSpec((kw, c), lambda j, b: (0, j))
    b_spec = pl.BlockSpec((1, c), lambda j, b: (0, j))
    ext = pltpu.VMEM((seq + 2 * zpad, c), F32)
    if not backward:
        return pl.pallas_call(fwd_body, grid=(ncb, lay.bl), in_specs=[seq_spec, w_spec, b_spec], out_specs=seq_spec,
                              out_shape=jax.ShapeDtypeStruct((lay.tx, d), F32), scratch_shapes=[ext], name=name,
                              compiler_params=_cparams(2))(u, cw, cb)
    return pl.pallas_call(bwd_body, grid=(ncb, lay.bl), in_specs=[seq_spec, seq_spec, w_spec],
                          out_specs=[seq_spec, w_spec, b_spec],
                          out_shape=[jax.ShapeDtypeStruct((lay.tx, d), F32), jax.ShapeDtypeStruct((kw, d), F32),
                                     jax.ShapeDtypeStruct((1, d), F32)],
                          scratch_shapes=[ext, ext], name=name, compiler_params=_cparams(2))(u, dy, cw)


def _ln_silu_f(u, w, b):
    mu = jnp.mean(u, axis=-1, keepdims=True)
    var = jnp.mean(jnp.square(u - mu), axis=-1, keepdims=True)
    return _silu((u - mu) * lax.rsqrt(var + EPS) * w + b)


def _ln_fwd(lay, mx, u, w, b, cat, name):
    d = mx.d_conv
    assert mx.d_ssm % d == 0

    def body(u_ref, w_ref, b_ref, cat_ref, o_ref):
        o_ref[...] = _ln_silu_f(u_ref[...], w_ref[...], b_ref[...]).astype(o_ref.dtype)

    return _tok_call(name, body, 1, lay.nsx,
                     [_tok(lay, d), _glob(1, d), _glob(1, d), pl.BlockSpec(memory_space=pl.ANY)],
                     _tok(lay, d, mx.d_ssm // d), jax.ShapeDtypeStruct(cat.shape, cat.dtype), (u, w, b, cat),
                     aliases={3: 0})


def _ln_bwd(lay, mx, u, w, b, dcat, name):
    d = mx.d_conv

    def body(u_ref, w_ref, b_ref, dc_ref, du_ref, dw_ref, db_ref):
        s = pl.program_id(1)
        _, vjp = jax.vjp(_ln_silu_f, u_ref[...], w_ref[...], b_ref[...])
        du, dw, db = vjp(dc_ref[...])
        du_ref[...] = du
        _acc(dw_ref, s == 0, dw)
        _acc(db_ref, s == 0, db)

    return _tok_call(name, body, 1, lay.nsx, [_tok(lay, d), _glob(1, d), _glob(1, d), _tok(lay, d, 1)],
                     [_tok(lay, d), _glob(1, d), _glob(1, d)],
                     [jax.ShapeDtypeStruct((lay.tx, d), F32), jax.ShapeDtypeStruct((1, d), F32),
                      jax.ShapeDtypeStruct((1, d), F32)], (u, w, b, dcat))


def _gate_tile(dff):
    return dff // 2 if (dff // 2) % 128 == 0 else dff


def _ffn_fwd(lay, nseg, x, nw, modv, k0, wts, tag, ride=None, ride_down=None, h=None, then_norm=None):
    wgu, wd, ft = wts
    if h is None:
        h = _norm_mod_fwd(lay, nseg, x, nw, modv, k0, tag + "_norm")
    t = h.shape[0]

    def act(acc):
        g, u = acc[:, :ft], acc[:, ft:]
        sg = jax.nn.sigmoid(g)
        sl = g * sg
        return jnp.concatenate([u * (sg * (1.0 + g * (1.0 - sg))), sl], axis=1), sl * u

    res = _mm(h, wgu, "nn", None, tag + "_gu", tn=2 * ft, tm=256 if t % 256 == 0 else None,
              epilogue=act, outs=[(2 * ft, BF16), (ft, BF16)], ride=ride)
    (s, a), rode = res if ride else (res, None)
    if callable(wd):
        wd = wd(rode)
    res = _mm_resid(lay, a, wd, x, modv, k0 + 2, 0.5, tag + "_down", then_norm=then_norm, ride=ride_down)
    (o, y, *h_next), rode_down = res if ride_down else (res, None)
    return y, (x, h, s, a, o), (rode, rode_down), (h_next[0] if h_next else None)


def _ffn_bwd(lay, nseg, dy, saved, nw, modv, k0, wts, tag, nout=None, rides=(None, None)):
    wgu, wd, ft = wts
    x, h, s, a, o = saved
    do, dgate = _resid_bwd(lay, nseg, dy, o, modv, k0 + 2, 0.5, tag + "_dres")

    def through_act(da, s_tile):
        return (jnp.concatenate([da, da], axis=1) * s_tile.astype(F32),)

    (dgu,) = _mm(do, wd.T, "nn", None, tag + "_da", tn=ft, tm=_big_tile(do.shape[0], 1024), extras=[(s, 2 * ft)],
                 epilogue=through_act, outs=[(2 * ft, BF16)])
    dwd = _mm(a, do, "tn", F32, tag + "_dwd")
    dh = _mm(dgu, wgu, "nt", F32, tag + "_dh", ride=rides[0])
    dwgu = _mm(h, dgu, "tn", F32, tag + "_dwgu", ride=rides[1])
    (dh, rode_a), (dwgu, rode_b) = (dh if rides[0] else (dh, None)), (dwgu if rides[1] else (dwgu, None))
    dx, dnw, dss = _norm_mod_bwd(lay, nseg, nseg, x, nw, modv, k0, dh, dy, tag + "_dnorm", nout=nout)
    return dx, (dwgu, dwd), dnw, jnp.concatenate([dss, dgate], axis=1), (rode_a, rode_b)


def _local_step(lay, mx, xa, target, modv, w, exch=None):
    d, bl = lay.d, lay.bl
    g = {}
    xa1, ffn1, late, ha = _ffn_fwd(lay, lay.ns, xa, w["norm_ffn1"], modv, 0, w["ffn1"], "ffn1",
                                   ride=exch.late_a if exch else None, ride_down=exch.late_b if exch else None,
                                   then_norm=(w["norm_mix"], 3))
    if exch:
        w = {**w, **exch.unpack_late(w["ffn1"], *late)}
    proj = _mm(ha, w["w_in"], "nn", F32, "mix_in")
    pre, xbc = _conv5_fwd(lay.fine(), mx, proj, w["conv_w"], w["conv_b"], "mix_conv")
    dt = _dt_fwd(lay, mx, proj, w["dt_bias"], "mix_dt")
    dtt = dt.T
    yf, hpf = _ssd_fwd(lay, mx, xbc, dt, dtt, w["a_row"], w["a_col"], False, "ssd_f")
    yb, hpb = _ssd_fwd(lay, mx, xbc, dt, dtt, w["a_row"], w["a_col"], True, "ssd_b")
    cat_y = _gate_fwd(lay, mx, yf, yb, xbc, proj, w["d_row"], w["ssm_norm_w"], "mix_gate")
    u0 = _glu_fwd(lay, mx, proj, "mix_glu")
    uc = _axial(lay, mx, u0, None, w["cconv_w"], w["cconv_b"], "mix_axial")
    cat = _ln_fwd(lay, mx, uc, w["ln_w"], w["ln_b"], cat_y, "mix_ln")
    mix, x2, h2 = _mm_resid(lay, cat, w["w_out"], xa1, modv, 5, 1.0, "mix_out", then_norm=(w["norm_ffn2"], 6))
    x3, ffn2, _, _ = _ffn_fwd(lay, lay.nsx, x2, w["norm_ffn2"], modv, 6, w["ffn2"], "ffn2", h=h2)
    loss, dx3, g["final_norm"] = _final_loss(lay, x3, w["final_norm"], target, "loss")
    dx2, g["ffn2"], g["norm_ffn2"], dmod2, _ = _ffn_bwd(lay, lay.nsx, dx3, ffn2, w["norm_ffn2"], modv, 6, w["ffn2"], "ffn2")
    dmix, dg2 = _resid_bwd(lay, lay.nsx, dx2, mix, modv, 5, 1.0, "mix_dres")
    dcat = _mm(dmix, w["w_out"], "nt", F32, "mix_dcat")
    g["w_out"] = _mm(cat, dmix, "tn", F32, "mix_dwout")
    duc, g["ln_w"], g["ln_b"] = _ln_bwd(lay, mx, uc, w["ln_w"], w["ln_b"], dcat, "mix_dln")
    du0, g["cconv_w"], g["cconv_b"] = _axial(lay, mx, u0, duc, w["cconv_w"], None, "mix_daxial")
    dyssd, dproj, g["d_row"], g["ssm_norm_w"] = _gate_bwd(lay, mx, yf, yb, xbc, proj, w["d_row"], w["ssm_norm_w"], dcat,
                                                          "mix_dgate")
    dproj = _glu_bwd(lay, mx, proj, du0, dproj, "mix_dglu")
    dxf, ddcf, ddrf, darf, dacf = _ssd_bwd(lay, mx, xbc, dt, dtt, w["a_row"], w["a_col"], hpf, dyssd, w["d_row"],
                                           False, "ssd_df")
    dxb, ddcb, ddrb, darb, dacb = _ssd_bwd(lay, mx, xbc, dt, dtt, w["a_row"], w["a_col"], hpb, dyssd, None,
                                           True, "ssd_db")
    g["a_row"] = (darf + darb) + (dacf + dacb).T
    dproj, g["conv_w"], g["conv_b"] = _conv5_bwd(lay.fine(), mx, proj, pre, dxf, dxb, w["conv_w"], dproj, "mix_dconv")
    dproj, g["dt_bias"] = _dt_bwd(lay, mx, proj, w["dt_bias"], (ddcf, ddcb, ddrf.T, ddrb.T), dproj, "mix_ddt")
    dha = _mm(dproj, w["w_in"], "nt", F32, "mix_dha")
    g["w_in"] = _mm(ha, dproj, "tn", F32, "mix_dwin")
    dxa1, g["norm_mix"], dss_mix = _norm_mod_bwd(lay, lay.ns, lay.nsx, xa1, w["norm_mix"], modv, 3, dha, dx2, "mix_dnorm")
    rides = exch.early_grads(g) if exch else (None, None)
    dx, g["ffn1"], g["norm_ffn1"], dmod1, rode = _ffn_bwd(lay, lay.ns, dxa1, ffn1, w["norm_ffn1"], modv, 0, w["ffn1"],
                                                         "ffn1", nout=lay.nsx, rides=rides)
    if exch:
        exch.take_early(rode)
    zrow = lambda t: jnp.concatenate([t, jnp.zeros((1,) + t.shape[1:], F32)], axis=0)
    dmodv = jnp.concatenate([dmod1, dss_mix, zrow(dg2), zrow(dmod2)], axis=1)
    return loss, dx, g, dmodv


class _GatherRide:
    def __init__(self, xs):
        self.arrays = list(xs)
        self.na = len(xs)
        self.out_shapes = [jax.ShapeDtypeStruct((N_DEV,) + tuple(x.shape), x.dtype) for x in xs]
        self.sems = [pltpu.SemaphoreType.DMA((7 * self.na,)), pltpu.SemaphoreType.DMA((7 * self.na,)),
                     pltpu.SemaphoreType.DMA((self.na,))]

    def _plan(self, x_refs, out_refs, send_sems, recv_sems, local_sems):
        mx_, my_, mc_ = lax.axis_index("x"), lax.axis_index("y"), lax.axis_index("c")
        me, sibling = (mx_, my_, mc_), (mx_, my_, 1 - mc_)
        chips = [(1 - mx_, my_), (mx_, 1 - my_), (1 - mx_, 1 - my_)]

        def slot(a, px, py, pc):
            return out_refs[a].at[4 * px + 2 * py + pc]

        def copy(a, k, block, to, own=False):
            return pltpu.make_async_remote_copy(
                src_ref=x_refs[a] if own else slot(a, *block), dst_ref=slot(a, *block),
                send_sem=send_sems.at[7 * a + k], recv_sem=recv_sems.at[7 * a + k], device_id=to, device_id_type=MESH)

        mine = [pltpu.make_async_copy(x_refs[a], slot(a, *me), local_sems.at[a]) for a in range(self.na)]
        first = []
        for a in range(self.na):
            first.append(copy(a, 0, me, sibling, own=True))
            first += [copy(a, 1 + j, me, (*chip, mc_), own=True) for j, chip in enumerate(chips)]
        return me, sibling, chips, mc_, copy, mine, first

    def start(self, x_refs, out_refs, send_sems, recv_sems, local_sems):
        *_, mine, first = self._plan(x_refs, out_refs, send_sems, recv_sems, local_sems)
        for cp in mine + first:
            cp.start()

    def finish(self, x_refs, out_refs, send_sems, recv_sems, local_sems):
        me, sibling, chips, mc_, copy, mine, first = self._plan(x_refs, out_refs, send_sems, recv_sems, local_sems)
        passed = []
        for j, chip in enumerate(chips):
            for a in range(self.na):
                copy(a, 1 + j, (*chip, mc_), me).wait_recv()
                fwd = copy(a, 4 + j, (*chip, mc_), sibling)
                fwd.start()
                passed.append(fwd)
        for a in range(self.na):
            copy(a, 0, sibling, me).wait_recv()
            for j, chip in enumerate(chips):
                copy(a, 4 + j, (*chip, 1 - mc_), me).wait_recv()
        for cp in first + passed:
            cp.wait_send()
        for cp in mine:
            cp.wait()


def _exchange(ride, name, in_hbm=True):
    n_in, n_out = len(ride.arrays), len(ride.out_shapes)

    def body(*refs):
        ins, outs, sems = refs[:n_in], refs[n_in:n_in + n_out], refs[n_in + n_out:]
        ride.start(ins, outs, *sems)
        ride.finish(ins, outs, *sems)

    space = pl.BlockSpec(memory_space=pl.ANY if in_hbm else pltpu.VMEM)
    return pl.pallas_call(body, out_shape=list(ride.out_shapes), in_specs=[space] * n_in, out_specs=[space] * n_out,
                          scratch_shapes=list(ride.sems), name=name)(*ride.arrays)


def _all_gather(xs, name, in_hbm):
    return _exchange(_GatherRide(xs), name, in_hbm)


N_CHIPS = 4


def _swap_sibling(gs, name):
    na = len(gs)

    def body(*refs):
        g_refs, out_refs, send_sems, recv_sems = refs[:na], refs[na:2 * na], refs[2 * na], refs[2 * na + 1]
        mx_, my_, mc_ = lax.axis_index("x"), lax.axis_index("y"), lax.axis_index("c")
        copies = [pltpu.make_async_remote_copy(
            src_ref=g_refs[a].at[k, 1 - mc_], dst_ref=out_refs[a].at[k], send_sem=send_sems.at[N_CHIPS * a + k],
            recv_sem=recv_sems.at[N_CHIPS * a + k], device_id=(mx_, my_, 1 - mc_), device_id_type=MESH)
            for a in range(na) for k in range(N_CHIPS)]
        for cp in copies:
            cp.start()
        for cp in copies:
            cp.wait_recv()
        for cp in copies:
            cp.wait_send()

    return pl.pallas_call(
        body, out_shape=[jax.ShapeDtypeStruct((N_CHIPS,) + tuple(g.shape[2:]), g.dtype) for g in gs],
        in_specs=[pl.BlockSpec(memory_space=pl.ANY)] * na, out_specs=[pl.BlockSpec(memory_space=pl.ANY)] * na,
        scratch_shapes=[pltpu.SemaphoreType.DMA((N_CHIPS * na,)), pltpu.SemaphoreType.DMA((N_CHIPS * na,))], name=name,
    )(*gs)


def _row_tile(r, n):
    if r * n * 4 <= (1 << 20):
        return r
    for t in (1024, 512, 256, 128, 64, 32, 16, 8):
        if r % t == 0 and t * n * 4 <= (1 << 20):
            return t
    return r


def _pair_add(place, g, got, name):
    _, _, r, n = g.shape
    tr = r if r * n * 4 <= (3 << 19) else _row_tile(r, n)

    def body(place_ref, g_ref, got_ref, o_ref, ob_ref):
        s = g_ref[...] + got_ref[...]
        o_ref[...] = s
        ob_ref[...] = s.astype(ob_ref.dtype)

    blk = pl.BlockSpec((None, tr, n), lambda k, i, pr: (k, i, 0))
    grid_spec = pltpu.PrefetchScalarGridSpec(
        num_scalar_prefetch=1, grid=(N_CHIPS, r // tr),
        in_specs=[pl.BlockSpec((None, None, tr, n), lambda k, i, pr: (k, pr[0], i, 0)), blk], out_specs=[blk, blk])
    return pl.pallas_call(body, grid_spec=grid_spec,
                          out_shape=[jax.ShapeDtypeStruct((N_CHIPS, r, n), F32), jax.ShapeDtypeStruct((N_CHIPS, r, n), BF16)],
                          name=name, compiler_params=_cparams(2))(place, g, got)


class _ChipSwapRide:
    def __init__(self, ps):
        self.arrays = list(ps)
        self.na = len(ps)
        self.out_shapes = [jax.ShapeDtypeStruct((N_CHIP_PEERS,) + tuple(p.shape[1:]), p.dtype) for p in ps]
        self.sems = [pltpu.SemaphoreType.DMA((N_CHIP_PEERS * self.na,)), pltpu.SemaphoreType.DMA((N_CHIP_PEERS * self.na,))]

    def _copies(self, p_refs, out_refs, send_sems, recv_sems):
        mx_, my_, mc_ = lax.axis_index("x"), lax.axis_index("y"), lax.axis_index("c")
        chips = [(1 - mx_, my_), (mx_, 1 - my_), (1 - mx_, 1 - my_)]
        return [pltpu.make_async_remote_copy(
            src_ref=p_refs[a].at[2 * cx + cy], dst_ref=out_refs[a].at[j], send_sem=send_sems.at[N_CHIP_PEERS * a + j],
            recv_sem=recv_sems.at[N_CHIP_PEERS * a + j], device_id=(cx, cy, mc_), device_id_type=MESH)
            for a in range(self.na) for j, (cx, cy) in enumerate(chips)]

    def start(self, p_refs, out_refs, send_sems, recv_sems):
        for cp in self._copies(p_refs, out_refs, send_sems, recv_sems):
            cp.start()

    def finish(self, p_refs, out_refs, send_sems, recv_sems):
        copies = self._copies(p_refs, out_refs, send_sems, recv_sems)
        for cp in copies:
            cp.wait_recv()
        for cp in copies:
            cp.wait_send()


def _sum_lead(x, name):
    k, r, n = x.shape
    tr = _row_tile(r, n * k)

    def body(x_ref, o_ref):
        acc = x_ref[0]
        for i in range(1, k):
            acc = acc + x_ref[i]
        o_ref[...] = acc

    return pl.pallas_call(body, grid=(r // tr,), in_specs=[pl.BlockSpec((k, tr, n), lambda i: (0, i, 0))],
                          out_specs=pl.BlockSpec((tr, n), lambda i: (i, 0)),
                          out_shape=jax.ShapeDtypeStruct((r, n), x.dtype), name=name, compiler_params=_cparams(1))(x)


def _adamw(place, w, parts, m, v, name):
    shape = w.shape
    cols = shape[-1]
    rows = math.prod(shape[:-1])
    to2 = lambda t: t.reshape(rows, cols)
    tr = _row_tile(rows, cols) if rows * cols * 4 > (1 << 20) else rows
    npart = len(parts)
    spec = pl.BlockSpec((tr, cols), lambda i, pr: (i, 0))
    native = len(shape) == 3 and shape[0] == 1
    own = pl.BlockSpec((None, tr, cols), lambda i, pr: (0, i, 0)) if native else spec
    as_own = (lambda t: t) if native else to2
    part_specs, part_args = [], []
    for piece in parts:
        if isinstance(piece, tuple):
            stack, k, row0 = piece
            part_args.append(stack.reshape(stack.shape[0], -1, cols))
            assert row0 % tr == 0
            if k == "chip":
                part_specs.append(pl.BlockSpec((None, tr, cols), functools.partial(lambda i, pr, b0: (pr[1], i + b0, 0),
                                                                                   b0=row0 // tr)))
            else:
                part_specs.append(pl.BlockSpec((None, tr, cols), functools.partial(
                    lambda i, pr, kk, b0: (kk, i + b0, 0), kk=k, b0=row0 // tr)))
        else:
            part_args.append(to2(piece))
            part_specs.append(spec)

    def body(place_ref, *refs):
        w_ref, m_ref, v_ref = refs[0], refs[1 + npart], refs[2 + npart]
        g_ref, d_ref, nm_ref, nv_ref = refs[3 + npart:]
        g = refs[1][...].astype(F32)
        for q in range(1, npart):
            g = g + refs[1 + q][...].astype(F32)
        mm = ADAM_B1 * m_ref[...] + (1.0 - ADAM_B1) * g
        vv = ADAM_B2 * v_ref[...] + (1.0 - ADAM_B2) * jnp.square(g)
        m_hat = mm / (1.0 - ADAM_B1 ** ADAM_STEP)
        v_hat = vv / (1.0 - ADAM_B2 ** ADAM_STEP)
        g_ref[...] = g
        d_ref[...] = -ADAM_LR * (m_hat / (jnp.sqrt(v_hat) + ADAM_EPS) + ADAM_WD * w_ref[...])
        nm_ref[...] = mm
        nv_ref[...] = vv

    sh = jax.ShapeDtypeStruct(shape if native else (rows, cols), F32)
    grid_spec = pltpu.PrefetchScalarGridSpec(num_scalar_prefetch=1, grid=(rows // tr,),
                                             in_specs=[own] + part_specs + [own, own], out_specs=[own] * 4)
    outs = pl.pallas_call(body, grid_spec=grid_spec, out_shape=[sh] * 4, name=name, compiler_params=_cparams(1),
                          )(place, as_own(w), *part_args, as_own(m), as_own(v))
    return tuple(o.reshape(shape) for o in outs)


def _packed_rows(n, width):
    return -(-n // (8 * width)) * 8


def _pack_rows(items, width):
    rows = []
    for t in items:
        flat = t.reshape(-1)
        n = flat.shape[0]
        k = _packed_rows(n, width)
        if k * width > n:
            flat = jnp.concatenate([flat, jnp.zeros((k * width - n,), t.dtype)])
        rows.append(flat.reshape(k, width))
    return jnp.concatenate(rows, axis=0)


def _unpack_rows(packed, shapes, lead=()):
    width = packed.shape[-1]
    out, r = [], 0
    for sh in shapes:
        n = math.prod(sh)
        k = _packed_rows(n, width)
        piece = packed[..., r:r + k, :].reshape(tuple(lead) + (k * width,))[..., :n]
        out.append(piece.reshape(tuple(lead) + tuple(sh)))
        r += k
    return out


def _cols_full(t):
    return jnp.transpose(t, (1, 0, 2)).reshape(t.shape[1], -1)


def _cols_shards(t):
    d = t.shape[0]
    return jnp.transpose(t.reshape(d, N_DEV, -1), (1, 0, 2))


BIG = ("ffn1_gate", "ffn1_up", "ffn1_down", "w_in", "w_out", "ffn2_gate", "ffn2_up", "ffn2_down")


def kernel(x, c, ctx, c_ctx, w_mod, b_mod, norm_ffn1, ffn1_gate, ffn1_up, ffn1_down, norm_mix, w_in, ssm_conv_w, ssm_conv_b, dt_bias_fwd, dt_bias_bwd, a_log_fwd, a_log_bwd, ssm_d, ssm_norm_w, cconv_w, cconv_b, cconv_ln_w, cconv_ln_b, w_out, norm_ffn2, ffn2_gate, ffn2_up, ffn2_down, final_norm, loss_target, m_c_ctx, m_w_mod, m_b_mod, m_norm_ffn1, m_ffn1_gate, m_ffn1_up, m_ffn1_down, m_norm_mix, m_w_in, m_ssm_conv_w, m_ssm_conv_b, m_dt_bias_fwd, m_dt_bias_bwd, m_a_log_fwd, m_a_log_bwd, m_ssm_d, m_ssm_norm_w, m_cconv_w, m_cconv_b, m_cconv_ln_w, m_cconv_ln_b, m_w_out, m_norm_ffn2, m_ffn2_gate, m_ffn2_up, m_ffn2_down, m_final_norm, v_c_ctx, v_w_mod, v_b_mod, v_norm_ffn1, v_ffn1_gate, v_ffn1_up, v_ffn1_down, v_norm_mix, v_w_in, v_ssm_conv_w, v_ssm_conv_b, v_dt_bias_fwd, v_dt_bias_bwd, v_a_log_fwd, v_a_log_bwd, v_ssm_d, v_ssm_norm_w, v_cconv_w, v_cconv_b, v_cconv_ln_w, v_cconv_ln_b, v_w_out, v_norm_ffn2, v_ffn2_gate, v_ffn2_up, v_ffn2_down, v_final_norm):
    args = dict(locals())
    names = ("c_ctx", "w_mod", "b_mod", "norm_ffn1", "ffn1_gate", "ffn1_up", "ffn1_down", "norm_mix", "w_in",
             "ssm_conv_w", "ssm_conv_b", "dt_bias_fwd", "dt_bias_bwd", "a_log_fwd", "a_log_bwd", "ssm_d", "ssm_norm_w",
             "cconv_w", "cconv_b", "cconv_ln_w", "cconv_ln_b", "w_out", "norm_ffn2", "ffn2_gate", "ffn2_up",
             "ffn2_down", "final_norm")
    wts = {n: args[n] for n in names}
    bl, seq, d = x.shape
    clen = ctx.shape[1]
    heads = dt_bias_fwd.shape[1]
    ft = _gate_tile(ffn1_gate.shape[2] * N_DEV)
    lay = _Lay(bl, seq, clen, d)
    mx = _Mix(d, heads)
    nb = bl * N_DEV
    me = 4 * lax.axis_index("x") + 2 * lax.axis_index("y") + lax.axis_index("c")
    mcols = w_mod.shape[2]
    n_ctx_mod = 5 * d

    place = jnp.stack([lax.axis_index("c"), 2 * lax.axis_index("x") + lax.axis_index("y")]).astype(jnp.int32)

    small_shapes = [(bl, d), ssm_conv_w.shape[1:], cconv_w.shape[1:]]
    (g1,) = _all_gather([_pack_rows([c, ssm_conv_w, cconv_w], d)], "gather_small", False)
    c_g, conv_g, cconv_g = _unpack_rows(g1, small_shapes, (N_DEV,))
    c_all = c_g.reshape(nb, d)
    conv_w_full = jnp.transpose(conv_g, (1, 0, 2)).reshape(conv_g.shape[1], -1)
    cconv_w_full = jnp.transpose(cconv_g, (1, 0, 2)).reshape(cconv_g.shape[1], -1)

    s_all = jnp.concatenate([_silu(c_all), _silu(c_ctx)[None, :], jnp.zeros((7, d), F32)], axis=0)
    mod_cols = _mm(s_all, w_mod[0], "nn", F32, "mod_cols")
    (g2,) = _all_gather([mod_cols], "gather_mod", False)
    mod_all = _cols_full(g2) + b_mod
    mod_mine = jnp.concatenate([lax.dynamic_slice_in_dim(mod_all, me * bl, bl, axis=0), mod_all[nb:nb + 1]], axis=0)
    modv = mod_mine.reshape(bl + 1, N_MOD, d)

    hh = 2 * heads
    shard16 = lambda n: wts[n][0].astype(BF16)

    nl = ffn1_gate.shape[2]
    spt = ft // nl
    assert ft % nl == 0 and N_DEV % spt == 0

    def ffn_weights(gate, up, down):
        both = jnp.stack([gate, up], axis=1).reshape(N_DEV // spt, spt, 2, d, nl)
        return jnp.transpose(both, (3, 0, 2, 1, 4)).reshape(d, -1), None if down is None else down.reshape(-1, d), ft

    def grads_by_dest(name, grad):
        if name == "w_in":
            grad = _cols_shards(jnp.concatenate([grad[:, mx.off_z:], grad[:, :mx.off_dt + hh],
                                                 grad[:, mx.off_glu:mx.off_z]], axis=1))
        elif name.endswith("_gu"):
            grad = jnp.transpose(grad.reshape(d, N_DEV // spt, 2, spt, nl), (1, 3, 2, 0, 4)).reshape(N_DEV, 2 * d, nl)
        else:
            grad = grad.reshape((N_DEV,) + tuple(wts[name].shape[1:]))
        return grad.reshape((N_CHIPS, 2) + tuple(grad.shape[1:]))

    def ffn_grads(tag, pair):
        return {tag + "_gu": pair[0], tag + "_down": pair[1]}

    def to_chip_sums(named):
        names_ = list(named)
        by_dest = [grads_by_dest(n, named[n]) for n in names_]
        got = _swap_sibling(by_dest, "rs_sibling_" + names_[0])
        return {n: _pair_add(place, t, s, "rs_pair_add_" + n) for n, t, s in zip(names_, by_dest, got)}

    class _Overlap:
        names_a = ("ffn1_down", "w_in", "ffn2_gate")
        names_b = ("w_out", "ffn2_up", "ffn2_down")
        late_a = _GatherRide([shard16(n) for n in names_a])
        late_b = _GatherRide([shard16(n) for n in names_b])
        sums, arrived = {}, {}

        def unpack_late(self, ffn1, outs_a, outs_b):
            full = {**dict(zip(self.names_a, outs_a)), **dict(zip(self.names_b, outs_b))}
            w_in_f = _cols_full(full["w_in"])
            w_in_p = jnp.concatenate([w_in_f[:, mx.ref_x:mx.ref_glu], jnp.zeros((d, DT_PAD - hh), BF16),
                                      w_in_f[:, mx.ref_glu:], w_in_f[:, :d]], axis=1)
            return {"w_in": w_in_p, "w_out": full["w_out"].reshape(-1, d),
                    "ffn1": (ffn1[0], full["ffn1_down"].reshape(-1, d), ffn1[2]),
                    "ffn2": ffn_weights(full["ffn2_gate"], full["ffn2_up"], full["ffn2_down"])}

        def early_grads(self, g):
            self.sums = to_chip_sums({**ffn_grads("ffn2", g["ffn2"]), "w_in": g["w_in"], "w_out": g["w_out"]})
            self.groups = (("ffn2_gu", "ffn2_down"), ("w_in", "w_out"))
            return tuple(_ChipSwapRide([self.sums[n][1] for n in grp]) for grp in self.groups)

        def take_early(self, rode):
            for grp, outs in zip(self.groups, rode):
                self.arrived.update(zip(grp, outs))

    exch = _Overlap()
    gate1, up1 = _all_gather([shard16("ffn1_gate"), shard16("ffn1_up")], "gather_weights", True)
    wgu1 = ffn_weights(gate1, up1, None)[0]
    lanes_pad = lambda a, b: jnp.concatenate([a, b, jnp.zeros((1, DT_LANES - hh), F32)], axis=1)
    a_vals = lanes_pad(-jnp.exp(a_log_fwd), -jnp.exp(a_log_bwd))
    w = {
        "norm_ffn1": norm_ffn1, "norm_mix": norm_mix, "norm_ffn2": norm_ffn2, "final_norm": final_norm[None, :],
        "ffn1": (wgu1, lambda outs_a: outs_a[0].reshape(-1, d), ft),
        "conv_w": jnp.concatenate([conv_w_full, jnp.zeros((3, mx.xw), F32)], axis=0), "conv_b": ssm_conv_b,
        "dt_bias": lanes_pad(dt_bias_fwd, dt_bias_bwd), "a_row": a_vals, "a_col": a_vals.T,
        "d_row": jnp.repeat(ssm_d, HEAD_DIM, axis=1), "ssm_norm_w": ssm_norm_w,
        "cconv_w": cconv_w_full, "cconv_b": cconv_b, "ln_w": cconv_ln_w, "ln_b": cconv_ln_b,
    }

    xa = jnp.concatenate([x.reshape(bl * seq, d), ctx.reshape(bl * clen, d)], axis=0)
    loss, grad_x, g, dmodv = _local_step(lay, mx, xa, loss_target.reshape(bl * seq, d), modv, w, exch)

    sums = {**exch.sums, **to_chip_sums(ffn_grads("ffn1", g["ffn1"]))}
    arrived = dict(exch.arrived)
    last = ("ffn1_gu", "ffn1_down")
    arrived.update(zip(last, _exchange(_ChipSwapRide([sums[n][1] for n in last]), "rs_chips")))

    def big_parts(n):
        key, row0 = (n[:4] + "_gu", d if n.endswith("_up") else 0) if n.endswith(("_gate", "_up")) else (n, 0)
        return [(sums[key][0], "chip", row0)] + [(arrived[key], k, row0) for k in range(N_CHIP_PEERS)]

    n9 = N_MOD * d
    dmod_rows = dmodv.reshape(bl + 1, n9)
    ctx_row = jnp.concatenate([dmod_rows[bl, :n_ctx_mod], jnp.zeros((n9 - n_ctx_mod,), F32), loss[0]])
    summed = [ctx_row, g["norm_ffn1"], g["norm_mix"], g["norm_ffn2"], g["final_norm"], g["conv_b"], g["dt_bias"],
              g["a_row"], g["d_row"], g["ssm_norm_w"], g["cconv_b"], g["ln_w"], g["ln_b"], g["conv_w"][:5], g["cconv_w"]]
    sum_shapes = [t.shape for t in summed]
    (g4,) = _all_gather([_pack_rows([dmod_rows[:bl]] + summed, d)], "gather_small_grads", False)
    dmod_batch = g4[:, :bl * N_MOD].reshape(nb, n9)
    tot = _sum_lead(g4[:, _packed_rows(bl * n9, d):], "sum_small_grads")
    (dctx, g_n1, g_nm, g_n2, g_fn, g_cb, g_dtb, g_a, g_drow, g_snw, g_ccb, g_lnw, g_lnb, g_cw, g_ccw) = _unpack_rows(tot, sum_shapes)
    dctx, loss = dctx[:n9], dctx[n9]
    dmod_all = jnp.concatenate([dmod_batch, dctx[None, :], jnp.zeros((7, n9), F32)], axis=0)

    dmod_my_cols = lax.dynamic_slice_in_dim(dmod_all, me * mcols, mcols, axis=1)
    g_w_mod = _mm(s_all, dmod_my_cols, "tn", F32, "dw_mod")[None]
    g_b_mod = _sum_lead(dmod_all.reshape(nb + 8, N_MOD, d), "db_mod").reshape(1, n9)
    ds_part = _mm(dmod_my_cols[nb:nb + 8], w_mod[0], "nt", F32, "ds_ctx")
    (g5,) = _all_gather([jnp.concatenate([ds_part[0:1], jnp.zeros((7, d), F32)], axis=0)], "gather_ds_ctx", False)
    ds_ctx = _sum_lead(g5, "sum_ds_ctx")[0]
    sg = jax.nn.sigmoid(c_ctx)
    g_c_ctx = ds_ctx * (sg * (1.0 + c_ctx * (1.0 - sg)))

    a_f, a_b = a_vals[:, :heads], a_vals[:, heads:hh]
    grads = {
        "c_ctx": [g_c_ctx], "w_mod": [g_w_mod], "b_mod": [g_b_mod],
        "norm_ffn1": [g_n1], "norm_mix": [g_nm], "norm_ffn2": [g_n2], "final_norm": [g_fn.reshape(-1)],
        "ssm_conv_w": [lax.dynamic_slice_in_dim(g_cw, me * ssm_conv_w.shape[2], ssm_conv_w.shape[2], axis=1)[None]],
        "ssm_conv_b": [g_cb],
        "dt_bias_fwd": [g_dtb[:, :heads]], "dt_bias_bwd": [g_dtb[:, heads:hh]],
        "a_log_fwd": [g_a[:, :heads] * a_f], "a_log_bwd": [g_a[:, heads:hh] * a_b],
        "ssm_d": [jnp.sum(g_drow.reshape(1, heads, HEAD_DIM), axis=2)], "ssm_norm_w": [g_snw],
        "cconv_w": [lax.dynamic_slice_in_dim(g_ccw, me * cconv_w.shape[2], cconv_w.shape[2], axis=1)[None]],
        "cconv_b": [g_ccb], "cconv_ln_w": [g_lnw], "cconv_ln_b": [g_lnb],
    }
    for n in BIG:
        grads[n] = big_parts(n)

    out_g, out_d, out_m, out_v = [], [], [], []
    for n in names:
        gr, de, nm, nv = _adamw(place, wts[n], grads[n], args["m_" + n], args["v_" + n], "adamw_" + n)
        out_g.append(gr)
        out_d.append(de)
        out_m.append(nm)
        out_v.append(nv)
    return (loss, grad_x.reshape(bl, seq, d), *out_g, *out_d, *out_m, *out_v)
```

```python
import functools
import math

import jax
import jax.numpy as jnp
from jax import lax
from jax.experimental import pallas as pl
from jax.experimental.pallas import tpu as pltpu

F32 = jnp.float32
BF16 = jnp.bfloat16
MESH = pl.DeviceIdType.MESH

N_DEV = 8
N_CHIP_PEERS = 3
HEAD_DIM = 64
N_STATE = 128
SSD_GROUPS = 2
CHUNK = 128
GRID_W = 64
N_MOD = 9
EPS = 1e-6
DT_PAD = 512
DT_LANES = 128
HALO = 8
ROW_TILE = 512
FINE_ROW_TILE = 256
VMEM_LIMIT = 48 * 1024 * 1024
NEG_BIG = -1e30

ADAM_LR = 0.001
ADAM_B1 = 0.9
ADAM_B2 = 0.999
ADAM_EPS = 1e-08
ADAM_WD = 0.01
ADAM_STEP = 10


def _pick(n, prefs):
    for p in prefs:
        if n % p == 0:
            return p
    return n


MM_TILE_CAP = 2816
MM_TILE_ELEMS = 3 << 20
MM_OUT_TILE_ELEMS = 3 << 19
MM_FULL_ROWS = 1024


def _big_tile(n, cap):
    if n <= cap:
        return n
    best = 0
    for t in range(128, cap + 1, 128):
        if n % t == 0:
            best = t
    return best or n


def _cparams(ndim):
    return pltpu.CompilerParams(dimension_semantics=("arbitrary",) * ndim, vmem_limit_bytes=VMEM_LIMIT)


def _silu(v):
    return v * jax.nn.sigmoid(v)


def _mm(a, b, mode, out_dtype, name, tn=None, tm=None, extras=(), epilogue=None, outs=None, ride=None):
    if mode == "tn":
        (K, M), (K2, N) = a.shape, b.shape
    elif mode == "nt":
        (M, K), (N, K2) = a.shape, b.shape
    else:
        (M, K), (K2, N) = a.shape, b.shape
    assert K == K2, (name, a.shape, b.shape)
    tm = tm or (M if M <= MM_FULL_ROWS else None)
    if tn is None:
        tn = _big_tile(N, min(MM_TILE_CAP, max(128, MM_OUT_TILE_ELEMS // (tm or 512))))
    if tm is None:
        tm = _big_tile(M, max(128, MM_OUT_TILE_ELEMS // tn))
    tk = _big_tile(K, min(MM_TILE_CAP, MM_TILE_ELEMS // max(tn, tm)))
    nk = K // tk
    ni, nj = M // tm, N // tn
    swap = nk == 1 and (K * N + M * K * nj) < (M * K + K * N * ni)
    ij = (lambda g0, g1: (g1, g0)) if swap else (lambda g0, g1: (g0, g1))
    if mode == "tn":
        a_spec = pl.BlockSpec((tk, tm), lambda g0, g1, k: (k, ij(g0, g1)[0]))
        dn = (((0,), (0,)), ((), ()))
    else:
        a_spec = pl.BlockSpec((tm, tk), lambda g0, g1, k: (ij(g0, g1)[0], k))
        dn = (((1,), (1,)), ((), ())) if mode == "nt" else (((1,), (0,)), ((), ()))
    if mode == "nt":
        b_spec = pl.BlockSpec((tn, tk), lambda g0, g1, k: (ij(g0, g1)[1], k))
    else:
        b_spec = pl.BlockSpec((tk, tn), lambda g0, g1, k: (k, ij(g0, g1)[1]))
    if outs is None:
        outs = [(tn, out_dtype)]
    nx = len(extras)

    def tile(w):
        return pl.BlockSpec((tm, w), lambda g0, g1, k: ij(g0, g1))

    def extra_spec(item):
        if len(item) == 3:
            return pl.BlockSpec((None,) + tuple(item[0].shape[1:]), lambda g0, g1, k: (item[1](ij(g0, g1)[0]), 0, 0))
        return tile(item[1])

    def finish(acc, refs):
        vals = (acc,) if epilogue is None else epilogue(acc, *[r[...] for r in refs[:nx]])
        for o_ref, v in zip(refs[nx:], vals):
            o_ref[...] = v.astype(o_ref.dtype)

    grid = (nj, ni, nk) if swap else (ni, nj, nk)
    nout = len(outs)
    r_in = len(ride.arrays) if ride else 0
    r_out = len(ride.out_shapes) if ride else 0

    def compute(a_ref, b_ref, refs):
        part = lax.dot_general(a_ref[...].astype(BF16), b_ref[...].astype(BF16), dn, preferred_element_type=F32)
        if nk == 1:
            finish(part, refs)
            return
        acc_ref, k = refs[-1], pl.program_id(2)
        _acc(acc_ref, k == 0, part)

        @pl.when(k == nk - 1)
        def _():
            finish(acc_ref[...], refs[:-1])

    def body(a_ref, b_ref, *refs):
        if ride is None:
            compute(a_ref, b_ref, refs)
            return
        x_refs, rin = refs[:nx], refs[nx:nx + r_in]
        o_refs, rout = refs[nx + r_in:nx + r_in + nout], refs[nx + r_in + nout:nx + r_in + nout + r_out]
        tail = refs[nx + r_in + nout + r_out:]
        nacc = 1 if nk > 1 else 0
        sems = tail[nacc:]
        ids = [pl.program_id(q) for q in range(3)]
        first = functools.reduce(jnp.logical_and, [i == 0 for i in ids])
        last = functools.reduce(jnp.logical_and, [i == n - 1 for i, n in zip(ids, grid)])
        pl.when(first)(lambda: ride.start(rin, rout, *sems))
        compute(a_ref, b_ref, tuple(x_refs) + tuple(o_refs) + tuple(tail[:nacc]))
        pl.when(last)(lambda: ride.finish(rin, rout, *sems))

    hbm = pl.BlockSpec(memory_space=pl.ANY)
    res = pl.pallas_call(
        body, grid=grid, in_specs=[a_spec, b_spec] + [extra_spec(x) for x in extras] + [hbm] * r_in,
        out_specs=[tile(w) for w, _ in outs] + [hbm] * r_out,
        out_shape=[jax.ShapeDtypeStruct((M, nj * w), dt) for w, dt in outs] + (list(ride.out_shapes) if ride else []),
        scratch_shapes=([pltpu.VMEM((tm, tn), F32)] if nk > 1 else []) + (list(ride.sems) if ride else []),
        name=name, compiler_params=_cparams(3),
    )(a, b, *[x[0] for x in extras], *(ride.arrays if ride else []))
    main = res[0] if epilogue is None else res[:nout]
    return (main, res[nout:]) if ride else main


class _Lay:
    def __init__(self, bl, seq, clen, d, tt=None):
        self.bl, self.seq, self.clen, self.d = bl, seq, clen, d
        self.tt = min(ROW_TILE, math.gcd(seq, bl * clen)) if tt is None else tt
        assert seq % self.tt == 0 and (bl * clen) % self.tt == 0 and self.tt % 8 == 0
        self.spb = seq // self.tt
        self.spc = clen // self.tt
        self.nsx = bl * self.spb
        self.nsc = bl * clen // self.tt
        self.ns = self.nsx + self.nsc
        self.tx = bl * seq
        self.ta = self.tx + bl * clen

    def fine(self):
        return _Lay(self.bl, self.seq, self.clen, self.d, min(FINE_ROW_TILE, self.clen))

    def mrow(self, s):
        return jnp.where(s < self.nsx, s // self.spb, self.bl)

    def first_of_row(self, s):
        return jnp.logical_or(jnp.logical_and(s < self.nsx, s % self.spb == 0), s == self.nsx)

    def seq_first(self, s):
        return jnp.where(s < self.nsx, s % self.spb == 0, (s - self.nsx) % self.spc == 0)

    def seq_last(self, s):
        return jnp.where(s < self.nsx, s % self.spb == self.spb - 1, (s - self.nsx) % self.spc == self.spc - 1)


def _tok(lay, c, cb=0, clamp=None):
    if clamp is None:
        return pl.BlockSpec((lay.tt, c), lambda j, s: (s, cb + j))
    return pl.BlockSpec((lay.tt, c), lambda j, s: (jnp.minimum(s, clamp), cb + j))


def _halo_prev(lay, c, cb=0):
    u = lay.tt // HALO
    return pl.BlockSpec((HALO, c), lambda j, s: (jnp.maximum(s * u - 1, 0), cb + j))


def _halo_next(lay, c, cb=0):
    u = lay.tt // HALO
    last = lay.ta // HALO - 1
    return pl.BlockSpec((HALO, c), lambda j, s: (jnp.minimum((s + 1) * u, last), cb + j))


def _row(lay, k, c):
    return pl.BlockSpec((None, k, c), lambda j, s: (lay.mrow(s), 0, 0))


def _glob(k, c, cb=None):
    if cb is None:
        return pl.BlockSpec((k, c), lambda j, s: (0, 0))
    return pl.BlockSpec((k, c), lambda j, s: (0, cb + j))


def _tok_call(name, body, ncb, nseg, in_specs, out_specs, out_shape, inputs, scratch=(), aliases=None):
    return pl.pallas_call(body, grid=(ncb, nseg), in_specs=in_specs, out_specs=out_specs, out_shape=out_shape,
                          scratch_shapes=list(scratch), name=name, compiler_params=_cparams(2),
                          input_output_aliases=aliases or {})(*inputs)


def _acc(ref, first, val):
    @pl.when(first)
    def _():
        ref[...] = val

    @pl.when(jnp.logical_not(first))
    def _():
        ref[...] += val


def _norm_mod_f(x, w, sh, sc):
    y = x * lax.rsqrt(jnp.mean(x * x, axis=-1, keepdims=True) + EPS) * w
    return y * (1.0 + sc) + sh


def _norm_mod_fwd(lay, nseg, x, w, modv, ksh, name):
    d = lay.d

    def body(x_ref, w_ref, m_ref, h_ref):
        h = _norm_mod_f(x_ref[...], w_ref[...], m_ref[ksh:ksh + 1, :], m_ref[ksh + 1:ksh + 2, :])
        h_ref[...] = h.astype(h_ref.dtype)

    return _tok_call(name, body, 1, nseg, [_tok(lay, d), _glob(1, d), _row(lay, N_MOD, d)], _tok(lay, d),
                     jax.ShapeDtypeStruct((nseg * lay.tt, d), BF16), (x, w, modv))


def _norm_mod_bwd(lay, nseg, nres, x, w, modv, ksh, dh, dres, name, nout=None):
    d = lay.d
    nrow = lay.bl + (1 if nseg > lay.nsx else 0)
    nout = nseg if nout is None else nout

    def body(x_ref, w_ref, m_ref, dh_ref, dres_ref, dx_ref, dw_ref, dm_ref):
        s = pl.program_id(1)
        _, vjp = jax.vjp(_norm_mod_f, x_ref[...], w_ref[...], m_ref[ksh:ksh + 1, :], m_ref[ksh + 1:ksh + 2, :])
        dx, dw, dsh, dsc = vjp(dh_ref[...])

        @pl.when(s < nout)
        def _():
            dx_ref[...] = dx + jnp.where(s < nres, dres_ref[...], 0.0)

        _acc(dw_ref, s == 0, dw)
        _acc(dm_ref, lay.first_of_row(s), jnp.concatenate([dsh, dsc], axis=0))

    return _tok_call(
        name, body, 1, nseg,
        [_tok(lay, d), _glob(1, d), _row(lay, N_MOD, d), _tok(lay, d), _tok(lay, d, clamp=nres - 1)],
        [_tok(lay, d, clamp=nout - 1), _glob(1, d), _row(lay, 2, d)],
        [jax.ShapeDtypeStruct((nout * lay.tt, d), F32), jax.ShapeDtypeStruct((1, d), F32),
         jax.ShapeDtypeStruct((nrow, 2, d), F32)],
        (x, w, modv, dh, dres))


def _mm_resid(lay, a, b, x, modv, kg, coef, name, then_norm=None, ride=None):
    d = lay.d
    tm = min(ROW_TILE, math.gcd(lay.seq, lay.bl * lay.clen))
    assert a.shape[0] % tm == 0 and b.shape[1] == d

    def row_of(i):
        return jnp.where(i * tm < lay.tx, (i * tm) // lay.seq, lay.bl)

    def add(acc, x_tile, m_blk, *nw):
        y = x_tile + (coef * m_blk[kg:kg + 1, :]) * acc
        if then_norm is None:
            return acc, y
        k = then_norm[1]
        return acc, y, _norm_mod_f(y, nw[0], m_blk[k:k + 1, :], m_blk[k + 1:k + 2, :])

    extras = [(x, d), (modv, row_of, "rows")]
    outs = [(d, F32), (d, F32)]
    if then_norm is not None:
        extras.append((then_norm[0].reshape(1, 1, d), lambda i: 0, "rows"))
        outs.append((d, BF16))
    return _mm(a, b, "nn", None, name, tm=tm, tn=d, extras=extras, epilogue=add, outs=outs, ride=ride)


def _resid_bwd(lay, nseg, dy, o, modv, kg, coef, name):
    d = lay.d
    nrow = lay.bl + (1 if nseg > lay.nsx else 0)

    def body(dy_ref, o_ref, m_ref, do_ref, dg_ref):
        s = pl.program_id(1)
        dy = dy_ref[...]
        do_ref[...] = (dy * (coef * m_ref[kg:kg + 1, :])).astype(do_ref.dtype)
        _acc(dg_ref, lay.first_of_row(s), jnp.sum(dy * o_ref[...], axis=0, keepdims=True) * coef)

    return _tok_call(name, body, 1, nseg, [_tok(lay, d), _tok(lay, d), _row(lay, N_MOD, d)],
                     [_tok(lay, d), _row(lay, 1, d)],
                     [jax.ShapeDtypeStruct((nseg * lay.tt, d), BF16), jax.ShapeDtypeStruct((nrow, 1, d), F32)],
                     (dy, o, modv))


def _final_loss(lay, x, wf, target, name):
    d = lay.d

    def body(x_ref, w_ref, t_ref, loss_ref, dx_ref, dw_ref):
        s = pl.program_id(1)

        def f(xv, wv):
            return xv * lax.rsqrt(jnp.mean(xv * xv, axis=-1, keepdims=True) + EPS) * wv

        y, vjp = jax.vjp(f, x_ref[...], w_ref[...])
        err = y - t_ref[...]
        part = 0.5 * jnp.sum(jnp.sum(err * err, axis=-1, keepdims=True), axis=0, keepdims=True) / d
        dx, dw = vjp(err / d)
        dx_ref[...] = dx
        _acc(loss_ref, s == 0, part)
        _acc(dw_ref, s == 0, dw)

    return _tok_call(name, body, 1, lay.nsx, [_tok(lay, d), _glob(1, d), _tok(lay, d)],
                     [_glob(1, 1), _tok(lay, d), _glob(1, d)],
                     [jax.ShapeDtypeStruct((1, 1), F32), jax.ShapeDtypeStruct((lay.tx, d), F32),
                      jax.ShapeDtypeStruct((1, d), F32)], (x, wf, target))


class _Mix:
    def __init__(self, d, heads):
        self.d_ssm = d
        self.d_conv = d
        self.heads = heads
        assert heads * HEAD_DIM == d and heads % (2 * SSD_GROUPS) == 0 and 2 * heads <= DT_LANES
        self.gn = SSD_GROUPS * N_STATE
        self.xw = d + 2 * self.gn
        self.off_x = 0
        self.off_dt = self.xw
        self.off_glu = self.xw + DT_PAD
        self.off_z = self.off_glu + 2 * d
        self.pw = self.off_z + d
        assert self.off_z % d == 0
        self.ref_x = d
        self.ref_dt = d + self.xw
        self.ref_glu = self.ref_dt + 2 * heads
        self.cc = self.xw if self.off_x % self.xw == 0 else _pick(self.xw, (512, 256, 128))


def _conv5_fwd(lay, mx, proj, cw, cb, name):
    c, tt = mx.cc, lay.tt
    cb0 = mx.off_x // c
    assert mx.off_x % c == 0

    def body(prev_ref, cur_ref, next_ref, w_ref, b_ref, pre_ref, act_ref, ext_ref):
        s = pl.program_id(1)
        ext_ref[0:HALO, :] = jnp.where(lay.seq_first(s), 0.0, prev_ref[...])
        ext_ref[HALO:HALO + tt, :] = cur_ref[...]
        ext_ref[HALO + tt:, :] = jnp.where(lay.seq_last(s), 0.0, next_ref[...])
        acc = jnp.zeros((tt, c), F32) + b_ref[...]
        for k in range(5):
            acc = acc + w_ref[k:k + 1, :] * ext_ref[pl.ds(HALO + k - 2, tt), :]
        pre_ref[...] = acc
        act_ref[...] = _silu(acc)

    sh = jax.ShapeDtypeStruct((lay.ta, mx.xw), F32)
    return _tok_call(name, body, mx.xw // c, lay.ns,
                     [_halo_prev(lay, c, cb0), _tok(lay, c, cb0), _halo_next(lay, c, cb0), _glob(8, c, 0), _glob(1, c, 0)],
                     [_tok(lay, c), _tok(lay, c)], [sh, sh], (proj, proj, proj, cw, cb),
                     scratch=[pltpu.VMEM((tt + 2 * HALO, c), F32)])


def _conv5_bwd(lay, mx, proj, pre, dact_f, dact_b, cw, dproj, name):
    c, tt = mx.cc, lay.tt
    cb0 = mx.off_x // c

    def dsilu(p):
        sg = jax.nn.sigmoid(p)
        return sg * (1.0 + p * (1.0 - sg))

    def body(xp_ref, xc_ref, xn_ref, pp_ref, pc_ref, pn_ref, fp_ref, fc_ref, fn_ref, bp_ref, bc_ref, bn_ref, w_ref,
             buf_ref, dx_ref, dw_ref, db_ref, extx_ref, extd_ref):
        s = pl.program_id(1)
        first, last = lay.seq_first(s), lay.seq_last(s)
        dcur = (fc_ref[...] + bc_ref[...]) * dsilu(pc_ref[...])
        extd_ref[0:HALO, :] = jnp.where(first, 0.0, (fp_ref[...] + bp_ref[...]) * dsilu(pp_ref[...]))
        extd_ref[HALO:HALO + tt, :] = dcur
        extd_ref[HALO + tt:, :] = jnp.where(last, 0.0, (fn_ref[...] + bn_ref[...]) * dsilu(pn_ref[...]))
        extx_ref[0:HALO, :] = jnp.where(first, 0.0, xp_ref[...])
        extx_ref[HALO:HALO + tt, :] = xc_ref[...]
        extx_ref[HALO + tt:, :] = jnp.where(last, 0.0, xn_ref[...])
        dx = jnp.zeros((tt, c), F32)
        rows = []
        for k in range(5):
            dx = dx + w_ref[k:k + 1, :] * extd_ref[pl.ds(HALO - (k - 2), tt), :]
            rows.append(jnp.sum(dcur * extx_ref[pl.ds(HALO + k - 2, tt), :], axis=0, keepdims=True))
        dx_ref[...] = dx.astype(dx_ref.dtype)
        rows.append(jnp.zeros((3, c), F32))
        _acc(dw_ref, s == 0, jnp.concatenate(rows, axis=0))
        _acc(db_ref, s == 0, jnp.sum(dcur, axis=0, keepdims=True))

    three = lambda cbx: [_halo_prev(lay, c, cbx), _tok(lay, c, cbx), _halo_next(lay, c, cbx)]
    ext = pltpu.VMEM((tt + 2 * HALO, c), F32)
    return _tok_call(name, body, mx.xw // c, lay.ns,
                     three(cb0) + three(0) + three(0) + three(0) + [_glob(8, c, 0), pl.BlockSpec(memory_space=pl.ANY)],
                     [_tok(lay, c, cb0), _glob(8, c, 0), _glob(1, c, 0)],
                     [jax.ShapeDtypeStruct(dproj.shape, dproj.dtype), jax.ShapeDtypeStruct((8, mx.xw), F32),
                      jax.ShapeDtypeStruct((1, mx.xw), F32)],
                     (proj, proj, proj, pre, pre, pre, dact_f, dact_f, dact_f, dact_b, dact_b, dact_b, cw, dproj),
                     scratch=[ext, ext], aliases={13: 0})


def _softplus(v):
    return jnp.maximum(v, 0.0) + jnp.log1p(jnp.exp(-jnp.abs(v)))


def _dt_fwd(lay, mx, proj, bias, name):
    cb = mx.off_dt // DT_LANES

    def body(p_ref, b_ref, dt_ref):
        dt_ref[...] = _softplus(p_ref[...] + b_ref[...])

    return _tok_call(name, body, 1, lay.ns, [_tok(lay, DT_LANES, cb), _glob(1, DT_LANES)], _tok(lay, DT_LANES),
                     jax.ShapeDtypeStruct((lay.ta, DT_LANES), F32), (proj, bias))


def _dt_bwd(lay, mx, proj, bias, parts, dproj, name):
    cb = mx.off_dt // DT_LANES
    wb = DT_PAD if mx.off_dt % DT_PAD == 0 else DT_LANES
    ncb = DT_PAD // wb

    def body(p_ref, b_ref, a_ref, b2_ref, c_ref, d_ref, buf_ref, dp_ref, db_ref):
        j, s = pl.program_id(0), pl.program_id(1)

        @pl.when(j == 0)
        def _():
            ddt = (a_ref[...] + b2_ref[...]) + (c_ref[...] + d_ref[...])
            draw = ddt * jax.nn.sigmoid(p_ref[...] + b_ref[...])
            dp_ref[:, 0:DT_LANES] = draw.astype(dp_ref.dtype)
            if wb > DT_LANES:
                dp_ref[:, DT_LANES:] = jnp.zeros((lay.tt, wb - DT_LANES), dp_ref.dtype)
            _acc(db_ref, s == 0, jnp.sum(draw, axis=0, keepdims=True))

        @pl.when(j > 0)
        def _():
            dp_ref[...] = jnp.zeros_like(dp_ref)

    t = pl.BlockSpec((lay.tt, DT_LANES), lambda j, s: (s, 0))
    return _tok_call(name, body, ncb, lay.ns,
                     [pl.BlockSpec((lay.tt, DT_LANES), lambda j, s: (s, cb)), _glob(1, DT_LANES), t, t, t, t,
                      pl.BlockSpec(memory_space=pl.ANY)],
                     [_tok(lay, wb, mx.off_dt // wb), _glob(1, DT_LANES)],
                     [jax.ShapeDtypeStruct(dproj.shape, dproj.dtype), jax.ShapeDtypeStruct((1, DT_LANES), F32)],
                     (proj, bias) + tuple(parts) + (dproj,), aliases={6: 0})


def _scan_mask(rev):
    r = lax.broadcasted_iota(jnp.int32, (CHUNK, CHUNK), 0)
    c = lax.broadcasted_iota(jnp.int32, (CHUNK, CHUNK), 1)
    return (r <= c) if rev else (r >= c)


def _split_bf16(x):
    hi = x.astype(BF16)
    return hi, (x - hi.astype(F32)).astype(BF16)


@functools.partial(jax.custom_vjp, nondiff_argnums=(0,))
def _cum_cols(rev, x):
    m = _scan_mask(rev).astype(BF16)
    hi, lo = _split_bf16(x)
    return jnp.dot(m, hi, preferred_element_type=F32) + jnp.dot(m, lo, preferred_element_type=F32)


_cum_cols.defvjp(lambda rev, x: (_cum_cols(rev, x), None), lambda rev, _, g: (_cum_cols(not rev, g),))


@functools.partial(jax.custom_vjp, nondiff_argnums=(0,))
def _cum_rows(rev, x):
    m = _scan_mask(not rev).astype(BF16)
    hi, lo = _split_bf16(x)
    return jnp.dot(hi, m, preferred_element_type=F32) + jnp.dot(lo, m, preferred_element_type=F32)


_cum_rows.defvjp(lambda rev, x: (_cum_rows(rev, x), None), lambda rev, _, g: (_cum_rows(not rev, g),))


def _ssd_chunk(xh_pairs, bcs, ccs, dtc, dtr, a_row, a_col, st_pairs, *, rev, heads, col0):
    cs_c, cs_r, tot, scores = _ssd_shared(bcs, ccs, dtc, dtr, a_row, a_col, rev=rev)
    ppg = heads // (2 * SSD_GROUPS)
    ys, sts = [], []
    for g in range(SSD_GROUPS):
        y, st = _ssd_group(xh_pairs[g * ppg:(g + 1) * ppg], bcs[g], ccs[g], st_pairs[g], cs_c, cs_r, tot, dtc,
                           scores[g], rev=rev, col=col0 + 2 * ppg * g)
        ys.append(y)
        sts.append(st)
    return ys, sts


_NT = (((1,), (1,)), ((), ()))
_TN = (((0,), (0,)), ((), ()))


def _ssd_shared(bcs, ccs, dtc, dtr, a_row, a_col, *, rev):
    da_c = dtc * a_row
    cs_c = _cum_cols(rev, da_c)
    cs_r = _cum_rows(rev, dtr * a_col)
    tot = jnp.sum(da_c, axis=0, keepdims=True)
    scores = [lax.dot_general(ccs[g].astype(BF16), bcs[g].astype(BF16), _NT, preferred_element_type=F32)
              for g in range(SSD_GROUPS)]
    return cs_c, cs_r, tot, scores


def _ssd_group(xh_pairs, bc, cc, st, cs_c, cs_r, tot, dtc, score, *, rev, col):
    n = CHUNK
    mask = _scan_mask(rev)
    lane = lax.broadcasted_iota(jnp.int32, (n, DT_LANES), 1)
    sub = lax.broadcasted_iota(jnp.int32, (DT_LANES, n), 0)
    lane1 = lax.broadcasted_iota(jnp.int32, (1, DT_LANES), 1)
    left = lax.broadcasted_iota(jnp.int32, (n, 2 * HEAD_DIM), 1) < HEAD_DIM
    top = lax.broadcasted_iota(jnp.int32, (2 * HEAD_DIM, 1), 0) < HEAD_DIM
    xs_all, wst_all, ecs_all, edec_all, y_diag = [], [], [], [], []
    for p, xh in enumerate(xh_pairs):
        per = []
        for c in (col + 2 * p, col + 2 * p + 1):
            csv = jnp.sum(jnp.where(lane == c, cs_c, 0.0), axis=1, keepdims=True)
            csr = jnp.sum(jnp.where(sub == c, cs_r, 0.0), axis=0, keepdims=True)
            dtv = jnp.sum(jnp.where(lane == c, dtc, 0.0), axis=1, keepdims=True)
            tv = jnp.sum(jnp.where(lane1 == c, tot, 0.0), axis=1, keepdims=True)
            m = score * jnp.exp(jnp.where(mask, csv - csr, NEG_BIG))
            per.append((csv, dtv, tv, m))
        (cs1, dt1, t1, m1), (cs2, dt2, t2, m2) = per
        xs = xh * jnp.where(left, dt1, dt2)
        both = jnp.dot(jnp.concatenate([m1, m2], axis=0).astype(BF16), xs.astype(BF16), preferred_element_type=F32)
        y_diag.append(jnp.where(left, both[:n], both[n:]))
        xs_all.append(xs)
        ecs_all.append(jnp.where(left, jnp.exp(cs1), jnp.exp(cs2)))
        wst_all.append(jnp.where(left, jnp.exp(t1 - cs1), jnp.exp(t2 - cs2)))
        edec_all.append(jnp.where(top, jnp.exp(t1), jnp.exp(t2)))
    cat = lambda parts, axis: parts[0] if len(parts) == 1 else jnp.concatenate(parts, axis=axis)
    xs, wst, ecs = cat(xs_all, 1), cat(wst_all, 1), cat(ecs_all, 1)
    y_off = lax.dot_general(cc.astype(BF16), st.astype(BF16), _NT, preferred_element_type=F32) * ecs
    cst = lax.dot_general((xs * wst).astype(BF16), bc.astype(BF16), _TN, preferred_element_type=F32)
    return cat(y_diag, 1) + y_off, st * cat(edec_all, 0) + cst


class _Scan:
    def __init__(self, lay, rev):
        self.ncx, self.ncc, self.bl, self.rev = lay.seq // CHUNK, lay.clen // CHUNK, lay.bl, rev
        self.nct = self.ncx + self.ncc

    def chunk(self, b, pos):
        kc = (self.ncc - 1 - pos) if self.rev else pos
        kx = (self.ncx - 1 - (pos - self.ncc)) if self.rev else (pos - self.ncc)
        return jnp.where(pos < self.ncc, self.bl * self.ncx + b * self.ncc + kc, b * self.ncx + kx)


def _ssd_io(mx, x_ref, st_src):
    np_ = mx.heads // 2
    d = mx.d_ssm
    xh = [x_ref[:, 128 * p:128 * (p + 1)] for p in range(np_)]
    bcs = [x_ref[:, d + N_STATE * g:d + N_STATE * (g + 1)] for g in range(SSD_GROUPS)]
    ccs = [x_ref[:, d + mx.gn + N_STATE * g:d + mx.gn + N_STATE * (g + 1)] for g in range(SSD_GROUPS)]
    gw = d // SSD_GROUPS
    sts = [st_src[gw * g:gw * (g + 1), :] for g in range(SSD_GROUPS)]
    return xh, bcs, ccs, sts


def _ssd_fwd(lay, mx, xbc, dt, dtt, a_row, a_col, rev, name):
    sc = _Scan(lay, rev)
    col0 = mx.heads if rev else 0
    hp = mx.heads * HEAD_DIM

    def body(x_ref, dt_ref, dtt_ref, ar_ref, ac_ref, y_ref, hp_ref, st_ref):
        @pl.when(pl.program_id(1) == 0)
        def _():
            st_ref[...] = jnp.zeros_like(st_ref)

        hp_ref[...] = st_ref[...]
        xh, bcs, ccs, sts = _ssd_io(mx, x_ref, st_ref)
        ys, new = _ssd_chunk(xh, bcs, ccs, dt_ref[...], dtt_ref[...], ar_ref[...], ac_ref[...], sts,
                             rev=rev, heads=mx.heads, col0=col0)
        gw = mx.d_ssm // SSD_GROUPS
        for g in range(SSD_GROUPS):
            y_ref[:, gw * g:gw * (g + 1)] = ys[g]
            st_ref[gw * g:gw * (g + 1), :] = new[g]

    ch = sc.chunk
    return pl.pallas_call(
        body, grid=(lay.bl, sc.nct),
        in_specs=[pl.BlockSpec((CHUNK, mx.xw), lambda b, i: (ch(b, i), 0)),
                  pl.BlockSpec((CHUNK, DT_LANES), lambda b, i: (ch(b, i), 0)),
                  pl.BlockSpec((DT_LANES, CHUNK), lambda b, i: (0, ch(b, i))),
                  pl.BlockSpec((1, DT_LANES), lambda b, i: (0, 0)),
                  pl.BlockSpec((DT_LANES, 1), lambda b, i: (0, 0))],
        out_specs=[pl.BlockSpec((CHUNK, mx.d_ssm), lambda b, i: (ch(b, i), 0)),
                   pl.BlockSpec((hp, N_STATE), lambda b, i: (b * sc.nct + i, 0))],
        out_shape=[jax.ShapeDtypeStruct((lay.ta, mx.d_ssm), F32),
                   jax.ShapeDtypeStruct((lay.bl * sc.nct * hp, N_STATE), F32)],
        scratch_shapes=[pltpu.VMEM((hp, N_STATE), F32)], name=name, compiler_params=_cparams(2),
    )(xbc, dt, dtt, a_row, a_col)


def _ssd_bwd(lay, mx, xbc, dt, dtt, a_row, a_col, hprev, dy, dskip, rev, name):
    sc = _Scan(lay, rev)
    col0 = mx.heads if rev else 0
    hp = mx.heads * HEAD_DIM
    np_ = mx.heads // 2
    d = mx.d_ssm
    with_skip = dskip is not None

    def body(*refs):
        if with_skip:
            x_ref, dt_ref, dtt_ref, ar_ref, ac_ref, hp_ref, dy_ref, sk_ref = refs[:8]
            rest = refs[8:]
        else:
            x_ref, dt_ref, dtt_ref, ar_ref, ac_ref, hp_ref, dy_ref = refs[:7]
            rest = refs[7:]
        dx_ref, ddc_ref, ddr_ref, dar_ref, dac_ref, ds_ref = rest
        b, i = pl.program_id(0), pl.program_id(1)

        @pl.when(i == 0)
        def _():
            ds_ref[...] = jnp.zeros_like(ds_ref)

        xh, bcs, ccs, sts = _ssd_io(mx, x_ref, hp_ref)
        dtc = dt_ref[...]
        shared, vjp_shared = jax.vjp(functools.partial(_ssd_shared, rev=rev), bcs, ccs, dtc, dtt_ref[...],
                                     ar_ref[...], ac_ref[...])
        cs_c, cs_r, tot, scores = shared
        plus = lambda acc, v: v if acc is None else acc + v
        d_cs_c = d_cs_r = d_tot = ddc = None
        d_scores, dbc, dcc = [], [], []
        ppg = np_ // SSD_GROUPS
        gw = d // SSD_GROUPS
        for g in range(SSD_GROUPS):
            dyg = dy_ref[:, gw * g:gw * (g + 1)]
            fn = functools.partial(_ssd_group, rev=rev, col=col0 + 2 * ppg * g)
            _, vjp = jax.vjp(fn, xh[g * ppg:(g + 1) * ppg], bcs[g], ccs[g], sts[g], cs_c, cs_r, tot, dtc, scores[g])
            dxh, dbc_g, dcc_g, dst, dcs_c_g, dcs_r_g, dtot_g, ddc_g, dsc_g = vjp((dyg, ds_ref[gw * g:gw * (g + 1), :]))
            for q in range(ppg):
                p = g * ppg + q
                v = dxh[q]
                if with_skip:
                    v = v + dyg[:, 128 * q:128 * (q + 1)] * sk_ref[:, 128 * p:128 * (p + 1)]
                dx_ref[:, 128 * p:128 * (p + 1)] = v
            ds_ref[gw * g:gw * (g + 1), :] = dst
            d_cs_c, d_cs_r, d_tot, ddc = plus(d_cs_c, dcs_c_g), plus(d_cs_r, dcs_r_g), plus(d_tot, dtot_g), plus(ddc, ddc_g)
            d_scores.append(dsc_g)
            dbc.append(dbc_g)
            dcc.append(dcc_g)
        dbc_s, dcc_s, ddc_s, ddr, dar, dac = vjp_shared((d_cs_c, d_cs_r, d_tot, d_scores))
        ddc = ddc + ddc_s
        dbc = [dbc[g] + dbc_s[g] for g in range(SSD_GROUPS)]
        dcc = [dcc[g] + dcc_s[g] for g in range(SSD_GROUPS)]
        for g in range(SSD_GROUPS):
            dx_ref[:, d + N_STATE * g:d + N_STATE * (g + 1)] = dbc[g]
            dx_ref[:, d + mx.gn + N_STATE * g:d + mx.gn + N_STATE * (g + 1)] = dcc[g]
        ddc_ref[...] = ddc
        ddr_ref[...] = ddr
        first = jnp.logical_and(b == 0, i == 0)
        _acc(dar_ref, first, dar)
        _acc(dac_ref, first, dac)

    ch = lambda b, i: sc.chunk(b, sc.nct - 1 - i)
    in_specs = [pl.BlockSpec((CHUNK, mx.xw), lambda b, i: (ch(b, i), 0)),
                pl.BlockSpec((CHUNK, DT_LANES), lambda b, i: (ch(b, i), 0)),
                pl.BlockSpec((DT_LANES, CHUNK), lambda b, i: (0, ch(b, i))),
                pl.BlockSpec((1, DT_LANES), lambda b, i: (0, 0)),
                pl.BlockSpec((DT_LANES, 1), lambda b, i: (0, 0)),
                pl.BlockSpec((hp, N_STATE), lambda b, i: (b * sc.nct + sc.nct - 1 - i, 0)),
                pl.BlockSpec((CHUNK, d), lambda b, i: (ch(b, i), 0))]
    inputs = [xbc, dt, dtt, a_row, a_col, hprev, dy]
    if with_skip:
        in_specs.append(pl.BlockSpec((1, d), lambda b, i: (0, 0)))
        inputs.append(dskip)
    return pl.pallas_call(
        body, grid=(lay.bl, sc.nct), in_specs=in_specs,
        out_specs=[pl.BlockSpec((CHUNK, mx.xw), lambda b, i: (ch(b, i), 0)),
                   pl.BlockSpec((CHUNK, DT_LANES), lambda b, i: (ch(b, i), 0)),
                   pl.BlockSpec((DT_LANES, CHUNK), lambda b, i: (0, ch(b, i))),
                   pl.BlockSpec((1, DT_LANES), lambda b, i: (0, 0)),
                   pl.BlockSpec((DT_LANES, 1), lambda b, i: (0, 0))],
        out_shape=[jax.ShapeDtypeStruct((lay.ta, mx.xw), F32), jax.ShapeDtypeStruct((lay.ta, DT_LANES), F32),
                   jax.ShapeDtypeStruct((DT_LANES, lay.ta), F32), jax.ShapeDtypeStruct((1, DT_LANES), F32),
                   jax.ShapeDtypeStruct((DT_LANES, 1), F32)],
        scratch_shapes=[pltpu.VMEM((hp, N_STATE), F32)], name=name, compiler_params=_cparams(2),
    )(*inputs)


def _gate_f(yf, yb, xh, z, drow, nw):
    dd = yf.shape[-1]
    half = dd // SSD_GROUPS
    yz = (yf + yb + drow * xh) * _silu(z)
    lo = lax.broadcasted_iota(jnp.int32, yz.shape, 1) < half
    sq = yz * yz
    ms1 = jnp.sum(jnp.where(lo, sq, 0.0), axis=-1, keepdims=True) / half
    ms2 = jnp.sum(jnp.where(lo, 0.0, sq), axis=-1, keepdims=True) / half
    return yz * jnp.where(lo, lax.rsqrt(ms1 + EPS), lax.rsqrt(ms2 + EPS)) * nw


def _gate_fwd(lay, mx, yf, yb, xbc, proj, drow, nw, name):
    d = mx.d_ssm

    def body(yf_ref, yb_ref, xh_ref, z_ref, d_ref, w_ref, o_ref):
        o_ref[...] = _gate_f(yf_ref[...], yb_ref[...], xh_ref[...], z_ref[...], d_ref[...], w_ref[...]).astype(o_ref.dtype)

    t, z = _tok(lay, d), _tok(lay, d, mx.off_z // d)
    return _tok_call(name, body, 1, lay.nsx, [t, t, t, z, _glob(1, d), _glob(1, d)], t,
                     jax.ShapeDtypeStruct((lay.tx, d + mx.d_conv), BF16), (yf, yb, xbc, proj, drow, nw))


def _gate_bwd(lay, mx, yf, yb, xbc, proj, drow, nw, dcat, name):
    d = mx.d_ssm
    nsx = lay.nsx

    def body(yf_ref, yb_ref, xh_ref, z_ref, d_ref, w_ref, dc_ref, dy_ref, dz_ref, dd_ref, dw_ref):
        s = pl.program_id(1)

        @pl.when(s < nsx)
        def _():
            _, vjp = jax.vjp(_gate_f, yf_ref[...], yb_ref[...], xh_ref[...], z_ref[...], d_ref[...], w_ref[...])
            dyf, _, _, dz, dd, dw = vjp(dc_ref[...])
            dy_ref[...] = dyf
            dz_ref[...] = dz.astype(dz_ref.dtype)
            _acc(dd_ref, s == 0, dd)
            _acc(dw_ref, s == 0, dw)

        @pl.when(s >= nsx)
        def _():
            dy_ref[...] = jnp.zeros_like(dy_ref)
            dz_ref[...] = jnp.zeros_like(dz_ref)

    t, z = _tok(lay, d), _tok(lay, d, mx.off_z // d)
    return _tok_call(name, body, 1, lay.ns, [t, t, t, z, _glob(1, d), _glob(1, d), _tok(lay, d, clamp=nsx - 1)],
                     [t, z, _glob(1, d), _glob(1, d)],
                     [jax.ShapeDtypeStruct((lay.ta, d), F32), jax.ShapeDtypeStruct((lay.ta, mx.pw), BF16),
                      jax.ShapeDtypeStruct((1, d), F32), jax.ShapeDtypeStruct((1, d), F32)],
                     (yf, yb, xbc, proj, drow, nw, dcat))


def _glu_fwd(lay, mx, proj, name):
    d = mx.d_conv
    c = math.gcd(mx.off_glu, d)
    cb = mx.off_glu // c

    def body(a_ref, b_ref, o_ref):
        o_ref[...] = a_ref[...] * jax.nn.sigmoid(b_ref[...])

    return _tok_call(name, body, d // c, lay.nsx, [_tok(lay, c, cb), _tok(lay, c, cb + d // c)], _tok(lay, c),
                     jax.ShapeDtypeStruct((lay.tx, d), F32), (proj, proj))


def _glu_bwd(lay, mx, proj, du, dproj, name):
    d = mx.d_conv
    nsx = lay.nsx
    whole = mx.off_glu % (2 * d) == 0
    c = d if whole else math.gcd(mx.off_glu, d)
    cb = mx.off_glu // c
    nc = d // c

    def body(a_ref, b_ref, du_ref, buf_ref, o_ref):
        j, s = pl.program_id(0), pl.program_id(1)

        @pl.when(s < nsx)
        def _():
            sg = jax.nn.sigmoid(b_ref[...])
            da = du_ref[...] * sg
            db = da * a_ref[...] * (1.0 - sg)
            if whole:
                o_ref[:, 0:d] = da.astype(o_ref.dtype)
                o_ref[:, d:] = db.astype(o_ref.dtype)
            else:
                o_ref[...] = jnp.where(j < nc, da, db).astype(o_ref.dtype)

        @pl.when(s >= nsx)
        def _():
            o_ref[...] = jnp.zeros_like(o_ref)

    win = lambda half: pl.BlockSpec((lay.tt, c), lambda j, s: (s, cb + half * nc + j % nc))
    out = _tok(lay, 2 * d, mx.off_glu // (2 * d)) if whole else _tok(lay, c, cb)
    return _tok_call(name, body, 1 if whole else 2 * nc, lay.ns,
                     [win(0), win(1), pl.BlockSpec((lay.tt, c), lambda j, s: (jnp.minimum(s, nsx - 1), j % nc)),
                      pl.BlockSpec(memory_space=pl.ANY)],
                     out, jax.ShapeDtypeStruct(dproj.shape, dproj.dtype), (proj, proj, du, dproj), aliases={3: 0})


def _axial(lay, mx, u, dy, cw, cb, name):
    d, seq = mx.d_conv, lay.seq
    kw = cw.shape[0]
    pad = kw // 2
    c = _pick(d // 2, (256, 128))
    ncb = d // c
    zpad = GRID_W * pad
    zpad = -(-zpad // 8) * 8
    backward = dy is not None

    def shifted(ext_ref, off):
        return ext_ref[pl.ds(zpad + off, seq), :]

    def valid_row(off):
        col = lax.broadcasted_iota(jnp.int32, (seq, c), 0) % GRID_W
        return jnp.logical_and(col + off >= 0, col + off < GRID_W)

    def fill(ext_ref, v):
        ext_ref[0:zpad, :] = jnp.zeros((zpad, c), F32)
        ext_ref[zpad:zpad + seq, :] = v
        ext_ref[zpad + seq:, :] = jnp.zeros((zpad, c), F32)

    def conv(ext_ref, w_ref, is_row, sign):
        acc = jnp.zeros((seq, c), F32)
        for k in range(kw):
            off = sign * ((k - pad) if is_row else GRID_W * (k - pad))
            v = shifted(ext_ref, off)
            if is_row:
                v = jnp.where(valid_row(off), v, 0.0)
            acc = acc + w_ref[k:k + 1, :] * v
        return acc

    def fwd_body(u_ref, w_ref, b_ref, o_ref, ext_ref):
        j = pl.program_id(0)
        fill(ext_ref, u_ref[...])

        @pl.when(j < ncb // 2)
        def _():
            o_ref[...] = conv(ext_ref, w_ref, True, 1) + b_ref[...]

        @pl.when(j >= ncb // 2)
        def _():
            o_ref[...] = conv(ext_ref, w_ref, False, 1) + b_ref[...]

    def bwd_body(u_ref, dy_ref, w_ref, du_ref, dw_ref, db_ref, extu_ref, extd_ref):
        j, b = pl.program_id(0), pl.program_id(1)
        dyv = dy_ref[...]
        fill(extu_ref, u_ref[...])
        fill(extd_ref, dyv)

        def grads(is_row):
            du_ref[...] = conv(extd_ref, w_ref, is_row, -1)
            rows = []
            for k in range(kw):
                off = (k - pad) if is_row else GRID_W * (k - pad)
                v = shifted(extu_ref, off)
                if is_row:
                    v = jnp.where(valid_row(off), v, 0.0)
                rows.append(jnp.sum(dyv * v, axis=0, keepdims=True))
            _acc(dw_ref, b == 0, jnp.concatenate(rows, axis=0))

        @pl.when(j < ncb // 2)
        def _():
            grads(True)

        @pl.when(j >= ncb // 2)
        def _():
            grads(False)

        _acc(db_ref, b == 0, jnp.sum(dyv, axis=0, keepdims=True))

    seq_spec = pl.BlockSpec((seq, c), lambda j, b: (b, j))
    w_spec = pl.BlockSpec((kw, c), lambda j, b: (0, j))
    b_spec = pl.BlockSpec((1, c), lambda j, b: (0, j))
    ext = pltpu.VMEM((seq + 2 * zpad, c), F32)
    if not backward:
        return pl.pallas_call(fwd_body, grid=(ncb, lay.bl), in_specs=[seq_spec, w_spec, b_spec], out_specs=seq_spec,
                              out_shape=jax.ShapeDtypeStruct((lay.tx, d), F32), scratch_shapes=[ext], name=name,
                              compiler_params=_cparams(2))(u, cw, cb)
    return pl.pallas_call(bwd_body, grid=(ncb, lay.bl), in_specs=[seq_spec, seq_spec, w_spec],
                          out_specs=[seq_spec, w_spec, b_spec],
                          out_shape=[jax.ShapeDtypeStruct((lay.tx, d), F32), jax.ShapeDtypeStruct((kw, d), F32),
                                     jax.ShapeDtypeStruct((1, d), F32)],
                          scratch_shapes=[ext, ext], name=name, compiler_params=_cparams(2))(u, dy, cw)


def _ln_silu_f(u, w, b):
    mu = jnp.mean(u, axis=-1, keepdims=True)
    var = jnp.mean(jnp.square(u - mu), axis=-1, keepdims=True)
    return _silu((u - mu) * lax.rsqrt(var + EPS) * w + b)


def _ln_fwd(lay, mx, u, w, b, cat, name):
    d = mx.d_conv
    assert mx.d_ssm % d == 0

    def body(u_ref, w_ref, b_ref, cat_ref, o_ref):
        o_ref[...] = _ln_silu_f(u_ref[...], w_ref[...], b_ref[...]).astype(o_ref.dtype)

    return _tok_call(name, body, 1, lay.nsx,
                     [_tok(lay, d), _glob(1, d), _glob(1, d), pl.BlockSpec(memory_space=pl.ANY)],
                     _tok(lay, d, mx.d_ssm // d), jax.ShapeDtypeStruct(cat.shape, cat.dtype), (u, w, b, cat),
                     aliases={3: 0})


def _ln_bwd(lay, mx, u, w, b, dcat, name):
    d = mx.d_conv

    def body(u_ref, w_ref, b_ref, dc_ref, du_ref, dw_ref, db_ref):
        s = pl.program_id(1)
        _, vjp = jax.vjp(_ln_silu_f, u_ref[...], w_ref[...], b_ref[...])
        du, dw, db = vjp(dc_ref[...])
        du_ref[...] = du
        _acc(dw_ref, s == 0, dw)
        _acc(db_ref, s == 0, db)

    return _tok_call(name, body, 1, lay.nsx, [_tok(lay, d), _glob(1, d), _glob(1, d), _tok(lay, d, 1)],
                     [_tok(lay, d), _glob(1, d), _glob(1, d)],
                     [jax.ShapeDtypeStruct((lay.tx, d), F32), jax.ShapeDtypeStruct((1, d), F32),
                      jax.ShapeDtypeStruct((1, d), F32)], (u, w, b, dcat))


def _gate_tile(dff):
    return dff // 2 if (dff // 2) % 128 == 0 else dff


def _ffn_fwd(lay, nseg, x, nw, modv, k0, wts, tag, ride=None, ride_down=None, h=None, then_norm=None):
    wgu, wd, ft = wts
    if h is None:
        h = _norm_mod_fwd(lay, nseg, x, nw, modv, k0, tag + "_norm")
    t = h.shape[0]

    def act(acc):
        g, u = acc[:, :ft], acc[:, ft:]
        sg = jax.nn.sigmoid(g)
        sl = g * sg
        return jnp.concatenate([u * (sg * (1.0 + g * (1.0 - sg))), sl], axis=1), sl * u

    res = _mm(h, wgu, "nn", None, tag + "_gu", tn=2 * ft, tm=256 if t % 256 == 0 else None,
              epilogue=act, outs=[(2 * ft, BF16), (ft, BF16)], ride=ride)
    (s, a), rode = res if ride else (res, None)
    if callable(wd):
        wd = wd(rode)
    res = _mm_resid(lay, a, wd, x, modv, k0 + 2, 0.5, tag + "_down", then_norm=then_norm, ride=ride_down)
    (o, y, *h_next), rode_down = res if ride_down else (res, None)
    return y, (x, h, s, a, o), (rode, rode_down), (h_next[0] if h_next else None)


def _ffn_bwd(lay, nseg, dy, saved, nw, modv, k0, wts, tag, nout=None, hooks=None):
    wgu, wd, ft = wts
    x, h, s, a, o = saved
    do, dgate = _resid_bwd(lay, nseg, dy, o, modv, k0 + 2, 0.5, tag + "_dres")

    def through_act(da, s_tile):
        return (jnp.concatenate([da, da], axis=1) * s_tile.astype(F32),)

    (dgu,) = _mm(do, wd.T, "nn", None, tag + "_da", tn=ft, tm=_big_tile(do.shape[0], 1024), extras=[(s, 2 * ft)],
                 epilogue=through_act, outs=[(2 * ft, BF16)])
    dwd = _mm(a, do, "tn", F32, tag + "_dwd")
    if hooks is None:
        dwgu = _mm(h, dgu, "tn", F32, tag + "_dwgu")
        dh = _mm(dgu, wgu, "nt", F32, tag + "_dh")
    else:
        dwgu, rode_a = _mm(h, dgu, "tn", F32, tag + "_dwgu", ride=hooks.after_dwd(dwd))
        dh, rode_b = _mm(dgu, wgu, "nt", F32, tag + "_dh", ride=hooks.after_dwgu(dwgu))
        hooks.take(rode_a, rode_b)
    dx, dnw, dss = _norm_mod_bwd(lay, nseg, nseg, x, nw, modv, k0, dh, dy, tag + "_dnorm", nout=nout)
    return dx, (dwgu, dwd), dnw, jnp.concatenate([dss, dgate], axis=1)


def _local_step(lay, mx, xa, target, modv, w, exch=None):
    d, bl = lay.d, lay.bl
    g = {}
    xa1, ffn1, late, ha = _ffn_fwd(lay, lay.ns, xa, w["norm_ffn1"], modv, 0, w["ffn1"], "ffn1",
                                   ride=exch.late_a if exch else None, ride_down=exch.late_b if exch else None,
                                   then_norm=(w["norm_mix"], 3))
    if exch:
        w = {**w, **exch.unpack_late(w["ffn1"], *late)}
    proj = _mm(ha, w["w_in"], "nn", F32, "mix_in")
    pre, xbc = _conv5_fwd(lay.fine(), mx, proj, w["conv_w"], w["conv_b"], "mix_conv")
    dt = _dt_fwd(lay, mx, proj, w["dt_bias"], "mix_dt")
    dtt = dt.T
    yf, hpf = _ssd_fwd(lay, mx, xbc, dt, dtt, w["a_row"], w["a_col"], False, "ssd_f")
    yb, hpb = _ssd_fwd(lay, mx, xbc, dt, dtt, w["a_row"], w["a_col"], True, "ssd_b")
    cat_y = _gate_fwd(lay, mx, yf, yb, xbc, proj, w["d_row"], w["ssm_norm_w"], "mix_gate")
    u0 = _glu_fwd(lay, mx, proj, "mix_glu")
    uc = _axial(lay, mx, u0, None, w["cconv_w"], w["cconv_b"], "mix_axial")
    cat = _ln_fwd(lay, mx, uc, w["ln_w"], w["ln_b"], cat_y, "mix_ln")
    mix, x2, h2 = _mm_resid(lay, cat, w["w_out"], xa1, modv, 5, 1.0, "mix_out", then_norm=(w["norm_ffn2"], 6))
    x3, ffn2, _, _ = _ffn_fwd(lay, lay.nsx, x2, w["norm_ffn2"], modv, 6, w["ffn2"], "ffn2", h=h2)
    loss, dx3, g["final_norm"] = _final_loss(lay, x3, w["final_norm"], target, "loss")
    dx2, g["ffn2"], g["norm_ffn2"], dmod2 = _ffn_bwd(lay, lay.nsx, dx3, ffn2, w["norm_ffn2"], modv, 6, w["ffn2"], "ffn2")
    dmix, dg2 = _resid_bwd(lay, lay.nsx, dx2, mix, modv, 5, 1.0, "mix_dres")
    dcat = _mm(dmix, w["w_out"], "nt", F32, "mix_dcat")
    g["w_out"] = _mm(cat, dmix, "tn", F32, "mix_dwout")
    duc, g["ln_w"], g["ln_b"] = _ln_bwd(lay, mx, uc, w["ln_w"], w["ln_b"], dcat, "mix_dln")
    du0, g["cconv_w"], g["cconv_b"] = _axial(lay, mx, u0, duc, w["cconv_w"], None, "mix_daxial")
    dyssd, dproj, g["d_row"], g["ssm_norm_w"] = _gate_bwd(lay, mx, yf, yb, xbc, proj, w["d_row"], w["ssm_norm_w"], dcat,
                                                          "mix_dgate")
    dproj = _glu_bwd(lay, mx, proj, du0, dproj, "mix_dglu")
    dxf, ddcf, ddrf, darf, dacf = _ssd_bwd(lay, mx, xbc, dt, dtt, w["a_row"], w["a_col"], hpf, dyssd, w["d_row"],
                                           False, "ssd_df")
    dxb, ddcb, ddrb, darb, dacb = _ssd_bwd(lay, mx, xbc, dt, dtt, w["a_row"], w["a_col"], hpb, dyssd, None,
                                           True, "ssd_db")
    g["a_row"] = (darf + darb) + (dacf + dacb).T
    dproj, g["conv_w"], g["conv_b"] = _conv5_bwd(lay.fine(), mx, proj, pre, dxf, dxb, w["conv_w"], dproj, "mix_dconv")
    dproj, g["dt_bias"] = _dt_bwd(lay, mx, proj, w["dt_bias"], (ddcf, ddcb, ddrf.T, ddrb.T), dproj, "mix_ddt")
    dha = _mm(dproj, w["w_in"], "nt", F32, "mix_dha")
    g["w_in"] = _mm(ha, dproj, "tn", F32, "mix_dwin")
    dxa1, g["norm_mix"], dss_mix = _norm_mod_bwd(lay, lay.ns, lay.nsx, xa1, w["norm_mix"], modv, 3, dha, dx2, "mix_dnorm")
    if exch:
        exch.early_grads(g)
    dx, g["ffn1"], g["norm_ffn1"], dmod1 = _ffn_bwd(lay, lay.ns, dxa1, ffn1, w["norm_ffn1"], modv, 0, w["ffn1"], "ffn1",
                                                   nout=lay.nsx, hooks=exch)
    zrow = lambda t: jnp.concatenate([t, jnp.zeros((1,) + t.shape[1:], F32)], axis=0)
    dmodv = jnp.concatenate([dmod1, dss_mix, zrow(dg2), zrow(dmod2)], axis=1)
    return loss, dx, g, dmodv


class _GatherRide:
    def __init__(self, xs):
        self.arrays = list(xs)
        self.na = len(xs)
        self.out_shapes = [jax.ShapeDtypeStruct((N_DEV,) + tuple(x.shape), x.dtype) for x in xs]
        self.sems = [pltpu.SemaphoreType.DMA((7 * self.na,)), pltpu.SemaphoreType.DMA((7 * self.na,)),
                     pltpu.SemaphoreType.DMA((self.na,))]

    def _plan(self, x_refs, out_refs, send_sems, recv_sems, local_sems):
        mx_, my_, mc_ = lax.axis_index("x"), lax.axis_index("y"), lax.axis_index("c")
        me, sibling = (mx_, my_, mc_), (mx_, my_, 1 - mc_)
        chips = [(1 - mx_, my_), (mx_, 1 - my_), (1 - mx_, 1 - my_)]

        def slot(a, px, py, pc):
            return out_refs[a].at[4 * px + 2 * py + pc]

        def copy(a, k, block, to, own=False):
            return pltpu.make_async_remote_copy(
                src_ref=x_refs[a] if own else slot(a, *block), dst_ref=slot(a, *block),
                send_sem=send_sems.at[7 * a + k], recv_sem=recv_sems.at[7 * a + k], device_id=to, device_id_type=MESH)

        mine = [pltpu.make_async_copy(x_refs[a], slot(a, *me), local_sems.at[a]) for a in range(self.na)]
        first = []
        for a in range(self.na):
            first.append(copy(a, 0, me, sibling, own=True))
            first += [copy(a, 1 + j, me, (*chip, mc_), own=True) for j, chip in enumerate(chips)]
        return me, sibling, chips, mc_, copy, mine, first

    def start(self, x_refs, out_refs, send_sems, recv_sems, local_sems):
        *_, mine, first = self._plan(x_refs, out_refs, send_sems, recv_sems, local_sems)
        for cp in mine + first:
            cp.start()

    def finish(self, x_refs, out_refs, send_sems, recv_sems, local_sems):
        me, sibling, chips, mc_, copy, mine, first = self._plan(x_refs, out_refs, send_sems, recv_sems, local_sems)
        passed = []
        for j, chip in enumerate(chips):
            for a in range(self.na):
                copy(a, 1 + j, (*chip, mc_), me).wait_recv()
                fwd = copy(a, 4 + j, (*chip, mc_), sibling)
                fwd.start()
                passed.append(fwd)
        for a in range(self.na):
            copy(a, 0, sibling, me).wait_recv()
            for j, chip in enumerate(chips):
                copy(a, 4 + j, (*chip, 1 - mc_), me).wait_recv()
        for cp in first + passed:
            cp.wait_send()
        for cp in mine:
            cp.wait()


def _exchange(ride, name, in_hbm=True):
    n_in, n_out = len(ride.arrays), len(ride.out_shapes)

    def body(*refs):
        ins, outs, sems = refs[:n_in], refs[n_in:n_in + n_out], refs[n_in + n_out:]
        ride.start(ins, outs, *sems)
        ride.finish(ins, outs, *sems)

    space = pl.BlockSpec(memory_space=pl.ANY if in_hbm else pltpu.VMEM)
    return pl.pallas_call(body, out_shape=list(ride.out_shapes), in_specs=[space] * n_in, out_specs=[space] * n_out,
                          scratch_shapes=list(ride.sems), name=name)(*ride.arrays)


def _all_gather(xs, name, in_hbm):
    return _exchange(_GatherRide(xs), name, in_hbm)


N_CHIPS = 4


def _swap_sibling(gs, name):
    na = len(gs)

    def body(*refs):
        g_refs, out_refs, send_sems, recv_sems = refs[:na], refs[na:2 * na], refs[2 * na], refs[2 * na + 1]
        mx_, my_, mc_ = lax.axis_index("x"), lax.axis_index("y"), lax.axis_index("c")
        copies = [pltpu.make_async_remote_copy(
            src_ref=g_refs[a].at[k, 1 - mc_], dst_ref=out_refs[a].at[k], send_sem=send_sems.at[N_CHIPS * a + k],
            recv_sem=recv_sems.at[N_CHIPS * a + k], device_id=(mx_, my_, 1 - mc_), device_id_type=MESH)
            for a in range(na) for k in range(N_CHIPS)]
        for cp in copies:
            cp.start()
        for cp in copies:
            cp.wait_recv()
        for cp in copies:
            cp.wait_send()

    return pl.pallas_call(
        body, out_shape=[jax.ShapeDtypeStruct((N_CHIPS,) + tuple(g.shape[2:]), g.dtype) for g in gs],
        in_specs=[pl.BlockSpec(memory_space=pl.ANY)] * na, out_specs=[pl.BlockSpec(memory_space=pl.ANY)] * na,
        scratch_shapes=[pltpu.SemaphoreType.DMA((N_CHIPS * na,)), pltpu.SemaphoreType.DMA((N_CHIPS * na,))], name=name,
    )(*gs)


def _row_tile(r, n):
    if r * n * 4 <= (1 << 20):
        return r
    for t in (1024, 512, 256, 128, 64, 32, 16, 8):
        if r % t == 0 and t * n * 4 <= (1 << 20):
            return t
    return r


def _pair_add(place, g, got, name):
    _, _, r, n = g.shape
    tr = r if r * n * 4 <= (3 << 19) else _row_tile(r, n)

    def body(place_ref, g_ref, got_ref, o_ref, ob_ref):
        s = g_ref[...] + got_ref[...]
        o_ref[...] = s
        ob_ref[...] = s.astype(ob_ref.dtype)

    blk = pl.BlockSpec((None, tr, n), lambda k, i, pr: (k, i, 0))
    grid_spec = pltpu.PrefetchScalarGridSpec(
        num_scalar_prefetch=1, grid=(N_CHIPS, r // tr),
        in_specs=[pl.BlockSpec((None, None, tr, n), lambda k, i, pr: (k, pr[0], i, 0)), blk], out_specs=[blk, blk])
    return pl.pallas_call(body, grid_spec=grid_spec,
                          out_shape=[jax.ShapeDtypeStruct((N_CHIPS, r, n), F32), jax.ShapeDtypeStruct((N_CHIPS, r, n), BF16)],
                          name=name, compiler_params=_cparams(2))(place, g, got)


class _ChipSwapRide:
    def __init__(self, ps):
        self.arrays = list(ps)
        self.na = len(ps)
        self.out_shapes = [jax.ShapeDtypeStruct((N_CHIP_PEERS,) + tuple(p.shape[1:]), p.dtype) for p in ps]
        self.sems = [pltpu.SemaphoreType.DMA((N_CHIP_PEERS * self.na,)), pltpu.SemaphoreType.DMA((N_CHIP_PEERS * self.na,))]

    def _copies(self, p_refs, out_refs, send_sems, recv_sems):
        mx_, my_, mc_ = lax.axis_index("x"), lax.axis_index("y"), lax.axis_index("c")
        chips = [(1 - mx_, my_), (mx_, 1 - my_), (1 - mx_, 1 - my_)]
        return [pltpu.make_async_remote_copy(
            src_ref=p_refs[a].at[2 * cx + cy], dst_ref=out_refs[a].at[j], send_sem=send_sems.at[N_CHIP_PEERS * a + j],
            recv_sem=recv_sems.at[N_CHIP_PEERS * a + j], device_id=(cx, cy, mc_), device_id_type=MESH)
            for a in range(self.na) for j, (cx, cy) in enumerate(chips)]

    def start(self, p_refs, out_refs, send_sems, recv_sems):
        for cp in self._copies(p_refs, out_refs, send_sems, recv_sems):
            cp.start()

    def finish(self, p_refs, out_refs, send_sems, recv_sems):
        copies = self._copies(p_refs, out_refs, send_sems, recv_sems)
        for cp in copies:
            cp.wait_recv()
        for cp in copies:
            cp.wait_send()


def _sum_lead(x, name):
    k, r, n = x.shape
    tr = _row_tile(r, n * k)

    def body(x_ref, o_ref):
        acc = x_ref[0]
        for i in range(1, k):
            acc = acc + x_ref[i]
        o_ref[...] = acc

    return pl.pallas_call(body, grid=(r // tr,), in_specs=[pl.BlockSpec((k, tr, n), lambda i: (0, i, 0))],
                          out_specs=pl.BlockSpec((tr, n), lambda i: (i, 0)),
                          out_shape=jax.ShapeDtypeStruct((r, n), x.dtype), name=name, compiler_params=_cparams(1))(x)


def _adamw(place, w, parts, m, v, name):
    shape = w.shape
    cols = shape[-1]
    rows = math.prod(shape[:-1])
    to2 = lambda t: t.reshape(rows, cols)
    tr = _row_tile(rows, cols) if rows * cols * 4 > (1 << 20) else rows
    npart = len(parts)
    spec = pl.BlockSpec((tr, cols), lambda i, pr: (i, 0))
    native = len(shape) == 3 and shape[0] == 1
    own = pl.BlockSpec((None, tr, cols), lambda i, pr: (0, i, 0)) if native else spec
    as_own = (lambda t: t) if native else to2
    part_specs, part_args = [], []
    for piece in parts:
        if isinstance(piece, tuple):
            stack, k, row0 = piece
            part_args.append(stack.reshape(stack.shape[0], -1, cols))
            assert row0 % tr == 0
            if k == "chip":
                part_specs.append(pl.BlockSpec((None, tr, cols), functools.partial(lambda i, pr, b0: (pr[1], i + b0, 0),
                                                                                   b0=row0 // tr)))
            else:
                part_specs.append(pl.BlockSpec((None, tr, cols), functools.partial(
                    lambda i, pr, kk, b0: (kk, i + b0, 0), kk=k, b0=row0 // tr)))
        else:
            part_args.append(to2(piece))
            part_specs.append(spec)

    def body(place_ref, *refs):
        w_ref, m_ref, v_ref = refs[0], refs[1 + npart], refs[2 + npart]
        g_ref, d_ref, nm_ref, nv_ref = refs[3 + npart:]
        g = refs[1][...].astype(F32)
        for q in range(1, npart):
            g = g + refs[1 + q][...].astype(F32)
        mm = ADAM_B1 * m_ref[...] + (1.0 - ADAM_B1) * g
        vv = ADAM_B2 * v_ref[...] + (1.0 - ADAM_B2) * jnp.square(g)
        m_hat = mm / (1.0 - ADAM_B1 ** ADAM_STEP)
        v_hat = vv / (1.0 - ADAM_B2 ** ADAM_STEP)
        g_ref[...] = g
        d_ref[...] = -ADAM_LR * (m_hat / (jnp.sqrt(v_hat) + ADAM_EPS) + ADAM_WD * w_ref[...])
        nm_ref[...] = mm
        nv_ref[...] = vv

    sh = jax.ShapeDtypeStruct(shape if native else (rows, cols), F32)
    grid_spec = pltpu.PrefetchScalarGridSpec(num_scalar_prefetch=1, grid=(rows // tr,),
                                             in_specs=[own] + part_specs + [own, own], out_specs=[own] * 4)
    outs = pl.pallas_call(body, grid_spec=grid_spec, out_shape=[sh] * 4, name=name, compiler_params=_cparams(1),
                          )(place, as_own(w), *part_args, as_own(m), as_own(v))
    return tuple(o.reshape(shape) for o in outs)


def _packed_rows(n, width):
    return -(-n // (8 * width)) * 8


def _pack_rows(items, width):
    rows = []
    for t in items:
        flat = t.reshape(-1)
        n = flat.shape[0]
        k = _packed_rows(n, width)
        if k * width > n:
            flat = jnp.concatenate([flat, jnp.zeros((k * width - n,), t.dtype)])
        rows.append(flat.reshape(k, width))
    return jnp.concatenate(rows, axis=0)


def _unpack_rows(packed, shapes, lead=()):
    width = packed.shape[-1]
    out, r = [], 0
    for sh in shapes:
        n = math.prod(sh)
        k = _packed_rows(n, width)
        piece = packed[..., r:r + k, :].reshape(tuple(lead) + (k * width,))[..., :n]
        out.append(piece.reshape(tuple(lead) + tuple(sh)))
        r += k
    return out


def _cols_full(t):
    return jnp.transpose(t, (1, 0, 2)).reshape(t.shape[1], -1)


def _cols_shards(t):
    d = t.shape[0]
    return jnp.transpose(t.reshape(d, N_DEV, -1), (1, 0, 2))


BIG = ("ffn1_gate", "ffn1_up", "ffn1_down", "w_in", "w_out", "ffn2_gate", "ffn2_up", "ffn2_down")


def kernel(x, c, ctx, c_ctx, w_mod, b_mod, norm_ffn1, ffn1_gate, ffn1_up, ffn1_down, norm_mix, w_in, ssm_conv_w, ssm_conv_b, dt_bias_fwd, dt_bias_bwd, a_log_fwd, a_log_bwd, ssm_d, ssm_norm_w, cconv_w, cconv_b, cconv_ln_w, cconv_ln_b, w_out, norm_ffn2, ffn2_gate, ffn2_up, ffn2_down, final_norm, loss_target, m_c_ctx, m_w_mod, m_b_mod, m_norm_ffn1, m_ffn1_gate, m_ffn1_up, m_ffn1_down, m_norm_mix, m_w_in, m_ssm_conv_w, m_ssm_conv_b, m_dt_bias_fwd, m_dt_bias_bwd, m_a_log_fwd, m_a_log_bwd, m_ssm_d, m_ssm_norm_w, m_cconv_w, m_cconv_b, m_cconv_ln_w, m_cconv_ln_b, m_w_out, m_norm_ffn2, m_ffn2_gate, m_ffn2_up, m_ffn2_down, m_final_norm, v_c_ctx, v_w_mod, v_b_mod, v_norm_ffn1, v_ffn1_gate, v_ffn1_up, v_ffn1_down, v_norm_mix, v_w_in, v_ssm_conv_w, v_ssm_conv_b, v_dt_bias_fwd, v_dt_bias_bwd, v_a_log_fwd, v_a_log_bwd, v_ssm_d, v_ssm_norm_w, v_cconv_w, v_cconv_b, v_cconv_ln_w, v_cconv_ln_b, v_w_out, v_norm_ffn2, v_ffn2_gate, v_ffn2_up, v_ffn2_down, v_final_norm):
    args = dict(locals())
    names = ("c_ctx", "w_mod", "b_mod", "norm_ffn1", "ffn1_gate", "ffn1_up", "ffn1_down", "norm_mix", "w_in",
             "ssm_conv_w", "ssm_conv_b", "dt_bias_fwd", "dt_bias_bwd", "a_log_fwd", "a_log_bwd", "ssm_d", "ssm_norm_w",
             "cconv_w", "cconv_b", "cconv_ln_w", "cconv_ln_b", "w_out", "norm_ffn2", "ffn2_gate", "ffn2_up",
             "ffn2_down", "final_norm")
    wts = {n: args[n] for n in names}
    bl, seq, d = x.shape
    clen = ctx.shape[1]
    heads = dt_bias_fwd.shape[1]
    ft = _gate_tile(ffn1_gate.shape[2] * N_DEV)
    lay = _Lay(bl, seq, clen, d)
    mx = _Mix(d, heads)
    nb = bl * N_DEV
    me = 4 * lax.axis_index("x") + 2 * lax.axis_index("y") + lax.axis_index("c")
    mcols = w_mod.shape[2]
    n_ctx_mod = 5 * d

    place = jnp.stack([lax.axis_index("c"), 2 * lax.axis_index("x") + lax.axis_index("y")]).astype(jnp.int32)

    small_shapes = [(bl, d), ssm_conv_w.shape[1:], cconv_w.shape[1:]]
    (g1,) = _all_gather([_pack_rows([c, ssm_conv_w, cconv_w], d)], "gather_small", False)
    c_g, conv_g, cconv_g = _unpack_rows(g1, small_shapes, (N_DEV,))
    c_all = c_g.reshape(nb, d)
    conv_w_full = jnp.transpose(conv_g, (1, 0, 2)).reshape(conv_g.shape[1], -1)
    cconv_w_full = jnp.transpose(cconv_g, (1, 0, 2)).reshape(cconv_g.shape[1], -1)

    s_all = jnp.concatenate([_silu(c_all), _silu(c_ctx)[None, :], jnp.zeros((7, d), F32)], axis=0)
    mod_cols = _mm(s_all, w_mod[0], "nn", F32, "mod_cols")
    (g2,) = _all_gather([mod_cols], "gather_mod", False)
    mod_all = _cols_full(g2) + b_mod
    mod_mine = jnp.concatenate([lax.dynamic_slice_in_dim(mod_all, me * bl, bl, axis=0), mod_all[nb:nb + 1]], axis=0)
    modv = mod_mine.reshape(bl + 1, N_MOD, d)

    hh = 2 * heads
    shard16 = lambda n: wts[n][0].astype(BF16)

    nl = ffn1_gate.shape[2]
    spt = ft // nl
    assert ft % nl == 0 and N_DEV % spt == 0

    def ffn_weights(gate, up, down):
        both = jnp.stack([gate, up], axis=1).reshape(N_DEV // spt, spt, 2, d, nl)
        return jnp.transpose(both, (3, 0, 2, 1, 4)).reshape(d, -1), None if down is None else down.reshape(-1, d), ft

    def grads_by_dest(name, grad):
        if name == "w_in":
            grad = _cols_shards(jnp.concatenate([grad[:, mx.off_z:], grad[:, :mx.off_dt + hh],
                                                 grad[:, mx.off_glu:mx.off_z]], axis=1))
        elif name.endswith("_gu"):
            grad = jnp.transpose(grad.reshape(d, N_DEV // spt, 2, spt, nl), (1, 3, 2, 0, 4)).reshape(N_DEV, 2 * d, nl)
        else:
            grad = grad.reshape((N_DEV,) + tuple(wts[name].shape[1:]))
        return grad.reshape((N_CHIPS, 2) + tuple(grad.shape[1:]))

    def ffn_grads(tag, pair):
        return {tag + "_gu": pair[0], tag + "_down": pair[1]}

    def to_chip_sums(named):
        names_ = list(named)
        by_dest = [grads_by_dest(n, named[n]) for n in names_]
        got = _swap_sibling(by_dest, "rs_sibling_" + names_[0])
        return {n: _pair_add(place, t, s, "rs_pair_add_" + n) for n, t, s in zip(names_, by_dest, got)}

    class _Overlap:
        names_a = ("ffn1_down", "w_in", "ffn2_gate")
        names_b = ("w_out", "ffn2_up", "ffn2_down")
        late_a = _GatherRide([shard16(n) for n in names_a])
        late_b = _GatherRide([shard16(n) for n in names_b])
        sums, arrived = {}, {}

        def unpack_late(self, ffn1, outs_a, outs_b):
            full = {**dict(zip(self.names_a, outs_a)), **dict(zip(self.names_b, outs_b))}
            w_in_f = _cols_full(full["w_in"])
            w_in_p = jnp.concatenate([w_in_f[:, mx.ref_x:mx.ref_glu], jnp.zeros((d, DT_PAD - hh), BF16),
                                      w_in_f[:, mx.ref_glu:], w_in_f[:, :d]], axis=1)
            return {"w_in": w_in_p, "w_out": full["w_out"].reshape(-1, d),
                    "ffn1": (ffn1[0], full["ffn1_down"].reshape(-1, d), ffn1[2]),
                    "ffn2": ffn_weights(full["ffn2_gate"], full["ffn2_up"], full["ffn2_down"])}

        groups = (("ffn2_gu", "ffn2_down", "ffn1_down"), ("w_in", "w_out", "ffn1_gu"))

        def early_grads(self, g):
            self.sums = to_chip_sums({**ffn_grads("ffn2", g["ffn2"]), "w_in": g["w_in"], "w_out": g["w_out"]})

        def _ride(self, grp):
            return _ChipSwapRide([self.sums[n][1] for n in grp])

        def after_dwd(self, dwd):
            self.sums.update(to_chip_sums({"ffn1_down": dwd}))
            return self._ride(self.groups[0])

        def after_dwgu(self, dwgu):
            self.sums.update(to_chip_sums({"ffn1_gu": dwgu}))
            return self._ride(self.groups[1])

        def take(self, *rode):
            for grp, outs in zip(self.groups, rode):
                self.arrived.update(zip(grp, outs))

    exch = _Overlap()
    gate1, up1 = _all_gather([shard16("ffn1_gate"), shard16("ffn1_up")], "gather_weights", True)
    wgu1 = ffn_weights(gate1, up1, None)[0]
    lanes_pad = lambda a, b: jnp.concatenate([a, b, jnp.zeros((1, DT_LANES - hh), F32)], axis=1)
    a_vals = lanes_pad(-jnp.exp(a_log_fwd), -jnp.exp(a_log_bwd))
    w = {
        "norm_ffn1": norm_ffn1, "norm_mix": norm_mix, "norm_ffn2": norm_ffn2, "final_norm": final_norm[None, :],
        "ffn1": (wgu1, lambda outs_a: outs_a[0].reshape(-1, d), ft),
        "conv_w": jnp.concatenate([conv_w_full, jnp.zeros((3, mx.xw), F32)], axis=0), "conv_b": ssm_conv_b,
        "dt_bias": lanes_pad(dt_bias_fwd, dt_bias_bwd), "a_row": a_vals, "a_col": a_vals.T,
        "d_row": jnp.repeat(ssm_d, HEAD_DIM, axis=1), "ssm_norm_w": ssm_norm_w,
        "cconv_w": cconv_w_full, "cconv_b": cconv_b, "ln_w": cconv_ln_w, "ln_b": cconv_ln_b,
    }

    xa = jnp.concatenate([x.reshape(bl * seq, d), ctx.reshape(bl * clen, d)], axis=0)
    loss, grad_x, g, dmodv = _local_step(lay, mx, xa, loss_target.reshape(bl * seq, d), modv, w, exch)
    loss = lax.psum(loss[0, 0], ("x", "y", "c"))

    sums, arrived = exch.sums, exch.arrived

    def big_parts(n):
        key, row0 = (n[:4] + "_gu", d if n.endswith("_up") else 0) if n.endswith(("_gate", "_up")) else (n, 0)
        return [(sums[key][0], "chip", row0)] + [(arrived[key], k, row0) for k in range(N_CHIP_PEERS)]

    n9 = N_MOD * d
    dmod_rows = dmodv.reshape(bl + 1, n9)
    ctx_row = jnp.concatenate([dmod_rows[bl, :n_ctx_mod], jnp.zeros((n9 - n_ctx_mod,), F32)])
    summed = [ctx_row, g["norm_ffn1"], g["norm_mix"], g["norm_ffn2"], g["final_norm"], g["conv_b"], g["dt_bias"],
              g["a_row"], g["d_row"], g["ssm_norm_w"], g["cconv_b"], g["ln_w"], g["ln_b"], g["conv_w"][:5], g["cconv_w"]]
    sum_shapes = [t.shape for t in summed]
    (g4,) = _all_gather([_pack_rows([dmod_rows[:bl]] + summed, d)], "gather_small_grads", False)
    dmod_batch = g4[:, :bl * N_MOD].reshape(nb, n9)
    tot = _sum_lead(g4[:, _packed_rows(bl * n9, d):], "sum_small_grads")
    (dctx, g_n1, g_nm, g_n2, g_fn, g_cb, g_dtb, g_a, g_drow, g_snw, g_ccb, g_lnw, g_lnb, g_cw, g_ccw) = _unpack_rows(tot, sum_shapes)
    dmod_all = jnp.concatenate([dmod_batch, dctx[None, :], jnp.zeros((7, n9), F32)], axis=0)

    dmod_my_cols = lax.dynamic_slice_in_dim(dmod_all, me * mcols, mcols, axis=1)
    g_w_mod = _mm(s_all, dmod_my_cols, "tn", F32, "dw_mod")[None]
    g_b_mod = _sum_lead(dmod_all.reshape(nb + 8, N_MOD, d), "db_mod").reshape(1, n9)
    ds_part = _mm(dmod_my_cols[nb:nb + 8], w_mod[0], "nt", F32, "ds_ctx")
    (g5,) = _all_gather([jnp.concatenate([ds_part[0:1], jnp.zeros((7, d), F32)], axis=0)], "gather_ds_ctx", False)
    ds_ctx = _sum_lead(g5, "sum_ds_ctx")[0]
    sg = jax.nn.sigmoid(c_ctx)
    g_c_ctx = ds_ctx * (sg * (1.0 + c_ctx * (1.0 - sg)))

    a_f, a_b = a_vals[:, :heads], a_vals[:, heads:hh]
    grads = {
        "c_ctx": [g_c_ctx], "w_mod": [g_w_mod], "b_mod": [g_b_mod],
        "norm_ffn1": [g_n1], "norm_mix": [g_nm], "norm_ffn2": [g_n2], "final_norm": [g_fn.reshape(-1)],
        "ssm_conv_w": [lax.dynamic_slice_in_dim(g_cw, me * ssm_conv_w.shape[2], ssm_conv_w.shape[2], axis=1)[None]],
        "ssm_conv_b": [g_cb],
        "dt_bias_fwd": [g_dtb[:, :heads]], "dt_bias_bwd": [g_dtb[:, heads:hh]],
        "a_log_fwd": [g_a[:, :heads] * a_f], "a_log_bwd": [g_a[:, heads:hh] * a_b],
        "ssm_d": [jnp.sum(g_drow.reshape(1, heads, HEAD_DIM), axis=2)], "ssm_norm_w": [g_snw],
        "cconv_w": [lax.dynamic_slice_in_dim(g_ccw, me * cconv_w.shape[2], cconv_w.shape[2], axis=1)[None]],
        "cconv_b": [g_ccb], "cconv_ln_w": [g_lnw], "cconv_ln_b": [g_lnb],
    }
    for n in BIG:
        grads[n] = big_parts(n)

    out_g, out_d, out_m, out_v = [], [], [], []
    for n in names:
        gr, de, nm, nv = _adamw(place, wts[n], grads[n], args["m_" + n], args["v_" + n], "adamw_" + n)
        out_g.append(gr)
        out_d.append(de)
        out_m.append(nm)
        out_v.append(nv)
    return (loss, grad_x.reshape(bl, seq, d), *out_g, *out_d, *out_m, *out_v)
```

```python
import functools
import math

import jax
import jax.numpy as jnp
from jax import lax
from jax.experimental import pallas as pl
from jax.experimental.pallas import tpu as pltpu

F32 = jnp.float32
BF16 = jnp.bfloat16
MESH = pl.DeviceIdType.MESH

N_DEV = 8
N_CHIP_PEERS = 3
HEAD_DIM = 64
N_STATE = 128
SSD_GROUPS = 2
CHUNK = 128
GRID_W = 64
N_MOD = 9
EPS = 1e-6
DT_PAD = 512
DT_LANES = 128
HALO = 8
ROW_TILE = 512
FINE_ROW_TILE = 256
VMEM_LIMIT = 48 * 1024 * 1024
NEG_BIG = -1e30

ADAM_LR = 0.001
ADAM_B1 = 0.9
ADAM_B2 = 0.999
ADAM_EPS = 1e-08
ADAM_WD = 0.01
ADAM_STEP = 10


def _pick(n, prefs):
    for p in prefs:
        if n % p == 0:
            return p
    return n


MM_TILE_CAP = 2816
MM_TILE_ELEMS = 3 << 20
MM_OUT_TILE_ELEMS = 3 << 19
MM_FULL_ROWS = 1024


def _big_tile(n, cap):
    if n <= cap:
        return n
    best = 0
    for t in range(128, cap + 1, 128):
        if n % t == 0:
            best = t
    return best or n


def _cparams(ndim):
    return pltpu.CompilerParams(dimension_semantics=("arbitrary",) * ndim, vmem_limit_bytes=VMEM_LIMIT)


def _silu(v):
    return v * jax.nn.sigmoid(v)


def _mm(a, b, mode, out_dtype, name, tn=None, tm=None, extras=(), epilogue=None, outs=None, ride=None):
    if mode == "tn":
        (K, M), (K2, N) = a.shape, b.shape
    elif mode == "nt":
        (M, K), (N, K2) = a.shape, b.shape
    else:
        (M, K), (K2, N) = a.shape, b.shape
    assert K == K2, (name, a.shape, b.shape)
    tm = tm or (M if M <= MM_FULL_ROWS else None)
    if tn is None:
        tn = _big_tile(N, min(MM_TILE_CAP, max(128, MM_OUT_TILE_ELEMS // (tm or 512))))
    if tm is None:
        tm = _big_tile(M, max(128, MM_OUT_TILE_ELEMS // tn))
    tk = _big_tile(K, min(MM_TILE_CAP, MM_TILE_ELEMS // max(tn, tm)))
    nk = K // tk
    ni, nj = M // tm, N // tn
    swap = nk == 1 and (K * N + M * K * nj) < (M * K + K * N * ni)
    ij = (lambda g0, g1: (g1, g0)) if swap else (lambda g0, g1: (g0, g1))
    if mode == "tn":
        a_spec = pl.BlockSpec((tk, tm), lambda g0, g1, k: (k, ij(g0, g1)[0]))
        dn = (((0,), (0,)), ((), ()))
    else:
        a_spec = pl.BlockSpec((tm, tk), lambda g0, g1, k: (ij(g0, g1)[0], k))
        dn = (((1,), (1,)), ((), ())) if mode == "nt" else (((1,), (0,)), ((), ()))
    if mode == "nt":
        b_spec = pl.BlockSpec((tn, tk), lambda g0, g1, k: (ij(g0, g1)[1], k))
    else:
        b_spec = pl.BlockSpec((tk, tn), lambda g0, g1, k: (k, ij(g0, g1)[1]))
    if outs is None:
        outs = [(tn, out_dtype)]
    nx = len(extras)

    def tile(w):
        return pl.BlockSpec((tm, w), lambda g0, g1, k: ij(g0, g1))

    def extra_spec(item):
        if len(item) == 3:
            return pl.BlockSpec((None,) + tuple(item[0].shape[1:]), lambda g0, g1, k: (item[1](ij(g0, g1)[0]), 0, 0))
        return tile(item[1])

    def finish(acc, refs):
        vals = (acc,) if epilogue is None else epilogue(acc, *[r[...] for r in refs[:nx]])
        for o_ref, v in zip(refs[nx:], vals):
            o_ref[...] = v.astype(o_ref.dtype)

    grid = (nj, ni, nk) if swap else (ni, nj, nk)
    nout = len(outs)
    r_in = len(ride.arrays) if ride else 0
    r_out = len(ride.out_shapes) if ride else 0

    def compute(a_ref, b_ref, refs):
        part = lax.dot_general(a_ref[...].astype(BF16), b_ref[...].astype(BF16), dn, preferred_element_type=F32)
        if nk == 1:
            finish(part, refs)
            return
        acc_ref, k = refs[-1], pl.program_id(2)
        _acc(acc_ref, k == 0, part)

        @pl.when(k == nk - 1)
        def _():
            finish(acc_ref[...], refs[:-1])

    def body(a_ref, b_ref, *refs):
        if ride is None:
            compute(a_ref, b_ref, refs)
            return
        x_refs, rin = refs[:nx], refs[nx:nx + r_in]
        o_refs, rout = refs[nx + r_in:nx + r_in + nout], refs[nx + r_in + nout:nx + r_in + nout + r_out]
        tail = refs[nx + r_in + nout + r_out:]
        nacc = 1 if nk > 1 else 0
        sems = tail[nacc:]
        ids = [pl.program_id(q) for q in range(3)]
        first = functools.reduce(jnp.logical_and, [i == 0 for i in ids])
        last = functools.reduce(jnp.logical_and, [i == n - 1 for i, n in zip(ids, grid)])
        pl.when(first)(lambda: ride.start(rin, rout, *sems))
        compute(a_ref, b_ref, tuple(x_refs) + tuple(o_refs) + tuple(tail[:nacc]))
        pl.when(last)(lambda: ride.finish(rin, rout, *sems))

    hbm = pl.BlockSpec(memory_space=pl.ANY)
    res = pl.pallas_call(
        body, grid=grid, in_specs=[a_spec, b_spec] + [extra_spec(x) for x in extras] + [hbm] * r_in,
        out_specs=[tile(w) for w, _ in outs] + [hbm] * r_out,
        out_shape=[jax.ShapeDtypeStruct((M, nj * w), dt) for w, dt in outs] + (list(ride.out_shapes) if ride else []),
        scratch_shapes=([pltpu.VMEM((tm, tn), F32)] if nk > 1 else []) + (list(ride.sems) if ride else []),
        name=name, compiler_params=_cparams(3),
    )(a, b, *[x[0] for x in extras], *(ride.arrays if ride else []))
    main = res[0] if epilogue is None else res[:nout]
    return (main, res[nout:]) if ride else main


class _Lay:
    def __init__(self, bl, seq, clen, d, tt=None):
        self.bl, self.seq, self.clen, self.d = bl, seq, clen, d
        self.tt = min(ROW_TILE, math.gcd(seq, bl * clen)) if tt is None else tt
        assert seq % self.tt == 0 and (bl * clen) % self.tt == 0 and self.tt % 8 == 0
        self.spb = seq // self.tt
        self.spc = clen // self.tt
        self.nsx = bl * self.spb
        self.nsc = bl * clen // self.tt
        self.ns = self.nsx + self.nsc
        self.tx = bl * seq
        self.ta = self.tx + bl * clen

    def fine(self):
        return _Lay(self.bl, self.seq, self.clen, self.d, min(FINE_ROW_TILE, self.clen))

    def mrow(self, s):
        return jnp.where(s < self.nsx, s // self.spb, self.bl)

    def first_of_row(self, s):
        return jnp.logical_or(jnp.logical_and(s < self.nsx, s % self.spb == 0), s == self.nsx)

    def seq_first(self, s):
        return jnp.where(s < self.nsx, s % self.spb == 0, (s - self.nsx) % self.spc == 0)

    def seq_last(self, s):
        return jnp.where(s < self.nsx, s % self.spb == self.spb - 1, (s - self.nsx) % self.spc == self.spc - 1)


def _tok(lay, c, cb=0, clamp=None):
    if clamp is None:
        return pl.BlockSpec((lay.tt, c), lambda j, s: (s, cb + j))
    return pl.BlockSpec((lay.tt, c), lambda j, s: (jnp.minimum(s, clamp), cb + j))


def _halo_prev(lay, c, cb=0):
    u = lay.tt // HALO
    return pl.BlockSpec((HALO, c), lambda j, s: (jnp.maximum(s * u - 1, 0), cb + j))


def _halo_next(lay, c, cb=0):
    u = lay.tt // HALO
    last = lay.ta // HALO - 1
    return pl.BlockSpec((HALO, c), lambda j, s: (jnp.minimum((s + 1) * u, last), cb + j))


def _row(lay, k, c):
    return pl.BlockSpec((None, k, c), lambda j, s: (lay.mrow(s), 0, 0))


def _glob(k, c, cb=None):
    if cb is None:
        return pl.BlockSpec((k, c), lambda j, s: (0, 0))
    return pl.BlockSpec((k, c), lambda j, s: (0, cb + j))


def _tok_call(name, body, ncb, nseg, in_specs, out_specs, out_shape, inputs, scratch=(), aliases=None):
    return pl.pallas_call(body, grid=(ncb, nseg), in_specs=in_specs, out_specs=out_specs, out_shape=out_shape,
                          scratch_shapes=list(scratch), name=name, compiler_params=_cparams(2),
                          input_output_aliases=aliases or {})(*inputs)


def _acc(ref, first, val):
    @pl.when(first)
    def _():
        ref[...] = val

    @pl.when(jnp.logical_not(first))
    def _():
        ref[...] += val


def _norm_mod_f(x, w, sh, sc):
    y = x * lax.rsqrt(jnp.mean(x * x, axis=-1, keepdims=True) + EPS) * w
    return y * (1.0 + sc) + sh


def _norm_mod_fwd(lay, nseg, x, w, modv, ksh, name):
    d = lay.d

    def body(x_ref, w_ref, m_ref, h_ref):
        h = _norm_mod_f(x_ref[...], w_ref[...], m_ref[ksh:ksh + 1, :], m_ref[ksh + 1:ksh + 2, :])
        h_ref[...] = h.astype(h_ref.dtype)

    return _tok_call(name, body, 1, nseg, [_tok(lay, d), _glob(1, d), _row(lay, N_MOD, d)], _tok(lay, d),
                     jax.ShapeDtypeStruct((nseg * lay.tt, d), BF16), (x, w, modv))


def _norm_mod_bwd(lay, nseg, nres, x, w, modv, ksh, dh, dres, name, nout=None):
    d = lay.d
    nrow = lay.bl + (1 if nseg > lay.nsx else 0)
    nout = nseg if nout is None else nout

    def body(x_ref, w_ref, m_ref, dh_ref, dres_ref, dx_ref, dw_ref, dm_ref):
        s = pl.program_id(1)
        _, vjp = jax.vjp(_norm_mod_f, x_ref[...], w_ref[...], m_ref[ksh:ksh + 1, :], m_ref[ksh + 1:ksh + 2, :])
        dx, dw, dsh, dsc = vjp(dh_ref[...])

        @pl.when(s < nout)
        def _():
            dx_ref[...] = dx + jnp.where(s < nres, dres_ref[...], 0.0)

        _acc(dw_ref, s == 0, dw)
        _acc(dm_ref, lay.first_of_row(s), jnp.concatenate([dsh, dsc], axis=0))

    return _tok_call(
        name, body, 1, nseg,
        [_tok(lay, d), _glob(1, d), _row(lay, N_MOD, d), _tok(lay, d), _tok(lay, d, clamp=nres - 1)],
        [_tok(lay, d, clamp=nout - 1), _glob(1, d), _row(lay, 2, d)],
        [jax.ShapeDtypeStruct((nout * lay.tt, d), F32), jax.ShapeDtypeStruct((1, d), F32),
         jax.ShapeDtypeStruct((nrow, 2, d), F32)],
        (x, w, modv, dh, dres))


def _mm_resid(lay, a, b, x, modv, kg, coef, name, then_norm=None, ride=None):
    d = lay.d
    tm = min(ROW_TILE, math.gcd(lay.seq, lay.bl * lay.clen))
    assert a.shape[0] % tm == 0 and b.shape[1] == d

    def row_of(i):
        return jnp.where(i * tm < lay.tx, (i * tm) // lay.seq, lay.bl)

    def add(acc, x_tile, m_blk, *nw):
        y = x_tile + (coef * m_blk[kg:kg + 1, :]) * acc
        if then_norm is None:
            return acc, y
        k = then_norm[1]
        return acc, y, _norm_mod_f(y, nw[0], m_blk[k:k + 1, :], m_blk[k + 1:k + 2, :])

    extras = [(x, d), (modv, row_of, "rows")]
    outs = [(d, F32), (d, F32)]
    if then_norm is not None:
        extras.append((then_norm[0].reshape(1, 1, d), lambda i: 0, "rows"))
        outs.append((d, BF16))
    return _mm(a, b, "nn", None, name, tm=tm, tn=d, extras=extras, epilogue=add, outs=outs, ride=ride)


def _resid_bwd(lay, nseg, dy, o, modv, kg, coef, name):
    d = lay.d
    nrow = lay.bl + (1 if nseg > lay.nsx else 0)

    def body(dy_ref, o_ref, m_ref, do_ref, dg_ref):
        s = pl.program_id(1)
        dy = dy_ref[...]
        do_ref[...] = (dy * (coef * m_ref[kg:kg + 1, :])).astype(do_ref.dtype)
        _acc(dg_ref, lay.first_of_row(s), jnp.sum(dy * o_ref[...], axis=0, keepdims=True) * coef)

    return _tok_call(name, body, 1, nseg, [_tok(lay, d), _tok(lay, d), _row(lay, N_MOD, d)],
                     [_tok(lay, d), _row(lay, 1, d)],
                     [jax.ShapeDtypeStruct((nseg * lay.tt, d), BF16), jax.ShapeDtypeStruct((nrow, 1, d), F32)],
                     (dy, o, modv))


def _final_loss(lay, x, wf, target, name):
    d = lay.d

    def body(x_ref, w_ref, t_ref, loss_ref, dx_ref, dw_ref):
        s = pl.program_id(1)

        def f(xv, wv):
            return xv * lax.rsqrt(jnp.mean(xv * xv, axis=-1, keepdims=True) + EPS) * wv

        y, vjp = jax.vjp(f, x_ref[...], w_ref[...])
        err = y - t_ref[...]
        part = 0.5 * jnp.sum(jnp.sum(err * err, axis=-1, keepdims=True), axis=0, keepdims=True) / d
        dx, dw = vjp(err / d)
        dx_ref[...] = dx
        _acc(loss_ref, s == 0, part)
        _acc(dw_ref, s == 0, dw)

    return _tok_call(name, body, 1, lay.nsx, [_tok(lay, d), _glob(1, d), _tok(lay, d)],
                     [_glob(1, 1), _tok(lay, d), _glob(1, d)],
                     [jax.ShapeDtypeStruct((1, 1), F32), jax.ShapeDtypeStruct((lay.tx, d), F32),
                      jax.ShapeDtypeStruct((1, d), F32)], (x, wf, target))


class _Mix:
    def __init__(self, d, heads):
        self.d_ssm = d
        self.d_conv = d
        self.heads = heads
        assert heads * HEAD_DIM == d and heads % (2 * SSD_GROUPS) == 0 and 2 * heads <= DT_LANES
        self.gn = SSD_GROUPS * N_STATE
        self.xw = d + 2 * self.gn
        self.off_x = 0
        self.off_dt = self.xw
        self.off_glu = self.xw + DT_PAD
        self.off_z = self.off_glu + 2 * d
        self.pw = self.off_z + d
        assert self.off_z % d == 0
        self.ref_x = d
        self.ref_dt = d + self.xw
        self.ref_glu = self.ref_dt + 2 * heads
        self.cc = self.xw if self.off_x % self.xw == 0 else _pick(self.xw, (512, 256, 128))


def _conv5_fwd(lay, mx, proj, cw, cb, name):
    c, tt = mx.cc, lay.tt
    cb0 = mx.off_x // c
    assert mx.off_x % c == 0

    def body(prev_ref, cur_ref, next_ref, w_ref, b_ref, pre_ref, act_ref, ext_ref):
        s = pl.program_id(1)
        ext_ref[0:HALO, :] = jnp.where(lay.seq_first(s), 0.0, prev_ref[...])
        ext_ref[HALO:HALO + tt, :] = cur_ref[...]
        ext_ref[HALO + tt:, :] = jnp.where(lay.seq_last(s), 0.0, next_ref[...])
        acc = jnp.zeros((tt, c), F32) + b_ref[...]
        for k in range(5):
            acc = acc + w_ref[k:k + 1, :] * ext_ref[pl.ds(HALO + k - 2, tt), :]
        pre_ref[...] = acc
        act_ref[...] = _silu(acc)

    sh = jax.ShapeDtypeStruct((lay.ta, mx.xw), F32)
    return _tok_call(name, body, mx.xw // c, lay.ns,
                     [_halo_prev(lay, c, cb0), _tok(lay, c, cb0), _halo_next(lay, c, cb0), _glob(8, c, 0), _glob(1, c, 0)],
                     [_tok(lay, c), _tok(lay, c)], [sh, sh], (proj, proj, proj, cw, cb),
                     scratch=[pltpu.VMEM((tt + 2 * HALO, c), F32)])


def _conv5_bwd(lay, mx, proj, pre, dact_f, dact_b, cw, dproj, name):
    c, tt = mx.cc, lay.tt
    cb0 = mx.off_x // c

    def dsilu(p):
        sg = jax.nn.sigmoid(p)
        return sg * (1.0 + p * (1.0 - sg))

    def body(xp_ref, xc_ref, xn_ref, pp_ref, pc_ref, pn_ref, fp_ref, fc_ref, fn_ref, bp_ref, bc_ref, bn_ref, w_ref,
             buf_ref, dx_ref, dw_ref, db_ref, extx_ref, extd_ref):
        s = pl.program_id(1)
        first, last = lay.seq_first(s), lay.seq_last(s)
        dcur = (fc_ref[...] + bc_ref[...]) * dsilu(pc_ref[...])
        extd_ref[0:HALO, :] = jnp.where(first, 0.0, (fp_ref[...] + bp_ref[...]) * dsilu(pp_ref[...]))
        extd_ref[HALO:HALO + tt, :] = dcur
        extd_ref[HALO + tt:, :] = jnp.where(last, 0.0, (fn_ref[...] + bn_ref[...]) * dsilu(pn_ref[...]))
        extx_ref[0:HALO, :] = jnp.where(first, 0.0, xp_ref[...])
        extx_ref[HALO:HALO + tt, :] = xc_ref[...]
        extx_ref[HALO + tt:, :] = jnp.where(last, 0.0, xn_ref[...])
        dx = jnp.zeros((tt, c), F32)
        rows = []
        for k in range(5):
            dx = dx + w_ref[k:k + 1, :] * extd_ref[pl.ds(HALO - (k - 2), tt), :]
            rows.append(jnp.sum(dcur * extx_ref[pl.ds(HALO + k - 2, tt), :], axis=0, keepdims=True))
        dx_ref[...] = dx.astype(dx_ref.dtype)
        rows.append(jnp.zeros((3, c), F32))
        _acc(dw_ref, s == 0, jnp.concatenate(rows, axis=0))
        _acc(db_ref, s == 0, jnp.sum(dcur, axis=0, keepdims=True))

    three = lambda cbx: [_halo_prev(lay, c, cbx), _tok(lay, c, cbx), _halo_next(lay, c, cbx)]
    ext = pltpu.VMEM((tt + 2 * HALO, c), F32)
    return _tok_call(name, body, mx.xw // c, lay.ns,
                     three(cb0) + three(0) + three(0) + three(0) + [_glob(8, c, 0), pl.BlockSpec(memory_space=pl.ANY)],
                     [_tok(lay, c, cb0), _glob(8, c, 0), _glob(1, c, 0)],
                     [jax.ShapeDtypeStruct(dproj.shape, dproj.dtype), jax.ShapeDtypeStruct((8, mx.xw), F32),
                      jax.ShapeDtypeStruct((1, mx.xw), F32)],
                     (proj, proj, proj, pre, pre, pre, dact_f, dact_f, dact_f, dact_b, dact_b, dact_b, cw, dproj),
                     scratch=[ext, ext], aliases={13: 0})


def _softplus(v):
    return jnp.maximum(v, 0.0) + jnp.log1p(jnp.exp(-jnp.abs(v)))


def _dt_fwd(lay, mx, proj, bias, name):
    cb = mx.off_dt // DT_LANES

    def body(p_ref, b_ref, dt_ref):
        dt_ref[...] = _softplus(p_ref[...] + b_ref[...])

    return _tok_call(name, body, 1, lay.ns, [_tok(lay, DT_LANES, cb), _glob(1, DT_LANES)], _tok(lay, DT_LANES),
                     jax.ShapeDtypeStruct((lay.ta, DT_LANES), F32), (proj, bias))


def _dt_bwd(lay, mx, proj, bias, parts, dproj, name):
    cb = mx.off_dt // DT_LANES
    wb = DT_PAD if mx.off_dt % DT_PAD == 0 else DT_LANES
    ncb = DT_PAD // wb

    def body(p_ref, b_ref, a_ref, b2_ref, c_ref, d_ref, buf_ref, dp_ref, db_ref):
        j, s = pl.program_id(0), pl.program_id(1)

        @pl.when(j == 0)
        def _():
            ddt = (a_ref[...] + b2_ref[...]) + (c_ref[...] + d_ref[...])
            draw = ddt * jax.nn.sigmoid(p_ref[...] + b_ref[...])
            dp_ref[:, 0:DT_LANES] = draw.astype(dp_ref.dtype)
            if wb > DT_LANES:
                dp_ref[:, DT_LANES:] = jnp.zeros((lay.tt, wb - DT_LANES), dp_ref.dtype)
            _acc(db_ref, s == 0, jnp.sum(draw, axis=0, keepdims=True))

        @pl.when(j > 0)
        def _():
            dp_ref[...] = jnp.zeros_like(dp_ref)

    t = pl.BlockSpec((lay.tt, DT_LANES), lambda j, s: (s, 0))
    return _tok_call(name, body, ncb, lay.ns,
                     [pl.BlockSpec((lay.tt, DT_LANES), lambda j, s: (s, cb)), _glob(1, DT_LANES), t, t, t, t,
                      pl.BlockSpec(memory_space=pl.ANY)],
                     [_tok(lay, wb, mx.off_dt // wb), _glob(1, DT_LANES)],
                     [jax.ShapeDtypeStruct(dproj.shape, dproj.dtype), jax.ShapeDtypeStruct((1, DT_LANES), F32)],
                     (proj, bias) + tuple(parts) + (dproj,), aliases={6: 0})


def _scan_mask(rev):
    r = lax.broadcasted_iota(jnp.int32, (CHUNK, CHUNK), 0)
    c = lax.broadcasted_iota(jnp.int32, (CHUNK, CHUNK), 1)
    return (r <= c) if rev else (r >= c)


def _split_bf16(x):
    hi = x.astype(BF16)
    return hi, (x - hi.astype(F32)).astype(BF16)


@functools.partial(jax.custom_vjp, nondiff_argnums=(0,))
def _cum_cols(rev, x):
    m = _scan_mask(rev).astype(BF16)
    hi, lo = _split_bf16(x)
    return jnp.dot(m, hi, preferred_element_type=F32) + jnp.dot(m, lo, preferred_element_type=F32)


_cum_cols.defvjp(lambda rev, x: (_cum_cols(rev, x), None), lambda rev, _, g: (_cum_cols(not rev, g),))


@functools.partial(jax.custom_vjp, nondiff_argnums=(0,))
def _cum_rows(rev, x):
    m = _scan_mask(not rev).astype(BF16)
    hi, lo = _split_bf16(x)
    return jnp.dot(hi, m, preferred_element_type=F32) + jnp.dot(lo, m, preferred_element_type=F32)


_cum_rows.defvjp(lambda rev, x: (_cum_rows(rev, x), None), lambda rev, _, g: (_cum_rows(not rev, g),))


def _ssd_chunk(xh_pairs, bcs, ccs, dtc, dtr, a_row, a_col, st_pairs, *, rev, heads, col0):
    cs_c, cs_r, tot, scores = _ssd_shared(bcs, ccs, dtc, dtr, a_row, a_col, rev=rev)
    ppg = heads // (2 * SSD_GROUPS)
    ys, sts = [], []
    for g in range(SSD_GROUPS):
        y, st = _ssd_group(xh_pairs[g * ppg:(g + 1) * ppg], bcs[g], ccs[g], st_pairs[g], cs_c, cs_r, tot, dtc,
                           scores[g], rev=rev, col=col0 + 2 * ppg * g)
        ys.append(y)
        sts.append(st)
    return ys, sts


_NT = (((1,), (1,)), ((), ()))
_TN = (((0,), (0,)), ((), ()))


def _ssd_shared(bcs, ccs, dtc, dtr, a_row, a_col, *, rev):
    da_c = dtc * a_row
    cs_c = _cum_cols(rev, da_c)
    cs_r = _cum_rows(rev, dtr * a_col)
    tot = jnp.sum(da_c, axis=0, keepdims=True)
    scores = [lax.dot_general(ccs[g].astype(BF16), bcs[g].astype(BF16), _NT, preferred_element_type=F32)
              for g in range(SSD_GROUPS)]
    return cs_c, cs_r, tot, scores


def _ssd_group(xh_pairs, bc, cc, st, cs_c, cs_r, tot, dtc, score, *, rev, col):
    n = CHUNK
    mask = _scan_mask(rev)
    lane = lax.broadcasted_iota(jnp.int32, (n, DT_LANES), 1)
    sub = lax.broadcasted_iota(jnp.int32, (DT_LANES, n), 0)
    lane1 = lax.broadcasted_iota(jnp.int32, (1, DT_LANES), 1)
    left = lax.broadcasted_iota(jnp.int32, (n, 2 * HEAD_DIM), 1) < HEAD_DIM
    top = lax.broadcasted_iota(jnp.int32, (2 * HEAD_DIM, 1), 0) < HEAD_DIM
    xs_all, wst_all, ecs_all, edec_all, y_diag = [], [], [], [], []
    for p, xh in enumerate(xh_pairs):
        per = []
        for c in (col + 2 * p, col + 2 * p + 1):
            csv = jnp.sum(jnp.where(lane == c, cs_c, 0.0), axis=1, keepdims=True)
            csr = jnp.sum(jnp.where(sub == c, cs_r, 0.0), axis=0, keepdims=True)
            dtv = jnp.sum(jnp.where(lane == c, dtc, 0.0), axis=1, keepdims=True)
            tv = jnp.sum(jnp.where(lane1 == c, tot, 0.0), axis=1, keepdims=True)
            m = score * jnp.exp(jnp.where(mask, csv - csr, NEG_BIG))
            per.append((csv, dtv, tv, m))
        (cs1, dt1, t1, m1), (cs2, dt2, t2, m2) = per
        xs = xh * jnp.where(left, dt1, dt2)
        both = jnp.dot(jnp.concatenate([m1, m2], axis=0).astype(BF16), xs.astype(BF16), preferred_element_type=F32)
        y_diag.append(jnp.where(left, both[:n], both[n:]))
        xs_all.append(xs)
        ecs_all.append(jnp.where(left, jnp.exp(cs1), jnp.exp(cs2)))
        wst_all.append(jnp.where(left, jnp.exp(t1 - cs1), jnp.exp(t2 - cs2)))
        edec_all.append(jnp.where(top, jnp.exp(t1), jnp.exp(t2)))
    cat = lambda parts, axis: parts[0] if len(parts) == 1 else jnp.concatenate(parts, axis=axis)
    xs, wst, ecs = cat(xs_all, 1), cat(wst_all, 1), cat(ecs_all, 1)
    y_off = lax.dot_general(cc.astype(BF16), st.astype(BF16), _NT, preferred_element_type=F32) * ecs
    cst = lax.dot_general((xs * wst).astype(BF16), bc.astype(BF16), _TN, preferred_element_type=F32)
    return cat(y_diag, 1) + y_off, st * cat(edec_all, 0) + cst


class _Scan:
    def __init__(self, lay, rev):
        self.ncx, self.ncc, self.bl, self.rev = lay.seq // CHUNK, lay.clen // CHUNK, lay.bl, rev
        self.nct = self.ncx + self.ncc

    def chunk(self, b, pos):
        kc = (self.ncc - 1 - pos) if self.rev else pos
        kx = (self.ncx - 1 - (pos - self.ncc)) if self.rev else (pos - self.ncc)
        return jnp.where(pos < self.ncc, self.bl * self.ncx + b * self.ncc + kc, b * self.ncx + kx)


def _ssd_io(mx, x_ref, st_src):
    np_ = mx.heads // 2
    d = mx.d_ssm
    xh = [x_ref[:, 128 * p:128 * (p + 1)] for p in range(np_)]
    bcs = [x_ref[:, d + N_STATE * g:d + N_STATE * (g + 1)] for g in range(SSD_GROUPS)]
    ccs = [x_ref[:, d + mx.gn + N_STATE * g:d + mx.gn + N_STATE * (g + 1)] for g in range(SSD_GROUPS)]
    gw = d // SSD_GROUPS
    sts = [st_src[gw * g:gw * (g + 1), :] for g in range(SSD_GROUPS)]
    return xh, bcs, ccs, sts


def _ssd_fwd(lay, mx, xbc, dt, dtt, a_row, a_col, rev, name):
    sc = _Scan(lay, rev)
    col0 = mx.heads if rev else 0
    hp = mx.heads * HEAD_DIM

    def body(x_ref, dt_ref, dtt_ref, ar_ref, ac_ref, y_ref, hp_ref, st_ref):
        @pl.when(pl.program_id(1) == 0)
        def _():
            st_ref[...] = jnp.zeros_like(st_ref)

        hp_ref[...] = st_ref[...]
        xh, bcs, ccs, sts = _ssd_io(mx, x_ref, st_ref)
        ys, new = _ssd_chunk(xh, bcs, ccs, dt_ref[...], dtt_ref[...], ar_ref[...], ac_ref[...], sts,
                             rev=rev, heads=mx.heads, col0=col0)
        gw = mx.d_ssm // SSD_GROUPS
        for g in range(SSD_GROUPS):
            y_ref[:, gw * g:gw * (g + 1)] = ys[g]
            st_ref[gw * g:gw * (g + 1), :] = new[g]

    ch = sc.chunk
    return pl.pallas_call(
        body, grid=(lay.bl, sc.nct),
        in_specs=[pl.BlockSpec((CHUNK, mx.xw), lambda b, i: (ch(b, i), 0)),
                  pl.BlockSpec((CHUNK, DT_LANES), lambda b, i: (ch(b, i), 0)),
                  pl.BlockSpec((DT_LANES, CHUNK), lambda b, i: (0, ch(b, i))),
                  pl.BlockSpec((1, DT_LANES), lambda b, i: (0, 0)),
                  pl.BlockSpec((DT_LANES, 1), lambda b, i: (0, 0))],
        out_specs=[pl.BlockSpec((CHUNK, mx.d_ssm), lambda b, i: (ch(b, i), 0)),
                   pl.BlockSpec((hp, N_STATE), lambda b, i: (b * sc.nct + i, 0))],
        out_shape=[jax.ShapeDtypeStruct((lay.ta, mx.d_ssm), F32),
                   jax.ShapeDtypeStruct((lay.bl * sc.nct * hp, N_STATE), F32)],
        scratch_shapes=[pltpu.VMEM((hp, N_STATE), F32)], name=name, compiler_params=_cparams(2),
    )(xbc, dt, dtt, a_row, a_col)


def _ssd_bwd(lay, mx, xbc, dt, dtt, a_row, a_col, hprev, dy, dskip, rev, name):
    sc = _Scan(lay, rev)
    col0 = mx.heads if rev else 0
    hp = mx.heads * HEAD_DIM
    np_ = mx.heads // 2
    d = mx.d_ssm
    with_skip = dskip is not None

    def body(*refs):
        if with_skip:
            x_ref, dt_ref, dtt_ref, ar_ref, ac_ref, hp_ref, dy_ref, sk_ref = refs[:8]
            rest = refs[8:]
        else:
            x_ref, dt_ref, dtt_ref, ar_ref, ac_ref, hp_ref, dy_ref = refs[:7]
            rest = refs[7:]
        dx_ref, ddc_ref, ddr_ref, dar_ref, dac_ref, ds_ref = rest
        b, i = pl.program_id(0), pl.program_id(1)

        @pl.when(i == 0)
        def _():
            ds_ref[...] = jnp.zeros_like(ds_ref)

        xh, bcs, ccs, sts = _ssd_io(mx, x_ref, hp_ref)
        dtc = dt_ref[...]
        shared, vjp_shared = jax.vjp(functools.partial(_ssd_shared, rev=rev), bcs, ccs, dtc, dtt_ref[...],
                                     ar_ref[...], ac_ref[...])
        cs_c, cs_r, tot, scores = shared
        plus = lambda acc, v: v if acc is None else acc + v
        d_cs_c = d_cs_r = d_tot = ddc = None
        d_scores, dbc, dcc = [], [], []
        ppg = np_ // SSD_GROUPS
        gw = d // SSD_GROUPS
        for g in range(SSD_GROUPS):
            dyg = dy_ref[:, gw * g:gw * (g + 1)]
            fn = functools.partial(_ssd_group, rev=rev, col=col0 + 2 * ppg * g)
            _, vjp = jax.vjp(fn, xh[g * ppg:(g + 1) * ppg], bcs[g], ccs[g], sts[g], cs_c, cs_r, tot, dtc, scores[g])
            dxh, dbc_g, dcc_g, dst, dcs_c_g, dcs_r_g, dtot_g, ddc_g, dsc_g = vjp((dyg, ds_ref[gw * g:gw * (g + 1), :]))
            for q in range(ppg):
                p = g * ppg + q
                v = dxh[q]
                if with_skip:
                    v = v + dyg[:, 128 * q:128 * (q + 1)] * sk_ref[:, 128 * p:128 * (p + 1)]
                dx_ref[:, 128 * p:128 * (p + 1)] = v
            ds_ref[gw * g:gw * (g + 1), :] = dst
            d_cs_c, d_cs_r, d_tot, ddc = plus(d_cs_c, dcs_c_g), plus(d_cs_r, dcs_r_g), plus(d_tot, dtot_g), plus(ddc, ddc_g)
            d_scores.append(dsc_g)
            dbc.append(dbc_g)
            dcc.append(dcc_g)
        dbc_s, dcc_s, ddc_s, ddr, dar, dac = vjp_shared((d_cs_c, d_cs_r, d_tot, d_scores))
        ddc = ddc + ddc_s
        dbc = [dbc[g] + dbc_s[g] for g in range(SSD_GROUPS)]
        dcc = [dcc[g] + dcc_s[g] for g in range(SSD_GROUPS)]
        for g in range(SSD_GROUPS):
            dx_ref[:, d + N_STATE * g:d + N_STATE * (g + 1)] = dbc[g]
            dx_ref[:, d + mx.gn + N_STATE * g:d + mx.gn + N_STATE * (g + 1)] = dcc[g]
        ddc_ref[...] = ddc
        ddr_ref[...] = ddr
        first = jnp.logical_and(b == 0, i == 0)
        _acc(dar_ref, first, dar)
        _acc(dac_ref, first, dac)

    ch = lambda b, i: sc.chunk(b, sc.nct - 1 - i)
    in_specs = [pl.BlockSpec((CHUNK, mx.xw), lambda b, i: (ch(b, i), 0)),
                pl.BlockSpec((CHUNK, DT_LANES), lambda b, i: (ch(b, i), 0)),
                pl.BlockSpec((DT_LANES, CHUNK), lambda b, i: (0, ch(b, i))),
                pl.BlockSpec((1, DT_LANES), lambda b, i: (0, 0)),
                pl.BlockSpec((DT_LANES, 1), lambda b, i: (0, 0)),
                pl.BlockSpec((hp, N_STATE), lambda b, i: (b * sc.nct + sc.nct - 1 - i, 0)),
                pl.BlockSpec((CHUNK, d), lambda b, i: (ch(b, i), 0))]
    inputs = [xbc, dt, dtt, a_row, a_col, hprev, dy]
    if with_skip:
        in_specs.append(pl.BlockSpec((1, d), lambda b, i: (0, 0)))
        inputs.append(dskip)
    return pl.pallas_call(
        body, grid=(lay.bl, sc.nct), in_specs=in_specs,
        out_specs=[pl.BlockSpec((CHUNK, mx.xw), lambda b, i: (ch(b, i), 0)),
                   pl.BlockSpec((CHUNK, DT_LANES), lambda b, i: (ch(b, i), 0)),
                   pl.BlockSpec((DT_LANES, CHUNK), lambda b, i: (0, ch(b, i))),
                   pl.BlockSpec((1, DT_LANES), lambda b, i: (0, 0)),
                   pl.BlockSpec((DT_LANES, 1), lambda b, i: (0, 0))],
        out_shape=[jax.ShapeDtypeStruct((lay.ta, mx.xw), F32), jax.ShapeDtypeStruct((lay.ta, DT_LANES), F32),
                   jax.ShapeDtypeStruct((DT_LANES, lay.ta), F32), jax.ShapeDtypeStruct((1, DT_LANES), F32),
                   jax.ShapeDtypeStruct((DT_LANES, 1), F32)],
        scratch_shapes=[pltpu.VMEM((hp, N_STATE), F32)], name=name, compiler_params=_cparams(2),
    )(*inputs)


def _gate_f(yf, yb, xh, z, drow, nw):
    dd = yf.shape[-1]
    half = dd // SSD_GROUPS
    yz = (yf + yb + drow * xh) * _silu(z)
    lo = lax.broadcasted_iota(jnp.int32, yz.shape, 1) < half
    sq = yz * yz
    ms1 = jnp.sum(jnp.where(lo, sq, 0.0), axis=-1, keepdims=True) / half
    ms2 = jnp.sum(jnp.where(lo, 0.0, sq), axis=-1, keepdims=True) / half
    return yz * jnp.where(lo, lax.rsqrt(ms1 + EPS), lax.rsqrt(ms2 + EPS)) * nw


def _gate_fwd(lay, mx, yf, yb, xbc, proj, drow, nw, name):
    d = mx.d_ssm

    def body(yf_ref, yb_ref, xh_ref, z_ref, d_ref, w_ref, o_ref):
        o_ref[...] = _gate_f(yf_ref[...], yb_ref[...], xh_ref[...], z_ref[...], d_ref[...], w_ref[...]).astype(o_ref.dtype)

    t, z = _tok(lay, d), _tok(lay, d, mx.off_z // d)
    return _tok_call(name, body, 1, lay.nsx, [t, t, t, z, _glob(1, d), _glob(1, d)], t,
                     jax.ShapeDtypeStruct((lay.tx, d + mx.d_conv), BF16), (yf, yb, xbc, proj, drow, nw))


def _gate_bwd(lay, mx, yf, yb, xbc, proj, drow, nw, dcat, name):
    d = mx.d_ssm
    nsx = lay.nsx

    def body(yf_ref, yb_ref, xh_ref, z_ref, d_ref, w_ref, dc_ref, dy_ref, dz_ref, dd_ref, dw_ref):
        s = pl.program_id(1)

        @pl.when(s < nsx)
        def _():
            _, vjp = jax.vjp(_gate_f, yf_ref[...], yb_ref[...], xh_ref[...], z_ref[...], d_ref[...], w_ref[...])
            dyf, _, _, dz, dd, dw = vjp(dc_ref[...])
            dy_ref[...] = dyf
            dz_ref[...] = dz.astype(dz_ref.dtype)
            _acc(dd_ref, s == 0, dd)
            _acc(dw_ref, s == 0, dw)

        @pl.when(s >= nsx)
        def _():
            dy_ref[...] = jnp.zeros_like(dy_ref)
            dz_ref[...] = jnp.zeros_like(dz_ref)

    t, z = _tok(lay, d), _tok(lay, d, mx.off_z // d)
    return _tok_call(name, body, 1, lay.ns, [t, t, t, z, _glob(1, d), _glob(1, d), _tok(lay, d, clamp=nsx - 1)],
                     [t, z, _glob(1, d), _glob(1, d)],
                     [jax.ShapeDtypeStruct((lay.ta, d), F32), jax.ShapeDtypeStruct((lay.ta, mx.pw), BF16),
                      jax.ShapeDtypeStruct((1, d), F32), jax.ShapeDtypeStruct((1, d), F32)],
                     (yf, yb, xbc, proj, drow, nw, dcat))


def _glu_fwd(lay, mx, proj, name):
    d = mx.d_conv
    c = math.gcd(mx.off_glu, d)
    cb = mx.off_glu // c

    def body(a_ref, b_ref, o_ref):
        o_ref[...] = a_ref[...] * jax.nn.sigmoid(b_ref[...])

    return _tok_call(name, body, d // c, lay.nsx, [_tok(lay, c, cb), _tok(lay, c, cb + d // c)], _tok(lay, c),
                     jax.ShapeDtypeStruct((lay.tx, d), F32), (proj, proj))


def _glu_bwd(lay, mx, proj, du, dproj, name):
    d = mx.d_conv
    nsx = lay.nsx
    whole = mx.off_glu % (2 * d) == 0
    c = d if whole else math.gcd(mx.off_glu, d)
    cb = mx.off_glu // c
    nc = d // c

    def body(a_ref, b_ref, du_ref, buf_ref, o_ref):
        j, s = pl.program_id(0), pl.program_id(1)

        @pl.when(s < nsx)
        def _():
            sg = jax.nn.sigmoid(b_ref[...])
            da = du_ref[...] * sg
            db = da * a_ref[...] * (1.0 - sg)
            if whole:
                o_ref[:, 0:d] = da.astype(o_ref.dtype)
                o_ref[:, d:] = db.astype(o_ref.dtype)
            else:
                o_ref[...] = jnp.where(j < nc, da, db).astype(o_ref.dtype)

        @pl.when(s >= nsx)
        def _():
            o_ref[...] = jnp.zeros_like(o_ref)

    win = lambda half: pl.BlockSpec((lay.tt, c), lambda j, s: (s, cb + half * nc + j % nc))
    out = _tok(lay, 2 * d, mx.off_glu // (2 * d)) if whole else _tok(lay, c, cb)
    return _tok_call(name, body, 1 if whole else 2 * nc, lay.ns,
                     [win(0), win(1), pl.BlockSpec((lay.tt, c), lambda j, s: (jnp.minimum(s, nsx - 1), j % nc)),
                      pl.BlockSpec(memory_space=pl.ANY)],
                     out, jax.ShapeDtypeStruct(dproj.shape, dproj.dtype), (proj, proj, du, dproj), aliases={3: 0})


def _axial(lay, mx, u, dy, cw, cb, name):
    d, seq = mx.d_conv, lay.seq
    kw = cw.shape[0]
    pad = kw // 2
    c = _pick(d // 2, (256, 128))
    ncb = d // c
    zpad = GRID_W * pad
    zpad = -(-zpad // 8) * 8
    backward = dy is not None

    def shifted(ext_ref, off):
        return ext_ref[pl.ds(zpad + off, seq), :]

    def valid_row(off):
        col = lax.broadcasted_iota(jnp.int32, (seq, c), 0) % GRID_W
        return jnp.logical_and(col + off >= 0, col + off < GRID_W)

    def fill(ext_ref, v):
        ext_ref[0:zpad, :] = jnp.zeros((zpad, c), F32)
        ext_ref[zpad:zpad + seq, :] = v
        ext_ref[zpad + seq:, :] = jnp.zeros((zpad, c), F32)

    def conv(ext_ref, w_ref, is_row, sign):
        acc = jnp.zeros((seq, c), F32)
        for k in range(kw):
            off = sign * ((k - pad) if is_row else GRID_W * (k - pad))
            v = shifted(ext_ref, off)
            if is_row:
                v = jnp.where(valid_row(off), v, 0.0)
            acc = acc + w_ref[k:k + 1, :] * v
        return acc

    def fwd_body(u_ref, w_ref, b_ref, o_ref, ext_ref):
        j = pl.program_id(0)
        fill(ext_ref, u_ref[...])

        @pl.when(j < ncb // 2)
        def _():
            o_ref[...] = conv(ext_ref, w_ref, True, 1) + b_ref[...]

        @pl.when(j >= ncb // 2)
        def _():
            o_ref[...] = conv(ext_ref, w_ref, False, 1) + b_ref[...]

    def bwd_body(u_ref, dy_ref, w_ref, du_ref, dw_ref, db_ref, extu_ref, extd_ref):
        j, b = pl.program_id(0), pl.program_id(1)
        dyv = dy_ref[...]
        fill(extu_ref, u_ref[...])
        fill(extd_ref, dyv)

        def grads(is_row):
            du_ref[...] = conv(extd_ref, w_ref, is_row, -1)
            rows = []
            for k in range(kw):
                off = (k - pad) if is_row else GRID_W * (k - pad)
                v = shifted(extu_ref, off)
                if is_row:
                    v = jnp.where(valid_row(off), v, 0.0)
                rows.append(jnp.sum(dyv * v, axis=0, keepdims=True))
            _acc(dw_ref, b == 0, jnp.concatenate(rows, axis=0))

        @pl.when(j < ncb // 2)
        def _():
            grads(True)

        @pl.when(j >= ncb // 2)
        def _():
            grads(False)

        _acc(db_ref, b == 0, jnp.sum(dyv, axis=0, keepdims=True))

    seq_spec = pl.BlockSpec((seq, c), lambda j, b: (b, j))
    w_spec = pl.BlockSpec((kw, c), lambda j, b: (0, j))
    b_spec = pl.BlockSpec((1, c), lambda j, b: (0, j))
    ext = pltpu.VMEM((seq + 2 * zpad, c), F32)
    if not backward:
        return pl.pallas_call(fwd_body, grid=(ncb, lay.bl), in_specs=[seq_spec, w_spec, b_spec], out_specs=seq_spec,
                              out_shape=jax.ShapeDtypeStruct((lay.tx, d), F32), scratch_shapes=[ext], name=name,
                              compiler_params=_cparams(2))(u, cw, cb)
    return pl.pallas_call(bwd_body, grid=(ncb, lay.bl), in_specs=[seq_spec, seq_spec, w_spec],
                          out_specs=[seq_spec, w_spec, b_spec],
                          out_shape=[jax.ShapeDtypeStruct((lay.tx, d), F32), jax.ShapeDtypeStruct((kw, d), F32),
                                     jax.ShapeDtypeStruct((1, d), F32)],
                          scratch_shapes=[ext, ext], name=name, compiler_params=_cparams(2))(u, dy, cw)


def _ln_silu_f(u, w, b):
    mu = jnp.mean(u, axis=-1, keepdims=True)
    var = jnp.mean(jnp.square(u - mu), axis=-1, keepdims=True)
    return _silu((u - mu) * lax.rsqrt(var + EPS) * w + b)


def _ln_fwd(lay, mx, u, w, b, cat, name):
    d = mx.d_conv
    assert mx.d_ssm % d == 0

    def body(u_ref, w_ref, b_ref, cat_ref, o_ref):
        o_ref[...] = _ln_silu_f(u_ref[...], w_ref[...], b_ref[...]).astype(o_ref.dtype)

    return _tok_call(name, body, 1, lay.nsx,
                     [_tok(lay, d), _glob(1, d), _glob(1, d), pl.BlockSpec(memory_space=pl.ANY)],
                     _tok(lay, d, mx.d_ssm // d), jax.ShapeDtypeStruct(cat.shape, cat.dtype), (u, w, b, cat),
                     aliases={3: 0})


def _ln_bwd(lay, mx, u, w, b, dcat, name):
    d = mx.d_conv

    def body(u_ref, w_ref, b_ref, dc_ref, du_ref, dw_ref, db_ref):
        s = pl.program_id(1)
        _, vjp = jax.vjp(_ln_silu_f, u_ref[...], w_ref[...], b_ref[...])
        du, dw, db = vjp(dc_ref[...])
        du_ref[...] = du
        _acc(dw_ref, s == 0, dw)
        _acc(db_ref, s == 0, db)

    return _tok_call(name, body, 1, lay.nsx, [_tok(lay, d), _glob(1, d), _glob(1, d), _tok(lay, d, 1)],
                     [_tok(lay, d), _glob(1, d), _glob(1, d)],
                     [jax.ShapeDtypeStruct((lay.tx, d), F32), jax.ShapeDtypeStruct((1, d), F32),
                      jax.ShapeDtypeStruct((1, d), F32)], (u, w, b, dcat))


def _gate_tile(dff):
    return dff // 2 if (dff // 2) % 128 == 0 else dff


def _ffn_fwd(lay, nseg, x, nw, modv, k0, wts, tag, ride=None, ride_down=None, h=None, then_norm=None):
    wgu, wd, ft = wts
    if h is None:
        h = _norm_mod_fwd(lay, nseg, x, nw, modv, k0, tag + "_norm")
    t = h.shape[0]

    def act(acc):
        g, u = acc[:, :ft], acc[:, ft:]
        sg = jax.nn.sigmoid(g)
        sl = g * sg
        return jnp.concatenate([u * (sg * (1.0 + g * (1.0 - sg))), sl], axis=1), sl * u

    res = _mm(h, wgu, "nn", None, tag + "_gu", tn=2 * ft, tm=256 if t % 256 == 0 else None,
              epilogue=act, outs=[(2 * ft, BF16), (ft, BF16)], ride=ride)
    (s, a), rode = res if ride else (res, None)
    if callable(wd):
        wd = wd(rode)
    res = _mm_resid(lay, a, wd, x, modv, k0 + 2, 0.5, tag + "_down", then_norm=then_norm, ride=ride_down)
    (o, y, *h_next), rode_down = res if ride_down else (res, None)
    return y, (x, h, s, a, o), (rode, rode_down), (h_next[0] if h_next else None)


def _ffn_bwd(lay, nseg, dy, saved, nw, modv, k0, wts, tag, nout=None, hooks=None):
    wgu, wd, ft = wts
    x, h, s, a, o = saved
    do, dgate = _resid_bwd(lay, nseg, dy, o, modv, k0 + 2, 0.5, tag + "_dres")

    def through_act(da, s_tile):
        return (jnp.concatenate([da, da], axis=1) * s_tile.astype(F32),)

    (dgu,) = _mm(do, wd.T, "nn", None, tag + "_da", tn=ft, tm=_big_tile(do.shape[0], 1024), extras=[(s, 2 * ft)],
                 epilogue=through_act, outs=[(2 * ft, BF16)])
    dwd = _mm(a, do, "tn", F32, tag + "_dwd")
    if hooks is None:
        dwgu = _mm(h, dgu, "tn", F32, tag + "_dwgu")
        dh = _mm(dgu, wgu, "nt", F32, tag + "_dh")
    else:
        dwgu, rode_a = _mm(h, dgu, "tn", F32, tag + "_dwgu", ride=hooks.after_dwd(dwd))
        dh, rode_b = _mm(dgu, wgu, "nt", F32, tag + "_dh", ride=hooks.after_dwgu(dwgu))
        hooks.take(rode_a, rode_b)
    dx, dnw, dss = _norm_mod_bwd(lay, nseg, nseg, x, nw, modv, k0, dh, dy, tag + "_dnorm", nout=nout)
    return dx, (dwgu, dwd), dnw, jnp.concatenate([dss, dgate], axis=1)


def _local_step(lay, mx, xa, target, modv, w, exch=None):
    d, bl = lay.d, lay.bl
    g = {}
    xa1, ffn1, late, ha = _ffn_fwd(lay, lay.ns, xa, w["norm_ffn1"], modv, 0, w["ffn1"], "ffn1",
                                   ride=exch.late_a if exch else None, ride_down=exch.late_b if exch else None,
                                   then_norm=(w["norm_mix"], 3))
    if exch:
        w = {**w, **exch.unpack_late(w["ffn1"], *late)}
    proj = _mm(ha, w["w_in"], "nn", F32, "mix_in")
    pre, xbc = _conv5_fwd(lay.fine(), mx, proj, w["conv_w"], w["conv_b"], "mix_conv")
    dt = _dt_fwd(lay, mx, proj, w["dt_bias"], "mix_dt")
    dtt = dt.T
    yf, hpf = _ssd_fwd(lay, mx, xbc, dt, dtt, w["a_row"], w["a_col"], False, "ssd_f")
    yb, hpb = _ssd_fwd(lay, mx, xbc, dt, dtt, w["a_row"], w["a_col"], True, "ssd_b")
    cat_y = _gate_fwd(lay, mx, yf, yb, xbc, proj, w["d_row"], w["ssm_norm_w"], "mix_gate")
    u0 = _glu_fwd(lay, mx, proj, "mix_glu")
    uc = _axial(lay, mx, u0, None, w["cconv_w"], w["cconv_b"], "mix_axial")
    cat = _ln_fwd(lay, mx, uc, w["ln_w"], w["ln_b"], cat_y, "mix_ln")
    mix, x2, h2 = _mm_resid(lay, cat, w["w_out"], xa1, modv, 5, 1.0, "mix_out", then_norm=(w["norm_ffn2"], 6))
    x3, ffn2, _, _ = _ffn_fwd(lay, lay.nsx, x2, w["norm_ffn2"], modv, 6, w["ffn2"], "ffn2", h=h2)
    loss, dx3, g["final_norm"] = _final_loss(lay, x3, w["final_norm"], target, "loss")
    dx2, g["ffn2"], g["norm_ffn2"], dmod2 = _ffn_bwd(lay, lay.nsx, dx3, ffn2, w["norm_ffn2"], modv, 6, w["ffn2"], "ffn2")
    dmix, dg2 = _resid_bwd(lay, lay.nsx, dx2, mix, modv, 5, 1.0, "mix_dres")
    dcat = _mm(dmix, w["w_out"], "nt", F32, "mix_dcat")
    g["w_out"] = _mm(cat, dmix, "tn", F32, "mix_dwout")
    duc, g["ln_w"], g["ln_b"] = _ln_bwd(lay, mx, uc, w["ln_w"], w["ln_b"], dcat, "mix_dln")
    du0, g["cconv_w"], g["cconv_b"] = _axial(lay, mx, u0, duc, w["cconv_w"], None, "mix_daxial")
    dyssd, dproj, g["d_row"], g["ssm_norm_w"] = _gate_bwd(lay, mx, yf, yb, xbc, proj, w["d_row"], w["ssm_norm_w"], dcat,
                                                          "mix_dgate")
    dproj = _glu_bwd(lay, mx, proj, du0, dproj, "mix_dglu")
    dxf, ddcf, ddrf, darf, dacf = _ssd_bwd(lay, mx, xbc, dt, dtt, w["a_row"], w["a_col"], hpf, dyssd, w["d_row"],
                                           False, "ssd_df")
    dxb, ddcb, ddrb, darb, dacb = _ssd_bwd(lay, mx, xbc, dt, dtt, w["a_row"], w["a_col"], hpb, dyssd, None,
                                           True, "ssd_db")
    g["a_row"] = (darf + darb) + (dacf + dacb).T
    dproj, g["conv_w"], g["conv_b"] = _conv5_bwd(lay.fine(), mx, proj, pre, dxf, dxb, w["conv_w"], dproj, "mix_dconv")
    dproj, g["dt_bias"] = _dt_bwd(lay, mx, proj, w["dt_bias"], (ddcf, ddcb, ddrf.T, ddrb.T), dproj, "mix_ddt")
    if exch:
        dha, got = _mm(dproj, w["w_in"], "nt", F32, "mix_dha", ride=exch.ffn2_sibling(g["ffn2"]))
        exch.take_ffn2(got)
    else:
        dha = _mm(dproj, w["w_in"], "nt", F32, "mix_dha")
    g["w_in"] = _mm(ha, dproj, "tn", F32, "mix_dwin")
    dxa1, g["norm_mix"], dss_mix = _norm_mod_bwd(lay, lay.ns, lay.nsx, xa1, w["norm_mix"], modv, 3, dha, dx2, "mix_dnorm")
    if exch:
        exch.early_grads(g)
    dx, g["ffn1"], g["norm_ffn1"], dmod1 = _ffn_bwd(lay, lay.ns, dxa1, ffn1, w["norm_ffn1"], modv, 0, w["ffn1"], "ffn1",
                                                   nout=lay.nsx, hooks=exch)
    zrow = lambda t: jnp.concatenate([t, jnp.zeros((1,) + t.shape[1:], F32)], axis=0)
    dmodv = jnp.concatenate([dmod1, dss_mix, zrow(dg2), zrow(dmod2)], axis=1)
    return loss, dx, g, dmodv


class _GatherRide:
    def __init__(self, xs):
        self.arrays = list(xs)
        self.na = len(xs)
        self.out_shapes = [jax.ShapeDtypeStruct((N_DEV,) + tuple(x.shape), x.dtype) for x in xs]
        self.sems = [pltpu.SemaphoreType.DMA((7 * self.na,)), pltpu.SemaphoreType.DMA((7 * self.na,)),
                     pltpu.SemaphoreType.DMA((self.na,))]

    def _plan(self, x_refs, out_refs, send_sems, recv_sems, local_sems):
        mx_, my_, mc_ = lax.axis_index("x"), lax.axis_index("y"), lax.axis_index("c")
        me, sibling = (mx_, my_, mc_), (mx_, my_, 1 - mc_)
        chips = [(1 - mx_, my_), (mx_, 1 - my_), (1 - mx_, 1 - my_)]

        def slot(a, px, py, pc):
            return out_refs[a].at[4 * px + 2 * py + pc]

        def copy(a, k, block, to, own=False):
            return pltpu.make_async_remote_copy(
                src_ref=x_refs[a] if own else slot(a, *block), dst_ref=slot(a, *block),
                send_sem=send_sems.at[7 * a + k], recv_sem=recv_sems.at[7 * a + k], device_id=to, device_id_type=MESH)

        mine = [pltpu.make_async_copy(x_refs[a], slot(a, *me), local_sems.at[a]) for a in range(self.na)]
        first = []
        for a in range(self.na):
            first.append(copy(a, 0, me, sibling, own=True))
            first += [copy(a, 1 + j, me, (*chip, mc_), own=True) for j, chip in enumerate(chips)]
        return me, sibling, chips, mc_, copy, mine, first

    def start(self, x_refs, out_refs, send_sems, recv_sems, local_sems):
        *_, mine, first = self._plan(x_refs, out_refs, send_sems, recv_sems, local_sems)
        for cp in mine + first:
            cp.start()

    def finish(self, x_refs, out_refs, send_sems, recv_sems, local_sems):
        me, sibling, chips, mc_, copy, mine, first = self._plan(x_refs, out_refs, send_sems, recv_sems, local_sems)
        passed = []
        for j, chip in enumerate(chips):
            for a in range(self.na):
                copy(a, 1 + j, (*chip, mc_), me).wait_recv()
                fwd = copy(a, 4 + j, (*chip, mc_), sibling)
                fwd.start()
                passed.append(fwd)
        for a in range(self.na):
            copy(a, 0, sibling, me).wait_recv()
            for j, chip in enumerate(chips):
                copy(a, 4 + j, (*chip, 1 - mc_), me).wait_recv()
        for cp in first + passed:
            cp.wait_send()
        for cp in mine:
            cp.wait()


def _exchange(ride, name, in_hbm=True):
    n_in, n_out = len(ride.arrays), len(ride.out_shapes)

    def body(*refs):
        ins, outs, sems = refs[:n_in], refs[n_in:n_in + n_out], refs[n_in + n_out:]
        ride.start(ins, outs, *sems)
        ride.finish(ins, outs, *sems)

    space = pl.BlockSpec(memory_space=pl.ANY if in_hbm else pltpu.VMEM)
    return pl.pallas_call(body, out_shape=list(ride.out_shapes), in_specs=[space] * n_in, out_specs=[space] * n_out,
                          scratch_shapes=list(ride.sems), name=name)(*ride.arrays)


def _all_gather(xs, name, in_hbm):
    return _exchange(_GatherRide(xs), name, in_hbm)


N_CHIPS = 4


class _SiblingRide:
    def __init__(self, gs):
        self.arrays = list(gs)
        self.na = len(gs)
        self.out_shapes = [jax.ShapeDtypeStruct((N_CHIPS,) + tuple(g.shape[2:]), g.dtype) for g in gs]
        self.sems = [pltpu.SemaphoreType.DMA((N_CHIPS * self.na,)), pltpu.SemaphoreType.DMA((N_CHIPS * self.na,))]

    def _copies(self, g_refs, out_refs, send_sems, recv_sems):
        mx_, my_, mc_ = lax.axis_index("x"), lax.axis_index("y"), lax.axis_index("c")
        return [pltpu.make_async_remote_copy(
            src_ref=g_refs[a].at[k, 1 - mc_], dst_ref=out_refs[a].at[k], send_sem=send_sems.at[N_CHIPS * a + k],
            recv_sem=recv_sems.at[N_CHIPS * a + k], device_id=(mx_, my_, 1 - mc_), device_id_type=MESH)
            for a in range(self.na) for k in range(N_CHIPS)]

    def start(self, g_refs, out_refs, send_sems, recv_sems):
        for cp in self._copies(g_refs, out_refs, send_sems, recv_sems):
            cp.start()

    def finish(self, g_refs, out_refs, send_sems, recv_sems):
        copies = self._copies(g_refs, out_refs, send_sems, recv_sems)
        for cp in copies:
            cp.wait_recv()
        for cp in copies:
            cp.wait_send()


def _row_tile(r, n):
    if r * n * 4 <= (1 << 20):
        return r
    for t in (1024, 512, 256, 128, 64, 32, 16, 8):
        if r % t == 0 and t * n * 4 <= (1 << 20):
            return t
    return r


def _pair_add(place, g, got, name):
    _, _, r, n = g.shape
    tr = r if r * n * 4 <= (3 << 19) else _row_tile(r, n)

    def body(place_ref, g_ref, got_ref, o_ref, ob_ref):
        s = g_ref[...] + got_ref[...]
        o_ref[...] = s
        ob_ref[...] = s.astype(ob_ref.dtype)

    blk = pl.BlockSpec((None, tr, n), lambda k, i, pr: (k, i, 0))
    grid_spec = pltpu.PrefetchScalarGridSpec(
        num_scalar_prefetch=1, grid=(N_CHIPS, r // tr),
        in_specs=[pl.BlockSpec((None, None, tr, n), lambda k, i, pr: (k, pr[0], i, 0)), blk], out_specs=[blk, blk])
    return pl.pallas_call(body, grid_spec=grid_spec,
                          out_shape=[jax.ShapeDtypeStruct((N_CHIPS, r, n), F32), jax.ShapeDtypeStruct((N_CHIPS, r, n), BF16)],
                          name=name, compiler_params=_cparams(2))(place, g, got)


class _ChipSwapRide:
    def __init__(self, ps):
        self.arrays = list(ps)
        self.na = len(ps)
        self.out_shapes = [jax.ShapeDtypeStruct((N_CHIP_PEERS,) + tuple(p.shape[1:]), p.dtype) for p in ps]
        self.sems = [pltpu.SemaphoreType.DMA((N_CHIP_PEERS * self.na,)), pltpu.SemaphoreType.DMA((N_CHIP_PEERS * self.na,))]

    def _copies(self, p_refs, out_refs, send_sems, recv_sems):
        mx_, my_, mc_ = lax.axis_index("x"), lax.axis_index("y"), lax.axis_index("c")
        chips = [(1 - mx_, my_), (mx_, 1 - my_), (1 - mx_, 1 - my_)]
        return [pltpu.make_async_remote_copy(
            src_ref=p_refs[a].at[2 * cx + cy], dst_ref=out_refs[a].at[j], send_sem=send_sems.at[N_CHIP_PEERS * a + j],
            recv_sem=recv_sems.at[N_CHIP_PEERS * a + j], device_id=(cx, cy, mc_), device_id_type=MESH)
            for a in range(self.na) for j, (cx, cy) in enumerate(chips)]

    def start(self, p_refs, out_refs, send_sems, recv_sems):
        for cp in self._copies(p_refs, out_refs, send_sems, recv_sems):
            cp.start()

    def finish(self, p_refs, out_refs, send_sems, recv_sems):
        copies = self._copies(p_refs, out_refs, send_sems, recv_sems)
        for cp in copies:
            cp.wait_recv()
        for cp in copies:
            cp.wait_send()


def _sum_lead(x, name):
    k, r, n = x.shape
    tr = _row_tile(r, n * k)

    def body(x_ref, o_ref):
        acc = x_ref[0]
        for i in range(1, k):
            acc = acc + x_ref[i]
        o_ref[...] = acc

    return pl.pallas_call(body, grid=(r // tr,), in_specs=[pl.BlockSpec((k, tr, n), lambda i: (0, i, 0))],
                          out_specs=pl.BlockSpec((tr, n), lambda i: (i, 0)),
                          out_shape=jax.ShapeDtypeStruct((r, n), x.dtype), name=name, compiler_params=_cparams(1))(x)


def _adamw(place, w, parts, m, v, name):
    shape = w.shape
    cols = shape[-1]
    rows = math.prod(shape[:-1])
    to2 = lambda t: t.reshape(rows, cols)
    tr = _row_tile(rows, cols) if rows * cols * 4 > (1 << 20) else rows
    npart = len(parts)
    spec = pl.BlockSpec((tr, cols), lambda i, pr: (i, 0))
    native = len(shape) == 3 and shape[0] == 1
    own = pl.BlockSpec((None, tr, cols), lambda i, pr: (0, i, 0)) if native else spec
    as_own = (lambda t: t) if native else to2
    part_specs, part_args = [], []
    for piece in parts:
        if isinstance(piece, tuple):
            stack, k, row0 = piece
            part_args.append(stack.reshape(stack.shape[0], -1, cols))
            assert row0 % tr == 0
            if k == "chip":
                part_specs.append(pl.BlockSpec((None, tr, cols), functools.partial(lambda i, pr, b0: (pr[1], i + b0, 0),
                                                                                   b0=row0 // tr)))
            else:
                part_specs.append(pl.BlockSpec((None, tr, cols), functools.partial(
                    lambda i, pr, kk, b0: (kk, i + b0, 0), kk=k, b0=row0 // tr)))
        else:
            part_args.append(to2(piece))
            part_specs.append(spec)

    def body(place_ref, *refs):
        w_ref, m_ref, v_ref = refs[0], refs[1 + npart], refs[2 + npart]
        g_ref, d_ref, nm_ref, nv_ref = refs[3 + npart:]
        g = refs[1][...].astype(F32)
        for q in range(1, npart):
            g = g + refs[1 + q][...].astype(F32)
        mm = ADAM_B1 * m_ref[...] + (1.0 - ADAM_B1) * g
        vv = ADAM_B2 * v_ref[...] + (1.0 - ADAM_B2) * jnp.square(g)
        m_hat = mm / (1.0 - ADAM_B1 ** ADAM_STEP)
        v_hat = vv / (1.0 - ADAM_B2 ** ADAM_STEP)
        g_ref[...] = g
        d_ref[...] = -ADAM_LR * (m_hat / (jnp.sqrt(v_hat) + ADAM_EPS) + ADAM_WD * w_ref[...])
        nm_ref[...] = mm
        nv_ref[...] = vv

    sh = jax.ShapeDtypeStruct(shape if native else (rows, cols), F32)
    grid_spec = pltpu.PrefetchScalarGridSpec(num_scalar_prefetch=1, grid=(rows // tr,),
                                             in_specs=[own] + part_specs + [own, own], out_specs=[own] * 4)
    outs = pl.pallas_call(body, grid_spec=grid_spec, out_shape=[sh] * 4, name=name, compiler_params=_cparams(1),
                          )(place, as_own(w), *part_args, as_own(m), as_own(v))
    return tuple(o.reshape(shape) for o in outs)


def _packed_rows(n, width):
    return -(-n // (8 * width)) * 8


def _pack_rows(items, width):
    rows = []
    for t in items:
        flat = t.reshape(-1)
        n = flat.shape[0]
        k = _packed_rows(n, width)
        if k * width > n:
            flat = jnp.concatenate([flat, jnp.zeros((k * width - n,), t.dtype)])
        rows.append(flat.reshape(k, width))
    return jnp.concatenate(rows, axis=0)


def _unpack_rows(packed, shapes, lead=()):
    width = packed.shape[-1]
    out, r = [], 0
    for sh in shapes:
        n = math.prod(sh)
        k = _packed_rows(n, width)
        piece = packed[..., r:r + k, :].reshape(tuple(lead) + (k * width,))[..., :n]
        out.append(piece.reshape(tuple(lead) + tuple(sh)))
        r += k
    return out


def _cols_full(t):
    return jnp.transpose(t, (1, 0, 2)).reshape(t.shape[1], -1)


def _cols_shards(t):
    d = t.shape[0]
    return jnp.transpose(t.reshape(d, N_DEV, -1), (1, 0, 2))


BIG = ("ffn1_gate", "ffn1_up", "ffn1_down", "w_in", "w_out", "ffn2_gate", "ffn2_up", "ffn2_down")


def kernel(x, c, ctx, c_ctx, w_mod, b_mod, norm_ffn1, ffn1_gate, ffn1_up, ffn1_down, norm_mix, w_in, ssm_conv_w, ssm_conv_b, dt_bias_fwd, dt_bias_bwd, a_log_fwd, a_log_bwd, ssm_d, ssm_norm_w, cconv_w, cconv_b, cconv_ln_w, cconv_ln_b, w_out, norm_ffn2, ffn2_gate, ffn2_up, ffn2_down, final_norm, loss_target, m_c_ctx, m_w_mod, m_b_mod, m_norm_ffn1, m_ffn1_gate, m_ffn1_up, m_ffn1_down, m_norm_mix, m_w_in, m_ssm_conv_w, m_ssm_conv_b, m_dt_bias_fwd, m_dt_bias_bwd, m_a_log_fwd, m_a_log_bwd, m_ssm_d, m_ssm_norm_w, m_cconv_w, m_cconv_b, m_cconv_ln_w, m_cconv_ln_b, m_w_out, m_norm_ffn2, m_ffn2_gate, m_ffn2_up, m_ffn2_down, m_final_norm, v_c_ctx, v_w_mod, v_b_mod, v_norm_ffn1, v_ffn1_gate, v_ffn1_up, v_ffn1_down, v_norm_mix, v_w_in, v_ssm_conv_w, v_ssm_conv_b, v_dt_bias_fwd, v_dt_bias_bwd, v_a_log_fwd, v_a_log_bwd, v_ssm_d, v_ssm_norm_w, v_cconv_w, v_cconv_b, v_cconv_ln_w, v_cconv_ln_b, v_w_out, v_norm_ffn2, v_ffn2_gate, v_ffn2_up, v_ffn2_down, v_final_norm):
    args = dict(locals())
    names = ("c_ctx", "w_mod", "b_mod", "norm_ffn1", "ffn1_gate", "ffn1_up", "ffn1_down", "norm_mix", "w_in",
             "ssm_conv_w", "ssm_conv_b", "dt_bias_fwd", "dt_bias_bwd", "a_log_fwd", "a_log_bwd", "ssm_d", "ssm_norm_w",
             "cconv_w", "cconv_b", "cconv_ln_w", "cconv_ln_b", "w_out", "norm_ffn2", "ffn2_gate", "ffn2_up",
             "ffn2_down", "final_norm")
    wts = {n: args[n] for n in names}
    bl, seq, d = x.shape
    clen = ctx.shape[1]
    heads = dt_bias_fwd.shape[1]
    ft = _gate_tile(ffn1_gate.shape[2] * N_DEV)
    lay = _Lay(bl, seq, clen, d)
    mx = _Mix(d, heads)
    nb = bl * N_DEV
    me = 4 * lax.axis_index("x") + 2 * lax.axis_index("y") + lax.axis_index("c")
    mcols = w_mod.shape[2]
    n_ctx_mod = 5 * d

    place = jnp.stack([lax.axis_index("c"), 2 * lax.axis_index("x") + lax.axis_index("y")]).astype(jnp.int32)

    small_shapes = [(bl, d), ssm_conv_w.shape[1:], cconv_w.shape[1:]]
    (g1,) = _all_gather([_pack_rows([c, ssm_conv_w, cconv_w], d)], "gather_small", False)
    c_g, conv_g, cconv_g = _unpack_rows(g1, small_shapes, (N_DEV,))
    c_all = c_g.reshape(nb, d)
    conv_w_full = jnp.transpose(conv_g, (1, 0, 2)).reshape(conv_g.shape[1], -1)
    cconv_w_full = jnp.transpose(cconv_g, (1, 0, 2)).reshape(cconv_g.shape[1], -1)

    s_all = jnp.concatenate([_silu(c_all), _silu(c_ctx)[None, :], jnp.zeros((7, d), F32)], axis=0)
    mod_cols = _mm(s_all, w_mod[0], "nn", F32, "mod_cols")
    (g2,) = _all_gather([mod_cols], "gather_mod", False)
    mod_all = _cols_full(g2) + b_mod
    mod_mine = jnp.concatenate([lax.dynamic_slice_in_dim(mod_all, me * bl, bl, axis=0), mod_all[nb:nb + 1]], axis=0)
    modv = mod_mine.reshape(bl + 1, N_MOD, d)

    hh = 2 * heads
    shard16 = lambda n: wts[n][0].astype(BF16)

    nl = ffn1_gate.shape[2]
    spt = ft // nl
    assert ft % nl == 0 and N_DEV % spt == 0

    def ffn_weights(gate, up, down):
        both = jnp.stack([gate, up], axis=1).reshape(N_DEV // spt, spt, 2, d, nl)
        return jnp.transpose(both, (3, 0, 2, 1, 4)).reshape(d, -1), None if down is None else down.reshape(-1, d), ft

    def grads_by_dest(name, grad):
        if name == "w_in":
            grad = _cols_shards(jnp.concatenate([grad[:, mx.off_z:], grad[:, :mx.off_dt + hh],
                                                 grad[:, mx.off_glu:mx.off_z]], axis=1))
        elif name.endswith("_gu"):
            grad = jnp.transpose(grad.reshape(d, N_DEV // spt, 2, spt, nl), (1, 3, 2, 0, 4)).reshape(N_DEV, 2 * d, nl)
        else:
            grad = grad.reshape((N_DEV,) + tuple(wts[name].shape[1:]))
        return grad.reshape((N_CHIPS, 2) + tuple(grad.shape[1:]))

    def ffn_grads(tag, pair):
        return {tag + "_gu": pair[0], tag + "_down": pair[1]}

    def by_dest_of(named):
        return {n: grads_by_dest(n, grad) for n, grad in named.items()}

    def to_chip_sums(named, got=None):
        by_dest = by_dest_of(named) if got is None else named
        names_ = list(by_dest)
        if got is None:
            got = _exchange(_SiblingRide([by_dest[n] for n in names_]), "rs_sibling_" + names_[0])
        return {n: _pair_add(place, by_dest[n], s, "rs_pair_add_" + n) for n, s in zip(names_, got)}

    class _Overlap:
        names_a = ("ffn1_down", "w_in", "ffn2_gate")
        names_b = ("w_out", "ffn2_up", "ffn2_down")
        late_a = _GatherRide([shard16(n) for n in names_a])
        late_b = _GatherRide([shard16(n) for n in names_b])
        sums, arrived = {}, {}

        def unpack_late(self, ffn1, outs_a, outs_b):
            full = {**dict(zip(self.names_a, outs_a)), **dict(zip(self.names_b, outs_b))}
            w_in_f = _cols_full(full["w_in"])
            w_in_p = jnp.concatenate([w_in_f[:, mx.ref_x:mx.ref_glu], jnp.zeros((d, DT_PAD - hh), BF16),
                                      w_in_f[:, mx.ref_glu:], w_in_f[:, :d]], axis=1)
            return {"w_in": w_in_p, "w_out": full["w_out"].reshape(-1, d),
                    "ffn1": (ffn1[0], full["ffn1_down"].reshape(-1, d), ffn1[2]),
                    "ffn2": ffn_weights(full["ffn2_gate"], full["ffn2_up"], full["ffn2_down"])}

        groups = (("ffn2_gu", "ffn2_down", "ffn1_down"), ("w_in", "w_out", "ffn1_gu"))

        def ffn2_sibling(self, g_ffn2):
            self.ffn2_dest = by_dest_of(ffn_grads("ffn2", g_ffn2))
            return _SiblingRide(list(self.ffn2_dest.values()))

        def take_ffn2(self, got):
            self.ffn2_got = got

        def early_grads(self, g):
            self.sums = {**to_chip_sums(self.ffn2_dest, self.ffn2_got),
                         **to_chip_sums({"w_in": g["w_in"], "w_out": g["w_out"]})}

        def _ride(self, grp):
            return _ChipSwapRide([self.sums[n][1] for n in grp])

        def after_dwd(self, dwd):
            self.sums.update(to_chip_sums({"ffn1_down": dwd}))
            return self._ride(self.groups[0])

        def after_dwgu(self, dwgu):
            self.sums.update(to_chip_sums({"ffn1_gu": dwgu}))
            return self._ride(self.groups[1])

        def take(self, *rode):
            for grp, outs in zip(self.groups, rode):
                self.arrived.update(zip(grp, outs))

    exch = _Overlap()
    gate1, up1 = _all_gather([shard16("ffn1_gate"), shard16("ffn1_up")], "gather_weights", True)
    wgu1 = ffn_weights(gate1, up1, None)[0]
    lanes_pad = lambda a, b: jnp.concatenate([a, b, jnp.zeros((1, DT_LANES - hh), F32)], axis=1)
    a_vals = lanes_pad(-jnp.exp(a_log_fwd), -jnp.exp(a_log_bwd))
    w = {
        "norm_ffn1": norm_ffn1, "norm_mix": norm_mix, "norm_ffn2": norm_ffn2, "final_norm": final_norm[None, :],
        "ffn1": (wgu1, lambda outs_a: outs_a[0].reshape(-1, d), ft),
        "conv_w": jnp.concatenate([conv_w_full, jnp.zeros((3, mx.xw), F32)], axis=0), "conv_b": ssm_conv_b,
        "dt_bias": lanes_pad(dt_bias_fwd, dt_bias_bwd), "a_row": a_vals, "a_col": a_vals.T,
        "d_row": jnp.repeat(ssm_d, HEAD_DIM, axis=1), "ssm_norm_w": ssm_norm_w,
        "cconv_w": cconv_w_full, "cconv_b": cconv_b, "ln_w": cconv_ln_w, "ln_b": cconv_ln_b,
    }

    xa = jnp.concatenate([x.reshape(bl * seq, d), ctx.reshape(bl * clen, d)], axis=0)
    loss, grad_x, g, dmodv = _local_step(lay, mx, xa, loss_target.reshape(bl * seq, d), modv, w, exch)
    loss = lax.psum(loss[0, 0], ("x", "y", "c"))

    sums, arrived = exch.sums, exch.arrived

    def big_parts(n):
        key, row0 = (n[:4] + "_gu", d if n.endswith("_up") else 0) if n.endswith(("_gate", "_up")) else (n, 0)
        return [(sums[key][0], "chip", row0)] + [(arrived[key], k, row0) for k in range(N_CHIP_PEERS)]

    n9 = N_MOD * d
    dmod_rows = dmodv.reshape(bl + 1, n9)
    ctx_row = jnp.concatenate([dmod_rows[bl, :n_ctx_mod], jnp.zeros((n9 - n_ctx_mod,), F32)])
    summed = [ctx_row, g["norm_ffn1"], g["norm_mix"], g["norm_ffn2"], g["final_norm"], g["conv_b"], g["dt_bias"],
              g["a_row"], g["d_row"], g["ssm_norm_w"], g["cconv_b"], g["ln_w"], g["ln_b"], g["conv_w"][:5], g["cconv_w"]]
    sum_shapes = [t.shape for t in summed]
    (g4,) = _all_gather([_pack_rows([dmod_rows[:bl]] + summed, d)], "gather_small_grads", False)
    dmod_batch = g4[:, :bl * N_MOD].reshape(nb, n9)
    tot = _sum_lead(g4[:, _packed_rows(bl * n9, d):], "sum_small_grads")
    (dctx, g_n1, g_nm, g_n2, g_fn, g_cb, g_dtb, g_a, g_drow, g_snw, g_ccb, g_lnw, g_lnb, g_cw, g_ccw) = _unpack_rows(tot, sum_shapes)
    dmod_all = jnp.concatenate([dmod_batch, dctx[None, :], jnp.zeros((7, n9), F32)], axis=0)

    dmod_my_cols = lax.dynamic_slice_in_dim(dmod_all, me * mcols, mcols, axis=1)
    g_w_mod = _mm(s_all, dmod_my_cols, "tn", F32, "dw_mod")[None]
    g_b_mod = _sum_lead(dmod_all.reshape(nb + 8, N_MOD, d), "db_mod").reshape(1, n9)
    ds_part = _mm(dmod_my_cols[nb:nb + 8], w_mod[0], "nt", F32, "ds_ctx")
    (g5,) = _all_gather([jnp.concatenate([ds_part[0:1], jnp.zeros((7, d), F32)], axis=0)], "gather_ds_ctx", False)
    ds_ctx = _sum_lead(g5, "sum_ds_ctx")[0]
    sg = jax.nn.sigmoid(c_ctx)
    g_c_ctx = ds_ctx * (sg * (1.0 + c_ctx * (1.0 - sg)))

    a_f, a_b = a_vals[:, :heads], a_vals[:, heads:hh]
    grads = {
        "c_ctx": [g_c_ctx], "w_mod": [g_w_mod], "b_mod": [g_b_mod],
        "norm_ffn1": [g_n1], "norm_mix": [g_nm], "norm_ffn2": [g_n2], "final_norm": [g_fn.reshape(-1)],
        "ssm_conv_w": [lax.dynamic_slice_in_dim(g_cw, me * ssm_conv_w.shape[2], ssm_conv_w.shape[2], axis=1)[None]],
        "ssm_conv_b": [g_cb],
        "dt_bias_fwd": [g_dtb[:, :heads]], "dt_bias_bwd": [g_dtb[:, heads:hh]],
        "a_log_fwd": [g_a[:, :heads] * a_f], "a_log_bwd": [g_a[:, heads:hh] * a_b],
        "ssm_d": [jnp.sum(g_drow.reshape(1, heads, HEAD_DIM), axis=2)], "ssm_norm_w": [g_snw],
        "cconv_w": [lax.dynamic_slice_in_dim(g_ccw, me * cconv_w.shape[2], cconv_w.shape[2], axis=1)[None]],
        "cconv_b": [g_ccb], "cconv_ln_w": [g_lnw], "cconv_ln_b": [g_lnb],
    }
    for n in BIG:
        grads[n] = big_parts(n)

    out_g, out_d, out_m, out_v = [], [], [], []
    for n in names:
        gr, de, nm, nv = _adamw(place, wts[n], grads[n], args["m_" + n], args["v_" + n], "adamw_" + n)
        out_g.append(gr)
        out_d.append(de)
        out_m.append(nm)
        out_v.append(nv)
    return (loss, grad_x.reshape(bl, seq, d), *out_g, *out_d, *out_m, *out_v)
```

```python
import functools
import math

import jax
import jax.numpy as jnp
from jax import lax
from jax.experimental import pallas as pl
from jax.experimental.pallas import tpu as pltpu

F32 = jnp.float32
BF16 = jnp.bfloat16
MESH = pl.DeviceIdType.MESH

N_DEV = 8
N_CHIP_PEERS = 3
HEAD_DIM = 64
N_STATE = 128
SSD_GROUPS = 2
CHUNK = 128
GRID_W = 64
N_MOD = 9
EPS = 1e-6
DT_PAD = 512
DT_LANES = 128
HALO = 8
ROW_TILE = 512
FINE_ROW_TILE = 256
VMEM_LIMIT = 48 * 1024 * 1024
NEG_BIG = -1e30

ADAM_LR = 0.001
ADAM_B1 = 0.9
ADAM_B2 = 0.999
ADAM_EPS = 1e-08
ADAM_WD = 0.01
ADAM_STEP = 10


def _pick(n, prefs):
    for p in prefs:
        if n % p == 0:
            return p
    return n


MM_TILE_CAP = 2816
MM_TILE_ELEMS = 3 << 20
MM_OUT_TILE_ELEMS = 3 << 19
MM_FULL_ROWS = 1024


def _big_tile(n, cap):
    if n <= cap:
        return n
    best = 0
    for t in range(128, cap + 1, 128):
        if n % t == 0:
            best = t
    return best or n


def _cparams(ndim):
    return pltpu.CompilerParams(dimension_semantics=("arbitrary",) * ndim, vmem_limit_bytes=VMEM_LIMIT)


def _silu(v):
    return v * jax.nn.sigmoid(v)


def _mm(a, b, mode, out_dtype, name, tn=None, tm=None, extras=(), epilogue=None, outs=None, ride=None):
    if mode == "tn":
        (K, M), (K2, N) = a.shape, b.shape
    elif mode == "nt":
        (M, K), (N, K2) = a.shape, b.shape
    else:
        (M, K), (K2, N) = a.shape, b.shape
    assert K == K2, (name, a.shape, b.shape)
    tm = tm or (M if M <= MM_FULL_ROWS else None)
    if tn is None:
        tn = _big_tile(N, min(MM_TILE_CAP, max(128, MM_OUT_TILE_ELEMS // (tm or 512))))
    if tm is None:
        tm = _big_tile(M, max(128, MM_OUT_TILE_ELEMS // tn))
    tk = _big_tile(K, min(MM_TILE_CAP, MM_TILE_ELEMS // max(tn, tm)))
    nk = K // tk
    ni, nj = M // tm, N // tn
    swap = nk == 1 and (K * N + M * K * nj) < (M * K + K * N * ni)
    ij = (lambda g0, g1: (g1, g0)) if swap else (lambda g0, g1: (g0, g1))
    if mode == "tn":
        a_spec = pl.BlockSpec((tk, tm), lambda g0, g1, k: (k, ij(g0, g1)[0]))
        dn = (((0,), (0,)), ((), ()))
    else:
        a_spec = pl.BlockSpec((tm, tk), lambda g0, g1, k: (ij(g0, g1)[0], k))
        dn = (((1,), (1,)), ((), ())) if mode == "nt" else (((1,), (0,)), ((), ()))
    if mode == "nt":
        b_spec = pl.BlockSpec((tn, tk), lambda g0, g1, k: (ij(g0, g1)[1], k))
    else:
        b_spec = pl.BlockSpec((tk, tn), lambda g0, g1, k: (k, ij(g0, g1)[1]))
    if outs is None:
        outs = [(tn, out_dtype)]
    nx = len(extras)

    def tile(w):
        return pl.BlockSpec((tm, w), lambda g0, g1, k: ij(g0, g1))

    def extra_spec(item):
        if len(item) == 3:
            return pl.BlockSpec((None,) + tuple(item[0].shape[1:]), lambda g0, g1, k: (item[1](ij(g0, g1)[0]), 0, 0))
        return tile(item[1])

    def finish(acc, refs):
        vals = (acc,) if epilogue is None else epilogue(acc, *[r[...] for r in refs[:nx]])
        for o_ref, v in zip(refs[nx:], vals):
            o_ref[...] = v.astype(o_ref.dtype)

    grid = (nj, ni, nk) if swap else (ni, nj, nk)
    nout = len(outs)
    r_in = len(ride.arrays) if ride else 0
    r_out = len(ride.out_shapes) if ride else 0

    def compute(a_ref, b_ref, refs):
        part = lax.dot_general(a_ref[...].astype(BF16), b_ref[...].astype(BF16), dn, preferred_element_type=F32)
        if nk == 1:
            finish(part, refs)
            return
        acc_ref, k = refs[-1], pl.program_id(2)
        _acc(acc_ref, k == 0, part)

        @pl.when(k == nk - 1)
        def _():
            finish(acc_ref[...], refs[:-1])

    def body(a_ref, b_ref, *refs):
        if ride is None:
            compute(a_ref, b_ref, refs)
            return
        x_refs, rin = refs[:nx], refs[nx:nx + r_in]
        o_refs, rout = refs[nx + r_in:nx + r_in + nout], refs[nx + r_in + nout:nx + r_in + nout + r_out]
        tail = refs[nx + r_in + nout + r_out:]
        nacc = 1 if nk > 1 else 0
        sems = tail[nacc:]
        ids = [pl.program_id(q) for q in range(3)]
        first = functools.reduce(jnp.logical_and, [i == 0 for i in ids])
        last = functools.reduce(jnp.logical_and, [i == n - 1 for i, n in zip(ids, grid)])
        pl.when(first)(lambda: ride.start(rin, rout, *sems))
        compute(a_ref, b_ref, tuple(x_refs) + tuple(o_refs) + tuple(tail[:nacc]))
        pl.when(last)(lambda: ride.finish(rin, rout, *sems))

    hbm = pl.BlockSpec(memory_space=pl.ANY)
    res = pl.pallas_call(
        body, grid=grid, in_specs=[a_spec, b_spec] + [extra_spec(x) for x in extras] + [hbm] * r_in,
        out_specs=[tile(w) for w, _ in outs] + [hbm] * r_out,
        out_shape=[jax.ShapeDtypeStruct((M, nj * w), dt) for w, dt in outs] + (list(ride.out_shapes) if ride else []),
        scratch_shapes=([pltpu.VMEM((tm, tn), F32)] if nk > 1 else []) + (list(ride.sems) if ride else []),
        name=name, compiler_params=_cparams(3),
    )(a, b, *[x[0] for x in extras], *(ride.arrays if ride else []))
    main = res[0] if epilogue is None else res[:nout]
    return (main, res[nout:]) if ride else main


class _Lay:
    def __init__(self, bl, seq, clen, d, tt=None):
        self.bl, self.seq, self.clen, self.d = bl, seq, clen, d
        self.tt = min(ROW_TILE, math.gcd(seq, bl * clen)) if tt is None else tt
        assert seq % self.tt == 0 and (bl * clen) % self.tt == 0 and self.tt % 8 == 0
        self.spb = seq // self.tt
        self.spc = clen // self.tt
        self.nsx = bl * self.spb
        self.nsc = bl * clen // self.tt
        self.ns = self.nsx + self.nsc
        self.tx = bl * seq
        self.ta = self.tx + bl * clen

    def fine(self):
        return _Lay(self.bl, self.seq, self.clen, self.d, min(FINE_ROW_TILE, self.clen))

    def mrow(self, s):
        return jnp.where(s < self.nsx, s // self.spb, self.bl)

    def first_of_row(self, s):
        return jnp.logical_or(jnp.logical_and(s < self.nsx, s % self.spb == 0), s == self.nsx)

    def seq_first(self, s):
        return jnp.where(s < self.nsx, s % self.spb == 0, (s - self.nsx) % self.spc == 0)

    def seq_last(self, s):
        return jnp.where(s < self.nsx, s % self.spb == self.spb - 1, (s - self.nsx) % self.spc == self.spc - 1)


def _tok(lay, c, cb=0, clamp=None):
    if clamp is None:
        return pl.BlockSpec((lay.tt, c), lambda j, s: (s, cb + j))
    return pl.BlockSpec((lay.tt, c), lambda j, s: (jnp.minimum(s, clamp), cb + j))


def _halo_prev(lay, c, cb=0):
    u = lay.tt // HALO
    return pl.BlockSpec((HALO, c), lambda j, s: (jnp.maximum(s * u - 1, 0), cb + j))


def _halo_next(lay, c, cb=0):
    u = lay.tt // HALO
    last = lay.ta // HALO - 1
    return pl.BlockSpec((HALO, c), lambda j, s: (jnp.minimum((s + 1) * u, last), cb + j))


def _row(lay, k, c):
    return pl.BlockSpec((None, k, c), lambda j, s: (lay.mrow(s), 0, 0))


def _glob(k, c, cb=None):
    if cb is None:
        return pl.BlockSpec((k, c), lambda j, s: (0, 0))
    return pl.BlockSpec((k, c), lambda j, s: (0, cb + j))


def _tok_call(name, body, ncb, nseg, in_specs, out_specs, out_shape, inputs, scratch=(), aliases=None):
    return pl.pallas_call(body, grid=(ncb, nseg), in_specs=in_specs, out_specs=out_specs, out_shape=out_shape,
                          scratch_shapes=list(scratch), name=name, compiler_params=_cparams(2),
                          input_output_aliases=aliases or {})(*inputs)


def _acc(ref, first, val):
    @pl.when(first)
    def _():
        ref[...] = val

    @pl.when(jnp.logical_not(first))
    def _():
        ref[...] += val


def _norm_mod_f(x, w, sh, sc):
    y = x * lax.rsqrt(jnp.mean(x * x, axis=-1, keepdims=True) + EPS) * w
    return y * (1.0 + sc) + sh


def _norm_mod_fwd(lay, nseg, x, w, modv, ksh, name):
    d = lay.d

    def body(x_ref, w_ref, m_ref, h_ref):
        h = _norm_mod_f(x_ref[...], w_ref[...], m_ref[ksh:ksh + 1, :], m_ref[ksh + 1:ksh + 2, :])
        h_ref[...] = h.astype(h_ref.dtype)

    return _tok_call(name, body, 1, nseg, [_tok(lay, d), _glob(1, d), _row(lay, N_MOD, d)], _tok(lay, d),
                     jax.ShapeDtypeStruct((nseg * lay.tt, d), BF16), (x, w, modv))


def _norm_mod_bwd(lay, nseg, nres, x, w, modv, ksh, dh, dres, name, nout=None):
    d = lay.d
    nrow = lay.bl + (1 if nseg > lay.nsx else 0)
    nout = nseg if nout is None else nout

    def body(x_ref, w_ref, m_ref, dh_ref, dres_ref, dx_ref, dw_ref, dm_ref):
        s = pl.program_id(1)
        _, vjp = jax.vjp(_norm_mod_f, x_ref[...], w_ref[...], m_ref[ksh:ksh + 1, :], m_ref[ksh + 1:ksh + 2, :])
        dx, dw, dsh, dsc = vjp(dh_ref[...])

        @pl.when(s < nout)
        def _():
            dx_ref[...] = dx + jnp.where(s < nres, dres_ref[...], 0.0)

        _acc(dw_ref, s == 0, dw)
        _acc(dm_ref, lay.first_of_row(s), jnp.concatenate([dsh, dsc], axis=0))

    return _tok_call(
        name, body, 1, nseg,
        [_tok(lay, d), _glob(1, d), _row(lay, N_MOD, d), _tok(lay, d), _tok(lay, d, clamp=nres - 1)],
        [_tok(lay, d, clamp=nout - 1), _glob(1, d), _row(lay, 2, d)],
        [jax.ShapeDtypeStruct((nout * lay.tt, d), F32), jax.ShapeDtypeStruct((1, d), F32),
         jax.ShapeDtypeStruct((nrow, 2, d), F32)],
        (x, w, modv, dh, dres))


def _mm_resid(lay, a, b, x, modv, kg, coef, name, then_norm=None, ride=None):
    d = lay.d
    tm = min(ROW_TILE, math.gcd(lay.seq, lay.bl * lay.clen))
    assert a.shape[0] % tm == 0 and b.shape[1] == d

    def row_of(i):
        return jnp.where(i * tm < lay.tx, (i * tm) // lay.seq, lay.bl)

    def add(acc, x_tile, m_blk, *nw):
        y = x_tile + (coef * m_blk[kg:kg + 1, :]) * acc
        if then_norm is None:
            return acc, y
        k = then_norm[1]
        return acc, y, _norm_mod_f(y, nw[0], m_blk[k:k + 1, :], m_blk[k + 1:k + 2, :])

    extras = [(x, d), (modv, row_of, "rows")]
    outs = [(d, F32), (d, F32)]
    if then_norm is not None:
        extras.append((then_norm[0].reshape(1, 1, d), lambda i: 0, "rows"))
        outs.append((d, BF16))
    return _mm(a, b, "nn", None, name, tm=tm, tn=d, extras=extras, epilogue=add, outs=outs, ride=ride)


def _resid_bwd(lay, nseg, dy, o, modv, kg, coef, name):
    d = lay.d
    nrow = lay.bl + (1 if nseg > lay.nsx else 0)

    def body(dy_ref, o_ref, m_ref, do_ref, dg_ref):
        s = pl.program_id(1)
        dy = dy_ref[...]
        do_ref[...] = (dy * (coef * m_ref[kg:kg + 1, :])).astype(do_ref.dtype)
        _acc(dg_ref, lay.first_of_row(s), jnp.sum(dy * o_ref[...], axis=0, keepdims=True) * coef)

    return _tok_call(name, body, 1, nseg, [_tok(lay, d), _tok(lay, d), _row(lay, N_MOD, d)],
                     [_tok(lay, d), _row(lay, 1, d)],
                     [jax.ShapeDtypeStruct((nseg * lay.tt, d), BF16), jax.ShapeDtypeStruct((nrow, 1, d), F32)],
                     (dy, o, modv))


def _final_loss(lay, x, wf, target, name):
    d = lay.d

    def body(x_ref, w_ref, t_ref, loss_ref, dx_ref, dw_ref):
        s = pl.program_id(1)

        def f(xv, wv):
            return xv * lax.rsqrt(jnp.mean(xv * xv, axis=-1, keepdims=True) + EPS) * wv

        y, vjp = jax.vjp(f, x_ref[...], w_ref[...])
        err = y - t_ref[...]
        part = 0.5 * jnp.sum(jnp.sum(err * err, axis=-1, keepdims=True), axis=0, keepdims=True) / d
        dx, dw = vjp(err / d)
        dx_ref[...] = dx
        _acc(loss_ref, s == 0, part)
        _acc(dw_ref, s == 0, dw)

    return _tok_call(name, body, 1, lay.nsx, [_tok(lay, d), _glob(1, d), _tok(lay, d)],
                     [_glob(1, 1), _tok(lay, d), _glob(1, d)],
                     [jax.ShapeDtypeStruct((1, 1), F32), jax.ShapeDtypeStruct((lay.tx, d), F32),
                      jax.ShapeDtypeStruct((1, d), F32)], (x, wf, target))


class _Mix:
    def __init__(self, d, heads):
        self.d_ssm = d
        self.d_conv = d
        self.heads = heads
        assert heads * HEAD_DIM == d and heads % (2 * SSD_GROUPS) == 0 and 2 * heads <= DT_LANES
        self.gn = SSD_GROUPS * N_STATE
        self.xw = d + 2 * self.gn
        self.off_x = 0
        self.off_dt = self.xw
        self.off_glu = self.xw + DT_PAD
        self.off_z = self.off_glu + 2 * d
        self.pw = self.off_z + d
        assert self.off_z % d == 0
        self.ref_x = d
        self.ref_dt = d + self.xw
        self.ref_glu = self.ref_dt + 2 * heads
        self.cc = self.xw if self.off_x % self.xw == 0 else _pick(self.xw, (512, 256, 128))


def _conv5_fwd(lay, mx, proj, cw, cb, name):
    c, tt = mx.cc, lay.tt
    cb0 = mx.off_x // c
    assert mx.off_x % c == 0

    def body(prev_ref, cur_ref, next_ref, w_ref, b_ref, pre_ref, act_ref, ext_ref):
        s = pl.program_id(1)
        ext_ref[0:HALO, :] = jnp.where(lay.seq_first(s), 0.0, prev_ref[...])
        ext_ref[HALO:HALO + tt, :] = cur_ref[...]
        ext_ref[HALO + tt:, :] = jnp.where(lay.seq_last(s), 0.0, next_ref[...])
        acc = jnp.zeros((tt, c), F32) + b_ref[...]
        for k in range(5):
            acc = acc + w_ref[k:k + 1, :] * ext_ref[pl.ds(HALO + k - 2, tt), :]
        pre_ref[...] = acc
        act_ref[...] = _silu(acc)

    sh = jax.ShapeDtypeStruct((lay.ta, mx.xw), F32)
    return _tok_call(name, body, mx.xw // c, lay.ns,
                     [_halo_prev(lay, c, cb0), _tok(lay, c, cb0), _halo_next(lay, c, cb0), _glob(8, c, 0), _glob(1, c, 0)],
                     [_tok(lay, c), _tok(lay, c)], [sh, sh], (proj, proj, proj, cw, cb),
                     scratch=[pltpu.VMEM((tt + 2 * HALO, c), F32)])


def _conv5_bwd(lay, mx, proj, pre, dact_f, dact_b, cw, dproj, name):
    c, tt = mx.cc, lay.tt
    cb0 = mx.off_x // c

    def dsilu(p):
        sg = jax.nn.sigmoid(p)
        return sg * (1.0 + p * (1.0 - sg))

    def body(xp_ref, xc_ref, xn_ref, pp_ref, pc_ref, pn_ref, fp_ref, fc_ref, fn_ref, bp_ref, bc_ref, bn_ref, w_ref,
             buf_ref, dx_ref, dw_ref, db_ref, extx_ref, extd_ref):
        s = pl.program_id(1)
        first, last = lay.seq_first(s), lay.seq_last(s)
        dcur = (fc_ref[...] + bc_ref[...]) * dsilu(pc_ref[...])
        extd_ref[0:HALO, :] = jnp.where(first, 0.0, (fp_ref[...] + bp_ref[...]) * dsilu(pp_ref[...]))
        extd_ref[HALO:HALO + tt, :] = dcur
        extd_ref[HALO + tt:, :] = jnp.where(last, 0.0, (fn_ref[...] + bn_ref[...]) * dsilu(pn_ref[...]))
        extx_ref[0:HALO, :] = jnp.where(first, 0.0, xp_ref[...])
        extx_ref[HALO:HALO + tt, :] = xc_ref[...]
        extx_ref[HALO + tt:, :] = jnp.where(last, 0.0, xn_ref[...])
        dx = jnp.zeros((tt, c), F32)
        rows = []
        for k in range(5):
            dx = dx + w_ref[k:k + 1, :] * extd_ref[pl.ds(HALO - (k - 2), tt), :]
            rows.append(jnp.sum(dcur * extx_ref[pl.ds(HALO + k - 2, tt), :], axis=0, keepdims=True))
        dx_ref[...] = dx.astype(dx_ref.dtype)
        rows.append(jnp.zeros((3, c), F32))
        _acc(dw_ref, s == 0, jnp.concatenate(rows, axis=0))
        _acc(db_ref, s == 0, jnp.sum(dcur, axis=0, keepdims=True))

    three = lambda cbx: [_halo_prev(lay, c, cbx), _tok(lay, c, cbx), _halo_next(lay, c, cbx)]
    ext = pltpu.VMEM((tt + 2 * HALO, c), F32)
    return _tok_call(name, body, mx.xw // c, lay.ns,
                     three(cb0) + three(0) + three(0) + three(0) + [_glob(8, c, 0), pl.BlockSpec(memory_space=pl.ANY)],
                     [_tok(lay, c, cb0), _glob(8, c, 0), _glob(1, c, 0)],
                     [jax.ShapeDtypeStruct(dproj.shape, dproj.dtype), jax.ShapeDtypeStruct((8, mx.xw), F32),
                      jax.ShapeDtypeStruct((1, mx.xw), F32)],
                     (proj, proj, proj, pre, pre, pre, dact_f, dact_f, dact_f, dact_b, dact_b, dact_b, cw, dproj),
                     scratch=[ext, ext], aliases={13: 0})


def _softplus(v):
    return jnp.maximum(v, 0.0) + jnp.log1p(jnp.exp(-jnp.abs(v)))


def _dt_fwd(lay, mx, proj, bias, name):
    cb = mx.off_dt // DT_LANES

    def body(p_ref, b_ref, dt_ref):
        dt_ref[...] = _softplus(p_ref[...] + b_ref[...])

    return _tok_call(name, body, 1, lay.ns, [_tok(lay, DT_LANES, cb), _glob(1, DT_LANES)], _tok(lay, DT_LANES),
                     jax.ShapeDtypeStruct((lay.ta, DT_LANES), F32), (proj, bias))


def _dt_bwd(lay, mx, proj, bias, parts, dproj, name):
    cb = mx.off_dt // DT_LANES
    wb = DT_PAD if mx.off_dt % DT_PAD == 0 else DT_LANES
    ncb = DT_PAD // wb

    def body(p_ref, b_ref, a_ref, b2_ref, c_ref, d_ref, buf_ref, dp_ref, db_ref):
        j, s = pl.program_id(0), pl.program_id(1)

        @pl.when(j == 0)
        def _():
            ddt = (a_ref[...] + b2_ref[...]) + (c_ref[...] + d_ref[...])
            draw = ddt * jax.nn.sigmoid(p_ref[...] + b_ref[...])
            dp_ref[:, 0:DT_LANES] = draw.astype(dp_ref.dtype)
            if wb > DT_LANES:
                dp_ref[:, DT_LANES:] = jnp.zeros((lay.tt, wb - DT_LANES), dp_ref.dtype)
            _acc(db_ref, s == 0, jnp.sum(draw, axis=0, keepdims=True))

        @pl.when(j > 0)
        def _():
            dp_ref[...] = jnp.zeros_like(dp_ref)

    t = pl.BlockSpec((lay.tt, DT_LANES), lambda j, s: (s, 0))
    return _tok_call(name, body, ncb, lay.ns,
                     [pl.BlockSpec((lay.tt, DT_LANES), lambda j, s: (s, cb)), _glob(1, DT_LANES), t, t, t, t,
                      pl.BlockSpec(memory_space=pl.ANY)],
                     [_tok(lay, wb, mx.off_dt // wb), _glob(1, DT_LANES)],
                     [jax.ShapeDtypeStruct(dproj.shape, dproj.dtype), jax.ShapeDtypeStruct((1, DT_LANES), F32)],
                     (proj, bias) + tuple(parts) + (dproj,), aliases={6: 0})


def _scan_mask(rev):
    r = lax.broadcasted_iota(jnp.int32, (CHUNK, CHUNK), 0)
    c = lax.broadcasted_iota(jnp.int32, (CHUNK, CHUNK), 1)
    return (r <= c) if rev else (r >= c)


def _split_bf16(x):
    hi = x.astype(BF16)
    return hi, (x - hi.astype(F32)).astype(BF16)


@functools.partial(jax.custom_vjp, nondiff_argnums=(0,))
def _cum_cols(rev, x):
    m = _scan_mask(rev).astype(BF16)
    hi, lo = _split_bf16(x)
    return jnp.dot(m, hi, preferred_element_type=F32) + jnp.dot(m, lo, preferred_element_type=F32)


_cum_cols.defvjp(lambda rev, x: (_cum_cols(rev, x), None), lambda rev, _, g: (_cum_cols(not rev, g),))


@functools.partial(jax.custom_vjp, nondiff_argnums=(0,))
def _cum_rows(rev, x):
    m = _scan_mask(not rev).astype(BF16)
    hi, lo = _split_bf16(x)
    return jnp.dot(hi, m, preferred_element_type=F32) + jnp.dot(lo, m, preferred_element_type=F32)


_cum_rows.defvjp(lambda rev, x: (_cum_rows(rev, x), None), lambda rev, _, g: (_cum_rows(not rev, g),))


def _ssd_chunk(xh_pairs, bcs, ccs, dtc, dtr, a_row, a_col, st_pairs, *, rev, heads, col0):
    cs_c, cs_r, tot, scores = _ssd_shared(bcs, ccs, dtc, dtr, a_row, a_col, rev=rev)
    ppg = heads // (2 * SSD_GROUPS)
    ys, sts = [], []
    for g in range(SSD_GROUPS):
        y, st = _ssd_group(xh_pairs[g * ppg:(g + 1) * ppg], bcs[g], ccs[g], st_pairs[g], cs_c, cs_r, tot, dtc,
                           scores[g], rev=rev, col=col0 + 2 * ppg * g)
        ys.append(y)
        sts.append(st)
    return ys, sts


_NT = (((1,), (1,)), ((), ()))
_TN = (((0,), (0,)), ((), ()))


def _ssd_shared(bcs, ccs, dtc, dtr, a_row, a_col, *, rev):
    da_c = dtc * a_row
    cs_c = _cum_cols(rev, da_c)
    cs_r = _cum_rows(rev, dtr * a_col)
    tot = jnp.sum(da_c, axis=0, keepdims=True)
    scores = [lax.dot_general(ccs[g].astype(BF16), bcs[g].astype(BF16), _NT, preferred_element_type=F32)
              for g in range(SSD_GROUPS)]
    return cs_c, cs_r, tot, scores


def _ssd_group(xh_pairs, bc, cc, st, cs_c, cs_r, tot, dtc, score, *, rev, col):
    n = CHUNK
    mask = _scan_mask(rev)
    lane = lax.broadcasted_iota(jnp.int32, (n, DT_LANES), 1)
    sub = lax.broadcasted_iota(jnp.int32, (DT_LANES, n), 0)
    lane1 = lax.broadcasted_iota(jnp.int32, (1, DT_LANES), 1)
    left = lax.broadcasted_iota(jnp.int32, (n, 2 * HEAD_DIM), 1) < HEAD_DIM
    top = lax.broadcasted_iota(jnp.int32, (2 * HEAD_DIM, 1), 0) < HEAD_DIM
    xs_all, wst_all, ecs_all, edec_all, y_diag = [], [], [], [], []
    for p, xh in enumerate(xh_pairs):
        per = []
        for c in (col + 2 * p, col + 2 * p + 1):
            csv = jnp.sum(jnp.where(lane == c, cs_c, 0.0), axis=1, keepdims=True)
            csr = jnp.sum(jnp.where(sub == c, cs_r, 0.0), axis=0, keepdims=True)
            dtv = jnp.sum(jnp.where(lane == c, dtc, 0.0), axis=1, keepdims=True)
            tv = jnp.sum(jnp.where(lane1 == c, tot, 0.0), axis=1, keepdims=True)
            m = score * jnp.exp(jnp.where(mask, csv - csr, NEG_BIG))
            per.append((csv, dtv, tv, m))
        (cs1, dt1, t1, m1), (cs2, dt2, t2, m2) = per
        xs = xh * jnp.where(left, dt1, dt2)
        both = jnp.dot(jnp.concatenate([m1, m2], axis=0).astype(BF16), xs.astype(BF16), preferred_element_type=F32)
        y_diag.append(jnp.where(left, both[:n], both[n:]))
        xs_all.append(xs)
        ecs_all.append(jnp.where(left, jnp.exp(cs1), jnp.exp(cs2)))
        wst_all.append(jnp.where(left, jnp.exp(t1 - cs1), jnp.exp(t2 - cs2)))
        edec_all.append(jnp.where(top, jnp.exp(t1), jnp.exp(t2)))
    cat = lambda parts, axis: parts[0] if len(parts) == 1 else jnp.concatenate(parts, axis=axis)
    xs, wst, ecs = cat(xs_all, 1), cat(wst_all, 1), cat(ecs_all, 1)
    y_off = lax.dot_general(cc.astype(BF16), st.astype(BF16), _NT, preferred_element_type=F32) * ecs
    cst = lax.dot_general((xs * wst).astype(BF16), bc.astype(BF16), _TN, preferred_element_type=F32)
    return cat(y_diag, 1) + y_off, st * cat(edec_all, 0) + cst


class _Scan:
    def __init__(self, lay, rev):
        self.ncx, self.ncc, self.bl, self.rev = lay.seq // CHUNK, lay.clen // CHUNK, lay.bl, rev
        self.nct = self.ncx + self.ncc

    def chunk(self, b, pos):
        kc = (self.ncc - 1 - pos) if self.rev else pos
        kx = (self.ncx - 1 - (pos - self.ncc)) if self.rev else (pos - self.ncc)
        return jnp.where(pos < self.ncc, self.bl * self.ncx + b * self.ncc + kc, b * self.ncx + kx)


def _ssd_io(mx, x_ref, st_src):
    np_ = mx.heads // 2
    d = mx.d_ssm
    xh = [x_ref[:, 128 * p:128 * (p + 1)] for p in range(np_)]
    bcs = [x_ref[:, d + N_STATE * g:d + N_STATE * (g + 1)] for g in range(SSD_GROUPS)]
    ccs = [x_ref[:, d + mx.gn + N_STATE * g:d + mx.gn + N_STATE * (g + 1)] for g in range(SSD_GROUPS)]
    gw = d // SSD_GROUPS
    sts = [st_src[gw * g:gw * (g + 1), :] for g in range(SSD_GROUPS)]
    return xh, bcs, ccs, sts


def _ssd_fwd(lay, mx, xbc, dt, dtt, a_row, a_col, rev, name):
    sc = _Scan(lay, rev)
    col0 = mx.heads if rev else 0
    hp = mx.heads * HEAD_DIM

    def body(x_ref, dt_ref, dtt_ref, ar_ref, ac_ref, y_ref, hp_ref, st_ref):
        @pl.when(pl.program_id(1) == 0)
        def _():
            st_ref[...] = jnp.zeros_like(st_ref)

        hp_ref[...] = st_ref[...]
        xh, bcs, ccs, sts = _ssd_io(mx, x_ref, st_ref)
        ys, new = _ssd_chunk(xh, bcs, ccs, dt_ref[...], dtt_ref[...], ar_ref[...], ac_ref[...], sts,
                             rev=rev, heads=mx.heads, col0=col0)
        gw = mx.d_ssm // SSD_GROUPS
        for g in range(SSD_GROUPS):
            y_ref[:, gw * g:gw * (g + 1)] = ys[g]
            st_ref[gw * g:gw * (g + 1), :] = new[g]

    ch = sc.chunk
    return pl.pallas_call(
        body, grid=(lay.bl, sc.nct),
        in_specs=[pl.BlockSpec((CHUNK, mx.xw), lambda b, i: (ch(b, i), 0)),
                  pl.BlockSpec((CHUNK, DT_LANES), lambda b, i: (ch(b, i), 0)),
                  pl.BlockSpec((DT_LANES, CHUNK), lambda b, i: (0, ch(b, i))),
                  pl.BlockSpec((1, DT_LANES), lambda b, i: (0, 0)),
                  pl.BlockSpec((DT_LANES, 1), lambda b, i: (0, 0))],
        out_specs=[pl.BlockSpec((CHUNK, mx.d_ssm), lambda b, i: (ch(b, i), 0)),
                   pl.BlockSpec((hp, N_STATE), lambda b, i: (b * sc.nct + i, 0))],
        out_shape=[jax.ShapeDtypeStruct((lay.ta, mx.d_ssm), F32),
                   jax.ShapeDtypeStruct((lay.bl * sc.nct * hp, N_STATE), F32)],
        scratch_shapes=[pltpu.VMEM((hp, N_STATE), F32)], name=name, compiler_params=_cparams(2),
    )(xbc, dt, dtt, a_row, a_col)


def _ssd_bwd(lay, mx, xbc, dt, dtt, a_row, a_col, hprev, dy, dskip, rev, name):
    sc = _Scan(lay, rev)
    col0 = mx.heads if rev else 0
    hp = mx.heads * HEAD_DIM
    np_ = mx.heads // 2
    d = mx.d_ssm
    with_skip = dskip is not None

    def body(*refs):
        if with_skip:
            x_ref, dt_ref, dtt_ref, ar_ref, ac_ref, hp_ref, dy_ref, sk_ref = refs[:8]
            rest = refs[8:]
        else:
            x_ref, dt_ref, dtt_ref, ar_ref, ac_ref, hp_ref, dy_ref = refs[:7]
            rest = refs[7:]
        dx_ref, ddc_ref, ddr_ref, dar_ref, dac_ref, ds_ref = rest
        b, i = pl.program_id(0), pl.program_id(1)

        @pl.when(i == 0)
        def _():
            ds_ref[...] = jnp.zeros_like(ds_ref)

        xh, bcs, ccs, sts = _ssd_io(mx, x_ref, hp_ref)
        dtc = dt_ref[...]
        shared, vjp_shared = jax.vjp(functools.partial(_ssd_shared, rev=rev), bcs, ccs, dtc, dtt_ref[...],
                                     ar_ref[...], ac_ref[...])
        cs_c, cs_r, tot, scores = shared
        plus = lambda acc, v: v if acc is None else acc + v
        d_cs_c = d_cs_r = d_tot = ddc = None
        d_scores, dbc, dcc = [], [], []
        ppg = np_ // SSD_GROUPS
        gw = d // SSD_GROUPS
        for g in range(SSD_GROUPS):
            dyg = dy_ref[:, gw * g:gw * (g + 1)]
            fn = functools.partial(_ssd_group, rev=rev, col=col0 + 2 * ppg * g)
            _, vjp = jax.vjp(fn, xh[g * ppg:(g + 1) * ppg], bcs[g], ccs[g], sts[g], cs_c, cs_r, tot, dtc, scores[g])
            dxh, dbc_g, dcc_g, dst, dcs_c_g, dcs_r_g, dtot_g, ddc_g, dsc_g = vjp((dyg, ds_ref[gw * g:gw * (g + 1), :]))
            for q in range(ppg):
                p = g * ppg + q
                v = dxh[q]
                if with_skip:
                    v = v + dyg[:, 128 * q:128 * (q + 1)] * sk_ref[:, 128 * p:128 * (p + 1)]
                dx_ref[:, 128 * p:128 * (p + 1)] = v
            ds_ref[gw * g:gw * (g + 1), :] = dst
            d_cs_c, d_cs_r, d_tot, ddc = plus(d_cs_c, dcs_c_g), plus(d_cs_r, dcs_r_g), plus(d_tot, dtot_g), plus(ddc, ddc_g)
            d_scores.append(dsc_g)
            dbc.append(dbc_g)
            dcc.append(dcc_g)
        dbc_s, dcc_s, ddc_s, ddr, dar, dac = vjp_shared((d_cs_c, d_cs_r, d_tot, d_scores))
        ddc = ddc + ddc_s
        dbc = [dbc[g] + dbc_s[g] for g in range(SSD_GROUPS)]
        dcc = [dcc[g] + dcc_s[g] for g in range(SSD_GROUPS)]
        for g in range(SSD_GROUPS):
            dx_ref[:, d + N_STATE * g:d + N_STATE * (g + 1)] = dbc[g]
            dx_ref[:, d + mx.gn + N_STATE * g:d + mx.gn + N_STATE * (g + 1)] = dcc[g]
        ddc_ref[...] = ddc
        ddr_ref[...] = ddr
        first = jnp.logical_and(b == 0, i == 0)
        _acc(dar_ref, first, dar)
        _acc(dac_ref, first, dac)

    ch = lambda b, i: sc.chunk(b, sc.nct - 1 - i)
    in_specs = [pl.BlockSpec((CHUNK, mx.xw), lambda b, i: (ch(b, i), 0)),
                pl.BlockSpec((CHUNK, DT_LANES), lambda b, i: (ch(b, i), 0)),
                pl.BlockSpec((DT_LANES, CHUNK), lambda b, i: (0, ch(b, i))),
                pl.BlockSpec((1, DT_LANES), lambda b, i: (0, 0)),
                pl.BlockSpec((DT_LANES, 1), lambda b, i: (0, 0)),
                pl.BlockSpec((hp, N_STATE), lambda b, i: (b * sc.nct + sc.nct - 1 - i, 0)),
                pl.BlockSpec((CHUNK, d), lambda b, i: (ch(b, i), 0))]
    inputs = [xbc, dt, dtt, a_row, a_col, hprev, dy]
    if with_skip:
        in_specs.append(pl.BlockSpec((1, d), lambda b, i: (0, 0)))
        inputs.append(dskip)
    return pl.pallas_call(
        body, grid=(lay.bl, sc.nct), in_specs=in_specs,
        out_specs=[pl.BlockSpec((CHUNK, mx.xw), lambda b, i: (ch(b, i), 0)),
                   pl.BlockSpec((CHUNK, DT_LANES), lambda b, i: (ch(b, i), 0)),
                   pl.BlockSpec((DT_LANES, CHUNK), lambda b, i: (0, ch(b, i))),
                   pl.BlockSpec((1, DT_LANES), lambda b, i: (0, 0)),
                   pl.BlockSpec((DT_LANES, 1), lambda b, i: (0, 0))],
        out_shape=[jax.ShapeDtypeStruct((lay.ta, mx.xw), F32), jax.ShapeDtypeStruct((lay.ta, DT_LANES), F32),
                   jax.ShapeDtypeStruct((DT_LANES, lay.ta), F32), jax.ShapeDtypeStruct((1, DT_LANES), F32),
                   jax.ShapeDtypeStruct((DT_LANES, 1), F32)],
        scratch_shapes=[pltpu.VMEM((hp, N_STATE), F32)], name=name, compiler_params=_cparams(2),
    )(*inputs)


def _gate_f(yf, yb, xh, z, drow, nw):
    dd = yf.shape[-1]
    half = dd // SSD_GROUPS
    yz = (yf + yb + drow * xh) * _silu(z)
    lo = lax.broadcasted_iota(jnp.int32, yz.shape, 1) < half
    sq = yz * yz
    ms1 = jnp.sum(jnp.where(lo, sq, 0.0), axis=-1, keepdims=True) / half
    ms2 = jnp.sum(jnp.where(lo, 0.0, sq), axis=-1, keepdims=True) / half
    return yz * jnp.where(lo, lax.rsqrt(ms1 + EPS), lax.rsqrt(ms2 + EPS)) * nw


def _gate_fwd(lay, mx, yf, yb, xbc, proj, drow, nw, name):
    d = mx.d_ssm

    def body(yf_ref, yb_ref, xh_ref, z_ref, d_ref, w_ref, o_ref):
        o_ref[...] = _gate_f(yf_ref[...], yb_ref[...], xh_ref[...], z_ref[...], d_ref[...], w_ref[...]).astype(o_ref.dtype)

    t, z = _tok(lay, d), _tok(lay, d, mx.off_z // d)
    return _tok_call(name, body, 1, lay.nsx, [t, t, t, z, _glob(1, d), _glob(1, d)], t,
                     jax.ShapeDtypeStruct((lay.tx, d + mx.d_conv), BF16), (yf, yb, xbc, proj, drow, nw))


def _gate_bwd(lay, mx, yf, yb, xbc, proj, drow, nw, dcat, name):
    d = mx.d_ssm
    nsx = lay.nsx

    def body(yf_ref, yb_ref, xh_ref, z_ref, d_ref, w_ref, dc_ref, dy_ref, dz_ref, dd_ref, dw_ref):
        s = pl.program_id(1)

        @pl.when(s < nsx)
        def _():
            _, vjp = jax.vjp(_gate_f, yf_ref[...], yb_ref[...], xh_ref[...], z_ref[...], d_ref[...], w_ref[...])
            dyf, _, _, dz, dd, dw = vjp(dc_ref[...])
            dy_ref[...] = dyf
            dz_ref[...] = dz.astype(dz_ref.dtype)
            _acc(dd_ref, s == 0, dd)
            _acc(dw_ref, s == 0, dw)

        @pl.when(s >= nsx)
        def _():
            dy_ref[...] = jnp.zeros_like(dy_ref)
            dz_ref[...] = jnp.zeros_like(dz_ref)

    t, z = _tok(lay, d), _tok(lay, d, mx.off_z // d)
    return _tok_call(name, body, 1, lay.ns, [t, t, t, z, _glob(1, d), _glob(1, d), _tok(lay, d, clamp=nsx - 1)],
                     [t, z, _glob(1, d), _glob(1, d)],
                     [jax.ShapeDtypeStruct((lay.ta, d), F32), jax.ShapeDtypeStruct((lay.ta, mx.pw), BF16),
                      jax.ShapeDtypeStruct((1, d), F32), jax.ShapeDtypeStruct((1, d), F32)],
                     (yf, yb, xbc, proj, drow, nw, dcat))


def _glu_fwd(lay, mx, proj, name):
    d = mx.d_conv
    c = math.gcd(mx.off_glu, d)
    cb = mx.off_glu // c

    def body(a_ref, b_ref, o_ref):
        o_ref[...] = a_ref[...] * jax.nn.sigmoid(b_ref[...])

    return _tok_call(name, body, d // c, lay.nsx, [_tok(lay, c, cb), _tok(lay, c, cb + d // c)], _tok(lay, c),
                     jax.ShapeDtypeStruct((lay.tx, d), F32), (proj, proj))


def _glu_bwd(lay, mx, proj, du, dproj, name):
    d = mx.d_conv
    nsx = lay.nsx
    whole = mx.off_glu % (2 * d) == 0
    c = d if whole else math.gcd(mx.off_glu, d)
    cb = mx.off_glu // c
    nc = d // c

    def body(a_ref, b_ref, du_ref, buf_ref, o_ref):
        j, s = pl.program_id(0), pl.program_id(1)

        @pl.when(s < nsx)
        def _():
            sg = jax.nn.sigmoid(b_ref[...])
            da = du_ref[...] * sg
            db = da * a_ref[...] * (1.0 - sg)
            if whole:
                o_ref[:, 0:d] = da.astype(o_ref.dtype)
                o_ref[:, d:] = db.astype(o_ref.dtype)
            else:
                o_ref[...] = jnp.where(j < nc, da, db).astype(o_ref.dtype)

        @pl.when(s >= nsx)
        def _():
            o_ref[...] = jnp.zeros_like(o_ref)

    win = lambda half: pl.BlockSpec((lay.tt, c), lambda j, s: (s, cb + half * nc + j % nc))
    out = _tok(lay, 2 * d, mx.off_glu // (2 * d)) if whole else _tok(lay, c, cb)
    return _tok_call(name, body, 1 if whole else 2 * nc, lay.ns,
                     [win(0), win(1), pl.BlockSpec((lay.tt, c), lambda j, s: (jnp.minimum(s, nsx - 1), j % nc)),
                      pl.BlockSpec(memory_space=pl.ANY)],
                     out, jax.ShapeDtypeStruct(dproj.shape, dproj.dtype), (proj, proj, du, dproj), aliases={3: 0})


def _axial(lay, mx, u, dy, cw, cb, name):
    d, seq = mx.d_conv, lay.seq
    kw = cw.shape[0]
    pad = kw // 2
    c = _pick(d // 2, (256, 128))
    ncb = d // c
    zpad = GRID_W * pad
    zpad = -(-zpad // 8) * 8
    backward = dy is not None

    def shifted(ext_ref, off):
        return ext_ref[pl.ds(zpad + off, seq), :]

    def valid_row(off):
        col = lax.broadcasted_iota(jnp.int32, (seq, c), 0) % GRID_W
        return jnp.logical_and(col + off >= 0, col + off < GRID_W)

    def fill(ext_ref, v):
        ext_ref[0:zpad, :] = jnp.zeros((zpad, c), F32)
        ext_ref[zpad:zpad + seq, :] = v
        ext_ref[zpad + seq:, :] = jnp.zeros((zpad, c), F32)

    def conv(ext_ref, w_ref, is_row, sign):
        acc = jnp.zeros((seq, c), F32)
        for k in range(kw):
            off = sign * ((k - pad) if is_row else GRID_W * (k - pad))
            v = shifted(ext_ref, off)
            if is_row:
                v = jnp.where(valid_row(off), v, 0.0)
            acc = acc + w_ref[k:k + 1, :] * v
        return acc

    def fwd_body(u_ref, w_ref, b_ref, o_ref, ext_ref):
        j = pl.program_id(0)
        fill(ext_ref, u_ref[...])

        @pl.when(j < ncb // 2)
        def _():
            o_ref[...] = conv(ext_ref, w_ref, True, 1) + b_ref[...]

        @pl.when(j >= ncb // 2)
        def _():
            o_ref[...] = conv(ext_ref, w_ref, False, 1) + b_ref[...]

    def bwd_body(u_ref, dy_ref, w_ref, du_ref, dw_ref, db_ref, extu_ref, extd_ref):
        j, b = pl.program_id(0), pl.program_id(1)
        dyv = dy_ref[...]
        fill(extu_ref, u_ref[...])
        fill(extd_ref, dyv)

        def grads(is_row):
            du_ref[...] = conv(extd_ref, w_ref, is_row, -1)
            rows = []
            for k in range(kw):
                off = (k - pad) if is_row else GRID_W * (k - pad)
                v = shifted(extu_ref, off)
                if is_row:
                    v = jnp.where(valid_row(off), v, 0.0)
                rows.append(jnp.sum(dyv * v, axis=0, keepdims=True))
            _acc(dw_ref, b == 0, jnp.concatenate(rows, axis=0))

        @pl.when(j < ncb // 2)
        def _():
            grads(True)

        @pl.when(j >= ncb // 2)
        def _():
            grads(False)

        _acc(db_ref, b == 0, jnp.sum(dyv, axis=0, keepdims=True))

    seq_spec = pl.BlockSpec((seq, c), lambda j, b: (b, j))
    w_spec = pl.BlockSpec((kw, c), lambda j, b: (0, j))
    b_spec = pl.BlockSpec((1, c), lambda j, b: (0, j))
    ext = pltpu.VMEM((seq + 2 * zpad, c), F32)
    if not backward:
        return pl.pallas_call(fwd_body, grid=(ncb, lay.bl), in_specs=[seq_spec, w_spec, b_spec], out_specs=seq_spec,
                              out_shape=jax.ShapeDtypeStruct((lay.tx, d), F32), scratch_shapes=[ext], name=name,
                              compiler_params=_cparams(2))(u, cw, cb)
    return pl.pallas_call(bwd_body, grid=(ncb, lay.bl), in_specs=[seq_spec, seq_spec, w_spec],
                          out_specs=[seq_spec, w_spec, b_spec],
                          out_shape=[jax.ShapeDtypeStruct((lay.tx, d), F32), jax.ShapeDtypeStruct((kw, d), F32),
                                     jax.ShapeDtypeStruct((1, d), F32)],
                          scratch_shapes=[ext, ext], name=name, compiler_params=_cparams(2))(u, dy, cw)


def _ln_silu_f(u, w, b):
    mu = jnp.mean(u, axis=-1, keepdims=True)
    var = jnp.mean(jnp.square(u - mu), axis=-1, keepdims=True)
    return _silu((u - mu) * lax.rsqrt(var + EPS) * w + b)


def _ln_fwd(lay, mx, u, w, b, cat, name):
    d = mx.d_conv
    assert mx.d_ssm % d == 0

    def body(u_ref, w_ref, b_ref, cat_ref, o_ref):
        o_ref[...] = _ln_silu_f(u_ref[...], w_ref[...], b_ref[...]).astype(o_ref.dtype)

    return _tok_call(name, body, 1, lay.nsx,
                     [_tok(lay, d), _glob(1, d), _glob(1, d), pl.BlockSpec(memory_space=pl.ANY)],
                     _tok(lay, d, mx.d_ssm // d), jax.ShapeDtypeStruct(cat.shape, cat.dtype), (u, w, b, cat),
                     aliases={3: 0})


def _ln_bwd(lay, mx, u, w, b, dcat, name):
    d = mx.d_conv

    def body(u_ref, w_ref, b_ref, dc_ref, du_ref, dw_ref, db_ref):
        s = pl.program_id(1)
        _, vjp = jax.vjp(_ln_silu_f, u_ref[...], w_ref[...], b_ref[...])
        du, dw, db = vjp(dc_ref[...])
        du_ref[...] = du
        _acc(dw_ref, s == 0, dw)
        _acc(db_ref, s == 0, db)

    return _tok_call(name, body, 1, lay.nsx, [_tok(lay, d), _glob(1, d), _glob(1, d), _tok(lay, d, 1)],
                     [_tok(lay, d), _glob(1, d), _glob(1, d)],
                     [jax.ShapeDtypeStruct((lay.tx, d), F32), jax.ShapeDtypeStruct((1, d), F32),
                      jax.ShapeDtypeStruct((1, d), F32)], (u, w, b, dcat))


def _gate_tile(dff):
    return dff // 2 if (dff // 2) % 128 == 0 else dff


def _ffn_fwd(lay, nseg, x, nw, modv, k0, wts, tag, ride=None, ride_down=None, h=None, then_norm=None):
    wgu, wd, ft = wts
    if h is None:
        h = _norm_mod_fwd(lay, nseg, x, nw, modv, k0, tag + "_norm")
    t = h.shape[0]

    def act(acc):
        g, u = acc[:, :ft], acc[:, ft:]
        sg = jax.nn.sigmoid(g)
        sl = g * sg
        return jnp.concatenate([u * (sg * (1.0 + g * (1.0 - sg))), sl], axis=1), sl * u

    res = _mm(h, wgu, "nn", None, tag + "_gu", tn=2 * ft, tm=256 if t % 256 == 0 else None,
              epilogue=act, outs=[(2 * ft, BF16), (ft, BF16)], ride=ride)
    (s, a), rode = res if ride else (res, None)
    if callable(wd):
        wd = wd(rode)
    res = _mm_resid(lay, a, wd, x, modv, k0 + 2, 0.5, tag + "_down", then_norm=then_norm, ride=ride_down)
    (o, y, *h_next), rode_down = res if ride_down else (res, None)
    return y, (x, h, s, a, o), (rode, rode_down), (h_next[0] if h_next else None)


def _ffn_bwd(lay, nseg, dy, saved, nw, modv, k0, wts, tag, nout=None, hooks=None):
    wgu, wd, ft = wts
    x, h, s, a, o = saved
    do, dgate = _resid_bwd(lay, nseg, dy, o, modv, k0 + 2, 0.5, tag + "_dres")

    def through_act(da, s_tile):
        return (jnp.concatenate([da, da], axis=1) * s_tile.astype(F32),)

    res = _mm(do, wd.T, "nn", None, tag + "_da", tn=ft, tm=_big_tile(do.shape[0], 1024), extras=[(s, 2 * ft)],
              epilogue=through_act, outs=[(2 * ft, BF16)], ride=hooks.during_da() if hooks else None)
    if hooks:
        hooks.take_during_da(res[1])
        res = res[0]
    (dgu,) = res
    dwd = _mm(a, do, "tn", F32, tag + "_dwd")
    if hooks is None:
        dwgu = _mm(h, dgu, "tn", F32, tag + "_dwgu")
        dh = _mm(dgu, wgu, "nt", F32, tag + "_dh")
    else:
        dwgu, rode_a = _mm(h, dgu, "tn", F32, tag + "_dwgu", ride=hooks.after_dwd(dwd))
        dh, rode_b = _mm(dgu, wgu, "nt", F32, tag + "_dh", ride=hooks.after_dwgu(dwgu))
        hooks.take(rode_a, rode_b)
    dx, dnw, dss = _norm_mod_bwd(lay, nseg, nseg, x, nw, modv, k0, dh, dy, tag + "_dnorm", nout=nout)
    return dx, (dwgu, dwd), dnw, jnp.concatenate([dss, dgate], axis=1)


def _local_step(lay, mx, xa, target, modv, w, exch=None):
    d, bl = lay.d, lay.bl
    g = {}
    xa1, ffn1, late, ha = _ffn_fwd(lay, lay.ns, xa, w["norm_ffn1"], modv, 0, w["ffn1"], "ffn1",
                                   ride=exch.late_a if exch else None, ride_down=exch.late_b if exch else None,
                                   then_norm=(w["norm_mix"], 3))
    if exch:
        w = {**w, **exch.unpack_late(w["ffn1"], *late)}
    proj = _mm(ha, w["w_in"], "nn", F32, "mix_in")
    pre, xbc = _conv5_fwd(lay.fine(), mx, proj, w["conv_w"], w["conv_b"], "mix_conv")
    dt = _dt_fwd(lay, mx, proj, w["dt_bias"], "mix_dt")
    dtt = dt.T
    yf, hpf = _ssd_fwd(lay, mx, xbc, dt, dtt, w["a_row"], w["a_col"], False, "ssd_f")
    yb, hpb = _ssd_fwd(lay, mx, xbc, dt, dtt, w["a_row"], w["a_col"], True, "ssd_b")
    cat_y = _gate_fwd(lay, mx, yf, yb, xbc, proj, w["d_row"], w["ssm_norm_w"], "mix_gate")
    u0 = _glu_fwd(lay, mx, proj, "mix_glu")
    uc = _axial(lay, mx, u0, None, w["cconv_w"], w["cconv_b"], "mix_axial")
    cat = _ln_fwd(lay, mx, uc, w["ln_w"], w["ln_b"], cat_y, "mix_ln")
    mix, x2, h2 = _mm_resid(lay, cat, w["w_out"], xa1, modv, 5, 1.0, "mix_out", then_norm=(w["norm_ffn2"], 6))
    x3, ffn2, _, _ = _ffn_fwd(lay, lay.nsx, x2, w["norm_ffn2"], modv, 6, w["ffn2"], "ffn2", h=h2)
    loss, dx3, g["final_norm"] = _final_loss(lay, x3, w["final_norm"], target, "loss")
    dx2, g["ffn2"], g["norm_ffn2"], dmod2 = _ffn_bwd(lay, lay.nsx, dx3, ffn2, w["norm_ffn2"], modv, 6, w["ffn2"], "ffn2")
    dmix, dg2 = _resid_bwd(lay, lay.nsx, dx2, mix, modv, 5, 1.0, "mix_dres")
    dcat = _mm(dmix, w["w_out"], "nt", F32, "mix_dcat")
    g["w_out"] = _mm(cat, dmix, "tn", F32, "mix_dwout")
    duc, g["ln_w"], g["ln_b"] = _ln_bwd(lay, mx, uc, w["ln_w"], w["ln_b"], dcat, "mix_dln")
    du0, g["cconv_w"], g["cconv_b"] = _axial(lay, mx, u0, duc, w["cconv_w"], None, "mix_daxial")
    dyssd, dproj, g["d_row"], g["ssm_norm_w"] = _gate_bwd(lay, mx, yf, yb, xbc, proj, w["d_row"], w["ssm_norm_w"], dcat,
                                                          "mix_dgate")
    dproj = _glu_bwd(lay, mx, proj, du0, dproj, "mix_dglu")
    dxf, ddcf, ddrf, darf, dacf = _ssd_bwd(lay, mx, xbc, dt, dtt, w["a_row"], w["a_col"], hpf, dyssd, w["d_row"],
                                           False, "ssd_df")
    dxb, ddcb, ddrb, darb, dacb = _ssd_bwd(lay, mx, xbc, dt, dtt, w["a_row"], w["a_col"], hpb, dyssd, None,
                                           True, "ssd_db")
    g["a_row"] = (darf + darb) + (dacf + dacb).T
    dproj, g["conv_w"], g["conv_b"] = _conv5_bwd(lay.fine(), mx, proj, pre, dxf, dxb, w["conv_w"], dproj, "mix_dconv")
    dproj, g["dt_bias"] = _dt_bwd(lay, mx, proj, w["dt_bias"], (ddcf, ddcb, ddrf.T, ddrb.T), dproj, "mix_ddt")
    if exch:
        dha, got = _mm(dproj, w["w_in"], "nt", F32, "mix_dha", ride=exch.ffn2_sibling(g["ffn2"]))
        exch.take_ffn2(got)
    else:
        dha = _mm(dproj, w["w_in"], "nt", F32, "mix_dha")
    g["w_in"] = _mm(ha, dproj, "tn", F32, "mix_dwin")
    dxa1, g["norm_mix"], dss_mix = _norm_mod_bwd(lay, lay.ns, lay.nsx, xa1, w["norm_mix"], modv, 3, dha, dx2, "mix_dnorm")
    if exch:
        exch.early_grads(g)
    dx, g["ffn1"], g["norm_ffn1"], dmod1 = _ffn_bwd(lay, lay.ns, dxa1, ffn1, w["norm_ffn1"], modv, 0, w["ffn1"], "ffn1",
                                                   nout=lay.nsx, hooks=exch)
    zrow = lambda t: jnp.concatenate([t, jnp.zeros((1,) + t.shape[1:], F32)], axis=0)
    dmodv = jnp.concatenate([dmod1, dss_mix, zrow(dg2), zrow(dmod2)], axis=1)
    return loss, dx, g, dmodv


class _GatherRide:
    def __init__(self, xs):
        self.arrays = list(xs)
        self.na = len(xs)
        self.out_shapes = [jax.ShapeDtypeStruct((N_DEV,) + tuple(x.shape), x.dtype) for x in xs]
        self.sems = [pltpu.SemaphoreType.DMA((7 * self.na,)), pltpu.SemaphoreType.DMA((7 * self.na,)),
                     pltpu.SemaphoreType.DMA((self.na,))]

    def _plan(self, x_refs, out_refs, send_sems, recv_sems, local_sems):
        mx_, my_, mc_ = lax.axis_index("x"), lax.axis_index("y"), lax.axis_index("c")
        me, sibling = (mx_, my_, mc_), (mx_, my_, 1 - mc_)
        chips = [(1 - mx_, my_), (mx_, 1 - my_), (1 - mx_, 1 - my_)]

        def slot(a, px, py, pc):
            return out_refs[a].at[4 * px + 2 * py + pc]

        def copy(a, k, block, to, own=False):
            return pltpu.make_async_remote_copy(
                src_ref=x_refs[a] if own else slot(a, *block), dst_ref=slot(a, *block),
                send_sem=send_sems.at[7 * a + k], recv_sem=recv_sems.at[7 * a + k], device_id=to, device_id_type=MESH)

        mine = [pltpu.make_async_copy(x_refs[a], slot(a, *me), local_sems.at[a]) for a in range(self.na)]
        first = []
        for a in range(self.na):
            first.append(copy(a, 0, me, sibling, own=True))
            first += [copy(a, 1 + j, me, (*chip, mc_), own=True) for j, chip in enumerate(chips)]
        return me, sibling, chips, mc_, copy, mine, first

    def start(self, x_refs, out_refs, send_sems, recv_sems, local_sems):
        *_, mine, first = self._plan(x_refs, out_refs, send_sems, recv_sems, local_sems)
        for cp in mine + first:
            cp.start()

    def finish(self, x_refs, out_refs, send_sems, recv_sems, local_sems):
        me, sibling, chips, mc_, copy, mine, first = self._plan(x_refs, out_refs, send_sems, recv_sems, local_sems)
        passed = []
        for j, chip in enumerate(chips):
            for a in range(self.na):
                copy(a, 1 + j, (*chip, mc_), me).wait_recv()
                fwd = copy(a, 4 + j, (*chip, mc_), sibling)
                fwd.start()
                passed.append(fwd)
        for a in range(self.na):
            copy(a, 0, sibling, me).wait_recv()
            for j, chip in enumerate(chips):
                copy(a, 4 + j, (*chip, 1 - mc_), me).wait_recv()
        for cp in first + passed:
            cp.wait_send()
        for cp in mine:
            cp.wait()


def _exchange(ride, name, in_hbm=True):
    n_in, n_out = len(ride.arrays), len(ride.out_shapes)

    def body(*refs):
        ins, outs, sems = refs[:n_in], refs[n_in:n_in + n_out], refs[n_in + n_out:]
        ride.start(ins, outs, *sems)
        ride.finish(ins, outs, *sems)

    space = pl.BlockSpec(memory_space=pl.ANY if in_hbm else pltpu.VMEM)
    return pl.pallas_call(body, out_shape=list(ride.out_shapes), in_specs=[space] * n_in, out_specs=[space] * n_out,
                          scratch_shapes=list(ride.sems), name=name)(*ride.arrays)


def _all_gather(xs, name, in_hbm):
    return _exchange(_GatherRide(xs), name, in_hbm)


N_CHIPS = 4


class _SiblingRide:
    def __init__(self, gs):
        self.arrays = list(gs)
        self.na = len(gs)
        self.out_shapes = [jax.ShapeDtypeStruct((N_CHIPS,) + tuple(g.shape[2:]), g.dtype) for g in gs]
        self.sems = [pltpu.SemaphoreType.DMA((N_CHIPS * self.na,)), pltpu.SemaphoreType.DMA((N_CHIPS * self.na,))]

    def _copies(self, g_refs, out_refs, send_sems, recv_sems):
        mx_, my_, mc_ = lax.axis_index("x"), lax.axis_index("y"), lax.axis_index("c")
        return [pltpu.make_async_remote_copy(
            src_ref=g_refs[a].at[k, 1 - mc_], dst_ref=out_refs[a].at[k], send_sem=send_sems.at[N_CHIPS * a + k],
            recv_sem=recv_sems.at[N_CHIPS * a + k], device_id=(mx_, my_, 1 - mc_), device_id_type=MESH)
            for a in range(self.na) for k in range(N_CHIPS)]

    def start(self, g_refs, out_refs, send_sems, recv_sems):
        for cp in self._copies(g_refs, out_refs, send_sems, recv_sems):
            cp.start()

    def finish(self, g_refs, out_refs, send_sems, recv_sems):
        copies = self._copies(g_refs, out_refs, send_sems, recv_sems)
        for cp in copies:
            cp.wait_recv()
        for cp in copies:
            cp.wait_send()


def _row_tile(r, n):
    if r * n * 4 <= (1 << 20):
        return r
    for t in (1024, 512, 256, 128, 64, 32, 16, 8):
        if r % t == 0 and t * n * 4 <= (1 << 20):
            return t
    return r


def _pair_add(place, g, got, name):
    _, _, r, n = g.shape
    tr = r if r * n * 4 <= (3 << 19) else _row_tile(r, n)

    def body(place_ref, g_ref, got_ref, o_ref, ob_ref):
        s = g_ref[...] + got_ref[...]
        o_ref[...] = s
        ob_ref[...] = s.astype(ob_ref.dtype)

    blk = pl.BlockSpec((None, tr, n), lambda k, i, pr: (k, i, 0))
    grid_spec = pltpu.PrefetchScalarGridSpec(
        num_scalar_prefetch=1, grid=(N_CHIPS, r // tr),
        in_specs=[pl.BlockSpec((None, None, tr, n), lambda k, i, pr: (k, pr[0], i, 0)), blk], out_specs=[blk, blk])
    return pl.pallas_call(body, grid_spec=grid_spec,
                          out_shape=[jax.ShapeDtypeStruct((N_CHIPS, r, n), F32), jax.ShapeDtypeStruct((N_CHIPS, r, n), BF16)],
                          name=name, compiler_params=_cparams(2))(place, g, got)


class _ChipSwapRide:
    def __init__(self, ps):
        self.arrays = list(ps)
        self.na = len(ps)
        self.out_shapes = [jax.ShapeDtypeStruct((N_CHIP_PEERS,) + tuple(p.shape[1:]), p.dtype) for p in ps]
        self.sems = [pltpu.SemaphoreType.DMA((N_CHIP_PEERS * self.na,)), pltpu.SemaphoreType.DMA((N_CHIP_PEERS * self.na,))]

    def _copies(self, p_refs, out_refs, send_sems, recv_sems):
        mx_, my_, mc_ = lax.axis_index("x"), lax.axis_index("y"), lax.axis_index("c")
        chips = [(1 - mx_, my_), (mx_, 1 - my_), (1 - mx_, 1 - my_)]
        return [pltpu.make_async_remote_copy(
            src_ref=p_refs[a].at[2 * cx + cy], dst_ref=out_refs[a].at[j], send_sem=send_sems.at[N_CHIP_PEERS * a + j],
            recv_sem=recv_sems.at[N_CHIP_PEERS * a + j], device_id=(cx, cy, mc_), device_id_type=MESH)
            for a in range(self.na) for j, (cx, cy) in enumerate(chips)]

    def start(self, p_refs, out_refs, send_sems, recv_sems):
        for cp in self._copies(p_refs, out_refs, send_sems, recv_sems):
            cp.start()

    def finish(self, p_refs, out_refs, send_sems, recv_sems):
        copies = self._copies(p_refs, out_refs, send_sems, recv_sems)
        for cp in copies:
            cp.wait_recv()
        for cp in copies:
            cp.wait_send()


def _sum_lead(x, name):
    k, r, n = x.shape
    tr = _row_tile(r, n * k)

    def body(x_ref, o_ref):
        acc = x_ref[0]
        for i in range(1, k):
            acc = acc + x_ref[i]
        o_ref[...] = acc

    return pl.pallas_call(body, grid=(r // tr,), in_specs=[pl.BlockSpec((k, tr, n), lambda i: (0, i, 0))],
                          out_specs=pl.BlockSpec((tr, n), lambda i: (i, 0)),
                          out_shape=jax.ShapeDtypeStruct((r, n), x.dtype), name=name, compiler_params=_cparams(1))(x)


def _adamw(place, w, parts, m, v, name):
    shape = w.shape
    cols = shape[-1]
    rows = math.prod(shape[:-1])
    to2 = lambda t: t.reshape(rows, cols)
    tr = _row_tile(rows, cols) if rows * cols * 4 > (1 << 20) else rows
    npart = len(parts)
    spec = pl.BlockSpec((tr, cols), lambda i, pr: (i, 0))
    native = len(shape) == 3 and shape[0] == 1
    own = pl.BlockSpec((None, tr, cols), lambda i, pr: (0, i, 0)) if native else spec
    as_own = (lambda t: t) if native else to2
    part_specs, part_args = [], []
    for piece in parts:
        if isinstance(piece, tuple):
            stack, k, row0 = piece
            part_args.append(stack.reshape(stack.shape[0], -1, cols))
            assert row0 % tr == 0
            if k == "chip":
                part_specs.append(pl.BlockSpec((None, tr, cols), functools.partial(lambda i, pr, b0: (pr[1], i + b0, 0),
                                                                                   b0=row0 // tr)))
            else:
                part_specs.append(pl.BlockSpec((None, tr, cols), functools.partial(
                    lambda i, pr, kk, b0: (kk, i + b0, 0), kk=k, b0=row0 // tr)))
        else:
            part_args.append(to2(piece))
            part_specs.append(spec)

    def body(place_ref, *refs):
        w_ref, m_ref, v_ref = refs[0], refs[1 + npart], refs[2 + npart]
        g_ref, d_ref, nm_ref, nv_ref = refs[3 + npart:]
        g = refs[1][...].astype(F32)
        for q in range(1, npart):
            g = g + refs[1 + q][...].astype(F32)
        mm = ADAM_B1 * m_ref[...] + (1.0 - ADAM_B1) * g
        vv = ADAM_B2 * v_ref[...] + (1.0 - ADAM_B2) * jnp.square(g)
        m_hat = mm / (1.0 - ADAM_B1 ** ADAM_STEP)
        v_hat = vv / (1.0 - ADAM_B2 ** ADAM_STEP)
        g_ref[...] = g
        d_ref[...] = -ADAM_LR * (m_hat / (jnp.sqrt(v_hat) + ADAM_EPS) + ADAM_WD * w_ref[...])
        nm_ref[...] = mm
        nv_ref[...] = vv

    sh = jax.ShapeDtypeStruct(shape if native else (rows, cols), F32)
    grid_spec = pltpu.PrefetchScalarGridSpec(num_scalar_prefetch=1, grid=(rows // tr,),
                                             in_specs=[own] + part_specs + [own, own], out_specs=[own] * 4)
    outs = pl.pallas_call(body, grid_spec=grid_spec, out_shape=[sh] * 4, name=name, compiler_params=_cparams(1),
                          )(place, as_own(w), *part_args, as_own(m), as_own(v))
    return tuple(o.reshape(shape) for o in outs)


def _packed_rows(n, width):
    return -(-n // (8 * width)) * 8


def _pack_rows(items, width):
    rows = []
    for t in items:
        flat = t.reshape(-1)
        n = flat.shape[0]
        k = _packed_rows(n, width)
        if k * width > n:
            flat = jnp.concatenate([flat, jnp.zeros((k * width - n,), t.dtype)])
        rows.append(flat.reshape(k, width))
    return jnp.concatenate(rows, axis=0)


def _unpack_rows(packed, shapes, lead=()):
    width = packed.shape[-1]
    out, r = [], 0
    for sh in shapes:
        n = math.prod(sh)
        k = _packed_rows(n, width)
        piece = packed[..., r:r + k, :].reshape(tuple(lead) + (k * width,))[..., :n]
        out.append(piece.reshape(tuple(lead) + tuple(sh)))
        r += k
    return out


def _cols_full(t):
    return jnp.transpose(t, (1, 0, 2)).reshape(t.shape[1], -1)


def _cols_shards(t):
    d = t.shape[0]
    return jnp.transpose(t.reshape(d, N_DEV, -1), (1, 0, 2))


BIG = ("ffn1_gate", "ffn1_up", "ffn1_down", "w_in", "w_out", "ffn2_gate", "ffn2_up", "ffn2_down")


def kernel(x, c, ctx, c_ctx, w_mod, b_mod, norm_ffn1, ffn1_gate, ffn1_up, ffn1_down, norm_mix, w_in, ssm_conv_w, ssm_conv_b, dt_bias_fwd, dt_bias_bwd, a_log_fwd, a_log_bwd, ssm_d, ssm_norm_w, cconv_w, cconv_b, cconv_ln_w, cconv_ln_b, w_out, norm_ffn2, ffn2_gate, ffn2_up, ffn2_down, final_norm, loss_target, m_c_ctx, m_w_mod, m_b_mod, m_norm_ffn1, m_ffn1_gate, m_ffn1_up, m_ffn1_down, m_norm_mix, m_w_in, m_ssm_conv_w, m_ssm_conv_b, m_dt_bias_fwd, m_dt_bias_bwd, m_a_log_fwd, m_a_log_bwd, m_ssm_d, m_ssm_norm_w, m_cconv_w, m_cconv_b, m_cconv_ln_w, m_cconv_ln_b, m_w_out, m_norm_ffn2, m_ffn2_gate, m_ffn2_up, m_ffn2_down, m_final_norm, v_c_ctx, v_w_mod, v_b_mod, v_norm_ffn1, v_ffn1_gate, v_ffn1_up, v_ffn1_down, v_norm_mix, v_w_in, v_ssm_conv_w, v_ssm_conv_b, v_dt_bias_fwd, v_dt_bias_bwd, v_a_log_fwd, v_a_log_bwd, v_ssm_d, v_ssm_norm_w, v_cconv_w, v_cconv_b, v_cconv_ln_w, v_cconv_ln_b, v_w_out, v_norm_ffn2, v_ffn2_gate, v_ffn2_up, v_ffn2_down, v_final_norm):
    args = dict(locals())
    names = ("c_ctx", "w_mod", "b_mod", "norm_ffn1", "ffn1_gate", "ffn1_up", "ffn1_down", "norm_mix", "w_in",
             "ssm_conv_w", "ssm_conv_b", "dt_bias_fwd", "dt_bias_bwd", "a_log_fwd", "a_log_bwd", "ssm_d", "ssm_norm_w",
             "cconv_w", "cconv_b", "cconv_ln_w", "cconv_ln_b", "w_out", "norm_ffn2", "ffn2_gate", "ffn2_up",
             "ffn2_down", "final_norm")
    wts = {n: args[n] for n in names}
    bl, seq, d = x.shape
    clen = ctx.shape[1]
    heads = dt_bias_fwd.shape[1]
    ft = _gate_tile(ffn1_gate.shape[2] * N_DEV)
    lay = _Lay(bl, seq, clen, d)
    mx = _Mix(d, heads)
    nb = bl * N_DEV
    me = 4 * lax.axis_index("x") + 2 * lax.axis_index("y") + lax.axis_index("c")
    mcols = w_mod.shape[2]
    n_ctx_mod = 5 * d

    place = jnp.stack([lax.axis_index("c"), 2 * lax.axis_index("x") + lax.axis_index("y")]).astype(jnp.int32)

    small_shapes = [(bl, d), ssm_conv_w.shape[1:], cconv_w.shape[1:]]
    (g1,) = _all_gather([_pack_rows([c, ssm_conv_w, cconv_w], d)], "gather_small", False)
    c_g, conv_g, cconv_g = _unpack_rows(g1, small_shapes, (N_DEV,))
    c_all = c_g.reshape(nb, d)
    conv_w_full = jnp.transpose(conv_g, (1, 0, 2)).reshape(conv_g.shape[1], -1)
    cconv_w_full = jnp.transpose(cconv_g, (1, 0, 2)).reshape(cconv_g.shape[1], -1)

    s_all = jnp.concatenate([_silu(c_all), _silu(c_ctx)[None, :], jnp.zeros((7, d), F32)], axis=0)
    mod_cols = _mm(s_all, w_mod[0], "nn", F32, "mod_cols")
    (g2,) = _all_gather([mod_cols], "gather_mod", False)
    mod_all = _cols_full(g2) + b_mod
    mod_mine = jnp.concatenate([lax.dynamic_slice_in_dim(mod_all, me * bl, bl, axis=0), mod_all[nb:nb + 1]], axis=0)
    modv = mod_mine.reshape(bl + 1, N_MOD, d)

    hh = 2 * heads
    shard16 = lambda n: wts[n][0].astype(BF16)

    nl = ffn1_gate.shape[2]
    spt = ft // nl
    assert ft % nl == 0 and N_DEV % spt == 0

    def ffn_weights(gate, up, down):
        both = jnp.stack([gate, up], axis=1).reshape(N_DEV // spt, spt, 2, d, nl)
        return jnp.transpose(both, (3, 0, 2, 1, 4)).reshape(d, -1), None if down is None else down.reshape(-1, d), ft

    def grads_by_dest(name, grad):
        if name == "w_in":
            grad = _cols_shards(jnp.concatenate([grad[:, mx.off_z:], grad[:, :mx.off_dt + hh],
                                                 grad[:, mx.off_glu:mx.off_z]], axis=1))
        elif name.endswith("_gu"):
            grad = jnp.transpose(grad.reshape(d, N_DEV // spt, 2, spt, nl), (1, 3, 2, 0, 4)).reshape(N_DEV, 2 * d, nl)
        else:
            grad = grad.reshape((N_DEV,) + tuple(wts[name].shape[1:]))
        return grad.reshape((N_CHIPS, 2) + tuple(grad.shape[1:]))

    def ffn_grads(tag, pair):
        return {tag + "_gu": pair[0], tag + "_down": pair[1]}

    def by_dest_of(named):
        return {n: grads_by_dest(n, grad) for n, grad in named.items()}

    def to_chip_sums(named, got=None):
        by_dest = by_dest_of(named) if got is None else named
        names_ = list(by_dest)
        if got is None:
            got = _exchange(_SiblingRide([by_dest[n] for n in names_]), "rs_sibling_" + names_[0])
        return {n: _pair_add(place, by_dest[n], s, "rs_pair_add_" + n) for n, s in zip(names_, got)}

    class _Overlap:
        names_a = ("ffn1_down", "w_in", "ffn2_gate")
        names_b = ("w_out", "ffn2_up", "ffn2_down")
        late_a = _GatherRide([shard16(n) for n in names_a])
        late_b = _GatherRide([shard16(n) for n in names_b])
        sums, arrived = {}, {}

        def unpack_late(self, ffn1, outs_a, outs_b):
            full = {**dict(zip(self.names_a, outs_a)), **dict(zip(self.names_b, outs_b))}
            w_in_f = _cols_full(full["w_in"])
            w_in_p = jnp.concatenate([w_in_f[:, mx.ref_x:mx.ref_glu], jnp.zeros((d, DT_PAD - hh), BF16),
                                      w_in_f[:, mx.ref_glu:], w_in_f[:, :d]], axis=1)
            return {"w_in": w_in_p, "w_out": full["w_out"].reshape(-1, d),
                    "ffn1": (ffn1[0], full["ffn1_down"].reshape(-1, d), ffn1[2]),
                    "ffn2": ffn_weights(full["ffn2_gate"], full["ffn2_up"], full["ffn2_down"])}

        groups = (("ffn2_gu", "ffn2_down", "ffn1_down"), ("w_in", "w_out", "ffn1_gu"))

        def ffn2_sibling(self, g_ffn2):
            self.ffn2_dest = by_dest_of(ffn_grads("ffn2", g_ffn2))
            return _SiblingRide(list(self.ffn2_dest.values()))

        def take_ffn2(self, got):
            self.ffn2_got = got

        def early_grads(self, g):
            self.sums = to_chip_sums(self.ffn2_dest, self.ffn2_got)
            self.mix_dest = by_dest_of({"w_in": g["w_in"], "w_out": g["w_out"]})

        def during_da(self):
            return _SiblingRide(list(self.mix_dest.values()))

        def take_during_da(self, got):
            self.mix_got = got

        def _ride(self, grp):
            return _ChipSwapRide([self.sums[n][1] for n in grp])

        def after_dwd(self, dwd):
            self.sums.update(to_chip_sums({"ffn1_down": dwd}))
            return self._ride(self.groups[0])

        def after_dwgu(self, dwgu):
            self.sums.update(to_chip_sums(self.mix_dest, self.mix_got))
            self.sums.update(to_chip_sums({"ffn1_gu": dwgu}))
            return self._ride(self.groups[1])

        def take(self, *rode):
            for grp, outs in zip(self.groups, rode):
                self.arrived.update(zip(grp, outs))

    exch = _Overlap()
    gate1, up1 = _all_gather([shard16("ffn1_gate"), shard16("ffn1_up")], "gather_weights", True)
    wgu1 = ffn_weights(gate1, up1, None)[0]
    lanes_pad = lambda a, b: jnp.concatenate([a, b, jnp.zeros((1, DT_LANES - hh), F32)], axis=1)
    a_vals = lanes_pad(-jnp.exp(a_log_fwd), -jnp.exp(a_log_bwd))
    w = {
        "norm_ffn1": norm_ffn1, "norm_mix": norm_mix, "norm_ffn2": norm_ffn2, "final_norm": final_norm[None, :],
        "ffn1": (wgu1, lambda outs_a: outs_a[0].reshape(-1, d), ft),
        "conv_w": jnp.concatenate([conv_w_full, jnp.zeros((3, mx.xw), F32)], axis=0), "conv_b": ssm_conv_b,
        "dt_bias": lanes_pad(dt_bias_fwd, dt_bias_bwd), "a_row": a_vals, "a_col": a_vals.T,
        "d_row": jnp.repeat(ssm_d, HEAD_DIM, axis=1), "ssm_norm_w": ssm_norm_w,
        "cconv_w": cconv_w_full, "cconv_b": cconv_b, "ln_w": cconv_ln_w, "ln_b": cconv_ln_b,
    }

    xa = jnp.concatenate([x.reshape(bl * seq, d), ctx.reshape(bl * clen, d)], axis=0)
    loss, grad_x, g, dmodv = _local_step(lay, mx, xa, loss_target.reshape(bl * seq, d), modv, w, exch)
    loss = lax.psum(loss[0, 0], ("x", "y", "c"))

    sums, arrived = exch.sums, exch.arrived

    def big_parts(n):
        key, row0 = (n[:4] + "_gu", d if n.endswith("_up") else 0) if n.endswith(("_gate", "_up")) else (n, 0)
        return [(sums[key][0], "chip", row0)] + [(arrived[key], k, row0) for k in range(N_CHIP_PEERS)]

    n9 = N_MOD * d
    dmod_rows = dmodv.reshape(bl + 1, n9)
    ctx_row = jnp.concatenate([dmod_rows[bl, :n_ctx_mod], jnp.zeros((n9 - n_ctx_mod,), F32)])
    summed = [ctx_row, g["norm_ffn1"], g["norm_mix"], g["norm_ffn2"], g["final_norm"], g["conv_b"], g["dt_bias"],
              g["a_row"], g["d_row"], g["ssm_norm_w"], g["cconv_b"], g["ln_w"], g["ln_b"], g["conv_w"][:5], g["cconv_w"]]
    sum_shapes = [t.shape for t in summed]
    (g4,) = _all_gather([_pack_rows([dmod_rows[:bl]] + summed, d)], "gather_small_grads", False)
    dmod_batch = g4[:, :bl * N_MOD].reshape(nb, n9)
    tot = _sum_lead(g4[:, _packed_rows(bl * n9, d):], "sum_small_grads")
    (dctx, g_n1, g_nm, g_n2, g_fn, g_cb, g_dtb, g_a, g_drow, g_snw, g_ccb, g_lnw, g_lnb, g_cw, g_ccw) = _unpack_rows(tot, sum_shapes)
    dmod_all = jnp.concatenate([dmod_batch, dctx[None, :], jnp.zeros((7, n9), F32)], axis=0)

    dmod_my_cols = lax.dynamic_slice_in_dim(dmod_all, me * mcols, mcols, axis=1)
    g_w_mod = _mm(s_all, dmod_my_cols, "tn", F32, "dw_mod")[None]
    g_b_mod = _sum_lead(dmod_all.reshape(nb + 8, N_MOD, d), "db_mod").reshape(1, n9)
    ds_part = _mm(dmod_my_cols[nb:nb + 8], w_mod[0], "nt", F32, "ds_ctx")
    (g5,) = _all_gather([jnp.concatenate([ds_part[0:1], jnp.zeros((7, d), F32)], axis=0)], "gather_ds_ctx", False)
    ds_ctx = _sum_lead(g5, "sum_ds_ctx")[0]
    sg = jax.nn.sigmoid(c_ctx)
    g_c_ctx = ds_ctx * (sg * (1.0 + c_ctx * (1.0 - sg)))

    a_f, a_b = a_vals[:, :heads], a_vals[:, heads:hh]
    grads = {
        "c_ctx": [g_c_ctx], "w_mod": [g_w_mod], "b_mod": [g_b_mod],
        "norm_ffn1": [g_n1], "norm_mix": [g_nm], "norm_ffn2": [g_n2], "final_norm": [g_fn.reshape(-1)],
        "ssm_conv_w": [lax.dynamic_slice_in_dim(g_cw, me * ssm_conv_w.shape[2], ssm_conv_w.shape[2], axis=1)[None]],
        "ssm_conv_b": [g_cb],
        "dt_bias_fwd": [g_dtb[:, :heads]], "dt_bias_bwd": [g_dtb[:, heads:hh]],
        "a_log_fwd": [g_a[:, :heads] * a_f], "a_log_bwd": [g_a[:, heads:hh] * a_b],
        "ssm_d": [jnp.sum(g_drow.reshape(1, heads, HEAD_DIM), axis=2)], "ssm_norm_w": [g_snw],
        "cconv_w": [lax.dynamic_slice_in_dim(g_ccw, me * cconv_w.shape[2], cconv_w.shape[2], axis=1)[None]],
        "cconv_b": [g_ccb], "cconv_ln_w": [g_lnw], "cconv_ln_b": [g_lnb],
    }
    for n in BIG:
        grads[n] = big_parts(n)

    out_g, out_d, out_m, out_v = [], [], [], []
    for n in names:
        gr, de, nm, nv = _adamw(place, wts[n], grads[n], args["m_" + n], args["v_" + n], "adamw_" + n)
        out_g.append(gr)
        out_d.append(de)
        out_m.append(nm)
        out_v.append(nv)
    return (loss, grad_x.reshape(bl, seq, d), *out_g, *out_d, *out_m, *out_v)
```

```python
import functools
import math

import jax
import jax.numpy as jnp
from jax import lax
from jax.experimental import pallas as pl
from jax.experimental.pallas import tpu as pltpu

F32 = jnp.float32
BF16 = jnp.bfloat16
MESH = pl.DeviceIdType.MESH

N_DEV = 8
N_CHIP_PEERS = 3
HEAD_DIM = 64
N_STATE = 128
SSD_GROUPS = 2
CHUNK = 128
GRID_W = 64
N_MOD = 9
EPS = 1e-6
DT_PAD = 512
DT_LANES = 128
HALO = 8
ROW_TILE = 512
FINE_ROW_TILE = 256
VMEM_LIMIT = 48 * 1024 * 1024
NEG_BIG = -1e30

ADAM_LR = 0.001
ADAM_B1 = 0.9
ADAM_B2 = 0.999
ADAM_EPS = 1e-08
ADAM_WD = 0.01
ADAM_STEP = 10


def _pick(n, prefs):
    for p in prefs:
        if n % p == 0:
            return p
    return n


MM_TILE_CAP = 2816
MM_TILE_ELEMS = 3 << 20
MM_OUT_TILE_ELEMS = 3 << 19
MM_FULL_ROWS = 1024


def _big_tile(n, cap):
    if n <= cap:
        return n
    best = 0
    for t in range(128, cap + 1, 128):
        if n % t == 0:
            best = t
    return best or n


def _cparams(ndim):
    return pltpu.CompilerParams(dimension_semantics=("arbitrary",) * ndim, vmem_limit_bytes=VMEM_LIMIT)


def _silu(v):
    return v * jax.nn.sigmoid(v)


def _mm(a, b, mode, out_dtype, name, tn=None, tm=None, extras=(), epilogue=None, outs=None, ride=None):
    if mode == "tn":
        (K, M), (K2, N) = a.shape, b.shape
    elif mode == "nt":
        (M, K), (N, K2) = a.shape, b.shape
    else:
        (M, K), (K2, N) = a.shape, b.shape
    assert K == K2, (name, a.shape, b.shape)
    tm = tm or (M if M <= MM_FULL_ROWS else None)
    if tn is None:
        tn = _big_tile(N, min(MM_TILE_CAP, max(128, MM_OUT_TILE_ELEMS // (tm or 512))))
    if tm is None:
        tm = _big_tile(M, max(128, MM_OUT_TILE_ELEMS // tn))
    tk = _big_tile(K, min(MM_TILE_CAP, MM_TILE_ELEMS // max(tn, tm)))
    nk = K // tk
    ni, nj = M // tm, N // tn
    swap = nk == 1 and (K * N + M * K * nj) < (M * K + K * N * ni)
    ij = (lambda g0, g1: (g1, g0)) if swap else (lambda g0, g1: (g0, g1))
    if mode == "tn":
        a_spec = pl.BlockSpec((tk, tm), lambda g0, g1, k: (k, ij(g0, g1)[0]))
        dn = (((0,), (0,)), ((), ()))
    else:
        a_spec = pl.BlockSpec((tm, tk), lambda g0, g1, k: (ij(g0, g1)[0], k))
        dn = (((1,), (1,)), ((), ())) if mode == "nt" else (((1,), (0,)), ((), ()))
    if mode == "nt":
        b_spec = pl.BlockSpec((tn, tk), lambda g0, g1, k: (ij(g0, g1)[1], k))
    else:
        b_spec = pl.BlockSpec((tk, tn), lambda g0, g1, k: (k, ij(g0, g1)[1]))
    if outs is None:
        outs = [(tn, out_dtype)]
    nx = len(extras)

    def tile(w):
        return pl.BlockSpec((tm, w), lambda g0, g1, k: ij(g0, g1))

    def extra_spec(item):
        if len(item) == 3:
            return pl.BlockSpec((None,) + tuple(item[0].shape[1:]), lambda g0, g1, k: (item[1](ij(g0, g1)[0]), 0, 0))
        return tile(item[1])

    def finish(acc, refs):
        vals = (acc,) if epilogue is None else epilogue(acc, *[r[...] for r in refs[:nx]])
        for o_ref, v in zip(refs[nx:], vals):
            o_ref[...] = v.astype(o_ref.dtype)

    grid = (nj, ni, nk) if swap else (ni, nj, nk)
    nout = len(outs)
    r_in = len(ride.arrays) if ride else 0
    r_out = len(ride.out_shapes) if ride else 0

    def compute(a_ref, b_ref, refs):
        part = lax.dot_general(a_ref[...].astype(BF16), b_ref[...].astype(BF16), dn, preferred_element_type=F32)
        if nk == 1:
            finish(part, refs)
            return
        acc_ref, k = refs[-1], pl.program_id(2)
        _acc(acc_ref, k == 0, part)

        @pl.when(k == nk - 1)
        def _():
            finish(acc_ref[...], refs[:-1])

    def body(a_ref, b_ref, *refs):
        if ride is None:
            compute(a_ref, b_ref, refs)
            return
        x_refs, rin = refs[:nx], refs[nx:nx + r_in]
        o_refs, rout = refs[nx + r_in:nx + r_in + nout], refs[nx + r_in + nout:nx + r_in + nout + r_out]
        tail = refs[nx + r_in + nout + r_out:]
        nacc = 1 if nk > 1 else 0
        sems = tail[nacc:]
        ids = [pl.program_id(q) for q in range(3)]
        first = functools.reduce(jnp.logical_and, [i == 0 for i in ids])
        last = functools.reduce(jnp.logical_and, [i == n - 1 for i, n in zip(ids, grid)])
        pl.when(first)(lambda: ride.start(rin, rout, *sems))
        compute(a_ref, b_ref, tuple(x_refs) + tuple(o_refs) + tuple(tail[:nacc]))
        pl.when(last)(lambda: ride.finish(rin, rout, *sems))

    hbm = pl.BlockSpec(memory_space=pl.ANY)
    res = pl.pallas_call(
        body, grid=grid, in_specs=[a_spec, b_spec] + [extra_spec(x) for x in extras] + [hbm] * r_in,
        out_specs=[tile(w) for w, _ in outs] + [hbm] * r_out,
        out_shape=[jax.ShapeDtypeStruct((M, nj * w), dt) for w, dt in outs] + (list(ride.out_shapes) if ride else []),
        scratch_shapes=([pltpu.VMEM((tm, tn), F32)] if nk > 1 else []) + (list(ride.sems) if ride else []),
        name=name, compiler_params=_cparams(3),
    )(a, b, *[x[0] for x in extras], *(ride.arrays if ride else []))
    main = res[0] if epilogue is None else res[:nout]
    return (main, res[nout:]) if ride else main


class _Lay:
    def __init__(self, bl, seq, clen, d, tt=None):
        self.bl, self.seq, self.clen, self.d = bl, seq, clen, d
        self.tt = min(ROW_TILE, math.gcd(seq, bl * clen)) if tt is None else tt
        assert seq % self.tt == 0 and (bl * clen) % self.tt == 0 and self.tt % 8 == 0
        self.spb = seq // self.tt
        self.spc = clen // self.tt
        self.nsx = bl * self.spb
        self.nsc = bl * clen // self.tt
        self.ns = self.nsx + self.nsc
        self.tx = bl * seq
        self.ta = self.tx + bl * clen

    def fine(self):
        return _Lay(self.bl, self.seq, self.clen, self.d, min(FINE_ROW_TILE, self.clen))

    def mrow(self, s):
        return jnp.where(s < self.nsx, s // self.spb, self.bl)

    def first_of_row(self, s):
        return jnp.logical_or(jnp.logical_and(s < self.nsx, s % self.spb == 0), s == self.nsx)

    def seq_first(self, s):
        return jnp.where(s < self.nsx, s % self.spb == 0, (s - self.nsx) % self.spc == 0)

    def seq_last(self, s):
        return jnp.where(s < self.nsx, s % self.spb == self.spb - 1, (s - self.nsx) % self.spc == self.spc - 1)


def _tok(lay, c, cb=0, clamp=None):
    if clamp is None:
        return pl.BlockSpec((lay.tt, c), lambda j, s: (s, cb + j))
    return pl.BlockSpec((lay.tt, c), lambda j, s: (jnp.minimum(s, clamp), cb + j))


def _halo_prev(lay, c, cb=0):
    u = lay.tt // HALO
    return pl.BlockSpec((HALO, c), lambda j, s: (jnp.maximum(s * u - 1, 0), cb + j))


def _halo_next(lay, c, cb=0):
    u = lay.tt // HALO
    last = lay.ta // HALO - 1
    return pl.BlockSpec((HALO, c), lambda j, s: (jnp.minimum((s + 1) * u, last), cb + j))


def _row(lay, k, c):
    return pl.BlockSpec((None, k, c), lambda j, s: (lay.mrow(s), 0, 0))


def _glob(k, c, cb=None):
    if cb is None:
        return pl.BlockSpec((k, c), lambda j, s: (0, 0))
    return pl.BlockSpec((k, c), lambda j, s: (0, cb + j))


def _tok_call(name, body, ncb, nseg, in_specs, out_specs, out_shape, inputs, scratch=(), aliases=None):
    return pl.pallas_call(body, grid=(ncb, nseg), in_specs=in_specs, out_specs=out_specs, out_shape=out_shape,
                          scratch_shapes=list(scratch), name=name, compiler_params=_cparams(2),
                          input_output_aliases=aliases or {})(*inputs)


def _acc(ref, first, val):
    @pl.when(first)
    def _():
        ref[...] = val

    @pl.when(jnp.logical_not(first))
    def _():
        ref[...] += val


def _norm_mod_f(x, w, sh, sc):
    y = x * lax.rsqrt(jnp.mean(x * x, axis=-1, keepdims=True) + EPS) * w
    return y * (1.0 + sc) + sh


def _norm_mod_fwd(lay, nseg, x, w, modv, ksh, name):
    d = lay.d

    def body(x_ref, w_ref, m_ref, h_ref):
        h = _norm_mod_f(x_ref[...], w_ref[...], m_ref[ksh:ksh + 1, :], m_ref[ksh + 1:ksh + 2, :])
        h_ref[...] = h.astype(h_ref.dtype)

    return _tok_call(name, body, 1, nseg, [_tok(lay, d), _glob(1, d), _row(lay, N_MOD, d)], _tok(lay, d),
                     jax.ShapeDtypeStruct((nseg * lay.tt, d), BF16), (x, w, modv))


def _norm_mod_bwd(lay, nseg, nres, x, w, modv, ksh, dh, dres, name, nout=None):
    d = lay.d
    nrow = lay.bl + (1 if nseg > lay.nsx else 0)
    nout = nseg if nout is None else nout

    def body(x_ref, w_ref, m_ref, dh_ref, dres_ref, dx_ref, dw_ref, dm_ref):
        s = pl.program_id(1)
        _, vjp = jax.vjp(_norm_mod_f, x_ref[...], w_ref[...], m_ref[ksh:ksh + 1, :], m_ref[ksh + 1:ksh + 2, :])
        dx, dw, dsh, dsc = vjp(dh_ref[...])

        @pl.when(s < nout)
        def _():
            dx_ref[...] = dx + jnp.where(s < nres, dres_ref[...], 0.0)

        _acc(dw_ref, s == 0, dw)
        _acc(dm_ref, lay.first_of_row(s), jnp.concatenate([dsh, dsc], axis=0))

    return _tok_call(
        name, body, 1, nseg,
        [_tok(lay, d), _glob(1, d), _row(lay, N_MOD, d), _tok(lay, d), _tok(lay, d, clamp=nres - 1)],
        [_tok(lay, d, clamp=nout - 1), _glob(1, d), _row(lay, 2, d)],
        [jax.ShapeDtypeStruct((nout * lay.tt, d), F32), jax.ShapeDtypeStruct((1, d), F32),
         jax.ShapeDtypeStruct((nrow, 2, d), F32)],
        (x, w, modv, dh, dres))


def _mm_resid(lay, a, b, x, modv, kg, coef, name, then_norm=None, ride=None):
    d = lay.d
    tm = min(ROW_TILE, math.gcd(lay.seq, lay.bl * lay.clen))
    assert a.shape[0] % tm == 0 and b.shape[1] == d

    def row_of(i):
        return jnp.where(i * tm < lay.tx, (i * tm) // lay.seq, lay.bl)

    def add(acc, x_tile, m_blk, *nw):
        y = x_tile + (coef * m_blk[kg:kg + 1, :]) * acc
        if then_norm is None:
            return acc, y
        k = then_norm[1]
        return acc, y, _norm_mod_f(y, nw[0], m_blk[k:k + 1, :], m_blk[k + 1:k + 2, :])

    extras = [(x, d), (modv, row_of, "rows")]
    outs = [(d, F32), (d, F32)]
    if then_norm is not None:
        extras.append((then_norm[0].reshape(1, 1, d), lambda i: 0, "rows"))
        outs.append((d, BF16))
    return _mm(a, b, "nn", None, name, tm=tm, tn=d, extras=extras, epilogue=add, outs=outs, ride=ride)


def _resid_bwd(lay, nseg, dy, o, modv, kg, coef, name):
    d = lay.d
    nrow = lay.bl + (1 if nseg > lay.nsx else 0)

    def body(dy_ref, o_ref, m_ref, do_ref, dg_ref):
        s = pl.program_id(1)
        dy = dy_ref[...]
        do_ref[...] = (dy * (coef * m_ref[kg:kg + 1, :])).astype(do_ref.dtype)
        _acc(dg_ref, lay.first_of_row(s), jnp.sum(dy * o_ref[...], axis=0, keepdims=True) * coef)

    return _tok_call(name, body, 1, nseg, [_tok(lay, d), _tok(lay, d), _row(lay, N_MOD, d)],
                     [_tok(lay, d), _row(lay, 1, d)],
                     [jax.ShapeDtypeStruct((nseg * lay.tt, d), BF16), jax.ShapeDtypeStruct((nrow, 1, d), F32)],
                     (dy, o, modv))


def _final_loss(lay, x, wf, target, name):
    d = lay.d

    def body(x_ref, w_ref, t_ref, loss_ref, dx_ref, dw_ref):
        s = pl.program_id(1)

        def f(xv, wv):
            return xv * lax.rsqrt(jnp.mean(xv * xv, axis=-1, keepdims=True) + EPS) * wv

        y, vjp = jax.vjp(f, x_ref[...], w_ref[...])
        err = y - t_ref[...]
        part = 0.5 * jnp.sum(jnp.sum(err * err, axis=-1, keepdims=True), axis=0, keepdims=True) / d
        dx, dw = vjp(err / d)
        dx_ref[...] = dx
        _acc(loss_ref, s == 0, part)
        _acc(dw_ref, s == 0, dw)

    return _tok_call(name, body, 1, lay.nsx, [_tok(lay, d), _glob(1, d), _tok(lay, d)],
                     [_glob(1, 1), _tok(lay, d), _glob(1, d)],
                     [jax.ShapeDtypeStruct((1, 1), F32), jax.ShapeDtypeStruct((lay.tx, d), F32),
                      jax.ShapeDtypeStruct((1, d), F32)], (x, wf, target))


class _Mix:
    def __init__(self, d, heads):
        self.d_ssm = d
        self.d_conv = d
        self.heads = heads
        assert heads * HEAD_DIM == d and heads % (2 * SSD_GROUPS) == 0 and 2 * heads <= DT_LANES
        self.gn = SSD_GROUPS * N_STATE
        self.xw = d + 2 * self.gn
        self.off_x = 0
        self.off_dt = self.xw
        self.off_glu = self.xw + DT_PAD
        self.off_z = self.off_glu + 2 * d
        self.pw = self.off_z + d
        assert self.off_z % d == 0
        self.ref_x = d
        self.ref_dt = d + self.xw
        self.ref_glu = self.ref_dt + 2 * heads
        self.cc = self.xw if self.off_x % self.xw == 0 else _pick(self.xw, (512, 256, 128))


def _conv5_fwd(lay, mx, proj, cw, cb, name):
    c, tt = mx.cc, lay.tt
    cb0 = mx.off_x // c
    assert mx.off_x % c == 0

    def body(prev_ref, cur_ref, next_ref, w_ref, b_ref, pre_ref, act_ref, ext_ref):
        s = pl.program_id(1)
        ext_ref[0:HALO, :] = jnp.where(lay.seq_first(s), 0.0, prev_ref[...])
        ext_ref[HALO:HALO + tt, :] = cur_ref[...]
        ext_ref[HALO + tt:, :] = jnp.where(lay.seq_last(s), 0.0, next_ref[...])
        acc = jnp.zeros((tt, c), F32) + b_ref[...]
        for k in range(5):
            acc = acc + w_ref[k:k + 1, :] * ext_ref[pl.ds(HALO + k - 2, tt), :]
        pre_ref[...] = acc
        act_ref[...] = _silu(acc)

    sh = jax.ShapeDtypeStruct((lay.ta, mx.xw), F32)
    return _tok_call(name, body, mx.xw // c, lay.ns,
                     [_halo_prev(lay, c, cb0), _tok(lay, c, cb0), _halo_next(lay, c, cb0), _glob(8, c, 0), _glob(1, c, 0)],
                     [_tok(lay, c), _tok(lay, c)], [sh, sh], (proj, proj, proj, cw, cb),
                     scratch=[pltpu.VMEM((tt + 2 * HALO, c), F32)])


def _conv5_bwd(lay, mx, proj, pre, dact_f, dact_b, cw, dproj, name):
    c, tt = mx.cc, lay.tt
    cb0 = mx.off_x // c

    def dsilu(p):
        sg = jax.nn.sigmoid(p)
        return sg * (1.0 + p * (1.0 - sg))

    def body(xp_ref, xc_ref, xn_ref, pp_ref, pc_ref, pn_ref, fp_ref, fc_ref, fn_ref, bp_ref, bc_ref, bn_ref, w_ref,
             buf_ref, dx_ref, dw_ref, db_ref, extx_ref, extd_ref):
        s = pl.program_id(1)
        first, last = lay.seq_first(s), lay.seq_last(s)
        dcur = (fc_ref[...] + bc_ref[...]) * dsilu(pc_ref[...])
        extd_ref[0:HALO, :] = jnp.where(first, 0.0, (fp_ref[...] + bp_ref[...]) * dsilu(pp_ref[...]))
        extd_ref[HALO:HALO + tt, :] = dcur
        extd_ref[HALO + tt:, :] = jnp.where(last, 0.0, (fn_ref[...] + bn_ref[...]) * dsilu(pn_ref[...]))
        extx_ref[0:HALO, :] = jnp.where(first, 0.0, xp_ref[...])
        extx_ref[HALO:HALO + tt, :] = xc_ref[...]
        extx_ref[HALO + tt:, :] = jnp.where(last, 0.0, xn_ref[...])
        dx = jnp.zeros((tt, c), F32)
        rows = []
        for k in range(5):
            dx = dx + w_ref[k:k + 1, :] * extd_ref[pl.ds(HALO - (k - 2), tt), :]
            rows.append(jnp.sum(dcur * extx_ref[pl.ds(HALO + k - 2, tt), :], axis=0, keepdims=True))
        dx_ref[...] = dx.astype(dx_ref.dtype)
        rows.append(jnp.zeros((3, c), F32))
        _acc(dw_ref, s == 0, jnp.concatenate(rows, axis=0))
        _acc(db_ref, s == 0, jnp.sum(dcur, axis=0, keepdims=True))

    three = lambda cbx: [_halo_prev(lay, c, cbx), _tok(lay, c, cbx), _halo_next(lay, c, cbx)]
    ext = pltpu.VMEM((tt + 2 * HALO, c), F32)
    return _tok_call(name, body, mx.xw // c, lay.ns,
                     three(cb0) + three(0) + three(0) + three(0) + [_glob(8, c, 0), pl.BlockSpec(memory_space=pl.ANY)],
                     [_tok(lay, c, cb0), _glob(8, c, 0), _glob(1, c, 0)],
                     [jax.ShapeDtypeStruct(dproj.shape, dproj.dtype), jax.ShapeDtypeStruct((8, mx.xw), F32),
                      jax.ShapeDtypeStruct((1, mx.xw), F32)],
                     (proj, proj, proj, pre, pre, pre, dact_f, dact_f, dact_f, dact_b, dact_b, dact_b, cw, dproj),
                     scratch=[ext, ext], aliases={13: 0})


def _softplus(v):
    return jnp.maximum(v, 0.0) + jnp.log1p(jnp.exp(-jnp.abs(v)))


def _dt_fwd(lay, mx, proj, bias, name):
    cb = mx.off_dt // DT_LANES

    def body(p_ref, b_ref, dt_ref):
        dt_ref[...] = _softplus(p_ref[...] + b_ref[...])

    return _tok_call(name, body, 1, lay.ns, [_tok(lay, DT_LANES, cb), _glob(1, DT_LANES)], _tok(lay, DT_LANES),
                     jax.ShapeDtypeStruct((lay.ta, DT_LANES), F32), (proj, bias))


def _dt_bwd(lay, mx, proj, bias, parts, dproj, name):
    cb = mx.off_dt // DT_LANES
    wb = DT_PAD if mx.off_dt % DT_PAD == 0 else DT_LANES
    ncb = DT_PAD // wb

    def body(p_ref, b_ref, a_ref, b2_ref, c_ref, d_ref, buf_ref, dp_ref, db_ref):
        j, s = pl.program_id(0), pl.program_id(1)

        @pl.when(j == 0)
        def _():
            ddt = (a_ref[...] + b2_ref[...]) + (c_ref[...] + d_ref[...])
            draw = ddt * jax.nn.sigmoid(p_ref[...] + b_ref[...])
            dp_ref[:, 0:DT_LANES] = draw.astype(dp_ref.dtype)
            if wb > DT_LANES:
                dp_ref[:, DT_LANES:] = jnp.zeros((lay.tt, wb - DT_LANES), dp_ref.dtype)
            _acc(db_ref, s == 0, jnp.sum(draw, axis=0, keepdims=True))

        @pl.when(j > 0)
        def _():
            dp_ref[...] = jnp.zeros_like(dp_ref)

    t = pl.BlockSpec((lay.tt, DT_LANES), lambda j, s: (s, 0))
    return _tok_call(name, body, ncb, lay.ns,
                     [pl.BlockSpec((lay.tt, DT_LANES), lambda j, s: (s, cb)), _glob(1, DT_LANES), t, t, t, t,
                      pl.BlockSpec(memory_space=pl.ANY)],
                     [_tok(lay, wb, mx.off_dt // wb), _glob(1, DT_LANES)],
                     [jax.ShapeDtypeStruct(dproj.shape, dproj.dtype), jax.ShapeDtypeStruct((1, DT_LANES), F32)],
                     (proj, bias) + tuple(parts) + (dproj,), aliases={6: 0})


def _scan_mask(rev):
    r = lax.broadcasted_iota(jnp.int32, (CHUNK, CHUNK), 0)
    c = lax.broadcasted_iota(jnp.int32, (CHUNK, CHUNK), 1)
    return (r <= c) if rev else (r >= c)


def _split_bf16(x):
    hi = x.astype(BF16)
    return hi, (x - hi.astype(F32)).astype(BF16)


@functools.partial(jax.custom_vjp, nondiff_argnums=(0,))
def _cum_cols(rev, x):
    m = _scan_mask(rev).astype(BF16)
    hi, lo = _split_bf16(x)
    return jnp.dot(m, hi, preferred_element_type=F32) + jnp.dot(m, lo, preferred_element_type=F32)


_cum_cols.defvjp(lambda rev, x: (_cum_cols(rev, x), None), lambda rev, _, g: (_cum_cols(not rev, g),))


@functools.partial(jax.custom_vjp, nondiff_argnums=(0,))
def _cum_rows(rev, x):
    m = _scan_mask(not rev).astype(BF16)
    hi, lo = _split_bf16(x)
    return jnp.dot(hi, m, preferred_element_type=F32) + jnp.dot(lo, m, preferred_element_type=F32)


_cum_rows.defvjp(lambda rev, x: (_cum_rows(rev, x), None), lambda rev, _, g: (_cum_rows(not rev, g),))


def _ssd_chunk(xh_pairs, bcs, ccs, dtc, dtr, a_row, a_col, st_pairs, *, rev, heads, col0):
    cs_c, cs_r, tot, scores = _ssd_shared(bcs, ccs, dtc, dtr, a_row, a_col, rev=rev)
    ppg = heads // (2 * SSD_GROUPS)
    ys, sts = [], []
    for g in range(SSD_GROUPS):
        y, st = _ssd_group(xh_pairs[g * ppg:(g + 1) * ppg], bcs[g], ccs[g], st_pairs[g], cs_c, cs_r, tot, dtc,
                           scores[g], rev=rev, col=col0 + 2 * ppg * g)
        ys.append(y)
        sts.append(st)
    return ys, sts


_NT = (((1,), (1,)), ((), ()))
_TN = (((0,), (0,)), ((), ()))


def _ssd_shared(bcs, ccs, dtc, dtr, a_row, a_col, *, rev):
    da_c = dtc * a_row
    cs_c = _cum_cols(rev, da_c)
    cs_r = _cum_rows(rev, dtr * a_col)
    tot = jnp.sum(da_c, axis=0, keepdims=True)
    scores = [lax.dot_general(ccs[g].astype(BF16), bcs[g].astype(BF16), _NT, preferred_element_type=F32)
              for g in range(SSD_GROUPS)]
    return cs_c, cs_r, tot, scores


def _ssd_group(xh_pairs, bc, cc, st, cs_c, cs_r, tot, dtc, score, *, rev, col):
    n = CHUNK
    mask = _scan_mask(rev)
    lane = lax.broadcasted_iota(jnp.int32, (n, DT_LANES), 1)
    sub = lax.broadcasted_iota(jnp.int32, (DT_LANES, n), 0)
    lane1 = lax.broadcasted_iota(jnp.int32, (1, DT_LANES), 1)
    left = lax.broadcasted_iota(jnp.int32, (n, 2 * HEAD_DIM), 1) < HEAD_DIM
    top = lax.broadcasted_iota(jnp.int32, (2 * HEAD_DIM, 1), 0) < HEAD_DIM
    xs_all, wst_all, ecs_all, edec_all, y_diag = [], [], [], [], []
    for p, xh in enumerate(xh_pairs):
        per = []
        for c in (col + 2 * p, col + 2 * p + 1):
            csv = jnp.sum(jnp.where(lane == c, cs_c, 0.0), axis=1, keepdims=True)
            csr = jnp.sum(jnp.where(sub == c, cs_r, 0.0), axis=0, keepdims=True)
            dtv = jnp.sum(jnp.where(lane == c, dtc, 0.0), axis=1, keepdims=True)
            tv = jnp.sum(jnp.where(lane1 == c, tot, 0.0), axis=1, keepdims=True)
            m = score * jnp.exp(jnp.where(mask, csv - csr, NEG_BIG))
            per.append((csv, dtv, tv, m))
        (cs1, dt1, t1, m1), (cs2, dt2, t2, m2) = per
        xs = xh * jnp.where(left, dt1, dt2)
        both = jnp.dot(jnp.concatenate([m1, m2], axis=0).astype(BF16), xs.astype(BF16), preferred_element_type=F32)
        y_diag.append(jnp.where(left, both[:n], both[n:]))
        xs_all.append(xs)
        ecs_all.append(jnp.where(left, jnp.exp(cs1), jnp.exp(cs2)))
        wst_all.append(jnp.where(left, jnp.exp(t1 - cs1), jnp.exp(t2 - cs2)))
        edec_all.append(jnp.where(top, jnp.exp(t1), jnp.exp(t2)))
    cat = lambda parts, axis: parts[0] if len(parts) == 1 else jnp.concatenate(parts, axis=axis)
    xs, wst, ecs = cat(xs_all, 1), cat(wst_all, 1), cat(ecs_all, 1)
    y_off = lax.dot_general(cc.astype(BF16), st.astype(BF16), _NT, preferred_element_type=F32) * ecs
    cst = lax.dot_general((xs * wst).astype(BF16), bc.astype(BF16), _TN, preferred_element_type=F32)
    return cat(y_diag, 1) + y_off, st * cat(edec_all, 0) + cst


class _Scan:
    def __init__(self, lay, rev):
        self.ncx, self.ncc, self.bl, self.rev = lay.seq // CHUNK, lay.clen // CHUNK, lay.bl, rev
        self.nct = self.ncx + self.ncc

    def chunk(self, b, pos):
        kc = (self.ncc - 1 - pos) if self.rev else pos
        kx = (self.ncx - 1 - (pos - self.ncc)) if self.rev else (pos - self.ncc)
        return jnp.where(pos < self.ncc, self.bl * self.ncx + b * self.ncc + kc, b * self.ncx + kx)


def _ssd_io(mx, x_ref, st_src):
    np_ = mx.heads // 2
    d = mx.d_ssm
    xh = [x_ref[:, 128 * p:128 * (p + 1)] for p in range(np_)]
    bcs = [x_ref[:, d + N_STATE * g:d + N_STATE * (g + 1)] for g in range(SSD_GROUPS)]
    ccs = [x_ref[:, d + mx.gn + N_STATE * g:d + mx.gn + N_STATE * (g + 1)] for g in range(SSD_GROUPS)]
    gw = d // SSD_GROUPS
    sts = [st_src[gw * g:gw * (g + 1), :] for g in range(SSD_GROUPS)]
    return xh, bcs, ccs, sts


def _ssd_fwd(lay, mx, xbc, dt, dtt, a_row, a_col, rev, name):
    sc = _Scan(lay, rev)
    col0 = mx.heads if rev else 0
    hp = mx.heads * HEAD_DIM

    def body(x_ref, dt_ref, dtt_ref, ar_ref, ac_ref, y_ref, hp_ref, st_ref):
        @pl.when(pl.program_id(1) == 0)
        def _():
            st_ref[...] = jnp.zeros_like(st_ref)

        hp_ref[...] = st_ref[...]
        xh, bcs, ccs, sts = _ssd_io(mx, x_ref, st_ref)
        ys, new = _ssd_chunk(xh, bcs, ccs, dt_ref[...], dtt_ref[...], ar_ref[...], ac_ref[...], sts,
                             rev=rev, heads=mx.heads, col0=col0)
        gw = mx.d_ssm // SSD_GROUPS
        for g in range(SSD_GROUPS):
            y_ref[:, gw * g:gw * (g + 1)] = ys[g]
            st_ref[gw * g:gw * (g + 1), :] = new[g]

    ch = sc.chunk
    return pl.pallas_call(
        body, grid=(lay.bl, sc.nct),
        in_specs=[pl.BlockSpec((CHUNK, mx.xw), lambda b, i: (ch(b, i), 0)),
                  pl.BlockSpec((CHUNK, DT_LANES), lambda b, i: (ch(b, i), 0)),
                  pl.BlockSpec((DT_LANES, CHUNK), lambda b, i: (0, ch(b, i))),
                  pl.BlockSpec((1, DT_LANES), lambda b, i: (0, 0)),
                  pl.BlockSpec((DT_LANES, 1), lambda b, i: (0, 0))],
        out_specs=[pl.BlockSpec((CHUNK, mx.d_ssm), lambda b, i: (ch(b, i), 0)),
                   pl.BlockSpec((hp, N_STATE), lambda b, i: (b * sc.nct + i, 0))],
        out_shape=[jax.ShapeDtypeStruct((lay.ta, mx.d_ssm), F32),
                   jax.ShapeDtypeStruct((lay.bl * sc.nct * hp, N_STATE), F32)],
        scratch_shapes=[pltpu.VMEM((hp, N_STATE), F32)], name=name, compiler_params=_cparams(2),
    )(xbc, dt, dtt, a_row, a_col)


def _ssd_bwd(lay, mx, xbc, dt, dtt, a_row, a_col, hprev, dy, dskip, rev, name):
    sc = _Scan(lay, rev)
    col0 = mx.heads if rev else 0
    hp = mx.heads * HEAD_DIM
    np_ = mx.heads // 2
    d = mx.d_ssm
    with_skip = dskip is not None

    def body(*refs):
        if with_skip:
            x_ref, dt_ref, dtt_ref, ar_ref, ac_ref, hp_ref, dy_ref, sk_ref = refs[:8]
            rest = refs[8:]
        else:
            x_ref, dt_ref, dtt_ref, ar_ref, ac_ref, hp_ref, dy_ref = refs[:7]
            rest = refs[7:]
        dx_ref, ddc_ref, ddr_ref, dar_ref, dac_ref, ds_ref = rest
        b, i = pl.program_id(0), pl.program_id(1)

        @pl.when(i == 0)
        def _():
            ds_ref[...] = jnp.zeros_like(ds_ref)

        xh, bcs, ccs, sts = _ssd_io(mx, x_ref, hp_ref)
        dtc = dt_ref[...]
        shared, vjp_shared = jax.vjp(functools.partial(_ssd_shared, rev=rev), bcs, ccs, dtc, dtt_ref[...],
                                     ar_ref[...], ac_ref[...])
        cs_c, cs_r, tot, scores = shared
        plus = lambda acc, v: v if acc is None else acc + v
        d_cs_c = d_cs_r = d_tot = ddc = None
        d_scores, dbc, dcc = [], [], []
        ppg = np_ // SSD_GROUPS
        gw = d // SSD_GROUPS
        for g in range(SSD_GROUPS):
            dyg = dy_ref[:, gw * g:gw * (g + 1)]
            fn = functools.partial(_ssd_group, rev=rev, col=col0 + 2 * ppg * g)
            _, vjp = jax.vjp(fn, xh[g * ppg:(g + 1) * ppg], bcs[g], ccs[g], sts[g], cs_c, cs_r, tot, dtc, scores[g])
            dxh, dbc_g, dcc_g, dst, dcs_c_g, dcs_r_g, dtot_g, ddc_g, dsc_g = vjp((dyg, ds_ref[gw * g:gw * (g + 1), :]))
            for q in range(ppg):
                p = g * ppg + q
                v = dxh[q]
                if with_skip:
                    v = v + dyg[:, 128 * q:128 * (q + 1)] * sk_ref[:, 128 * p:128 * (p + 1)]
                dx_ref[:, 128 * p:128 * (p + 1)] = v
            ds_ref[gw * g:gw * (g + 1), :] = dst
            d_cs_c, d_cs_r, d_tot, ddc = plus(d_cs_c, dcs_c_g), plus(d_cs_r, dcs_r_g), plus(d_tot, dtot_g), plus(ddc, ddc_g)
            d_scores.append(dsc_g)
            dbc.append(dbc_g)
            dcc.append(dcc_g)
        dbc_s, dcc_s, ddc_s, ddr, dar, dac = vjp_shared((d_cs_c, d_cs_r, d_tot, d_scores))
        ddc = ddc + ddc_s
        dbc = [dbc[g] + dbc_s[g] for g in range(SSD_GROUPS)]
        dcc = [dcc[g] + dcc_s[g] for g in range(SSD_GROUPS)]
        for g in range(SSD_GROUPS):
            dx_ref[:, d + N_STATE * g:d + N_STATE * (g + 1)] = dbc[g]
            dx_ref[:, d + mx.gn + N_STATE * g:d + mx.gn + N_STATE * (g + 1)] = dcc[g]
        ddc_ref[...] = ddc
        ddr_ref[...] = ddr
        first = jnp.logical_and(b == 0, i == 0)
        _acc(dar_ref, first, dar)
        _acc(dac_ref, first, dac)

    ch = lambda b, i: sc.chunk(b, sc.nct - 1 - i)
    in_specs = [pl.BlockSpec((CHUNK, mx.xw), lambda b, i: (ch(b, i), 0)),
                pl.BlockSpec((CHUNK, DT_LANES), lambda b, i: (ch(b, i), 0)),
                pl.BlockSpec((DT_LANES, CHUNK), lambda b, i: (0, ch(b, i))),
                pl.BlockSpec((1, DT_LANES), lambda b, i: (0, 0)),
                pl.BlockSpec((DT_LANES, 1), lambda b, i: (0, 0)),
                pl.BlockSpec((hp, N_STATE), lambda b, i: (b * sc.nct + sc.nct - 1 - i, 0)),
                pl.BlockSpec((CHUNK, d), lambda b, i: (ch(b, i), 0))]
    inputs = [xbc, dt, dtt, a_row, a_col, hprev, dy]
    if with_skip:
        in_specs.append(pl.BlockSpec((1, d), lambda b, i: (0, 0)))
        inputs.append(dskip)
    return pl.pallas_call(
        body, grid=(lay.bl, sc.nct), in_specs=in_specs,
        out_specs=[pl.BlockSpec((CHUNK, mx.xw), lambda b, i: (ch(b, i), 0)),
                   pl.BlockSpec((CHUNK, DT_LANES), lambda b, i: (ch(b, i), 0)),
                   pl.BlockSpec((DT_LANES, CHUNK), lambda b, i: (0, ch(b, i))),
                   pl.BlockSpec((1, DT_LANES), lambda b, i: (0, 0)),
                   pl.BlockSpec((DT_LANES, 1), lambda b, i: (0, 0))],
        out_shape=[jax.ShapeDtypeStruct((lay.ta, mx.xw), F32), jax.ShapeDtypeStruct((lay.ta, DT_LANES), F32),
                   jax.ShapeDtypeStruct((DT_LANES, lay.ta), F32), jax.ShapeDtypeStruct((1, DT_LANES), F32),
                   jax.ShapeDtypeStruct((DT_LANES, 1), F32)],
        scratch_shapes=[pltpu.VMEM((hp, N_STATE), F32)], name=name, compiler_params=_cparams(2),
    )(*inputs)


def _gate_f(yf, yb, xh, z, drow, nw):
    dd = yf.shape[-1]
    half = dd // SSD_GROUPS
    yz = (yf + yb + drow * xh) * _silu(z)
    lo = lax.broadcasted_iota(jnp.int32, yz.shape, 1) < half
    sq = yz * yz
    ms1 = jnp.sum(jnp.where(lo, sq, 0.0), axis=-1, keepdims=True) / half
    ms2 = jnp.sum(jnp.where(lo, 0.0, sq), axis=-1, keepdims=True) / half
    return yz * jnp.where(lo, lax.rsqrt(ms1 + EPS), lax.rsqrt(ms2 + EPS)) * nw


def _gate_fwd(lay, mx, yf, yb, xbc, proj, drow, nw, name):
    d = mx.d_ssm

    def body(yf_ref, yb_ref, xh_ref, z_ref, d_ref, w_ref, o_ref):
        o_ref[...] = _gate_f(yf_ref[...], yb_ref[...], xh_ref[...], z_ref[...], d_ref[...], w_ref[...]).astype(o_ref.dtype)

    t, z = _tok(lay, d), _tok(lay, d, mx.off_z // d)
    return _tok_call(name, body, 1, lay.nsx, [t, t, t, z, _glob(1, d), _glob(1, d)], t,
                     jax.ShapeDtypeStruct((lay.tx, d + mx.d_conv), BF16), (yf, yb, xbc, proj, drow, nw))


def _gate_bwd(lay, mx, yf, yb, xbc, proj, drow, nw, dcat, name):
    d = mx.d_ssm
    nsx = lay.nsx

    def body(yf_ref, yb_ref, xh_ref, z_ref, d_ref, w_ref, dc_ref, dy_ref, dz_ref, dd_ref, dw_ref):
        s = pl.program_id(1)

        @pl.when(s < nsx)
        def _():
            _, vjp = jax.vjp(_gate_f, yf_ref[...], yb_ref[...], xh_ref[...], z_ref[...], d_ref[...], w_ref[...])
            dyf, _, _, dz, dd, dw = vjp(dc_ref[...])
            dy_ref[...] = dyf
            dz_ref[...] = dz.astype(dz_ref.dtype)
            _acc(dd_ref, s == 0, dd)
            _acc(dw_ref, s == 0, dw)

        @pl.when(s >= nsx)
        def _():
            dy_ref[...] = jnp.zeros_like(dy_ref)
            dz_ref[...] = jnp.zeros_like(dz_ref)

    t, z = _tok(lay, d), _tok(lay, d, mx.off_z // d)
    return _tok_call(name, body, 1, lay.ns, [t, t, t, z, _glob(1, d), _glob(1, d), _tok(lay, d, clamp=nsx - 1)],
                     [t, z, _glob(1, d), _glob(1, d)],
                     [jax.ShapeDtypeStruct((lay.ta, d), F32), jax.ShapeDtypeStruct((lay.ta, mx.pw), BF16),
                      jax.ShapeDtypeStruct((1, d), F32), jax.ShapeDtypeStruct((1, d), F32)],
                     (yf, yb, xbc, proj, drow, nw, dcat))


def _glu_fwd(lay, mx, proj, name):
    d = mx.d_conv
    c = math.gcd(mx.off_glu, d)
    cb = mx.off_glu // c

    def body(a_ref, b_ref, o_ref):
        o_ref[...] = a_ref[...] * jax.nn.sigmoid(b_ref[...])

    return _tok_call(name, body, d // c, lay.nsx, [_tok(lay, c, cb), _tok(lay, c, cb + d // c)], _tok(lay, c),
                     jax.ShapeDtypeStruct((lay.tx, d), F32), (proj, proj))


def _glu_bwd(lay, mx, proj, du, dproj, name):
    d = mx.d_conv
    nsx = lay.nsx
    whole = mx.off_glu % (2 * d) == 0
    c = d if whole else math.gcd(mx.off_glu, d)
    cb = mx.off_glu // c
    nc = d // c

    def body(a_ref, b_ref, du_ref, buf_ref, o_ref):
        j, s = pl.program_id(0), pl.program_id(1)

        @pl.when(s < nsx)
        def _():
            sg = jax.nn.sigmoid(b_ref[...])
            da = du_ref[...] * sg
            db = da * a_ref[...] * (1.0 - sg)
            if whole:
                o_ref[:, 0:d] = da.astype(o_ref.dtype)
                o_ref[:, d:] = db.astype(o_ref.dtype)
            else:
                o_ref[...] = jnp.where(j < nc, da, db).astype(o_ref.dtype)

        @pl.when(s >= nsx)
        def _():
            o_ref[...] = jnp.zeros_like(o_ref)

    win = lambda half: pl.BlockSpec((lay.tt, c), lambda j, s: (s, cb + half * nc + j % nc))
    out = _tok(lay, 2 * d, mx.off_glu // (2 * d)) if whole else _tok(lay, c, cb)
    return _tok_call(name, body, 1 if whole else 2 * nc, lay.ns,
                     [win(0), win(1), pl.BlockSpec((lay.tt, c), lambda j, s: (jnp.minimum(s, nsx - 1), j % nc)),
                      pl.BlockSpec(memory_space=pl.ANY)],
                     out, jax.ShapeDtypeStruct(dproj.shape, dproj.dtype), (proj, proj, du, dproj), aliases={3: 0})


def _axial(lay, mx, u, dy, cw, cb, name):
    d, seq = mx.d_conv, lay.seq
    kw = cw.shape[0]
    pad = kw // 2
    c = _pick(d // 2, (256, 128))
    ncb = d // c
    zpad = GRID_W * pad
    zpad = -(-zpad // 8) * 8
    backward = dy is not None

    def shifted(ext_ref, off):
        return ext_ref[pl.ds(zpad + off, seq), :]

    def valid_row(off):
        col = lax.broadcasted_iota(jnp.int32, (seq, c), 0) % GRID_W
        return jnp.logical_and(col + off >= 0, col + off < GRID_W)

    def fill(ext_ref, v):
        ext_ref[0:zpad, :] = jnp.zeros((zpad, c), F32)
        ext_ref[zpad:zpad + seq, :] = v
        ext_ref[zpad + seq:, :] = jnp.zeros((zpad, c), F32)

    def conv(ext_ref, w_ref, is_row, sign):
        acc = jnp.zeros((seq, c), F32)
        for k in range(kw):
            off = sign * ((k - pad) if is_row else GRID_W * (k - pad))
            v = shifted(ext_ref, off)
            if is_row:
                v = jnp.where(valid_row(off), v, 0.0)
            acc = acc + w_ref[k:k + 1, :] * v
        return acc

    def fwd_body(u_ref, w_ref, b_ref, o_ref, ext_ref):
        j = pl.program_id(0)
        fill(ext_ref, u_ref[...])

        @pl.when(j < ncb // 2)
        def _():
            o_ref[...] = conv(ext_ref, w_ref, True, 1) + b_ref[...]

        @pl.when(j >= ncb // 2)
        def _():
            o_ref[...] = conv(ext_ref, w_ref, False, 1) + b_ref[...]

    def bwd_body(u_ref, dy_ref, w_ref, du_ref, dw_ref, db_ref, extu_ref, extd_ref):
        j, b = pl.program_id(0), pl.program_id(1)
        dyv = dy_ref[...]
        fill(extu_ref, u_ref[...])
        fill(extd_ref, dyv)

        def grads(is_row):
            du_ref[...] = conv(extd_ref, w_ref, is_row, -1)
            rows = []
            for k in range(kw):
                off = (k - pad) if is_row else GRID_W * (k - pad)
                v = shifted(extu_ref, off)
                if is_row:
                    v = jnp.where(valid_row(off), v, 0.0)
                rows.append(jnp.sum(dyv * v, axis=0, keepdims=True))
            _acc(dw_ref, b == 0, jnp.concatenate(rows, axis=0))

        @pl.when(j < ncb // 2)
        def _():
            grads(True)

        @pl.when(j >= ncb // 2)
        def _():
            grads(False)

        _acc(db_ref, b == 0, jnp.sum(dyv, axis=0, keepdims=True))

    seq_spec = pl.BlockSpec((seq, c), lambda j, b: (b, j))
    w_spec = pl.BlockSpec((kw, c), lambda j, b: (0, j))
    b_spec = pl.BlockSpec((1, c), lambda j, b: (0, j))
    ext = pltpu.VMEM((seq + 2 * zpad, c), F32)
    if not backward:
        return pl.pallas_call(fwd_body, grid=(ncb, lay.bl), in_specs=[seq_spec, w_spec, b_spec], out_specs=seq_spec,
                              out_shape=jax.ShapeDtypeStruct((lay.tx, d), F32), scratch_shapes=[ext], name=name,
                              compiler_params=_cparams(2))(u, cw, cb)
    return pl.pallas_call(bwd_body, grid=(ncb, lay.bl), in_specs=[seq_spec, seq_spec, w_spec],
                          out_specs=[seq_spec, w_spec, b_spec],
                          out_shape=[jax.ShapeDtypeStruct((lay.tx, d), F32), jax.ShapeDtypeStruct((kw, d), F32),
                                     jax.ShapeDtypeStruct((1, d), F32)],
                          scratch_shapes=[ext, ext], name=name, compiler_params=_cparams(2))(u, dy, cw)


def _ln_silu_f(u, w, b):
    mu = jnp.mean(u, axis=-1, keepdims=True)
    var = jnp.mean(jnp.square(u - mu), axis=-1, keepdims=True)
    return _silu((u - mu) * lax.rsqrt(var + EPS) * w + b)


def _ln_fwd(lay, mx, u, w, b, cat, name):
    d = mx.d_conv
    assert mx.d_ssm % d == 0

    def body(u_ref, w_ref, b_ref, cat_ref, o_ref):
        o_ref[...] = _ln_silu_f(u_ref[...], w_ref[...], b_ref[...]).astype(o_ref.dtype)

    return _tok_call(name, body, 1, lay.nsx,
                     [_tok(lay, d), _glob(1, d), _glob(1, d), pl.BlockSpec(memory_space=pl.ANY)],
                     _tok(lay, d, mx.d_ssm // d), jax.ShapeDtypeStruct(cat.shape, cat.dtype), (u, w, b, cat),
                     aliases={3: 0})


def _ln_bwd(lay, mx, u, w, b, dcat, name):
    d = mx.d_conv

    def body(u_ref, w_ref, b_ref, dc_ref, du_ref, dw_ref, db_ref):
        s = pl.program_id(1)
        _, vjp = jax.vjp(_ln_silu_f, u_ref[...], w_ref[...], b_ref[...])
        du, dw, db = vjp(dc_ref[...])
        du_ref[...] = du
        _acc(dw_ref, s == 0, dw)
        _acc(db_ref, s == 0, db)

    return _tok_call(name, body, 1, lay.nsx, [_tok(lay, d), _glob(1, d), _glob(1, d), _tok(lay, d, 1)],
                     [_tok(lay, d), _glob(1, d), _glob(1, d)],
                     [jax.ShapeDtypeStruct((lay.tx, d), F32), jax.ShapeDtypeStruct((1, d), F32),
                      jax.ShapeDtypeStruct((1, d), F32)], (u, w, b, dcat))


def _gate_tile(dff):
    return dff // 2 if (dff // 2) % 128 == 0 else dff


def _ffn_fwd(lay, nseg, x, nw, modv, k0, wts, tag, ride=None, ride_down=None, h=None, then_norm=None):
    wgu, wd, ft = wts
    if h is None:
        h = _norm_mod_fwd(lay, nseg, x, nw, modv, k0, tag + "_norm")
    t = h.shape[0]

    def act(acc):
        g, u = acc[:, :ft], acc[:, ft:]
        sg = jax.nn.sigmoid(g)
        sl = g * sg
        return jnp.concatenate([u * (sg * (1.0 + g * (1.0 - sg))), sl], axis=1), sl * u

    res = _mm(h, wgu, "nn", None, tag + "_gu", tn=2 * ft, tm=256 if t % 256 == 0 else None,
              epilogue=act, outs=[(2 * ft, BF16), (ft, BF16)], ride=ride)
    (s, a), rode = res if ride else (res, None)
    if callable(wd):
        wd = wd(rode)
    res = _mm_resid(lay, a, wd, x, modv, k0 + 2, 0.5, tag + "_down", then_norm=then_norm, ride=ride_down)
    (o, y, *h_next), rode_down = res if ride_down else (res, None)
    return y, (x, h, s, a, o), (rode, rode_down), (h_next[0] if h_next else None)


def _ffn_bwd(lay, nseg, dy, saved, nw, modv, k0, wts, tag, nout=None, hooks=None):
    wgu, wd, ft = wts
    x, h, s, a, o = saved
    do, dgate = _resid_bwd(lay, nseg, dy, o, modv, k0 + 2, 0.5, tag + "_dres")

    def through_act(da, s_tile):
        return (jnp.concatenate([da, da], axis=1) * s_tile.astype(F32),)

    res = _mm(do, wd.T, "nn", None, tag + "_da", tn=ft, tm=_big_tile(do.shape[0], 1024), extras=[(s, 2 * ft)],
              epilogue=through_act, outs=[(2 * ft, BF16)], ride=hooks.during_da() if hooks else None)
    if hooks:
        hooks.take_during_da(res[1])
        res = res[0]
    (dgu,) = res
    dwd = _mm(a, do, "tn", F32, tag + "_dwd")
    if hooks is None:
        dwgu = _mm(h, dgu, "tn", F32, tag + "_dwgu")
        dh = _mm(dgu, wgu, "nt", F32, tag + "_dh")
    else:
        dwgu, rode_a = _mm(h, dgu, "tn", F32, tag + "_dwgu", ride=hooks.after_dwd(dwd))
        dh, rode_b = _mm(dgu, wgu, "nt", F32, tag + "_dh", ride=hooks.after_dwgu(dwgu))
        hooks.take(rode_a, rode_b)
    dx, dnw, dss = _norm_mod_bwd(lay, nseg, nseg, x, nw, modv, k0, dh, dy, tag + "_dnorm", nout=nout)
    return dx, (dwgu, dwd), dnw, jnp.concatenate([dss, dgate], axis=1)


def _local_step(lay, mx, xa, target, modv, w, exch=None):
    d, bl = lay.d, lay.bl
    g = {}
    xa1, ffn1, late, ha = _ffn_fwd(lay, lay.ns, xa, w["norm_ffn1"], modv, 0, w["ffn1"], "ffn1",
                                   ride=exch.late_a if exch else None, ride_down=exch.late_b if exch else None,
                                   then_norm=(w["norm_mix"], 3))
    if exch:
        w = {**w, **exch.unpack_late(w["ffn1"], *late)}
    proj = _mm(ha, w["w_in"], "nn", F32, "mix_in")
    pre, xbc = _conv5_fwd(lay.fine(), mx, proj, w["conv_w"], w["conv_b"], "mix_conv")
    dt = _dt_fwd(lay, mx, proj, w["dt_bias"], "mix_dt")
    dtt = dt.T
    yf, hpf = _ssd_fwd(lay, mx, xbc, dt, dtt, w["a_row"], w["a_col"], False, "ssd_f")
    yb, hpb = _ssd_fwd(lay, mx, xbc, dt, dtt, w["a_row"], w["a_col"], True, "ssd_b")
    cat_y = _gate_fwd(lay, mx, yf, yb, xbc, proj, w["d_row"], w["ssm_norm_w"], "mix_gate")
    u0 = _glu_fwd(lay, mx, proj, "mix_glu")
    uc = _axial(lay, mx, u0, None, w["cconv_w"], w["cconv_b"], "mix_axial")
    cat = _ln_fwd(lay, mx, uc, w["ln_w"], w["ln_b"], cat_y, "mix_ln")
    mix, x2, h2 = _mm_resid(lay, cat, w["w_out"], xa1, modv, 5, 1.0, "mix_out", then_norm=(w["norm_ffn2"], 6))
    x3, ffn2, _, _ = _ffn_fwd(lay, lay.nsx, x2, w["norm_ffn2"], modv, 6, w["ffn2"], "ffn2", h=h2)
    loss, dx3, g["final_norm"] = _final_loss(lay, x3, w["final_norm"], target, "loss")
    dx2, g["ffn2"], g["norm_ffn2"], dmod2 = _ffn_bwd(lay, lay.nsx, dx3, ffn2, w["norm_ffn2"], modv, 6, w["ffn2"], "ffn2")
    dmix, dg2 = _resid_bwd(lay, lay.nsx, dx2, mix, modv, 5, 1.0, "mix_dres")
    dcat = _mm(dmix, w["w_out"], "nt", F32, "mix_dcat")
    g["w_out"] = _mm(cat, dmix, "tn", F32, "mix_dwout")
    duc, g["ln_w"], g["ln_b"] = _ln_bwd(lay, mx, uc, w["ln_w"], w["ln_b"], dcat, "mix_dln")
    du0, g["cconv_w"], g["cconv_b"] = _axial(lay, mx, u0, duc, w["cconv_w"], None, "mix_daxial")
    dyssd, dproj, g["d_row"], g["ssm_norm_w"] = _gate_bwd(lay, mx, yf, yb, xbc, proj, w["d_row"], w["ssm_norm_w"], dcat,
                                                          "mix_dgate")
    dproj = _glu_bwd(lay, mx, proj, du0, dproj, "mix_dglu")
    dxf, ddcf, ddrf, darf, dacf = _ssd_bwd(lay, mx, xbc, dt, dtt, w["a_row"], w["a_col"], hpf, dyssd, w["d_row"],
                                           False, "ssd_df")
    dxb, ddcb, ddrb, darb, dacb = _ssd_bwd(lay, mx, xbc, dt, dtt, w["a_row"], w["a_col"], hpb, dyssd, None,
                                           True, "ssd_db")
    g["a_row"] = (darf + darb) + (dacf + dacb).T
    dproj, g["conv_w"], g["conv_b"] = _conv5_bwd(lay.fine(), mx, proj, pre, dxf, dxb, w["conv_w"], dproj, "mix_dconv")
    dproj, g["dt_bias"] = _dt_bwd(lay, mx, proj, w["dt_bias"], (ddcf, ddcb, ddrf.T, ddrb.T), dproj, "mix_ddt")
    if exch:
        dha, got = _mm(dproj, w["w_in"], "nt", F32, "mix_dha", ride=exch.ffn2_sibling(g["ffn2"]))
        exch.take_ffn2(got)
    else:
        dha = _mm(dproj, w["w_in"], "nt", F32, "mix_dha")
    g["w_in"] = _mm(ha, dproj, "tn", F32, "mix_dwin")
    dxa1, g["norm_mix"], dss_mix = _norm_mod_bwd(lay, lay.ns, lay.nsx, xa1, w["norm_mix"], modv, 3, dha, dx2, "mix_dnorm")
    if exch:
        exch.early_grads(g)
    dx, g["ffn1"], g["norm_ffn1"], dmod1 = _ffn_bwd(lay, lay.ns, dxa1, ffn1, w["norm_ffn1"], modv, 0, w["ffn1"], "ffn1",
                                                   nout=lay.nsx, hooks=exch)
    zrow = lambda t: jnp.concatenate([t, jnp.zeros((1,) + t.shape[1:], F32)], axis=0)
    dmodv = jnp.concatenate([dmod1, dss_mix, zrow(dg2), zrow(dmod2)], axis=1)
    return loss, dx, g, dmodv


class _GatherRide:
    def __init__(self, xs):
        self.arrays = list(xs)
        self.na = len(xs)
        self.out_shapes = [jax.ShapeDtypeStruct((N_DEV,) + tuple(x.shape), x.dtype) for x in xs]
        self.sems = [pltpu.SemaphoreType.DMA((7 * self.na,)), pltpu.SemaphoreType.DMA((7 * self.na,)),
                     pltpu.SemaphoreType.DMA((self.na,))]

    def _plan(self, x_refs, out_refs, send_sems, recv_sems, local_sems):
        mx_, my_, mc_ = lax.axis_index("x"), lax.axis_index("y"), lax.axis_index("c")
        me, sibling = (mx_, my_, mc_), (mx_, my_, 1 - mc_)
        chips = [(1 - mx_, my_), (mx_, 1 - my_), (1 - mx_, 1 - my_)]

        def slot(a, px, py, pc):
            return out_refs[a].at[4 * px + 2 * py + pc]

        def copy(a, k, block, to, own=False):
            return pltpu.make_async_remote_copy(
                src_ref=x_refs[a] if own else slot(a, *block), dst_ref=slot(a, *block),
                send_sem=send_sems.at[7 * a + k], recv_sem=recv_sems.at[7 * a + k], device_id=to, device_id_type=MESH)

        mine = [pltpu.make_async_copy(x_refs[a], slot(a, *me), local_sems.at[a]) for a in range(self.na)]
        first = []
        for a in range(self.na):
            first.append(copy(a, 0, me, sibling, own=True))
            first += [copy(a, 1 + j, me, (*chip, mc_), own=True) for j, chip in enumerate(chips)]
        return me, sibling, chips, mc_, copy, mine, first

    def start(self, x_refs, out_refs, send_sems, recv_sems, local_sems):
        *_, mine, first = self._plan(x_refs, out_refs, send_sems, recv_sems, local_sems)
        for cp in mine + first:
            cp.start()

    def finish(self, x_refs, out_refs, send_sems, recv_sems, local_sems):
        me, sibling, chips, mc_, copy, mine, first = self._plan(x_refs, out_refs, send_sems, recv_sems, local_sems)
        passed = []
        for j, chip in enumerate(chips):
            for a in range(self.na):
                copy(a, 1 + j, (*chip, mc_), me).wait_recv()
                fwd = copy(a, 4 + j, (*chip, mc_), sibling)
                fwd.start()
                passed.append(fwd)
        for a in range(self.na):
            copy(a, 0, sibling, me).wait_recv()
            for j, chip in enumerate(chips):
                copy(a, 4 + j, (*chip, 1 - mc_), me).wait_recv()
        for cp in first + passed:
            cp.wait_send()
        for cp in mine:
            cp.wait()


def _exchange(ride, name, in_hbm=True):
    n_in, n_out = len(ride.arrays), len(ride.out_shapes)

    def body(*refs):
        ins, outs, sems = refs[:n_in], refs[n_in:n_in + n_out], refs[n_in + n_out:]
        ride.start(ins, outs, *sems)
        ride.finish(ins, outs, *sems)

    space = pl.BlockSpec(memory_space=pl.ANY if in_hbm else pltpu.VMEM)
    return pl.pallas_call(body, out_shape=list(ride.out_shapes), in_specs=[space] * n_in, out_specs=[space] * n_out,
                          scratch_shapes=list(ride.sems), name=name)(*ride.arrays)


def _all_gather(xs, name, in_hbm):
    return _exchange(_GatherRide(xs), name, in_hbm)


N_CHIPS = 4


class _SiblingRide:
    def __init__(self, gs):
        self.arrays = list(gs)
        self.na = len(gs)
        self.out_shapes = [jax.ShapeDtypeStruct((N_CHIPS,) + tuple(g.shape[2:]), g.dtype) for g in gs]
        self.sems = [pltpu.SemaphoreType.DMA((N_CHIPS * self.na,)), pltpu.SemaphoreType.DMA((N_CHIPS * self.na,))]

    def _copies(self, g_refs, out_refs, send_sems, recv_sems):
        mx_, my_, mc_ = lax.axis_index("x"), lax.axis_index("y"), lax.axis_index("c")
        return [pltpu.make_async_remote_copy(
            src_ref=g_refs[a].at[k, 1 - mc_], dst_ref=out_refs[a].at[k], send_sem=send_sems.at[N_CHIPS * a + k],
            recv_sem=recv_sems.at[N_CHIPS * a + k], device_id=(mx_, my_, 1 - mc_), device_id_type=MESH)
            for a in range(self.na) for k in range(N_CHIPS)]

    def start(self, g_refs, out_refs, send_sems, recv_sems):
        for cp in self._copies(g_refs, out_refs, send_sems, recv_sems):
            cp.start()

    def finish(self, g_refs, out_refs, send_sems, recv_sems):
        copies = self._copies(g_refs, out_refs, send_sems, recv_sems)
        for cp in copies:
            cp.wait_recv()
        for cp in copies:
            cp.wait_send()


def _row_tile(r, n):
    if r * n * 4 <= (1 << 20):
        return r
    for t in (1024, 512, 256, 128, 64, 32, 16, 8):
        if r % t == 0 and t * n * 4 <= (1 << 20):
            return t
    return r


def _pair_add(place, g, got, name):
    _, _, r, n = g.shape
    tr = r if r * n * 4 <= (3 << 19) else _row_tile(r, n)

    def body(place_ref, g_ref, got_ref, o_ref, ob_ref):
        s = g_ref[...] + got_ref[...]
        o_ref[...] = s
        ob_ref[...] = s.astype(ob_ref.dtype)

    blk = pl.BlockSpec((None, tr, n), lambda k, i, pr: (k, i, 0))
    grid_spec = pltpu.PrefetchScalarGridSpec(
        num_scalar_prefetch=1, grid=(N_CHIPS, r // tr),
        in_specs=[pl.BlockSpec((None, None, tr, n), lambda k, i, pr: (k, pr[0], i, 0)), blk], out_specs=[blk, blk])
    return pl.pallas_call(body, grid_spec=grid_spec,
                          out_shape=[jax.ShapeDtypeStruct((N_CHIPS, r, n), F32), jax.ShapeDtypeStruct((N_CHIPS, r, n), BF16)],
                          name=name, compiler_params=_cparams(2))(place, g, got)


class _ChipSwapRide:
    def __init__(self, ps):
        self.arrays = list(ps)
        self.na = len(ps)
        self.out_shapes = [jax.ShapeDtypeStruct((N_CHIP_PEERS,) + tuple(p.shape[1:]), p.dtype) for p in ps]
        self.sems = [pltpu.SemaphoreType.DMA((N_CHIP_PEERS * self.na,)), pltpu.SemaphoreType.DMA((N_CHIP_PEERS * self.na,))]

    def _copies(self, p_refs, out_refs, send_sems, recv_sems):
        mx_, my_, mc_ = lax.axis_index("x"), lax.axis_index("y"), lax.axis_index("c")
        chips = [(1 - mx_, my_), (mx_, 1 - my_), (1 - mx_, 1 - my_)]
        return [pltpu.make_async_remote_copy(
            src_ref=p_refs[a].at[2 * cx + cy], dst_ref=out_refs[a].at[j], send_sem=send_sems.at[N_CHIP_PEERS * a + j],
            recv_sem=recv_sems.at[N_CHIP_PEERS * a + j], device_id=(cx, cy, mc_), device_id_type=MESH)
            for a in range(self.na) for j, (cx, cy) in enumerate(chips)]

    def start(self, p_refs, out_refs, send_sems, recv_sems):
        for cp in self._copies(p_refs, out_refs, send_sems, recv_sems):
            cp.start()

    def finish(self, p_refs, out_refs, send_sems, recv_sems):
        copies = self._copies(p_refs, out_refs, send_sems, recv_sems)
        for cp in copies:
            cp.wait_recv()
        for cp in copies:
            cp.wait_send()


def _sum_lead(x, name):
    k, r, n = x.shape
    tr = _row_tile(r, n * k)

    def body(x_ref, o_ref):
        acc = x_ref[0]
        for i in range(1, k):
            acc = acc + x_ref[i]
        o_ref[...] = acc

    return pl.pallas_call(body, grid=(r // tr,), in_specs=[pl.BlockSpec((k, tr, n), lambda i: (0, i, 0))],
                          out_specs=pl.BlockSpec((tr, n), lambda i: (i, 0)),
                          out_shape=jax.ShapeDtypeStruct((r, n), x.dtype), name=name, compiler_params=_cparams(1))(x)


def _adamw(place, w, parts, m, v, name):
    shape = w.shape
    cols = shape[-1]
    rows = math.prod(shape[:-1])
    to2 = lambda t: t.reshape(rows, cols)
    tr = _row_tile(rows, cols) if rows * cols * 4 > (1 << 20) else rows
    npart = len(parts)
    spec = pl.BlockSpec((tr, cols), lambda i, pr: (i, 0))
    native = len(shape) == 3 and shape[0] == 1
    own = pl.BlockSpec((None, tr, cols), lambda i, pr: (0, i, 0)) if native else spec
    as_own = (lambda t: t) if native else to2
    part_specs, part_args = [], []
    for piece in parts:
        if isinstance(piece, tuple):
            stack, k, row0 = piece
            part_args.append(stack.reshape(stack.shape[0], -1, cols))
            assert row0 % tr == 0
            if k == "chip":
                part_specs.append(pl.BlockSpec((None, tr, cols), functools.partial(lambda i, pr, b0: (pr[1], i + b0, 0),
                                                                                   b0=row0 // tr)))
            else:
                part_specs.append(pl.BlockSpec((None, tr, cols), functools.partial(
                    lambda i, pr, kk, b0: (kk, i + b0, 0), kk=k, b0=row0 // tr)))
        else:
            part_args.append(to2(piece))
            part_specs.append(spec)

    def body(place_ref, *refs):
        w_ref, m_ref, v_ref = refs[0], refs[1 + npart], refs[2 + npart]
        g_ref, d_ref, nm_ref, nv_ref = refs[3 + npart:]
        g = refs[1][...].astype(F32)
        for q in range(1, npart):
            g = g + refs[1 + q][...].astype(F32)
        mm = ADAM_B1 * m_ref[...] + (1.0 - ADAM_B1) * g
        vv = ADAM_B2 * v_ref[...] + (1.0 - ADAM_B2) * jnp.square(g)
        m_hat = mm / (1.0 - ADAM_B1 ** ADAM_STEP)
        v_hat = vv / (1.0 - ADAM_B2 ** ADAM_STEP)
        g_ref[...] = g
        d_ref[...] = -ADAM_LR * (m_hat / (jnp.sqrt(v_hat) + ADAM_EPS) + ADAM_WD * w_ref[...])
        nm_ref[...] = mm
        nv_ref[...] = vv

    sh = jax.ShapeDtypeStruct(shape if native else (rows, cols), F32)
    grid_spec = pltpu.PrefetchScalarGridSpec(num_scalar_prefetch=1, grid=(rows // tr,),
                                             in_specs=[own] + part_specs + [own, own], out_specs=[own] * 4)
    outs = pl.pallas_call(body, grid_spec=grid_spec, out_shape=[sh] * 4, name=name, compiler_params=_cparams(1),
                          )(place, as_own(w), *part_args, as_own(m), as_own(v))
    return tuple(o.reshape(shape) for o in outs)


def _packed_rows(n, width):
    return -(-n // (8 * width)) * 8


def _pack_rows(items, width):
    rows = []
    for t in items:
        flat = t.reshape(-1)
        n = flat.shape[0]
        k = _packed_rows(n, width)
        if k * width > n:
            flat = jnp.concatenate([flat, jnp.zeros((k * width - n,), t.dtype)])
        rows.append(flat.reshape(k, width))
    return jnp.concatenate(rows, axis=0)


def _unpack_rows(packed, shapes, lead=()):
    width = packed.shape[-1]
    out, r = [], 0
    for sh in shapes:
        n = math.prod(sh)
        k = _packed_rows(n, width)
        piece = packed[..., r:r + k, :].reshape(tuple(lead) + (k * width,))[..., :n]
        out.append(piece.reshape(tuple(lead) + tuple(sh)))
        r += k
    return out


def _cols_full(t):
    return jnp.transpose(t, (1, 0, 2)).reshape(t.shape[1], -1)


def _cols_shards(t):
    d = t.shape[0]
    return jnp.transpose(t.reshape(d, N_DEV, -1), (1, 0, 2))


BIG = ("ffn1_gate", "ffn1_up", "ffn1_down", "w_in", "w_out", "ffn2_gate", "ffn2_up", "ffn2_down")


def kernel(x, c, ctx, c_ctx, w_mod, b_mod, norm_ffn1, ffn1_gate, ffn1_up, ffn1_down, norm_mix, w_in, ssm_conv_w, ssm_conv_b, dt_bias_fwd, dt_bias_bwd, a_log_fwd, a_log_bwd, ssm_d, ssm_norm_w, cconv_w, cconv_b, cconv_ln_w, cconv_ln_b, w_out, norm_ffn2, ffn2_gate, ffn2_up, ffn2_down, final_norm, loss_target, m_c_ctx, m_w_mod, m_b_mod, m_norm_ffn1, m_ffn1_gate, m_ffn1_up, m_ffn1_down, m_norm_mix, m_w_in, m_ssm_conv_w, m_ssm_conv_b, m_dt_bias_fwd, m_dt_bias_bwd, m_a_log_fwd, m_a_log_bwd, m_ssm_d, m_ssm_norm_w, m_cconv_w, m_cconv_b, m_cconv_ln_w, m_cconv_ln_b, m_w_out, m_norm_ffn2, m_ffn2_gate, m_ffn2_up, m_ffn2_down, m_final_norm, v_c_ctx, v_w_mod, v_b_mod, v_norm_ffn1, v_ffn1_gate, v_ffn1_up, v_ffn1_down, v_norm_mix, v_w_in, v_ssm_conv_w, v_ssm_conv_b, v_dt_bias_fwd, v_dt_bias_bwd, v_a_log_fwd, v_a_log_bwd, v_ssm_d, v_ssm_norm_w, v_cconv_w, v_cconv_b, v_cconv_ln_w, v_cconv_ln_b, v_w_out, v_norm_ffn2, v_ffn2_gate, v_ffn2_up, v_ffn2_down, v_final_norm):
    args = dict(locals())
    names = ("c_ctx", "w_mod", "b_mod", "norm_ffn1", "ffn1_gate", "ffn1_up", "ffn1_down", "norm_mix", "w_in",
             "ssm_conv_w", "ssm_conv_b", "dt_bias_fwd", "dt_bias_bwd", "a_log_fwd", "a_log_bwd", "ssm_d", "ssm_norm_w",
             "cconv_w", "cconv_b", "cconv_ln_w", "cconv_ln_b", "w_out", "norm_ffn2", "ffn2_gate", "ffn2_up",
             "ffn2_down", "final_norm")
    wts = {n: args[n] for n in names}
    bl, seq, d = x.shape
    clen = ctx.shape[1]
    heads = dt_bias_fwd.shape[1]
    ft = _gate_tile(ffn1_gate.shape[2] * N_DEV)
    lay = _Lay(bl, seq, clen, d)
    mx = _Mix(d, heads)
    nb = bl * N_DEV
    me = 4 * lax.axis_index("x") + 2 * lax.axis_index("y") + lax.axis_index("c")
    mcols = w_mod.shape[2]
    n_ctx_mod = 5 * d

    place = jnp.stack([lax.axis_index("c"), 2 * lax.axis_index("x") + lax.axis_index("y")]).astype(jnp.int32)

    small_shapes = [(bl, d), ssm_conv_w.shape[1:], cconv_w.shape[1:]]
    g1, gate1, up1 = _all_gather([_pack_rows([c, ssm_conv_w, cconv_w], d), ffn1_gate[0].astype(BF16),
                                  ffn1_up[0].astype(BF16)], "gather_first", True)
    c_g, conv_g, cconv_g = _unpack_rows(g1, small_shapes, (N_DEV,))
    c_all = c_g.reshape(nb, d)
    conv_w_full = jnp.transpose(conv_g, (1, 0, 2)).reshape(conv_g.shape[1], -1)
    cconv_w_full = jnp.transpose(cconv_g, (1, 0, 2)).reshape(cconv_g.shape[1], -1)

    s_all = jnp.concatenate([_silu(c_all), _silu(c_ctx)[None, :], jnp.zeros((7, d), F32)], axis=0)
    mod_cols = _mm(s_all, w_mod[0], "nn", F32, "mod_cols")
    (g2,) = _all_gather([mod_cols], "gather_mod", False)
    mod_all = _cols_full(g2) + b_mod
    mod_mine = jnp.concatenate([lax.dynamic_slice_in_dim(mod_all, me * bl, bl, axis=0), mod_all[nb:nb + 1]], axis=0)
    modv = mod_mine.reshape(bl + 1, N_MOD, d)

    hh = 2 * heads
    shard16 = lambda n: wts[n][0].astype(BF16)

    nl = ffn1_gate.shape[2]
    spt = ft // nl
    assert ft % nl == 0 and N_DEV % spt == 0

    def ffn_weights(gate, up, down):
        both = jnp.stack([gate, up], axis=1).reshape(N_DEV // spt, spt, 2, d, nl)
        return jnp.transpose(both, (3, 0, 2, 1, 4)).reshape(d, -1), None if down is None else down.reshape(-1, d), ft

    def grads_by_dest(name, grad):
        if name == "w_in":
            grad = _cols_shards(jnp.concatenate([grad[:, mx.off_z:], grad[:, :mx.off_dt + hh],
                                                 grad[:, mx.off_glu:mx.off_z]], axis=1))
        elif name.endswith("_gu"):
            grad = jnp.transpose(grad.reshape(d, N_DEV // spt, 2, spt, nl), (1, 3, 2, 0, 4)).reshape(N_DEV, 2 * d, nl)
        else:
            grad = grad.reshape((N_DEV,) + tuple(wts[name].shape[1:]))
        return grad.reshape((N_CHIPS, 2) + tuple(grad.shape[1:]))

    def ffn_grads(tag, pair):
        return {tag + "_gu": pair[0], tag + "_down": pair[1]}

    def by_dest_of(named):
        return {n: grads_by_dest(n, grad) for n, grad in named.items()}

    def to_chip_sums(named, got=None):
        by_dest = by_dest_of(named) if got is None else named
        names_ = list(by_dest)
        if got is None:
            got = _exchange(_SiblingRide([by_dest[n] for n in names_]), "rs_sibling_" + names_[0])
        return {n: _pair_add(place, by_dest[n], s, "rs_pair_add_" + n) for n, s in zip(names_, got)}

    class _Overlap:
        names_a = ("ffn1_down", "w_in", "ffn2_gate")
        names_b = ("w_out", "ffn2_up", "ffn2_down")
        late_a = _GatherRide([shard16(n) for n in names_a])
        late_b = _GatherRide([shard16(n) for n in names_b])
        sums, arrived = {}, {}

        def unpack_late(self, ffn1, outs_a, outs_b):
            full = {**dict(zip(self.names_a, outs_a)), **dict(zip(self.names_b, outs_b))}
            w_in_f = _cols_full(full["w_in"])
            w_in_p = jnp.concatenate([w_in_f[:, mx.ref_x:mx.ref_glu], jnp.zeros((d, DT_PAD - hh), BF16),
                                      w_in_f[:, mx.ref_glu:], w_in_f[:, :d]], axis=1)
            return {"w_in": w_in_p, "w_out": full["w_out"].reshape(-1, d),
                    "ffn1": (ffn1[0], full["ffn1_down"].reshape(-1, d), ffn1[2]),
                    "ffn2": ffn_weights(full["ffn2_gate"], full["ffn2_up"], full["ffn2_down"])}

        groups = (("ffn2_gu", "ffn2_down", "ffn1_down"), ("w_in", "w_out", "ffn1_gu"))

        def ffn2_sibling(self, g_ffn2):
            self.ffn2_dest = by_dest_of(ffn_grads("ffn2", g_ffn2))
            return _SiblingRide(list(self.ffn2_dest.values()))

        def take_ffn2(self, got):
            self.ffn2_got = got

        def early_grads(self, g):
            self.sums = to_chip_sums(self.ffn2_dest, self.ffn2_got)
            self.mix_dest = by_dest_of({"w_in": g["w_in"], "w_out": g["w_out"]})

        def during_da(self):
            return _SiblingRide(list(self.mix_dest.values()))

        def take_during_da(self, got):
            self.mix_got = got

        def _ride(self, grp):
            return _ChipSwapRide([self.sums[n][1] for n in grp])

        def after_dwd(self, dwd):
            self.sums.update(to_chip_sums({"ffn1_down": dwd}))
            return self._ride(self.groups[0])

        def after_dwgu(self, dwgu):
            self.sums.update(to_chip_sums(self.mix_dest, self.mix_got))
            self.sums.update(to_chip_sums({"ffn1_gu": dwgu}))
            return self._ride(self.groups[1])

        def take(self, *rode):
            for grp, outs in zip(self.groups, rode):
                self.arrived.update(zip(grp, outs))

    exch = _Overlap()
    wgu1 = ffn_weights(gate1, up1, None)[0]
    lanes_pad = lambda a, b: jnp.concatenate([a, b, jnp.zeros((1, DT_LANES - hh), F32)], axis=1)
    a_vals = lanes_pad(-jnp.exp(a_log_fwd), -jnp.exp(a_log_bwd))
    w = {
        "norm_ffn1": norm_ffn1, "norm_mix": norm_mix, "norm_ffn2": norm_ffn2, "final_norm": final_norm[None, :],
        "ffn1": (wgu1, lambda outs_a: outs_a[0].reshape(-1, d), ft),
        "conv_w": jnp.concatenate([conv_w_full, jnp.zeros((3, mx.xw), F32)], axis=0), "conv_b": ssm_conv_b,
        "dt_bias": lanes_pad(dt_bias_fwd, dt_bias_bwd), "a_row": a_vals, "a_col": a_vals.T,
        "d_row": jnp.repeat(ssm_d, HEAD_DIM, axis=1), "ssm_norm_w": ssm_norm_w,
        "cconv_w": cconv_w_full, "cconv_b": cconv_b, "ln_w": cconv_ln_w, "ln_b": cconv_ln_b,
    }

    xa = jnp.concatenate([x.reshape(bl * seq, d), ctx.reshape(bl * clen, d)], axis=0)
    loss, grad_x, g, dmodv = _local_step(lay, mx, xa, loss_target.reshape(bl * seq, d), modv, w, exch)
    loss = lax.psum(loss[0, 0], ("x", "y", "c"))

    sums, arrived = exch.sums, exch.arrived

    def big_parts(n):
        key, row0 = (n[:4] + "_gu", d if n.endswith("_up") else 0) if n.endswith(("_gate", "_up")) else (n, 0)
        return [(sums[key][0], "chip", row0)] + [(arrived[key], k, row0) for k in range(N_CHIP_PEERS)]

    n9 = N_MOD * d
    dmod_rows = dmodv.reshape(bl + 1, n9)
    ctx_row = jnp.concatenate([dmod_rows[bl, :n_ctx_mod], jnp.zeros((n9 - n_ctx_mod,), F32)])
    summed = [ctx_row, g["norm_ffn1"], g["norm_mix"], g["norm_ffn2"], g["final_norm"], g["conv_b"], g["dt_bias"],
              g["a_row"], g["d_row"], g["ssm_norm_w"], g["cconv_b"], g["ln_w"], g["ln_b"], g["conv_w"][:5], g["cconv_w"]]
    sum_shapes = [t.shape for t in summed]
    (g4,) = _all_gather([_pack_rows([dmod_rows[:bl]] + summed, d)], "gather_small_grads", False)
    dmod_batch = g4[:, :bl * N_MOD].reshape(nb, n9)
    tot = _sum_lead(g4[:, _packed_rows(bl * n9, d):], "sum_small_grads")
    (dctx, g_n1, g_nm, g_n2, g_fn, g_cb, g_dtb, g_a, g_drow, g_snw, g_ccb, g_lnw, g_lnb, g_cw, g_ccw) = _unpack_rows(tot, sum_shapes)
    dmod_all = jnp.concatenate([dmod_batch, dctx[None, :], jnp.zeros((7, n9), F32)], axis=0)

    dmod_my_cols = lax.dynamic_slice_in_dim(dmod_all, me * mcols, mcols, axis=1)
    g_w_mod = _mm(s_all, dmod_my_cols, "tn", F32, "dw_mod")[None]
    g_b_mod = _sum_lead(dmod_all.reshape(nb + 8, N_MOD, d), "db_mod").reshape(1, n9)
    ds_part = _mm(dmod_my_cols[nb:nb + 8], w_mod[0], "nt", F32, "ds_ctx")
    (g5,) = _all_gather([jnp.concatenate([ds_part[0:1], jnp.zeros((7, d), F32)], axis=0)], "gather_ds_ctx", False)
    ds_ctx = _sum_lead(g5, "sum_ds_ctx")[0]
    sg = jax.nn.sigmoid(c_ctx)
    g_c_ctx = ds_ctx * (sg * (1.0 + c_ctx * (1.0 - sg)))

    a_f, a_b = a_vals[:, :heads], a_vals[:, heads:hh]
    grads = {
        "c_ctx": [g_c_ctx], "w_mod": [g_w_mod], "b_mod": [g_b_mod],
        "norm_ffn1": [g_n1], "norm_mix": [g_nm], "norm_ffn2": [g_n2], "final_norm": [g_fn.reshape(-1)],
        "ssm_conv_w": [lax.dynamic_slice_in_dim(g_cw, me * ssm_conv_w.shape[2], ssm_conv_w.shape[2], axis=1)[None]],
        "ssm_conv_b": [g_cb],
        "dt_bias_fwd": [g_dtb[:, :heads]], "dt_bias_bwd": [g_dtb[:, heads:hh]],
        "a_log_fwd": [g_a[:, :heads] * a_f], "a_log_bwd": [g_a[:, heads:hh] * a_b],
        "ssm_d": [jnp.sum(g_drow.reshape(1, heads, HEAD_DIM), axis=2)], "ssm_norm_w": [g_snw],
        "cconv_w": [lax.dynamic_slice_in_dim(g_ccw, me * cconv_w.shape[2], cconv_w.shape[2], axis=1)[None]],
        "cconv_b": [g_ccb], "cconv_ln_w": [g_lnw], "cconv_ln_b": [g_lnb],
    }
    for n in BIG:
        grads[n] = big_parts(n)

    out_g, out_d, out_m, out_v = [], [], [], []
    for n in names:
        gr, de, nm, nv = _adamw(place, wts[n], grads[n], args["m_" + n], args["v_" + n], "adamw_" + n)
        out_g.append(gr)
        out_d.append(de)
        out_m.append(nm)
        out_v.append(nv)
    return (loss, grad_x.reshape(bl, seq, d), *out_g, *out_d, *out_m, *out_v)
```
